```python
import math
import jax, jax.numpy as jnp
from jax import lax
import numpy as np

D_MODEL = 2048
BATCH = 8
SEQ = 2048
DEPTH = 1

MEM_LEN = 256
EPS = 1e-6

ATTN_HEADS = 16
ATTN_KV_HEADS = 2
ATTN_HEAD_DIM = 64
ATTN_WIDTH = ATTN_HEADS * ATTN_HEAD_DIM
KV_WIDTH = ATTN_KV_HEADS * ATTN_HEAD_DIM
WINDOW = 128
BLOCK = 128

REL_BUCKETS = 32
REL_MAX_DIST = 128

SSM_GROUP_CH = 16
SSM_GROUPS = 32
SSM_STATE = 64
SSM_WIDTH = SSM_GROUPS * SSM_GROUP_CH

XATTN_HEADS = 4
XATTN_HEAD_DIM = 128
XATTN_WIDTH = XATTN_HEADS * XATTN_HEAD_DIM

MIX_WIDTH = ATTN_WIDTH + SSM_WIDTH + XATTN_WIDTH
IN_COLS = ATTN_WIDTH + 2 * KV_WIDTH + SSM_WIDTH + XATTN_WIDTH

N_EXPERT_GROUPS = 8
EXPERTS_PER_GROUP = 8
N_EXPERTS = N_EXPERT_GROUPS * EXPERTS_PER_GROUP
TOP_K_INNER = 2
D_EXPERT = 512
MOE_BLOCK = 128

kernel_name = "hymba_swa_s5_xattn_hmoe"


def rms_norm(x, g):
    xf = x.astype(jnp.float32)
    y = xf * lax.rsqrt(jnp.mean(xf * xf, axis=-1, keepdims=True) + EPS)
    return (y * g.astype(jnp.float32)).astype(x.dtype)


def t5_bucket(delta):
    n = jnp.maximum(delta, 0)
    max_exact = REL_BUCKETS // 2
    nf = jnp.maximum(n, 1).astype(jnp.float32)
    large = max_exact + (jnp.log(nf / max_exact) / math.log(REL_MAX_DIST / max_exact)
                         * (REL_BUCKETS - max_exact)).astype(jnp.int32)
    large = jnp.minimum(large, REL_BUCKETS - 1)
    return jnp.where(n < max_exact, n, large)


def sliding_window_attention(q, k, v, sinks, rel_bias):
    B, S = q.shape[0], q.shape[1]
    nb = S // BLOCK
    G = ATTN_HEADS // ATTN_KV_HEADS
    f32 = jnp.float32
    qb = q.astype(f32).reshape(B, nb, BLOCK, ATTN_KV_HEADS, G, ATTN_HEAD_DIM)

    def banded(t):
        tp = jnp.pad(t, ((0, 0), (BLOCK, 0), (0, 0), (0, 0)))
        tb = tp.reshape(B, nb + 1, BLOCK, ATTN_KV_HEADS, ATTN_HEAD_DIM)
        return jnp.concatenate([tb[:, :-1], tb[:, 1:]], axis=2)

    kc = banded(k).astype(f32)
    vc = banded(v)
    logits = jnp.einsum('bnqhgd,bnkhd->bnhgqk', qb, kc) / math.sqrt(ATTN_HEAD_DIM)

    qi = jnp.arange(BLOCK, dtype=jnp.int32)[:, None]
    ki = jnp.arange(2 * BLOCK, dtype=jnp.int32)[None, :]
    delta = BLOCK + qi - ki
    in_window = (delta >= 0) & (delta < WINDOW)
    blk = jnp.arange(nb, dtype=jnp.int32)[:, None, None]
    valid = in_window[None] & ((ki >= BLOCK)[None] | (blk > 0))

    bias = rel_bias.astype(f32)[t5_bucket(delta)]
    bias = jnp.transpose(bias, (2, 0, 1)).reshape(ATTN_KV_HEADS, G, BLOCK, 2 * BLOCK)
    logits = logits + bias[None, None]
    logits = jnp.where(valid[None, :, None, None], logits, jnp.float32(-1e30))

    sink = jnp.broadcast_to(sinks.astype(f32).reshape(ATTN_KV_HEADS, G, 1, 1),
                            logits.shape[:-1] + (1,))
    probs = jax.nn.softmax(jnp.concatenate([logits, sink], axis=-1), axis=-1)[..., :-1]
    out = jnp.einsum('bnhgqk,bnkhd->bnqhgd', probs.astype(v.dtype), vc)
    return out.reshape(B, S, ATTN_WIDTH)


def s5_layer(u, lam_re, lam_im, log_dt, b_re, b_im, c_re, c_im, d_skip, w_glu):
    B, S, _ = u.shape
    f32 = jnp.float32
    uf = u.astype(f32).reshape(B, S, SSM_GROUPS, SSM_GROUP_CH)
    lr, li = lam_re.astype(f32), lam_im.astype(f32)
    dt = jnp.exp(log_dt.astype(f32))[:, None]
    mag = jnp.exp(lr * dt)
    a_re, a_im = mag * jnp.cos(li * dt), mag * jnp.sin(li * dt)
    den = lr * lr + li * li
    nr, ni = a_re - 1.0, a_im
    coef_re = (nr * lr + ni * li) / den
    coef_im = (ni * lr - nr * li) / den
    br, bi = b_re.astype(f32), b_im.astype(f32)
    bbar_re = coef_re[..., None] * br - coef_im[..., None] * bi
    bbar_im = coef_re[..., None] * bi + coef_im[..., None] * br
    bu_re = jnp.einsum('bsgc,gnc->bsgn', uf, bbar_re)
    bu_im = jnp.einsum('bsgc,gnc->bsgn', uf, bbar_im)
    ar = jnp.broadcast_to(a_re, bu_re.shape)
    ai = jnp.broadcast_to(a_im, bu_re.shape)

    def combine(left, right):
        a1r, a1i, b1r, b1i = left
        a2r, a2i, b2r, b2i = right
        return (a2r * a1r - a2i * a1i,
                a2r * a1i + a2i * a1r,
                a2r * b1r - a2i * b1i + b2r,
                a2r * b1i + a2i * b1r + b2i)

    _, _, xr, xi = lax.associative_scan(combine, (ar, ai, bu_re, bu_im), axis=1)
    y = (jnp.einsum('gcn,bsgn->bsgc', c_re.astype(f32), xr)
         - jnp.einsum('gcn,bsgn->bsgc', c_im.astype(f32), xi)
         + d_skip.astype(f32).reshape(SSM_GROUPS, SSM_GROUP_CH) * uf)
    y = jax.nn.gelu(y.reshape(B, S, SSM_WIDTH))
    y = y * jax.nn.sigmoid(y @ w_glu.astype(f32))
    return y.astype(u.dtype)


def memory_cross_attention(qx, mem, mem_norm, w_mem_kv, xq_norm, xk_norm):
    B, S, _ = qx.shape
    M = mem.shape[1]
    kv = rms_norm(mem, mem_norm) @ w_mem_kv
    km, vm = jnp.split(kv, 2, axis=-1)
    q = rms_norm(qx.reshape(B, S, XATTN_HEADS, XATTN_HEAD_DIM), xq_norm).astype(jnp.float32)
    km = rms_norm(km.reshape(B, M, XATTN_HEADS, XATTN_HEAD_DIM), xk_norm).astype(jnp.float32)
    vm = vm.reshape(B, M, XATTN_HEADS, XATTN_HEAD_DIM)
    logits = jnp.einsum('bshd,bmhd->bhsm', q, km) / math.sqrt(XATTN_HEAD_DIM)
    probs = jax.nn.softmax(logits, axis=-1)
    out = jnp.einsum('bhsm,bmhd->bshd', probs.astype(vm.dtype), vm)
    return out.reshape(B, S, XATTN_WIDTH)


def hierarchical_moe(h, w_router_group, w_router_expert, w_gate, w_up, w_down):
    B, S, D = h.shape
    T = B * S
    f32 = jnp.float32
    xt = h.reshape(T, D)
    g_logits = (xt @ w_router_group).astype(f32)
    g_prob = jax.nn.softmax(g_logits, axis=-1)
    g_idx = jnp.argmax(g_logits, axis=-1)
    g_gate = jnp.take_along_axis(g_prob, g_idx[:, None], axis=-1)
    e_logits = (xt @ w_router_expert).astype(f32).reshape(T, N_EXPERT_GROUPS, EXPERTS_PER_GROUP)
    e_logits = jnp.take_along_axis(e_logits, g_idx[:, None, None], axis=1)[:, 0]
    top_val, top_idx = lax.top_k(e_logits, TOP_K_INNER)
    weights = g_gate * jax.nn.softmax(top_val, axis=-1)
    expert_ids = g_idx[:, None] * EXPERTS_PER_GROUP + top_idx

    A = T * TOP_K_INNER
    flat_e = expert_ids.reshape(A).astype(jnp.int32)
    flat_tok = jnp.repeat(jnp.arange(T, dtype=jnp.int32), TOP_K_INNER)
    flat_w = weights.reshape(A)
    order = jnp.argsort(flat_e)
    sorted_e = flat_e[order]
    sorted_tok = flat_tok[order]
    counts = jnp.bincount(flat_e, length=N_EXPERTS)
    starts = jnp.cumsum(counts) - counts
    padded = ((counts + MOE_BLOCK - 1) // MOE_BLOCK) * MOE_BLOCK
    pends = jnp.cumsum(padded)
    pstarts = pends - padded
    dest = pstarts[sorted_e] + (jnp.arange(A, dtype=jnp.int32) - starts[sorted_e])
    P = A + N_EXPERTS * MOE_BLOCK
    n_blocks = P // MOE_BLOCK
    buf = jnp.zeros((P, D), xt.dtype).at[dest].set(xt[sorted_tok])
    blk_start = jnp.arange(n_blocks, dtype=jnp.int32) * MOE_BLOCK
    blk_expert = jnp.minimum(jnp.searchsorted(pends, blk_start, side='right'), N_EXPERTS - 1)

    def expert_block(args):
        xb, e = args
        return (jax.nn.silu(xb @ w_gate[e]) * (xb @ w_up[e])) @ w_down[e]

    yb = lax.map(expert_block, (buf.reshape(n_blocks, MOE_BLOCK, D), blk_expert))
    ys = yb.reshape(P, D)[dest]
    contrib = ys.astype(f32) * flat_w[order][:, None]
    out = jax.ops.segment_sum(contrib, sorted_tok, num_segments=T)
    return out.reshape(B, S, D).astype(h.dtype)


def setup_inputs(seed: int = 0) -> dict:
    key = jax.random.key(seed)
    ks = jax.random.split(key, 32)
    f32 = jnp.float32
    L = DEPTH

    def nrm(k, shape, scale):
        return jax.random.normal(k, shape, f32) * scale

    def gain(k, shape):
        return 1.0 + 0.05 * jax.random.normal(k, shape, f32)

    n_idx = jnp.arange(SSM_STATE, dtype=f32)
    gn = (L, SSM_GROUPS, SSM_STATE)
    return {
        "x": nrm(ks[0], (BATCH, SEQ, D_MODEL), 1.0),
        "mem": nrm(ks[1], (BATCH, MEM_LEN, D_MODEL), 1.0),
        "norm_mix": gain(ks[2], (L, D_MODEL)),
        "w_in": nrm(ks[3], (L, D_MODEL, IN_COLS), D_MODEL ** -0.5),
        "q_norm": gain(ks[4], (L, ATTN_HEAD_DIM)),
        "k_norm": gain(ks[5], (L, ATTN_HEAD_DIM)),
        "attn_sinks": nrm(ks[6], (L, ATTN_HEADS), 0.5),
        "rel_bias": nrm(ks[7], (REL_BUCKETS, ATTN_HEADS), 0.2),
        "ssm_lambda_re": -0.5 + nrm(ks[8], gn, 0.01),
        "ssm_lambda_im": math.pi * n_idx + nrm(ks[9], gn, 0.01),
        "ssm_log_dt": jax.random.uniform(ks[10], (L, SSM_GROUPS), f32,
                                         minval=math.log(1e-3), maxval=math.log(1e-1)),
        "ssm_b_re": nrm(ks[11], (L, SSM_GROUPS, SSM_STATE, SSM_GROUP_CH), (2 * SSM_GROUP_CH) ** -0.5),
        "ssm_b_im": nrm(ks[12], (L, SSM_GROUPS, SSM_STATE, SSM_GROUP_CH), (2 * SSM_GROUP_CH) ** -0.5),
        "ssm_c_re": nrm(ks[13], (L, SSM_GROUPS, SSM_GROUP_CH, SSM_STATE), (2 * SSM_STATE) ** -0.5),
        "ssm_c_im": nrm(ks[14], (L, SSM_GROUPS, SSM_GROUP_CH, SSM_STATE), (2 * SSM_STATE) ** -0.5),
        "ssm_d": nrm(ks[15], (L, SSM_WIDTH), 1.0),
        "ssm_w_glu": nrm(ks[16], (L, SSM_WIDTH, SSM_WIDTH), SSM_WIDTH ** -0.5),
        "mem_norm": gain(ks[17], (L, D_MODEL)),
        "w_mem_kv": nrm(ks[18], (L, D_MODEL, 2 * XATTN_WIDTH), D_MODEL ** -0.5),
        "xq_norm": gain(ks[19], (L, XATTN_HEAD_DIM)),
        "xk_norm": gain(ks[20], (L, XATTN_HEAD_DIM)),
        "out_norm_attn": gain(ks[21], (L, ATTN_WIDTH)),
        "out_norm_ssm": gain(ks[22], (L, SSM_WIDTH)),
        "out_norm_xattn": gain(ks[23], (L, XATTN_WIDTH)),
        "w_o": nrm(ks[24], (L, MIX_WIDTH, D_MODEL), MIX_WIDTH ** -0.5),
        "norm_ffn": gain(ks[25], (L, D_MODEL)),
        "w_router_group": nrm(ks[26], (L, D_MODEL, N_EXPERT_GROUPS), D_MODEL ** -0.5),
        "w_router_expert": nrm(ks[27], (L, D_MODEL, N_EXPERTS), D_MODEL ** -0.5),
        "w_gate": nrm(ks[28], (L, N_EXPERTS, D_MODEL, D_EXPERT), D_MODEL ** -0.5),
        "w_up": nrm(ks[29], (L, N_EXPERTS, D_MODEL, D_EXPERT), D_MODEL ** -0.5),
        "w_down": nrm(ks[30], (L, N_EXPERTS, D_EXPERT, D_MODEL), D_EXPERT ** -0.5),
    }


def reference(x, mem, norm_mix, w_in, q_norm, k_norm, attn_sinks, rel_bias,
              ssm_lambda_re, ssm_lambda_im, ssm_log_dt, ssm_b_re, ssm_b_im,
              ssm_c_re, ssm_c_im, ssm_d, ssm_w_glu, mem_norm, w_mem_kv,
              xq_norm, xk_norm, out_norm_attn, out_norm_ssm, out_norm_xattn, w_o,
              norm_ffn, w_router_group, w_router_expert, w_gate, w_up, w_down):
    B, S, _ = x.shape
    splits = [ATTN_WIDTH, ATTN_WIDTH + KV_WIDTH, ATTN_WIDTH + 2 * KV_WIDTH,
              ATTN_WIDTH + 2 * KV_WIDTH + SSM_WIDTH]
    for l in range(DEPTH):
        h = rms_norm(x, norm_mix[l])
        proj = h @ w_in[l]
        q, k, v, u, qx = jnp.split(proj, splits, axis=-1)

        q = rms_norm(q.reshape(B, S, ATTN_HEADS, ATTN_HEAD_DIM), q_norm[l])
        k = rms_norm(k.reshape(B, S, ATTN_KV_HEADS, ATTN_HEAD_DIM), k_norm[l])
        v = v.reshape(B, S, ATTN_KV_HEADS, ATTN_HEAD_DIM)
        y_attn = sliding_window_attention(q, k, v, attn_sinks[l], rel_bias)

        y_ssm = s5_layer(u, ssm_lambda_re[l], ssm_lambda_im[l], ssm_log_dt[l],
                         ssm_b_re[l], ssm_b_im[l], ssm_c_re[l], ssm_c_im[l],
                         ssm_d[l], ssm_w_glu[l])

        y_x = memory_cross_attention(qx, mem, mem_norm[l], w_mem_kv[l], xq_norm[l], xk_norm[l])

        mix = jnp.concatenate([rms_norm(y_attn, out_norm_attn[l]),
                               rms_norm(y_ssm, out_norm_ssm[l]),
                               rms_norm(y_x, out_norm_xattn[l])], axis=-1)
        x = x + mix @ w_o[l]

        h2 = rms_norm(x, norm_ffn[l])
        x = x + hierarchical_moe(h2, w_router_group[l], w_router_expert[l],
                                 w_gate[l], w_up[l], w_down[l])
    return x
```

```python
import functools
import math

import numpy as np
import jax
import jax.numpy as jnp
from jax import lax
from jax.experimental import pallas as pl
from jax.experimental.pallas import tpu as pltpu

F32 = jnp.float32
BF16 = jnp.bfloat16
EPS = 1e-6

ATTN_HEADS = 16
ATTN_KV_HEADS = 2
ATTN_HEAD_DIM = 64
ATTN_WIDTH = ATTN_HEADS * ATTN_HEAD_DIM
WINDOW = 128
BLOCK = 128
REL_BUCKETS = 32
REL_MAX_DIST = 128
SSM_GROUP_CH = 16
SSM_GROUPS = 32
SSM_STATE = 64
SSM_WIDTH = SSM_GROUPS * SSM_GROUP_CH
XATTN_HEADS = 4
XATTN_HEAD_DIM = 128
XATTN_WIDTH = XATTN_HEADS * XATTN_HEAD_DIM
N_EXPERT_GROUPS = 8
EXPERTS_PER_GROUP = 8
N_EXPERTS = N_EXPERT_GROUPS * EXPERTS_PER_GROUP
D_EXPERT = 512
MOE_BLOCK = 128

LANES = 128
SSM_GROUPS_PER_TILE = LANES // SSM_GROUP_CH
SSM_TILES = SSM_WIDTH // LANES
SSM_TILE_STATE = SSM_GROUPS_PER_TILE * SSM_STATE
SSM_STATES = SSM_GROUPS * SSM_STATE
META_LANES = 512
VMEM_LIMIT = 56 * 1024 * 1024

_NT = (((1,), (1,)), ((), ()))


def _cparams(*sem):
    return pltpu.CompilerParams(dimension_semantics=sem, vmem_limit_bytes=VMEM_LIMIT)


def _rms(x, gain):
    ms = jnp.mean(x * x, axis=-1, keepdims=True)
    return x * lax.rsqrt(ms + EPS) * gain


def _t5_bucket_table():
    qi = np.arange(BLOCK, dtype=np.int32)[:, None]
    ki = np.arange(2 * BLOCK, dtype=np.int32)[None, :]
    delta = BLOCK + qi - ki
    n = np.maximum(delta, 0)
    max_exact = REL_BUCKETS // 2
    nf = np.maximum(n, 1).astype(np.float32)
    large = max_exact + (np.log(nf / np.float32(max_exact)) / np.float32(math.log(REL_MAX_DIST / max_exact))
                         * np.float32(REL_BUCKETS - max_exact)).astype(np.int32)
    large = np.minimum(large, REL_BUCKETS - 1)
    return np.where(n < max_exact, n, large).astype(np.int32)


def _bias_kernel(rb_ref, bucket_ref, out_ref):
    h = pl.program_id(0)
    bucket = bucket_ref[...]
    acc = jnp.zeros(bucket.shape, F32)
    for b in range(REL_BUCKETS):
        acc = jnp.where(bucket == b, rb_ref[b, h], acc)
    qi = lax.broadcasted_iota(jnp.int32, bucket.shape, 0)
    ki = lax.broadcasted_iota(jnp.int32, bucket.shape, 1)
    delta = BLOCK + qi - ki
    valid = jnp.logical_and(delta >= 0, delta < WINDOW)
    out_ref[0] = jnp.where(valid, acc, jnp.float32(-1e30))


def _bias_table(rel_bias):
    bucket = jnp.asarray(_t5_bucket_table())
    return pl.pallas_call(
        _bias_kernel,
        grid=(ATTN_HEADS,),
        in_specs=[pl.BlockSpec(memory_space=pltpu.SMEM),
                  pl.BlockSpec((BLOCK, 2 * BLOCK), lambda h: (0, 0))],
        out_specs=pl.BlockSpec((1, BLOCK, 2 * BLOCK), lambda h: (h, 0, 0)),
        out_shape=jax.ShapeDtypeStruct((ATTN_HEADS, BLOCK, 2 * BLOCK), F32),
        compiler_params=_cparams("arbitrary"),
        name="t5_bias_table",
    )(rel_bias.astype(F32), bucket)


def _ssm_prep_kernel(lr_ref, li_ref, ldt_ref, br_ref, bi_ref, are_ref, aim_ref, bbr_ref, bbi_ref):
    lr = lr_ref[...]
    li = li_ref[...]
    dt = jnp.exp(ldt_ref[...])
    mag = jnp.exp(lr * dt)
    a_re = mag * jnp.cos(li * dt)
    a_im = mag * jnp.sin(li * dt)
    den = lr * lr + li * li
    nr = a_re - 1.0
    ni = a_im
    coef_re = (nr * lr + ni * li) / den
    coef_im = (ni * lr - nr * li) / den
    are_ref[...] = a_re
    aim_ref[...] = a_im
    br = br_ref[...]
    bi = bi_ref[...]
    bbr_ref[...] = coef_re * br - coef_im * bi
    bbi_ref[...] = coef_re * bi + coef_im * br


def _ssm_prep(lam_re, lam_im, log_dt, b_re, b_im):
    g, n, c = b_re.shape
    vec = jax.ShapeDtypeStruct((g, 1, n), F32)
    mat = jax.ShapeDtypeStruct((g, c, n), F32)
    return pl.pallas_call(
        _ssm_prep_kernel,
        out_shape=(vec, vec, mat, mat),
        name="ssm_discretise",
    )(lam_re.astype(F32).reshape(g, 1, n), lam_im.astype(F32).reshape(g, 1, n),
      log_dt.astype(F32).reshape(g, 1, 1),
      jnp.transpose(b_re.astype(F32), (0, 2, 1)), jnp.transpose(b_im.astype(F32), (0, 2, 1)))


def _block_diag_tiles(m):
    g, r, c = m.shape
    t = g // SSM_GROUPS_PER_TILE
    eye = jnp.eye(SSM_GROUPS_PER_TILE, dtype=m.dtype)
    m4 = m.reshape(t, SSM_GROUPS_PER_TILE, r, c)
    out = m4[:, :, :, None, :] * eye[None, :, None, :, None]
    return out.reshape(t, SSM_GROUPS_PER_TILE * r, SSM_GROUPS_PER_TILE * c)


def _inproj_kernel(x_ref, g_ref, w_ref, q_ref, k_ref, v_ref, u_ref, qx_ref):
    h = _rms(x_ref[...], g_ref[...]).astype(BF16)
    c0 = 0
    for ref in (q_ref, k_ref, v_ref, u_ref, qx_ref):
        c1 = c0 + ref.shape[-1]
        ref[...] = jnp.dot(h, w_ref[:, c0:c1], preferred_element_type=F32).astype(ref.dtype)
        c0 = c1


def _inproj(x2, gain, w_cols, batch, seq, tm):
    t, d = x2.shape
    nsb = seq // tm
    kvw = 2 * LANES
    ncol = w_cols.shape[1]
    row = lambda w: pl.BlockSpec((tm, w), lambda i: (i, 0))
    return pl.pallas_call(
        _inproj_kernel,
        grid=(t // tm,),
        in_specs=[row(d),
                  pl.BlockSpec((1, d), lambda i: (0, 0)),
                  pl.BlockSpec((d, ncol), lambda i: (0, 0))],
        out_specs=[row(ATTN_WIDTH), row(kvw), row(kvw),
                   pl.BlockSpec((tm, SSM_WIDTH), lambda i: (i % nsb, i // nsb)),
                   row(XATTN_WIDTH)],
        out_shape=[jax.ShapeDtypeStruct((t, ATTN_WIDTH), BF16),
                   jax.ShapeDtypeStruct((t, kvw), BF16),
                   jax.ShapeDtypeStruct((t, kvw), BF16),
                   jax.ShapeDtypeStruct((seq, batch * SSM_WIDTH), BF16),
                   jax.ShapeDtypeStruct((t, XATTN_WIDTH), BF16)],
        compiler_params=_cparams("arbitrary"),
        name="in_proj",
    )(x2, gain, w_cols)


def _swa_kernel(sinks_ref, q_ref, kp_ref, kc_ref, vp_ref, vc_ref, bias_ref, qg_ref, kg_ref, og_ref, out_ref):
    n = pl.program_id(1)
    kcat = jnp.concatenate([kp_ref[...], kc_ref[...]], axis=0).astype(F32)
    vcat = jnp.concatenate([vp_ref[...], vc_ref[...]], axis=0)
    lane = lax.broadcasted_iota(jnp.int32, (1, LANES), 1)
    lo = lane < ATTN_HEAD_DIM
    key = lax.broadcasted_iota(jnp.int32, (1, 2 * BLOCK), 1)
    kmask = jnp.where(jnp.logical_and(n == 0, key < BLOCK), jnp.float32(-1e30), jnp.float32(0.0))
    qgain = qg_ref[...]
    pairs_per_kv = ATTN_HEADS // ATTN_KV_HEADS // 2
    outs = []
    for g in range(ATTN_KV_HEADS):
        kn = _rms(kcat[:, g * LANES:(g + 1) * LANES], kg_ref[...]).astype(BF16)
        vg = vcat[:, g * LANES:(g + 1) * LANES]
        for j in range(pairs_per_kv):
            tile = g * pairs_per_kv + j
            qp = q_ref[:, tile * LANES:(tile + 1) * LANES].astype(F32)
            sq = qp * qp
            s_lo = jnp.sum(jnp.where(lo, sq, 0.0), axis=-1, keepdims=True)
            s_hi = jnp.sum(sq, axis=-1, keepdims=True) - s_lo
            inv = jnp.where(lo, lax.rsqrt(s_lo / ATTN_HEAD_DIM + EPS), lax.rsqrt(s_hi / ATTN_HEAD_DIM + EPS))
            qn = qp * inv * qgain
            halves = []
            for half in range(2):
                hd = 2 * tile + half
                keep = lo if half == 0 else jnp.logical_not(lo)
                qh = jnp.where(keep, qn, 0.0).astype(BF16)
                l = lax.dot_general(qh, kn, _NT, preferred_element_type=F32)
                l = l + bias_ref[hd] + kmask
                sink = sinks_ref[hd]
                m = jnp.maximum(jnp.max(l, axis=-1, keepdims=True), sink)
                p = jnp.exp(l - m)
                den = jnp.sum(p, axis=-1, keepdims=True) + jnp.exp(sink - m)
                halves.append(jnp.dot(p.astype(BF16), vg, preferred_element_type=F32) / den)
            outs.append(jnp.where(lo, halves[0], halves[1]))
    y = jnp.concatenate(outs, axis=-1)
    out_ref[...] = _rms(y, og_ref[...]).astype(out_ref.dtype)


def _swa(q, kk, vv, bias, sinks, qgain, kgain, ogain, batch, seq):
    nb = seq // BLOCK
    kvw = 2 * LANES
    cur = lambda b, n: (b * nb + n, 0)
    prev = lambda b, n: (b * nb + jnp.maximum(n - 1, 0), 0)
    const2 = lambda b, n: (0, 0)
    return pl.pallas_call(
        _swa_kernel,
        grid=(batch, nb),
        in_specs=[pl.BlockSpec(memory_space=pltpu.SMEM),
                  pl.BlockSpec((BLOCK, ATTN_WIDTH), cur),
                  pl.BlockSpec((BLOCK, kvw), prev), pl.BlockSpec((BLOCK, kvw), cur),
                  pl.BlockSpec((BLOCK, kvw), prev), pl.BlockSpec((BLOCK, kvw), cur),
                  pl.BlockSpec((ATTN_HEADS, BLOCK, 2 * BLOCK), lambda b, n: (0, 0, 0)),
                  pl.BlockSpec((1, LANES), const2), pl.BlockSpec((1, LANES), const2),
                  pl.BlockSpec((1, ATTN_WIDTH), const2)],
        out_specs=pl.BlockSpec((BLOCK, ATTN_WIDTH), cur),
        out_shape=jax.ShapeDtypeStruct((batch * seq, ATTN_WIDTH), BF16),
        compiler_params=_cparams("arbitrary", "arbitrary"),
        name="swa_attention",
    )(sinks, q, kk, kk, vv, vv, bias, qgain, kgain, ogain)


def _memkv_kernel(m_ref, g_ref, w_ref, kg_ref, k_ref, v_ref):
    h = _rms(m_ref[...], g_ref[...]).astype(BF16)
    km = jnp.dot(h, w_ref[:, :XATTN_WIDTH], preferred_element_type=F32)
    for hd in range(XATTN_HEADS):
        sl = slice(hd * XATTN_HEAD_DIM, (hd + 1) * XATTN_HEAD_DIM)
        k_ref[:, sl] = _rms(km[:, sl], kg_ref[...]).astype(k_ref.dtype)
    v_ref[...] = jnp.dot(h, w_ref[:, XATTN_WIDTH:], preferred_element_type=F32).astype(v_ref.dtype)


def _memkv(mem2, gain, w_kv, kgain, tm):
    r, d = mem2.shape
    row = lambda w: pl.BlockSpec((tm, w), lambda i: (i, 0))
    const = lambda shape: pl.BlockSpec(shape, lambda i: (0, 0))
    return pl.pallas_call(
        _memkv_kernel,
        grid=(r // tm,),
        in_specs=[row(d), const((1, d)), const((d, 2 * XATTN_WIDTH)), const((1, XATTN_HEAD_DIM))],
        out_specs=[row(XATTN_WIDTH), row(XATTN_WIDTH)],
        out_shape=[jax.ShapeDtypeStruct((r, XATTN_WIDTH), BF16)] * 2,
        compiler_params=_cparams("arbitrary"),
        name="mem_kv_proj",
    )(mem2, gain, w_kv, kgain)


def _xattn_kernel(q_ref, k_ref, v_ref, qg_ref, og_ref, out_ref):
    outs = []
    for hd in range(XATTN_HEADS):
        sl = slice(hd * XATTN_HEAD_DIM, (hd + 1) * XATTN_HEAD_DIM)
        qn = _rms(q_ref[:, sl].astype(F32), qg_ref[...]).astype(BF16)
        l = lax.dot_general(qn, k_ref[:, sl], _NT, preferred_element_type=F32)
        m = jnp.max(l, axis=-1, keepdims=True)
        p = jnp.exp(l - m)
        den = jnp.sum(p, axis=-1, keepdims=True)
        outs.append(jnp.dot(p.astype(BF16), v_ref[:, sl], preferred_element_type=F32) / den)
    y = jnp.concatenate(outs, axis=-1)
    out_ref[...] = _rms(y, og_ref[...]).astype(out_ref.dtype)


def _xattn(qx, km, vm, qgain, ogain, batch, seq, mem_len, tq):
    nq = seq // tq
    const2 = lambda b, i: (0, 0)
    return pl.pallas_call(
        _xattn_kernel,
        grid=(batch, nq),
        in_specs=[pl.BlockSpec((tq, XATTN_WIDTH), lambda b, i: (b * nq + i, 0)),
                  pl.BlockSpec((mem_len, XATTN_WIDTH), lambda b, i: (b, 0)),
                  pl.BlockSpec((mem_len, XATTN_WIDTH), lambda b, i: (b, 0)),
                  pl.BlockSpec((1, XATTN_HEAD_DIM), const2),
                  pl.BlockSpec((1, XATTN_WIDTH), const2)],
        out_specs=pl.BlockSpec((tq, XATTN_WIDTH), lambda b, i: (b * nq + i, 0)),
        out_shape=jax.ShapeDtypeStruct((batch * seq, XATTN_WIDTH), BF16),
        compiler_params=_cparams("arbitrary", "arbitrary"),
        name="mem_xattn",
    )(qx, km, vm, qgain, ogain)


def _ssm_kernel(u_ref, bmat_ref, cmat_ref, are_ref, aim_ref, d_ref, wglu_ref, og_ref, out_ref,
                xr_ref, xi_ref, sr_ref, si_ref, *, batch, steps):
    @pl.when(pl.program_id(0) == 0)
    def _():
        sr_ref[...] = jnp.zeros_like(sr_ref)
        si_ref[...] = jnp.zeros_like(si_ref)

    u = u_ref[...]
    for j in range(SSM_TILES):
        bu = jnp.dot(u[:, j * LANES:(j + 1) * LANES], bmat_ref[j], preferred_element_type=F32)
        xr_ref[:, j * SSM_TILE_STATE:(j + 1) * SSM_TILE_STATE] = bu[:, :SSM_TILE_STATE]
        xi_ref[:, j * SSM_TILE_STATE:(j + 1) * SSM_TILE_STATE] = bu[:, SSM_TILE_STATE:]

    chunk = 1024
    for c0 in range(0, SSM_STATES, chunk):
        cs = slice(c0, c0 + chunk)
        ar = jnp.broadcast_to(are_ref[:, cs], (batch, chunk))
        ai = jnp.broadcast_to(aim_ref[:, cs], (batch, chunk))

        def step(t, carry, cs=cs, ar=ar, ai=ai):
            s_r, s_i = carry
            rows = pl.ds(pl.multiple_of(t * batch, batch), batch)
            n_r = ar * s_r - ai * s_i + xr_ref[rows, cs]
            n_i = ar * s_i + ai * s_r + xi_ref[rows, cs]
            xr_ref[rows, cs] = n_r
            xi_ref[rows, cs] = n_i
            return n_r, n_i

        s_r, s_i = lax.fori_loop(0, steps, step, (sr_ref[:, cs], si_ref[:, cs]))
        sr_ref[:, cs] = s_r
        si_ref[:, cs] = s_i

    ys = []
    for j in range(SSM_TILES):
        sl = slice(j * SSM_TILE_STATE, (j + 1) * SSM_TILE_STATE)
        xcat = jnp.concatenate([xr_ref[:, sl], xi_ref[:, sl]], axis=-1).astype(BF16)
        ys.append(jnp.dot(xcat, cmat_ref[j], preferred_element_type=F32))
    y = jnp.concatenate(ys, axis=-1) + d_ref[...] * u.astype(F32)
    y = jax.nn.gelu(y)
    y = y * jax.nn.sigmoid(jnp.dot(y.astype(BF16), wglu_ref[...], preferred_element_type=F32))
    out_ref[...] = _rms(y, og_ref[...]).astype(out_ref.dtype)


def _ssm(u_tb, bmat, cmat, a_re, a_im, d_skip, w_glu, ogain, batch, seq, steps):
    rows = steps * batch
    const2 = lambda c: (0, 0)
    const3 = lambda c: (0, 0, 0)
    return pl.pallas_call(
        functools.partial(_ssm_kernel, batch=batch, steps=steps),
        grid=(seq // steps,),
        in_specs=[pl.BlockSpec((rows, SSM_WIDTH), lambda c: (c, 0)),
                  pl.BlockSpec(bmat.shape, const3), pl.BlockSpec(cmat.shape, const3),
                  pl.BlockSpec((1, SSM_STATES), const2), pl.BlockSpec((1, SSM_STATES), const2),
                  pl.BlockSpec((1, SSM_WIDTH), const2),
                  pl.BlockSpec((SSM_WIDTH, SSM_WIDTH), const2),
                  pl.BlockSpec((1, SSM_WIDTH), const2)],
        out_specs=pl.BlockSpec((rows, SSM_WIDTH), lambda c: (c, 0)),
        out_shape=jax.ShapeDtypeStruct((seq * batch, SSM_WIDTH), BF16),
        scratch_shapes=[pltpu.VMEM((rows, SSM_STATES), F32), pltpu.VMEM((rows, SSM_STATES), F32),
                        pltpu.VMEM((batch, SSM_STATES), F32), pltpu.VMEM((batch, SSM_STATES), F32)],
        compiler_params=_cparams("arbitrary"),
        name="s5_layer",
    )(u_tb, bmat, cmat, a_re, a_im, d_skip, w_glu, ogain)


def _outproj_kernel(ya_ref, ys_ref, yx_ref, x_ref, wo_ref, g_ref, wrh_ref, wrl_ref, x1_ref, lt_ref):
    a0, a1 = ATTN_WIDTH, ATTN_WIDTH + SSM_WIDTH
    acc = jnp.dot(ya_ref[...], wo_ref[:a0, :], preferred_element_type=F32)
    acc = acc + jnp.dot(ys_ref[...], wo_ref[a0:a1, :], preferred_element_type=F32)
    acc = acc + jnp.dot(yx_ref[...], wo_ref[a1:, :], preferred_element_type=F32)
    x1 = x_ref[...] + acc
    x1_ref[...] = x1
    h2 = _rms(x1, g_ref[...])
    hi = h2.astype(BF16)
    lo = (h2 - hi.astype(F32)).astype(BF16)
    lt = lax.dot_general(wrh_ref[...], hi, _NT, preferred_element_type=F32)
    lt = lt + lax.dot_general(wrl_ref[...], hi, _NT, preferred_element_type=F32)
    lt = lt + lax.dot_general(wrh_ref[...], lo, _NT, preferred_element_type=F32)
    lt_ref[...] = lt


def _outproj(ya, ys_sb, yx, x2, w_o, gain, wr_hi, wr_lo, batch, seq, tm):
    t, d = x2.shape
    nsb = seq // tm
    row = lambda w: pl.BlockSpec((tm, w), lambda i: (i, 0))
    const = lambda shape: pl.BlockSpec(shape, lambda i: (0, 0))
    return pl.pallas_call(
        _outproj_kernel,
        grid=(t // tm,),
        in_specs=[row(ATTN_WIDTH),
                  pl.BlockSpec((tm, SSM_WIDTH), lambda i: (i % nsb, i // nsb)),
                  row(XATTN_WIDTH), row(d),
                  const(w_o.shape), const((1, d)), const(wr_hi.shape), const(wr_lo.shape)],
        out_specs=[row(d), pl.BlockSpec((LANES, tm), lambda i: (0, i))],
        out_shape=[jax.ShapeDtypeStruct((t, d), F32), jax.ShapeDtypeStruct((LANES, t), F32)],
        compiler_params=_cparams("arbitrary"),
        name="out_proj_router",
    )(ya, ys_sb, yx, x2, w_o, gain, wr_hi, wr_lo)


def _route_kernel(lt_ref, tri_ref, dest_ref, w_ref, meta_ref, cnt_ref, carry_ref, pstart_ref):
    phase = pl.program_id(0)
    c = pl.program_id(1)
    logits = lt_ref[...]
    tc = logits.shape[1]
    ng, epg = N_EXPERT_GROUPS, EXPERTS_PER_GROUP
    row8 = lax.broadcasted_iota(jnp.int32, (ng, tc), 0)

    gl = logits[0:ng]
    gmax = jnp.max(gl, axis=0, keepdims=True)
    gidx = jnp.min(jnp.where(gl == gmax, row8, ng), axis=0, keepdims=True)
    gate = 1.0 / jnp.sum(jnp.exp(gl - gmax), axis=0, keepdims=True)
    sel = jnp.zeros((epg, tc), F32)
    for g in range(ng):
        sel = jnp.where(gidx == g, logits[ng + g * epg:ng + (g + 1) * epg], sel)
    v1 = jnp.max(sel, axis=0, keepdims=True)
    i1 = jnp.min(jnp.where(sel == v1, row8, epg), axis=0, keepdims=True)
    sel2 = jnp.where(row8 == i1, -jnp.inf, sel)
    v2 = jnp.max(sel2, axis=0, keepdims=True)
    i2 = jnp.min(jnp.where(sel2 == v2, row8, epg), axis=0, keepdims=True)
    e = jnp.exp(v2 - v1)
    w1 = gate * (1.0 / (1.0 + e))
    w2 = gate * (e / (1.0 + e))
    e1 = gidx * epg + i1
    e2 = gidx * epg + i2
    rowe = lax.broadcasted_iota(jnp.int32, (N_EXPERTS, tc), 0)
    oh1 = rowe == e1
    oh2 = rowe == e2
    member = jnp.where(jnp.logical_or(oh1, oh2), 1.0, 0.0)
    chunk_cnt = jnp.sum(member, axis=1, keepdims=True)

    @pl.when(phase == 0)
    def _():
        @pl.when(c == 0)
        def _():
            cnt_ref[...] = jnp.zeros_like(cnt_ref)
        cnt_ref[...] += chunk_cnt

    @pl.when(phase == 1)
    def _():
        @pl.when(c == 0)
        def _():
            cnt = cnt_ref[...]
            padded = jnp.floor((cnt + (MOE_BLOCK - 1)) * (1.0 / MOE_BLOCK)) * MOE_BLOCK
            r = lax.broadcasted_iota(jnp.int32, (N_EXPERTS, N_EXPERTS), 0)
            cidx = lax.broadcasted_iota(jnp.int32, (N_EXPERTS, N_EXPERTS), 1)
            prow = jnp.sum(jnp.where(r == cidx, padded, 0.0), axis=0, keepdims=True)
            pends = jnp.sum(jnp.where(cidx <= r, prow, 0.0), axis=1, keepdims=True)
            pstart_ref[...] = pends - padded
            carry_ref[...] = jnp.zeros_like(carry_ref)
            lane = lax.broadcasted_iota(jnp.int32, (1, META_LANES), 1)
            bstart = (lane * MOE_BLOCK).astype(F32)
            blk = jnp.sum(jnp.where(pends <= bstart, 1.0, 0.0), axis=0, keepdims=True)
            blk = jnp.minimum(blk, N_EXPERTS - 1.0)
            n_used = jnp.sum(padded, axis=0, keepdims=True) * (1.0 / MOE_BLOCK)
            meta_ref[...] = jnp.where(lane == META_LANES - 1, n_used, blk).astype(jnp.int32)

        before = carry_ref[...] + jnp.dot(member.astype(BF16), tri_ref[...], preferred_element_type=F32)
        pos = before + pstart_ref[...]
        dest_ref[0:1, :] = jnp.sum(jnp.where(oh1, pos, 0.0), axis=0, keepdims=True).astype(jnp.int32)
        dest_ref[1:2, :] = jnp.sum(jnp.where(oh2, pos, 0.0), axis=0, keepdims=True).astype(jnp.int32)
        w_ref[0:1, :] = w1
        w_ref[1:2, :] = w2
        carry_ref[...] += chunk_cnt


def _route(logits_t, tc):
    t = logits_t.shape[1]
    nc = t // tc
    tri = jnp.asarray(np.triu(np.ones((tc, tc), np.float32), k=1), dtype=BF16)
    col = jax.ShapeDtypeStruct((N_EXPERTS, 1), F32)
    return pl.pallas_call(
        _route_kernel,
        grid=(2, nc),
        in_specs=[pl.BlockSpec((LANES, tc), lambda p, c: (0, c)),
                  pl.BlockSpec((tc, tc), lambda p, c: (0, 0))],
        out_specs=[pl.BlockSpec((2, tc), lambda p, c: (0, c * p)),
                   pl.BlockSpec((2, tc), lambda p, c: (0, c * p)),
                   pl.BlockSpec((1, META_LANES), lambda p, c: (0, 0))],
        out_shape=[jax.ShapeDtypeStruct((2, t), jnp.int32), jax.ShapeDtypeStruct((2, t), F32),
                   jax.ShapeDtypeStruct((1, META_LANES), jnp.int32)],
        scratch_shapes=[pltpu.VMEM((N_EXPERTS, 1), F32)] * 3,
        compiler_params=_cparams("arbitrary", "arbitrary"),
        name="moe_route",
    )(logits_t, tri)


def _invert_kernel(dest_ref, slot_ref, *, tokens, slots):
    def zero(i, carry):
        slot_ref[i] = 0
        return carry
    lax.fori_loop(0, slots, zero, 0)

    def put(t, carry):
        slot_ref[dest_ref[t]] = t
        slot_ref[dest_ref[tokens + t]] = t
        return carry
    lax.fori_loop(0, tokens, put, 0)


def _invert(dest_flat, tokens, slots):
    return pl.pallas_call(
        functools.partial(_invert_kernel, tokens=tokens, slots=slots),
        in_specs=[pl.BlockSpec(memory_space=pltpu.SMEM)],
        out_specs=pl.BlockSpec(memory_space=pltpu.SMEM),
        out_shape=jax.ShapeDtypeStruct((slots,), jnp.int32),
        name="moe_slot_tokens",
    )(dest_flat)


def _expert_kernel(meta_ref, slot_ref, x1_ref, g_ref, wg_ref, wu_ref, wd_ref, out_ref,
                   xbuf, wg_bf, wu_bf, wd_bf, sem):
    j = pl.program_id(0)
    n_used = meta_ref[META_LANES - 1]

    @pl.when(j >= n_used)
    def _():
        out_ref[...] = jnp.zeros_like(out_ref)

    @pl.when(j < n_used)
    def _():
        base = j * MOE_BLOCK

        def issue(r, carry):
            tok = slot_ref[base + r]
            pltpu.make_async_copy(x1_ref.at[pl.ds(tok, 1), :], xbuf.at[pl.ds(r, 1), :], sem).start()
            return carry
        lax.fori_loop(0, MOE_BLOCK, issue, 0)

        new_expert = jnp.logical_or(j == 0, meta_ref[j] != meta_ref[jnp.maximum(j - 1, 0)])

        @pl.when(new_expert)
        def _():
            wg_bf[...] = wg_ref[0].astype(BF16)
            wu_bf[...] = wu_ref[0].astype(BF16)
            wd_bf[...] = wd_ref[0].astype(BF16)

        pltpu.make_async_copy(x1_ref.at[pl.ds(0, MOE_BLOCK), :], xbuf, sem).wait()
        h = _rms(xbuf[...], g_ref[...]).astype(BF16)
        gate = jnp.dot(h, wg_bf[...], preferred_element_type=F32)
        up = jnp.dot(h, wu_bf[...], preferred_element_type=F32)
        act = (jax.nn.silu(gate) * up).astype(BF16)
        out_ref[...] = jnp.dot(act, wd_bf[...], preferred_element_type=F32)


def _experts(meta, slot_tok, x1, gain, w_gate, w_up, w_down, n_blocks):
    t, d = x1.shape
    de = w_gate.shape[-1]

    def blk(j, meta_ref, slot_ref):
        return jnp.minimum(j, meta_ref[META_LANES - 1] - 1)

    wspec = lambda shape: pl.BlockSpec(shape, lambda j, m, s: (m[blk(j, m, s)], 0, 0))
    grid_spec = pltpu.PrefetchScalarGridSpec(
        num_scalar_prefetch=2,
        grid=(n_blocks,),
        in_specs=[pl.BlockSpec(memory_space=pl.ANY),
                  pl.BlockSpec((1, d), lambda j, m, s: (0, 0)),
                  wspec((1, d, de)), wspec((1, d, de)), wspec((1, de, d))],
        out_specs=pl.BlockSpec((MOE_BLOCK, d), lambda j, m, s: (j, 0)),
        scratch_shapes=[pltpu.VMEM((MOE_BLOCK, d), F32),
                        pltpu.VMEM((d, de), BF16), pltpu.VMEM((d, de), BF16), pltpu.VMEM((de, d), BF16),
                        pltpu.SemaphoreType.DMA(())],
    )
    return pl.pallas_call(
        _expert_kernel,
        grid_spec=grid_spec,
        out_shape=jax.ShapeDtypeStruct((n_blocks * MOE_BLOCK, d), F32),
        compiler_params=_cparams("arbitrary"),
        name="moe_experts",
    )(meta, slot_tok, x1, gain, w_gate, w_up, w_down)


def _combine_kernel(dest_ref, yb_ref, x1_ref, w_ref, out_ref, gbuf, sem, *, tokens):
    tm = x1_ref.shape[0]
    base = pl.program_id(0) * tm

    def issue(r, carry):
        for k in range(2):
            d = dest_ref[k * tokens + base + r]
            pltpu.make_async_copy(yb_ref.at[pl.ds(d, 1), :], gbuf.at[k, pl.ds(r, 1), :], sem).start()
        return carry
    lax.fori_loop(0, tm, issue, 0)
    for k in range(2):
        pltpu.make_async_copy(yb_ref.at[pl.ds(0, tm), :], gbuf.at[k], sem).wait()
    w = w_ref[...]
    out_ref[...] = x1_ref[...] + (gbuf[0] * w[:, 0:1] + gbuf[1] * w[:, 1:2])


def _combine(dest_flat, yb, x1, w_tok, tm):
    t, d = x1.shape
    grid_spec = pltpu.PrefetchScalarGridSpec(
        num_scalar_prefetch=1,
        grid=(t // tm,),
        in_specs=[pl.BlockSpec(memory_space=pl.ANY),
                  pl.BlockSpec((tm, d), lambda i, dr: (i, 0)),
                  pl.BlockSpec((tm, 2), lambda i, dr: (i, 0))],
        out_specs=pl.BlockSpec((tm, d), lambda i, dr: (i, 0)),
        scratch_shapes=[pltpu.VMEM((2, tm, d), F32), pltpu.SemaphoreType.DMA(())],
    )
    return pl.pallas_call(
        functools.partial(_combine_kernel, tokens=t),
        grid_spec=grid_spec,
        out_shape=jax.ShapeDtypeStruct((t, d), F32),
        compiler_params=_cparams("arbitrary"),
        name="moe_combine",
    )(dest_flat, yb, x1, w_tok)


def _row(v):
    return v.astype(F32).reshape(1, -1)


def _layer(x2, mem2, batch, seq, mem_len, p):
    t, d = x2.shape

    w_in = p["w_in"]
    a0 = ATTN_WIDTH
    kw = ATTN_KV_HEADS * ATTN_HEAD_DIM
    dup = lambda w: jnp.concatenate(
        [w[:, h * ATTN_HEAD_DIM:(h + 1) * ATTN_HEAD_DIM] for h in range(ATTN_KV_HEADS) for _ in range(2)], axis=1)
    w_cols = jnp.concatenate([w_in[:, :a0], dup(w_in[:, a0:a0 + kw]), dup(w_in[:, a0 + kw:a0 + 2 * kw]),
                              w_in[:, a0 + 2 * kw:]], axis=1).astype(BF16)
    tm_in = min(256, seq)
    q, kk, vv, u_sb, qx = _inproj(x2, _row(p["norm_mix"]), w_cols, batch, seq, tm_in)

    bias = _bias_table(p["rel_bias"])
    qgain = jnp.tile(_row(p["q_norm"]), (1, 2)) * (1.0 / math.sqrt(ATTN_HEAD_DIM))
    kgain = jnp.tile(_row(p["k_norm"]), (1, 2))
    ya = _swa(q, kk, vv, bias, p["attn_sinks"].astype(F32), qgain, kgain, _row(p["out_norm_attn"]), batch, seq)

    km, vm = _memkv(mem2, _row(p["mem_norm"]), p["w_mem_kv"].astype(BF16), _row(p["xk_norm"]),
                    min(256, mem2.shape[0]))
    xq_gain = _row(p["xq_norm"]) * (1.0 / math.sqrt(XATTN_HEAD_DIM))
    yx = _xattn(qx, km, vm, xq_gain, _row(p["out_norm_xattn"]), batch, seq, mem_len, min(512, seq))

    a_re, a_im, bbr, bbi = _ssm_prep(p["ssm_lambda_re"], p["ssm_lambda_im"], p["ssm_log_dt"],
                                     p["ssm_b_re"], p["ssm_b_im"])
    bmat = jnp.concatenate([_block_diag_tiles(bbr), _block_diag_tiles(bbi)], axis=-1).astype(BF16)
    c_re_t = jnp.transpose(p["ssm_c_re"].astype(F32), (0, 2, 1))
    c_im_t = jnp.transpose(p["ssm_c_im"].astype(F32), (0, 2, 1))
    cmat = jnp.concatenate([_block_diag_tiles(c_re_t), _block_diag_tiles(-c_im_t)], axis=1).astype(BF16)
    steps = min(64, seq)
    ys_tb = _ssm(u_sb.reshape(seq * batch, SSM_WIDTH), bmat, cmat,
                 a_re.reshape(1, SSM_STATES), a_im.reshape(1, SSM_STATES), _row(p["ssm_d"]),
                 p["ssm_w_glu"].astype(BF16), _row(p["out_norm_ssm"]), batch, seq, steps)

    wr = jnp.concatenate([p["w_router_group"], p["w_router_expert"]], axis=1).astype(F32)
    wr = jnp.pad(wr, ((0, 0), (0, LANES - wr.shape[1]))).T
    wr_hi = wr.astype(BF16)
    wr_lo = (wr - wr_hi.astype(F32)).astype(BF16)
    tm_out = min(256, seq)
    x1, logits_t = _outproj(ya, ys_tb.reshape(seq, batch * SSM_WIDTH), yx, x2, p["w_o"].astype(BF16),
                            _row(p["norm_ffn"]), wr_hi, wr_lo, batch, seq, tm_out)

    dest, w_k, meta = _route(logits_t, min(512, t))
    slots = 2 * t + N_EXPERTS * MOE_BLOCK
    n_blocks = slots // MOE_BLOCK
    dest_flat = dest.reshape(2 * t)
    slot_tok = _invert(dest_flat, t, slots)
    yb = _experts(meta.reshape(META_LANES), slot_tok, x1, _row(p["norm_ffn"]),
                  p["w_gate"], p["w_up"], p["w_down"], n_blocks)
    return _combine(dest_flat, yb, x1, w_k.T, min(256, t))


def kernel(x, mem, norm_mix, w_in, q_norm, k_norm, attn_sinks, rel_bias, ssm_lambda_re, ssm_lambda_im, ssm_log_dt, ssm_b_re, ssm_b_im, ssm_c_re, ssm_c_im, ssm_d, ssm_w_glu, mem_norm, w_mem_kv, xq_norm, xk_norm, out_norm_attn, out_norm_ssm, out_norm_xattn, w_o, norm_ffn, w_router_group, w_router_expert, w_gate, w_up, w_down):
    batch, seq, d = x.shape
    mem_len = mem.shape[1]
    per_layer = dict(norm_mix=norm_mix, w_in=w_in, q_norm=q_norm, k_norm=k_norm, attn_sinks=attn_sinks,
                     ssm_lambda_re=ssm_lambda_re, ssm_lambda_im=ssm_lambda_im, ssm_log_dt=ssm_log_dt,
                     ssm_b_re=ssm_b_re, ssm_b_im=ssm_b_im, ssm_c_re=ssm_c_re, ssm_c_im=ssm_c_im,
                     ssm_d=ssm_d, ssm_w_glu=ssm_w_glu, mem_norm=mem_norm, w_mem_kv=w_mem_kv,
                     xq_norm=xq_norm, xk_norm=xk_norm, out_norm_attn=out_norm_attn,
                     out_norm_ssm=out_norm_ssm, out_norm_xattn=out_norm_xattn, w_o=w_o, norm_ffn=norm_ffn,
                     w_router_group=w_router_group, w_router_expert=w_router_expert,
                     w_gate=w_gate, w_up=w_up, w_down=w_down)
    x2 = x.astype(F32).reshape(batch * seq, d)
    mem2 = mem.astype(F32).reshape(batch * mem_len, d)
    for l in range(norm_mix.shape[0]):
        p = {k: v[l] for k, v in per_layer.items()}
        p["rel_bias"] = rel_bias
        x2 = _layer(x2, mem2, batch, seq, mem_len, p)
    return x2.reshape(batch, seq, d).astype(x.dtype)
```

```python
import functools
import math

import numpy as np
import jax
import jax.numpy as jnp
from jax import lax
from jax.experimental import pallas as pl
from jax.experimental.pallas import tpu as pltpu

F32 = jnp.float32
BF16 = jnp.bfloat16
EPS = 1e-6

ATTN_HEADS = 16
ATTN_KV_HEADS = 2
ATTN_HEAD_DIM = 64
ATTN_WIDTH = ATTN_HEADS * ATTN_HEAD_DIM
WINDOW = 128
BLOCK = 128
REL_BUCKETS = 32
REL_MAX_DIST = 128
SSM_GROUP_CH = 16
SSM_GROUPS = 32
SSM_STATE = 64
SSM_WIDTH = SSM_GROUPS * SSM_GROUP_CH
XATTN_HEADS = 4
XATTN_HEAD_DIM = 128
XATTN_WIDTH = XATTN_HEADS * XATTN_HEAD_DIM
N_EXPERT_GROUPS = 8
EXPERTS_PER_GROUP = 8
N_EXPERTS = N_EXPERT_GROUPS * EXPERTS_PER_GROUP
D_EXPERT = 512
MOE_BLOCK = 128

LANES = 128
SSM_GROUPS_PER_TILE = LANES // SSM_GROUP_CH
SSM_TILES = SSM_WIDTH // LANES
SSM_TILE_STATE = SSM_GROUPS_PER_TILE * SSM_STATE
SSM_STATES = SSM_GROUPS * SSM_STATE
CHUNK_BLOCKS = 4
META_ROWS = 8
M_OWNER, M_FIRST, M_SIZE, M_NCHUNK, M_LAST, M_NBLK, M_NUSED = range(7)
VMEM_LIMIT = 56 * 1024 * 1024

_NT = (((1,), (1,)), ((), ()))


def _cparams(*sem):
    return pltpu.CompilerParams(dimension_semantics=sem, vmem_limit_bytes=VMEM_LIMIT)


def _rms(x, gain):
    ms = jnp.mean(x * x, axis=-1, keepdims=True)
    return x * lax.rsqrt(ms + EPS) * gain


def _t5_bucket_table():
    qi = np.arange(BLOCK, dtype=np.int32)[:, None]
    ki = np.arange(2 * BLOCK, dtype=np.int32)[None, :]
    delta = BLOCK + qi - ki
    n = np.maximum(delta, 0)
    max_exact = REL_BUCKETS // 2
    nf = np.maximum(n, 1).astype(np.float32)
    large = max_exact + (np.log(nf / np.float32(max_exact)) / np.float32(math.log(REL_MAX_DIST / max_exact))
                         * np.float32(REL_BUCKETS - max_exact)).astype(np.int32)
    large = np.minimum(large, REL_BUCKETS - 1)
    return np.where(n < max_exact, n, large).astype(np.int32)


def _bias_kernel(rb_ref, bucket_ref, out_ref):
    h = pl.program_id(0)
    bucket = bucket_ref[...]
    acc = jnp.zeros(bucket.shape, F32)
    for b in range(REL_BUCKETS):
        acc = jnp.where(bucket == b, rb_ref[b, h], acc)
    qi = lax.broadcasted_iota(jnp.int32, bucket.shape, 0)
    ki = lax.broadcasted_iota(jnp.int32, bucket.shape, 1)
    delta = BLOCK + qi - ki
    valid = jnp.logical_and(delta >= 0, delta < WINDOW)
    out_ref[0] = jnp.where(valid, acc, jnp.float32(-1e30))


def _bias_table(rel_bias):
    bucket = jnp.asarray(_t5_bucket_table())
    return pl.pallas_call(
        _bias_kernel,
        grid=(ATTN_HEADS,),
        in_specs=[pl.BlockSpec(memory_space=pltpu.SMEM),
                  pl.BlockSpec((BLOCK, 2 * BLOCK), lambda h: (0, 0))],
        out_specs=pl.BlockSpec((1, BLOCK, 2 * BLOCK), lambda h: (h, 0, 0)),
        out_shape=jax.ShapeDtypeStruct((ATTN_HEADS, BLOCK, 2 * BLOCK), F32),
        compiler_params=_cparams("arbitrary"),
        name="t5_bias_table",
    )(rel_bias.astype(F32), bucket)


def _ssm_prep_kernel(lr_ref, li_ref, ldt_ref, br_ref, bi_ref, are_ref, aim_ref, bbr_ref, bbi_ref):
    lr = lr_ref[...]
    li = li_ref[...]
    dt = jnp.exp(ldt_ref[...])
    mag = jnp.exp(lr * dt)
    a_re = mag * jnp.cos(li * dt)
    a_im = mag * jnp.sin(li * dt)
    den = lr * lr + li * li
    nr = a_re - 1.0
    ni = a_im
    coef_re = (nr * lr + ni * li) / den
    coef_im = (ni * lr - nr * li) / den
    are_ref[...] = a_re
    aim_ref[...] = a_im
    br = br_ref[...]
    bi = bi_ref[...]
    bbr_ref[...] = coef_re * br - coef_im * bi
    bbi_ref[...] = coef_re * bi + coef_im * br


def _ssm_prep(lam_re, lam_im, log_dt, b_re, b_im):
    g, n, c = b_re.shape
    vec = jax.ShapeDtypeStruct((g, 1, n), F32)
    mat = jax.ShapeDtypeStruct((g, c, n), F32)
    return pl.pallas_call(
        _ssm_prep_kernel,
        out_shape=(vec, vec, mat, mat),
        name="ssm_discretise",
    )(lam_re.astype(F32).reshape(g, 1, n), lam_im.astype(F32).reshape(g, 1, n),
      log_dt.astype(F32).reshape(g, 1, 1),
      jnp.transpose(b_re.astype(F32), (0, 2, 1)), jnp.transpose(b_im.astype(F32), (0, 2, 1)))


def _block_diag_tiles(m):
    g, r, c = m.shape
    t = g // SSM_GROUPS_PER_TILE
    eye = jnp.eye(SSM_GROUPS_PER_TILE, dtype=m.dtype)
    m4 = m.reshape(t, SSM_GROUPS_PER_TILE, r, c)
    out = m4[:, :, :, None, :] * eye[None, :, None, :, None]
    return out.reshape(t, SSM_GROUPS_PER_TILE * r, SSM_GROUPS_PER_TILE * c)


def _inproj_kernel(x_ref, g_ref, w_ref, q_ref, k_ref, v_ref, u_ref, qx_ref):
    h = _rms(x_ref[...], g_ref[...]).astype(BF16)
    c0 = 0
    for ref in (q_ref, k_ref, v_ref, u_ref, qx_ref):
        c1 = c0 + ref.shape[-1]
        ref[...] = jnp.dot(h, w_ref[:, c0:c1], preferred_element_type=F32).astype(ref.dtype)
        c0 = c1


def _inproj(x2, gain, w_cols, batch, seq, tm):
    t, d = x2.shape
    nsb = seq // tm
    kvw = 2 * LANES
    ncol = w_cols.shape[1]
    row = lambda w: pl.BlockSpec((tm, w), lambda i: (i, 0))
    return pl.pallas_call(
        _inproj_kernel,
        grid=(t // tm,),
        in_specs=[row(d),
                  pl.BlockSpec((1, d), lambda i: (0, 0)),
                  pl.BlockSpec((d, ncol), lambda i: (0, 0))],
        out_specs=[row(ATTN_WIDTH), row(kvw), row(kvw),
                   pl.BlockSpec((tm, SSM_WIDTH), lambda i: (i % nsb, i // nsb)),
                   row(XATTN_WIDTH)],
        out_shape=[jax.ShapeDtypeStruct((t, ATTN_WIDTH), BF16),
                   jax.ShapeDtypeStruct((t, kvw), BF16),
                   jax.ShapeDtypeStruct((t, kvw), BF16),
                   jax.ShapeDtypeStruct((seq, batch * SSM_WIDTH), BF16),
                   jax.ShapeDtypeStruct((t, XATTN_WIDTH), BF16)],
        compiler_params=_cparams("arbitrary"),
        name="in_proj",
    )(x2, gain, w_cols)


def _swa_kernel(sinks_ref, q_ref, kp_ref, kc_ref, vp_ref, vc_ref, bias_ref, qg_ref, kg_ref, og_ref, out_ref):
    n = pl.program_id(1)
    kcat = jnp.concatenate([kp_ref[...], kc_ref[...]], axis=0).astype(F32)
    vcat = jnp.concatenate([vp_ref[...], vc_ref[...]], axis=0)
    lane = lax.broadcasted_iota(jnp.int32, (1, LANES), 1)
    lo = lane < ATTN_HEAD_DIM
    key = lax.broadcasted_iota(jnp.int32, (1, 2 * BLOCK), 1)
    kmask = jnp.where(jnp.logical_and(n == 0, key < BLOCK), jnp.float32(-1e30), jnp.float32(0.0))
    qgain = qg_ref[...]
    pairs_per_kv = ATTN_HEADS // ATTN_KV_HEADS // 2
    outs = []
    for g in range(ATTN_KV_HEADS):
        kn = _rms(kcat[:, g * LANES:(g + 1) * LANES], kg_ref[...]).astype(BF16)
        vg = vcat[:, g * LANES:(g + 1) * LANES]
        for j in range(pairs_per_kv):
            tile = g * pairs_per_kv + j
            qp = q_ref[:, tile * LANES:(tile + 1) * LANES].astype(F32)
            sq = qp * qp
            s_lo = jnp.sum(jnp.where(lo, sq, 0.0), axis=-1, keepdims=True)
            s_hi = jnp.sum(sq, axis=-1, keepdims=True) - s_lo
            inv = jnp.where(lo, lax.rsqrt(s_lo / ATTN_HEAD_DIM + EPS), lax.rsqrt(s_hi / ATTN_HEAD_DIM + EPS))
            qn = qp * inv * qgain
            halves = []
            for half in range(2):
                hd = 2 * tile + half
                keep = lo if half == 0 else jnp.logical_not(lo)
                qh = jnp.where(keep, qn, 0.0).astype(BF16)
                l = lax.dot_general(qh, kn, _NT, preferred_element_type=F32)
                l = l + bias_ref[hd] + kmask
                sink = sinks_ref[hd]
                m = jnp.maximum(jnp.max(l, axis=-1, keepdims=True), sink)
                p = jnp.exp(l - m)
                den = jnp.sum(p, axis=-1, keepdims=True) + jnp.exp(sink - m)
                halves.append(jnp.dot(p.astype(BF16), vg, preferred_element_type=F32) / den)
            outs.append(jnp.where(lo, halves[0], halves[1]))
    y = jnp.concatenate(outs, axis=-1)
    out_ref[...] = _rms(y, og_ref[...]).astype(out_ref.dtype)


def _swa(q, kk, vv, bias, sinks, qgain, kgain, ogain, batch, seq):
    nb = seq // BLOCK
    kvw = 2 * LANES
    cur = lambda b, n: (b * nb + n, 0)
    prev = lambda b, n: (b * nb + jnp.maximum(n - 1, 0), 0)
    const2 = lambda b, n: (0, 0)
    return pl.pallas_call(
        _swa_kernel,
        grid=(batch, nb),
        in_specs=[pl.BlockSpec(memory_space=pltpu.SMEM),
                  pl.BlockSpec((BLOCK, ATTN_WIDTH), cur),
                  pl.BlockSpec((BLOCK, kvw), prev), pl.BlockSpec((BLOCK, kvw), cur),
                  pl.BlockSpec((BLOCK, kvw), prev), pl.BlockSpec((BLOCK, kvw), cur),
                  pl.BlockSpec((ATTN_HEADS, BLOCK, 2 * BLOCK), lambda b, n: (0, 0, 0)),
                  pl.BlockSpec((1, LANES), const2), pl.BlockSpec((1, LANES), const2),
                  pl.BlockSpec((1, ATTN_WIDTH), const2)],
        out_specs=pl.BlockSpec((BLOCK, ATTN_WIDTH), cur),
        out_shape=jax.ShapeDtypeStruct((batch * seq, ATTN_WIDTH), BF16),
        compiler_params=_cparams("arbitrary", "arbitrary"),
        name="swa_attention",
    )(sinks, q, kk, kk, vv, vv, bias, qgain, kgain, ogain)


def _memkv_kernel(m_ref, g_ref, w_ref, kg_ref, k_ref, v_ref):
    h = _rms(m_ref[...], g_ref[...]).astype(BF16)
    km = jnp.dot(h, w_ref[:, :XATTN_WIDTH], preferred_element_type=F32)
    for hd in range(XATTN_HEADS):
        sl = slice(hd * XATTN_HEAD_DIM, (hd + 1) * XATTN_HEAD_DIM)
        k_ref[:, sl] = _rms(km[:, sl], kg_ref[...]).astype(k_ref.dtype)
    v_ref[...] = jnp.dot(h, w_ref[:, XATTN_WIDTH:], preferred_element_type=F32).astype(v_ref.dtype)


def _memkv(mem2, gain, w_kv, kgain, tm):
    r, d = mem2.shape
    row = lambda w: pl.BlockSpec((tm, w), lambda i: (i, 0))
    const = lambda shape: pl.BlockSpec(shape, lambda i: (0, 0))
    return pl.pallas_call(
        _memkv_kernel,
        grid=(r // tm,),
        in_specs=[row(d), const((1, d)), const((d, 2 * XATTN_WIDTH)), const((1, XATTN_HEAD_DIM))],
        out_specs=[row(XATTN_WIDTH), row(XATTN_WIDTH)],
        out_shape=[jax.ShapeDtypeStruct((r, XATTN_WIDTH), BF16)] * 2,
        compiler_params=_cparams("arbitrary"),
        name="mem_kv_proj",
    )(mem2, gain, w_kv, kgain)


def _xattn_kernel(q_ref, k_ref, v_ref, qg_ref, og_ref, out_ref):
    outs = []
    for hd in range(XATTN_HEADS):
        sl = slice(hd * XATTN_HEAD_DIM, (hd + 1) * XATTN_HEAD_DIM)
        qn = _rms(q_ref[:, sl].astype(F32), qg_ref[...]).astype(BF16)
        l = lax.dot_general(qn, k_ref[:, sl], _NT, preferred_element_type=F32)
        m = jnp.max(l, axis=-1, keepdims=True)
        p = jnp.exp(l - m)
        den = jnp.sum(p, axis=-1, keepdims=True)
        outs.append(jnp.dot(p.astype(BF16), v_ref[:, sl], preferred_element_type=F32) / den)
    y = jnp.concatenate(outs, axis=-1)
    out_ref[...] = _rms(y, og_ref[...]).astype(out_ref.dtype)


def _xattn(qx, km, vm, qgain, ogain, batch, seq, mem_len, tq):
    nq = seq // tq
    const2 = lambda b, i: (0, 0)
    return pl.pallas_call(
        _xattn_kernel,
        grid=(batch, nq),
        in_specs=[pl.BlockSpec((tq, XATTN_WIDTH), lambda b, i: (b * nq + i, 0)),
                  pl.BlockSpec((mem_len, XATTN_WIDTH), lambda b, i: (b, 0)),
                  pl.BlockSpec((mem_len, XATTN_WIDTH), lambda b, i: (b, 0)),
                  pl.BlockSpec((1, XATTN_HEAD_DIM), const2),
                  pl.BlockSpec((1, XATTN_WIDTH), const2)],
        out_specs=pl.BlockSpec((tq, XATTN_WIDTH), lambda b, i: (b * nq + i, 0)),
        out_shape=jax.ShapeDtypeStruct((batch * seq, XATTN_WIDTH), BF16),
        compiler_params=_cparams("arbitrary", "arbitrary"),
        name="mem_xattn",
    )(qx, km, vm, qgain, ogain)


def _ssm_kernel(u_ref, bmat_ref, cmat_ref, are_ref, aim_ref, d_ref, wglu_ref, og_ref, out_ref,
                xr_ref, xi_ref, sr_ref, si_ref, *, batch, steps):
    @pl.when(pl.program_id(0) == 0)
    def _():
        sr_ref[...] = jnp.zeros_like(sr_ref)
        si_ref[...] = jnp.zeros_like(si_ref)

    u = u_ref[...]
    for j in range(SSM_TILES):
        bu = jnp.dot(u[:, j * LANES:(j + 1) * LANES], bmat_ref[j], preferred_element_type=F32)
        xr_ref[:, j * SSM_TILE_STATE:(j + 1) * SSM_TILE_STATE] = bu[:, :SSM_TILE_STATE]
        xi_ref[:, j * SSM_TILE_STATE:(j + 1) * SSM_TILE_STATE] = bu[:, SSM_TILE_STATE:]

    chunk = 1024
    for c0 in range(0, SSM_STATES, chunk):
        cs = slice(c0, c0 + chunk)
        ar = jnp.broadcast_to(are_ref[:, cs], (batch, chunk))
        ai = jnp.broadcast_to(aim_ref[:, cs], (batch, chunk))

        def step(t, carry, cs=cs, ar=ar, ai=ai):
            s_r, s_i = carry
            rows = pl.ds(pl.multiple_of(t * batch, batch), batch)
            n_r = ar * s_r - ai * s_i + xr_ref[rows, cs]
            n_i = ar * s_i + ai * s_r + xi_ref[rows, cs]
            xr_ref[rows, cs] = n_r
            xi_ref[rows, cs] = n_i
            return n_r, n_i

        s_r, s_i = lax.fori_loop(0, steps, step, (sr_ref[:, cs], si_ref[:, cs]))
        sr_ref[:, cs] = s_r
        si_ref[:, cs] = s_i

    ys = []
    for j in range(SSM_TILES):
        sl = slice(j * SSM_TILE_STATE, (j + 1) * SSM_TILE_STATE)
        xcat = jnp.concatenate([xr_ref[:, sl], xi_ref[:, sl]], axis=-1).astype(BF16)
        ys.append(jnp.dot(xcat, cmat_ref[j], preferred_element_type=F32))
    y = jnp.concatenate(ys, axis=-1) + d_ref[...] * u.astype(F32)
    y = jax.nn.gelu(y)
    y = y * jax.nn.sigmoid(jnp.dot(y.astype(BF16), wglu_ref[...], preferred_element_type=F32))
    out_ref[...] = _rms(y, og_ref[...]).astype(out_ref.dtype)


def _ssm(u_tb, bmat, cmat, a_re, a_im, d_skip, w_glu, ogain, batch, seq, steps):
    rows = steps * batch
    const2 = lambda c: (0, 0)
    const3 = lambda c: (0, 0, 0)
    return pl.pallas_call(
        functools.partial(_ssm_kernel, batch=batch, steps=steps),
        grid=(seq // steps,),
        in_specs=[pl.BlockSpec((rows, SSM_WIDTH), lambda c: (c, 0)),
                  pl.BlockSpec(bmat.shape, const3), pl.BlockSpec(cmat.shape, const3),
                  pl.BlockSpec((1, SSM_STATES), const2), pl.BlockSpec((1, SSM_STATES), const2),
                  pl.BlockSpec((1, SSM_WIDTH), const2),
                  pl.BlockSpec((SSM_WIDTH, SSM_WIDTH), const2),
                  pl.BlockSpec((1, SSM_WIDTH), const2)],
        out_specs=pl.BlockSpec((rows, SSM_WIDTH), lambda c: (c, 0)),
        out_shape=jax.ShapeDtypeStruct((seq * batch, SSM_WIDTH), BF16),
        scratch_shapes=[pltpu.VMEM((rows, SSM_STATES), F32), pltpu.VMEM((rows, SSM_STATES), F32),
                        pltpu.VMEM((batch, SSM_STATES), F32), pltpu.VMEM((batch, SSM_STATES), F32)],
        compiler_params=_cparams("arbitrary"),
        name="s5_layer",
    )(u_tb, bmat, cmat, a_re, a_im, d_skip, w_glu, ogain)


def _outproj_kernel(ya_ref, ys_ref, yx_ref, x_ref, wo_ref, g_ref, wrh_ref, wrl_ref, x1_ref, lt_ref, hp_ref):
    a0, a1 = ATTN_WIDTH, ATTN_WIDTH + SSM_WIDTH
    acc = jnp.dot(ya_ref[...], wo_ref[:a0, :], preferred_element_type=F32)
    acc = acc + jnp.dot(ys_ref[...], wo_ref[a0:a1, :], preferred_element_type=F32)
    acc = acc + jnp.dot(yx_ref[...], wo_ref[a1:, :], preferred_element_type=F32)
    x1 = x_ref[...] + acc
    x1_ref[...] = x1
    h2 = _rms(x1, g_ref[...])
    hi = h2.astype(BF16)
    lo = (h2 - hi.astype(F32)).astype(BF16)
    lt = lax.dot_general(wrh_ref[...], hi, _NT, preferred_element_type=F32)
    lt = lt + lax.dot_general(wrl_ref[...], hi, _NT, preferred_element_type=F32)
    lt = lt + lax.dot_general(wrh_ref[...], lo, _NT, preferred_element_type=F32)
    lt_ref[...] = lt
    bits = lax.bitcast_convert_type(hi.astype(F32), jnp.uint32)
    half = bits.shape[1] // 2
    hp_ref[...] = (bits[:, half:] & jnp.uint32(0xFFFF0000)) | (bits[:, :half] >> 16)


def _outproj(ya, ys_sb, yx, x2, w_o, gain, wr_hi, wr_lo, batch, seq, tm):
    t, d = x2.shape
    nsb = seq // tm
    row = lambda w: pl.BlockSpec((tm, w), lambda i: (i, 0))
    const = lambda shape: pl.BlockSpec(shape, lambda i: (0, 0))
    return pl.pallas_call(
        _outproj_kernel,
        grid=(t // tm,),
        in_specs=[row(ATTN_WIDTH),
                  pl.BlockSpec((tm, SSM_WIDTH), lambda i: (i % nsb, i // nsb)),
                  row(XATTN_WIDTH), row(d),
                  const(w_o.shape), const((1, d)), const(wr_hi.shape), const(wr_lo.shape)],
        out_specs=[row(d), pl.BlockSpec((LANES, tm), lambda i: (0, i)), row(d // 2)],
        out_shape=[jax.ShapeDtypeStruct((t, d), F32), jax.ShapeDtypeStruct((LANES, t), F32),
                   jax.ShapeDtypeStruct((t, d // 2), jnp.uint32)],
        compiler_params=_cparams("arbitrary"),
        name="out_proj_router",
    )(ya, ys_sb, yx, x2, w_o, gain, wr_hi, wr_lo)


def _route_kernel(lt_ref, tri_ref, dest_ref, w_ref, meta_ref, cnt_ref, carry_ref, pstart_ref):
    phase = pl.program_id(0)
    c = pl.program_id(1)
    logits = lt_ref[...]
    tc = logits.shape[1]
    ng, epg = N_EXPERT_GROUPS, EXPERTS_PER_GROUP
    row8 = lax.broadcasted_iota(jnp.int32, (ng, tc), 0)

    gl = logits[0:ng]
    gmax = jnp.max(gl, axis=0, keepdims=True)
    gidx = jnp.min(jnp.where(gl == gmax, row8, ng), axis=0, keepdims=True)
    gate = 1.0 / jnp.sum(jnp.exp(gl - gmax), axis=0, keepdims=True)
    sel = jnp.zeros((epg, tc), F32)
    for g in range(ng):
        sel = jnp.where(gidx == g, logits[ng + g * epg:ng + (g + 1) * epg], sel)
    v1 = jnp.max(sel, axis=0, keepdims=True)
    i1 = jnp.min(jnp.where(sel == v1, row8, epg), axis=0, keepdims=True)
    sel2 = jnp.where(row8 == i1, -jnp.inf, sel)
    v2 = jnp.max(sel2, axis=0, keepdims=True)
    i2 = jnp.min(jnp.where(sel2 == v2, row8, epg), axis=0, keepdims=True)
    e = jnp.exp(v2 - v1)
    w1 = gate * (1.0 / (1.0 + e))
    w2 = gate * (e / (1.0 + e))
    e1 = gidx * epg + i1
    e2 = gidx * epg + i2
    rowe = lax.broadcasted_iota(jnp.int32, (N_EXPERTS, tc), 0)
    oh1 = rowe == e1
    oh2 = rowe == e2
    member = jnp.where(jnp.logical_or(oh1, oh2), 1.0, 0.0)
    chunk_cnt = jnp.sum(member, axis=1, keepdims=True)

    @pl.when(phase == 0)
    def _():
        @pl.when(c == 0)
        def _():
            cnt_ref[...] = jnp.zeros_like(cnt_ref)
        cnt_ref[...] += chunk_cnt

    @pl.when(phase == 1)
    def _():
        @pl.when(c == 0)
        def _():
            cnt = cnt_ref[...]
            nblk = jnp.floor((cnt + (MOE_BLOCK - 1)) * (1.0 / MOE_BLOCK))
            nchunk = jnp.floor((nblk + (CHUNK_BLOCKS - 1)) * (1.0 / CHUNK_BLOCKS))
            r = lax.broadcasted_iota(jnp.int32, (N_EXPERTS, LANES), 0)
            cidx = lax.broadcasted_iota(jnp.int32, (N_EXPERTS, LANES), 1)
            to_row = lambda col: jnp.sum(jnp.where(r == cidx, col, 0.0), axis=0, keepdims=True)
            cumsum_col = lambda col: jnp.sum(jnp.where(cidx <= r, to_row(col), 0.0), axis=1, keepdims=True)
            cumsum_row = lambda col: jnp.sum(jnp.where(r <= cidx, col, 0.0), axis=0, keepdims=True)
            bend = cumsum_col(nblk)
            bstart = bend - nblk
            cend = cumsum_col(nchunk)
            cstart = cend - nchunk
            pstart_ref[...] = bstart * MOE_BLOCK
            carry_ref[...] = jnp.zeros_like(carry_ref)
            lanef = lax.broadcasted_iota(jnp.int32, (1, LANES), 1).astype(F32)
            owner = jnp.minimum(jnp.sum(jnp.where(cend <= lanef, 1.0, 0.0), axis=0, keepdims=True),
                                N_EXPERTS - 1.0)
            own = r.astype(F32) == owner
            pick = lambda col: jnp.sum(jnp.where(own, col, 0.0), axis=0, keepdims=True)
            idx = lanef - pick(cstart)
            first = pick(bstart) + CHUNK_BLOCKS * idx
            size = jnp.clip(pick(nblk) - CHUNK_BLOCKS * idx, 0.0, float(CHUNK_BLOCKS))
            zero = jnp.zeros((1, LANES), F32)
            rows = [owner, first, size,
                    zero + jnp.sum(nchunk, axis=0, keepdims=True),
                    cumsum_row(nblk) - 1.0,
                    to_row(nblk),
                    zero + jnp.sum(nblk, axis=0, keepdims=True),
                    zero]
            for k, v in enumerate(rows):
                meta_ref[k:k + 1, :] = v.astype(jnp.int32)

        before = carry_ref[...] + jnp.dot(member.astype(BF16), tri_ref[...], preferred_element_type=F32)
        pos = before + pstart_ref[...]
        dest_ref[0:1, :] = jnp.sum(jnp.where(oh1, pos, 0.0), axis=0, keepdims=True).astype(jnp.int32)
        dest_ref[1:2, :] = jnp.sum(jnp.where(oh2, pos, 0.0), axis=0, keepdims=True).astype(jnp.int32)
        w_ref[0:1, :] = w1
        w_ref[1:2, :] = w2
        carry_ref[...] += chunk_cnt


def _route(logits_t, tc):
    t = logits_t.shape[1]
    nc = t // tc
    tri = jnp.asarray(np.triu(np.ones((tc, tc), np.float32), k=1), dtype=BF16)
    return pl.pallas_call(
        _route_kernel,
        grid=(2, nc),
        in_specs=[pl.BlockSpec((LANES, tc), lambda p, c: (0, c)),
                  pl.BlockSpec((tc, tc), lambda p, c: (0, 0))],
        out_specs=[pl.BlockSpec((2, tc), lambda p, c: (0, c * p)),
                   pl.BlockSpec((2, tc), lambda p, c: (0, c * p)),
                   pl.BlockSpec((META_ROWS, LANES), lambda p, c: (0, 0))],
        out_shape=[jax.ShapeDtypeStruct((2, t), jnp.int32), jax.ShapeDtypeStruct((2, t), F32),
                   jax.ShapeDtypeStruct((META_ROWS, LANES), jnp.int32)],
        scratch_shapes=[pltpu.VMEM((N_EXPERTS, 1), F32)] * 3,
        compiler_params=_cparams("arbitrary", "arbitrary"),
        name="moe_route",
    )(logits_t, tri)


def _meta(meta_ref, row, lane=0):
    return meta_ref[row * LANES + lane]


def _fill_blocks(meta_ref, zbuf, dst_ref, sem, n_blocks, *, expert_tails):
    zbuf[...] = jnp.zeros_like(zbuf)
    n_used = _meta(meta_ref, M_NUSED)
    block = lambda b: pltpu.make_async_copy(zbuf, dst_ref.at[pl.ds(b * MOE_BLOCK, MOE_BLOCK), :], sem)

    def tails(fn):
        def body(e, carry):
            @pl.when(_meta(meta_ref, M_NBLK, e) > 0)
            def _():
                fn(block(_meta(meta_ref, M_LAST, e)))
            return carry
        lax.fori_loop(0, N_EXPERTS, body, 0)

    def unused(fn):
        def body(b, carry):
            fn(block(b))
            return carry
        lax.fori_loop(n_used, n_blocks, body, 0)

    for phase in (lambda cp: cp.start(), lambda cp: cp.wait()):
        if expert_tails:
            tails(phase)
        unused(phase)


def _dispatch_kernel(dest_ref, meta_ref, h_ref, xs_ref, zbuf, sem_z, sem, *, tokens, n_blocks):
    tm = h_ref.shape[0]

    @pl.when(pl.program_id(0) == 0)
    def _():
        _fill_blocks(meta_ref, zbuf, xs_ref, sem_z, n_blocks, expert_tails=True)

    base = pl.program_id(0) * tm

    def issue(r, carry):
        for k in range(2):
            d = dest_ref[k * tokens + base + r]
            pltpu.make_async_copy(h_ref.at[pl.ds(r, 1), :], xs_ref.at[pl.ds(d, 1), :], sem).start()
        return carry
    lax.fori_loop(0, tm, issue, 0)
    for k in range(2):
        pltpu.make_async_copy(h_ref, xs_ref.at[pl.ds(0, tm), :], sem).wait()


def _dispatch(dest_flat, meta_flat, h2p, n_blocks, tm):
    t, w = h2p.shape
    grid_spec = pltpu.PrefetchScalarGridSpec(
        num_scalar_prefetch=2,
        grid=(t // tm,),
        in_specs=[pl.BlockSpec((tm, w), lambda i, d, m: (i, 0))],
        out_specs=pl.BlockSpec(memory_space=pl.ANY),
        scratch_shapes=[pltpu.VMEM((MOE_BLOCK, w), h2p.dtype),
                        pltpu.SemaphoreType.DMA(()), pltpu.SemaphoreType.DMA(())],
    )
    return pl.pallas_call(
        functools.partial(_dispatch_kernel, tokens=t, n_blocks=n_blocks),
        grid_spec=grid_spec,
        out_shape=jax.ShapeDtypeStruct((n_blocks * MOE_BLOCK, w), h2p.dtype),
        compiler_params=_cparams("arbitrary"),
        name="moe_dispatch",
    )(dest_flat, meta_flat, h2p)


def _unpack_rows(words):
    lo = lax.bitcast_convert_type(words << 16, F32)
    hi = lax.bitcast_convert_type(words & jnp.uint32(0xFFFF0000), F32)
    return jnp.concatenate([lo, hi], axis=-1).astype(BF16)


def _expert_kernel(meta_ref, xs_ref, wg_ref, wu_ref, wd_ref, yb_ref,
                   xbuf, ybuf, zbuf, wg_bf, wu_bf, wd_bf, sem_in, sem_out, sem_z, *, n_blocks):
    c = pl.program_id(0)
    n_chunks = _meta(meta_ref, M_NCHUNK)
    slot = c % 2

    def in_copy(k, s, nb):
        rows = nb * MOE_BLOCK
        src = xs_ref.at[pl.ds(_meta(meta_ref, M_FIRST, k) * MOE_BLOCK, rows), :]
        return pltpu.make_async_copy(src, xbuf.at[s, pl.ds(0, rows), :], sem_in.at[s])

    def out_copy(k, s, nb):
        rows = nb * MOE_BLOCK
        dst = yb_ref.at[pl.ds(_meta(meta_ref, M_FIRST, k) * MOE_BLOCK, rows), :]
        return pltpu.make_async_copy(ybuf.at[s, pl.ds(0, rows), :], dst, sem_out.at[s])

    def by_size(k, fn):
        for nb in range(1, CHUNK_BLOCKS + 1):
            pl.when(_meta(meta_ref, M_SIZE, k) == nb)(functools.partial(fn, nb))

    @pl.when(c == 0)
    def _():
        by_size(0, lambda nb: in_copy(0, 0, nb).start())

    @pl.when(c + 1 < n_chunks)
    def _():
        by_size(c + 1, lambda nb: in_copy(c + 1, 1 - slot, nb).start())

    @pl.when(c < n_chunks)
    def _():
        prev = jnp.maximum(c - 1, 0)
        new_expert = jnp.logical_or(c == 0, _meta(meta_ref, M_OWNER, c) != _meta(meta_ref, M_OWNER, prev))

        @pl.when(new_expert)
        def _():
            wg_bf[...] = wg_ref[0].astype(BF16)
            wu_bf[...] = wu_ref[0].astype(BF16)
            wd_bf[...] = wd_ref[0].astype(BF16)

        @pl.when(c >= 2)
        def _():
            by_size(c - 2, lambda nb: out_copy(c - 2, slot, nb).wait())

        def compute(nb):
            rows = nb * MOE_BLOCK
            in_copy(c, slot, nb).wait()
            h = _unpack_rows(xbuf[slot, 0:rows, :])
            gate = jnp.dot(h, wg_bf[...], preferred_element_type=F32)
            up = jnp.dot(h, wu_bf[...], preferred_element_type=F32)
            act = (jax.nn.silu(gate) * up).astype(BF16)
            ybuf[slot, 0:rows, :] = jnp.dot(act, wd_bf[...], preferred_element_type=F32)
            out_copy(c, slot, nb).start()
        by_size(c, compute)

    @pl.when(c == pl.num_programs(0) - 1)
    def _():
        _fill_blocks(meta_ref, zbuf, yb_ref, sem_z, n_blocks, expert_tails=False)
        for back in (2, 1):
            @pl.when(n_chunks >= back)
            def _(back=back):
                k = n_chunks - back
                by_size(k, lambda nb: out_copy(k, k % 2, nb).wait())


def _experts(meta_flat, xs, w_gate, w_up, w_down, n_blocks, n_chunks_max):
    d, de = w_gate.shape[1], w_gate.shape[2]
    rows = CHUNK_BLOCKS * MOE_BLOCK

    def owner(c, m):
        return m[M_OWNER * LANES + jnp.minimum(c, m[M_NCHUNK * LANES] - 1)]

    wspec = lambda shape: pl.BlockSpec(shape, lambda c, m: (owner(c, m), 0, 0))
    grid_spec = pltpu.PrefetchScalarGridSpec(
        num_scalar_prefetch=1,
        grid=(n_chunks_max,),
        in_specs=[pl.BlockSpec(memory_space=pl.ANY),
                  wspec((1, d, de)), wspec((1, d, de)), wspec((1, de, d))],
        out_specs=pl.BlockSpec(memory_space=pl.ANY),
        scratch_shapes=[pltpu.VMEM((2, rows, xs.shape[1]), xs.dtype),
                        pltpu.VMEM((2, rows, d), F32),
                        pltpu.VMEM((MOE_BLOCK, d), F32),
                        pltpu.VMEM((d, de), BF16), pltpu.VMEM((d, de), BF16), pltpu.VMEM((de, d), BF16),
                        pltpu.SemaphoreType.DMA((2,)), pltpu.SemaphoreType.DMA((2,)),
                        pltpu.SemaphoreType.DMA(())],
    )
    return pl.pallas_call(
        functools.partial(_expert_kernel, n_blocks=n_blocks),
        grid_spec=grid_spec,
        out_shape=jax.ShapeDtypeStruct((n_blocks * MOE_BLOCK, d), F32),
        compiler_params=_cparams("arbitrary"),
        name="moe_experts",
    )(meta_flat, xs, w_gate, w_up, w_down)


def _combine_kernel(dest_ref, yb_ref, x1_ref, w_ref, out_ref, gbuf, sem, *, tokens):
    tm = x1_ref.shape[0]
    base = pl.program_id(0) * tm

    def issue(r, carry):
        for k in range(2):
            d = dest_ref[k * tokens + base + r]
            pltpu.make_async_copy(yb_ref.at[pl.ds(d, 1), :], gbuf.at[k, pl.ds(r, 1), :], sem).start()
        return carry
    lax.fori_loop(0, tm, issue, 0)
    for k in range(2):
        pltpu.make_async_copy(yb_ref.at[pl.ds(0, tm), :], gbuf.at[k], sem).wait()
    w = w_ref[...]
    out_ref[...] = x1_ref[...] + (gbuf[0] * w[:, 0:1] + gbuf[1] * w[:, 1:2])


def _combine(dest_flat, yb, x1, w_tok, tm):
    t, d = x1.shape
    grid_spec = pltpu.PrefetchScalarGridSpec(
        num_scalar_prefetch=1,
        grid=(t // tm,),
        in_specs=[pl.BlockSpec(memory_space=pl.ANY),
                  pl.BlockSpec((tm, d), lambda i, dr: (i, 0)),
                  pl.BlockSpec((tm, 2), lambda i, dr: (i, 0))],
        out_specs=pl.BlockSpec((tm, d), lambda i, dr: (i, 0)),
        scratch_shapes=[pltpu.VMEM((2, tm, d), F32), pltpu.SemaphoreType.DMA(())],
    )
    return pl.pallas_call(
        functools.partial(_combine_kernel, tokens=t),
        grid_spec=grid_spec,
        out_shape=jax.ShapeDtypeStruct((t, d), F32),
        compiler_params=_cparams("arbitrary"),
        name="moe_combine",
    )(dest_flat, yb, x1, w_tok)


def _row(v):
    return v.astype(F32).reshape(1, -1)


def _layer(x2, mem2, batch, seq, mem_len, p):
    t, d = x2.shape

    w_in = p["w_in"]
    a0 = ATTN_WIDTH
    kw = ATTN_KV_HEADS * ATTN_HEAD_DIM
    dup = lambda w: jnp.concatenate(
        [w[:, h * ATTN_HEAD_DIM:(h + 1) * ATTN_HEAD_DIM] for h in range(ATTN_KV_HEADS) for _ in range(2)], axis=1)
    w_cols = jnp.concatenate([w_in[:, :a0], dup(w_in[:, a0:a0 + kw]), dup(w_in[:, a0 + kw:a0 + 2 * kw]),
                              w_in[:, a0 + 2 * kw:]], axis=1).astype(BF16)
    tm_in = min(256, seq)
    q, kk, vv, u_sb, qx = _inproj(x2, _row(p["norm_mix"]), w_cols, batch, seq, tm_in)

    bias = _bias_table(p["rel_bias"])
    qgain = jnp.tile(_row(p["q_norm"]), (1, 2)) * (1.0 / math.sqrt(ATTN_HEAD_DIM))
    kgain = jnp.tile(_row(p["k_norm"]), (1, 2))
    ya = _swa(q, kk, vv, bias, p["attn_sinks"].astype(F32), qgain, kgain, _row(p["out_norm_attn"]), batch, seq)

    km, vm = _memkv(mem2, _row(p["mem_norm"]), p["w_mem_kv"].astype(BF16), _row(p["xk_norm"]),
                    min(256, mem2.shape[0]))
    xq_gain = _row(p["xq_norm"]) * (1.0 / math.sqrt(XATTN_HEAD_DIM))
    yx = _xattn(qx, km, vm, xq_gain, _row(p["out_norm_xattn"]), batch, seq, mem_len, min(512, seq))

    a_re, a_im, bbr, bbi = _ssm_prep(p["ssm_lambda_re"], p["ssm_lambda_im"], p["ssm_log_dt"],
                                     p["ssm_b_re"], p["ssm_b_im"])
    bmat = jnp.concatenate([_block_diag_tiles(bbr), _block_diag_tiles(bbi)], axis=-1).astype(BF16)
    c_re_t = jnp.transpose(p["ssm_c_re"].astype(F32), (0, 2, 1))
    c_im_t = jnp.transpose(p["ssm_c_im"].astype(F32), (0, 2, 1))
    cmat = jnp.concatenate([_block_diag_tiles(c_re_t), _block_diag_tiles(-c_im_t)], axis=1).astype(BF16)
    steps = min(64, seq)
    ys_tb = _ssm(u_sb.reshape(seq * batch, SSM_WIDTH), bmat, cmat,
                 a_re.reshape(1, SSM_STATES), a_im.reshape(1, SSM_STATES), _row(p["ssm_d"]),
                 p["ssm_w_glu"].astype(BF16), _row(p["out_norm_ssm"]), batch, seq, steps)

    wr = jnp.concatenate([p["w_router_group"], p["w_router_expert"]], axis=1).astype(F32)
    wr = jnp.pad(wr, ((0, 0), (0, LANES - wr.shape[1]))).T
    wr_hi = wr.astype(BF16)
    wr_lo = (wr - wr_hi.astype(F32)).astype(BF16)
    tm_out = min(256, seq)
    x1, logits_t, h2p = _outproj(ya, ys_tb.reshape(seq, batch * SSM_WIDTH), yx, x2, p["w_o"].astype(BF16),
                            _row(p["norm_ffn"]), wr_hi, wr_lo, batch, seq, tm_out)

    dest, w_k, meta = _route(logits_t, min(512, t))
    n_blocks = (2 * t) // MOE_BLOCK + N_EXPERTS
    n_chunks_max = (n_blocks + (CHUNK_BLOCKS - 1) * N_EXPERTS) // CHUNK_BLOCKS
    dest_flat = dest.reshape(2 * t)
    meta_flat = meta.reshape(META_ROWS * LANES)
    xs = _dispatch(dest_flat, meta_flat, h2p, n_blocks, min(1024, t))
    yb = _experts(meta_flat, xs, p["w_gate"], p["w_up"], p["w_down"], n_blocks, n_chunks_max)
    return _combine(dest_flat, yb, x1, w_k.T, min(256, t))


def kernel(x, mem, norm_mix, w_in, q_norm, k_norm, attn_sinks, rel_bias, ssm_lambda_re, ssm_lambda_im, ssm_log_dt, ssm_b_re, ssm_b_im, ssm_c_re, ssm_c_im, ssm_d, ssm_w_glu, mem_norm, w_mem_kv, xq_norm, xk_norm, out_norm_attn, out_norm_ssm, out_norm_xattn, w_o, norm_ffn, w_router_group, w_router_expert, w_gate, w_up, w_down):
    batch, seq, d = x.shape
    mem_len = mem.shape[1]
    per_layer = dict(norm_mix=norm_mix, w_in=w_in, q_norm=q_norm, k_norm=k_norm, attn_sinks=attn_sinks,
                     ssm_lambda_re=ssm_lambda_re, ssm_lambda_im=ssm_lambda_im, ssm_log_dt=ssm_log_dt,
                     ssm_b_re=ssm_b_re, ssm_b_im=ssm_b_im, ssm_c_re=ssm_c_re, ssm_c_im=ssm_c_im,
                     ssm_d=ssm_d, ssm_w_glu=ssm_w_glu, mem_norm=mem_norm, w_mem_kv=w_mem_kv,
                     xq_norm=xq_norm, xk_norm=xk_norm, out_norm_attn=out_norm_attn,
                     out_norm_ssm=out_norm_ssm, out_norm_xattn=out_norm_xattn, w_o=w_o, norm_ffn=norm_ffn,
                     w_router_group=w_router_group, w_router_expert=w_router_expert,
                     w_gate=w_gate, w_up=w_up, w_down=w_down)
    x2 = x.astype(F32).reshape(batch * seq, d)
    mem2 = mem.astype(F32).reshape(batch * mem_len, d)
    for l in range(norm_mix.shape[0]):
        p = {k: v[l] for k, v in per_layer.items()}
        p["rel_bias"] = rel_bias
        x2 = _layer(x2, mem2, batch, seq, mem_len, p)
    return x2.reshape(batch, seq, d).astype(x.dtype)
```

```python
import functools
import math

import numpy as np
import jax
import jax.numpy as jnp
from jax import lax
from jax.experimental import pallas as pl
from jax.experimental.pallas import tpu as pltpu

F32 = jnp.float32
BF16 = jnp.bfloat16
EPS = 1e-6

ATTN_HEADS = 16
ATTN_KV_HEADS = 2
ATTN_HEAD_DIM = 64
ATTN_WIDTH = ATTN_HEADS * ATTN_HEAD_DIM
WINDOW = 128
BLOCK = 128
REL_BUCKETS = 32
REL_MAX_DIST = 128
SSM_GROUP_CH = 16
SSM_GROUPS = 32
SSM_STATE = 64
SSM_WIDTH = SSM_GROUPS * SSM_GROUP_CH
XATTN_HEADS = 4
XATTN_HEAD_DIM = 128
XATTN_WIDTH = XATTN_HEADS * XATTN_HEAD_DIM
N_EXPERT_GROUPS = 8
EXPERTS_PER_GROUP = 8
N_EXPERTS = N_EXPERT_GROUPS * EXPERTS_PER_GROUP
D_EXPERT = 512
MOE_BLOCK = 128

LANES = 128
SSM_GROUPS_PER_TILE = LANES // SSM_GROUP_CH
SSM_TILES = SSM_WIDTH // LANES
SSM_TILE_STATE = SSM_GROUPS_PER_TILE * SSM_STATE
SSM_STATES = SSM_GROUPS * SSM_STATE
CHUNK_BLOCKS = 4
META_ROWS = 8
M_OWNER, M_FIRST, M_SIZE, M_NCHUNK, M_LAST, M_NBLK, M_NUSED = range(7)
VMEM_LIMIT = 56 * 1024 * 1024

_NT = (((1,), (1,)), ((), ()))


def _cparams(*sem):
    return pltpu.CompilerParams(dimension_semantics=sem, vmem_limit_bytes=VMEM_LIMIT)


def _rms(x, gain):
    ms = jnp.mean(x * x, axis=-1, keepdims=True)
    return x * lax.rsqrt(ms + EPS) * gain


def _t5_bucket_table():
    qi = np.arange(BLOCK, dtype=np.int32)[:, None]
    ki = np.arange(2 * BLOCK, dtype=np.int32)[None, :]
    delta = BLOCK + qi - ki
    n = np.maximum(delta, 0)
    max_exact = REL_BUCKETS // 2
    nf = np.maximum(n, 1).astype(np.float32)
    large = max_exact + (np.log(nf / np.float32(max_exact)) / np.float32(math.log(REL_MAX_DIST / max_exact))
                         * np.float32(REL_BUCKETS - max_exact)).astype(np.int32)
    large = np.minimum(large, REL_BUCKETS - 1)
    return np.where(n < max_exact, n, large).astype(np.int32)


def _bias_kernel(rb_ref, bucket_ref, out_ref):
    h = pl.program_id(0)
    bucket = bucket_ref[...]
    acc = jnp.zeros(bucket.shape, F32)
    for b in range(REL_BUCKETS):
        acc = jnp.where(bucket == b, rb_ref[b, h], acc)
    qi = lax.broadcasted_iota(jnp.int32, bucket.shape, 0)
    ki = lax.broadcasted_iota(jnp.int32, bucket.shape, 1)
    delta = BLOCK + qi - ki
    valid = jnp.logical_and(delta >= 0, delta < WINDOW)
    out_ref[0] = jnp.where(valid, acc, jnp.float32(-1e30))


def _bias_table(rel_bias):
    bucket = jnp.asarray(_t5_bucket_table())
    return pl.pallas_call(
        _bias_kernel,
        grid=(ATTN_HEADS,),
        in_specs=[pl.BlockSpec(memory_space=pltpu.SMEM),
                  pl.BlockSpec((BLOCK, 2 * BLOCK), lambda h: (0, 0))],
        out_specs=pl.BlockSpec((1, BLOCK, 2 * BLOCK), lambda h: (h, 0, 0)),
        out_shape=jax.ShapeDtypeStruct((ATTN_HEADS, BLOCK, 2 * BLOCK), F32),
        compiler_params=_cparams("arbitrary"),
        name="t5_bias_table",
    )(rel_bias.astype(F32), bucket)


def _ssm_prep_kernel(lr_ref, li_ref, ldt_ref, br_ref, bi_ref, are_ref, aim_ref, bbr_ref, bbi_ref):
    lr = lr_ref[...]
    li = li_ref[...]
    dt = jnp.exp(ldt_ref[...])
    mag = jnp.exp(lr * dt)
    a_re = mag * jnp.cos(li * dt)
    a_im = mag * jnp.sin(li * dt)
    den = lr * lr + li * li
    nr = a_re - 1.0
    ni = a_im
    coef_re = (nr * lr + ni * li) / den
    coef_im = (ni * lr - nr * li) / den
    are_ref[...] = a_re
    aim_ref[...] = a_im
    br = br_ref[...]
    bi = bi_ref[...]
    bbr_ref[...] = coef_re * br - coef_im * bi
    bbi_ref[...] = coef_re * bi + coef_im * br


def _ssm_prep(lam_re, lam_im, log_dt, b_re, b_im):
    g, n, c = b_re.shape
    vec = jax.ShapeDtypeStruct((g, 1, n), F32)
    mat = jax.ShapeDtypeStruct((g, c, n), F32)
    return pl.pallas_call(
        _ssm_prep_kernel,
        out_shape=(vec, vec, mat, mat),
        name="ssm_discretise",
    )(lam_re.astype(F32).reshape(g, 1, n), lam_im.astype(F32).reshape(g, 1, n),
      log_dt.astype(F32).reshape(g, 1, 1),
      jnp.transpose(b_re.astype(F32), (0, 2, 1)), jnp.transpose(b_im.astype(F32), (0, 2, 1)))


def _block_diag_tiles(m):
    g, r, c = m.shape
    t = g // SSM_GROUPS_PER_TILE
    eye = jnp.eye(SSM_GROUPS_PER_TILE, dtype=m.dtype)
    m4 = m.reshape(t, SSM_GROUPS_PER_TILE, r, c)
    out = m4[:, :, :, None, :] * eye[None, :, None, :, None]
    return out.reshape(t, SSM_GROUPS_PER_TILE * r, SSM_GROUPS_PER_TILE * c)


def _inproj_kernel(x_ref, g_ref, w_ref, q_ref, k_ref, v_ref, u_ref, qx_ref):
    h = _rms(x_ref[...], g_ref[...]).astype(BF16)
    c0 = 0
    for ref in (q_ref, k_ref, v_ref, u_ref, qx_ref):
        c1 = c0 + ref.shape[-1]
        ref[...] = jnp.dot(h, w_ref[:, c0:c1], preferred_element_type=F32).astype(ref.dtype)
        c0 = c1


def _inproj(x2, gain, w_cols, batch, seq, tm):
    t, d = x2.shape
    nsb = seq // tm
    kvw = 2 * LANES
    ncol = w_cols.shape[1]
    row = lambda w: pl.BlockSpec((tm, w), lambda i: (i, 0))
    return pl.pallas_call(
        _inproj_kernel,
        grid=(t // tm,),
        in_specs=[row(d),
                  pl.BlockSpec((1, d), lambda i: (0, 0)),
                  pl.BlockSpec((d, ncol), lambda i: (0, 0), pipeline_mode=pl.Buffered(1))],
        out_specs=[row(ATTN_WIDTH), row(kvw), row(kvw),
                   pl.BlockSpec((tm, SSM_WIDTH), lambda i: (i % nsb, i // nsb)),
                   row(XATTN_WIDTH)],
        out_shape=[jax.ShapeDtypeStruct((t, ATTN_WIDTH), BF16),
                   jax.ShapeDtypeStruct((t, kvw), BF16),
                   jax.ShapeDtypeStruct((t, kvw), BF16),
                   jax.ShapeDtypeStruct((seq, batch * SSM_WIDTH), BF16),
                   jax.ShapeDtypeStruct((t, XATTN_WIDTH), BF16)],
        compiler_params=_cparams("arbitrary"),
        name="in_proj",
    )(x2, gain, w_cols)


def _swa_kernel(sinks_ref, q_ref, kp_ref, kc_ref, vp_ref, vc_ref, bias_ref, qg_ref, kg_ref, og_ref, out_ref):
    n = pl.program_id(1)
    kcat = jnp.concatenate([kp_ref[...], kc_ref[...]], axis=0).astype(F32)
    vcat = jnp.concatenate([vp_ref[...], vc_ref[...]], axis=0)
    lane = lax.broadcasted_iota(jnp.int32, (1, LANES), 1)
    lo = lane < ATTN_HEAD_DIM
    key = lax.broadcasted_iota(jnp.int32, (1, 2 * BLOCK), 1)
    kmask = jnp.where(jnp.logical_and(n == 0, key < BLOCK), jnp.float32(-1e30), jnp.float32(0.0))
    qgain = qg_ref[...]
    pairs_per_kv = ATTN_HEADS // ATTN_KV_HEADS // 2
    outs = []
    for g in range(ATTN_KV_HEADS):
        kn = _rms(kcat[:, g * LANES:(g + 1) * LANES], kg_ref[...]).astype(BF16)
        vg = vcat[:, g * LANES:(g + 1) * LANES]
        for j in range(pairs_per_kv):
            tile = g * pairs_per_kv + j
            qp = q_ref[:, tile * LANES:(tile + 1) * LANES].astype(F32)
            sq = qp * qp
            s_lo = jnp.sum(jnp.where(lo, sq, 0.0), axis=-1, keepdims=True)
            s_hi = jnp.sum(sq, axis=-1, keepdims=True) - s_lo
            inv = jnp.where(lo, lax.rsqrt(s_lo / ATTN_HEAD_DIM + EPS), lax.rsqrt(s_hi / ATTN_HEAD_DIM + EPS))
            qn = qp * inv * qgain
            halves = []
            for half in range(2):
                hd = 2 * tile + half
                keep = lo if half == 0 else jnp.logical_not(lo)
                qh = jnp.where(keep, qn, 0.0).astype(BF16)
                l = lax.dot_general(qh, kn, _NT, preferred_element_type=F32)
                l = l + bias_ref[hd] + kmask
                sink = sinks_ref[hd]
                m = jnp.maximum(jnp.max(l, axis=-1, keepdims=True), sink)
                p = jnp.exp(l - m)
                den = jnp.sum(p, axis=-1, keepdims=True) + jnp.exp(sink - m)
                halves.append(jnp.dot(p.astype(BF16), vg, preferred_element_type=F32) / den)
            outs.append(jnp.where(lo, halves[0], halves[1]))
    y = jnp.concatenate(outs, axis=-1)
    out_ref[...] = _rms(y, og_ref[...]).astype(out_ref.dtype)


def _swa(q, kk, vv, bias, sinks, qgain, kgain, ogain, batch, seq):
    nb = seq // BLOCK
    kvw = 2 * LANES
    cur = lambda b, n: (b * nb + n, 0)
    prev = lambda b, n: (b * nb + jnp.maximum(n - 1, 0), 0)
    const2 = lambda b, n: (0, 0)
    return pl.pallas_call(
        _swa_kernel,
        grid=(batch, nb),
        in_specs=[pl.BlockSpec(memory_space=pltpu.SMEM),
                  pl.BlockSpec((BLOCK, ATTN_WIDTH), cur),
                  pl.BlockSpec((BLOCK, kvw), prev), pl.BlockSpec((BLOCK, kvw), cur),
                  pl.BlockSpec((BLOCK, kvw), prev), pl.BlockSpec((BLOCK, kvw), cur),
                  pl.BlockSpec((ATTN_HEADS, BLOCK, 2 * BLOCK), lambda b, n: (0, 0, 0)),
                  pl.BlockSpec((1, LANES), const2), pl.BlockSpec((1, LANES), const2),
                  pl.BlockSpec((1, ATTN_WIDTH), const2)],
        out_specs=pl.BlockSpec((BLOCK, ATTN_WIDTH), cur),
        out_shape=jax.ShapeDtypeStruct((batch * seq, ATTN_WIDTH), BF16),
        compiler_params=_cparams("arbitrary", "arbitrary"),
        name="swa_attention",
    )(sinks, q, kk, kk, vv, vv, bias, qgain, kgain, ogain)


def _memkv_kernel(m_ref, g_ref, w_ref, kg_ref, k_ref, v_ref):
    h = _rms(m_ref[...], g_ref[...]).astype(BF16)
    km = jnp.dot(h, w_ref[:, :XATTN_WIDTH], preferred_element_type=F32)
    for hd in range(XATTN_HEADS):
        sl = slice(hd * XATTN_HEAD_DIM, (hd + 1) * XATTN_HEAD_DIM)
        k_ref[:, sl] = _rms(km[:, sl], kg_ref[...]).astype(k_ref.dtype)
    v_ref[...] = jnp.dot(h, w_ref[:, XATTN_WIDTH:], preferred_element_type=F32).astype(v_ref.dtype)


def _memkv(mem2, gain, w_kv, kgain, tm):
    r, d = mem2.shape
    row = lambda w: pl.BlockSpec((tm, w), lambda i: (i, 0))
    const = lambda shape: pl.BlockSpec(shape, lambda i: (0, 0))
    return pl.pallas_call(
        _memkv_kernel,
        grid=(r // tm,),
        in_specs=[row(d), const((1, d)), const((d, 2 * XATTN_WIDTH)), const((1, XATTN_HEAD_DIM))],
        out_specs=[row(XATTN_WIDTH), row(XATTN_WIDTH)],
        out_shape=[jax.ShapeDtypeStruct((r, XATTN_WIDTH), BF16)] * 2,
        compiler_params=_cparams("arbitrary"),
        name="mem_kv_proj",
    )(mem2, gain, w_kv, kgain)


def _xattn_kernel(q_ref, k_ref, v_ref, qg_ref, og_ref, out_ref):
    outs = []
    for hd in range(XATTN_HEADS):
        sl = slice(hd * XATTN_HEAD_DIM, (hd + 1) * XATTN_HEAD_DIM)
        qn = _rms(q_ref[:, sl].astype(F32), qg_ref[...]).astype(BF16)
        l = lax.dot_general(qn, k_ref[:, sl], _NT, preferred_element_type=F32)
        m = jnp.max(l, axis=-1, keepdims=True)
        p = jnp.exp(l - m)
        den = jnp.sum(p, axis=-1, keepdims=True)
        outs.append(jnp.dot(p.astype(BF16), v_ref[:, sl], preferred_element_type=F32) / den)
    y = jnp.concatenate(outs, axis=-1)
    out_ref[...] = _rms(y, og_ref[...]).astype(out_ref.dtype)


def _xattn(qx, km, vm, qgain, ogain, batch, seq, mem_len, tq):
    nq = seq // tq
    const2 = lambda b, i: (0, 0)
    return pl.pallas_call(
        _xattn_kernel,
        grid=(batch, nq),
        in_specs=[pl.BlockSpec((tq, XATTN_WIDTH), lambda b, i: (b * nq + i, 0)),
                  pl.BlockSpec((mem_len, XATTN_WIDTH), lambda b, i: (b, 0)),
                  pl.BlockSpec((mem_len, XATTN_WIDTH), lambda b, i: (b, 0)),
                  pl.BlockSpec((1, XATTN_HEAD_DIM), const2),
                  pl.BlockSpec((1, XATTN_WIDTH), const2)],
        out_specs=pl.BlockSpec((tq, XATTN_WIDTH), lambda b, i: (b * nq + i, 0)),
        out_shape=jax.ShapeDtypeStruct((batch * seq, XATTN_WIDTH), BF16),
        compiler_params=_cparams("arbitrary", "arbitrary"),
        name="mem_xattn",
    )(qx, km, vm, qgain, ogain)


def _ssm_kernel(u_ref, bmat_ref, cmat_ref, are_ref, aim_ref, d_ref, wglu_ref, og_ref, out_ref,
                tb_ref, xr_ref, xi_ref, sr_ref, si_ref, *, batch, steps):
    @pl.when(pl.program_id(0) == 0)
    def _():
        sr_ref[...] = jnp.zeros_like(sr_ref)
        si_ref[...] = jnp.zeros_like(si_ref)

    for b in range(batch):
        for j in range(SSM_TILES):
            c0 = b * SSM_WIDTH + j * LANES
            tb_ref[j, pl.ds(b, steps, stride=batch), :] = u_ref[:, c0:c0 + LANES].astype(F32)
    uf = jnp.concatenate([tb_ref[j] for j in range(SSM_TILES)], axis=-1)
    u = uf.astype(BF16)
    for j in range(SSM_TILES):
        bu = jnp.dot(u[:, j * LANES:(j + 1) * LANES], bmat_ref[j], preferred_element_type=F32)
        xr_ref[:, j * SSM_TILE_STATE:(j + 1) * SSM_TILE_STATE] = bu[:, :SSM_TILE_STATE]
        xi_ref[:, j * SSM_TILE_STATE:(j + 1) * SSM_TILE_STATE] = bu[:, SSM_TILE_STATE:]

    chunk = 1024
    for c0 in range(0, SSM_STATES, chunk):
        cs = slice(c0, c0 + chunk)
        ar = jnp.broadcast_to(are_ref[:, cs], (batch, chunk))
        ai = jnp.broadcast_to(aim_ref[:, cs], (batch, chunk))

        def step(t, carry, cs=cs, ar=ar, ai=ai):
            s_r, s_i = carry
            rows = pl.ds(pl.multiple_of(t * batch, batch), batch)
            n_r = ar * s_r - ai * s_i + xr_ref[rows, cs]
            n_i = ar * s_i + ai * s_r + xi_ref[rows, cs]
            xr_ref[rows, cs] = n_r
            xi_ref[rows, cs] = n_i
            return n_r, n_i

        s_r, s_i = lax.fori_loop(0, steps, step, (sr_ref[:, cs], si_ref[:, cs]))
        sr_ref[:, cs] = s_r
        si_ref[:, cs] = s_i

    ys = []
    for j in range(SSM_TILES):
        sl = slice(j * SSM_TILE_STATE, (j + 1) * SSM_TILE_STATE)
        xcat = jnp.concatenate([xr_ref[:, sl], xi_ref[:, sl]], axis=-1).astype(BF16)
        ys.append(jnp.dot(xcat, cmat_ref[j], preferred_element_type=F32))
    y = jnp.concatenate(ys, axis=-1) + d_ref[...] * uf
    y = jax.nn.gelu(y)
    y = y * jax.nn.sigmoid(jnp.dot(y.astype(BF16), wglu_ref[...], preferred_element_type=F32))
    y = _rms(y, og_ref[...])
    for j in range(SSM_TILES):
        tb_ref[j] = y[:, j * LANES:(j + 1) * LANES]
    for b in range(batch):
        for j in range(SSM_TILES):
            c0 = b * SSM_WIDTH + j * LANES
            out_ref[:, c0:c0 + LANES] = tb_ref[j, pl.ds(b, steps, stride=batch), :].astype(out_ref.dtype)


def _ssm(u_sb, bmat, cmat, a_re, a_im, d_skip, w_glu, ogain, batch, seq, steps):
    rows = steps * batch
    const2 = lambda c: (0, 0)
    const3 = lambda c: (0, 0, 0)
    return pl.pallas_call(
        functools.partial(_ssm_kernel, batch=batch, steps=steps),
        grid=(seq // steps,),
        in_specs=[pl.BlockSpec((steps, batch * SSM_WIDTH), lambda c: (c, 0)),
                  pl.BlockSpec(bmat.shape, const3), pl.BlockSpec(cmat.shape, const3),
                  pl.BlockSpec((1, SSM_STATES), const2), pl.BlockSpec((1, SSM_STATES), const2),
                  pl.BlockSpec((1, SSM_WIDTH), const2),
                  pl.BlockSpec((SSM_WIDTH, SSM_WIDTH), const2),
                  pl.BlockSpec((1, SSM_WIDTH), const2)],
        out_specs=pl.BlockSpec((steps, batch * SSM_WIDTH), lambda c: (c, 0)),
        out_shape=jax.ShapeDtypeStruct((seq, batch * SSM_WIDTH), BF16),
        scratch_shapes=[pltpu.VMEM((SSM_TILES, rows, LANES), F32),
                        pltpu.VMEM((rows, SSM_STATES), F32), pltpu.VMEM((rows, SSM_STATES), F32),
                        pltpu.VMEM((batch, SSM_STATES), F32), pltpu.VMEM((batch, SSM_STATES), F32)],
        compiler_params=_cparams("arbitrary"),
        name="s5_layer",
    )(u_sb, bmat, cmat, a_re, a_im, d_skip, w_glu, ogain)


def _outproj_kernel(ya_ref, ys_ref, yx_ref, x_ref, wo_ref, g_ref, wr_ref, x1_ref, lt_ref, hp_ref, *, sub):
    for r0 in range(0, x_ref.shape[0], sub):
        rows = slice(r0, r0 + sub)
        mix = jnp.concatenate([ya_ref[rows, :], ys_ref[rows, :], yx_ref[rows, :]], axis=-1)
        x1 = x_ref[rows, :] + jnp.dot(mix, wo_ref[...], preferred_element_type=F32)
        x1_ref[rows, :] = x1
        h2 = _rms(x1, g_ref[...])
        hi = h2.astype(BF16)
        hi32 = hi.astype(F32)
        lo = (h2 - hi32).astype(BF16)
        both = jnp.dot(hi, wr_ref[...], preferred_element_type=F32)
        lg = both[:, :LANES] + both[:, LANES:] + jnp.dot(lo, wr_ref[:, :LANES], preferred_element_type=F32)
        lt_ref[rows, :] = lg
        bits = lax.bitcast_convert_type(hi32, jnp.uint32)
        half = bits.shape[1] // 2
        hp_ref[rows, :] = (bits[:, half:] & jnp.uint32(0xFFFF0000)) | (bits[:, :half] >> 16)


def _outproj(ya, ys_sb, yx, x2, w_o, gain, wr, batch, seq, tm):
    t, d = x2.shape
    nsb = seq // tm
    row = lambda w: pl.BlockSpec((tm, w), lambda i: (i, 0))
    const = lambda shape: pl.BlockSpec(shape, lambda i: (0, 0), pipeline_mode=pl.Buffered(1))
    return pl.pallas_call(
        functools.partial(_outproj_kernel, sub=min(256, tm)),
        grid=(t // tm,),
        in_specs=[row(ATTN_WIDTH),
                  pl.BlockSpec((tm, SSM_WIDTH), lambda i: (i % nsb, i // nsb)),
                  row(XATTN_WIDTH), row(d),
                  const(w_o.shape), const((1, d)), const(wr.shape)],
        out_specs=[row(d), row(LANES), row(d // 2)],
        out_shape=[jax.ShapeDtypeStruct((t, d), F32), jax.ShapeDtypeStruct((t, LANES), F32),
                   jax.ShapeDtypeStruct((t, d // 2), jnp.uint32)],
        compiler_params=_cparams("arbitrary"),
        name="out_proj_router",
    )(ya, ys_sb, yx, x2, w_o, gain, wr)


def _route_kernel(lt_ref, tri_ref, dest_ref, w_ref, meta_ref, cnt_ref, carry_ref, pstart_ref):
    phase = pl.program_id(0)
    c = pl.program_id(1)
    logits = lt_ref[...].T
    tc = logits.shape[1]
    ng, epg = N_EXPERT_GROUPS, EXPERTS_PER_GROUP
    row8 = lax.broadcasted_iota(jnp.int32, (ng, tc), 0)

    gl = logits[0:ng]
    gmax = jnp.max(gl, axis=0, keepdims=True)
    gidx = jnp.min(jnp.where(gl == gmax, row8, ng), axis=0, keepdims=True)
    gate = 1.0 / jnp.sum(jnp.exp(gl - gmax), axis=0, keepdims=True)
    sel = jnp.zeros((epg, tc), F32)
    for g in range(ng):
        sel = jnp.where(gidx == g, logits[ng + g * epg:ng + (g + 1) * epg], sel)
    v1 = jnp.max(sel, axis=0, keepdims=True)
    i1 = jnp.min(jnp.where(sel == v1, row8, epg), axis=0, keepdims=True)
    sel2 = jnp.where(row8 == i1, -jnp.inf, sel)
    v2 = jnp.max(sel2, axis=0, keepdims=True)
    i2 = jnp.min(jnp.where(sel2 == v2, row8, epg), axis=0, keepdims=True)
    e = jnp.exp(v2 - v1)
    w1 = gate * (1.0 / (1.0 + e))
    w2 = gate * (e / (1.0 + e))
    e1 = gidx * epg + i1
    e2 = gidx * epg + i2
    rowe = lax.broadcasted_iota(jnp.int32, (N_EXPERTS, tc), 0)
    oh1 = rowe == e1
    oh2 = rowe == e2
    member = jnp.where(jnp.logical_or(oh1, oh2), 1.0, 0.0)
    chunk_cnt = jnp.sum(member, axis=1, keepdims=True)

    @pl.when(phase == 0)
    def _():
        @pl.when(c == 0)
        def _():
            cnt_ref[...] = jnp.zeros_like(cnt_ref)
        cnt_ref[...] += chunk_cnt

    @pl.when(phase == 1)
    def _():
        @pl.when(c == 0)
        def _():
            cnt = cnt_ref[...]
            nblk = jnp.floor((cnt + (MOE_BLOCK - 1)) * (1.0 / MOE_BLOCK))
            nchunk = jnp.floor((nblk + (CHUNK_BLOCKS - 1)) * (1.0 / CHUNK_BLOCKS))
            r = lax.broadcasted_iota(jnp.int32, (N_EXPERTS, LANES), 0)
            cidx = lax.broadcasted_iota(jnp.int32, (N_EXPERTS, LANES), 1)
            to_row = lambda col: jnp.sum(jnp.where(r == cidx, col, 0.0), axis=0, keepdims=True)
            cumsum_col = lambda col: jnp.sum(jnp.where(cidx <= r, to_row(col), 0.0), axis=1, keepdims=True)
            cumsum_row = lambda col: jnp.sum(jnp.where(r <= cidx, col, 0.0), axis=0, keepdims=True)
            bend = cumsum_col(nblk)
            bstart = bend - nblk
            cend = cumsum_col(nchunk)
            cstart = cend - nchunk
            pstart_ref[...] = bstart * MOE_BLOCK
            carry_ref[...] = jnp.zeros_like(carry_ref)
            lanef = lax.broadcasted_iota(jnp.int32, (1, LANES), 1).astype(F32)
            owner = jnp.minimum(jnp.sum(jnp.where(cend <= lanef, 1.0, 0.0), axis=0, keepdims=True),
                                N_EXPERTS - 1.0)
            own = r.astype(F32) == owner
            pick = lambda col: jnp.sum(jnp.where(own, col, 0.0), axis=0, keepdims=True)
            idx = lanef - pick(cstart)
            first = pick(bstart) + CHUNK_BLOCKS * idx
            size = jnp.clip(pick(nblk) - CHUNK_BLOCKS * idx, 0.0, float(CHUNK_BLOCKS))
            zero = jnp.zeros((1, LANES), F32)
            rows = [owner, first, size,
                    zero + jnp.sum(nchunk, axis=0, keepdims=True),
                    cumsum_row(nblk) - 1.0,
                    to_row(nblk),
                    zero + jnp.sum(nblk, axis=0, keepdims=True),
                    zero]
            for k, v in enumerate(rows):
                meta_ref[k:k + 1, :] = v.astype(jnp.int32)

        before = carry_ref[...] + jnp.dot(member.astype(BF16), tri_ref[...], preferred_element_type=F32)
        pos = before + pstart_ref[...]
        dest_ref[0:1, :] = jnp.sum(jnp.where(oh1, pos, 0.0), axis=0, keepdims=True).astype(jnp.int32)
        dest_ref[1:2, :] = jnp.sum(jnp.where(oh2, pos, 0.0), axis=0, keepdims=True).astype(jnp.int32)
        w_ref[0:1, :] = w1
        w_ref[1:2, :] = w2
        carry_ref[...] += chunk_cnt


def _route(logits_t, tc):
    t = logits_t.shape[0]
    nc = t // tc
    tri = jnp.asarray(np.triu(np.ones((tc, tc), np.float32), k=1), dtype=BF16)
    return pl.pallas_call(
        _route_kernel,
        grid=(2, nc),
        in_specs=[pl.BlockSpec((tc, LANES), lambda p, c: (c, 0)),
                  pl.BlockSpec((tc, tc), lambda p, c: (0, 0))],
        out_specs=[pl.BlockSpec((2, tc), lambda p, c: (0, c * p)),
                   pl.BlockSpec((2, tc), lambda p, c: (0, c * p)),
                   pl.BlockSpec((META_ROWS, LANES), lambda p, c: (0, 0))],
        out_shape=[jax.ShapeDtypeStruct((2, t), jnp.int32), jax.ShapeDtypeStruct((2, t), F32),
                   jax.ShapeDtypeStruct((META_ROWS, LANES), jnp.int32)],
        scratch_shapes=[pltpu.VMEM((N_EXPERTS, 1), F32)] * 3,
        compiler_params=_cparams("arbitrary", "arbitrary"),
        name="moe_route",
    )(logits_t, tri)


def _meta(meta_ref, row, lane=0):
    return meta_ref[row * LANES + lane]


def _fill_blocks(meta_ref, zbuf, dst_ref, sem, n_blocks, *, expert_tails):
    zbuf[...] = jnp.zeros_like(zbuf)
    n_used = _meta(meta_ref, M_NUSED)
    block = lambda b: pltpu.make_async_copy(zbuf, dst_ref.at[pl.ds(b * MOE_BLOCK, MOE_BLOCK), :], sem)

    def tails(fn):
        def body(e, carry):
            @pl.when(_meta(meta_ref, M_NBLK, e) > 0)
            def _():
                fn(block(_meta(meta_ref, M_LAST, e)))
            return carry
        lax.fori_loop(0, N_EXPERTS, body, 0)

    def unused(fn):
        def body(b, carry):
            fn(block(b))
            return carry
        lax.fori_loop(n_used, n_blocks, body, 0)

    for phase in (lambda cp: cp.start(), lambda cp: cp.wait()):
        if expert_tails:
            tails(phase)
        unused(phase)


def _dispatch_kernel(dest_ref, meta_ref, h_ref, xs_ref, zbuf, sem_z, sem, *, tokens, n_blocks):
    tm = h_ref.shape[0]

    @pl.when(pl.program_id(0) == 0)
    def _():
        _fill_blocks(meta_ref, zbuf, xs_ref, sem_z, n_blocks, expert_tails=True)

    base = pl.program_id(0) * tm

    def issue(r, carry):
        for k in range(2):
            d = dest_ref[k * tokens + base + r]
            pltpu.make_async_copy(h_ref.at[pl.ds(r, 1), :], xs_ref.at[pl.ds(d, 1), :], sem).start()
        return carry
    lax.fori_loop(0, tm, issue, 0, unroll=8)
    for k in range(2):
        pltpu.make_async_copy(h_ref, xs_ref.at[pl.ds(0, tm), :], sem).wait()


def _dispatch(dest_flat, meta_flat, h2p, n_blocks, tm):
    t, w = h2p.shape
    grid_spec = pltpu.PrefetchScalarGridSpec(
        num_scalar_prefetch=2,
        grid=(t // tm,),
        in_specs=[pl.BlockSpec((tm, w), lambda i, d, m: (i, 0))],
        out_specs=pl.BlockSpec(memory_space=pl.ANY),
        scratch_shapes=[pltpu.VMEM((MOE_BLOCK, w), h2p.dtype),
                        pltpu.SemaphoreType.DMA(()), pltpu.SemaphoreType.DMA(())],
    )
    return pl.pallas_call(
        functools.partial(_dispatch_kernel, tokens=t, n_blocks=n_blocks),
        grid_spec=grid_spec,
        out_shape=jax.ShapeDtypeStruct((n_blocks * MOE_BLOCK, w), h2p.dtype),
        compiler_params=_cparams("arbitrary"),
        name="moe_dispatch",
    )(dest_flat, meta_flat, h2p)


def _unpack_rows(words):
    lo = lax.bitcast_convert_type(words << 16, F32)
    hi = lax.bitcast_convert_type(words & jnp.uint32(0xFFFF0000), F32)
    return jnp.concatenate([lo, hi], axis=-1).astype(BF16)


def _expert_kernel(meta_ref, xs_ref, wg_ref, wu_ref, wd_ref, yb_ref,
                   xbuf, ybuf, zbuf, wg_bf, wu_bf, wd_bf, sem_in, sem_out, sem_z, *, n_blocks):
    c = pl.program_id(0)
    n_chunks = _meta(meta_ref, M_NCHUNK)
    slot = c % 2

    def in_copy(k, s, nb):
        rows = nb * MOE_BLOCK
        src = xs_ref.at[pl.ds(_meta(meta_ref, M_FIRST, k) * MOE_BLOCK, rows), :]
        return pltpu.make_async_copy(src, xbuf.at[s, pl.ds(0, rows), :], sem_in.at[s])

    def out_copy(k, s, nb):
        rows = nb * MOE_BLOCK
        dst = yb_ref.at[pl.ds(_meta(meta_ref, M_FIRST, k) * MOE_BLOCK, rows), :]
        return pltpu.make_async_copy(ybuf.at[s, pl.ds(0, rows), :], dst, sem_out.at[s])

    def by_size(k, fn):
        for nb in range(1, CHUNK_BLOCKS + 1):
            pl.when(_meta(meta_ref, M_SIZE, k) == nb)(functools.partial(fn, nb))

    @pl.when(c == 0)
    def _():
        by_size(0, lambda nb: in_copy(0, 0, nb).start())

    @pl.when(c + 1 < n_chunks)
    def _():
        by_size(c + 1, lambda nb: in_copy(c + 1, 1 - slot, nb).start())

    @pl.when(c < n_chunks)
    def _():
        prev = jnp.maximum(c - 1, 0)
        new_expert = jnp.logical_or(c == 0, _meta(meta_ref, M_OWNER, c) != _meta(meta_ref, M_OWNER, prev))

        @pl.when(new_expert)
        def _():
            wg_bf[...] = wg_ref[0].astype(BF16)
            wu_bf[...] = wu_ref[0].astype(BF16)
            wd_bf[...] = wd_ref[0].astype(BF16)

        @pl.when(c >= 2)
        def _():
            by_size(c - 2, lambda nb: out_copy(c - 2, slot, nb).wait())

        def compute(nb):
            rows = nb * MOE_BLOCK
            in_copy(c, slot, nb).wait()
            h = _unpack_rows(xbuf[slot, 0:rows, :])
            gate = jnp.dot(h, wg_bf[...], preferred_element_type=F32)
            up = jnp.dot(h, wu_bf[...], preferred_element_type=F32)
            act = (jax.nn.silu(gate) * up).astype(BF16)
            ybuf[slot, 0:rows, :] = jnp.dot(act, wd_bf[...], preferred_element_type=F32)
            out_copy(c, slot, nb).start()
        by_size(c, compute)

    @pl.when(c == pl.num_programs(0) - 1)
    def _():
        _fill_blocks(meta_ref, zbuf, yb_ref, sem_z, n_blocks, expert_tails=False)
        for back in (2, 1):
            @pl.when(n_chunks >= back)
            def _(back=back):
                k = n_chunks - back
                by_size(k, lambda nb: out_copy(k, k % 2, nb).wait())


def _experts(meta_flat, xs, w_gate, w_up, w_down, n_blocks, n_chunks_max):
    d, de = w_gate.shape[1], w_gate.shape[2]
    rows = CHUNK_BLOCKS * MOE_BLOCK

    def owner(c, m):
        return m[M_OWNER * LANES + jnp.minimum(c, m[M_NCHUNK * LANES] - 1)]

    wspec = lambda shape: pl.BlockSpec(shape, lambda c, m: (owner(c, m), 0, 0))
    grid_spec = pltpu.PrefetchScalarGridSpec(
        num_scalar_prefetch=1,
        grid=(n_chunks_max,),
        in_specs=[pl.BlockSpec(memory_space=pl.ANY),
                  wspec((1, d, de)), wspec((1, d, de)), wspec((1, de, d))],
        out_specs=pl.BlockSpec(memory_space=pl.ANY),
        scratch_shapes=[pltpu.VMEM((2, rows, xs.shape[1]), xs.dtype),
                        pltpu.VMEM((2, rows, d), F32),
                        pltpu.VMEM((MOE_BLOCK, d), F32),
                        pltpu.VMEM((d, de), BF16), pltpu.VMEM((d, de), BF16), pltpu.VMEM((de, d), BF16),
                        pltpu.SemaphoreType.DMA((2,)), pltpu.SemaphoreType.DMA((2,)),
                        pltpu.SemaphoreType.DMA(())],
    )
    return pl.pallas_call(
        functools.partial(_expert_kernel, n_blocks=n_blocks),
        grid_spec=grid_spec,
        out_shape=jax.ShapeDtypeStruct((n_blocks * MOE_BLOCK, d), F32),
        compiler_params=_cparams("arbitrary"),
        name="moe_experts",
    )(meta_flat, xs, w_gate, w_up, w_down)


def _combine_kernel(dest_ref, yb_ref, x1_ref, w_ref, out_ref, gbuf, sem, *, tokens):
    tm = x1_ref.shape[0]
    base = pl.program_id(0) * tm

    def issue(r, carry):
        for k in range(2):
            d = dest_ref[k * tokens + base + r]
            pltpu.make_async_copy(yb_ref.at[pl.ds(d, 1), :], gbuf.at[k, pl.ds(r, 1), :], sem).start()
        return carry
    lax.fori_loop(0, tm, issue, 0, unroll=8)
    for k in range(2):
        pltpu.make_async_copy(yb_ref.at[pl.ds(0, tm), :], gbuf.at[k], sem).wait()
    w = w_ref[...]
    out_ref[...] = x1_ref[...] + (gbuf[0] * w[:, 0:1] + gbuf[1] * w[:, 1:2])


def _combine(dest_flat, yb, x1, w_tok, tm):
    t, d = x1.shape
    grid_spec = pltpu.PrefetchScalarGridSpec(
        num_scalar_prefetch=1,
        grid=(t // tm,),
        in_specs=[pl.BlockSpec(memory_space=pl.ANY),
                  pl.BlockSpec((tm, d), lambda i, dr: (i, 0)),
                  pl.BlockSpec((tm, 2), lambda i, dr: (i, 0))],
        out_specs=pl.BlockSpec((tm, d), lambda i, dr: (i, 0)),
        scratch_shapes=[pltpu.VMEM((2, tm, d), F32), pltpu.SemaphoreType.DMA(())],
    )
    return pl.pallas_call(
        functools.partial(_combine_kernel, tokens=t),
        grid_spec=grid_spec,
        out_shape=jax.ShapeDtypeStruct((t, d), F32),
        compiler_params=_cparams("arbitrary"),
        name="moe_combine",
    )(dest_flat, yb, x1, w_tok)


def _row(v):
    return v.astype(F32).reshape(1, -1)


def _layer(x2, mem2, batch, seq, mem_len, p):
    t, d = x2.shape

    w_in = p["w_in"]
    a0 = ATTN_WIDTH
    kw = ATTN_KV_HEADS * ATTN_HEAD_DIM
    dup = lambda w: jnp.concatenate(
        [w[:, h * ATTN_HEAD_DIM:(h + 1) * ATTN_HEAD_DIM] for h in range(ATTN_KV_HEADS) for _ in range(2)], axis=1)
    w_cols = jnp.concatenate([w_in[:, :a0], dup(w_in[:, a0:a0 + kw]), dup(w_in[:, a0 + kw:a0 + 2 * kw]),
                              w_in[:, a0 + 2 * kw:]], axis=1).astype(BF16)
    tm_in = min(512, seq)
    q, kk, vv, u_sb, qx = _inproj(x2, _row(p["norm_mix"]), w_cols, batch, seq, tm_in)

    bias = _bias_table(p["rel_bias"])
    qgain = jnp.tile(_row(p["q_norm"]), (1, 2)) * (1.0 / math.sqrt(ATTN_HEAD_DIM))
    kgain = jnp.tile(_row(p["k_norm"]), (1, 2))
    ya = _swa(q, kk, vv, bias, p["attn_sinks"].astype(F32), qgain, kgain, _row(p["out_norm_attn"]), batch, seq)

    km, vm = _memkv(mem2, _row(p["mem_norm"]), p["w_mem_kv"].astype(BF16), _row(p["xk_norm"]),
                    min(256, mem2.shape[0]))
    xq_gain = _row(p["xq_norm"]) * (1.0 / math.sqrt(XATTN_HEAD_DIM))
    yx = _xattn(qx, km, vm, xq_gain, _row(p["out_norm_xattn"]), batch, seq, mem_len, min(512, seq))

    a_re, a_im, bbr, bbi = _ssm_prep(p["ssm_lambda_re"], p["ssm_lambda_im"], p["ssm_log_dt"],
                                     p["ssm_b_re"], p["ssm_b_im"])
    bmat = jnp.concatenate([_block_diag_tiles(bbr), _block_diag_tiles(bbi)], axis=-1).astype(BF16)
    c_re_t = jnp.transpose(p["ssm_c_re"].astype(F32), (0, 2, 1))
    c_im_t = jnp.transpose(p["ssm_c_im"].astype(F32), (0, 2, 1))
    cmat = jnp.concatenate([_block_diag_tiles(c_re_t), _block_diag_tiles(-c_im_t)], axis=1).astype(BF16)
    steps = min(64, seq)
    ys_sb = _ssm(u_sb, bmat, cmat,
                 a_re.reshape(1, SSM_STATES), a_im.reshape(1, SSM_STATES), _row(p["ssm_d"]),
                 p["ssm_w_glu"].astype(BF16), _row(p["out_norm_ssm"]), batch, seq, steps)

    wr = jnp.concatenate([p["w_router_group"], p["w_router_expert"]], axis=1).astype(F32)
    wr = jnp.pad(wr, ((0, 0), (0, LANES - wr.shape[1])))
    wr_hi = wr.astype(BF16)
    wr_lo = (wr - wr_hi.astype(F32)).astype(BF16)
    tm_out = min(512, seq)
    x1, logits_t, h2p = _outproj(ya, ys_sb, yx, x2, p["w_o"].astype(BF16), _row(p["norm_ffn"]),
                                 jnp.concatenate([wr_hi, wr_lo], axis=1), batch, seq, tm_out)

    dest, w_k, meta = _route(logits_t, min(512, t))
    n_blocks = (2 * t) // MOE_BLOCK + N_EXPERTS
    n_chunks_max = (n_blocks + (CHUNK_BLOCKS - 1) * N_EXPERTS) // CHUNK_BLOCKS
    dest_flat = dest.reshape(2 * t)
    meta_flat = meta.reshape(META_ROWS * LANES)
    xs = _dispatch(dest_flat, meta_flat, h2p, n_blocks, min(1024, t))
    yb = _experts(meta_flat, xs, p["w_gate"], p["w_up"], p["w_down"], n_blocks, n_chunks_max)
    return _combine(dest_flat, yb, x1, w_k.T, min(256, t))


def kernel(x, mem, norm_mix, w_in, q_norm, k_norm, attn_sinks, rel_bias, ssm_lambda_re, ssm_lambda_im, ssm_log_dt, ssm_b_re, ssm_b_im, ssm_c_re, ssm_c_im, ssm_d, ssm_w_glu, mem_norm, w_mem_kv, xq_norm, xk_norm, out_norm_attn, out_norm_ssm, out_norm_xattn, w_o, norm_ffn, w_router_group, w_router_expert, w_gate, w_up, w_down):
    batch, seq, d = x.shape
    mem_len = mem.shape[1]
    per_layer = dict(norm_mix=norm_mix, w_in=w_in, q_norm=q_norm, k_norm=k_norm, attn_sinks=attn_sinks,
                     ssm_lambda_re=ssm_lambda_re, ssm_lambda_im=ssm_lambda_im, ssm_log_dt=ssm_log_dt,
                     ssm_b_re=ssm_b_re, ssm_b_im=ssm_b_im, ssm_c_re=ssm_c_re, ssm_c_im=ssm_c_im,
                     ssm_d=ssm_d, ssm_w_glu=ssm_w_glu, mem_norm=mem_norm, w_mem_kv=w_mem_kv,
                     xq_norm=xq_norm, xk_norm=xk_norm, out_norm_attn=out_norm_attn,
                     out_norm_ssm=out_norm_ssm, out_norm_xattn=out_norm_xattn, w_o=w_o, norm_ffn=norm_ffn,
                     w_router_group=w_router_group, w_router_expert=w_router_expert,
                     w_gate=w_gate, w_up=w_up, w_down=w_down)
    x2 = x.astype(F32).reshape(batch * seq, d)
    mem2 = mem.astype(F32).reshape(batch * mem_len, d)
    for l in range(norm_mix.shape[0]):
        p = {k: v[l] for k, v in per_layer.items()}
        p["rel_bias"] = rel_bias
        x2 = _layer(x2, mem2, batch, seq, mem_len, p)
    return x2.reshape(batch, seq, d).astype(x.dtype)
```

```python
import functools
import math

import numpy as np
import jax
import jax.numpy as jnp
from jax import lax
from jax.experimental import pallas as pl
from jax.experimental.pallas import tpu as pltpu

F32 = jnp.float32
BF16 = jnp.bfloat16
EPS = 1e-6

ATTN_HEADS = 16
ATTN_KV_HEADS = 2
ATTN_HEAD_DIM = 64
ATTN_WIDTH = ATTN_HEADS * ATTN_HEAD_DIM
WINDOW = 128
BLOCK = 128
REL_BUCKETS = 32
REL_MAX_DIST = 128
SSM_GROUP_CH = 16
SSM_GROUPS = 32
SSM_STATE = 64
SSM_WIDTH = SSM_GROUPS * SSM_GROUP_CH
XATTN_HEADS = 4
XATTN_HEAD_DIM = 128
XATTN_WIDTH = XATTN_HEADS * XATTN_HEAD_DIM
N_EXPERT_GROUPS = 8
EXPERTS_PER_GROUP = 8
N_EXPERTS = N_EXPERT_GROUPS * EXPERTS_PER_GROUP
D_EXPERT = 512
MOE_BLOCK = 128

LANES = 128
SSM_GROUPS_PER_TILE = LANES // SSM_GROUP_CH
SSM_TILES = SSM_WIDTH // LANES
SSM_TILE_STATE = SSM_GROUPS_PER_TILE * SSM_STATE
SSM_STATES = SSM_GROUPS * SSM_STATE
CHUNK_BLOCKS = 4
META_ROWS = 8
M_OWNER, M_FIRST, M_SIZE, M_NCHUNK, M_LAST, M_NBLK, M_NUSED, M_OWNER_CHUNKS = range(8)
VMEM_LIMIT = 56 * 1024 * 1024

_NT = (((1,), (1,)), ((), ()))


def _cparams(*sem):
    return pltpu.CompilerParams(dimension_semantics=sem, vmem_limit_bytes=VMEM_LIMIT)


def _rms(x, gain):
    ms = jnp.mean(x * x, axis=-1, keepdims=True)
    return x * lax.rsqrt(ms + EPS) * gain


def _t5_bucket_table():
    qi = np.arange(BLOCK, dtype=np.int32)[:, None]
    ki = np.arange(2 * BLOCK, dtype=np.int32)[None, :]
    delta = BLOCK + qi - ki
    n = np.maximum(delta, 0)
    max_exact = REL_BUCKETS // 2
    nf = np.maximum(n, 1).astype(np.float32)
    large = max_exact + (np.log(nf / np.float32(max_exact)) / np.float32(math.log(REL_MAX_DIST / max_exact))
                         * np.float32(REL_BUCKETS - max_exact)).astype(np.int32)
    large = np.minimum(large, REL_BUCKETS - 1)
    return np.where(n < max_exact, n, large).astype(np.int32)


def _bias_kernel(rb_ref, bucket_ref, out_ref):
    h = pl.program_id(0)
    bucket = bucket_ref[...]
    acc = jnp.zeros(bucket.shape, F32)
    for b in range(REL_BUCKETS):
        acc = jnp.where(bucket == b, rb_ref[b, h], acc)
    qi = lax.broadcasted_iota(jnp.int32, bucket.shape, 0)
    ki = lax.broadcasted_iota(jnp.int32, bucket.shape, 1)
    delta = BLOCK + qi - ki
    valid = jnp.logical_and(delta >= 0, delta < WINDOW)
    out_ref[0] = jnp.where(valid, acc, jnp.float32(-1e30))


def _bias_table(rel_bias):
    bucket = jnp.asarray(_t5_bucket_table())
    return pl.pallas_call(
        _bias_kernel,
        grid=(ATTN_HEADS,),
        in_specs=[pl.BlockSpec(memory_space=pltpu.SMEM),
                  pl.BlockSpec((BLOCK, 2 * BLOCK), lambda h: (0, 0))],
        out_specs=pl.BlockSpec((1, BLOCK, 2 * BLOCK), lambda h: (h, 0, 0)),
        out_shape=jax.ShapeDtypeStruct((ATTN_HEADS, BLOCK, 2 * BLOCK), F32),
        compiler_params=_cparams("arbitrary"),
        name="t5_bias_table",
    )(rel_bias.astype(F32), bucket)


def _ssm_prep_kernel(lr_ref, li_ref, ldt_ref, br_ref, bi_ref, are_ref, aim_ref, bbr_ref, bbi_ref):
    lr = lr_ref[...]
    li = li_ref[...]
    dt = jnp.exp(ldt_ref[...])
    mag = jnp.exp(lr * dt)
    a_re = mag * jnp.cos(li * dt)
    a_im = mag * jnp.sin(li * dt)
    den = lr * lr + li * li
    nr = a_re - 1.0
    ni = a_im
    coef_re = (nr * lr + ni * li) / den
    coef_im = (ni * lr - nr * li) / den
    are_ref[...] = a_re
    aim_ref[...] = a_im
    br = br_ref[...]
    bi = bi_ref[...]
    bbr_ref[...] = coef_re * br - coef_im * bi
    bbi_ref[...] = coef_re * bi + coef_im * br


def _ssm_prep(lam_re, lam_im, log_dt, b_re, b_im):
    g, n, c = b_re.shape
    vec = jax.ShapeDtypeStruct((g, 1, n), F32)
    mat = jax.ShapeDtypeStruct((g, c, n), F32)
    return pl.pallas_call(
        _ssm_prep_kernel,
        out_shape=(vec, vec, mat, mat),
        name="ssm_discretise",
    )(lam_re.astype(F32).reshape(g, 1, n), lam_im.astype(F32).reshape(g, 1, n),
      log_dt.astype(F32).reshape(g, 1, 1),
      jnp.transpose(b_re.astype(F32), (0, 2, 1)), jnp.transpose(b_im.astype(F32), (0, 2, 1)))


def _block_diag_tiles(m):
    g, r, c = m.shape
    t = g // SSM_GROUPS_PER_TILE
    eye = jnp.eye(SSM_GROUPS_PER_TILE, dtype=m.dtype)
    m4 = m.reshape(t, SSM_GROUPS_PER_TILE, r, c)
    out = m4[:, :, :, None, :] * eye[None, :, None, :, None]
    return out.reshape(t, SSM_GROUPS_PER_TILE * r, SSM_GROUPS_PER_TILE * c)


def _inproj_kernel(x_ref, g_ref, w_ref, q_ref, k_ref, v_ref, u_ref, qx_ref):
    h = _rms(x_ref[...], g_ref[...]).astype(BF16)
    c0 = 0
    for ref in (q_ref, k_ref, v_ref, u_ref, qx_ref):
        c1 = c0 + ref.shape[-1]
        ref[...] = jnp.dot(h, w_ref[:, c0:c1], preferred_element_type=F32).astype(ref.dtype)
        c0 = c1


def _inproj(x2, gain, w_cols, batch, seq, tm):
    t, d = x2.shape
    nsb = seq // tm
    kvw = 2 * LANES
    ncol = w_cols.shape[1]
    row = lambda w: pl.BlockSpec((tm, w), lambda i: (i, 0))
    return pl.pallas_call(
        _inproj_kernel,
        grid=(t // tm,),
        in_specs=[row(d),
                  pl.BlockSpec((1, d), lambda i: (0, 0)),
                  pl.BlockSpec((d, ncol), lambda i: (0, 0), pipeline_mode=pl.Buffered(1))],
        out_specs=[row(ATTN_WIDTH), row(kvw), row(kvw),
                   pl.BlockSpec((tm, SSM_WIDTH), lambda i: (i % nsb, i // nsb)),
                   row(XATTN_WIDTH)],
        out_shape=[jax.ShapeDtypeStruct((t, ATTN_WIDTH), BF16),
                   jax.ShapeDtypeStruct((t, kvw), BF16),
                   jax.ShapeDtypeStruct((t, kvw), BF16),
                   jax.ShapeDtypeStruct((seq, batch * SSM_WIDTH), BF16),
                   jax.ShapeDtypeStruct((t, XATTN_WIDTH), BF16)],
        compiler_params=_cparams("arbitrary"),
        name="in_proj",
    )(x2, gain, w_cols)


def _swa_kernel(sinks_ref, q_ref, kp_ref, kc_ref, vp_ref, vc_ref, bias_ref, qg_ref, kg_ref, og_ref, out_ref):
    n = pl.program_id(1)
    kcat = jnp.concatenate([kp_ref[...], kc_ref[...]], axis=0).astype(F32)
    vcat = jnp.concatenate([vp_ref[...], vc_ref[...]], axis=0)
    lane = lax.broadcasted_iota(jnp.int32, (1, LANES), 1)
    lo = lane < ATTN_HEAD_DIM
    key = lax.broadcasted_iota(jnp.int32, (1, 2 * BLOCK), 1)
    kmask = jnp.where(jnp.logical_and(n == 0, key < BLOCK), jnp.float32(-1e30), jnp.float32(0.0))
    qgain = qg_ref[...]
    pairs_per_kv = ATTN_HEADS // ATTN_KV_HEADS // 2
    outs = []
    for g in range(ATTN_KV_HEADS):
        kn = _rms(kcat[:, g * LANES:(g + 1) * LANES], kg_ref[...]).astype(BF16)
        vg = vcat[:, g * LANES:(g + 1) * LANES]
        for j in range(pairs_per_kv):
            tile = g * pairs_per_kv + j
            qp = q_ref[:, tile * LANES:(tile + 1) * LANES].astype(F32)
            sq = qp * qp
            s_lo = jnp.sum(jnp.where(lo, sq, 0.0), axis=-1, keepdims=True)
            s_hi = jnp.sum(sq, axis=-1, keepdims=True) - s_lo
            inv = jnp.where(lo, lax.rsqrt(s_lo / ATTN_HEAD_DIM + EPS), lax.rsqrt(s_hi / ATTN_HEAD_DIM + EPS))
            qn = qp * inv * qgain
            halves = []
            for half in range(2):
                hd = 2 * tile + half
                keep = lo if half == 0 else jnp.logical_not(lo)
                qh = jnp.where(keep, qn, 0.0).astype(BF16)
                l = lax.dot_general(qh, kn, _NT, preferred_element_type=F32)
                l = l + bias_ref[hd] + kmask
                sink = sinks_ref[hd]
                m = jnp.maximum(jnp.max(l, axis=-1, keepdims=True), sink)
                p = jnp.exp(l - m)
                den = jnp.sum(p, axis=-1, keepdims=True) + jnp.exp(sink - m)
                halves.append(jnp.dot(p.astype(BF16), vg, preferred_element_type=F32) / den)
            outs.append(jnp.where(lo, halves[0], halves[1]))
    y = jnp.concatenate(outs, axis=-1)
    out_ref[...] = _rms(y, og_ref[...]).astype(out_ref.dtype)


def _swa(q, kk, vv, bias, sinks, qgain, kgain, ogain, batch, seq):
    nb = seq // BLOCK
    kvw = 2 * LANES
    cur = lambda b, n: (b * nb + n, 0)
    prev = lambda b, n: (b * nb + jnp.maximum(n - 1, 0), 0)
    const2 = lambda b, n: (0, 0)
    return pl.pallas_call(
        _swa_kernel,
        grid=(batch, nb),
        in_specs=[pl.BlockSpec(memory_space=pltpu.SMEM),
                  pl.BlockSpec((BLOCK, ATTN_WIDTH), cur),
                  pl.BlockSpec((BLOCK, kvw), prev), pl.BlockSpec((BLOCK, kvw), cur),
                  pl.BlockSpec((BLOCK, kvw), prev), pl.BlockSpec((BLOCK, kvw), cur),
                  pl.BlockSpec((ATTN_HEADS, BLOCK, 2 * BLOCK), lambda b, n: (0, 0, 0)),
                  pl.BlockSpec((1, LANES), const2), pl.BlockSpec((1, LANES), const2),
                  pl.BlockSpec((1, ATTN_WIDTH), const2)],
        out_specs=pl.BlockSpec((BLOCK, ATTN_WIDTH), cur),
        out_shape=jax.ShapeDtypeStruct((batch * seq, ATTN_WIDTH), BF16),
        compiler_params=_cparams("arbitrary", "arbitrary"),
        name="swa_attention",
    )(sinks, q, kk, kk, vv, vv, bias, qgain, kgain, ogain)


def _memkv_kernel(m_ref, g_ref, w_ref, kg_ref, k_ref, v_ref):
    h = _rms(m_ref[...], g_ref[...]).astype(BF16)
    km = jnp.dot(h, w_ref[:, :XATTN_WIDTH], preferred_element_type=F32)
    for hd in range(XATTN_HEADS):
        sl = slice(hd * XATTN_HEAD_DIM, (hd + 1) * XATTN_HEAD_DIM)
        k_ref[:, sl] = _rms(km[:, sl], kg_ref[...]).astype(k_ref.dtype)
    v_ref[...] = jnp.dot(h, w_ref[:, XATTN_WIDTH:], preferred_element_type=F32).astype(v_ref.dtype)


def _memkv(mem2, gain, w_kv, kgain, tm):
    r, d = mem2.shape
    row = lambda w: pl.BlockSpec((tm, w), lambda i: (i, 0))
    const = lambda shape: pl.BlockSpec(shape, lambda i: (0, 0))
    return pl.pallas_call(
        _memkv_kernel,
        grid=(r // tm,),
        in_specs=[row(d), const((1, d)), const((d, 2 * XATTN_WIDTH)), const((1, XATTN_HEAD_DIM))],
        out_specs=[row(XATTN_WIDTH), row(XATTN_WIDTH)],
        out_shape=[jax.ShapeDtypeStruct((r, XATTN_WIDTH), BF16)] * 2,
        compiler_params=_cparams("arbitrary"),
        name="mem_kv_proj",
    )(mem2, gain, w_kv, kgain)


def _xattn_kernel(q_ref, k_ref, v_ref, qg_ref, og_ref, out_ref):
    outs = []
    for hd in range(XATTN_HEADS):
        sl = slice(hd * XATTN_HEAD_DIM, (hd + 1) * XATTN_HEAD_DIM)
        qn = _rms(q_ref[:, sl].astype(F32), qg_ref[...]).astype(BF16)
        l = lax.dot_general(qn, k_ref[:, sl], _NT, preferred_element_type=F32)
        m = jnp.max(l, axis=-1, keepdims=True)
        p = jnp.exp(l - m)
        den = jnp.sum(p, axis=-1, keepdims=True)
        outs.append(jnp.dot(p.astype(BF16), v_ref[:, sl], preferred_element_type=F32) / den)
    y = jnp.concatenate(outs, axis=-1)
    out_ref[...] = _rms(y, og_ref[...]).astype(out_ref.dtype)


def _xattn(qx, km, vm, qgain, ogain, batch, seq, mem_len, tq):
    nq = seq // tq
    const2 = lambda b, i: (0, 0)
    return pl.pallas_call(
        _xattn_kernel,
        grid=(batch, nq),
        in_specs=[pl.BlockSpec((tq, XATTN_WIDTH), lambda b, i: (b * nq + i, 0)),
                  pl.BlockSpec((mem_len, XATTN_WIDTH), lambda b, i: (b, 0)),
                  pl.BlockSpec((mem_len, XATTN_WIDTH), lambda b, i: (b, 0)),
                  pl.BlockSpec((1, XATTN_HEAD_DIM), const2),
                  pl.BlockSpec((1, XATTN_WIDTH), const2)],
        out_specs=pl.BlockSpec((tq, XATTN_WIDTH), lambda b, i: (b * nq + i, 0)),
        out_shape=jax.ShapeDtypeStruct((batch * seq, XATTN_WIDTH), BF16),
        compiler_params=_cparams("arbitrary", "arbitrary"),
        name="mem_xattn",
    )(qx, km, vm, qgain, ogain)


def _ssm_kernel(u_ref, bmat_ref, cmat_ref, are_ref, aim_ref, d_ref, wglu_ref, og_ref, out_ref,
                tb_ref, xr_ref, xi_ref, sr_ref, si_ref, *, batch, steps):
    @pl.when(pl.program_id(0) == 0)
    def _():
        sr_ref[...] = jnp.zeros_like(sr_ref)
        si_ref[...] = jnp.zeros_like(si_ref)

    for b in range(batch):
        for j in range(SSM_TILES):
            c0 = b * SSM_WIDTH + j * LANES
            tb_ref[j, pl.ds(b, steps, stride=batch), :] = u_ref[:, c0:c0 + LANES].astype(F32)
    uf = jnp.concatenate([tb_ref[j] for j in range(SSM_TILES)], axis=-1)
    u = uf.astype(BF16)
    for j in range(SSM_TILES):
        bu = jnp.dot(u[:, j * LANES:(j + 1) * LANES], bmat_ref[j], preferred_element_type=F32)
        xr_ref[:, j * SSM_TILE_STATE:(j + 1) * SSM_TILE_STATE] = bu[:, :SSM_TILE_STATE]
        xi_ref[:, j * SSM_TILE_STATE:(j + 1) * SSM_TILE_STATE] = bu[:, SSM_TILE_STATE:]

    chunk = 1024
    for c0 in range(0, SSM_STATES, chunk):
        cs = slice(c0, c0 + chunk)
        ar = jnp.broadcast_to(are_ref[:, cs], (batch, chunk))
        ai = jnp.broadcast_to(aim_ref[:, cs], (batch, chunk))

        def step(t, carry, cs=cs, ar=ar, ai=ai):
            s_r, s_i = carry
            rows = pl.ds(pl.multiple_of(t * batch, batch), batch)
            n_r = ar * s_r - ai * s_i + xr_ref[rows, cs]
            n_i = ar * s_i + ai * s_r + xi_ref[rows, cs]
            xr_ref[rows, cs] = n_r
            xi_ref[rows, cs] = n_i
            return n_r, n_i

        s_r, s_i = lax.fori_loop(0, steps, step, (sr_ref[:, cs], si_ref[:, cs]))
        sr_ref[:, cs] = s_r
        si_ref[:, cs] = s_i

    ys = []
    for j in range(SSM_TILES):
        sl = slice(j * SSM_TILE_STATE, (j + 1) * SSM_TILE_STATE)
        xcat = jnp.concatenate([xr_ref[:, sl], xi_ref[:, sl]], axis=-1).astype(BF16)
        ys.append(jnp.dot(xcat, cmat_ref[j], preferred_element_type=F32))
    y = jnp.concatenate(ys, axis=-1) + d_ref[...] * uf
    y = jax.nn.gelu(y)
    y = y * jax.nn.sigmoid(jnp.dot(y.astype(BF16), wglu_ref[...], preferred_element_type=F32))
    y = _rms(y, og_ref[...])
    for j in range(SSM_TILES):
        tb_ref[j] = y[:, j * LANES:(j + 1) * LANES]
    for b in range(batch):
        for j in range(SSM_TILES):
            c0 = b * SSM_WIDTH + j * LANES
            out_ref[:, c0:c0 + LANES] = tb_ref[j, pl.ds(b, steps, stride=batch), :].astype(out_ref.dtype)


def _ssm(u_sb, bmat, cmat, a_re, a_im, d_skip, w_glu, ogain, batch, seq, steps):
    rows = steps * batch
    const2 = lambda c: (0, 0)
    const3 = lambda c: (0, 0, 0)
    return pl.pallas_call(
        functools.partial(_ssm_kernel, batch=batch, steps=steps),
        grid=(seq // steps,),
        in_specs=[pl.BlockSpec((steps, batch * SSM_WIDTH), lambda c: (c, 0)),
                  pl.BlockSpec(bmat.shape, const3), pl.BlockSpec(cmat.shape, const3),
                  pl.BlockSpec((1, SSM_STATES), const2), pl.BlockSpec((1, SSM_STATES), const2),
                  pl.BlockSpec((1, SSM_WIDTH), const2),
                  pl.BlockSpec((SSM_WIDTH, SSM_WIDTH), const2),
                  pl.BlockSpec((1, SSM_WIDTH), const2)],
        out_specs=pl.BlockSpec((steps, batch * SSM_WIDTH), lambda c: (c, 0)),
        out_shape=jax.ShapeDtypeStruct((seq, batch * SSM_WIDTH), BF16),
        scratch_shapes=[pltpu.VMEM((SSM_TILES, rows, LANES), F32),
                        pltpu.VMEM((rows, SSM_STATES), F32), pltpu.VMEM((rows, SSM_STATES), F32),
                        pltpu.VMEM((batch, SSM_STATES), F32), pltpu.VMEM((batch, SSM_STATES), F32)],
        compiler_params=_cparams("arbitrary"),
        name="s5_layer",
    )(u_sb, bmat, cmat, a_re, a_im, d_skip, w_glu, ogain)


def _outproj_kernel(ya_ref, ys_ref, yx_ref, x_ref, wo_ref, g_ref, wr_ref, x1_ref, lt_ref, hp_ref, *, sub):
    for r0 in range(0, x_ref.shape[0], sub):
        rows = slice(r0, r0 + sub)
        mix = jnp.concatenate([ya_ref[rows, :], ys_ref[rows, :], yx_ref[rows, :]], axis=-1)
        x1 = x_ref[rows, :] + jnp.dot(mix, wo_ref[...], preferred_element_type=F32)
        x1_ref[rows, :] = x1
        h2 = _rms(x1, g_ref[...])
        hi = h2.astype(BF16)
        hi32 = hi.astype(F32)
        lo = (h2 - hi32).astype(BF16)
        both = jnp.dot(hi, wr_ref[...], preferred_element_type=F32)
        lg = both[:, :LANES] + both[:, LANES:] + jnp.dot(lo, wr_ref[:, :LANES], preferred_element_type=F32)
        lt_ref[rows, :] = lg
        bits = lax.bitcast_convert_type(hi32, jnp.uint32)
        half = bits.shape[1] // 2
        hp_ref[rows, :] = (bits[:, half:] & jnp.uint32(0xFFFF0000)) | (bits[:, :half] >> 16)


def _outproj(ya, ys_sb, yx, x2, w_o, gain, wr, batch, seq, tm):
    t, d = x2.shape
    nsb = seq // tm
    row = lambda w: pl.BlockSpec((tm, w), lambda i: (i, 0))
    const = lambda shape: pl.BlockSpec(shape, lambda i: (0, 0), pipeline_mode=pl.Buffered(1))
    return pl.pallas_call(
        functools.partial(_outproj_kernel, sub=min(256, tm)),
        grid=(t // tm,),
        in_specs=[row(ATTN_WIDTH),
                  pl.BlockSpec((tm, SSM_WIDTH), lambda i: (i % nsb, i // nsb)),
                  row(XATTN_WIDTH), row(d),
                  const(w_o.shape), const((1, d)), const(wr.shape)],
        out_specs=[row(d), row(LANES), row(d // 2)],
        out_shape=[jax.ShapeDtypeStruct((t, d), F32), jax.ShapeDtypeStruct((t, LANES), F32),
                   jax.ShapeDtypeStruct((t, d // 2), jnp.uint32)],
        compiler_params=_cparams("arbitrary"),
        name="out_proj_router",
    )(ya, ys_sb, yx, x2, w_o, gain, wr)


def _route_kernel(lt_ref, tri_ref, dest_ref, w_ref, meta_ref, cnt_ref, carry_ref, pstart_ref):
    phase = pl.program_id(0)
    c = pl.program_id(1)
    logits = lt_ref[...].T
    tc = logits.shape[1]
    ng, epg = N_EXPERT_GROUPS, EXPERTS_PER_GROUP
    row8 = lax.broadcasted_iota(jnp.int32, (ng, tc), 0)

    gl = logits[0:ng]
    gmax = jnp.max(gl, axis=0, keepdims=True)
    gidx = jnp.min(jnp.where(gl == gmax, row8, ng), axis=0, keepdims=True)
    gate = 1.0 / jnp.sum(jnp.exp(gl - gmax), axis=0, keepdims=True)
    sel = jnp.zeros((epg, tc), F32)
    for g in range(ng):
        sel = jnp.where(gidx == g, logits[ng + g * epg:ng + (g + 1) * epg], sel)
    v1 = jnp.max(sel, axis=0, keepdims=True)
    i1 = jnp.min(jnp.where(sel == v1, row8, epg), axis=0, keepdims=True)
    sel2 = jnp.where(row8 == i1, -jnp.inf, sel)
    v2 = jnp.max(sel2, axis=0, keepdims=True)
    i2 = jnp.min(jnp.where(sel2 == v2, row8, epg), axis=0, keepdims=True)
    e = jnp.exp(v2 - v1)
    w1 = gate * (1.0 / (1.0 + e))
    w2 = gate * (e / (1.0 + e))
    e1 = gidx * epg + i1
    e2 = gidx * epg + i2
    rowe = lax.broadcasted_iota(jnp.int32, (N_EXPERTS, tc), 0)
    oh1 = rowe == e1
    oh2 = rowe == e2
    member = jnp.where(jnp.logical_or(oh1, oh2), 1.0, 0.0)
    chunk_cnt = jnp.sum(member, axis=1, keepdims=True)

    @pl.when(phase == 0)
    def _():
        @pl.when(c == 0)
        def _():
            cnt_ref[...] = jnp.zeros_like(cnt_ref)
        cnt_ref[...] += chunk_cnt

    @pl.when(phase == 1)
    def _():
        @pl.when(c == 0)
        def _():
            cnt = cnt_ref[...]
            nblk = jnp.floor((cnt + (MOE_BLOCK - 1)) * (1.0 / MOE_BLOCK))
            nchunk = jnp.floor((nblk + (CHUNK_BLOCKS - 1)) * (1.0 / CHUNK_BLOCKS))
            r = lax.broadcasted_iota(jnp.int32, (N_EXPERTS, LANES), 0)
            cidx = lax.broadcasted_iota(jnp.int32, (N_EXPERTS, LANES), 1)
            to_row = lambda col: jnp.sum(jnp.where(r == cidx, col, 0.0), axis=0, keepdims=True)
            cumsum_col = lambda col: jnp.sum(jnp.where(cidx <= r, to_row(col), 0.0), axis=1, keepdims=True)
            cumsum_row = lambda col: jnp.sum(jnp.where(r <= cidx, col, 0.0), axis=0, keepdims=True)
            bend = cumsum_col(nblk)
            bstart = bend - nblk
            cend = cumsum_col(nchunk)
            cstart = cend - nchunk
            pstart_ref[...] = bstart * MOE_BLOCK
            carry_ref[...] = jnp.zeros_like(carry_ref)
            lanef = lax.broadcasted_iota(jnp.int32, (1, LANES), 1).astype(F32)
            owner = jnp.minimum(jnp.sum(jnp.where(cend <= lanef, 1.0, 0.0), axis=0, keepdims=True),
                                N_EXPERTS - 1.0)
            own = r.astype(F32) == owner
            pick = lambda col: jnp.sum(jnp.where(own, col, 0.0), axis=0, keepdims=True)
            idx = lanef - pick(cstart)
            first = pick(bstart) + CHUNK_BLOCKS * idx
            size = jnp.clip(pick(nblk) - CHUNK_BLOCKS * idx, 0.0, float(CHUNK_BLOCKS))
            zero = jnp.zeros((1, LANES), F32)
            rows = [owner, first, size,
                    zero + jnp.sum(nchunk, axis=0, keepdims=True),
                    cumsum_row(nblk) - 1.0,
                    to_row(nblk),
                    zero + jnp.sum(nblk, axis=0, keepdims=True),
                    pick(nchunk)]
            for k, v in enumerate(rows):
                meta_ref[k:k + 1, :] = v.astype(jnp.int32)

        before = carry_ref[...] + jnp.dot(member.astype(BF16), tri_ref[...], preferred_element_type=F32)
        pos = before + pstart_ref[...]
        dest_ref[0:1, :] = jnp.sum(jnp.where(oh1, pos, 0.0), axis=0, keepdims=True).astype(jnp.int32)
        dest_ref[1:2, :] = jnp.sum(jnp.where(oh2, pos, 0.0), axis=0, keepdims=True).astype(jnp.int32)
        w_ref[0:1, :] = w1
        w_ref[1:2, :] = w2
        carry_ref[...] += chunk_cnt


def _route(logits_t, tc):
    t = logits_t.shape[0]
    nc = t // tc
    tri = jnp.asarray(np.triu(np.ones((tc, tc), np.float32), k=1), dtype=BF16)
    return pl.pallas_call(
        _route_kernel,
        grid=(2, nc),
        in_specs=[pl.BlockSpec((tc, LANES), lambda p, c: (c, 0)),
                  pl.BlockSpec((tc, tc), lambda p, c: (0, 0))],
        out_specs=[pl.BlockSpec((2, tc), lambda p, c: (0, c * p)),
                   pl.BlockSpec((2, tc), lambda p, c: (0, c * p)),
                   pl.BlockSpec((META_ROWS, LANES), lambda p, c: (0, 0))],
        out_shape=[jax.ShapeDtypeStruct((2, t), jnp.int32), jax.ShapeDtypeStruct((2, t), F32),
                   jax.ShapeDtypeStruct((META_ROWS, LANES), jnp.int32)],
        scratch_shapes=[pltpu.VMEM((N_EXPERTS, 1), F32)] * 3,
        compiler_params=_cparams("arbitrary", "arbitrary"),
        name="moe_route",
    )(logits_t, tri)


def _meta(meta_ref, row, lane=0):
    return meta_ref[row * LANES + lane]


def _fill_blocks(meta_ref, zbuf, dst_ref, sem, n_blocks, *, expert_tails):
    zbuf[...] = jnp.zeros_like(zbuf)
    n_used = _meta(meta_ref, M_NUSED)
    block = lambda b: pltpu.make_async_copy(zbuf, dst_ref.at[pl.ds(b * MOE_BLOCK, MOE_BLOCK), :], sem)

    def tails(fn):
        def body(e, carry):
            @pl.when(_meta(meta_ref, M_NBLK, e) > 0)
            def _():
                fn(block(_meta(meta_ref, M_LAST, e)))
            return carry
        lax.fori_loop(0, N_EXPERTS, body, 0)

    def unused(fn):
        def body(b, carry):
            fn(block(b))
            return carry
        lax.fori_loop(n_used, n_blocks, body, 0)

    for phase in (lambda cp: cp.start(), lambda cp: cp.wait()):
        if expert_tails:
            tails(phase)
        unused(phase)


def _dispatch_kernel(dest_ref, meta_ref, h_ref, xs_ref, zbuf, sem_z, sem, *, tokens, n_blocks):
    tm = h_ref.shape[0]

    @pl.when(pl.program_id(0) == 0)
    def _():
        _fill_blocks(meta_ref, zbuf, xs_ref, sem_z, n_blocks, expert_tails=True)

    base = pl.program_id(0) * tm

    def issue(r, carry):
        for k in range(2):
            d = dest_ref[k * tokens + base + r]
            pltpu.make_async_copy(h_ref.at[pl.ds(r, 1), :], xs_ref.at[pl.ds(d, 1), :], sem).start()
        return carry
    lax.fori_loop(0, tm, issue, 0, unroll=8)
    for k in range(2):
        pltpu.make_async_copy(h_ref, xs_ref.at[pl.ds(0, tm), :], sem).wait()


def _dispatch(dest_flat, meta_flat, h2p, n_blocks, tm):
    t, w = h2p.shape
    grid_spec = pltpu.PrefetchScalarGridSpec(
        num_scalar_prefetch=2,
        grid=(t // tm,),
        in_specs=[pl.BlockSpec((tm, w), lambda i, d, m: (i, 0))],
        out_specs=pl.BlockSpec(memory_space=pl.ANY),
        scratch_shapes=[pltpu.VMEM((MOE_BLOCK, w), h2p.dtype),
                        pltpu.SemaphoreType.DMA(()), pltpu.SemaphoreType.DMA(())],
    )
    return pl.pallas_call(
        functools.partial(_dispatch_kernel, tokens=t, n_blocks=n_blocks),
        grid_spec=grid_spec,
        out_shape=jax.ShapeDtypeStruct((n_blocks * MOE_BLOCK, w), h2p.dtype),
        compiler_params=_cparams("arbitrary"),
        name="moe_dispatch",
    )(dest_flat, meta_flat, h2p)


def _unpack_rows(words):
    lo = lax.bitcast_convert_type(words << 16, F32)
    hi = lax.bitcast_convert_type(words & jnp.uint32(0xFFFF0000), F32)
    return jnp.concatenate([lo, hi], axis=-1).astype(BF16)


def _expert_kernel(meta_ref, xs_ref, wg_ref, wu_ref, wd_ref, yb_ref,
                   xbuf, ybuf, zbuf, wg_f32, wu_f32, wd_f32, wg_bf, wu_bf, wd_bf, wslot_ref,
                   sem_in, sem_out, sem_z, sem_w, *, n_blocks):
    c = pl.program_id(0)
    n_chunks = _meta(meta_ref, M_NCHUNK)
    slot = c % 2

    def weight_copies(k, s):
        e = _meta(meta_ref, M_OWNER, k)
        return [pltpu.make_async_copy(src.at[e], dst.at[s], sem_w.at[s])
                for src, dst in ((wg_ref, wg_f32), (wu_ref, wu_f32), (wd_ref, wd_f32))]

    def in_copy(k, s, nb):
        rows = nb * MOE_BLOCK
        src = xs_ref.at[pl.ds(_meta(meta_ref, M_FIRST, k) * MOE_BLOCK, rows), :]
        return pltpu.make_async_copy(src, xbuf.at[s, pl.ds(0, rows), :], sem_in.at[s])

    def out_copy(k, s, nb):
        rows = nb * MOE_BLOCK
        dst = yb_ref.at[pl.ds(_meta(meta_ref, M_FIRST, k) * MOE_BLOCK, rows), :]
        return pltpu.make_async_copy(ybuf.at[s, pl.ds(0, rows), :], dst, sem_out.at[s])

    def by_size(k, fn):
        for nb in range(1, CHUNK_BLOCKS + 1):
            pl.when(_meta(meta_ref, M_SIZE, k) == nb)(functools.partial(fn, nb))

    @pl.when(c == 0)
    def _():
        by_size(0, lambda nb: in_copy(0, 0, nb).start())

    @pl.when(c + 1 < n_chunks)
    def _():
        by_size(c + 1, lambda nb: in_copy(c + 1, 1 - slot, nb).start())

    @pl.when(c < n_chunks)
    def _():
        prev = jnp.maximum(c - 1, 0)
        new_expert = jnp.logical_or(c == 0, _meta(meta_ref, M_OWNER, c) != _meta(meta_ref, M_OWNER, prev))

        @pl.when(c == 0)
        def _():
            wslot_ref[0] = 0
            for cp in weight_copies(0, 0):
                cp.start()

        @pl.when(new_expert)
        def _():
            ws = jnp.where(c == 0, 0, 1 - wslot_ref[0])
            wslot_ref[0] = ws
            nxt = c + _meta(meta_ref, M_OWNER_CHUNKS, c)

            @pl.when(nxt < n_chunks)
            def _():
                for cp in weight_copies(nxt, 1 - ws):
                    cp.start()
            for cp in weight_copies(c, ws):
                cp.wait()
            wg_bf[...] = wg_f32[ws].astype(BF16)
            wu_bf[...] = wu_f32[ws].astype(BF16)
            wd_bf[...] = wd_f32[ws].astype(BF16)

        @pl.when(c >= 2)
        def _():
            by_size(c - 2, lambda nb: out_copy(c - 2, slot, nb).wait())

        def compute(nb):
            rows = nb * MOE_BLOCK
            in_copy(c, slot, nb).wait()
            h = _unpack_rows(xbuf[slot, 0:rows, :])
            gate = jnp.dot(h, wg_bf[...], preferred_element_type=F32)
            up = jnp.dot(h, wu_bf[...], preferred_element_type=F32)
            act = (jax.nn.silu(gate) * up).astype(BF16)
            ybuf[slot, 0:rows, :] = jnp.dot(act, wd_bf[...], preferred_element_type=F32)
            out_copy(c, slot, nb).start()
        by_size(c, compute)

    @pl.when(c == pl.num_programs(0) - 1)
    def _():
        _fill_blocks(meta_ref, zbuf, yb_ref, sem_z, n_blocks, expert_tails=False)
        for back in (2, 1):
            @pl.when(n_chunks >= back)
            def _(back=back):
                k = n_chunks - back
                by_size(k, lambda nb: out_copy(k, k % 2, nb).wait())


def _experts(meta_flat, xs, w_gate, w_up, w_down, n_blocks, n_chunks_max):
    d, de = w_gate.shape[1], w_gate.shape[2]
    rows = CHUNK_BLOCKS * MOE_BLOCK

    hbm = pl.BlockSpec(memory_space=pl.ANY)
    grid_spec = pltpu.PrefetchScalarGridSpec(
        num_scalar_prefetch=1,
        grid=(n_chunks_max,),
        in_specs=[hbm, hbm, hbm, hbm],
        out_specs=hbm,
        scratch_shapes=[pltpu.VMEM((2, rows, xs.shape[1]), xs.dtype),
                        pltpu.VMEM((2, rows, d), F32),
                        pltpu.VMEM((MOE_BLOCK, d), F32),
                        pltpu.VMEM((2, d, de), F32), pltpu.VMEM((2, d, de), F32), pltpu.VMEM((2, de, d), F32),
                        pltpu.VMEM((d, de), BF16), pltpu.VMEM((d, de), BF16), pltpu.VMEM((de, d), BF16),
                        pltpu.SMEM((1,), jnp.int32),
                        pltpu.SemaphoreType.DMA((2,)), pltpu.SemaphoreType.DMA((2,)),
                        pltpu.SemaphoreType.DMA(()), pltpu.SemaphoreType.DMA((2,))],
    )
    return pl.pallas_call(
        functools.partial(_expert_kernel, n_blocks=n_blocks),
        grid_spec=grid_spec,
        out_shape=jax.ShapeDtypeStruct((n_blocks * MOE_BLOCK, d), F32),
        compiler_params=_cparams("arbitrary"),
        name="moe_experts",
    )(meta_flat, xs, w_gate, w_up, w_down)


def _combine_kernel(dest_ref, yb_ref, x1_ref, w_ref, out_ref, gbuf, sem, *, tokens):
    tm = x1_ref.shape[0]
    base = pl.program_id(0) * tm

    def issue(r, carry):
        for k in range(2):
            d = dest_ref[k * tokens + base + r]
            pltpu.make_async_copy(yb_ref.at[pl.ds(d, 1), :], gbuf.at[k, pl.ds(r, 1), :], sem).start()
        return carry
    lax.fori_loop(0, tm, issue, 0, unroll=8)
    for k in range(2):
        pltpu.make_async_copy(yb_ref.at[pl.ds(0, tm), :], gbuf.at[k], sem).wait()
    w = w_ref[...]
    out_ref[...] = x1_ref[...] + (gbuf[0] * w[:, 0:1] + gbuf[1] * w[:, 1:2])


def _combine(dest_flat, yb, x1, w_tok, tm):
    t, d = x1.shape
    grid_spec = pltpu.PrefetchScalarGridSpec(
        num_scalar_prefetch=1,
        grid=(t // tm,),
        in_specs=[pl.BlockSpec(memory_space=pl.ANY),
                  pl.BlockSpec((tm, d), lambda i, dr: (i, 0)),
                  pl.BlockSpec((tm, 2), lambda i, dr: (i, 0))],
        out_specs=pl.BlockSpec((tm, d), lambda i, dr: (i, 0)),
        scratch_shapes=[pltpu.VMEM((2, tm, d), F32), pltpu.SemaphoreType.DMA(())],
    )
    return pl.pallas_call(
        functools.partial(_combine_kernel, tokens=t),
        grid_spec=grid_spec,
        out_shape=jax.ShapeDtypeStruct((t, d), F32),
        compiler_params=_cparams("arbitrary"),
        name="moe_combine",
    )(dest_flat, yb, x1, w_tok)


def _row(v):
    return v.astype(F32).reshape(1, -1)


def _layer(x2, mem2, batch, seq, mem_len, p):
    t, d = x2.shape

    w_in = p["w_in"]
    a0 = ATTN_WIDTH
    kw = ATTN_KV_HEADS * ATTN_HEAD_DIM
    dup = lambda w: jnp.concatenate(
        [w[:, h * ATTN_HEAD_DIM:(h + 1) * ATTN_HEAD_DIM] for h in range(ATTN_KV_HEADS) for _ in range(2)], axis=1)
    w_cols = jnp.concatenate([w_in[:, :a0], dup(w_in[:, a0:a0 + kw]), dup(w_in[:, a0 + kw:a0 + 2 * kw]),
                              w_in[:, a0 + 2 * kw:]], axis=1).astype(BF16)
    tm_in = min(512, seq)
    q, kk, vv, u_sb, qx = _inproj(x2, _row(p["norm_mix"]), w_cols, batch, seq, tm_in)

    bias = _bias_table(p["rel_bias"])
    qgain = jnp.tile(_row(p["q_norm"]), (1, 2)) * (1.0 / math.sqrt(ATTN_HEAD_DIM))
    kgain = jnp.tile(_row(p["k_norm"]), (1, 2))
    ya = _swa(q, kk, vv, bias, p["attn_sinks"].astype(F32), qgain, kgain, _row(p["out_norm_attn"]), batch, seq)

    km, vm = _memkv(mem2, _row(p["mem_norm"]), p["w_mem_kv"].astype(BF16), _row(p["xk_norm"]),
                    min(256, mem2.shape[0]))
    xq_gain = _row(p["xq_norm"]) * (1.0 / math.sqrt(XATTN_HEAD_DIM))
    yx = _xattn(qx, km, vm, xq_gain, _row(p["out_norm_xattn"]), batch, seq, mem_len, min(512, seq))

    a_re, a_im, bbr, bbi = _ssm_prep(p["ssm_lambda_re"], p["ssm_lambda_im"], p["ssm_log_dt"],
                                     p["ssm_b_re"], p["ssm_b_im"])
    bmat = jnp.concatenate([_block_diag_tiles(bbr), _block_diag_tiles(bbi)], axis=-1).astype(BF16)
    c_re_t = jnp.transpose(p["ssm_c_re"].astype(F32), (0, 2, 1))
    c_im_t = jnp.transpose(p["ssm_c_im"].astype(F32), (0, 2, 1))
    cmat = jnp.concatenate([_block_diag_tiles(c_re_t), _block_diag_tiles(-c_im_t)], axis=1).astype(BF16)
    steps = min(64, seq)
    ys_sb = _ssm(u_sb, bmat, cmat,
                 a_re.reshape(1, SSM_STATES), a_im.reshape(1, SSM_STATES), _row(p["ssm_d"]),
                 p["ssm_w_glu"].astype(BF16), _row(p["out_norm_ssm"]), batch, seq, steps)

    wr = jnp.concatenate([p["w_router_group"], p["w_router_expert"]], axis=1).astype(F32)
    wr = jnp.pad(wr, ((0, 0), (0, LANES - wr.shape[1])))
    wr_hi = wr.astype(BF16)
    wr_lo = (wr - wr_hi.astype(F32)).astype(BF16)
    tm_out = min(512, seq)
    x1, logits_t, h2p = _outproj(ya, ys_sb, yx, x2, p["w_o"].astype(BF16), _row(p["norm_ffn"]),
                                 jnp.concatenate([wr_hi, wr_lo], axis=1), batch, seq, tm_out)

    dest, w_k, meta = _route(logits_t, min(512, t))
    n_blocks = (2 * t) // MOE_BLOCK + N_EXPERTS
    n_chunks_max = (n_blocks + (CHUNK_BLOCKS - 1) * N_EXPERTS) // CHUNK_BLOCKS
    dest_flat = dest.reshape(2 * t)
    meta_flat = meta.reshape(META_ROWS * LANES)
    xs = _dispatch(dest_flat, meta_flat, h2p, n_blocks, min(1024, t))
    yb = _experts(meta_flat, xs, p["w_gate"], p["w_up"], p["w_down"], n_blocks, n_chunks_max)
    return _combine(dest_flat, yb, x1, w_k.T, min(256, t))


def kernel(x, mem, norm_mix, w_in, q_norm, k_norm, attn_sinks, rel_bias, ssm_lambda_re, ssm_lambda_im, ssm_log_dt, ssm_b_re, ssm_b_im, ssm_c_re, ssm_c_im, ssm_d, ssm_w_glu, mem_norm, w_mem_kv, xq_norm, xk_norm, out_norm_attn, out_norm_ssm, out_norm_xattn, w_o, norm_ffn, w_router_group, w_router_expert, w_gate, w_up, w_down):
    batch, seq, d = x.shape
    mem_len = mem.shape[1]
    per_layer = dict(norm_mix=norm_mix, w_in=w_in, q_norm=q_norm, k_norm=k_norm, attn_sinks=attn_sinks,
                     ssm_lambda_re=ssm_lambda_re, ssm_lambda_im=ssm_lambda_im, ssm_log_dt=ssm_log_dt,
                     ssm_b_re=ssm_b_re, ssm_b_im=ssm_b_im, ssm_c_re=ssm_c_re, ssm_c_im=ssm_c_im,
                     ssm_d=ssm_d, ssm_w_glu=ssm_w_glu, mem_norm=mem_norm, w_mem_kv=w_mem_kv,
                     xq_norm=xq_norm, xk_norm=xk_norm, out_norm_attn=out_norm_attn,
                     out_norm_ssm=out_norm_ssm, out_norm_xattn=out_norm_xattn, w_o=w_o, norm_ffn=norm_ffn,
                     w_router_group=w_router_group, w_router_expert=w_router_expert,
                     w_gate=w_gate, w_up=w_up, w_down=w_down)
    x2 = x.astype(F32).reshape(batch * seq, d)
    mem2 = mem.astype(F32).reshape(batch * mem_len, d)
    for l in range(norm_mix.shape[0]):
        p = {k: v[l] for k, v in per_layer.items()}
        p["rel_bias"] = rel_bias
        x2 = _layer(x2, mem2, batch, seq, mem_len, p)
    return x2.reshape(batch, seq, d).astype(x.dtype)
```

```python
import functools
import math

import numpy as np
import jax
import jax.numpy as jnp
from jax import lax
from jax.experimental import pallas as pl
from jax.experimental.pallas import tpu as pltpu

F32 = jnp.float32
BF16 = jnp.bfloat16
EPS = 1e-6

ATTN_HEADS = 16
ATTN_KV_HEADS = 2
ATTN_HEAD_DIM = 64
ATTN_WIDTH = ATTN_HEADS * ATTN_HEAD_DIM
WINDOW = 128
BLOCK = 128
REL_BUCKETS = 32
REL_MAX_DIST = 128
SSM_GROUP_CH = 16
SSM_GROUPS = 32
SSM_STATE = 64
SSM_WIDTH = SSM_GROUPS * SSM_GROUP_CH
XATTN_HEADS = 4
XATTN_HEAD_DIM = 128
XATTN_WIDTH = XATTN_HEADS * XATTN_HEAD_DIM
N_EXPERT_GROUPS = 8
EXPERTS_PER_GROUP = 8
N_EXPERTS = N_EXPERT_GROUPS * EXPERTS_PER_GROUP
D_EXPERT = 512
MOE_BLOCK = 128

LANES = 128
SSM_GROUPS_PER_TILE = LANES // SSM_GROUP_CH
SSM_TILES = SSM_WIDTH // LANES
SSM_TILE_STATE = SSM_GROUPS_PER_TILE * SSM_STATE
SSM_STATES = SSM_GROUPS * SSM_STATE
CHUNK_BLOCKS = 4
META_ROWS = 8
M_OWNER, M_FIRST, M_SIZE, M_NCHUNK, M_LAST, M_NBLK, M_NUSED, M_OWNER_CHUNKS = range(8)
VMEM_LIMIT = 56 * 1024 * 1024

_NT = (((1,), (1,)), ((), ()))


def _cparams(*sem):
    return pltpu.CompilerParams(dimension_semantics=sem, vmem_limit_bytes=VMEM_LIMIT)


def _rms(x, gain):
    ms = jnp.mean(x * x, axis=-1, keepdims=True)
    return x * lax.rsqrt(ms + EPS) * gain


def _t5_bucket_table():
    qi = np.arange(BLOCK, dtype=np.int32)[:, None]
    ki = np.arange(2 * BLOCK, dtype=np.int32)[None, :]
    delta = BLOCK + qi - ki
    n = np.maximum(delta, 0)
    max_exact = REL_BUCKETS // 2
    nf = np.maximum(n, 1).astype(np.float32)
    large = max_exact + (np.log(nf / np.float32(max_exact)) / np.float32(math.log(REL_MAX_DIST / max_exact))
                         * np.float32(REL_BUCKETS - max_exact)).astype(np.int32)
    large = np.minimum(large, REL_BUCKETS - 1)
    return np.where(n < max_exact, n, large).astype(np.int32)


def _bias_kernel(rb_ref, bucket_ref, out_ref):
    h = pl.program_id(0)
    bucket = bucket_ref[...]
    acc = jnp.zeros(bucket.shape, F32)
    for b in range(REL_BUCKETS):
        acc = jnp.where(bucket == b, rb_ref[b, h], acc)
    qi = lax.broadcasted_iota(jnp.int32, bucket.shape, 0)
    ki = lax.broadcasted_iota(jnp.int32, bucket.shape, 1)
    delta = BLOCK + qi - ki
    valid = jnp.logical_and(delta >= 0, delta < WINDOW)
    out_ref[0] = jnp.where(valid, acc, jnp.float32(-1e30))


def _bias_table(rel_bias):
    bucket = jnp.asarray(_t5_bucket_table())
    return pl.pallas_call(
        _bias_kernel,
        grid=(ATTN_HEADS,),
        in_specs=[pl.BlockSpec(memory_space=pltpu.SMEM),
                  pl.BlockSpec((BLOCK, 2 * BLOCK), lambda h: (0, 0))],
        out_specs=pl.BlockSpec((1, BLOCK, 2 * BLOCK), lambda h: (h, 0, 0)),
        out_shape=jax.ShapeDtypeStruct((ATTN_HEADS, BLOCK, 2 * BLOCK), F32),
        compiler_params=_cparams("arbitrary"),
        name="t5_bias_table",
    )(rel_bias.astype(F32), bucket)


def _ssm_prep_kernel(lr_ref, li_ref, ldt_ref, br_ref, bi_ref, are_ref, aim_ref, bbr_ref, bbi_ref):
    lr = lr_ref[...]
    li = li_ref[...]
    dt = jnp.exp(ldt_ref[...])
    mag = jnp.exp(lr * dt)
    a_re = mag * jnp.cos(li * dt)
    a_im = mag * jnp.sin(li * dt)
    den = lr * lr + li * li
    nr = a_re - 1.0
    ni = a_im
    coef_re = (nr * lr + ni * li) / den
    coef_im = (ni * lr - nr * li) / den
    are_ref[...] = a_re
    aim_ref[...] = a_im
    br = br_ref[...]
    bi = bi_ref[...]
    bbr_ref[...] = coef_re * br - coef_im * bi
    bbi_ref[...] = coef_re * bi + coef_im * br


def _ssm_prep(lam_re, lam_im, log_dt, b_re, b_im):
    g, n, c = b_re.shape
    vec = jax.ShapeDtypeStruct((g, 1, n), F32)
    mat = jax.ShapeDtypeStruct((g, c, n), F32)
    return pl.pallas_call(
        _ssm_prep_kernel,
        out_shape=(vec, vec, mat, mat),
        name="ssm_discretise",
    )(lam_re.astype(F32).reshape(g, 1, n), lam_im.astype(F32).reshape(g, 1, n),
      log_dt.astype(F32).reshape(g, 1, 1),
      jnp.transpose(b_re.astype(F32), (0, 2, 1)), jnp.transpose(b_im.astype(F32), (0, 2, 1)))


def _block_diag_tiles(m):
    g, r, c = m.shape
    t = g // SSM_GROUPS_PER_TILE
    eye = jnp.eye(SSM_GROUPS_PER_TILE, dtype=m.dtype)
    m4 = m.reshape(t, SSM_GROUPS_PER_TILE, r, c)
    out = m4[:, :, :, None, :] * eye[None, :, None, :, None]
    return out.reshape(t, SSM_GROUPS_PER_TILE * r, SSM_GROUPS_PER_TILE * c)


def _inproj_kernel(x_ref, g_ref, w_ref, q_ref, k_ref, v_ref, u_ref, qx_ref):
    h = _rms(x_ref[...], g_ref[...]).astype(BF16)
    lo = lax.broadcasted_iota(jnp.int32, (1, LANES), 1) < ATTN_HEAD_DIM
    c0 = 0
    for ref in (q_ref, k_ref, v_ref, u_ref, qx_ref):
        if ref is k_ref or ref is v_ref:
            p = jnp.dot(h, w_ref[:, c0:c0 + LANES], preferred_element_type=F32)
            swapped = pltpu.roll(p, ATTN_HEAD_DIM, axis=1)
            ref[...] = jnp.concatenate([jnp.where(lo, p, swapped), jnp.where(lo, swapped, p)],
                                       axis=-1).astype(ref.dtype)
            c0 += LANES
        else:
            c1 = c0 + ref.shape[-1]
            ref[...] = jnp.dot(h, w_ref[:, c0:c1], preferred_element_type=F32).astype(ref.dtype)
            c0 = c1


def _inproj(x2, gain, w_cols, batch, seq, tm):
    t, d = x2.shape
    nsb = seq // tm
    kvw = 2 * LANES
    ncol = w_cols.shape[1]
    row = lambda w: pl.BlockSpec((tm, w), lambda i: (i, 0))
    return pl.pallas_call(
        _inproj_kernel,
        grid=(t // tm,),
        in_specs=[row(d),
                  pl.BlockSpec((1, d), lambda i: (0, 0)),
                  pl.BlockSpec((d, ncol), lambda i: (0, 0), pipeline_mode=pl.Buffered(1))],
        out_specs=[row(ATTN_WIDTH), row(kvw), row(kvw),
                   pl.BlockSpec((tm, SSM_WIDTH), lambda i: (i % nsb, i // nsb)),
                   row(XATTN_WIDTH)],
        out_shape=[jax.ShapeDtypeStruct((t, ATTN_WIDTH), BF16),
                   jax.ShapeDtypeStruct((t, kvw), BF16),
                   jax.ShapeDtypeStruct((t, kvw), BF16),
                   jax.ShapeDtypeStruct((seq, batch * SSM_WIDTH), BF16),
                   jax.ShapeDtypeStruct((t, XATTN_WIDTH), BF16)],
        compiler_params=_cparams("arbitrary"),
        name="in_proj",
    )(x2, gain, w_cols)


def _swa_kernel(sinks_ref, q_ref, kp_ref, kc_ref, vp_ref, vc_ref, bias_ref, qg_ref, kg_ref, og_ref, out_ref):
    n = pl.program_id(1)
    kcat = jnp.concatenate([kp_ref[...], kc_ref[...]], axis=0).astype(F32)
    vcat = jnp.concatenate([vp_ref[...], vc_ref[...]], axis=0)
    lane = lax.broadcasted_iota(jnp.int32, (1, LANES), 1)
    lo = lane < ATTN_HEAD_DIM
    key = lax.broadcasted_iota(jnp.int32, (1, 2 * BLOCK), 1)
    kmask = jnp.where(jnp.logical_and(n == 0, key < BLOCK), jnp.float32(-1e30), jnp.float32(0.0))
    qgain = qg_ref[...]
    pairs_per_kv = ATTN_HEADS // ATTN_KV_HEADS // 2
    outs = []
    for g in range(ATTN_KV_HEADS):
        kn = _rms(kcat[:, g * LANES:(g + 1) * LANES], kg_ref[...]).astype(BF16)
        vg = vcat[:, g * LANES:(g + 1) * LANES]
        for j in range(pairs_per_kv):
            tile = g * pairs_per_kv + j
            qp = q_ref[:, tile * LANES:(tile + 1) * LANES].astype(F32)
            sq = qp * qp
            s_lo = jnp.sum(jnp.where(lo, sq, 0.0), axis=-1, keepdims=True)
            s_hi = jnp.sum(sq, axis=-1, keepdims=True) - s_lo
            inv = jnp.where(lo, lax.rsqrt(s_lo / ATTN_HEAD_DIM + EPS), lax.rsqrt(s_hi / ATTN_HEAD_DIM + EPS))
            qn = qp * inv * qgain
            halves = []
            for half in range(2):
                hd = 2 * tile + half
                keep = lo if half == 0 else jnp.logical_not(lo)
                qh = jnp.where(keep, qn, 0.0).astype(BF16)
                l = lax.dot_general(qh, kn, _NT, preferred_element_type=F32)
                l = l + bias_ref[hd] + kmask
                sink = sinks_ref[hd]
                m = jnp.maximum(jnp.max(l, axis=-1, keepdims=True), sink)
                p = jnp.exp(l - m)
                den = jnp.sum(p, axis=-1, keepdims=True) + jnp.exp(sink - m)
                halves.append(jnp.dot(p.astype(BF16), vg, preferred_element_type=F32) / den)
            outs.append(jnp.where(lo, halves[0], halves[1]))
    y = jnp.concatenate(outs, axis=-1)
    out_ref[...] = _rms(y, og_ref[...]).astype(out_ref.dtype)


def _swa(q, kk, vv, bias, sinks, qgain, kgain, ogain, batch, seq):
    nb = seq // BLOCK
    kvw = 2 * LANES
    cur = lambda b, n: (b * nb + n, 0)
    prev = lambda b, n: (b * nb + jnp.maximum(n - 1, 0), 0)
    const2 = lambda b, n: (0, 0)
    return pl.pallas_call(
        _swa_kernel,
        grid=(batch, nb),
        in_specs=[pl.BlockSpec(memory_space=pltpu.SMEM),
                  pl.BlockSpec((BLOCK, ATTN_WIDTH), cur),
                  pl.BlockSpec((BLOCK, kvw), prev), pl.BlockSpec((BLOCK, kvw), cur),
                  pl.BlockSpec((BLOCK, kvw), prev), pl.BlockSpec((BLOCK, kvw), cur),
                  pl.BlockSpec((ATTN_HEADS, BLOCK, 2 * BLOCK), lambda b, n: (0, 0, 0)),
                  pl.BlockSpec((1, LANES), const2), pl.BlockSpec((1, LANES), const2),
                  pl.BlockSpec((1, ATTN_WIDTH), const2)],
        out_specs=pl.BlockSpec((BLOCK, ATTN_WIDTH), cur),
        out_shape=jax.ShapeDtypeStruct((batch * seq, ATTN_WIDTH), BF16),
        compiler_params=_cparams("arbitrary", "arbitrary"),
        name="swa_attention",
    )(sinks, q, kk, kk, vv, vv, bias, qgain, kgain, ogain)


def _memkv_kernel(m_ref, g_ref, w_ref, kg_ref, k_ref, v_ref):
    h = _rms(m_ref[...], g_ref[...]).astype(BF16)
    km = jnp.dot(h, w_ref[:, :XATTN_WIDTH], preferred_element_type=F32)
    for hd in range(XATTN_HEADS):
        sl = slice(hd * XATTN_HEAD_DIM, (hd + 1) * XATTN_HEAD_DIM)
        k_ref[:, sl] = _rms(km[:, sl], kg_ref[...]).astype(k_ref.dtype)
    v_ref[...] = jnp.dot(h, w_ref[:, XATTN_WIDTH:], preferred_element_type=F32).astype(v_ref.dtype)


def _memkv(mem2, gain, w_kv, kgain, tm):
    r, d = mem2.shape
    row = lambda w: pl.BlockSpec((tm, w), lambda i: (i, 0))
    const = lambda shape: pl.BlockSpec(shape, lambda i: (0, 0))
    return pl.pallas_call(
        _memkv_kernel,
        grid=(r // tm,),
        in_specs=[row(d), const((1, d)), const((d, 2 * XATTN_WIDTH)), const((1, XATTN_HEAD_DIM))],
        out_specs=[row(XATTN_WIDTH), row(XATTN_WIDTH)],
        out_shape=[jax.ShapeDtypeStruct((r, XATTN_WIDTH), BF16)] * 2,
        compiler_params=_cparams("arbitrary"),
        name="mem_kv_proj",
    )(mem2, gain, w_kv, kgain)


def _xattn_kernel(q_ref, k_ref, v_ref, qg_ref, og_ref, out_ref):
    outs = []
    for hd in range(XATTN_HEADS):
        sl = slice(hd * XATTN_HEAD_DIM, (hd + 1) * XATTN_HEAD_DIM)
        qn = _rms(q_ref[:, sl].astype(F32), qg_ref[...]).astype(BF16)
        l = lax.dot_general(qn, k_ref[:, sl], _NT, preferred_element_type=F32)
        m = jnp.max(l, axis=-1, keepdims=True)
        p = jnp.exp(l - m)
        den = jnp.sum(p, axis=-1, keepdims=True)
        outs.append(jnp.dot(p.astype(BF16), v_ref[:, sl], preferred_element_type=F32) / den)
    y = jnp.concatenate(outs, axis=-1)
    out_ref[...] = _rms(y, og_ref[...]).astype(out_ref.dtype)


def _xattn(qx, km, vm, qgain, ogain, batch, seq, mem_len, tq):
    nq = seq // tq
    const2 = lambda b, i: (0, 0)
    return pl.pallas_call(
        _xattn_kernel,
        grid=(batch, nq),
        in_specs=[pl.BlockSpec((tq, XATTN_WIDTH), lambda b, i: (b * nq + i, 0)),
                  pl.BlockSpec((mem_len, XATTN_WIDTH), lambda b, i: (b, 0)),
                  pl.BlockSpec((mem_len, XATTN_WIDTH), lambda b, i: (b, 0)),
                  pl.BlockSpec((1, XATTN_HEAD_DIM), const2),
                  pl.BlockSpec((1, XATTN_WIDTH), const2)],
        out_specs=pl.BlockSpec((tq, XATTN_WIDTH), lambda b, i: (b * nq + i, 0)),
        out_shape=jax.ShapeDtypeStruct((batch * seq, XATTN_WIDTH), BF16),
        compiler_params=_cparams("arbitrary", "arbitrary"),
        name="mem_xattn",
    )(qx, km, vm, qgain, ogain)


def _ssm_kernel(u_ref, bmat_ref, cmat_ref, are_ref, aim_ref, d_ref, wglu_ref, og_ref, out_ref,
                tb_ref, xr_ref, xi_ref, sr_ref, si_ref, *, batch, steps):
    @pl.when(pl.program_id(0) == 0)
    def _():
        sr_ref[...] = jnp.zeros_like(sr_ref)
        si_ref[...] = jnp.zeros_like(si_ref)

    for b in range(batch):
        for j in range(SSM_TILES):
            c0 = b * SSM_WIDTH + j * LANES
            tb_ref[j, pl.ds(b, steps, stride=batch), :] = u_ref[:, c0:c0 + LANES].astype(F32)
    uf = jnp.concatenate([tb_ref[j] for j in range(SSM_TILES)], axis=-1)
    u = uf.astype(BF16)
    for j in range(SSM_TILES):
        bu = jnp.dot(u[:, j * LANES:(j + 1) * LANES], bmat_ref[j], preferred_element_type=F32)
        xr_ref[:, j * SSM_TILE_STATE:(j + 1) * SSM_TILE_STATE] = bu[:, :SSM_TILE_STATE]
        xi_ref[:, j * SSM_TILE_STATE:(j + 1) * SSM_TILE_STATE] = bu[:, SSM_TILE_STATE:]

    chunk = 1024
    for c0 in range(0, SSM_STATES, chunk):
        cs = slice(c0, c0 + chunk)
        ar = jnp.broadcast_to(are_ref[:, cs], (batch, chunk))
        ai = jnp.broadcast_to(aim_ref[:, cs], (batch, chunk))

        def step(t, carry, cs=cs, ar=ar, ai=ai):
            s_r, s_i = carry
            rows = pl.ds(pl.multiple_of(t * batch, batch), batch)
            n_r = ar * s_r - ai * s_i + xr_ref[rows, cs]
            n_i = ar * s_i + ai * s_r + xi_ref[rows, cs]
            xr_ref[rows, cs] = n_r
            xi_ref[rows, cs] = n_i
            return n_r, n_i

        s_r, s_i = lax.fori_loop(0, steps, step, (sr_ref[:, cs], si_ref[:, cs]))
        sr_ref[:, cs] = s_r
        si_ref[:, cs] = s_i

    ys = []
    for j in range(SSM_TILES):
        sl = slice(j * SSM_TILE_STATE, (j + 1) * SSM_TILE_STATE)
        xcat = jnp.concatenate([xr_ref[:, sl], xi_ref[:, sl]], axis=-1).astype(BF16)
        ys.append(jnp.dot(xcat, cmat_ref[j], preferred_element_type=F32))
    y = jnp.concatenate(ys, axis=-1) + d_ref[...] * uf
    y = jax.nn.gelu(y)
    y = y * jax.nn.sigmoid(jnp.dot(y.astype(BF16), wglu_ref[...], preferred_element_type=F32))
    y = _rms(y, og_ref[...])
    for j in range(SSM_TILES):
        tb_ref[j] = y[:, j * LANES:(j + 1) * LANES]
    for b in range(batch):
        for j in range(SSM_TILES):
            c0 = b * SSM_WIDTH + j * LANES
            out_ref[:, c0:c0 + LANES] = tb_ref[j, pl.ds(b, steps, stride=batch), :].astype(out_ref.dtype)


def _ssm(u_sb, bmat, cmat, a_re, a_im, d_skip, w_glu, ogain, batch, seq, steps):
    rows = steps * batch
    const2 = lambda c: (0, 0)
    const3 = lambda c: (0, 0, 0)
    return pl.pallas_call(
        functools.partial(_ssm_kernel, batch=batch, steps=steps),
        grid=(seq // steps,),
        in_specs=[pl.BlockSpec((steps, batch * SSM_WIDTH), lambda c: (c, 0)),
                  pl.BlockSpec(bmat.shape, const3), pl.BlockSpec(cmat.shape, const3),
                  pl.BlockSpec((1, SSM_STATES), const2), pl.BlockSpec((1, SSM_STATES), const2),
                  pl.BlockSpec((1, SSM_WIDTH), const2),
                  pl.BlockSpec((SSM_WIDTH, SSM_WIDTH), const2),
                  pl.BlockSpec((1, SSM_WIDTH), const2)],
        out_specs=pl.BlockSpec((steps, batch * SSM_WIDTH), lambda c: (c, 0)),
        out_shape=jax.ShapeDtypeStruct((seq, batch * SSM_WIDTH), BF16),
        scratch_shapes=[pltpu.VMEM((SSM_TILES, rows, LANES), F32),
                        pltpu.VMEM((rows, SSM_STATES), F32), pltpu.VMEM((rows, SSM_STATES), F32),
                        pltpu.VMEM((batch, SSM_STATES), F32), pltpu.VMEM((batch, SSM_STATES), F32)],
        compiler_params=_cparams("arbitrary"),
        name="s5_layer",
    )(u_sb, bmat, cmat, a_re, a_im, d_skip, w_glu, ogain)


def _outproj_kernel(ya_ref, ys_ref, yx_ref, x_ref, wo_ref, g_ref, wr_ref, x1_ref, lt_ref, hp_ref, *, sub):
    for r0 in range(0, x_ref.shape[0], sub):
        rows = slice(r0, r0 + sub)
        mix = jnp.concatenate([ya_ref[rows, :], ys_ref[rows, :], yx_ref[rows, :]], axis=-1)
        x1 = x_ref[rows, :] + jnp.dot(mix, wo_ref[...], preferred_element_type=F32)
        x1_ref[rows, :] = x1
        h2 = _rms(x1, g_ref[...])
        hi = h2.astype(BF16)
        hi32 = hi.astype(F32)
        lo = (h2 - hi32).astype(BF16)
        both = jnp.dot(hi, wr_ref[...], preferred_element_type=F32)
        lg = both[:, :LANES] + both[:, LANES:] + jnp.dot(lo, wr_ref[:, :LANES], preferred_element_type=F32)
        lt_ref[rows, :] = lg
        hp_ref[rows, :] = _pack_rows(hi32)


def _outproj(ya, ys_sb, yx, x2, w_o, gain, wr, batch, seq, tm):
    t, d = x2.shape
    nsb = seq // tm
    row = lambda w: pl.BlockSpec((tm, w), lambda i: (i, 0))
    const = lambda shape: pl.BlockSpec(shape, lambda i: (0, 0), pipeline_mode=pl.Buffered(1))
    return pl.pallas_call(
        functools.partial(_outproj_kernel, sub=min(256, tm)),
        grid=(t // tm,),
        in_specs=[row(ATTN_WIDTH),
                  pl.BlockSpec((tm, SSM_WIDTH), lambda i: (i % nsb, i // nsb)),
                  row(XATTN_WIDTH), row(d),
                  const(w_o.shape), const((1, d)), const(wr.shape)],
        out_specs=[row(d), row(LANES), row(d // 2)],
        out_shape=[jax.ShapeDtypeStruct((t, d), F32), jax.ShapeDtypeStruct((t, LANES), F32),
                   jax.ShapeDtypeStruct((t, d // 2), jnp.uint32)],
        compiler_params=_cparams("arbitrary"),
        name="out_proj_router",
    )(ya, ys_sb, yx, x2, w_o, gain, wr)


def _route_kernel(lt_ref, tri_ref, dest_ref, w_ref, meta_ref, cnt_ref, carry_ref, pstart_ref):
    phase = pl.program_id(0)
    c = pl.program_id(1)
    logits = lt_ref[...].T
    tc = logits.shape[1]
    ng, epg = N_EXPERT_GROUPS, EXPERTS_PER_GROUP
    row8 = lax.broadcasted_iota(jnp.int32, (ng, tc), 0)

    gl = logits[0:ng]
    gmax = jnp.max(gl, axis=0, keepdims=True)
    gidx = jnp.min(jnp.where(gl == gmax, row8, ng), axis=0, keepdims=True)
    gate = 1.0 / jnp.sum(jnp.exp(gl - gmax), axis=0, keepdims=True)
    sel = jnp.zeros((epg, tc), F32)
    for g in range(ng):
        sel = jnp.where(gidx == g, logits[ng + g * epg:ng + (g + 1) * epg], sel)
    v1 = jnp.max(sel, axis=0, keepdims=True)
    i1 = jnp.min(jnp.where(sel == v1, row8, epg), axis=0, keepdims=True)
    sel2 = jnp.where(row8 == i1, -jnp.inf, sel)
    v2 = jnp.max(sel2, axis=0, keepdims=True)
    i2 = jnp.min(jnp.where(sel2 == v2, row8, epg), axis=0, keepdims=True)
    e = jnp.exp(v2 - v1)
    w1 = gate * (1.0 / (1.0 + e))
    w2 = gate * (e / (1.0 + e))
    e1 = gidx * epg + i1
    e2 = gidx * epg + i2
    rowe = lax.broadcasted_iota(jnp.int32, (N_EXPERTS, tc), 0)
    oh1 = rowe == e1
    oh2 = rowe == e2
    member = jnp.where(jnp.logical_or(oh1, oh2), 1.0, 0.0)
    chunk_cnt = jnp.sum(member, axis=1, keepdims=True)

    @pl.when(phase == 0)
    def _():
        @pl.when(c == 0)
        def _():
            cnt_ref[...] = jnp.zeros_like(cnt_ref)
        cnt_ref[...] += chunk_cnt

    @pl.when(phase == 1)
    def _():
        @pl.when(c == 0)
        def _():
            cnt = cnt_ref[...]
            nblk = jnp.floor((cnt + (MOE_BLOCK - 1)) * (1.0 / MOE_BLOCK))
            nchunk = jnp.floor((nblk + (CHUNK_BLOCKS - 1)) * (1.0 / CHUNK_BLOCKS))
            r = lax.broadcasted_iota(jnp.int32, (N_EXPERTS, LANES), 0)
            cidx = lax.broadcasted_iota(jnp.int32, (N_EXPERTS, LANES), 1)
            to_row = lambda col: jnp.sum(jnp.where(r == cidx, col, 0.0), axis=0, keepdims=True)
            cumsum_col = lambda col: jnp.sum(jnp.where(cidx <= r, to_row(col), 0.0), axis=1, keepdims=True)
            cumsum_row = lambda col: jnp.sum(jnp.where(r <= cidx, col, 0.0), axis=0, keepdims=True)
            bend = cumsum_col(nblk)
            bstart = bend - nblk
            cend = cumsum_col(nchunk)
            cstart = cend - nchunk
            pstart_ref[...] = bstart * MOE_BLOCK
            carry_ref[...] = jnp.zeros_like(carry_ref)
            lanef = lax.broadcasted_iota(jnp.int32, (1, LANES), 1).astype(F32)
            owner = jnp.minimum(jnp.sum(jnp.where(cend <= lanef, 1.0, 0.0), axis=0, keepdims=True),
                                N_EXPERTS - 1.0)
            own = r.astype(F32) == owner
            pick = lambda col: jnp.sum(jnp.where(own, col, 0.0), axis=0, keepdims=True)
            idx = lanef - pick(cstart)
            first = pick(bstart) + CHUNK_BLOCKS * idx
            size = jnp.clip(pick(nblk) - CHUNK_BLOCKS * idx, 0.0, float(CHUNK_BLOCKS))
            zero = jnp.zeros((1, LANES), F32)
            rows = [owner, first, size,
                    zero + jnp.sum(nchunk, axis=0, keepdims=True),
                    cumsum_row(nblk) - 1.0,
                    to_row(nblk),
                    zero + jnp.sum(nblk, axis=0, keepdims=True),
                    pick(nchunk)]
            for k, v in enumerate(rows):
                meta_ref[k:k + 1, :] = v.astype(jnp.int32)

        before = carry_ref[...] + jnp.dot(member.astype(BF16), tri_ref[...], preferred_element_type=F32)
        pos = before + pstart_ref[...]
        dest_ref[0:1, :] = jnp.sum(jnp.where(oh1, pos, 0.0), axis=0, keepdims=True).astype(jnp.int32)
        dest_ref[1:2, :] = jnp.sum(jnp.where(oh2, pos, 0.0), axis=0, keepdims=True).astype(jnp.int32)
        w_ref[0:1, :] = w1
        w_ref[1:2, :] = w2
        carry_ref[...] += chunk_cnt


def _route(logits_t, tc):
    t = logits_t.shape[0]
    nc = t // tc
    tri = jnp.asarray(np.triu(np.ones((tc, tc), np.float32), k=1), dtype=BF16)
    return pl.pallas_call(
        _route_kernel,
        grid=(2, nc),
        in_specs=[pl.BlockSpec((tc, LANES), lambda p, c: (c, 0)),
                  pl.BlockSpec((tc, tc), lambda p, c: (0, 0))],
        out_specs=[pl.BlockSpec((2, tc), lambda p, c: (0, c * p)),
                   pl.BlockSpec((2, tc), lambda p, c: (0, c * p)),
                   pl.BlockSpec((META_ROWS, LANES), lambda p, c: (0, 0))],
        out_shape=[jax.ShapeDtypeStruct((2, t), jnp.int32), jax.ShapeDtypeStruct((2, t), F32),
                   jax.ShapeDtypeStruct((META_ROWS, LANES), jnp.int32)],
        scratch_shapes=[pltpu.VMEM((N_EXPERTS, 1), F32)] * 3,
        compiler_params=_cparams("arbitrary", "arbitrary"),
        name="moe_route",
    )(logits_t, tri)


def _meta(meta_ref, row, lane=0):
    return meta_ref[row * LANES + lane]


def _fill_blocks(meta_ref, zbuf, dst_ref, sem, n_blocks, *, expert_tails):
    zbuf[...] = jnp.zeros_like(zbuf)
    n_used = _meta(meta_ref, M_NUSED)
    block = lambda b: pltpu.make_async_copy(zbuf, dst_ref.at[pl.ds(b * MOE_BLOCK, MOE_BLOCK), :], sem)

    def tails(fn):
        def body(e, carry):
            @pl.when(_meta(meta_ref, M_NBLK, e) > 0)
            def _():
                fn(block(_meta(meta_ref, M_LAST, e)))
            return carry
        lax.fori_loop(0, N_EXPERTS, body, 0)

    def unused(fn):
        def body(b, carry):
            fn(block(b))
            return carry
        lax.fori_loop(n_used, n_blocks, body, 0)

    for phase in (lambda cp: cp.start(), lambda cp: cp.wait()):
        if expert_tails:
            tails(phase)
        unused(phase)


def _dispatch_kernel(dest_ref, meta_ref, h_ref, xs_ref, zbuf, sem_z, sem, *, tokens, n_blocks):
    tm = h_ref.shape[0]

    @pl.when(pl.program_id(0) == 0)
    def _():
        _fill_blocks(meta_ref, zbuf, xs_ref, sem_z, n_blocks, expert_tails=True)

    base = pl.program_id(0) * tm

    def issue(r, carry):
        for k in range(2):
            d = dest_ref[k * tokens + base + r]
            pltpu.make_async_copy(h_ref.at[pl.ds(r, 1), :], xs_ref.at[pl.ds(d, 1), :], sem).start()
        return carry
    lax.fori_loop(0, tm, issue, 0, unroll=8)
    for k in range(2):
        pltpu.make_async_copy(h_ref, xs_ref.at[pl.ds(0, tm), :], sem).wait()


def _dispatch(dest_flat, meta_flat, h2p, n_blocks, tm):
    t, w = h2p.shape
    grid_spec = pltpu.PrefetchScalarGridSpec(
        num_scalar_prefetch=2,
        grid=(t // tm,),
        in_specs=[pl.BlockSpec((tm, w), lambda i, d, m: (i, 0))],
        out_specs=pl.BlockSpec(memory_space=pl.ANY),
        scratch_shapes=[pltpu.VMEM((MOE_BLOCK, w), h2p.dtype),
                        pltpu.SemaphoreType.DMA(()), pltpu.SemaphoreType.DMA(())],
    )
    return pl.pallas_call(
        functools.partial(_dispatch_kernel, tokens=t, n_blocks=n_blocks),
        grid_spec=grid_spec,
        out_shape=jax.ShapeDtypeStruct((n_blocks * MOE_BLOCK, w), h2p.dtype),
        compiler_params=_cparams("arbitrary"),
        name="moe_dispatch",
    )(dest_flat, meta_flat, h2p)


def _pack_rows(x):
    bits = lax.bitcast_convert_type(x, jnp.uint32)
    half = x.shape[1] // 2
    return (bits[:, half:] & jnp.uint32(0xFFFF0000)) | (bits[:, :half] >> 16)


def _unpack_halves(words):
    lo = lax.bitcast_convert_type(words << 16, F32)
    hi = lax.bitcast_convert_type(words & jnp.uint32(0xFFFF0000), F32)
    return lo, hi


def _unpack_rows(words):
    return jnp.concatenate(_unpack_halves(words), axis=-1).astype(BF16)


def _expert_kernel(meta_ref, xs_ref, wg_ref, wu_ref, wd_ref, yb_ref,
                   xbuf, ybuf, zbuf, wg_f32, wu_f32, wd_f32, wg_bf, wu_bf, wd_bf, wslot_ref,
                   sem_in, sem_out, sem_z, sem_w, *, n_blocks):
    c = pl.program_id(0)
    n_chunks = _meta(meta_ref, M_NCHUNK)
    slot = c % 2

    def weight_copies(k, s):
        e = _meta(meta_ref, M_OWNER, k)
        return [pltpu.make_async_copy(src.at[e], dst.at[s], sem_w.at[s])
                for src, dst in ((wg_ref, wg_f32), (wu_ref, wu_f32), (wd_ref, wd_f32))]

    def in_copy(k, s, nb):
        rows = nb * MOE_BLOCK
        src = xs_ref.at[pl.ds(_meta(meta_ref, M_FIRST, k) * MOE_BLOCK, rows), :]
        return pltpu.make_async_copy(src, xbuf.at[s, pl.ds(0, rows), :], sem_in.at[s])

    def out_copy(k, s, nb):
        rows = nb * MOE_BLOCK
        dst = yb_ref.at[pl.ds(_meta(meta_ref, M_FIRST, k) * MOE_BLOCK, rows), :]
        return pltpu.make_async_copy(ybuf.at[s, pl.ds(0, rows), :], dst, sem_out.at[s])

    def by_size(k, fn):
        for nb in range(1, CHUNK_BLOCKS + 1):
            pl.when(_meta(meta_ref, M_SIZE, k) == nb)(functools.partial(fn, nb))

    @pl.when(c == 0)
    def _():
        by_size(0, lambda nb: in_copy(0, 0, nb).start())

    @pl.when(c + 1 < n_chunks)
    def _():
        by_size(c + 1, lambda nb: in_copy(c + 1, 1 - slot, nb).start())

    @pl.when(c < n_chunks)
    def _():
        prev = jnp.maximum(c - 1, 0)
        new_expert = jnp.logical_or(c == 0, _meta(meta_ref, M_OWNER, c) != _meta(meta_ref, M_OWNER, prev))

        @pl.when(c == 0)
        def _():
            wslot_ref[0] = 0
            for cp in weight_copies(0, 0):
                cp.start()

        @pl.when(new_expert)
        def _():
            ws = jnp.where(c == 0, 0, 1 - wslot_ref[0])
            wslot_ref[0] = ws
            nxt = c + _meta(meta_ref, M_OWNER_CHUNKS, c)

            @pl.when(nxt < n_chunks)
            def _():
                for cp in weight_copies(nxt, 1 - ws):
                    cp.start()
            for cp in weight_copies(c, ws):
                cp.wait()
            wg_bf[...] = wg_f32[ws].astype(BF16)
            wu_bf[...] = wu_f32[ws].astype(BF16)
            wd_bf[...] = wd_f32[ws].astype(BF16)

        @pl.when(c >= 2)
        def _():
            by_size(c - 2, lambda nb: out_copy(c - 2, slot, nb).wait())

        def compute(nb):
            rows = nb * MOE_BLOCK
            in_copy(c, slot, nb).wait()
            h = _unpack_rows(xbuf[slot, 0:rows, :])
            gate = jnp.dot(h, wg_bf[...], preferred_element_type=F32)
            up = jnp.dot(h, wu_bf[...], preferred_element_type=F32)
            act = (jax.nn.silu(gate) * up).astype(BF16)
            y = jnp.dot(act, wd_bf[...], preferred_element_type=F32)
            ybuf[slot, 0:rows, :] = _pack_rows(y.astype(BF16).astype(F32))
            out_copy(c, slot, nb).start()
        by_size(c, compute)

    @pl.when(c == pl.num_programs(0) - 1)
    def _():
        _fill_blocks(meta_ref, zbuf, yb_ref, sem_z, n_blocks, expert_tails=False)
        for back in (2, 1):
            @pl.when(n_chunks >= back)
            def _(back=back):
                k = n_chunks - back
                by_size(k, lambda nb: out_copy(k, k % 2, nb).wait())


def _experts(meta_flat, xs, w_gate, w_up, w_down, n_blocks, n_chunks_max):
    d, de = w_gate.shape[1], w_gate.shape[2]
    rows = CHUNK_BLOCKS * MOE_BLOCK

    hbm = pl.BlockSpec(memory_space=pl.ANY)
    grid_spec = pltpu.PrefetchScalarGridSpec(
        num_scalar_prefetch=1,
        grid=(n_chunks_max,),
        in_specs=[hbm, hbm, hbm, hbm],
        out_specs=hbm,
        scratch_shapes=[pltpu.VMEM((2, rows, xs.shape[1]), xs.dtype),
                        pltpu.VMEM((2, rows, d // 2), jnp.uint32),
                        pltpu.VMEM((MOE_BLOCK, d // 2), jnp.uint32),
                        pltpu.VMEM((2, d, de), F32), pltpu.VMEM((2, d, de), F32), pltpu.VMEM((2, de, d), F32),
                        pltpu.VMEM((d, de), BF16), pltpu.VMEM((d, de), BF16), pltpu.VMEM((de, d), BF16),
                        pltpu.SMEM((1,), jnp.int32),
                        pltpu.SemaphoreType.DMA((2,)), pltpu.SemaphoreType.DMA((2,)),
                        pltpu.SemaphoreType.DMA(()), pltpu.SemaphoreType.DMA((2,))],
    )
    return pl.pallas_call(
        functools.partial(_expert_kernel, n_blocks=n_blocks),
        grid_spec=grid_spec,
        out_shape=jax.ShapeDtypeStruct((n_blocks * MOE_BLOCK, d // 2), jnp.uint32),
        compiler_params=_cparams("arbitrary"),
        name="moe_experts",
    )(meta_flat, xs, w_gate, w_up, w_down)


def _combine_kernel(dest_ref, yb_ref, x1_ref, w_ref, out_ref, gbuf, sem, *, tokens):
    i = pl.program_id(0)
    tm = x1_ref.shape[0]
    slot = i % 2

    def gather(tile, s):
        base = tile * tm

        def issue(r, carry):
            for k in range(2):
                d = dest_ref[k * tokens + base + r]
                pltpu.make_async_copy(yb_ref.at[pl.ds(d, 1), :], gbuf.at[s, k, pl.ds(r, 1), :], sem.at[s]).start()
            return carry
        lax.fori_loop(0, tm, issue, 0, unroll=8)

    @pl.when(i == 0)
    def _():
        gather(0, 0)

    @pl.when(i + 1 < pl.num_programs(0))
    def _():
        gather(i + 1, 1 - slot)

    for k in range(2):
        pltpu.make_async_copy(yb_ref.at[pl.ds(0, tm), :], gbuf.at[slot, k], sem.at[slot]).wait()
    w = w_ref[...]
    half = x1_ref.shape[1] // 2
    lo0, hi0 = _unpack_halves(gbuf[slot, 0])
    lo1, hi1 = _unpack_halves(gbuf[slot, 1])
    out_ref[:, :half] = x1_ref[:, :half] + (lo0 * w[:, 0:1] + lo1 * w[:, 1:2])
    out_ref[:, half:] = x1_ref[:, half:] + (hi0 * w[:, 0:1] + hi1 * w[:, 1:2])


def _combine(dest_flat, yb, x1, w_tok, tm):
    t, d = x1.shape
    grid_spec = pltpu.PrefetchScalarGridSpec(
        num_scalar_prefetch=1,
        grid=(t // tm,),
        in_specs=[pl.BlockSpec(memory_space=pl.ANY),
                  pl.BlockSpec((tm, d), lambda i, dr: (i, 0)),
                  pl.BlockSpec((tm, 2), lambda i, dr: (i, 0))],
        out_specs=pl.BlockSpec((tm, d), lambda i, dr: (i, 0)),
        scratch_shapes=[pltpu.VMEM((2, 2, tm, d // 2), jnp.uint32), pltpu.SemaphoreType.DMA((2,))],
    )
    return pl.pallas_call(
        functools.partial(_combine_kernel, tokens=t),
        grid_spec=grid_spec,
        out_shape=jax.ShapeDtypeStruct((t, d), F32),
        compiler_params=_cparams("arbitrary"),
        name="moe_combine",
    )(dest_flat, yb, x1, w_tok)


def _row(v):
    return v.astype(F32).reshape(1, -1)


def _layer(x2, mem2, batch, seq, mem_len, p):
    t, d = x2.shape

    tm_in = min(512, seq)
    q, kk, vv, u_sb, qx = _inproj(x2, _row(p["norm_mix"]), p["w_in"].astype(BF16), batch, seq, tm_in)

    bias = _bias_table(p["rel_bias"])
    qgain = jnp.tile(_row(p["q_norm"]), (1, 2)) * (1.0 / math.sqrt(ATTN_HEAD_DIM))
    kgain = jnp.tile(_row(p["k_norm"]), (1, 2))
    ya = _swa(q, kk, vv, bias, p["attn_sinks"].astype(F32), qgain, kgain, _row(p["out_norm_attn"]), batch, seq)

    km, vm = _memkv(mem2, _row(p["mem_norm"]), p["w_mem_kv"].astype(BF16), _row(p["xk_norm"]),
                    min(256, mem2.shape[0]))
    xq_gain = _row(p["xq_norm"]) * (1.0 / math.sqrt(XATTN_HEAD_DIM))
    yx = _xattn(qx, km, vm, xq_gain, _row(p["out_norm_xattn"]), batch, seq, mem_len, min(512, seq))

    a_re, a_im, bbr, bbi = _ssm_prep(p["ssm_lambda_re"], p["ssm_lambda_im"], p["ssm_log_dt"],
                                     p["ssm_b_re"], p["ssm_b_im"])
    bmat = jnp.concatenate([_block_diag_tiles(bbr), _block_diag_tiles(bbi)], axis=-1).astype(BF16)
    c_re_t = jnp.transpose(p["ssm_c_re"].astype(F32), (0, 2, 1))
    c_im_t = jnp.transpose(p["ssm_c_im"].astype(F32), (0, 2, 1))
    cmat = jnp.concatenate([_block_diag_tiles(c_re_t), _block_diag_tiles(-c_im_t)], axis=1).astype(BF16)
    steps = min(64, seq)
    ys_sb = _ssm(u_sb, bmat, cmat,
                 a_re.reshape(1, SSM_STATES), a_im.reshape(1, SSM_STATES), _row(p["ssm_d"]),
                 p["ssm_w_glu"].astype(BF16), _row(p["out_norm_ssm"]), batch, seq, steps)

    wr = jnp.concatenate([p["w_router_group"], p["w_router_expert"]], axis=1).astype(F32)
    wr = jnp.pad(wr, ((0, 0), (0, LANES - wr.shape[1])))
    wr_hi = wr.astype(BF16)
    wr_lo = (wr - wr_hi.astype(F32)).astype(BF16)
    tm_out = min(512, seq)
    x1, logits_t, h2p = _outproj(ya, ys_sb, yx, x2, p["w_o"].astype(BF16), _row(p["norm_ffn"]),
                                 jnp.concatenate([wr_hi, wr_lo], axis=1), batch, seq, tm_out)

    dest, w_k, meta = _route(logits_t, min(1024, t))
    n_blocks = (2 * t) // MOE_BLOCK + N_EXPERTS
    n_chunks_max = (n_blocks + (CHUNK_BLOCKS - 1) * N_EXPERTS) // CHUNK_BLOCKS
    dest_flat = dest.reshape(2 * t)
    meta_flat = meta.reshape(META_ROWS * LANES)
    xs = _dispatch(dest_flat, meta_flat, h2p, n_blocks, min(1024, t))
    yb = _experts(meta_flat, xs, p["w_gate"], p["w_up"], p["w_down"], n_blocks, n_chunks_max)
    return _combine(dest_flat, yb, x1, w_k.T, min(256, t))


def kernel(x, mem, norm_mix, w_in, q_norm, k_norm, attn_sinks, rel_bias, ssm_lambda_re, ssm_lambda_im, ssm_log_dt, ssm_b_re, ssm_b_im, ssm_c_re, ssm_c_im, ssm_d, ssm_w_glu, mem_norm, w_mem_kv, xq_norm, xk_norm, out_norm_attn, out_norm_ssm, out_norm_xattn, w_o, norm_ffn, w_router_group, w_router_expert, w_gate, w_up, w_down):
    batch, seq, d = x.shape
    mem_len = mem.shape[1]
    per_layer = dict(norm_mix=norm_mix, w_in=w_in, q_norm=q_norm, k_norm=k_norm, attn_sinks=attn_sinks,
                     ssm_lambda_re=ssm_lambda_re, ssm_lambda_im=ssm_lambda_im, ssm_log_dt=ssm_log_dt,
                     ssm_b_re=ssm_b_re, ssm_b_im=ssm_b_im, ssm_c_re=ssm_c_re, ssm_c_im=ssm_c_im,
                     ssm_d=ssm_d, ssm_w_glu=ssm_w_glu, mem_norm=mem_norm, w_mem_kv=w_mem_kv,
                     xq_norm=xq_norm, xk_norm=xk_norm, out_norm_attn=out_norm_attn,
                     out_norm_ssm=out_norm_ssm, out_norm_xattn=out_norm_xattn, w_o=w_o, norm_ffn=norm_ffn,
                     w_router_group=w_router_group, w_router_expert=w_router_expert,
                     w_gate=w_gate, w_up=w_up, w_down=w_down)
    x2 = x.astype(F32).reshape(batch * seq, d)
    mem2 = mem.astype(F32).reshape(batch * mem_len, d)
    for l in range(norm_mix.shape[0]):
        p = {k: v[l] for k, v in per_layer.items()}
        p["rel_bias"] = rel_bias
        x2 = _layer(x2, mem2, batch, seq, mem_len, p)
    return x2.reshape(batch, seq, d).astype(x.dtype)
```

```python
import functools
import math

import numpy as np
import jax
import jax.numpy as jnp
from jax import lax
from jax.experimental import pallas as pl
from jax.experimental.pallas import tpu as pltpu

F32 = jnp.float32
BF16 = jnp.bfloat16
EPS = 1e-6

ATTN_HEADS = 16
ATTN_KV_HEADS = 2
ATTN_HEAD_DIM = 64
ATTN_WIDTH = ATTN_HEADS * ATTN_HEAD_DIM
WINDOW = 128
BLOCK = 128
REL_BUCKETS = 32
REL_MAX_DIST = 128
SSM_GROUP_CH = 16
SSM_GROUPS = 32
SSM_STATE = 64
SSM_WIDTH = SSM_GROUPS * SSM_GROUP_CH
XATTN_HEADS = 4
XATTN_HEAD_DIM = 128
XATTN_WIDTH = XATTN_HEADS * XATTN_HEAD_DIM
N_EXPERT_GROUPS = 8
EXPERTS_PER_GROUP = 8
N_EXPERTS = N_EXPERT_GROUPS * EXPERTS_PER_GROUP
D_EXPERT = 512
MOE_BLOCK = 128

LANES = 128
SUBLANES = 8
SSM_GROUPS_PER_TILE = LANES // SSM_GROUP_CH
SSM_TILES = SSM_WIDTH // LANES
SSM_TILE_STATE = SSM_GROUPS_PER_TILE * SSM_STATE
SSM_STATES = SSM_GROUPS * SSM_STATE
CHUNK_BLOCKS = 4
META_ROWS = 8
M_OWNER, M_FIRST, M_SIZE, M_NCHUNK, M_LAST, M_NBLK, M_NUSED, M_OWNER_CHUNKS = range(8)
VMEM_LIMIT = 56 * 1024 * 1024

_NT = (((1,), (1,)), ((), ()))


def _cparams(*sem):
    return pltpu.CompilerParams(dimension_semantics=sem, vmem_limit_bytes=VMEM_LIMIT)


def _rms(x, gain):
    ms = jnp.mean(x * x, axis=-1, keepdims=True)
    return x * lax.rsqrt(ms + EPS) * gain


def _t5_bucket_table():
    qi = np.arange(BLOCK, dtype=np.int32)[:, None]
    ki = np.arange(2 * BLOCK, dtype=np.int32)[None, :]
    delta = BLOCK + qi - ki
    n = np.maximum(delta, 0)
    max_exact = REL_BUCKETS // 2
    nf = np.maximum(n, 1).astype(np.float32)
    large = max_exact + (np.log(nf / np.float32(max_exact)) / np.float32(math.log(REL_MAX_DIST / max_exact))
                         * np.float32(REL_BUCKETS - max_exact)).astype(np.int32)
    large = np.minimum(large, REL_BUCKETS - 1)
    return np.where(n < max_exact, n, large).astype(np.int32)


def _bias_kernel(rb_ref, bucket_ref, out_ref):
    h = pl.program_id(0)
    bucket = bucket_ref[...]
    acc = jnp.zeros(bucket.shape, F32)
    for b in range(REL_BUCKETS):
        acc = jnp.where(bucket == b, rb_ref[b, h], acc)
    qi = lax.broadcasted_iota(jnp.int32, (BLOCK, BLOCK), 0)
    ki = lax.broadcasted_iota(jnp.int32, (BLOCK, BLOCK), 1)
    upper = ki > qi
    out_ref[0, 0] = jnp.where(upper, acc[:, :BLOCK], acc[:, BLOCK:])
    out_ref[1, 0] = jnp.where(upper, jnp.float32(-1e30), acc[:, BLOCK:])


def _bias_table(rel_bias):
    bucket = jnp.asarray(_t5_bucket_table())
    return pl.pallas_call(
        _bias_kernel,
        grid=(ATTN_HEADS,),
        in_specs=[pl.BlockSpec(memory_space=pltpu.SMEM),
                  pl.BlockSpec((BLOCK, 2 * BLOCK), lambda h: (0, 0))],
        out_specs=pl.BlockSpec((2, 1, BLOCK, BLOCK), lambda h: (0, h, 0, 0)),
        out_shape=jax.ShapeDtypeStruct((2, ATTN_HEADS, BLOCK, BLOCK), F32),
        compiler_params=_cparams("arbitrary"),
        name="t5_bias_table",
    )(rel_bias.astype(F32), bucket)


def _ssm_prep_kernel(lr_ref, li_ref, ldt_ref, br_ref, bi_ref, are_ref, aim_ref, bbr_ref, bbi_ref):
    lr = lr_ref[...]
    li = li_ref[...]
    dt = jnp.exp(ldt_ref[...])
    mag = jnp.exp(lr * dt)
    a_re = mag * jnp.cos(li * dt)
    a_im = mag * jnp.sin(li * dt)
    den = lr * lr + li * li
    nr = a_re - 1.0
    ni = a_im
    coef_re = (nr * lr + ni * li) / den
    coef_im = (ni * lr - nr * li) / den
    are_ref[...] = a_re
    aim_ref[...] = a_im
    br = br_ref[...]
    bi = bi_ref[...]
    bbr_ref[...] = coef_re * br - coef_im * bi
    bbi_ref[...] = coef_re * bi + coef_im * br


def _ssm_prep(lam_re, lam_im, log_dt, b_re, b_im):
    g, n, c = b_re.shape
    vec = jax.ShapeDtypeStruct((g, 1, n), F32)
    mat = jax.ShapeDtypeStruct((g, c, n), F32)
    return pl.pallas_call(
        _ssm_prep_kernel,
        out_shape=(vec, vec, mat, mat),
        name="ssm_discretise",
    )(lam_re.astype(F32).reshape(g, 1, n), lam_im.astype(F32).reshape(g, 1, n),
      log_dt.astype(F32).reshape(g, 1, 1),
      jnp.transpose(b_re.astype(F32), (0, 2, 1)), jnp.transpose(b_im.astype(F32), (0, 2, 1)))


def _block_diag_tiles(m):
    g, r, c = m.shape
    t = g // SSM_GROUPS_PER_TILE
    eye = jnp.eye(SSM_GROUPS_PER_TILE, dtype=m.dtype)
    m4 = m.reshape(t, SSM_GROUPS_PER_TILE, r, c)
    out = m4[:, :, :, None, :] * eye[None, :, None, :, None]
    return out.reshape(t, SSM_GROUPS_PER_TILE * r, SSM_GROUPS_PER_TILE * c)


def _inproj_kernel(x_ref, g_ref, w_ref, q_ref, k_ref, v_ref, u_ref, qx_ref):
    h = _rms(x_ref[...], g_ref[...]).astype(BF16)
    lo = lax.broadcasted_iota(jnp.int32, (1, LANES), 1) < ATTN_HEAD_DIM
    c0 = 0
    for ref in (q_ref, k_ref, v_ref, u_ref, qx_ref):
        if ref is k_ref or ref is v_ref:
            p = jnp.dot(h, w_ref[:, c0:c0 + LANES], preferred_element_type=F32)
            swapped = pltpu.roll(p, ATTN_HEAD_DIM, axis=1)
            ref[...] = jnp.concatenate([jnp.where(lo, p, swapped), jnp.where(lo, swapped, p)],
                                       axis=-1).astype(ref.dtype)
            c0 += LANES
        else:
            c1 = c0 + ref.shape[-1]
            ref[...] = jnp.dot(h, w_ref[:, c0:c1], preferred_element_type=F32).astype(ref.dtype)
            c0 = c1


def _inproj(x2, gain, w_cols, batch, seq, tm):
    t, d = x2.shape
    nsb = seq // tm
    kvw = 2 * LANES
    ncol = w_cols.shape[1]
    row = lambda w: pl.BlockSpec((tm, w), lambda i: (i, 0))
    return pl.pallas_call(
        _inproj_kernel,
        grid=(t // tm,),
        in_specs=[row(d),
                  pl.BlockSpec((1, d), lambda i: (0, 0)),
                  pl.BlockSpec((d, ncol), lambda i: (0, 0), pipeline_mode=pl.Buffered(1))],
        out_specs=[row(ATTN_WIDTH), row(kvw), row(kvw),
                   pl.BlockSpec((tm, SSM_WIDTH), lambda i: (i % nsb, i // nsb)),
                   row(XATTN_WIDTH)],
        out_shape=[jax.ShapeDtypeStruct((t, ATTN_WIDTH), BF16),
                   jax.ShapeDtypeStruct((t, kvw), BF16),
                   jax.ShapeDtypeStruct((t, kvw), BF16),
                   jax.ShapeDtypeStruct((seq, batch * SSM_WIDTH), BF16),
                   jax.ShapeDtypeStruct((t, XATTN_WIDTH), BF16)],
        compiler_params=_cparams("arbitrary"),
        name="in_proj",
    )(x2, gain, w_cols)


def _swa_kernel(sinks_ref, q_ref, kp_ref, kc_ref, vp_ref, vc_ref, bias_ref, qg_ref, kg_ref, og_ref, out_ref):
    kcat = jnp.concatenate([kp_ref[...], kc_ref[...]], axis=0).astype(F32)
    vcat = jnp.concatenate([vp_ref[...], vc_ref[...]], axis=0)
    lane = lax.broadcasted_iota(jnp.int32, (1, LANES), 1)
    lo = lane < ATTN_HEAD_DIM
    upper = (lax.broadcasted_iota(jnp.int32, (BLOCK, BLOCK), 1) >
             lax.broadcasted_iota(jnp.int32, (BLOCK, BLOCK), 0))
    qgain = qg_ref[...]
    pairs_per_kv = ATTN_HEADS // ATTN_KV_HEADS // 2
    outs = []
    for g in range(ATTN_KV_HEADS):
        kn = _rms(kcat[:, g * LANES:(g + 1) * LANES], kg_ref[...]).astype(BF16)
        vg = vcat[:, g * LANES:(g + 1) * LANES]
        for j in range(pairs_per_kv):
            tile = g * pairs_per_kv + j
            qp = q_ref[:, tile * LANES:(tile + 1) * LANES].astype(F32)
            sq = qp * qp
            s_lo = jnp.sum(jnp.where(lo, sq, 0.0), axis=-1, keepdims=True)
            s_hi = jnp.sum(sq, axis=-1, keepdims=True) - s_lo
            inv = jnp.where(lo, lax.rsqrt(s_lo / ATTN_HEAD_DIM + EPS), lax.rsqrt(s_hi / ATTN_HEAD_DIM + EPS))
            qn = qp * inv * qgain
            halves = []
            for half in range(2):
                hd = 2 * tile + half
                keep = lo if half == 0 else jnp.logical_not(lo)
                qh = jnp.where(keep, qn, 0.0).astype(BF16)
                both = lax.dot_general(qh, kn, _NT, preferred_element_type=F32)
                l = jnp.where(upper, both[:, :BLOCK], both[:, BLOCK:]) + bias_ref[0, hd]
                sink = sinks_ref[hd]
                m = jnp.maximum(jnp.max(l, axis=-1, keepdims=True), sink)
                p = jnp.exp(l - m)
                den = jnp.sum(p, axis=-1, keepdims=True) + jnp.exp(sink - m)
                pz = jnp.concatenate([jnp.where(upper, p, 0.0), jnp.where(upper, 0.0, p)], axis=-1)
                halves.append(jnp.dot(pz.astype(BF16), vg, preferred_element_type=F32) / den)
            outs.append(jnp.where(lo, halves[0], halves[1]))
    y = jnp.concatenate(outs, axis=-1)
    out_ref[...] = _rms(y, og_ref[...]).astype(out_ref.dtype)


def _swa(q, kk, vv, bias, sinks, qgain, kgain, ogain, batch, seq):
    nb = seq // BLOCK
    kvw = 2 * LANES
    cur = lambda b, n: (b * nb + n, 0)
    prev = lambda b, n: (b * nb + jnp.maximum(n - 1, 0), 0)
    const2 = lambda b, n: (0, 0)
    return pl.pallas_call(
        _swa_kernel,
        grid=(batch, nb),
        in_specs=[pl.BlockSpec(memory_space=pltpu.SMEM),
                  pl.BlockSpec((BLOCK, ATTN_WIDTH), cur),
                  pl.BlockSpec((BLOCK, kvw), prev), pl.BlockSpec((BLOCK, kvw), cur),
                  pl.BlockSpec((BLOCK, kvw), prev), pl.BlockSpec((BLOCK, kvw), cur),
                  pl.BlockSpec((1, ATTN_HEADS, BLOCK, BLOCK), lambda b, n: (jnp.where(n == 0, 1, 0), 0, 0, 0)),
                  pl.BlockSpec((1, LANES), const2), pl.BlockSpec((1, LANES), const2),
                  pl.BlockSpec((1, ATTN_WIDTH), const2)],
        out_specs=pl.BlockSpec((BLOCK, ATTN_WIDTH), cur),
        out_shape=jax.ShapeDtypeStruct((batch * seq, ATTN_WIDTH), BF16),
        compiler_params=_cparams("arbitrary", "arbitrary"),
        name="swa_attention",
    )(sinks, q, kk, kk, vv, vv, bias, qgain, kgain, ogain)


def _memkv_kernel(m_ref, g_ref, w_ref, kg_ref, k_ref, v_ref):
    h = _rms(m_ref[...], g_ref[...]).astype(BF16)
    km = jnp.dot(h, w_ref[:, :XATTN_WIDTH], preferred_element_type=F32)
    for hd in range(XATTN_HEADS):
        sl = slice(hd * XATTN_HEAD_DIM, (hd + 1) * XATTN_HEAD_DIM)
        k_ref[:, sl] = _rms(km[:, sl], kg_ref[...]).astype(k_ref.dtype)
    v_ref[...] = jnp.dot(h, w_ref[:, XATTN_WIDTH:], preferred_element_type=F32).astype(v_ref.dtype)


def _memkv(mem2, gain, w_kv, kgain, tm):
    r, d = mem2.shape
    row = lambda w: pl.BlockSpec((tm, w), lambda i: (i, 0))
    const = lambda shape: pl.BlockSpec(shape, lambda i: (0, 0))
    return pl.pallas_call(
        _memkv_kernel,
        grid=(r // tm,),
        in_specs=[row(d), const((1, d)), const((d, 2 * XATTN_WIDTH)), const((1, XATTN_HEAD_DIM))],
        out_specs=[row(XATTN_WIDTH), row(XATTN_WIDTH)],
        out_shape=[jax.ShapeDtypeStruct((r, XATTN_WIDTH), BF16)] * 2,
        compiler_params=_cparams("arbitrary"),
        name="mem_kv_proj",
    )(mem2, gain, w_kv, kgain)


def _xattn_kernel(q_ref, k_ref, v_ref, qg_ref, og_ref, out_ref):
    outs = []
    for hd in range(XATTN_HEADS):
        sl = slice(hd * XATTN_HEAD_DIM, (hd + 1) * XATTN_HEAD_DIM)
        qn = _rms(q_ref[:, sl].astype(F32), qg_ref[...]).astype(BF16)
        l = lax.dot_general(qn, k_ref[:, sl], _NT, preferred_element_type=F32)
        m = jnp.max(l, axis=-1, keepdims=True)
        p = jnp.exp(l - m)
        den = jnp.sum(p, axis=-1, keepdims=True)
        outs.append(jnp.dot(p.astype(BF16), v_ref[:, sl], preferred_element_type=F32) / den)
    y = jnp.concatenate(outs, axis=-1)
    out_ref[...] = _rms(y, og_ref[...]).astype(out_ref.dtype)


def _xattn(qx, km, vm, qgain, ogain, batch, seq, mem_len, tq):
    nq = seq // tq
    const2 = lambda b, i: (0, 0)
    return pl.pallas_call(
        _xattn_kernel,
        grid=(batch, nq),
        in_specs=[pl.BlockSpec((tq, XATTN_WIDTH), lambda b, i: (b * nq + i, 0)),
                  pl.BlockSpec((mem_len, XATTN_WIDTH), lambda b, i: (b, 0)),
                  pl.BlockSpec((mem_len, XATTN_WIDTH), lambda b, i: (b, 0)),
                  pl.BlockSpec((1, XATTN_HEAD_DIM), const2),
                  pl.BlockSpec((1, XATTN_WIDTH), const2)],
        out_specs=pl.BlockSpec((tq, XATTN_WIDTH), lambda b, i: (b * nq + i, 0)),
        out_shape=jax.ShapeDtypeStruct((batch * seq, XATTN_WIDTH), BF16),
        compiler_params=_cparams("arbitrary", "arbitrary"),
        name="mem_xattn",
    )(qx, km, vm, qgain, ogain)


def _ssm_kernel(u_ref, bmat_ref, cmat_ref, are_ref, aim_ref, d_ref, wglu_ref, og_ref, out_ref,
                tb_ref, xr_ref, xi_ref, sr_ref, si_ref, *, batch, steps):
    @pl.when(pl.program_id(0) == 0)
    def _():
        sr_ref[...] = jnp.zeros_like(sr_ref)
        si_ref[...] = jnp.zeros_like(si_ref)

    for b in range(batch):
        for j in range(SSM_TILES):
            c0 = b * SSM_WIDTH + j * LANES
            tb_ref[j, pl.ds(b, steps, stride=batch), :] = u_ref[:, c0:c0 + LANES].astype(F32)
    uf = jnp.concatenate([tb_ref[j] for j in range(SSM_TILES)], axis=-1)
    u = uf.astype(BF16)
    for j in range(SSM_TILES):
        bu = jnp.dot(u[:, j * LANES:(j + 1) * LANES], bmat_ref[j], preferred_element_type=F32)
        xr_ref[:, j * SSM_TILE_STATE:(j + 1) * SSM_TILE_STATE] = bu[:, :SSM_TILE_STATE]
        xi_ref[:, j * SSM_TILE_STATE:(j + 1) * SSM_TILE_STATE] = bu[:, SSM_TILE_STATE:]

    chunk = 1024
    for c0 in range(0, SSM_STATES, chunk):
        cs = slice(c0, c0 + chunk)
        ar = jnp.broadcast_to(are_ref[:, cs], (batch, chunk))
        ai = jnp.broadcast_to(aim_ref[:, cs], (batch, chunk))

        def step(t, carry, cs=cs, ar=ar, ai=ai):
            s_r, s_i = carry
            rows = pl.ds(pl.multiple_of(t * batch, batch), batch)
            n_r = ar * s_r - ai * s_i + xr_ref[rows, cs]
            n_i = ar * s_i + ai * s_r + xi_ref[rows, cs]
            xr_ref[rows, cs] = n_r
            xi_ref[rows, cs] = n_i
            return n_r, n_i

        s_r, s_i = lax.fori_loop(0, steps, step, (sr_ref[:, cs], si_ref[:, cs]))
        sr_ref[:, cs] = s_r
        si_ref[:, cs] = s_i

    ys = []
    for j in range(SSM_TILES):
        sl = slice(j * SSM_TILE_STATE, (j + 1) * SSM_TILE_STATE)
        xcat = jnp.concatenate([xr_ref[:, sl], xi_ref[:, sl]], axis=-1).astype(BF16)
        ys.append(jnp.dot(xcat, cmat_ref[j], preferred_element_type=F32))
    y = jnp.concatenate(ys, axis=-1) + d_ref[...] * uf
    y = jax.nn.gelu(y)
    y = y * jax.nn.sigmoid(jnp.dot(y.astype(BF16), wglu_ref[...], preferred_element_type=F32))
    y = _rms(y, og_ref[...])
    for j in range(SSM_TILES):
        tb_ref[j] = y[:, j * LANES:(j + 1) * LANES]
    for b in range(batch):
        for j in range(SSM_TILES):
            c0 = b * SSM_WIDTH + j * LANES
            out_ref[:, c0:c0 + LANES] = tb_ref[j, pl.ds(b, steps, stride=batch), :].astype(out_ref.dtype)


def _ssm(u_sb, bmat, cmat, a_re, a_im, d_skip, w_glu, ogain, batch, seq, steps):
    rows = steps * batch
    const2 = lambda c: (0, 0)
    const3 = lambda c: (0, 0, 0)
    return pl.pallas_call(
        functools.partial(_ssm_kernel, batch=batch, steps=steps),
        grid=(seq // steps,),
        in_specs=[pl.BlockSpec((steps, batch * SSM_WIDTH), lambda c: (c, 0)),
                  pl.BlockSpec(bmat.shape, const3), pl.BlockSpec(cmat.shape, const3),
                  pl.BlockSpec((1, SSM_STATES), const2), pl.BlockSpec((1, SSM_STATES), const2),
                  pl.BlockSpec((1, SSM_WIDTH), const2),
                  pl.BlockSpec((SSM_WIDTH, SSM_WIDTH), const2),
                  pl.BlockSpec((1, SSM_WIDTH), const2)],
        out_specs=pl.BlockSpec((steps, batch * SSM_WIDTH), lambda c: (c, 0)),
        out_shape=jax.ShapeDtypeStruct((seq, batch * SSM_WIDTH), BF16),
        scratch_shapes=[pltpu.VMEM((SSM_TILES, rows, LANES), F32),
                        pltpu.VMEM((rows, SSM_STATES), F32), pltpu.VMEM((rows, SSM_STATES), F32),
                        pltpu.VMEM((batch, SSM_STATES), F32), pltpu.VMEM((batch, SSM_STATES), F32)],
        compiler_params=_cparams("arbitrary"),
        name="s5_layer",
    )(u_sb, bmat, cmat, a_re, a_im, d_skip, w_glu, ogain)


def _outproj_kernel(ya_ref, ys_ref, yx_ref, x_ref, wo_ref, g_ref, wr_ref, x1_ref, lt_ref, hp_ref, *, sub):
    for r0 in range(0, x_ref.shape[0], sub):
        rows = slice(r0, r0 + sub)
        mix = jnp.concatenate([ya_ref[rows, :], ys_ref[rows, :], yx_ref[rows, :]], axis=-1)
        x1 = x_ref[rows, :] + jnp.dot(mix, wo_ref[...], preferred_element_type=F32)
        x1_ref[rows, :] = x1
        h2 = _rms(x1, g_ref[...])
        hi = h2.astype(BF16)
        hi32 = hi.astype(F32)
        lo = (h2 - hi32).astype(BF16)
        both = jnp.dot(hi, wr_ref[...], preferred_element_type=F32)
        lg = both[:, :LANES] + both[:, LANES:] + jnp.dot(lo, wr_ref[:, :LANES], preferred_element_type=F32)
        lt_ref[rows, :] = lg
        hp_ref[rows, :] = _pack_rows(hi32)


def _outproj(ya, ys_sb, yx, x2, w_o, gain, wr, batch, seq, tm):
    t, d = x2.shape
    nsb = seq // tm
    row = lambda w: pl.BlockSpec((tm, w), lambda i: (i, 0))
    const = lambda shape: pl.BlockSpec(shape, lambda i: (0, 0), pipeline_mode=pl.Buffered(1))
    return pl.pallas_call(
        functools.partial(_outproj_kernel, sub=min(256, tm)),
        grid=(t // tm,),
        in_specs=[row(ATTN_WIDTH),
                  pl.BlockSpec((tm, SSM_WIDTH), lambda i: (i % nsb, i // nsb)),
                  row(XATTN_WIDTH), row(d),
                  const(w_o.shape), const((1, d)), const(wr.shape)],
        out_specs=[row(d), row(LANES), row(d // 2)],
        out_shape=[jax.ShapeDtypeStruct((t, d), F32), jax.ShapeDtypeStruct((t, LANES), F32),
                   jax.ShapeDtypeStruct((t, d // 2), jnp.uint32)],
        compiler_params=_cparams("arbitrary"),
        name="out_proj_router",
    )(ya, ys_sb, yx, x2, w_o, gain, wr)


def _route_kernel(lt_ref, tri_ref, dest_ref, w_ref, meta_ref, cnt_ref, carry_ref, pstart_ref):
    phase = pl.program_id(0)
    c = pl.program_id(1)
    logits = lt_ref[...].T
    tc = logits.shape[1]
    ng, epg = N_EXPERT_GROUPS, EXPERTS_PER_GROUP
    row8 = lax.broadcasted_iota(jnp.int32, (ng, tc), 0)

    gl = logits[0:ng]
    gmax = jnp.max(gl, axis=0, keepdims=True)
    gidx = jnp.min(jnp.where(gl == gmax, row8, ng), axis=0, keepdims=True)
    gate = 1.0 / jnp.sum(jnp.exp(gl - gmax), axis=0, keepdims=True)
    sel = jnp.zeros((epg, tc), F32)
    for g in range(ng):
        sel = jnp.where(gidx == g, logits[ng + g * epg:ng + (g + 1) * epg], sel)
    v1 = jnp.max(sel, axis=0, keepdims=True)
    i1 = jnp.min(jnp.where(sel == v1, row8, epg), axis=0, keepdims=True)
    sel2 = jnp.where(row8 == i1, -jnp.inf, sel)
    v2 = jnp.max(sel2, axis=0, keepdims=True)
    i2 = jnp.min(jnp.where(sel2 == v2, row8, epg), axis=0, keepdims=True)
    e = jnp.exp(v2 - v1)
    w1 = gate * (1.0 / (1.0 + e))
    w2 = gate * (e / (1.0 + e))
    e1 = gidx * epg + i1
    e2 = gidx * epg + i2
    rowe = lax.broadcasted_iota(jnp.int32, (N_EXPERTS, tc), 0)
    oh1 = rowe == e1
    oh2 = rowe == e2
    member = jnp.where(jnp.logical_or(oh1, oh2), 1.0, 0.0)
    chunk_cnt = jnp.sum(member, axis=1, keepdims=True)

    @pl.when(phase == 0)
    def _():
        @pl.when(c == 0)
        def _():
            cnt_ref[...] = jnp.zeros_like(cnt_ref)
        cnt_ref[...] += chunk_cnt

    @pl.when(phase == 1)
    def _():
        @pl.when(c == 0)
        def _():
            cnt = cnt_ref[...]
            nblk = jnp.floor((cnt + (MOE_BLOCK - 1)) * (1.0 / MOE_BLOCK))
            nchunk = jnp.floor((nblk + (CHUNK_BLOCKS - 1)) * (1.0 / CHUNK_BLOCKS))
            r = lax.broadcasted_iota(jnp.int32, (N_EXPERTS, LANES), 0)
            cidx = lax.broadcasted_iota(jnp.int32, (N_EXPERTS, LANES), 1)
            to_row = lambda col: jnp.sum(jnp.where(r == cidx, col, 0.0), axis=0, keepdims=True)
            cumsum_col = lambda col: jnp.sum(jnp.where(cidx <= r, to_row(col), 0.0), axis=1, keepdims=True)
            cumsum_row = lambda col: jnp.sum(jnp.where(r <= cidx, col, 0.0), axis=0, keepdims=True)
            bend = cumsum_col(nblk)
            bstart = bend - nblk
            cend = cumsum_col(nchunk)
            cstart = cend - nchunk
            pstart_ref[...] = bstart * MOE_BLOCK
            carry_ref[...] = jnp.zeros_like(carry_ref)
            lanef = lax.broadcasted_iota(jnp.int32, (1, LANES), 1).astype(F32)
            owner = jnp.minimum(jnp.sum(jnp.where(cend <= lanef, 1.0, 0.0), axis=0, keepdims=True),
                                N_EXPERTS - 1.0)
            own = r.astype(F32) == owner
            pick = lambda col: jnp.sum(jnp.where(own, col, 0.0), axis=0, keepdims=True)
            idx = lanef - pick(cstart)
            first = pick(bstart) + CHUNK_BLOCKS * idx
            size = jnp.clip(pick(nblk) - CHUNK_BLOCKS * idx, 0.0, float(CHUNK_BLOCKS))
            zero = jnp.zeros((1, LANES), F32)
            rows = [owner, first, size,
                    zero + jnp.sum(nchunk, axis=0, keepdims=True),
                    cumsum_row(nblk) - 1.0,
                    to_row(nblk),
                    zero + jnp.sum(nblk, axis=0, keepdims=True),
                    pick(nchunk)]
            for k, v in enumerate(rows):
                meta_ref[k:k + 1, :] = v.astype(jnp.int32)

        before = carry_ref[...] + jnp.dot(member.astype(BF16), tri_ref[...], preferred_element_type=F32)
        pos = before + pstart_ref[...]
        dest_ref[0:1, :] = jnp.sum(jnp.where(oh1, pos, 0.0), axis=0, keepdims=True).astype(jnp.int32)
        dest_ref[1:2, :] = jnp.sum(jnp.where(oh2, pos, 0.0), axis=0, keepdims=True).astype(jnp.int32)
        w_ref[0:1, :] = w1
        w_ref[1:2, :] = w2
        carry_ref[...] += chunk_cnt


def _route(logits_t, tc):
    t = logits_t.shape[0]
    nc = t // tc
    tri = jnp.asarray(np.triu(np.ones((tc, tc), np.float32), k=1), dtype=BF16)
    return pl.pallas_call(
        _route_kernel,
        grid=(2, nc),
        in_specs=[pl.BlockSpec((tc, LANES), lambda p, c: (c, 0)),
                  pl.BlockSpec((tc, tc), lambda p, c: (0, 0))],
        out_specs=[pl.BlockSpec((2, tc), lambda p, c: (0, c * p)),
                   pl.BlockSpec((2, tc), lambda p, c: (0, c * p)),
                   pl.BlockSpec((META_ROWS, LANES), lambda p, c: (0, 0))],
        out_shape=[jax.ShapeDtypeStruct((2, t), jnp.int32), jax.ShapeDtypeStruct((2, t), F32),
                   jax.ShapeDtypeStruct((META_ROWS, LANES), jnp.int32)],
        scratch_shapes=[pltpu.VMEM((N_EXPERTS, 1), F32)] * 3,
        compiler_params=_cparams("arbitrary", "arbitrary"),
        name="moe_route",
    )(logits_t, tri)


def _meta(meta_ref, row, lane=0):
    return meta_ref[row * LANES + lane]


def _fill_blocks(meta_ref, zbuf, dst_ref, sem, n_blocks, *, expert_tails):
    zbuf[...] = jnp.zeros_like(zbuf)
    n_used = _meta(meta_ref, M_NUSED)
    block = lambda b: pltpu.make_async_copy(zbuf, dst_ref.at[pl.ds(b * MOE_BLOCK, MOE_BLOCK), :], sem)

    def tails(fn):
        def body(e, carry):
            @pl.when(_meta(meta_ref, M_NBLK, e) > 0)
            def _():
                fn(block(_meta(meta_ref, M_LAST, e)))
            return carry
        lax.fori_loop(0, N_EXPERTS, body, 0)

    def unused(fn):
        def body(b, carry):
            fn(block(b))
            return carry
        lax.fori_loop(n_used, n_blocks, body, 0)

    for phase in (lambda cp: cp.start(), lambda cp: cp.wait()):
        if expert_tails:
            tails(phase)
        unused(phase)


def _dispatch_kernel(dest_ref, meta_ref, h_ref, xs_ref, zbuf, sem_z, sem, *, tokens, n_blocks):
    tm = h_ref.shape[0]

    @pl.when(pl.program_id(0) == 0)
    def _():
        _fill_blocks(meta_ref, zbuf, xs_ref, sem_z, n_blocks, expert_tails=True)

    base = pl.program_id(0) * tm

    def issue(g, carry):
        r8 = pl.multiple_of(g * SUBLANES, SUBLANES)
        for s in range(SUBLANES):
            for k in range(2):
                d = dest_ref[k * tokens + base + r8 + s]
                pltpu.make_async_copy(h_ref.at[pl.ds(r8 + s, 1), :], xs_ref.at[pl.ds(d, 1), :], sem).start()
        return carry
    lax.fori_loop(0, tm // SUBLANES, issue, 0)
    for k in range(2):
        pltpu.make_async_copy(h_ref, xs_ref.at[pl.ds(0, tm), :], sem).wait()


def _dispatch(dest_flat, meta_flat, h2p, n_blocks, tm):
    t, w = h2p.shape
    grid_spec = pltpu.PrefetchScalarGridSpec(
        num_scalar_prefetch=2,
        grid=(t // tm,),
        in_specs=[pl.BlockSpec((tm, w), lambda i, d, m: (i, 0))],
        out_specs=pl.BlockSpec(memory_space=pl.ANY),
        scratch_shapes=[pltpu.VMEM((MOE_BLOCK, w), h2p.dtype),
                        pltpu.SemaphoreType.DMA(()), pltpu.SemaphoreType.DMA(())],
    )
    return pl.pallas_call(
        functools.partial(_dispatch_kernel, tokens=t, n_blocks=n_blocks),
        grid_spec=grid_spec,
        out_shape=jax.ShapeDtypeStruct((n_blocks * MOE_BLOCK, w), h2p.dtype),
        compiler_params=_cparams("arbitrary"),
        name="moe_dispatch",
    )(dest_flat, meta_flat, h2p)


def _pack_rows(x):
    bits = lax.bitcast_convert_type(x, jnp.uint32)
    half = x.shape[1] // 2
    return (bits[:, half:] & jnp.uint32(0xFFFF0000)) | (bits[:, :half] >> 16)


def _unpack_halves(words):
    lo = lax.bitcast_convert_type(words << 16, F32)
    hi = lax.bitcast_convert_type(words & jnp.uint32(0xFFFF0000), F32)
    return lo, hi


def _unpack_rows(words):
    return jnp.concatenate(_unpack_halves(words), axis=-1).astype(BF16)


def _expert_kernel(meta_ref, xs_ref, wg_ref, wu_ref, wd_ref, yb_ref,
                   xbuf, ybuf, zbuf, wg_f32, wu_f32, wd_f32, wslot_ref,
                   sem_in, sem_out, sem_z, sem_w, *, n_blocks):
    c = pl.program_id(0)
    n_chunks = _meta(meta_ref, M_NCHUNK)
    slot = c % 2

    def weight_copies(k, s):
        e = _meta(meta_ref, M_OWNER, k)
        return [pltpu.make_async_copy(src.at[e], dst.at[s], sem_w.at[s])
                for src, dst in ((wg_ref, wg_f32), (wu_ref, wu_f32), (wd_ref, wd_f32))]

    def in_copy(k, s, nb):
        rows = nb * MOE_BLOCK
        src = xs_ref.at[pl.ds(_meta(meta_ref, M_FIRST, k) * MOE_BLOCK, rows), :]
        return pltpu.make_async_copy(src, xbuf.at[s, pl.ds(0, rows), :], sem_in.at[s])

    def out_copy(k, s, nb):
        rows = nb * MOE_BLOCK
        dst = yb_ref.at[pl.ds(_meta(meta_ref, M_FIRST, k) * MOE_BLOCK, rows), :]
        return pltpu.make_async_copy(ybuf.at[s, pl.ds(0, rows), :], dst, sem_out.at[s])

    def by_size(k, fn):
        for nb in range(1, CHUNK_BLOCKS + 1):
            pl.when(_meta(meta_ref, M_SIZE, k) == nb)(functools.partial(fn, nb))

    @pl.when(c == 0)
    def _():
        by_size(0, lambda nb: in_copy(0, 0, nb).start())

    @pl.when(c + 1 < n_chunks)
    def _():
        by_size(c + 1, lambda nb: in_copy(c + 1, 1 - slot, nb).start())

    @pl.when(c < n_chunks)
    def _():
        prev = jnp.maximum(c - 1, 0)
        new_expert = jnp.logical_or(c == 0, _meta(meta_ref, M_OWNER, c) != _meta(meta_ref, M_OWNER, prev))

        @pl.when(c == 0)
        def _():
            wslot_ref[0] = 0
            for cp in weight_copies(0, 0):
                cp.start()

        @pl.when(new_expert)
        def _():
            ws = jnp.where(c == 0, 0, 1 - wslot_ref[0])
            wslot_ref[0] = ws
            nxt = c + _meta(meta_ref, M_OWNER_CHUNKS, c)

            @pl.when(nxt < n_chunks)
            def _():
                for cp in weight_copies(nxt, 1 - ws):
                    cp.start()
            for cp in weight_copies(c, ws):
                cp.wait()

        @pl.when(c >= 2)
        def _():
            by_size(c - 2, lambda nb: out_copy(c - 2, slot, nb).wait())

        def compute(nb):
            rows = nb * MOE_BLOCK
            ws = wslot_ref[0]
            in_copy(c, slot, nb).wait()
            h = _unpack_rows(xbuf[slot, 0:rows, :])
            gate = jnp.dot(h, wg_f32[ws].astype(BF16), preferred_element_type=F32)
            up = jnp.dot(h, wu_f32[ws].astype(BF16), preferred_element_type=F32)
            act = (jax.nn.silu(gate) * up).astype(BF16)
            y = jnp.dot(act, wd_f32[ws].astype(BF16), preferred_element_type=F32)
            ybuf[slot, 0:rows, :] = _pack_rows(y.astype(BF16).astype(F32))
            out_copy(c, slot, nb).start()
        by_size(c, compute)

    @pl.when(c == pl.num_programs(0) - 1)
    def _():
        _fill_blocks(meta_ref, zbuf, yb_ref, sem_z, n_blocks, expert_tails=False)
        for back in (2, 1):
            @pl.when(n_chunks >= back)
            def _(back=back):
                k = n_chunks - back
                by_size(k, lambda nb: out_copy(k, k % 2, nb).wait())


def _experts(meta_flat, xs, w_gate, w_up, w_down, n_blocks, n_chunks_max):
    d, de = w_gate.shape[1], w_gate.shape[2]
    rows = CHUNK_BLOCKS * MOE_BLOCK

    hbm = pl.BlockSpec(memory_space=pl.ANY)
    grid_spec = pltpu.PrefetchScalarGridSpec(
        num_scalar_prefetch=1,
        grid=(n_chunks_max,),
        in_specs=[hbm, hbm, hbm, hbm],
        out_specs=hbm,
        scratch_shapes=[pltpu.VMEM((2, rows, xs.shape[1]), xs.dtype),
                        pltpu.VMEM((2, rows, d // 2), jnp.uint32),
                        pltpu.VMEM((MOE_BLOCK, d // 2), jnp.uint32),
                        pltpu.VMEM((2, d, de), F32), pltpu.VMEM((2, d, de), F32), pltpu.VMEM((2, de, d), F32),
                        pltpu.SMEM((1,), jnp.int32),
                        pltpu.SemaphoreType.DMA((2,)), pltpu.SemaphoreType.DMA((2,)),
                        pltpu.SemaphoreType.DMA(()), pltpu.SemaphoreType.DMA((2,))],
    )
    return pl.pallas_call(
        functools.partial(_expert_kernel, n_blocks=n_blocks),
        grid_spec=grid_spec,
        out_shape=jax.ShapeDtypeStruct((n_blocks * MOE_BLOCK, d // 2), jnp.uint32),
        compiler_params=_cparams("arbitrary"),
        name="moe_experts",
    )(meta_flat, xs, w_gate, w_up, w_down)


def _combine_kernel(dest_ref, yb_ref, x1_ref, w_ref, out_ref, gbuf, sem, *, tokens):
    i = pl.program_id(0)
    tm = x1_ref.shape[0]
    slot = i % 2

    def gather(tile, s):
        base = tile * tm

        def issue(g, carry):
            r8 = pl.multiple_of(g * SUBLANES, SUBLANES)
            for sub in range(SUBLANES):
                for k in range(2):
                    d = dest_ref[k * tokens + base + r8 + sub]
                    pltpu.make_async_copy(yb_ref.at[pl.ds(d, 1), :], gbuf.at[s, k, pl.ds(r8 + sub, 1), :],
                                          sem.at[s]).start()
            return carry
        lax.fori_loop(0, tm // SUBLANES, issue, 0)

    @pl.when(i == 0)
    def _():
        gather(0, 0)

    @pl.when(i + 1 < pl.num_programs(0))
    def _():
        gather(i + 1, 1 - slot)

    for k in range(2):
        pltpu.make_async_copy(yb_ref.at[pl.ds(0, tm), :], gbuf.at[slot, k], sem.at[slot]).wait()
    w = w_ref[...]
    half = x1_ref.shape[1] // 2
    lo0, hi0 = _unpack_halves(gbuf[slot, 0])
    lo1, hi1 = _unpack_halves(gbuf[slot, 1])
    out_ref[:, :half] = x1_ref[:, :half] + (lo0 * w[:, 0:1] + lo1 * w[:, 1:2])
    out_ref[:, half:] = x1_ref[:, half:] + (hi0 * w[:, 0:1] + hi1 * w[:, 1:2])


def _combine(dest_flat, yb, x1, w_tok, tm):
    t, d = x1.shape
    grid_spec = pltpu.PrefetchScalarGridSpec(
        num_scalar_prefetch=1,
        grid=(t // tm,),
        in_specs=[pl.BlockSpec(memory_space=pl.ANY),
                  pl.BlockSpec((tm, d), lambda i, dr: (i, 0)),
                  pl.BlockSpec((tm, 2), lambda i, dr: (i, 0))],
        out_specs=pl.BlockSpec((tm, d), lambda i, dr: (i, 0)),
        scratch_shapes=[pltpu.VMEM((2, 2, tm, d // 2), jnp.uint32), pltpu.SemaphoreType.DMA((2,))],
    )
    return pl.pallas_call(
        functools.partial(_combine_kernel, tokens=t),
        grid_spec=grid_spec,
        out_shape=jax.ShapeDtypeStruct((t, d), F32),
        compiler_params=_cparams("arbitrary"),
        name="moe_combine",
    )(dest_flat, yb, x1, w_tok)


def _row(v):
    return v.astype(F32).reshape(1, -1)


def _layer(x2, mem2, batch, seq, mem_len, p):
    t, d = x2.shape

    tm_in = min(512, seq)
    q, kk, vv, u_sb, qx = _inproj(x2, _row(p["norm_mix"]), p["w_in"].astype(BF16), batch, seq, tm_in)

    bias = _bias_table(p["rel_bias"])
    qgain = jnp.tile(_row(p["q_norm"]), (1, 2)) * (1.0 / math.sqrt(ATTN_HEAD_DIM))
    kgain = jnp.tile(_row(p["k_norm"]), (1, 2))
    ya = _swa(q, kk, vv, bias, p["attn_sinks"].astype(F32), qgain, kgain, _row(p["out_norm_attn"]), batch, seq)

    km, vm = _memkv(mem2, _row(p["mem_norm"]), p["w_mem_kv"].astype(BF16), _row(p["xk_norm"]),
                    min(256, mem2.shape[0]))
    xq_gain = _row(p["xq_norm"]) * (1.0 / math.sqrt(XATTN_HEAD_DIM))
    yx = _xattn(qx, km, vm, xq_gain, _row(p["out_norm_xattn"]), batch, seq, mem_len, min(512, seq))

    a_re, a_im, bbr, bbi = _ssm_prep(p["ssm_lambda_re"], p["ssm_lambda_im"], p["ssm_log_dt"],
                                     p["ssm_b_re"], p["ssm_b_im"])
    bmat = jnp.concatenate([_block_diag_tiles(bbr), _block_diag_tiles(bbi)], axis=-1).astype(BF16)
    c_re_t = jnp.transpose(p["ssm_c_re"].astype(F32), (0, 2, 1))
    c_im_t = jnp.transpose(p["ssm_c_im"].astype(F32), (0, 2, 1))
    cmat = jnp.concatenate([_block_diag_tiles(c_re_t), _block_diag_tiles(-c_im_t)], axis=1).astype(BF16)
    steps = min(64, seq)
    ys_sb = _ssm(u_sb, bmat, cmat,
                 a_re.reshape(1, SSM_STATES), a_im.reshape(1, SSM_STATES), _row(p["ssm_d"]),
                 p["ssm_w_glu"].astype(BF16), _row(p["out_norm_ssm"]), batch, seq, steps)

    wr = jnp.concatenate([p["w_router_group"], p["w_router_expert"]], axis=1).astype(F32)
    wr = jnp.pad(wr, ((0, 0), (0, LANES - wr.shape[1])))
    wr_hi = wr.astype(BF16)
    wr_lo = (wr - wr_hi.astype(F32)).astype(BF16)
    tm_out = min(512, seq)
    x1, logits_t, h2p = _outproj(ya, ys_sb, yx, x2, p["w_o"].astype(BF16), _row(p["norm_ffn"]),
                                 jnp.concatenate([wr_hi, wr_lo], axis=1), batch, seq, tm_out)

    dest, w_k, meta = _route(logits_t, min(1024, t))
    n_blocks = (2 * t) // MOE_BLOCK + N_EXPERTS
    n_chunks_max = (n_blocks + (CHUNK_BLOCKS - 1) * N_EXPERTS) // CHUNK_BLOCKS
    dest_flat = dest.reshape(2 * t)
    meta_flat = meta.reshape(META_ROWS * LANES)
    xs = _dispatch(dest_flat, meta_flat, h2p, n_blocks, min(1024, t))
    yb = _experts(meta_flat, xs, p["w_gate"], p["w_up"], p["w_down"], n_blocks, n_chunks_max)
    return _combine(dest_flat, yb, x1, w_k.T, min(256, t))


def kernel(x, mem, norm_mix, w_in, q_norm, k_norm, attn_sinks, rel_bias, ssm_lambda_re, ssm_lambda_im, ssm_log_dt, ssm_b_re, ssm_b_im, ssm_c_re, ssm_c_im, ssm_d, ssm_w_glu, mem_norm, w_mem_kv, xq_norm, xk_norm, out_norm_attn, out_norm_ssm, out_norm_xattn, w_o, norm_ffn, w_router_group, w_router_expert, w_gate, w_up, w_down):
    batch, seq, d = x.shape
    mem_len = mem.shape[1]
    per_layer = dict(norm_mix=norm_mix, w_in=w_in, q_norm=q_norm, k_norm=k_norm, attn_sinks=attn_sinks,
                     ssm_lambda_re=ssm_lambda_re, ssm_lambda_im=ssm_lambda_im, ssm_log_dt=ssm_log_dt,
                     ssm_b_re=ssm_b_re, ssm_b_im=ssm_b_im, ssm_c_re=ssm_c_re, ssm_c_im=ssm_c_im,
                     ssm_d=ssm_d, ssm_w_glu=ssm_w_glu, mem_norm=mem_norm, w_mem_kv=w_mem_kv,
                     xq_norm=xq_norm, xk_norm=xk_norm, out_norm_attn=out_norm_attn,
                     out_norm_ssm=out_norm_ssm, out_norm_xattn=out_norm_xattn, w_o=w_o, norm_ffn=norm_ffn,
                     w_router_group=w_router_group, w_router_expert=w_router_expert,
                     w_gate=w_gate, w_up=w_up, w_down=w_down)
    x2 = x.astype(F32).reshape(batch * seq, d)
    mem2 = mem.astype(F32).reshape(batch * mem_len, d)
    for l in range(norm_mix.shape[0]):
        p = {k: v[l] for k, v in per_layer.items()}
        p["rel_bias"] = rel_bias
        x2 = _layer(x2, mem2, batch, seq, mem_len, p)
    return x2.reshape(batch, seq, d).astype(x.dtype)
```

```python
import functools
import math

import numpy as np
import jax
import jax.numpy as jnp
from jax import lax
from jax.experimental import pallas as pl
from jax.experimental.pallas import tpu as pltpu

F32 = jnp.float32
BF16 = jnp.bfloat16
EPS = 1e-6

ATTN_HEADS = 16
ATTN_KV_HEADS = 2
ATTN_HEAD_DIM = 64
ATTN_WIDTH = ATTN_HEADS * ATTN_HEAD_DIM
WINDOW = 128
BLOCK = 128
REL_BUCKETS = 32
REL_MAX_DIST = 128
SSM_GROUP_CH = 16
SSM_GROUPS = 32
SSM_STATE = 64
SSM_WIDTH = SSM_GROUPS * SSM_GROUP_CH
XATTN_HEADS = 4
XATTN_HEAD_DIM = 128
XATTN_WIDTH = XATTN_HEADS * XATTN_HEAD_DIM
N_EXPERT_GROUPS = 8
EXPERTS_PER_GROUP = 8
N_EXPERTS = N_EXPERT_GROUPS * EXPERTS_PER_GROUP
D_EXPERT = 512
MOE_BLOCK = 128

LANES = 128
SUBLANES = 8
SSM_GROUPS_PER_TILE = LANES // SSM_GROUP_CH
SSM_TILES = SSM_WIDTH // LANES
SSM_TILE_STATE = SSM_GROUPS_PER_TILE * SSM_STATE
SSM_STATES = SSM_GROUPS * SSM_STATE
CHUNK_BLOCKS = 4
META_ROWS = 8
M_OWNER, M_FIRST, M_SIZE, M_NCHUNK, M_LAST, M_NBLK, M_NUSED, M_OWNER_CHUNKS = range(8)
VMEM_LIMIT = 56 * 1024 * 1024

_NT = (((1,), (1,)), ((), ()))


def _cparams(*sem):
    return pltpu.CompilerParams(dimension_semantics=sem, vmem_limit_bytes=VMEM_LIMIT)


def _rms(x, gain):
    ms = jnp.mean(x * x, axis=-1, keepdims=True)
    return x * lax.rsqrt(ms + EPS) * gain


def _t5_bucket_table():
    qi = np.arange(BLOCK, dtype=np.int32)[:, None]
    ki = np.arange(2 * BLOCK, dtype=np.int32)[None, :]
    delta = BLOCK + qi - ki
    n = np.maximum(delta, 0)
    max_exact = REL_BUCKETS // 2
    nf = np.maximum(n, 1).astype(np.float32)
    large = max_exact + (np.log(nf / np.float32(max_exact)) / np.float32(math.log(REL_MAX_DIST / max_exact))
                         * np.float32(REL_BUCKETS - max_exact)).astype(np.int32)
    large = np.minimum(large, REL_BUCKETS - 1)
    return np.where(n < max_exact, n, large).astype(np.int32)


def _upper(cols):
    k = lax.broadcasted_iota(jnp.int32, (BLOCK, cols), 0)
    q = lax.broadcasted_iota(jnp.int32, (BLOCK, cols), 1) % BLOCK
    return k > q


def _bias_kernel(rb_ref, bucket_ref, out_ref):
    pair = pl.program_id(0)
    bucket = bucket_ref[...]
    upper = _upper(BLOCK)
    for half in range(2):
        acc = jnp.zeros(bucket.shape, F32)
        for b in range(REL_BUCKETS):
            acc = jnp.where(bucket == b, rb_ref[b, 2 * pair + half], acc)
        cols = slice(half * BLOCK, (half + 1) * BLOCK)
        out_ref[0, 0, :, cols] = jnp.where(upper, acc[:BLOCK], acc[BLOCK:])
        out_ref[1, 0, :, cols] = jnp.where(upper, jnp.float32(-1e30), acc[BLOCK:])


def _bias_table(rel_bias):
    bucket = jnp.asarray(_t5_bucket_table().T)
    pairs = ATTN_HEADS // 2
    return pl.pallas_call(
        _bias_kernel,
        grid=(pairs,),
        in_specs=[pl.BlockSpec(memory_space=pltpu.SMEM),
                  pl.BlockSpec((2 * BLOCK, BLOCK), lambda h: (0, 0))],
        out_specs=pl.BlockSpec((2, 1, BLOCK, 2 * BLOCK), lambda h: (0, h, 0, 0)),
        out_shape=jax.ShapeDtypeStruct((2, pairs, BLOCK, 2 * BLOCK), F32),
        compiler_params=_cparams("arbitrary"),
        name="t5_bias_table",
    )(rel_bias.astype(F32), bucket)


def _ssm_prep_kernel(lr_ref, li_ref, ldt_ref, br_ref, bi_ref, are_ref, aim_ref, bbr_ref, bbi_ref):
    lr = lr_ref[...]
    li = li_ref[...]
    dt = jnp.exp(ldt_ref[...])
    mag = jnp.exp(lr * dt)
    a_re = mag * jnp.cos(li * dt)
    a_im = mag * jnp.sin(li * dt)
    den = lr * lr + li * li
    nr = a_re - 1.0
    ni = a_im
    coef_re = (nr * lr + ni * li) / den
    coef_im = (ni * lr - nr * li) / den
    are_ref[...] = a_re
    aim_ref[...] = a_im
    br = br_ref[...]
    bi = bi_ref[...]
    bbr_ref[...] = coef_re * br - coef_im * bi
    bbi_ref[...] = coef_re * bi + coef_im * br


def _ssm_prep(lam_re, lam_im, log_dt, b_re, b_im):
    g, n, c = b_re.shape
    vec = jax.ShapeDtypeStruct((g, 1, n), F32)
    mat = jax.ShapeDtypeStruct((g, c, n), F32)
    return pl.pallas_call(
        _ssm_prep_kernel,
        out_shape=(vec, vec, mat, mat),
        name="ssm_discretise",
    )(lam_re.astype(F32).reshape(g, 1, n), lam_im.astype(F32).reshape(g, 1, n),
      log_dt.astype(F32).reshape(g, 1, 1),
      jnp.transpose(b_re.astype(F32), (0, 2, 1)), jnp.transpose(b_im.astype(F32), (0, 2, 1)))


def _block_diag_tiles(m):
    g, r, c = m.shape
    t = g // SSM_GROUPS_PER_TILE
    eye = jnp.eye(SSM_GROUPS_PER_TILE, dtype=m.dtype)
    m4 = m.reshape(t, SSM_GROUPS_PER_TILE, r, c)
    out = m4[:, :, :, None, :] * eye[None, :, None, :, None]
    return out.reshape(t, SSM_GROUPS_PER_TILE * r, SSM_GROUPS_PER_TILE * c)


def _inproj_kernel(x_ref, g_ref, w_ref, q_ref, k_ref, v_ref, u_ref, qx_ref):
    h = _rms(x_ref[...], g_ref[...]).astype(BF16)
    c0 = 0
    for ref in (q_ref, k_ref, v_ref, u_ref, qx_ref):
        transposed = ref is q_ref or ref is v_ref
        c1 = c0 + (ref.shape[0] if transposed else ref.shape[1])
        p = jnp.dot(h, w_ref[:, c0:c1], preferred_element_type=F32)
        if transposed:
            p = p.T
        ref[...] = p.astype(ref.dtype)
        c0 = c1


def _inproj(x2, gain, w_cols, batch, seq, tm):
    t, d = x2.shape
    nsb = seq // tm
    kvw = ATTN_KV_HEADS * ATTN_HEAD_DIM
    ncol = w_cols.shape[1]
    row = lambda w: pl.BlockSpec((tm, w), lambda i: (i, 0))
    return pl.pallas_call(
        _inproj_kernel,
        grid=(t // tm,),
        in_specs=[row(d),
                  pl.BlockSpec((1, d), lambda i: (0, 0)),
                  pl.BlockSpec((d, ncol), lambda i: (0, 0), pipeline_mode=pl.Buffered(1))],
        out_specs=[pl.BlockSpec((ATTN_WIDTH, tm), lambda i: (0, i)),
                   row(kvw), pl.BlockSpec((kvw, tm), lambda i: (0, i)),
                   pl.BlockSpec((tm, SSM_WIDTH), lambda i: (i % nsb, i // nsb)),
                   row(XATTN_WIDTH)],
        out_shape=[jax.ShapeDtypeStruct((ATTN_WIDTH, t), BF16),
                   jax.ShapeDtypeStruct((t, kvw), BF16),
                   jax.ShapeDtypeStruct((kvw, t), BF16),
                   jax.ShapeDtypeStruct((seq, batch * SSM_WIDTH), BF16),
                   jax.ShapeDtypeStruct((t, XATTN_WIDTH), BF16)],
        compiler_params=_cparams("arbitrary"),
        name="in_proj",
    )(x2, gain, w_cols)


def _swa_kernel(sinks_ref, q_ref, kp_ref, kc_ref, vp_ref, vc_ref, bias_ref, qg_ref, kg_ref, og_ref, out_ref):
    dh = ATTN_HEAD_DIM
    kcat = jnp.concatenate([kp_ref[...], kc_ref[...]], axis=0).astype(F32)
    lo = lax.broadcasted_iota(jnp.int32, (1, ATTN_KV_HEADS * dh), 1) < dh
    sq = kcat * kcat
    s_lo = jnp.sum(jnp.where(lo, sq, 0.0), axis=-1, keepdims=True)
    s_hi = jnp.sum(sq, axis=-1, keepdims=True) - s_lo
    kinv = jnp.where(lo, lax.rsqrt(s_lo / dh + EPS), lax.rsqrt(s_hi / dh + EPS))
    kn = (kcat * kinv * kg_ref[...]).astype(BF16)
    v_t = jnp.concatenate([vp_ref[...], vc_ref[...]], axis=1)
    upper = _upper(2 * BLOCK)
    first_head = lax.broadcasted_iota(jnp.int32, (1, 2 * BLOCK), 1) < BLOCK
    zeros = jnp.zeros((dh, BLOCK), F32)
    heads_per_kv = ATTN_HEADS // ATTN_KV_HEADS
    pairs = range(ATTN_HEADS // 2)
    kv_of = lambda pair: (2 * pair) // heads_per_kv

    logits = []
    for pair in pairs:
        cols = []
        for half in range(2):
            hd = 2 * pair + half
            qh = q_ref[hd * dh:(hd + 1) * dh, :].astype(F32)
            ms = jnp.mean(qh * qh, axis=0, keepdims=True)
            qn = qh * lax.rsqrt(ms + EPS) * qg_ref[...]
            cols.append(jnp.concatenate([qn, zeros] if kv_of(pair) == 0 else [zeros, qn], axis=0))
        rhs = jnp.concatenate(cols, axis=1).astype(BF16)
        logits.append(jnp.dot(kn, rhs, preferred_element_type=F32))

    probs, dens = [], []
    for pair in pairs:
        both = logits[pair]
        l = jnp.where(upper, both[:BLOCK], both[BLOCK:]) + bias_ref[0, pair]
        sink = jnp.where(first_head, sinks_ref[2 * pair], sinks_ref[2 * pair + 1])
        m = jnp.maximum(jnp.max(l, axis=0, keepdims=True), sink)
        p = jnp.exp(l - m)
        dens.append(jnp.sum(p, axis=0, keepdims=True) + jnp.exp(sink - m))
        probs.append(jnp.concatenate([jnp.where(upper, p, 0.0), jnp.where(upper, 0.0, p)], axis=0).astype(BF16))

    outs = []
    for pair in pairs:
        g = kv_of(pair)
        o = jnp.dot(v_t[g * dh:(g + 1) * dh, :], probs[pair], preferred_element_type=F32) / dens[pair]
        outs += [o[:, :BLOCK], o[:, BLOCK:]]
    y_t = jnp.concatenate(outs, axis=0)
    ms = jnp.mean(y_t * y_t, axis=0, keepdims=True)
    out_ref[...] = (y_t * lax.rsqrt(ms + EPS) * og_ref[...]).T.astype(out_ref.dtype)


def _swa(q_t, k, v, bias, sinks, qgain, kgain, ogain, batch, seq):
    nb = seq // BLOCK
    kvw = ATTN_KV_HEADS * ATTN_HEAD_DIM
    cur = lambda b, n: (b * nb + n, 0)
    prev = lambda b, n: (b * nb + jnp.maximum(n - 1, 0), 0)
    const2 = lambda b, n: (0, 0)
    return pl.pallas_call(
        _swa_kernel,
        grid=(batch, nb),
        in_specs=[pl.BlockSpec(memory_space=pltpu.SMEM),
                  pl.BlockSpec((ATTN_WIDTH, BLOCK), lambda b, n: (0, b * nb + n)),
                  pl.BlockSpec((BLOCK, kvw), prev), pl.BlockSpec((BLOCK, kvw), cur),
                  pl.BlockSpec((kvw, BLOCK), lambda b, n: (0, b * nb + jnp.maximum(n - 1, 0))),
                  pl.BlockSpec((kvw, BLOCK), lambda b, n: (0, b * nb + n)),
                  pl.BlockSpec((1, ATTN_HEADS // 2, BLOCK, 2 * BLOCK),
                               lambda b, n: (jnp.where(n == 0, 1, 0), 0, 0, 0)),
                  pl.BlockSpec((ATTN_HEAD_DIM, BLOCK), const2), pl.BlockSpec((1, kvw), const2),
                  pl.BlockSpec((ATTN_WIDTH, BLOCK), const2)],
        out_specs=pl.BlockSpec((BLOCK, ATTN_WIDTH), cur),
        out_shape=jax.ShapeDtypeStruct((batch * seq, ATTN_WIDTH), BF16),
        compiler_params=_cparams("arbitrary", "arbitrary"),
        name="swa_attention",
    )(sinks, q_t, k, k, v, v, bias, qgain, kgain, ogain)


def _memkv_kernel(m_ref, g_ref, w_ref, kg_ref, k_ref, v_ref):
    h = _rms(m_ref[...], g_ref[...]).astype(BF16)
    km = jnp.dot(h, w_ref[:, :XATTN_WIDTH], preferred_element_type=F32)
    for hd in range(XATTN_HEADS):
        sl = slice(hd * XATTN_HEAD_DIM, (hd + 1) * XATTN_HEAD_DIM)
        k_ref[:, sl] = _rms(km[:, sl], kg_ref[...]).astype(k_ref.dtype)
    v_ref[...] = jnp.dot(h, w_ref[:, XATTN_WIDTH:], preferred_element_type=F32).astype(v_ref.dtype)


def _memkv(mem2, gain, w_kv, kgain, tm):
    r, d = mem2.shape
    row = lambda w: pl.BlockSpec((tm, w), lambda i: (i, 0))
    const = lambda shape: pl.BlockSpec(shape, lambda i: (0, 0))
    return pl.pallas_call(
        _memkv_kernel,
        grid=(r // tm,),
        in_specs=[row(d), const((1, d)), const((d, 2 * XATTN_WIDTH)), const((1, XATTN_HEAD_DIM))],
        out_specs=[row(XATTN_WIDTH), row(XATTN_WIDTH)],
        out_shape=[jax.ShapeDtypeStruct((r, XATTN_WIDTH), BF16)] * 2,
        compiler_params=_cparams("arbitrary"),
        name="mem_kv_proj",
    )(mem2, gain, w_kv, kgain)


def _xattn_kernel(q_ref, k_ref, v_ref, qg_ref, og_ref, out_ref):
    outs = []
    for hd in range(XATTN_HEADS):
        sl = slice(hd * XATTN_HEAD_DIM, (hd + 1) * XATTN_HEAD_DIM)
        qn = _rms(q_ref[:, sl].astype(F32), qg_ref[...]).astype(BF16)
        l = lax.dot_general(qn, k_ref[:, sl], _NT, preferred_element_type=F32)
        m = jnp.max(l, axis=-1, keepdims=True)
        p = jnp.exp(l - m)
        den = jnp.sum(p, axis=-1, keepdims=True)
        outs.append(jnp.dot(p.astype(BF16), v_ref[:, sl], preferred_element_type=F32) / den)
    y = jnp.concatenate(outs, axis=-1)
    out_ref[...] = _rms(y, og_ref[...]).astype(out_ref.dtype)


def _xattn(qx, km, vm, qgain, ogain, batch, seq, mem_len, tq):
    nq = seq // tq
    const2 = lambda b, i: (0, 0)
    return pl.pallas_call(
        _xattn_kernel,
        grid=(batch, nq),
        in_specs=[pl.BlockSpec((tq, XATTN_WIDTH), lambda b, i: (b * nq + i, 0)),
                  pl.BlockSpec((mem_len, XATTN_WIDTH), lambda b, i: (b, 0)),
                  pl.BlockSpec((mem_len, XATTN_WIDTH), lambda b, i: (b, 0)),
                  pl.BlockSpec((1, XATTN_HEAD_DIM), const2),
                  pl.BlockSpec((1, XATTN_WIDTH), const2)],
        out_specs=pl.BlockSpec((tq, XATTN_WIDTH), lambda b, i: (b * nq + i, 0)),
        out_shape=jax.ShapeDtypeStruct((batch * seq, XATTN_WIDTH), BF16),
        compiler_params=_cparams("arbitrary", "arbitrary"),
        name="mem_xattn",
    )(qx, km, vm, qgain, ogain)


def _ssm_kernel(u_ref, bmat_ref, cmat_ref, are_ref, aim_ref, d_ref, wglu_ref, og_ref, out_ref,
                tb_ref, xr_ref, xi_ref, sr_ref, si_ref, *, batch, steps):
    @pl.when(pl.program_id(0) == 0)
    def _():
        sr_ref[...] = jnp.zeros_like(sr_ref)
        si_ref[...] = jnp.zeros_like(si_ref)

    for b in range(batch):
        for j in range(SSM_TILES):
            c0 = b * SSM_WIDTH + j * LANES
            tb_ref[j, pl.ds(b, steps, stride=batch), :] = u_ref[:, c0:c0 + LANES].astype(F32)
    uf = jnp.concatenate([tb_ref[j] for j in range(SSM_TILES)], axis=-1)
    u = uf.astype(BF16)
    for j in range(SSM_TILES):
        bu = jnp.dot(u[:, j * LANES:(j + 1) * LANES], bmat_ref[j], preferred_element_type=F32)
        xr_ref[:, j * SSM_TILE_STATE:(j + 1) * SSM_TILE_STATE] = bu[:, :SSM_TILE_STATE]
        xi_ref[:, j * SSM_TILE_STATE:(j + 1) * SSM_TILE_STATE] = bu[:, SSM_TILE_STATE:]

    chunk = 1024
    for c0 in range(0, SSM_STATES, chunk):
        cs = slice(c0, c0 + chunk)
        ar = jnp.broadcast_to(are_ref[:, cs], (batch, chunk))
        ai = jnp.broadcast_to(aim_ref[:, cs], (batch, chunk))

        def step(t, carry, cs=cs, ar=ar, ai=ai):
            s_r, s_i = carry
            rows = pl.ds(pl.multiple_of(t * batch, batch), batch)
            n_r = ar * s_r - ai * s_i + xr_ref[rows, cs]
            n_i = ar * s_i + ai * s_r + xi_ref[rows, cs]
            xr_ref[rows, cs] = n_r
            xi_ref[rows, cs] = n_i
            return n_r, n_i

        s_r, s_i = lax.fori_loop(0, steps, step, (sr_ref[:, cs], si_ref[:, cs]))
        sr_ref[:, cs] = s_r
        si_ref[:, cs] = s_i

    ys = []
    for j in range(SSM_TILES):
        sl = slice(j * SSM_TILE_STATE, (j + 1) * SSM_TILE_STATE)
        xcat = jnp.concatenate([xr_ref[:, sl], xi_ref[:, sl]], axis=-1).astype(BF16)
        ys.append(jnp.dot(xcat, cmat_ref[j], preferred_element_type=F32))
    y = jnp.concatenate(ys, axis=-1) + d_ref[...] * uf
    y = jax.nn.gelu(y)
    y = y * jax.nn.sigmoid(jnp.dot(y.astype(BF16), wglu_ref[...], preferred_element_type=F32))
    y = _rms(y, og_ref[...])
    for j in range(SSM_TILES):
        tb_ref[j] = y[:, j * LANES:(j + 1) * LANES]
    for b in range(batch):
        for j in range(SSM_TILES):
            c0 = b * SSM_WIDTH + j * LANES
            out_ref[:, c0:c0 + LANES] = tb_ref[j, pl.ds(b, steps, stride=batch), :].astype(out_ref.dtype)


def _ssm(u_sb, bmat, cmat, a_re, a_im, d_skip, w_glu, ogain, batch, seq, steps):
    rows = steps * batch
    const2 = lambda c: (0, 0)
    const3 = lambda c: (0, 0, 0)
    return pl.pallas_call(
        functools.partial(_ssm_kernel, batch=batch, steps=steps),
        grid=(seq // steps,),
        in_specs=[pl.BlockSpec((steps, batch * SSM_WIDTH), lambda c: (c, 0)),
                  pl.BlockSpec(bmat.shape, const3), pl.BlockSpec(cmat.shape, const3),
                  pl.BlockSpec((1, SSM_STATES), const2), pl.BlockSpec((1, SSM_STATES), const2),
                  pl.BlockSpec((1, SSM_WIDTH), const2),
                  pl.BlockSpec((SSM_WIDTH, SSM_WIDTH), const2),
                  pl.BlockSpec((1, SSM_WIDTH), const2)],
        out_specs=pl.BlockSpec((steps, batch * SSM_WIDTH), lambda c: (c, 0)),
        out_shape=jax.ShapeDtypeStruct((seq, batch * SSM_WIDTH), BF16),
        scratch_shapes=[pltpu.VMEM((SSM_TILES, rows, LANES), F32),
                        pltpu.VMEM((rows, SSM_STATES), F32), pltpu.VMEM((rows, SSM_STATES), F32),
                        pltpu.VMEM((batch, SSM_STATES), F32), pltpu.VMEM((batch, SSM_STATES), F32)],
        compiler_params=_cparams("arbitrary"),
        name="s5_layer",
    )(u_sb, bmat, cmat, a_re, a_im, d_skip, w_glu, ogain)


def _outproj_kernel(ya_ref, ys_ref, yx_ref, x_ref, wo_ref, g_ref, wr_ref, x1_ref, lt_ref, hp_ref, *, sub):
    for r0 in range(0, x_ref.shape[0], sub):
        rows = slice(r0, r0 + sub)
        mix = jnp.concatenate([ya_ref[rows, :], ys_ref[rows, :], yx_ref[rows, :]], axis=-1)
        x1 = x_ref[rows, :] + jnp.dot(mix, wo_ref[...], preferred_element_type=F32)
        x1_ref[rows, :] = x1
        h2 = _rms(x1, g_ref[...])
        hi = h2.astype(BF16)
        hi32 = hi.astype(F32)
        lo = (h2 - hi32).astype(BF16)
        both = jnp.dot(hi, wr_ref[...], preferred_element_type=F32)
        lg = both[:, :LANES] + both[:, LANES:] + jnp.dot(lo, wr_ref[:, :LANES], preferred_element_type=F32)
        lt_ref[rows, :] = lg
        hp_ref[rows, :] = _pack_rows(hi32)


def _outproj(ya, ys_sb, yx, x2, w_o, gain, wr, batch, seq, tm):
    t, d = x2.shape
    nsb = seq // tm
    row = lambda w: pl.BlockSpec((tm, w), lambda i: (i, 0))
    const = lambda shape: pl.BlockSpec(shape, lambda i: (0, 0), pipeline_mode=pl.Buffered(1))
    return pl.pallas_call(
        functools.partial(_outproj_kernel, sub=min(256, tm)),
        grid=(t // tm,),
        in_specs=[row(ATTN_WIDTH),
                  pl.BlockSpec((tm, SSM_WIDTH), lambda i: (i % nsb, i // nsb)),
                  row(XATTN_WIDTH), row(d),
                  const(w_o.shape), const((1, d)), const(wr.shape)],
        out_specs=[row(d), row(LANES), row(d // 2)],
        out_shape=[jax.ShapeDtypeStruct((t, d), F32), jax.ShapeDtypeStruct((t, LANES), F32),
                   jax.ShapeDtypeStruct((t, d // 2), jnp.uint32)],
        compiler_params=_cparams("arbitrary"),
        name="out_proj_router",
    )(ya, ys_sb, yx, x2, w_o, gain, wr)


def _route_kernel(lt_ref, tri_ref, dest_ref, w_ref, meta_ref, cnt_ref, carry_ref, pstart_ref):
    phase = pl.program_id(0)
    c = pl.program_id(1)
    logits = lt_ref[...].T
    tc = logits.shape[1]
    ng, epg = N_EXPERT_GROUPS, EXPERTS_PER_GROUP
    row8 = lax.broadcasted_iota(jnp.int32, (ng, tc), 0)

    gl = logits[0:ng]
    gmax = jnp.max(gl, axis=0, keepdims=True)
    gidx = jnp.min(jnp.where(gl == gmax, row8, ng), axis=0, keepdims=True)
    gate = 1.0 / jnp.sum(jnp.exp(gl - gmax), axis=0, keepdims=True)
    sel = jnp.zeros((epg, tc), F32)
    for g in range(ng):
        sel = jnp.where(gidx == g, logits[ng + g * epg:ng + (g + 1) * epg], sel)
    v1 = jnp.max(sel, axis=0, keepdims=True)
    i1 = jnp.min(jnp.where(sel == v1, row8, epg), axis=0, keepdims=True)
    sel2 = jnp.where(row8 == i1, -jnp.inf, sel)
    v2 = jnp.max(sel2, axis=0, keepdims=True)
    i2 = jnp.min(jnp.where(sel2 == v2, row8, epg), axis=0, keepdims=True)
    e = jnp.exp(v2 - v1)
    w1 = gate * (1.0 / (1.0 + e))
    w2 = gate * (e / (1.0 + e))
    e1 = gidx * epg + i1
    e2 = gidx * epg + i2
    rowe = lax.broadcasted_iota(jnp.int32, (N_EXPERTS, tc), 0)
    oh1 = rowe == e1
    oh2 = rowe == e2
    member = jnp.where(jnp.logical_or(oh1, oh2), 1.0, 0.0)
    chunk_cnt = jnp.sum(member, axis=1, keepdims=True)

    @pl.when(phase == 0)
    def _():
        @pl.when(c == 0)
        def _():
            cnt_ref[...] = jnp.zeros_like(cnt_ref)
        cnt_ref[...] += chunk_cnt

    @pl.when(phase == 1)
    def _():
        @pl.when(c == 0)
        def _():
            cnt = cnt_ref[...]
            nblk = jnp.floor((cnt + (MOE_BLOCK - 1)) * (1.0 / MOE_BLOCK))
            nchunk = jnp.floor((nblk + (CHUNK_BLOCKS - 1)) * (1.0 / CHUNK_BLOCKS))
            r = lax.broadcasted_iota(jnp.int32, (N_EXPERTS, LANES), 0)
            cidx = lax.broadcasted_iota(jnp.int32, (N_EXPERTS, LANES), 1)
            to_row = lambda col: jnp.sum(jnp.where(r == cidx, col, 0.0), axis=0, keepdims=True)
            cumsum_col = lambda col: jnp.sum(jnp.where(cidx <= r, to_row(col), 0.0), axis=1, keepdims=True)
            cumsum_row = lambda col: jnp.sum(jnp.where(r <= cidx, col, 0.0), axis=0, keepdims=True)
            bend = cumsum_col(nblk)
            bstart = bend - nblk
            cend = cumsum_col(nchunk)
            cstart = cend - nchunk
            pstart_ref[...] = bstart * MOE_BLOCK
            carry_ref[...] = jnp.zeros_like(carry_ref)
            lanef = lax.broadcasted_iota(jnp.int32, (1, LANES), 1).astype(F32)
            owner = jnp.minimum(jnp.sum(jnp.where(cend <= lanef, 1.0, 0.0), axis=0, keepdims=True),
                                N_EXPERTS - 1.0)
            own = r.astype(F32) == owner
            pick = lambda col: jnp.sum(jnp.where(own, col, 0.0), axis=0, keepdims=True)
            idx = lanef - pick(cstart)
            first = pick(bstart) + CHUNK_BLOCKS * idx
            size = jnp.clip(pick(nblk) - CHUNK_BLOCKS * idx, 0.0, float(CHUNK_BLOCKS))
            zero = jnp.zeros((1, LANES), F32)
            rows = [owner, first, size,
                    zero + jnp.sum(nchunk, axis=0, keepdims=True),
                    cumsum_row(nblk) - 1.0,
                    to_row(nblk),
                    zero + jnp.sum(nblk, axis=0, keepdims=True),
                    pick(nchunk)]
            for k, v in enumerate(rows):
                meta_ref[k:k + 1, :] = v.astype(jnp.int32)

        before = carry_ref[...] + jnp.dot(member.astype(BF16), tri_ref[...], preferred_element_type=F32)
        pos = before + pstart_ref[...]
        dest_ref[0:1, :] = jnp.sum(jnp.where(oh1, pos, 0.0), axis=0, keepdims=True).astype(jnp.int32)
        dest_ref[1:2, :] = jnp.sum(jnp.where(oh2, pos, 0.0), axis=0, keepdims=True).astype(jnp.int32)
        w_ref[0:1, :] = w1
        w_ref[1:2, :] = w2
        carry_ref[...] += chunk_cnt


def _route(logits_t, tc):
    t = logits_t.shape[0]
    nc = t // tc
    tri = jnp.asarray(np.triu(np.ones((tc, tc), np.float32), k=1), dtype=BF16)
    return pl.pallas_call(
        _route_kernel,
        grid=(2, nc),
        in_specs=[pl.BlockSpec((tc, LANES), lambda p, c: (c, 0)),
                  pl.BlockSpec((tc, tc), lambda p, c: (0, 0))],
        out_specs=[pl.BlockSpec((2, tc), lambda p, c: (0, c * p)),
                   pl.BlockSpec((2, tc), lambda p, c: (0, c * p)),
                   pl.BlockSpec((META_ROWS, LANES), lambda p, c: (0, 0))],
        out_shape=[jax.ShapeDtypeStruct((2, t), jnp.int32), jax.ShapeDtypeStruct((2, t), F32),
                   jax.ShapeDtypeStruct((META_ROWS, LANES), jnp.int32)],
        scratch_shapes=[pltpu.VMEM((N_EXPERTS, 1), F32)] * 3,
        compiler_params=_cparams("arbitrary", "arbitrary"),
        name="moe_route",
    )(logits_t, tri)


def _meta(meta_ref, row, lane=0):
    return meta_ref[row * LANES + lane]


def _fill_blocks(meta_ref, zbuf, dst_ref, sem, n_blocks, *, expert_tails):
    zbuf[...] = jnp.zeros_like(zbuf)
    n_used = _meta(meta_ref, M_NUSED)
    block = lambda b: pltpu.make_async_copy(zbuf, dst_ref.at[pl.ds(b * MOE_BLOCK, MOE_BLOCK), :], sem)

    def tails(fn):
        def body(e, carry):
            @pl.when(_meta(meta_ref, M_NBLK, e) > 0)
            def _():
                fn(block(_meta(meta_ref, M_LAST, e)))
            return carry
        lax.fori_loop(0, N_EXPERTS, body, 0)

    def unused(fn):
        def body(b, carry):
            fn(block(b))
            return carry
        lax.fori_loop(n_used, n_blocks, body, 0)

    for phase in (lambda cp: cp.start(), lambda cp: cp.wait()):
        if expert_tails:
            tails(phase)
        unused(phase)


def _dispatch_kernel(dest_ref, meta_ref, h_ref, xs_ref, zbuf, sem_z, sem, *, tokens, n_blocks):
    tm = h_ref.shape[0]

    @pl.when(pl.program_id(0) == 0)
    def _():
        _fill_blocks(meta_ref, zbuf, xs_ref, sem_z, n_blocks, expert_tails=True)

    base = pl.program_id(0) * tm

    def issue(g, carry):
        r8 = pl.multiple_of(g * SUBLANES, SUBLANES)
        for s in range(SUBLANES):
            for k in range(2):
                d = dest_ref[k * tokens + base + r8 + s]
                pltpu.make_async_copy(h_ref.at[pl.ds(r8 + s, 1), :], xs_ref.at[pl.ds(d, 1), :], sem).start()
        return carry
    lax.fori_loop(0, tm // SUBLANES, issue, 0)
    for k in range(2):
        pltpu.make_async_copy(h_ref, xs_ref.at[pl.ds(0, tm), :], sem).wait()


def _dispatch(dest_flat, meta_flat, h2p, n_blocks, tm):
    t, w = h2p.shape
    grid_spec = pltpu.PrefetchScalarGridSpec(
        num_scalar_prefetch=2,
        grid=(t // tm,),
        in_specs=[pl.BlockSpec((tm, w), lambda i, d, m: (i, 0))],
        out_specs=pl.BlockSpec(memory_space=pl.ANY),
        scratch_shapes=[pltpu.VMEM((MOE_BLOCK, w), h2p.dtype),
                        pltpu.SemaphoreType.DMA(()), pltpu.SemaphoreType.DMA(())],
    )
    return pl.pallas_call(
        functools.partial(_dispatch_kernel, tokens=t, n_blocks=n_blocks),
        grid_spec=grid_spec,
        out_shape=jax.ShapeDtypeStruct((n_blocks * MOE_BLOCK, w), h2p.dtype),
        compiler_params=_cparams("arbitrary"),
        name="moe_dispatch",
    )(dest_flat, meta_flat, h2p)


def _pack_rows(x):
    bits = lax.bitcast_convert_type(x, jnp.uint32)
    half = x.shape[1] // 2
    return (bits[:, half:] & jnp.uint32(0xFFFF0000)) | (bits[:, :half] >> 16)


def _unpack_halves(words):
    lo = lax.bitcast_convert_type(words << 16, F32)
    hi = lax.bitcast_convert_type(words & jnp.uint32(0xFFFF0000), F32)
    return lo, hi


def _unpack_rows(words):
    return jnp.concatenate(_unpack_halves(words), axis=-1).astype(BF16)


def _expert_kernel(meta_ref, xs_ref, wg_ref, wu_ref, wd_ref, yb_ref,
                   xbuf, ybuf, zbuf, wg_f32, wu_f32, wd_f32, wslot_ref,
                   sem_in, sem_out, sem_z, sem_w, *, n_blocks):
    c = pl.program_id(0)
    n_chunks = _meta(meta_ref, M_NCHUNK)
    slot = c % 2

    def weight_copies(k, s):
        e = _meta(meta_ref, M_OWNER, k)
        return [pltpu.make_async_copy(src.at[e], dst.at[s], sem_w.at[s])
                for src, dst in ((wg_ref, wg_f32), (wu_ref, wu_f32), (wd_ref, wd_f32))]

    def in_copy(k, s, nb):
        rows = nb * MOE_BLOCK
        src = xs_ref.at[pl.ds(_meta(meta_ref, M_FIRST, k) * MOE_BLOCK, rows), :]
        return pltpu.make_async_copy(src, xbuf.at[s, pl.ds(0, rows), :], sem_in.at[s])

    def out_copy(k, s, nb):
        rows = nb * MOE_BLOCK
        dst = yb_ref.at[pl.ds(_meta(meta_ref, M_FIRST, k) * MOE_BLOCK, rows), :]
        return pltpu.make_async_copy(ybuf.at[s, pl.ds(0, rows), :], dst, sem_out.at[s])

    def by_size(k, fn):
        for nb in range(1, CHUNK_BLOCKS + 1):
            pl.when(_meta(meta_ref, M_SIZE, k) == nb)(functools.partial(fn, nb))

    @pl.when(c == 0)
    def _():
        by_size(0, lambda nb: in_copy(0, 0, nb).start())

    @pl.when(c + 1 < n_chunks)
    def _():
        by_size(c + 1, lambda nb: in_copy(c + 1, 1 - slot, nb).start())

    @pl.when(c < n_chunks)
    def _():
        prev = jnp.maximum(c - 1, 0)
        new_expert = jnp.logical_or(c == 0, _meta(meta_ref, M_OWNER, c) != _meta(meta_ref, M_OWNER, prev))

        @pl.when(c == 0)
        def _():
            wslot_ref[0] = 0
            for cp in weight_copies(0, 0):
                cp.start()

        @pl.when(new_expert)
        def _():
            ws = jnp.where(c == 0, 0, 1 - wslot_ref[0])
            wslot_ref[0] = ws
            nxt = c + _meta(meta_ref, M_OWNER_CHUNKS, c)

            @pl.when(nxt < n_chunks)
            def _():
                for cp in weight_copies(nxt, 1 - ws):
                    cp.start()
            for cp in weight_copies(c, ws):
                cp.wait()

        @pl.when(c >= 2)
        def _():
            by_size(c - 2, lambda nb: out_copy(c - 2, slot, nb).wait())

        def compute(nb):
            rows = nb * MOE_BLOCK
            ws = wslot_ref[0]
            in_copy(c, slot, nb).wait()
            h = _unpack_rows(xbuf[slot, 0:rows, :])
            gate = jnp.dot(h, wg_f32[ws].astype(BF16), preferred_element_type=F32)
            up = jnp.dot(h, wu_f32[ws].astype(BF16), preferred_element_type=F32)
            act = (jax.nn.silu(gate) * up).astype(BF16)
            y = jnp.dot(act, wd_f32[ws].astype(BF16), preferred_element_type=F32)
            ybuf[slot, 0:rows, :] = _pack_rows(y.astype(BF16).astype(F32))
            out_copy(c, slot, nb).start()
        by_size(c, compute)

    @pl.when(c == pl.num_programs(0) - 1)
    def _():
        _fill_blocks(meta_ref, zbuf, yb_ref, sem_z, n_blocks, expert_tails=False)
        for back in (2, 1):
            @pl.when(n_chunks >= back)
            def _(back=back):
                k = n_chunks - back
                by_size(k, lambda nb: out_copy(k, k % 2, nb).wait())


def _experts(meta_flat, xs, w_gate, w_up, w_down, n_blocks, n_chunks_max):
    d, de = w_gate.shape[1], w_gate.shape[2]
    rows = CHUNK_BLOCKS * MOE_BLOCK

    hbm = pl.BlockSpec(memory_space=pl.ANY)
    grid_spec = pltpu.PrefetchScalarGridSpec(
        num_scalar_prefetch=1,
        grid=(n_chunks_max,),
        in_specs=[hbm, hbm, hbm, hbm],
        out_specs=hbm,
        scratch_shapes=[pltpu.VMEM((2, rows, xs.shape[1]), xs.dtype),
                        pltpu.VMEM((2, rows, d // 2), jnp.uint32),
                        pltpu.VMEM((MOE_BLOCK, d // 2), jnp.uint32),
                        pltpu.VMEM((2, d, de), F32), pltpu.VMEM((2, d, de), F32), pltpu.VMEM((2, de, d), F32),
                        pltpu.SMEM((1,), jnp.int32),
                        pltpu.SemaphoreType.DMA((2,)), pltpu.SemaphoreType.DMA((2,)),
                        pltpu.SemaphoreType.DMA(()), pltpu.SemaphoreType.DMA((2,))],
    )
    return pl.pallas_call(
        functools.partial(_expert_kernel, n_blocks=n_blocks),
        grid_spec=grid_spec,
        out_shape=jax.ShapeDtypeStruct((n_blocks * MOE_BLOCK, d // 2), jnp.uint32),
        compiler_params=_cparams("arbitrary"),
        name="moe_experts",
    )(meta_flat, xs, w_gate, w_up, w_down)


def _combine_kernel(dest_ref, yb_ref, x1_ref, w_ref, out_ref, gbuf, sem, *, tokens):
    i = pl.program_id(0)
    tm = x1_ref.shape[0]
    slot = i % 2

    def gather(tile, s):
        base = tile * tm

        def issue(g, carry):
            r8 = pl.multiple_of(g * SUBLANES, SUBLANES)
            for sub in range(SUBLANES):
                for k in range(2):
                    d = dest_ref[k * tokens + base + r8 + sub]
                    pltpu.make_async_copy(yb_ref.at[pl.ds(d, 1), :], gbuf.at[s, k, pl.ds(r8 + sub, 1), :],
                                          sem.at[s]).start()
            return carry
        lax.fori_loop(0, tm // SUBLANES, issue, 0)

    @pl.when(i == 0)
    def _():
        gather(0, 0)

    @pl.when(i + 1 < pl.num_programs(0))
    def _():
        gather(i + 1, 1 - slot)

    for k in range(2):
        pltpu.make_async_copy(yb_ref.at[pl.ds(0, tm), :], gbuf.at[slot, k], sem.at[slot]).wait()
    w = w_ref[...]
    half = x1_ref.shape[1] // 2
    lo0, hi0 = _unpack_halves(gbuf[slot, 0])
    lo1, hi1 = _unpack_halves(gbuf[slot, 1])
    out_ref[:, :half] = x1_ref[:, :half] + (lo0 * w[:, 0:1] + lo1 * w[:, 1:2])
    out_ref[:, half:] = x1_ref[:, half:] + (hi0 * w[:, 0:1] + hi1 * w[:, 1:2])


def _combine(dest_flat, yb, x1, w_tok, tm):
    t, d = x1.shape
    grid_spec = pltpu.PrefetchScalarGridSpec(
        num_scalar_prefetch=1,
        grid=(t // tm,),
        in_specs=[pl.BlockSpec(memory_space=pl.ANY),
                  pl.BlockSpec((tm, d), lambda i, dr: (i, 0)),
                  pl.BlockSpec((tm, 2), lambda i, dr: (i, 0))],
        out_specs=pl.BlockSpec((tm, d), lambda i, dr: (i, 0)),
        scratch_shapes=[pltpu.VMEM((2, 2, tm, d // 2), jnp.uint32), pltpu.SemaphoreType.DMA((2,))],
    )
    return pl.pallas_call(
        functools.partial(_combine_kernel, tokens=t),
        grid_spec=grid_spec,
        out_shape=jax.ShapeDtypeStruct((t, d), F32),
        compiler_params=_cparams("arbitrary"),
        name="moe_combine",
    )(dest_flat, yb, x1, w_tok)


def _row(v):
    return v.astype(F32).reshape(1, -1)


def _layer(x2, mem2, batch, seq, mem_len, p):
    t, d = x2.shape

    tm_in = min(512, seq)
    q, kk, vv, u_sb, qx = _inproj(x2, _row(p["norm_mix"]), p["w_in"].astype(BF16), batch, seq, tm_in)

    bias = _bias_table(p["rel_bias"])
    col = lambda v: jnp.broadcast_to(v.astype(F32)[:, None], (v.shape[0], BLOCK))
    qgain = col(p["q_norm"]) * (1.0 / math.sqrt(ATTN_HEAD_DIM))
    kgain = jnp.tile(_row(p["k_norm"]), (1, ATTN_KV_HEADS))
    ya = _swa(q, kk, vv, bias, p["attn_sinks"].astype(F32), qgain, kgain, col(p["out_norm_attn"]), batch, seq)

    km, vm = _memkv(mem2, _row(p["mem_norm"]), p["w_mem_kv"].astype(BF16), _row(p["xk_norm"]),
                    min(256, mem2.shape[0]))
    xq_gain = _row(p["xq_norm"]) * (1.0 / math.sqrt(XATTN_HEAD_DIM))
    yx = _xattn(qx, km, vm, xq_gain, _row(p["out_norm_xattn"]), batch, seq, mem_len, min(512, seq))

    a_re, a_im, bbr, bbi = _ssm_prep(p["ssm_lambda_re"], p["ssm_lambda_im"], p["ssm_log_dt"],
                                     p["ssm_b_re"], p["ssm_b_im"])
    bmat = jnp.concatenate([_block_diag_tiles(bbr), _block_diag_tiles(bbi)], axis=-1).astype(BF16)
    c_re_t = jnp.transpose(p["ssm_c_re"].astype(F32), (0, 2, 1))
    c_im_t = jnp.transpose(p["ssm_c_im"].astype(F32), (0, 2, 1))
    cmat = jnp.concatenate([_block_diag_tiles(c_re_t), _block_diag_tiles(-c_im_t)], axis=1).astype(BF16)
    steps = min(64, seq)
    ys_sb = _ssm(u_sb, bmat, cmat,
                 a_re.reshape(1, SSM_STATES), a_im.reshape(1, SSM_STATES), _row(p["ssm_d"]),
                 p["ssm_w_glu"].astype(BF16), _row(p["out_norm_ssm"]), batch, seq, steps)

    wr = jnp.concatenate([p["w_router_group"], p["w_router_expert"]], axis=1).astype(F32)
    wr = jnp.pad(wr, ((0, 0), (0, LANES - wr.shape[1])))
    wr_hi = wr.astype(BF16)
    wr_lo = (wr - wr_hi.astype(F32)).astype(BF16)
    tm_out = min(512, seq)
    x1, logits_t, h2p = _outproj(ya, ys_sb, yx, x2, p["w_o"].astype(BF16), _row(p["norm_ffn"]),
                                 jnp.concatenate([wr_hi, wr_lo], axis=1), batch, seq, tm_out)

    dest, w_k, meta = _route(logits_t, min(1024, t))
    n_blocks = (2 * t) // MOE_BLOCK + N_EXPERTS
    n_chunks_max = (n_blocks + (CHUNK_BLOCKS - 1) * N_EXPERTS) // CHUNK_BLOCKS
    dest_flat = dest.reshape(2 * t)
    meta_flat = meta.reshape(META_ROWS * LANES)
    xs = _dispatch(dest_flat, meta_flat, h2p, n_blocks, min(1024, t))
    yb = _experts(meta_flat, xs, p["w_gate"], p["w_up"], p["w_down"], n_blocks, n_chunks_max)
    return _combine(dest_flat, yb, x1, w_k.T, min(256, t))


def kernel(x, mem, norm_mix, w_in, q_norm, k_norm, attn_sinks, rel_bias, ssm_lambda_re, ssm_lambda_im, ssm_log_dt, ssm_b_re, ssm_b_im, ssm_c_re, ssm_c_im, ssm_d, ssm_w_glu, mem_norm, w_mem_kv, xq_norm, xk_norm, out_norm_attn, out_norm_ssm, out_norm_xattn, w_o, norm_ffn, w_router_group, w_router_expert, w_gate, w_up, w_down):
    batch, seq, d = x.shape
    mem_len = mem.shape[1]
    per_layer = dict(norm_mix=norm_mix, w_in=w_in, q_norm=q_norm, k_norm=k_norm, attn_sinks=attn_sinks,
                     ssm_lambda_re=ssm_lambda_re, ssm_lambda_im=ssm_lambda_im, ssm_log_dt=ssm_log_dt,
                     ssm_b_re=ssm_b_re, ssm_b_im=ssm_b_im, ssm_c_re=ssm_c_re, ssm_c_im=ssm_c_im,
                     ssm_d=ssm_d, ssm_w_glu=ssm_w_glu, mem_norm=mem_norm, w_mem_kv=w_mem_kv,
                     xq_norm=xq_norm, xk_norm=xk_norm, out_norm_attn=out_norm_attn,
                     out_norm_ssm=out_norm_ssm, out_norm_xattn=out_norm_xattn, w_o=w_o, norm_ffn=norm_ffn,
                     w_router_group=w_router_group, w_router_expert=w_router_expert,
                     w_gate=w_gate, w_up=w_up, w_down=w_down)
    x2 = x.astype(F32).reshape(batch * seq, d)
    mem2 = mem.astype(F32).reshape(batch * mem_len, d)
    for l in range(norm_mix.shape[0]):
        p = {k: v[l] for k, v in per_layer.items()}
        p["rel_bias"] = rel_bias
        x2 = _layer(x2, mem2, batch, seq, mem_len, p)
    return x2.reshape(batch, seq, d).astype(x.dtype)
```

```python
import functools
import math

import numpy as np
import jax
import jax.numpy as jnp
from jax import lax
from jax.experimental import pallas as pl
from jax.experimental.pallas import tpu as pltpu

F32 = jnp.float32
BF16 = jnp.bfloat16
EPS = 1e-6

ATTN_HEADS = 16
ATTN_KV_HEADS = 2
ATTN_HEAD_DIM = 64
ATTN_WIDTH = ATTN_HEADS * ATTN_HEAD_DIM
WINDOW = 128
BLOCK = 128
REL_BUCKETS = 32
REL_MAX_DIST = 128
SSM_GROUP_CH = 16
SSM_GROUPS = 32
SSM_STATE = 64
SSM_WIDTH = SSM_GROUPS * SSM_GROUP_CH
XATTN_HEADS = 4
XATTN_HEAD_DIM = 128
XATTN_WIDTH = XATTN_HEADS * XATTN_HEAD_DIM
N_EXPERT_GROUPS = 8
EXPERTS_PER_GROUP = 8
N_EXPERTS = N_EXPERT_GROUPS * EXPERTS_PER_GROUP
D_EXPERT = 512
MOE_BLOCK = 128

LANES = 128
SUBLANES = 8
SSM_GROUPS_PER_TILE = LANES // SSM_GROUP_CH
SSM_TILES = SSM_WIDTH // LANES
SSM_TILE_STATE = SSM_GROUPS_PER_TILE * SSM_STATE
SSM_STATES = SSM_GROUPS * SSM_STATE
SSM_PIECES = 4
CHUNK_BLOCKS = 4
META_ROWS = 8
M_OWNER, M_FIRST, M_SIZE, M_NCHUNK, M_LAST, M_NBLK, M_NUSED, M_OWNER_CHUNKS = range(8)
VMEM_LIMIT = 56 * 1024 * 1024

_NT = (((1,), (1,)), ((), ()))


def _cparams(*sem):
    return pltpu.CompilerParams(dimension_semantics=sem, vmem_limit_bytes=VMEM_LIMIT)


def _rms(x, gain):
    ms = jnp.mean(x * x, axis=-1, keepdims=True)
    return x * lax.rsqrt(ms + EPS) * gain


def _t5_bucket_table():
    qi = np.arange(BLOCK, dtype=np.int32)[:, None]
    ki = np.arange(2 * BLOCK, dtype=np.int32)[None, :]
    delta = BLOCK + qi - ki
    n = np.maximum(delta, 0)
    max_exact = REL_BUCKETS // 2
    nf = np.maximum(n, 1).astype(np.float32)
    large = max_exact + (np.log(nf / np.float32(max_exact)) / np.float32(math.log(REL_MAX_DIST / max_exact))
                         * np.float32(REL_BUCKETS - max_exact)).astype(np.int32)
    large = np.minimum(large, REL_BUCKETS - 1)
    return np.where(n < max_exact, n, large).astype(np.int32)


def _upper(cols):
    k = lax.broadcasted_iota(jnp.int32, (BLOCK, cols), 0)
    q = lax.broadcasted_iota(jnp.int32, (BLOCK, cols), 1) % BLOCK
    return k > q


def _bias_kernel(rb_ref, bucket_ref, out_ref):
    pair = pl.program_id(0)
    bucket = bucket_ref[...]
    upper = _upper(BLOCK)
    for half in range(2):
        acc = jnp.zeros(bucket.shape, F32)
        for b in range(REL_BUCKETS):
            acc = jnp.where(bucket == b, rb_ref[b, 2 * pair + half], acc)
        cols = slice(half * BLOCK, (half + 1) * BLOCK)
        out_ref[0, 0, :, cols] = jnp.where(upper, acc[:BLOCK], acc[BLOCK:])
        out_ref[1, 0, :, cols] = jnp.where(upper, jnp.float32(-1e30), acc[BLOCK:])


def _bias_table(rel_bias):
    bucket = jnp.asarray(_t5_bucket_table().T)
    pairs = ATTN_HEADS // 2
    return pl.pallas_call(
        _bias_kernel,
        grid=(pairs,),
        in_specs=[pl.BlockSpec(memory_space=pltpu.SMEM),
                  pl.BlockSpec((2 * BLOCK, BLOCK), lambda h: (0, 0))],
        out_specs=pl.BlockSpec((2, 1, BLOCK, 2 * BLOCK), lambda h: (0, h, 0, 0)),
        out_shape=jax.ShapeDtypeStruct((2, pairs, BLOCK, 2 * BLOCK), F32),
        compiler_params=_cparams("arbitrary"),
        name="t5_bias_table",
    )(rel_bias.astype(F32), bucket)


def _ssm_prep_kernel(lr_ref, li_ref, ldt_ref, br_ref, bi_ref, are_ref, aim_ref, bbr_ref, bbi_ref):
    lr = lr_ref[...]
    li = li_ref[...]
    dt = jnp.exp(ldt_ref[...])
    mag = jnp.exp(lr * dt)
    a_re = mag * jnp.cos(li * dt)
    a_im = mag * jnp.sin(li * dt)
    den = lr * lr + li * li
    nr = a_re - 1.0
    ni = a_im
    coef_re = (nr * lr + ni * li) / den
    coef_im = (ni * lr - nr * li) / den
    are_ref[...] = a_re
    aim_ref[...] = a_im
    br = br_ref[...]
    bi = bi_ref[...]
    bbr_ref[...] = coef_re * br - coef_im * bi
    bbi_ref[...] = coef_re * bi + coef_im * br


def _ssm_prep(lam_re, lam_im, log_dt, b_re, b_im):
    g, n, c = b_re.shape
    vec = jax.ShapeDtypeStruct((g, 1, n), F32)
    mat = jax.ShapeDtypeStruct((g, c, n), F32)
    return pl.pallas_call(
        _ssm_prep_kernel,
        out_shape=(vec, vec, mat, mat),
        name="ssm_discretise",
    )(lam_re.astype(F32).reshape(g, 1, n), lam_im.astype(F32).reshape(g, 1, n),
      log_dt.astype(F32).reshape(g, 1, 1),
      jnp.transpose(b_re.astype(F32), (0, 2, 1)), jnp.transpose(b_im.astype(F32), (0, 2, 1)))


def _block_diag_tiles(m):
    g, r, c = m.shape
    t = g // SSM_GROUPS_PER_TILE
    eye = jnp.eye(SSM_GROUPS_PER_TILE, dtype=m.dtype)
    m4 = m.reshape(t, SSM_GROUPS_PER_TILE, r, c)
    out = m4[:, :, :, None, :] * eye[None, :, None, :, None]
    return out.reshape(t, SSM_GROUPS_PER_TILE * r, SSM_GROUPS_PER_TILE * c)


def _inproj_kernel(x_ref, g_ref, w_ref, q_ref, k_ref, v_ref, u_ref, qx_ref):
    h = _rms(x_ref[...], g_ref[...]).astype(BF16)
    c0 = 0
    for ref in (q_ref, k_ref, v_ref, u_ref, qx_ref):
        transposed = ref is q_ref or ref is v_ref
        c1 = c0 + (ref.shape[0] if transposed else ref.shape[1])
        p = jnp.dot(h, w_ref[:, c0:c1], preferred_element_type=F32)
        if transposed:
            p = p.T
        ref[...] = p.astype(ref.dtype)
        c0 = c1


def _inproj(x2, gain, w_cols, batch, seq, tm):
    t, d = x2.shape
    nsb = seq // tm
    kvw = ATTN_KV_HEADS * ATTN_HEAD_DIM
    ncol = w_cols.shape[1]
    row = lambda w: pl.BlockSpec((tm, w), lambda i: (i, 0))
    return pl.pallas_call(
        _inproj_kernel,
        grid=(t // tm,),
        in_specs=[row(d),
                  pl.BlockSpec((1, d), lambda i: (0, 0)),
                  pl.BlockSpec((d, ncol), lambda i: (0, 0), pipeline_mode=pl.Buffered(1))],
        out_specs=[pl.BlockSpec((ATTN_WIDTH, tm), lambda i: (0, i)),
                   row(kvw), pl.BlockSpec((kvw, tm), lambda i: (0, i)),
                   pl.BlockSpec((tm, SSM_WIDTH), lambda i: (i % nsb, i // nsb)),
                   row(XATTN_WIDTH)],
        out_shape=[jax.ShapeDtypeStruct((ATTN_WIDTH, t), BF16),
                   jax.ShapeDtypeStruct((t, kvw), BF16),
                   jax.ShapeDtypeStruct((kvw, t), BF16),
                   jax.ShapeDtypeStruct((seq, batch * SSM_WIDTH), BF16),
                   jax.ShapeDtypeStruct((t, XATTN_WIDTH), BF16)],
        compiler_params=_cparams("arbitrary"),
        name="in_proj",
    )(x2, gain, w_cols)


def _swa_kernel(sinks_ref, q_ref, kp_ref, kc_ref, vp_ref, vc_ref, bias_ref, qg_ref, kg_ref, og_ref, out_ref):
    dh = ATTN_HEAD_DIM
    kcat = jnp.concatenate([kp_ref[...], kc_ref[...]], axis=0).astype(F32)
    lo = lax.broadcasted_iota(jnp.int32, (1, ATTN_KV_HEADS * dh), 1) < dh
    sq = kcat * kcat
    s_lo = jnp.sum(jnp.where(lo, sq, 0.0), axis=-1, keepdims=True)
    s_hi = jnp.sum(sq, axis=-1, keepdims=True) - s_lo
    kinv = jnp.where(lo, lax.rsqrt(s_lo / dh + EPS), lax.rsqrt(s_hi / dh + EPS))
    kn = (kcat * kinv * kg_ref[...]).astype(BF16)
    v_t = jnp.concatenate([vp_ref[...], vc_ref[...]], axis=1)
    upper = _upper(2 * BLOCK)
    first_head = lax.broadcasted_iota(jnp.int32, (1, 2 * BLOCK), 1) < BLOCK
    zeros = jnp.zeros((dh, BLOCK), F32)
    heads_per_kv = ATTN_HEADS // ATTN_KV_HEADS
    pairs = range(ATTN_HEADS // 2)
    kv_of = lambda pair: (2 * pair) // heads_per_kv

    logits = []
    for pair in pairs:
        cols = []
        for half in range(2):
            hd = 2 * pair + half
            qh = q_ref[hd * dh:(hd + 1) * dh, :].astype(F32)
            ms = jnp.mean(qh * qh, axis=0, keepdims=True)
            qn = qh * lax.rsqrt(ms + EPS) * qg_ref[...]
            cols.append(jnp.concatenate([qn, zeros] if kv_of(pair) == 0 else [zeros, qn], axis=0))
        rhs = jnp.concatenate(cols, axis=1).astype(BF16)
        logits.append(jnp.dot(kn, rhs, preferred_element_type=F32))

    probs, dens = [], []
    for pair in pairs:
        both = logits[pair]
        l = jnp.where(upper, both[:BLOCK], both[BLOCK:]) + bias_ref[0, pair]
        sink = jnp.where(first_head, sinks_ref[2 * pair], sinks_ref[2 * pair + 1])
        m = jnp.maximum(jnp.max(l, axis=0, keepdims=True), sink)
        p = jnp.exp(l - m)
        dens.append(jnp.sum(p, axis=0, keepdims=True) + jnp.exp(sink - m))
        probs.append(jnp.concatenate([jnp.where(upper, p, 0.0), jnp.where(upper, 0.0, p)], axis=0).astype(BF16))

    outs = []
    for pair in pairs:
        g = kv_of(pair)
        o = jnp.dot(v_t[g * dh:(g + 1) * dh, :], probs[pair], preferred_element_type=F32) / dens[pair]
        outs += [o[:, :BLOCK], o[:, BLOCK:]]
    y_t = jnp.concatenate(outs, axis=0)
    ms = jnp.mean(y_t * y_t, axis=0, keepdims=True)
    out_ref[...] = (y_t * lax.rsqrt(ms + EPS) * og_ref[...]).T.astype(out_ref.dtype)


def _swa(q_t, k, v, bias, sinks, qgain, kgain, ogain, batch, seq):
    nb = seq // BLOCK
    kvw = ATTN_KV_HEADS * ATTN_HEAD_DIM
    cur = lambda b, n: (b * nb + n, 0)
    prev = lambda b, n: (b * nb + jnp.maximum(n - 1, 0), 0)
    const2 = lambda b, n: (0, 0)
    return pl.pallas_call(
        _swa_kernel,
        grid=(batch, nb),
        in_specs=[pl.BlockSpec(memory_space=pltpu.SMEM),
                  pl.BlockSpec((ATTN_WIDTH, BLOCK), lambda b, n: (0, b * nb + n)),
                  pl.BlockSpec((BLOCK, kvw), prev), pl.BlockSpec((BLOCK, kvw), cur),
                  pl.BlockSpec((kvw, BLOCK), lambda b, n: (0, b * nb + jnp.maximum(n - 1, 0))),
                  pl.BlockSpec((kvw, BLOCK), lambda b, n: (0, b * nb + n)),
                  pl.BlockSpec((1, ATTN_HEADS // 2, BLOCK, 2 * BLOCK),
                               lambda b, n: (jnp.where(n == 0, 1, 0), 0, 0, 0)),
                  pl.BlockSpec((ATTN_HEAD_DIM, BLOCK), const2), pl.BlockSpec((1, kvw), const2),
                  pl.BlockSpec((ATTN_WIDTH, BLOCK), const2)],
        out_specs=pl.BlockSpec((BLOCK, ATTN_WIDTH), cur),
        out_shape=jax.ShapeDtypeStruct((batch * seq, ATTN_WIDTH), BF16),
        compiler_params=_cparams("arbitrary", "arbitrary"),
        name="swa_attention",
    )(sinks, q_t, k, k, v, v, bias, qgain, kgain, ogain)


def _memkv_kernel(m_ref, g_ref, w_ref, kg_ref, k_ref, v_ref):
    h = _rms(m_ref[...], g_ref[...]).astype(BF16)
    km = jnp.dot(h, w_ref[:, :XATTN_WIDTH], preferred_element_type=F32)
    for hd in range(XATTN_HEADS):
        sl = slice(hd * XATTN_HEAD_DIM, (hd + 1) * XATTN_HEAD_DIM)
        k_ref[:, sl] = _rms(km[:, sl], kg_ref[...]).astype(k_ref.dtype)
    v_ref[...] = jnp.dot(h, w_ref[:, XATTN_WIDTH:], preferred_element_type=F32).astype(v_ref.dtype)


def _memkv(mem2, gain, w_kv, kgain, tm):
    r, d = mem2.shape
    row = lambda w: pl.BlockSpec((tm, w), lambda i: (i, 0))
    const = lambda shape: pl.BlockSpec(shape, lambda i: (0, 0))
    return pl.pallas_call(
        _memkv_kernel,
        grid=(r // tm,),
        in_specs=[row(d), const((1, d)), const((d, 2 * XATTN_WIDTH)), const((1, XATTN_HEAD_DIM))],
        out_specs=[row(XATTN_WIDTH), row(XATTN_WIDTH)],
        out_shape=[jax.ShapeDtypeStruct((r, XATTN_WIDTH), BF16)] * 2,
        compiler_params=_cparams("arbitrary"),
        name="mem_kv_proj",
    )(mem2, gain, w_kv, kgain)


def _xattn_kernel(q_ref, k_ref, v_ref, qg_ref, og_ref, out_ref):
    outs = []
    for hd in range(XATTN_HEADS):
        sl = slice(hd * XATTN_HEAD_DIM, (hd + 1) * XATTN_HEAD_DIM)
        qn = _rms(q_ref[:, sl].astype(F32), qg_ref[...]).astype(BF16)
        l = lax.dot_general(qn, k_ref[:, sl], _NT, preferred_element_type=F32)
        m = jnp.max(l, axis=-1, keepdims=True)
        p = jnp.exp(l - m)
        den = jnp.sum(p, axis=-1, keepdims=True)
        outs.append(jnp.dot(p.astype(BF16), v_ref[:, sl], preferred_element_type=F32) / den)
    y = jnp.concatenate(outs, axis=-1)
    out_ref[...] = _rms(y, og_ref[...]).astype(out_ref.dtype)


def _xattn(qx, km, vm, qgain, ogain, batch, seq, mem_len, tq):
    nq = seq // tq
    const2 = lambda b, i: (0, 0)
    return pl.pallas_call(
        _xattn_kernel,
        grid=(batch, nq),
        in_specs=[pl.BlockSpec((tq, XATTN_WIDTH), lambda b, i: (b * nq + i, 0)),
                  pl.BlockSpec((mem_len, XATTN_WIDTH), lambda b, i: (b, 0)),
                  pl.BlockSpec((mem_len, XATTN_WIDTH), lambda b, i: (b, 0)),
                  pl.BlockSpec((1, XATTN_HEAD_DIM), const2),
                  pl.BlockSpec((1, XATTN_WIDTH), const2)],
        out_specs=pl.BlockSpec((tq, XATTN_WIDTH), lambda b, i: (b * nq + i, 0)),
        out_shape=jax.ShapeDtypeStruct((batch * seq, XATTN_WIDTH), BF16),
        compiler_params=_cparams("arbitrary", "arbitrary"),
        name="mem_xattn",
    )(qx, km, vm, qgain, ogain)


def _ssm_kernel(u_ref, bmat_ref, cmat_ref, are_ref, aim_ref, d_ref, wglu_ref, og_ref, out_ref,
                tb_ref, xr_ref, xi_ref, sr_ref, si_ref, *, batch, steps):
    @pl.when(pl.program_id(0) == 0)
    def _():
        sr_ref[...] = jnp.zeros_like(sr_ref)
        si_ref[...] = jnp.zeros_like(si_ref)

    for b in range(batch):
        for j in range(SSM_TILES):
            c0 = b * SSM_WIDTH + j * LANES
            tb_ref[j, pl.ds(b, steps, stride=batch), :] = u_ref[:, c0:c0 + LANES].astype(F32)
    uf = jnp.concatenate([tb_ref[j] for j in range(SSM_TILES)], axis=-1)
    u = uf.astype(BF16)
    total = steps * batch
    piece = total // SSM_PIECES
    tiles_per_half = SSM_TILES // 2
    half_states = SSM_STATES // 2

    def bu_piece(j, k):
        rows = slice(k * piece, (k + 1) * piece)
        bu = jnp.dot(u[rows, j * LANES:(j + 1) * LANES], bmat_ref[j], preferred_element_type=F32)
        xr_ref[rows, j * SSM_TILE_STATE:(j + 1) * SSM_TILE_STATE] = bu[:, :SSM_TILE_STATE]
        xi_ref[rows, j * SSM_TILE_STATE:(j + 1) * SSM_TILE_STATE] = bu[:, SSM_TILE_STATE:]

    def c_piece(j, k):
        rows = slice(k * piece, (k + 1) * piece)
        sl = slice(j * SSM_TILE_STATE, (j + 1) * SSM_TILE_STATE)
        xcat = jnp.concatenate([xr_ref[rows, sl], xi_ref[rows, sl]], axis=-1).astype(BF16)
        return jnp.dot(xcat, cmat_ref[j], preferred_element_type=F32)

    def scan_half(hf, between):
        cs = slice(hf * half_states, (hf + 1) * half_states)
        ar = jnp.broadcast_to(are_ref[:, cs], (batch, half_states))
        ai = jnp.broadcast_to(aim_ref[:, cs], (batch, half_states))
        s_r, s_i = sr_ref[:, cs], si_ref[:, cs]
        every = steps // len(between)
        for t in range(steps):
            rows = slice(t * batch, (t + 1) * batch)
            s_r, s_i = (ar * s_r - ai * s_i + xr_ref[rows, cs], ar * s_i + ai * s_r + xi_ref[rows, cs])
            xr_ref[rows, cs] = s_r
            xi_ref[rows, cs] = s_i
            if t % every == every - 1:
                between[t // every]()
        sr_ref[:, cs] = s_r
        si_ref[:, cs] = s_i

    first = [(j, k) for j in range(tiles_per_half) for k in range(SSM_PIECES)]
    second = [(j, k) for j in range(tiles_per_half, SSM_TILES) for k in range(SSM_PIECES)]
    for j, k in first:
        bu_piece(j, k)
    scan_half(0, [functools.partial(bu_piece, j, k) for j, k in second])
    y_pieces = {}
    scan_half(1, [functools.partial(lambda j, k: y_pieces.__setitem__((j, k), c_piece(j, k)), j, k)
                  for j, k in first])
    for j, k in second:
        y_pieces[(j, k)] = c_piece(j, k)
    ys = [jnp.concatenate([y_pieces[(j, k)] for k in range(SSM_PIECES)], axis=0) for j in range(SSM_TILES)]
    y = jnp.concatenate(ys, axis=-1) + d_ref[...] * uf
    y = jax.nn.gelu(y)
    y = y * jax.nn.sigmoid(jnp.dot(y.astype(BF16), wglu_ref[...], preferred_element_type=F32))
    y = _rms(y, og_ref[...])
    for j in range(SSM_TILES):
        tb_ref[j] = y[:, j * LANES:(j + 1) * LANES]
    for b in range(batch):
        for j in range(SSM_TILES):
            c0 = b * SSM_WIDTH + j * LANES
            out_ref[:, c0:c0 + LANES] = tb_ref[j, pl.ds(b, steps, stride=batch), :].astype(out_ref.dtype)


def _ssm(u_sb, bmat, cmat, a_re, a_im, d_skip, w_glu, ogain, batch, seq, steps):
    rows = steps * batch
    const2 = lambda c: (0, 0)
    const3 = lambda c: (0, 0, 0)
    return pl.pallas_call(
        functools.partial(_ssm_kernel, batch=batch, steps=steps),
        grid=(seq // steps,),
        in_specs=[pl.BlockSpec((steps, batch * SSM_WIDTH), lambda c: (c, 0)),
                  pl.BlockSpec(bmat.shape, const3), pl.BlockSpec(cmat.shape, const3),
                  pl.BlockSpec((1, SSM_STATES), const2), pl.BlockSpec((1, SSM_STATES), const2),
                  pl.BlockSpec((1, SSM_WIDTH), const2),
                  pl.BlockSpec((SSM_WIDTH, SSM_WIDTH), const2),
                  pl.BlockSpec((1, SSM_WIDTH), const2)],
        out_specs=pl.BlockSpec((steps, batch * SSM_WIDTH), lambda c: (c, 0)),
        out_shape=jax.ShapeDtypeStruct((seq, batch * SSM_WIDTH), BF16),
        scratch_shapes=[pltpu.VMEM((SSM_TILES, rows, LANES), F32),
                        pltpu.VMEM((rows, SSM_STATES), F32), pltpu.VMEM((rows, SSM_STATES), F32),
                        pltpu.VMEM((batch, SSM_STATES), F32), pltpu.VMEM((batch, SSM_STATES), F32)],
        compiler_params=_cparams("arbitrary"),
        name="s5_layer",
    )(u_sb, bmat, cmat, a_re, a_im, d_skip, w_glu, ogain)


def _outproj_kernel(ya_ref, ys_ref, yx_ref, x_ref, wo_ref, g_ref, wr_ref, x1_ref, lt_ref, hp_ref, *, sub):
    for r0 in range(0, x_ref.shape[0], sub):
        rows = slice(r0, r0 + sub)
        mix = jnp.concatenate([ya_ref[rows, :], ys_ref[rows, :], yx_ref[rows, :]], axis=-1)
        x1 = x_ref[rows, :] + jnp.dot(mix, wo_ref[...], preferred_element_type=F32)
        x1_ref[rows, :] = x1
        h2 = _rms(x1, g_ref[...])
        hi = h2.astype(BF16)
        lt_ref[rows, :] = jnp.dot(hi, wr_ref[...], preferred_element_type=F32)
        hp_ref[rows, :] = _pack_rows(hi.astype(F32))


def _outproj(ya, ys_sb, yx, x2, w_o, gain, wr, batch, seq, tm):
    t, d = x2.shape
    nsb = seq // tm
    row = lambda w: pl.BlockSpec((tm, w), lambda i: (i, 0))
    const = lambda shape: pl.BlockSpec(shape, lambda i: (0, 0), pipeline_mode=pl.Buffered(1))
    return pl.pallas_call(
        functools.partial(_outproj_kernel, sub=min(256, tm)),
        grid=(t // tm,),
        in_specs=[row(ATTN_WIDTH),
                  pl.BlockSpec((tm, SSM_WIDTH), lambda i: (i % nsb, i // nsb)),
                  row(XATTN_WIDTH), row(d),
                  const(w_o.shape), const((1, d)), const(wr.shape)],
        out_specs=[row(d), row(LANES), row(d // 2)],
        out_shape=[jax.ShapeDtypeStruct((t, d), F32), jax.ShapeDtypeStruct((t, LANES), F32),
                   jax.ShapeDtypeStruct((t, d // 2), jnp.uint32)],
        compiler_params=_cparams("arbitrary"),
        name="out_proj_router",
    )(ya, ys_sb, yx, x2, w_o, gain, wr)


def _route_kernel(lt_ref, tri_ref, dest_ref, w_ref, meta_ref, cnt_ref, carry_ref, pstart_ref):
    phase = pl.program_id(0)
    c = pl.program_id(1)
    logits = lt_ref[...].T
    tc = logits.shape[1]
    ng, epg = N_EXPERT_GROUPS, EXPERTS_PER_GROUP
    row8 = lax.broadcasted_iota(jnp.int32, (ng, tc), 0)

    gl = logits[0:ng]
    gmax = jnp.max(gl, axis=0, keepdims=True)
    gidx = jnp.min(jnp.where(gl == gmax, row8, ng), axis=0, keepdims=True)
    gate = 1.0 / jnp.sum(jnp.exp(gl - gmax), axis=0, keepdims=True)
    sel = jnp.zeros((epg, tc), F32)
    for g in range(ng):
        sel = jnp.where(gidx == g, logits[ng + g * epg:ng + (g + 1) * epg], sel)
    v1 = jnp.max(sel, axis=0, keepdims=True)
    i1 = jnp.min(jnp.where(sel == v1, row8, epg), axis=0, keepdims=True)
    sel2 = jnp.where(row8 == i1, -jnp.inf, sel)
    v2 = jnp.max(sel2, axis=0, keepdims=True)
    i2 = jnp.min(jnp.where(sel2 == v2, row8, epg), axis=0, keepdims=True)
    e = jnp.exp(v2 - v1)
    w1 = gate * (1.0 / (1.0 + e))
    w2 = gate * (e / (1.0 + e))
    e1 = gidx * epg + i1
    e2 = gidx * epg + i2
    rowe = lax.broadcasted_iota(jnp.int32, (N_EXPERTS, tc), 0)
    oh1 = rowe == e1
    oh2 = rowe == e2
    member = jnp.where(jnp.logical_or(oh1, oh2), 1.0, 0.0)
    chunk_cnt = jnp.sum(member, axis=1, keepdims=True)

    @pl.when(phase == 0)
    def _():
        @pl.when(c == 0)
        def _():
            cnt_ref[...] = jnp.zeros_like(cnt_ref)
        cnt_ref[...] += chunk_cnt

    @pl.when(phase == 1)
    def _():
        @pl.when(c == 0)
        def _():
            cnt = cnt_ref[...]
            nblk = jnp.floor((cnt + (MOE_BLOCK - 1)) * (1.0 / MOE_BLOCK))
            nchunk = jnp.floor((nblk + (CHUNK_BLOCKS - 1)) * (1.0 / CHUNK_BLOCKS))
            r = lax.broadcasted_iota(jnp.int32, (N_EXPERTS, LANES), 0)
            cidx = lax.broadcasted_iota(jnp.int32, (N_EXPERTS, LANES), 1)
            to_row = lambda col: jnp.sum(jnp.where(r == cidx, col, 0.0), axis=0, keepdims=True)
            cumsum_col = lambda col: jnp.sum(jnp.where(cidx <= r, to_row(col), 0.0), axis=1, keepdims=True)
            cumsum_row = lambda col: jnp.sum(jnp.where(r <= cidx, col, 0.0), axis=0, keepdims=True)
            bend = cumsum_col(nblk)
            bstart = bend - nblk
            cend = cumsum_col(nchunk)
            cstart = cend - nchunk
            pstart_ref[...] = bstart * MOE_BLOCK
            carry_ref[...] = jnp.zeros_like(carry_ref)
            lanef = lax.broadcasted_iota(jnp.int32, (1, LANES), 1).astype(F32)
            owner = jnp.minimum(jnp.sum(jnp.where(cend <= lanef, 1.0, 0.0), axis=0, keepdims=True),
                                N_EXPERTS - 1.0)
            own = r.astype(F32) == owner
            pick = lambda col: jnp.sum(jnp.where(own, col, 0.0), axis=0, keepdims=True)
            idx = lanef - pick(cstart)
            first = pick(bstart) + CHUNK_BLOCKS * idx
            size = jnp.clip(pick(nblk) - CHUNK_BLOCKS * idx, 0.0, float(CHUNK_BLOCKS))
            zero = jnp.zeros((1, LANES), F32)
            rows = [owner, first, size,
                    zero + jnp.sum(nchunk, axis=0, keepdims=True),
                    cumsum_row(nblk) - 1.0,
                    to_row(nblk),
                    zero + jnp.sum(nblk, axis=0, keepdims=True),
                    pick(nchunk)]
            for k, v in enumerate(rows):
                meta_ref[k:k + 1, :] = v.astype(jnp.int32)

        before = carry_ref[...] + jnp.dot(member.astype(BF16), tri_ref[...], preferred_element_type=F32)
        pos = before + pstart_ref[...]
        dest_ref[0:1, :] = jnp.sum(jnp.where(oh1, pos, 0.0), axis=0, keepdims=True).astype(jnp.int32)
        dest_ref[1:2, :] = jnp.sum(jnp.where(oh2, pos, 0.0), axis=0, keepdims=True).astype(jnp.int32)
        w_ref[0:1, :] = w1
        w_ref[1:2, :] = w2
        carry_ref[...] += chunk_cnt


def _route(logits_t, tc):
    t = logits_t.shape[0]
    nc = t // tc
    tri = jnp.asarray(np.triu(np.ones((tc, tc), np.float32), k=1), dtype=BF16)
    return pl.pallas_call(
        _route_kernel,
        grid=(2, nc),
        in_specs=[pl.BlockSpec((tc, LANES), lambda p, c: (c, 0)),
                  pl.BlockSpec((tc, tc), lambda p, c: (0, 0))],
        out_specs=[pl.BlockSpec((2, tc), lambda p, c: (0, c * p)),
                   pl.BlockSpec((2, tc), lambda p, c: (0, c * p)),
                   pl.BlockSpec((META_ROWS, LANES), lambda p, c: (0, 0))],
        out_shape=[jax.ShapeDtypeStruct((2, t), jnp.int32), jax.ShapeDtypeStruct((2, t), F32),
                   jax.ShapeDtypeStruct((META_ROWS, LANES), jnp.int32)],
        scratch_shapes=[pltpu.VMEM((N_EXPERTS, 1), F32)] * 3,
        compiler_params=_cparams("arbitrary", "arbitrary"),
        name="moe_route",
    )(logits_t, tri)


def _meta(meta_ref, row, lane=0):
    return meta_ref[row * LANES + lane]


def _fill_blocks(meta_ref, zbuf, dst_ref, sem, n_blocks, *, expert_tails):
    zbuf[...] = jnp.zeros_like(zbuf)
    n_used = _meta(meta_ref, M_NUSED)
    block = lambda b: pltpu.make_async_copy(zbuf, dst_ref.at[pl.ds(b * MOE_BLOCK, MOE_BLOCK), :], sem)

    def tails(fn):
        def body(e, carry):
            @pl.when(_meta(meta_ref, M_NBLK, e) > 0)
            def _():
                fn(block(_meta(meta_ref, M_LAST, e)))
            return carry
        lax.fori_loop(0, N_EXPERTS, body, 0)

    def unused(fn):
        def body(b, carry):
            fn(block(b))
            return carry
        lax.fori_loop(n_used, n_blocks, body, 0)

    for phase in (lambda cp: cp.start(), lambda cp: cp.wait()):
        if expert_tails:
            tails(phase)
        unused(phase)


def _dispatch_kernel(dest_ref, meta_ref, h_ref, xs_ref, zbuf, sem_z, sem, *, tokens, n_blocks):
    tm = h_ref.shape[0]

    @pl.when(pl.program_id(0) == 0)
    def _():
        _fill_blocks(meta_ref, zbuf, xs_ref, sem_z, n_blocks, expert_tails=True)

    base = pl.program_id(0) * tm

    def issue(g, carry):
        r8 = pl.multiple_of(g * SUBLANES, SUBLANES)
        for s in range(SUBLANES):
            for k in range(2):
                d = dest_ref[k * tokens + base + r8 + s]
                pltpu.make_async_copy(h_ref.at[pl.ds(r8 + s, 1), :], xs_ref.at[pl.ds(d, 1), :], sem).start()
        return carry
    lax.fori_loop(0, tm // SUBLANES, issue, 0)
    for k in range(2):
        pltpu.make_async_copy(h_ref, xs_ref.at[pl.ds(0, tm), :], sem).wait()


def _dispatch(dest_flat, meta_flat, h2p, n_blocks, tm):
    t, w = h2p.shape
    grid_spec = pltpu.PrefetchScalarGridSpec(
        num_scalar_prefetch=2,
        grid=(t // tm,),
        in_specs=[pl.BlockSpec((tm, w), lambda i, d, m: (i, 0))],
        out_specs=pl.BlockSpec(memory_space=pl.ANY),
        scratch_shapes=[pltpu.VMEM((MOE_BLOCK, w), h2p.dtype),
                        pltpu.SemaphoreType.DMA(()), pltpu.SemaphoreType.DMA(())],
    )
    return pl.pallas_call(
        functools.partial(_dispatch_kernel, tokens=t, n_blocks=n_blocks),
        grid_spec=grid_spec,
        out_shape=jax.ShapeDtypeStruct((n_blocks * MOE_BLOCK, w), h2p.dtype),
        compiler_params=_cparams("arbitrary"),
        name="moe_dispatch",
    )(dest_flat, meta_flat, h2p)


def _pack_rows(x):
    bits = lax.bitcast_convert_type(x, jnp.uint32)
    half = x.shape[1] // 2
    return (bits[:, half:] & jnp.uint32(0xFFFF0000)) | (bits[:, :half] >> 16)


def _unpack_halves(words):
    lo = lax.bitcast_convert_type(words << 16, F32)
    hi = lax.bitcast_convert_type(words & jnp.uint32(0xFFFF0000), F32)
    return lo, hi


def _unpack_rows(words):
    return jnp.concatenate(_unpack_halves(words), axis=-1).astype(BF16)


def _expert_kernel(meta_ref, xs_ref, wg_ref, wu_ref, wd_ref, yb_ref,
                   xbuf, ybuf, zbuf, wg_f32, wu_f32, wd_f32, wslot_ref,
                   sem_in, sem_out, sem_z, sem_w, *, n_blocks):
    c = pl.program_id(0)
    n_chunks = _meta(meta_ref, M_NCHUNK)
    slot = c % 2

    def weight_copies(k, s):
        e = _meta(meta_ref, M_OWNER, k)
        return [pltpu.make_async_copy(src.at[e], dst.at[s], sem_w.at[s])
                for src, dst in ((wg_ref, wg_f32), (wu_ref, wu_f32), (wd_ref, wd_f32))]

    def in_copy(k, s, nb):
        rows = nb * MOE_BLOCK
        src = xs_ref.at[pl.ds(_meta(meta_ref, M_FIRST, k) * MOE_BLOCK, rows), :]
        return pltpu.make_async_copy(src, xbuf.at[s, pl.ds(0, rows), :], sem_in.at[s])

    def out_copy(k, s, nb):
        rows = nb * MOE_BLOCK
        dst = yb_ref.at[pl.ds(_meta(meta_ref, M_FIRST, k) * MOE_BLOCK, rows), :]
        return pltpu.make_async_copy(ybuf.at[s, pl.ds(0, rows), :], dst, sem_out.at[s])

    def by_size(k, fn):
        for nb in range(1, CHUNK_BLOCKS + 1):
            pl.when(_meta(meta_ref, M_SIZE, k) == nb)(functools.partial(fn, nb))

    @pl.when(c == 0)
    def _():
        by_size(0, lambda nb: in_copy(0, 0, nb).start())

    @pl.when(c + 1 < n_chunks)
    def _():
        by_size(c + 1, lambda nb: in_copy(c + 1, 1 - slot, nb).start())

    @pl.when(c < n_chunks)
    def _():
        prev = jnp.maximum(c - 1, 0)
        new_expert = jnp.logical_or(c == 0, _meta(meta_ref, M_OWNER, c) != _meta(meta_ref, M_OWNER, prev))

        @pl.when(c == 0)
        def _():
            wslot_ref[0] = 0
            for cp in weight_copies(0, 0):
                cp.start()

        @pl.when(new_expert)
        def _():
            ws = jnp.where(c == 0, 0, 1 - wslot_ref[0])
            wslot_ref[0] = ws
            nxt = c + _meta(meta_ref, M_OWNER_CHUNKS, c)

            @pl.when(nxt < n_chunks)
            def _():
                for cp in weight_copies(nxt, 1 - ws):
                    cp.start()
            for cp in weight_copies(c, ws):
                cp.wait()

        @pl.when(c >= 2)
        def _():
            by_size(c - 2, lambda nb: out_copy(c - 2, slot, nb).wait())

        def compute(nb):
            rows = nb * MOE_BLOCK
            ws = wslot_ref[0]
            in_copy(c, slot, nb).wait()
            h = _unpack_rows(xbuf[slot, 0:rows, :])
            gate = jnp.dot(h, wg_f32[ws].astype(BF16), preferred_element_type=F32)
            up = jnp.dot(h, wu_f32[ws].astype(BF16), preferred_element_type=F32)
            act = (jax.nn.silu(gate) * up).astype(BF16)
            y = jnp.dot(act, wd_f32[ws].astype(BF16), preferred_element_type=F32)
            ybuf[slot, 0:rows, :] = _pack_rows(y.astype(BF16).astype(F32))
            out_copy(c, slot, nb).start()
        by_size(c, compute)

    @pl.when(c == pl.num_programs(0) - 1)
    def _():
        _fill_blocks(meta_ref, zbuf, yb_ref, sem_z, n_blocks, expert_tails=False)
        for back in (2, 1):
            @pl.when(n_chunks >= back)
            def _(back=back):
                k = n_chunks - back
                by_size(k, lambda nb: out_copy(k, k % 2, nb).wait())


def _experts(meta_flat, xs, w_gate, w_up, w_down, n_blocks, n_chunks_max):
    d, de = w_gate.shape[1], w_gate.shape[2]
    rows = CHUNK_BLOCKS * MOE_BLOCK

    hbm = pl.BlockSpec(memory_space=pl.ANY)
    grid_spec = pltpu.PrefetchScalarGridSpec(
        num_scalar_prefetch=1,
        grid=(n_chunks_max,),
        in_specs=[hbm, hbm, hbm, hbm],
        out_specs=hbm,
        scratch_shapes=[pltpu.VMEM((2, rows, xs.shape[1]), xs.dtype),
                        pltpu.VMEM((2, rows, d // 2), jnp.uint32),
                        pltpu.VMEM((MOE_BLOCK, d // 2), jnp.uint32),
                        pltpu.VMEM((2, d, de), F32), pltpu.VMEM((2, d, de), F32), pltpu.VMEM((2, de, d), F32),
                        pltpu.SMEM((1,), jnp.int32),
                        pltpu.SemaphoreType.DMA((2,)), pltpu.SemaphoreType.DMA((2,)),
                        pltpu.SemaphoreType.DMA(()), pltpu.SemaphoreType.DMA((2,))],
    )
    return pl.pallas_call(
        functools.partial(_expert_kernel, n_blocks=n_blocks),
        grid_spec=grid_spec,
        out_shape=jax.ShapeDtypeStruct((n_blocks * MOE_BLOCK, d // 2), jnp.uint32),
        compiler_params=_cparams("arbitrary"),
        name="moe_experts",
    )(meta_flat, xs, w_gate, w_up, w_down)


def _combine_kernel(dest_ref, yb_ref, x1_ref, w_ref, out_ref, gbuf, sem, *, tokens):
    i = pl.program_id(0)
    tm = x1_ref.shape[0]
    slot = i % 2

    def gather(tile, s):
        base = tile * tm

        def issue(g, carry):
            r8 = pl.multiple_of(g * SUBLANES, SUBLANES)
            for sub in range(SUBLANES):
                for k in range(2):
                    d = dest_ref[k * tokens + base + r8 + sub]
                    pltpu.make_async_copy(yb_ref.at[pl.ds(d, 1), :], gbuf.at[s, k, pl.ds(r8 + sub, 1), :],
                                          sem.at[s]).start()
            return carry
        lax.fori_loop(0, tm // SUBLANES, issue, 0)

    @pl.when(i == 0)
    def _():
        gather(0, 0)

    @pl.when(i + 1 < pl.num_programs(0))
    def _():
        gather(i + 1, 1 - slot)

    for k in range(2):
        pltpu.make_async_copy(yb_ref.at[pl.ds(0, tm), :], gbuf.at[slot, k], sem.at[slot]).wait()
    w = w_ref[...]
    half = x1_ref.shape[1] // 2
    lo0, hi0 = _unpack_halves(gbuf[slot, 0])
    lo1, hi1 = _unpack_halves(gbuf[slot, 1])
    out_ref[:, :half] = x1_ref[:, :half] + (lo0 * w[:, 0:1] + lo1 * w[:, 1:2])
    out_ref[:, half:] = x1_ref[:, half:] + (hi0 * w[:, 0:1] + hi1 * w[:, 1:2])


def _combine(dest_flat, yb, x1, w_tok, tm):
    t, d = x1.shape
    grid_spec = pltpu.PrefetchScalarGridSpec(
        num_scalar_prefetch=1,
        grid=(t // tm,),
        in_specs=[pl.BlockSpec(memory_space=pl.ANY),
                  pl.BlockSpec((tm, d), lambda i, dr: (i, 0)),
                  pl.BlockSpec((tm, 2), lambda i, dr: (i, 0))],
        out_specs=pl.BlockSpec((tm, d), lambda i, dr: (i, 0)),
        scratch_shapes=[pltpu.VMEM((2, 2, tm, d // 2), jnp.uint32), pltpu.SemaphoreType.DMA((2,))],
    )
    return pl.pallas_call(
        functools.partial(_combine_kernel, tokens=t),
        grid_spec=grid_spec,
        out_shape=jax.ShapeDtypeStruct((t, d), F32),
        compiler_params=_cparams("arbitrary"),
        name="moe_combine",
    )(dest_flat, yb, x1, w_tok)


def _row(v):
    return v.astype(F32).reshape(1, -1)


def _layer(x2, mem2, batch, seq, mem_len, p):
    t, d = x2.shape

    tm_in = min(512, seq)
    q, kk, vv, u_sb, qx = _inproj(x2, _row(p["norm_mix"]), p["w_in"].astype(BF16), batch, seq, tm_in)

    bias = _bias_table(p["rel_bias"])
    col = lambda v: jnp.broadcast_to(v.astype(F32)[:, None], (v.shape[0], BLOCK))
    qgain = col(p["q_norm"]) * (1.0 / math.sqrt(ATTN_HEAD_DIM))
    kgain = jnp.tile(_row(p["k_norm"]), (1, ATTN_KV_HEADS))
    ya = _swa(q, kk, vv, bias, p["attn_sinks"].astype(F32), qgain, kgain, col(p["out_norm_attn"]), batch, seq)

    km, vm = _memkv(mem2, _row(p["mem_norm"]), p["w_mem_kv"].astype(BF16), _row(p["xk_norm"]),
                    min(256, mem2.shape[0]))
    xq_gain = _row(p["xq_norm"]) * (1.0 / math.sqrt(XATTN_HEAD_DIM))
    yx = _xattn(qx, km, vm, xq_gain, _row(p["out_norm_xattn"]), batch, seq, mem_len, min(512, seq))

    a_re, a_im, bbr, bbi = _ssm_prep(p["ssm_lambda_re"], p["ssm_lambda_im"], p["ssm_log_dt"],
                                     p["ssm_b_re"], p["ssm_b_im"])
    bmat = jnp.concatenate([_block_diag_tiles(bbr), _block_diag_tiles(bbi)], axis=-1).astype(BF16)
    c_re_t = jnp.transpose(p["ssm_c_re"].astype(F32), (0, 2, 1))
    c_im_t = jnp.transpose(p["ssm_c_im"].astype(F32), (0, 2, 1))
    cmat = jnp.concatenate([_block_diag_tiles(c_re_t), _block_diag_tiles(-c_im_t)], axis=1).astype(BF16)
    steps = min(64, seq)
    ys_sb = _ssm(u_sb, bmat, cmat,
                 a_re.reshape(1, SSM_STATES), a_im.reshape(1, SSM_STATES), _row(p["ssm_d"]),
                 p["ssm_w_glu"].astype(BF16), _row(p["out_norm_ssm"]), batch, seq, steps)

    wr = jnp.concatenate([p["w_router_group"], p["w_router_expert"]], axis=1).astype(F32)
    wr = jnp.pad(wr, ((0, 0), (0, LANES - wr.shape[1]))).astype(BF16)
    tm_out = min(512, seq)
    x1, logits_t, h2p = _outproj(ya, ys_sb, yx, x2, p["w_o"].astype(BF16), _row(p["norm_ffn"]), wr,
                                 batch, seq, tm_out)

    dest, w_k, meta = _route(logits_t, min(1024, t))
    n_blocks = (2 * t) // MOE_BLOCK + N_EXPERTS
    n_chunks_max = (n_blocks + (CHUNK_BLOCKS - 1) * N_EXPERTS) // CHUNK_BLOCKS
    dest_flat = dest.reshape(2 * t)
    meta_flat = meta.reshape(META_ROWS * LANES)
    xs = _dispatch(dest_flat, meta_flat, h2p, n_blocks, min(2048, t))
    yb = _experts(meta_flat, xs, p["w_gate"], p["w_up"], p["w_down"], n_blocks, n_chunks_max)
    return _combine(dest_flat, yb, x1, w_k.T, min(256, t))


def kernel(x, mem, norm_mix, w_in, q_norm, k_norm, attn_sinks, rel_bias, ssm_lambda_re, ssm_lambda_im, ssm_log_dt, ssm_b_re, ssm_b_im, ssm_c_re, ssm_c_im, ssm_d, ssm_w_glu, mem_norm, w_mem_kv, xq_norm, xk_norm, out_norm_attn, out_norm_ssm, out_norm_xattn, w_o, norm_ffn, w_router_group, w_router_expert, w_gate, w_up, w_down):
    batch, seq, d = x.shape
    mem_len = mem.shape[1]
    per_layer = dict(norm_mix=norm_mix, w_in=w_in, q_norm=q_norm, k_norm=k_norm, attn_sinks=attn_sinks,
                     ssm_lambda_re=ssm_lambda_re, ssm_lambda_im=ssm_lambda_im, ssm_log_dt=ssm_log_dt,
                     ssm_b_re=ssm_b_re, ssm_b_im=ssm_b_im, ssm_c_re=ssm_c_re, ssm_c_im=ssm_c_im,
                     ssm_d=ssm_d, ssm_w_glu=ssm_w_glu, mem_norm=mem_norm, w_mem_kv=w_mem_kv,
                     xq_norm=xq_norm, xk_norm=xk_norm, out_norm_attn=out_norm_attn,
                     out_norm_ssm=out_norm_ssm, out_norm_xattn=out_norm_xattn, w_o=w_o, norm_ffn=norm_ffn,
                     w_router_group=w_router_group, w_router_expert=w_router_expert,
                     w_gate=w_gate, w_up=w_up, w_down=w_down)
    x2 = x.astype(F32).reshape(batch * seq, d)
    mem2 = mem.astype(F32).reshape(batch * mem_len, d)
    for l in range(norm_mix.shape[0]):
        p = {k: v[l] for k, v in per_layer.items()}
        p["rel_bias"] = rel_bias
        x2 = _layer(x2, mem2, batch, seq, mem_len, p)
    return x2.reshape(batch, seq, d).astype(x.dtype)
```

```python
import functools
import math

import numpy as np
import jax
import jax.numpy as jnp
from jax import lax
from jax.experimental import pallas as pl
from jax.experimental.pallas import tpu as pltpu

F32 = jnp.float32
BF16 = jnp.bfloat16
EPS = 1e-6

ATTN_HEADS = 16
ATTN_KV_HEADS = 2
ATTN_HEAD_DIM = 64
ATTN_WIDTH = ATTN_HEADS * ATTN_HEAD_DIM
WINDOW = 128
BLOCK = 128
REL_BUCKETS = 32
REL_MAX_DIST = 128
SSM_GROUP_CH = 16
SSM_GROUPS = 32
SSM_STATE = 64
SSM_WIDTH = SSM_GROUPS * SSM_GROUP_CH
XATTN_HEADS = 4
XATTN_HEAD_DIM = 128
XATTN_WIDTH = XATTN_HEADS * XATTN_HEAD_DIM
N_EXPERT_GROUPS = 8
EXPERTS_PER_GROUP = 8
N_EXPERTS = N_EXPERT_GROUPS * EXPERTS_PER_GROUP
D_EXPERT = 512
MOE_BLOCK = 128

LANES = 128
SUBLANES = 8
SSM_GROUPS_PER_TILE = LANES // SSM_GROUP_CH
SSM_TILES = SSM_WIDTH // LANES
SSM_TILE_STATE = SSM_GROUPS_PER_TILE * SSM_STATE
SSM_STATES = SSM_GROUPS * SSM_STATE
SSM_PIECES = 4
CHUNK_BLOCKS = 4
META_ROWS = 8
M_OWNER, M_FIRST, M_SIZE, M_NCHUNK, M_LAST, M_NBLK, M_NUSED, M_OWNER_CHUNKS = range(8)
VMEM_LIMIT = 56 * 1024 * 1024

_NT = (((1,), (1,)), ((), ()))


def _cparams(*sem):
    return pltpu.CompilerParams(dimension_semantics=sem, vmem_limit_bytes=VMEM_LIMIT)


def _rms(x, gain):
    ms = jnp.mean(x * x, axis=-1, keepdims=True)
    return x * lax.rsqrt(ms + EPS) * gain


def _t5_bucket_table():
    qi = np.arange(BLOCK, dtype=np.int32)[:, None]
    ki = np.arange(2 * BLOCK, dtype=np.int32)[None, :]
    delta = BLOCK + qi - ki
    n = np.maximum(delta, 0)
    max_exact = REL_BUCKETS // 2
    nf = np.maximum(n, 1).astype(np.float32)
    large = max_exact + (np.log(nf / np.float32(max_exact)) / np.float32(math.log(REL_MAX_DIST / max_exact))
                         * np.float32(REL_BUCKETS - max_exact)).astype(np.int32)
    large = np.minimum(large, REL_BUCKETS - 1)
    return np.where(n < max_exact, n, large).astype(np.int32)


def _upper(cols):
    k = lax.broadcasted_iota(jnp.int32, (BLOCK, cols), 0)
    q = lax.broadcasted_iota(jnp.int32, (BLOCK, cols), 1) % BLOCK
    return k > q


def _bias_kernel(rb_ref, bucket_ref, out_ref):
    pair = pl.program_id(0)
    bucket = bucket_ref[...]
    upper = _upper(BLOCK)
    for half in range(2):
        acc = jnp.zeros(bucket.shape, F32)
        for b in range(REL_BUCKETS):
            acc = jnp.where(bucket == b, rb_ref[b, 2 * pair + half], acc)
        cols = slice(half * BLOCK, (half + 1) * BLOCK)
        out_ref[0, 0, :, cols] = jnp.where(upper, acc[:BLOCK], acc[BLOCK:])
        out_ref[1, 0, :, cols] = jnp.where(upper, jnp.float32(-1e30), acc[BLOCK:])


def _bias_table(rel_bias):
    bucket = jnp.asarray(_t5_bucket_table().T)
    pairs = ATTN_HEADS // 2
    return pl.pallas_call(
        _bias_kernel,
        grid=(pairs,),
        in_specs=[pl.BlockSpec(memory_space=pltpu.SMEM),
                  pl.BlockSpec((2 * BLOCK, BLOCK), lambda h: (0, 0))],
        out_specs=pl.BlockSpec((2, 1, BLOCK, 2 * BLOCK), lambda h: (0, h, 0, 0)),
        out_shape=jax.ShapeDtypeStruct((2, pairs, BLOCK, 2 * BLOCK), F32),
        compiler_params=_cparams("arbitrary"),
        name="t5_bias_table",
    )(rel_bias.astype(F32), bucket)


def _ssm_prep_kernel(lr_ref, li_ref, ldt_ref, br_ref, bi_ref, are_ref, aim_ref, bbr_ref, bbi_ref):
    lr = lr_ref[...]
    li = li_ref[...]
    dt = jnp.exp(ldt_ref[...])
    mag = jnp.exp(lr * dt)
    a_re = mag * jnp.cos(li * dt)
    a_im = mag * jnp.sin(li * dt)
    den = lr * lr + li * li
    nr = a_re - 1.0
    ni = a_im
    coef_re = (nr * lr + ni * li) / den
    coef_im = (ni * lr - nr * li) / den
    are_ref[...] = a_re
    aim_ref[...] = a_im
    br = br_ref[...]
    bi = bi_ref[...]
    bbr_ref[...] = coef_re * br - coef_im * bi
    bbi_ref[...] = coef_re * bi + coef_im * br


def _ssm_prep(lam_re, lam_im, log_dt, b_re, b_im):
    g, n, c = b_re.shape
    vec = jax.ShapeDtypeStruct((g, 1, n), F32)
    mat = jax.ShapeDtypeStruct((g, c, n), F32)
    return pl.pallas_call(
        _ssm_prep_kernel,
        out_shape=(vec, vec, mat, mat),
        name="ssm_discretise",
    )(lam_re.astype(F32).reshape(g, 1, n), lam_im.astype(F32).reshape(g, 1, n),
      log_dt.astype(F32).reshape(g, 1, 1),
      jnp.transpose(b_re.astype(F32), (0, 2, 1)), jnp.transpose(b_im.astype(F32), (0, 2, 1)))


def _block_diag_tiles(m):
    g, r, c = m.shape
    t = g // SSM_GROUPS_PER_TILE
    eye = jnp.eye(SSM_GROUPS_PER_TILE, dtype=m.dtype)
    m4 = m.reshape(t, SSM_GROUPS_PER_TILE, r, c)
    out = m4[:, :, :, None, :] * eye[None, :, None, :, None]
    return out.reshape(t, SSM_GROUPS_PER_TILE * r, SSM_GROUPS_PER_TILE * c)


def _inproj_kernel(x_ref, g_ref, w_ref, q_ref, k_ref, v_ref, u_ref, qx_ref):
    h = _rms(x_ref[...], g_ref[...]).astype(BF16)
    c0 = 0
    for ref in (q_ref, k_ref, v_ref, u_ref, qx_ref):
        transposed = ref is q_ref or ref is v_ref
        c1 = c0 + (ref.shape[0] if transposed else ref.shape[1])
        p = jnp.dot(h, w_ref[:, c0:c1], preferred_element_type=F32)
        if transposed:
            p = p.T
        ref[...] = p.astype(ref.dtype)
        c0 = c1


def _inproj(x2, gain, w_cols, batch, seq, tm):
    t, d = x2.shape
    nsb = seq // tm
    kvw = ATTN_KV_HEADS * ATTN_HEAD_DIM
    ncol = w_cols.shape[1]
    row = lambda w: pl.BlockSpec((tm, w), lambda i: (i, 0))
    return pl.pallas_call(
        _inproj_kernel,
        grid=(t // tm,),
        in_specs=[row(d),
                  pl.BlockSpec((1, d), lambda i: (0, 0)),
                  pl.BlockSpec((d, ncol), lambda i: (0, 0), pipeline_mode=pl.Buffered(1))],
        out_specs=[pl.BlockSpec((ATTN_WIDTH, tm), lambda i: (0, i)),
                   row(kvw), pl.BlockSpec((kvw, tm), lambda i: (0, i)),
                   pl.BlockSpec((tm, SSM_WIDTH), lambda i: (i % nsb, i // nsb)),
                   row(XATTN_WIDTH)],
        out_shape=[jax.ShapeDtypeStruct((ATTN_WIDTH, t), BF16),
                   jax.ShapeDtypeStruct((t, kvw), BF16),
                   jax.ShapeDtypeStruct((kvw, t), BF16),
                   jax.ShapeDtypeStruct((seq, batch * SSM_WIDTH), BF16),
                   jax.ShapeDtypeStruct((t, XATTN_WIDTH), BF16)],
        compiler_params=_cparams("arbitrary"),
        name="in_proj",
    )(x2, gain, w_cols)


def _swa_kernel(sinks_ref, q_ref, kp_ref, kc_ref, vp_ref, vc_ref, bias_ref, qg_ref, kg_ref, og_ref, out_ref):
    dh = ATTN_HEAD_DIM
    kcat = jnp.concatenate([kp_ref[...], kc_ref[...]], axis=0).astype(F32)
    lo = lax.broadcasted_iota(jnp.int32, (1, ATTN_KV_HEADS * dh), 1) < dh
    sq = kcat * kcat
    s_lo = jnp.sum(jnp.where(lo, sq, 0.0), axis=-1, keepdims=True)
    s_hi = jnp.sum(sq, axis=-1, keepdims=True) - s_lo
    kinv = jnp.where(lo, lax.rsqrt(s_lo / dh + EPS), lax.rsqrt(s_hi / dh + EPS))
    kn = (kcat * kinv * kg_ref[...]).astype(BF16)
    v_t = jnp.concatenate([vp_ref[...], vc_ref[...]], axis=1)
    upper = _upper(2 * BLOCK)
    first_head = lax.broadcasted_iota(jnp.int32, (1, 2 * BLOCK), 1) < BLOCK
    zeros = jnp.zeros((dh, BLOCK), F32)
    heads_per_kv = ATTN_HEADS // ATTN_KV_HEADS
    pairs = range(ATTN_HEADS // 2)
    kv_of = lambda pair: (2 * pair) // heads_per_kv

    def qk(pair):
        cols = []
        for half in range(2):
            hd = 2 * pair + half
            qh = q_ref[hd * dh:(hd + 1) * dh, :].astype(F32)
            ms = jnp.mean(qh * qh, axis=0, keepdims=True)
            qn = qh * lax.rsqrt(ms + EPS) * qg_ref[...]
            cols.append(jnp.concatenate([qn, zeros] if kv_of(pair) == 0 else [zeros, qn], axis=0))
        rhs = jnp.concatenate(cols, axis=1).astype(BF16)
        return jnp.dot(kn, rhs, preferred_element_type=F32)

    def softmax(pair, both):
        l = jnp.where(upper, both[:BLOCK], both[BLOCK:]) + bias_ref[0, pair]
        sink = jnp.where(first_head, sinks_ref[2 * pair], sinks_ref[2 * pair + 1])
        m = jnp.maximum(jnp.max(l, axis=0, keepdims=True), sink)
        p = jnp.exp(l - m)
        den = jnp.sum(p, axis=0, keepdims=True) + jnp.exp(sink - m)
        pz = jnp.concatenate([jnp.where(upper, p, 0.0), jnp.where(upper, 0.0, p)], axis=0).astype(BF16)
        return pz, den

    def pv(pair, pz, den):
        g = kv_of(pair)
        o = jnp.dot(v_t[g * dh:(g + 1) * dh, :], pz, preferred_element_type=F32) / den
        return [o[:, :BLOCK], o[:, BLOCK:]]

    logits = [qk(pair) for pair in pairs]
    outs = []
    prev = None
    for pair in pairs:
        cur = softmax(pair, logits[pair])
        if prev is not None:
            outs += pv(pair - 1, *prev)
        prev = cur
    outs += pv(len(pairs) - 1, *prev)
    y_t = jnp.concatenate(outs, axis=0)
    ms = jnp.mean(y_t * y_t, axis=0, keepdims=True)
    out_ref[...] = (y_t * lax.rsqrt(ms + EPS) * og_ref[...]).T.astype(out_ref.dtype)


def _swa(q_t, k, v, bias, sinks, qgain, kgain, ogain, batch, seq):
    nb = seq // BLOCK
    kvw = ATTN_KV_HEADS * ATTN_HEAD_DIM
    cur = lambda b, n: (b * nb + n, 0)
    prev = lambda b, n: (b * nb + jnp.maximum(n - 1, 0), 0)
    const2 = lambda b, n: (0, 0)
    return pl.pallas_call(
        _swa_kernel,
        grid=(batch, nb),
        in_specs=[pl.BlockSpec(memory_space=pltpu.SMEM),
                  pl.BlockSpec((ATTN_WIDTH, BLOCK), lambda b, n: (0, b * nb + n)),
                  pl.BlockSpec((BLOCK, kvw), prev), pl.BlockSpec((BLOCK, kvw), cur),
                  pl.BlockSpec((kvw, BLOCK), lambda b, n: (0, b * nb + jnp.maximum(n - 1, 0))),
                  pl.BlockSpec((kvw, BLOCK), lambda b, n: (0, b * nb + n)),
                  pl.BlockSpec((1, ATTN_HEADS // 2, BLOCK, 2 * BLOCK),
                               lambda b, n: (jnp.where(n == 0, 1, 0), 0, 0, 0)),
                  pl.BlockSpec((ATTN_HEAD_DIM, BLOCK), const2), pl.BlockSpec((1, kvw), const2),
                  pl.BlockSpec((ATTN_WIDTH, BLOCK), const2)],
        out_specs=pl.BlockSpec((BLOCK, ATTN_WIDTH), cur),
        out_shape=jax.ShapeDtypeStruct((batch * seq, ATTN_WIDTH), BF16),
        compiler_params=_cparams("arbitrary", "arbitrary"),
        name="swa_attention",
    )(sinks, q_t, k, k, v, v, bias, qgain, kgain, ogain)


def _memkv_kernel(m_ref, g_ref, w_ref, kg_ref, k_ref, v_ref):
    h = _rms(m_ref[...], g_ref[...]).astype(BF16)
    km = jnp.dot(h, w_ref[:, :XATTN_WIDTH], preferred_element_type=F32)
    for hd in range(XATTN_HEADS):
        sl = slice(hd * XATTN_HEAD_DIM, (hd + 1) * XATTN_HEAD_DIM)
        k_ref[:, sl] = _rms(km[:, sl], kg_ref[...]).astype(k_ref.dtype)
    v_ref[...] = jnp.dot(h, w_ref[:, XATTN_WIDTH:], preferred_element_type=F32).astype(v_ref.dtype)


def _memkv(mem2, gain, w_kv, kgain, tm):
    r, d = mem2.shape
    row = lambda w: pl.BlockSpec((tm, w), lambda i: (i, 0))
    const = lambda shape: pl.BlockSpec(shape, lambda i: (0, 0))
    return pl.pallas_call(
        _memkv_kernel,
        grid=(r // tm,),
        in_specs=[row(d), const((1, d)), const((d, 2 * XATTN_WIDTH)), const((1, XATTN_HEAD_DIM))],
        out_specs=[row(XATTN_WIDTH), row(XATTN_WIDTH)],
        out_shape=[jax.ShapeDtypeStruct((r, XATTN_WIDTH), BF16)] * 2,
        compiler_params=_cparams("arbitrary"),
        name="mem_kv_proj",
    )(mem2, gain, w_kv, kgain)


def _xattn_kernel(q_ref, k_ref, v_ref, qg_ref, og_ref, out_ref):
    outs = []
    for hd in range(XATTN_HEADS):
        sl = slice(hd * XATTN_HEAD_DIM, (hd + 1) * XATTN_HEAD_DIM)
        qn = _rms(q_ref[:, sl].astype(F32), qg_ref[...]).astype(BF16)
        l = lax.dot_general(qn, k_ref[:, sl], _NT, preferred_element_type=F32)
        m = jnp.max(l, axis=-1, keepdims=True)
        p = jnp.exp(l - m)
        den = jnp.sum(p, axis=-1, keepdims=True)
        outs.append(jnp.dot(p.astype(BF16), v_ref[:, sl], preferred_element_type=F32) / den)
    y = jnp.concatenate(outs, axis=-1)
    out_ref[...] = _rms(y, og_ref[...]).astype(out_ref.dtype)


def _xattn(qx, km, vm, qgain, ogain, batch, seq, mem_len, tq):
    nq = seq // tq
    const2 = lambda b, i: (0, 0)
    return pl.pallas_call(
        _xattn_kernel,
        grid=(batch, nq),
        in_specs=[pl.BlockSpec((tq, XATTN_WIDTH), lambda b, i: (b * nq + i, 0)),
                  pl.BlockSpec((mem_len, XATTN_WIDTH), lambda b, i: (b, 0)),
                  pl.BlockSpec((mem_len, XATTN_WIDTH), lambda b, i: (b, 0)),
                  pl.BlockSpec((1, XATTN_HEAD_DIM), const2),
                  pl.BlockSpec((1, XATTN_WIDTH), const2)],
        out_specs=pl.BlockSpec((tq, XATTN_WIDTH), lambda b, i: (b * nq + i, 0)),
        out_shape=jax.ShapeDtypeStruct((batch * seq, XATTN_WIDTH), BF16),
        compiler_params=_cparams("arbitrary", "arbitrary"),
        name="mem_xattn",
    )(qx, km, vm, qgain, ogain)


def _ssm_kernel(u_ref, bmat_ref, cmat_ref, are_ref, aim_ref, d_ref, wglu_ref, og_ref, out_ref,
                tb_ref, xr_ref, xi_ref, sr_ref, si_ref, *, batch, steps):
    @pl.when(pl.program_id(0) == 0)
    def _():
        sr_ref[...] = jnp.zeros_like(sr_ref)
        si_ref[...] = jnp.zeros_like(si_ref)

    for b in range(batch):
        for j in range(SSM_TILES):
            c0 = b * SSM_WIDTH + j * LANES
            tb_ref[j, pl.ds(b, steps, stride=batch), :] = u_ref[:, c0:c0 + LANES].astype(F32)
    uf = jnp.concatenate([tb_ref[j] for j in range(SSM_TILES)], axis=-1)
    u = uf.astype(BF16)
    total = steps * batch
    piece = total // SSM_PIECES
    tiles_per_half = SSM_TILES // 2
    half_states = SSM_STATES // 2

    def bu_piece(j, k):
        rows = slice(k * piece, (k + 1) * piece)
        bu = jnp.dot(u[rows, j * LANES:(j + 1) * LANES], bmat_ref[j], preferred_element_type=F32)
        xr_ref[rows, j * SSM_TILE_STATE:(j + 1) * SSM_TILE_STATE] = bu[:, :SSM_TILE_STATE]
        xi_ref[rows, j * SSM_TILE_STATE:(j + 1) * SSM_TILE_STATE] = bu[:, SSM_TILE_STATE:]

    def c_piece(j, k):
        rows = slice(k * piece, (k + 1) * piece)
        sl = slice(j * SSM_TILE_STATE, (j + 1) * SSM_TILE_STATE)
        xcat = jnp.concatenate([xr_ref[rows, sl], xi_ref[rows, sl]], axis=-1).astype(BF16)
        return jnp.dot(xcat, cmat_ref[j], preferred_element_type=F32)

    def scan_half(hf, between):
        cs = slice(hf * half_states, (hf + 1) * half_states)
        ar = jnp.broadcast_to(are_ref[:, cs], (batch, half_states))
        ai = jnp.broadcast_to(aim_ref[:, cs], (batch, half_states))
        s_r, s_i = sr_ref[:, cs], si_ref[:, cs]
        every = steps // len(between)
        for t in range(steps):
            rows = slice(t * batch, (t + 1) * batch)
            s_r, s_i = (ar * s_r - ai * s_i + xr_ref[rows, cs], ar * s_i + ai * s_r + xi_ref[rows, cs])
            xr_ref[rows, cs] = s_r
            xi_ref[rows, cs] = s_i
            if t % every == every - 1:
                between[t // every]()
        sr_ref[:, cs] = s_r
        si_ref[:, cs] = s_i

    first = [(j, k) for j in range(tiles_per_half) for k in range(SSM_PIECES)]
    second = [(j, k) for j in range(tiles_per_half, SSM_TILES) for k in range(SSM_PIECES)]
    for j, k in first:
        bu_piece(j, k)
    scan_half(0, [functools.partial(bu_piece, j, k) for j, k in second])
    y_pieces = {}
    scan_half(1, [functools.partial(lambda j, k: y_pieces.__setitem__((j, k), c_piece(j, k)), j, k)
                  for j, k in first])
    for j, k in second:
        y_pieces[(j, k)] = c_piece(j, k)
    ys = [jnp.concatenate([y_pieces[(j, k)] for k in range(SSM_PIECES)], axis=0) for j in range(SSM_TILES)]
    y = jnp.concatenate(ys, axis=-1) + d_ref[...] * uf
    y = jax.nn.gelu(y)
    y = y * jax.nn.sigmoid(jnp.dot(y.astype(BF16), wglu_ref[...], preferred_element_type=F32))
    y = _rms(y, og_ref[...])
    for j in range(SSM_TILES):
        tb_ref[j] = y[:, j * LANES:(j + 1) * LANES]
    for b in range(batch):
        for j in range(SSM_TILES):
            c0 = b * SSM_WIDTH + j * LANES
            out_ref[:, c0:c0 + LANES] = tb_ref[j, pl.ds(b, steps, stride=batch), :].astype(out_ref.dtype)


def _ssm(u_sb, bmat, cmat, a_re, a_im, d_skip, w_glu, ogain, batch, seq, steps):
    rows = steps * batch
    const2 = lambda c: (0, 0)
    const3 = lambda c: (0, 0, 0)
    return pl.pallas_call(
        functools.partial(_ssm_kernel, batch=batch, steps=steps),
        grid=(seq // steps,),
        in_specs=[pl.BlockSpec((steps, batch * SSM_WIDTH), lambda c: (c, 0)),
                  pl.BlockSpec(bmat.shape, const3), pl.BlockSpec(cmat.shape, const3),
                  pl.BlockSpec((1, SSM_STATES), const2), pl.BlockSpec((1, SSM_STATES), const2),
                  pl.BlockSpec((1, SSM_WIDTH), const2),
                  pl.BlockSpec((SSM_WIDTH, SSM_WIDTH), const2),
                  pl.BlockSpec((1, SSM_WIDTH), const2)],
        out_specs=pl.BlockSpec((steps, batch * SSM_WIDTH), lambda c: (c, 0)),
        out_shape=jax.ShapeDtypeStruct((seq, batch * SSM_WIDTH), BF16),
        scratch_shapes=[pltpu.VMEM((SSM_TILES, rows, LANES), F32),
                        pltpu.VMEM((rows, SSM_STATES), F32), pltpu.VMEM((rows, SSM_STATES), F32),
                        pltpu.VMEM((batch, SSM_STATES), F32), pltpu.VMEM((batch, SSM_STATES), F32)],
        compiler_params=_cparams("arbitrary"),
        name="s5_layer",
    )(u_sb, bmat, cmat, a_re, a_im, d_skip, w_glu, ogain)


def _outproj_kernel(ya_ref, ys_ref, yx_ref, x_ref, wo_ref, g_ref, wr_ref, x1_ref, lt_ref, hp_ref, *, sub):
    for r0 in range(0, x_ref.shape[0], sub):
        rows = slice(r0, r0 + sub)
        mix = jnp.concatenate([ya_ref[rows, :], ys_ref[rows, :], yx_ref[rows, :]], axis=-1)
        x1 = x_ref[rows, :] + jnp.dot(mix, wo_ref[...], preferred_element_type=F32)
        x1_ref[rows, :] = x1
        h2 = _rms(x1, g_ref[...])
        hi = h2.astype(BF16)
        lt_ref[rows, :] = jnp.dot(hi, wr_ref[...], preferred_element_type=F32)
        hp_ref[rows, :] = _pack_rows(hi.astype(F32))


def _outproj(ya, ys_sb, yx, x2, w_o, gain, wr, batch, seq, tm):
    t, d = x2.shape
    nsb = seq // tm
    row = lambda w: pl.BlockSpec((tm, w), lambda i: (i, 0))
    const = lambda shape: pl.BlockSpec(shape, lambda i: (0, 0), pipeline_mode=pl.Buffered(1))
    return pl.pallas_call(
        functools.partial(_outproj_kernel, sub=min(256, tm)),
        grid=(t // tm,),
        in_specs=[row(ATTN_WIDTH),
                  pl.BlockSpec((tm, SSM_WIDTH), lambda i: (i % nsb, i // nsb)),
                  row(XATTN_WIDTH), row(d),
                  const(w_o.shape), const((1, d)), const(wr.shape)],
        out_specs=[row(d), row(LANES), row(d // 2)],
        out_shape=[jax.ShapeDtypeStruct((t, d), F32), jax.ShapeDtypeStruct((t, LANES), F32),
                   jax.ShapeDtypeStruct((t, d // 2), jnp.uint32)],
        compiler_params=_cparams("arbitrary"),
        name="out_proj_router",
    )(ya, ys_sb, yx, x2, w_o, gain, wr)


def _route_kernel(lt_ref, tri_ref, dest_ref, w_ref, meta_ref, cnt_ref, carry_ref, pstart_ref):
    phase = pl.program_id(0)
    c = pl.program_id(1)
    logits = lt_ref[...].T
    tc = logits.shape[1]
    ng, epg = N_EXPERT_GROUPS, EXPERTS_PER_GROUP
    row8 = lax.broadcasted_iota(jnp.int32, (ng, tc), 0)

    gl = logits[0:ng]
    gmax = jnp.max(gl, axis=0, keepdims=True)
    gidx = jnp.min(jnp.where(gl == gmax, row8, ng), axis=0, keepdims=True)
    gate = 1.0 / jnp.sum(jnp.exp(gl - gmax), axis=0, keepdims=True)
    sel = jnp.zeros((epg, tc), F32)
    for g in range(ng):
        sel = jnp.where(gidx == g, logits[ng + g * epg:ng + (g + 1) * epg], sel)
    v1 = jnp.max(sel, axis=0, keepdims=True)
    i1 = jnp.min(jnp.where(sel == v1, row8, epg), axis=0, keepdims=True)
    sel2 = jnp.where(row8 == i1, -jnp.inf, sel)
    v2 = jnp.max(sel2, axis=0, keepdims=True)
    i2 = jnp.min(jnp.where(sel2 == v2, row8, epg), axis=0, keepdims=True)
    e = jnp.exp(v2 - v1)
    w1 = gate * (1.0 / (1.0 + e))
    w2 = gate * (e / (1.0 + e))
    e1 = gidx * epg + i1
    e2 = gidx * epg + i2
    rowe = lax.broadcasted_iota(jnp.int32, (N_EXPERTS, tc), 0)
    oh1 = rowe == e1
    oh2 = rowe == e2
    member = jnp.where(jnp.logical_or(oh1, oh2), 1.0, 0.0)
    chunk_cnt = jnp.sum(member, axis=1, keepdims=True)

    @pl.when(phase == 0)
    def _():
        @pl.when(c == 0)
        def _():
            cnt_ref[...] = jnp.zeros_like(cnt_ref)
        cnt_ref[...] += chunk_cnt

    @pl.when(phase == 1)
    def _():
        @pl.when(c == 0)
        def _():
            cnt = cnt_ref[...]
            nblk = jnp.floor((cnt + (MOE_BLOCK - 1)) * (1.0 / MOE_BLOCK))
            nchunk = jnp.floor((nblk + (CHUNK_BLOCKS - 1)) * (1.0 / CHUNK_BLOCKS))
            r = lax.broadcasted_iota(jnp.int32, (N_EXPERTS, LANES), 0)
            cidx = lax.broadcasted_iota(jnp.int32, (N_EXPERTS, LANES), 1)
            to_row = lambda col: jnp.sum(jnp.where(r == cidx, col, 0.0), axis=0, keepdims=True)
            cumsum_col = lambda col: jnp.sum(jnp.where(cidx <= r, to_row(col), 0.0), axis=1, keepdims=True)
            cumsum_row = lambda col: jnp.sum(jnp.where(r <= cidx, col, 0.0), axis=0, keepdims=True)
            bend = cumsum_col(nblk)
            bstart = bend - nblk
            cend = cumsum_col(nchunk)
            cstart = cend - nchunk
            pstart_ref[...] = bstart * MOE_BLOCK
            carry_ref[...] = jnp.zeros_like(carry_ref)
            lanef = lax.broadcasted_iota(jnp.int32, (1, LANES), 1).astype(F32)
            owner = jnp.minimum(jnp.sum(jnp.where(cend <= lanef, 1.0, 0.0), axis=0, keepdims=True),
                                N_EXPERTS - 1.0)
            own = r.astype(F32) == owner
            pick = lambda col: jnp.sum(jnp.where(own, col, 0.0), axis=0, keepdims=True)
            idx = lanef - pick(cstart)
            first = pick(bstart) + CHUNK_BLOCKS * idx
            size = jnp.clip(pick(nblk) - CHUNK_BLOCKS * idx, 0.0, float(CHUNK_BLOCKS))
            zero = jnp.zeros((1, LANES), F32)
            rows = [owner, first, size,
                    zero + jnp.sum(nchunk, axis=0, keepdims=True),
                    cumsum_row(nblk) - 1.0,
                    to_row(nblk),
                    zero + jnp.sum(nblk, axis=0, keepdims=True),
                    pick(nchunk)]
            for k, v in enumerate(rows):
                meta_ref[k:k + 1, :] = v.astype(jnp.int32)

        before = carry_ref[...] + jnp.dot(member.astype(BF16), tri_ref[...], preferred_element_type=F32)
        pos = before + pstart_ref[...]
        dest_ref[0:1, :] = jnp.sum(jnp.where(oh1, pos, 0.0), axis=0, keepdims=True).astype(jnp.int32)
        dest_ref[1:2, :] = jnp.sum(jnp.where(oh2, pos, 0.0), axis=0, keepdims=True).astype(jnp.int32)
        w_ref[0:1, :] = w1
        w_ref[1:2, :] = w2
        carry_ref[...] += chunk_cnt


def _route(logits_t, tc):
    t = logits_t.shape[0]
    nc = t // tc
    tri = jnp.asarray(np.triu(np.ones((tc, tc), np.float32), k=1), dtype=BF16)
    return pl.pallas_call(
        _route_kernel,
        grid=(2, nc),
        in_specs=[pl.BlockSpec((tc, LANES), lambda p, c: (c, 0)),
                  pl.BlockSpec((tc, tc), lambda p, c: (0, 0))],
        out_specs=[pl.BlockSpec((2, tc), lambda p, c: (0, c * p)),
                   pl.BlockSpec((2, tc), lambda p, c: (0, c * p)),
                   pl.BlockSpec((META_ROWS, LANES), lambda p, c: (0, 0))],
        out_shape=[jax.ShapeDtypeStruct((2, t), jnp.int32), jax.ShapeDtypeStruct((2, t), F32),
                   jax.ShapeDtypeStruct((META_ROWS, LANES), jnp.int32)],
        scratch_shapes=[pltpu.VMEM((N_EXPERTS, 1), F32)] * 3,
        compiler_params=_cparams("arbitrary", "arbitrary"),
        name="moe_route",
    )(logits_t, tri)


def _meta(meta_ref, row, lane=0):
    return meta_ref[row * LANES + lane]


def _fill_blocks(meta_ref, zbuf, dst_ref, sem, n_blocks, *, expert_tails):
    zbuf[...] = jnp.zeros_like(zbuf)
    n_used = _meta(meta_ref, M_NUSED)
    block = lambda b: pltpu.make_async_copy(zbuf, dst_ref.at[pl.ds(b * MOE_BLOCK, MOE_BLOCK), :], sem)

    def tails(fn):
        def body(e, carry):
            @pl.when(_meta(meta_ref, M_NBLK, e) > 0)
            def _():
                fn(block(_meta(meta_ref, M_LAST, e)))
            return carry
        lax.fori_loop(0, N_EXPERTS, body, 0)

    def unused(fn):
        def body(b, carry):
            fn(block(b))
            return carry
        lax.fori_loop(n_used, n_blocks, body, 0)

    for phase in (lambda cp: cp.start(), lambda cp: cp.wait()):
        if expert_tails:
            tails(phase)
        unused(phase)


def _dispatch_kernel(dest_ref, meta_ref, h_ref, xs_ref, zbuf, sem_z, sem, *, tokens, n_blocks):
    tm = h_ref.shape[0]

    @pl.when(pl.program_id(0) == 0)
    def _():
        _fill_blocks(meta_ref, zbuf, xs_ref, sem_z, n_blocks, expert_tails=True)

    base = pl.program_id(0) * tm

    def issue(g, carry):
        r8 = pl.multiple_of(g * SUBLANES, SUBLANES)
        for s in range(SUBLANES):
            for k in range(2):
                d = dest_ref[k * tokens + base + r8 + s]
                pltpu.make_async_copy(h_ref.at[pl.ds(r8 + s, 1), :], xs_ref.at[pl.ds(d, 1), :], sem).start()
        return carry
    lax.fori_loop(0, tm // SUBLANES, issue, 0)
    for k in range(2):
        pltpu.make_async_copy(h_ref, xs_ref.at[pl.ds(0, tm), :], sem).wait()


def _dispatch(dest_flat, meta_flat, h2p, n_blocks, tm):
    t, w = h2p.shape
    grid_spec = pltpu.PrefetchScalarGridSpec(
        num_scalar_prefetch=2,
        grid=(t // tm,),
        in_specs=[pl.BlockSpec((tm, w), lambda i, d, m: (i, 0))],
        out_specs=pl.BlockSpec(memory_space=pl.ANY),
        scratch_shapes=[pltpu.VMEM((MOE_BLOCK, w), h2p.dtype),
                        pltpu.SemaphoreType.DMA(()), pltpu.SemaphoreType.DMA(())],
    )
    return pl.pallas_call(
        functools.partial(_dispatch_kernel, tokens=t, n_blocks=n_blocks),
        grid_spec=grid_spec,
        out_shape=jax.ShapeDtypeStruct((n_blocks * MOE_BLOCK, w), h2p.dtype),
        compiler_params=_cparams("arbitrary"),
        name="moe_dispatch",
    )(dest_flat, meta_flat, h2p)


def _pack_rows(x):
    bits = lax.bitcast_convert_type(x, jnp.uint32)
    half = x.shape[1] // 2
    return (bits[:, half:] & jnp.uint32(0xFFFF0000)) | (bits[:, :half] >> 16)


def _unpack_halves(words):
    lo = lax.bitcast_convert_type(words << 16, F32)
    hi = lax.bitcast_convert_type(words & jnp.uint32(0xFFFF0000), F32)
    return lo, hi


def _unpack_rows(words):
    return jnp.concatenate(_unpack_halves(words), axis=-1).astype(BF16)


def _expert_kernel(meta_ref, xs_ref, wg_ref, wu_ref, wd_ref, yb_ref,
                   xbuf, ybuf, zbuf, wg_f32, wu_f32, wd_f32, wslot_ref,
                   sem_in, sem_out, sem_z, sem_w, *, n_blocks):
    c = pl.program_id(0)
    n_chunks = _meta(meta_ref, M_NCHUNK)
    slot = c % 2

    def weight_copies(k, s):
        e = _meta(meta_ref, M_OWNER, k)
        return [pltpu.make_async_copy(src.at[e], dst.at[s], sem_w.at[s])
                for src, dst in ((wg_ref, wg_f32), (wu_ref, wu_f32), (wd_ref, wd_f32))]

    def in_copy(k, s, nb):
        rows = nb * MOE_BLOCK
        src = xs_ref.at[pl.ds(_meta(meta_ref, M_FIRST, k) * MOE_BLOCK, rows), :]
        return pltpu.make_async_copy(src, xbuf.at[s, pl.ds(0, rows), :], sem_in.at[s])

    def out_copy(k, s, nb):
        rows = nb * MOE_BLOCK
        dst = yb_ref.at[pl.ds(_meta(meta_ref, M_FIRST, k) * MOE_BLOCK, rows), :]
        return pltpu.make_async_copy(ybuf.at[s, pl.ds(0, rows), :], dst, sem_out.at[s])

    def by_size(k, fn):
        for nb in range(1, CHUNK_BLOCKS + 1):
            pl.when(_meta(meta_ref, M_SIZE, k) == nb)(functools.partial(fn, nb))

    @pl.when(c == 0)
    def _():
        by_size(0, lambda nb: in_copy(0, 0, nb).start())

    @pl.when(c + 1 < n_chunks)
    def _():
        by_size(c + 1, lambda nb: in_copy(c + 1, 1 - slot, nb).start())

    @pl.when(c < n_chunks)
    def _():
        prev = jnp.maximum(c - 1, 0)
        new_expert = jnp.logical_or(c == 0, _meta(meta_ref, M_OWNER, c) != _meta(meta_ref, M_OWNER, prev))

        @pl.when(c == 0)
        def _():
            wslot_ref[0] = 0
            for cp in weight_copies(0, 0):
                cp.start()

        @pl.when(new_expert)
        def _():
            ws = jnp.where(c == 0, 0, 1 - wslot_ref[0])
            wslot_ref[0] = ws
            nxt = c + _meta(meta_ref, M_OWNER_CHUNKS, c)

            @pl.when(nxt < n_chunks)
            def _():
                for cp in weight_copies(nxt, 1 - ws):
                    cp.start()
            for cp in weight_copies(c, ws):
                cp.wait()

        @pl.when(c >= 2)
        def _():
            by_size(c - 2, lambda nb: out_copy(c - 2, slot, nb).wait())

        def compute(nb):
            rows = nb * MOE_BLOCK
            ws = wslot_ref[0]
            in_copy(c, slot, nb).wait()
            h = _unpack_rows(xbuf[slot, 0:rows, :])
            gate = jnp.dot(h, wg_f32[ws].astype(BF16), preferred_element_type=F32)
            up = jnp.dot(h, wu_f32[ws].astype(BF16), preferred_element_type=F32)
            act = (jax.nn.silu(gate) * up).astype(BF16)
            y = jnp.dot(act, wd_f32[ws].astype(BF16), preferred_element_type=F32)
            ybuf[slot, 0:rows, :] = _pack_rows(y.astype(BF16).astype(F32))
            out_copy(c, slot, nb).start()
        by_size(c, compute)

    @pl.when(c == pl.num_programs(0) - 1)
    def _():
        _fill_blocks(meta_ref, zbuf, yb_ref, sem_z, n_blocks, expert_tails=False)
        for back in (2, 1):
            @pl.when(n_chunks >= back)
            def _(back=back):
                k = n_chunks - back
                by_size(k, lambda nb: out_copy(k, k % 2, nb).wait())


def _experts(meta_flat, xs, w_gate, w_up, w_down, n_blocks, n_chunks_max):
    d, de = w_gate.shape[1], w_gate.shape[2]
    rows = CHUNK_BLOCKS * MOE_BLOCK

    hbm = pl.BlockSpec(memory_space=pl.ANY)
    grid_spec = pltpu.PrefetchScalarGridSpec(
        num_scalar_prefetch=1,
        grid=(n_chunks_max,),
        in_specs=[hbm, hbm, hbm, hbm],
        out_specs=hbm,
        scratch_shapes=[pltpu.VMEM((2, rows, xs.shape[1]), xs.dtype),
                        pltpu.VMEM((2, rows, d // 2), jnp.uint32),
                        pltpu.VMEM((MOE_BLOCK, d // 2), jnp.uint32),
                        pltpu.VMEM((2, d, de), F32), pltpu.VMEM((2, d, de), F32), pltpu.VMEM((2, de, d), F32),
                        pltpu.SMEM((1,), jnp.int32),
                        pltpu.SemaphoreType.DMA((2,)), pltpu.SemaphoreType.DMA((2,)),
                        pltpu.SemaphoreType.DMA(()), pltpu.SemaphoreType.DMA((2,))],
    )
    return pl.pallas_call(
        functools.partial(_expert_kernel, n_blocks=n_blocks),
        grid_spec=grid_spec,
        out_shape=jax.ShapeDtypeStruct((n_blocks * MOE_BLOCK, d // 2), jnp.uint32),
        compiler_params=_cparams("arbitrary"),
        name="moe_experts",
    )(meta_flat, xs, w_gate, w_up, w_down)


def _combine_kernel(dest_ref, yb_ref, x1_ref, w_ref, out_ref, gbuf, sem, *, tokens):
    i = pl.program_id(0)
    tm = x1_ref.shape[0]
    slot = i % 2

    def gather(tile, s):
        base = tile * tm

        def issue(g, carry):
            r8 = pl.multiple_of(g * SUBLANES, SUBLANES)
            for sub in range(SUBLANES):
                for k in range(2):
                    d = dest_ref[k * tokens + base + r8 + sub]
                    pltpu.make_async_copy(yb_ref.at[pl.ds(d, 1), :], gbuf.at[s, k, pl.ds(r8 + sub, 1), :],
                                          sem.at[s]).start()
            return carry
        lax.fori_loop(0, tm // SUBLANES, issue, 0)

    @pl.when(i == 0)
    def _():
        gather(0, 0)

    @pl.when(i + 1 < pl.num_programs(0))
    def _():
        gather(i + 1, 1 - slot)

    for k in range(2):
        pltpu.make_async_copy(yb_ref.at[pl.ds(0, tm), :], gbuf.at[slot, k], sem.at[slot]).wait()
    w = w_ref[...]
    half = x1_ref.shape[1] // 2
    lo0, hi0 = _unpack_halves(gbuf[slot, 0])
    lo1, hi1 = _unpack_halves(gbuf[slot, 1])
    out_ref[:, :half] = x1_ref[:, :half] + (lo0 * w[:, 0:1] + lo1 * w[:, 1:2])
    out_ref[:, half:] = x1_ref[:, half:] + (hi0 * w[:, 0:1] + hi1 * w[:, 1:2])


def _combine(dest_flat, yb, x1, w_tok, tm):
    t, d = x1.shape
    grid_spec = pltpu.PrefetchScalarGridSpec(
        num_scalar_prefetch=1,
        grid=(t // tm,),
        in_specs=[pl.BlockSpec(memory_space=pl.ANY),
                  pl.BlockSpec((tm, d), lambda i, dr: (i, 0)),
                  pl.BlockSpec((tm, 2), lambda i, dr: (i, 0))],
        out_specs=pl.BlockSpec((tm, d), lambda i, dr: (i, 0)),
        scratch_shapes=[pltpu.VMEM((2, 2, tm, d // 2), jnp.uint32), pltpu.SemaphoreType.DMA((2,))],
    )
    return pl.pallas_call(
        functools.partial(_combine_kernel, tokens=t),
        grid_spec=grid_spec,
        out_shape=jax.ShapeDtypeStruct((t, d), F32),
        compiler_params=_cparams("arbitrary"),
        name="moe_combine",
    )(dest_flat, yb, x1, w_tok)


def _row(v):
    return v.astype(F32).reshape(1, -1)


def _layer(x2, mem2, batch, seq, mem_len, p):
    t, d = x2.shape

    tm_in = min(512, seq)
    q, kk, vv, u_sb, qx = _inproj(x2, _row(p["norm_mix"]), p["w_in"].astype(BF16), batch, seq, tm_in)

    bias = _bias_table(p["rel_bias"])
    col = lambda v: jnp.broadcast_to(v.astype(F32)[:, None], (v.shape[0], BLOCK))
    qgain = col(p["q_norm"]) * (1.0 / math.sqrt(ATTN_HEAD_DIM))
    kgain = jnp.tile(_row(p["k_norm"]), (1, ATTN_KV_HEADS))
    ya = _swa(q, kk, vv, bias, p["attn_sinks"].astype(F32), qgain, kgain, col(p["out_norm_attn"]), batch, seq)

    km, vm = _memkv(mem2, _row(p["mem_norm"]), p["w_mem_kv"].astype(BF16), _row(p["xk_norm"]),
                    min(256, mem2.shape[0]))
    xq_gain = _row(p["xq_norm"]) * (1.0 / math.sqrt(XATTN_HEAD_DIM))
    yx = _xattn(qx, km, vm, xq_gain, _row(p["out_norm_xattn"]), batch, seq, mem_len, min(512, seq))

    a_re, a_im, bbr, bbi = _ssm_prep(p["ssm_lambda_re"], p["ssm_lambda_im"], p["ssm_log_dt"],
                                     p["ssm_b_re"], p["ssm_b_im"])
    bmat = jnp.concatenate([_block_diag_tiles(bbr), _block_diag_tiles(bbi)], axis=-1).astype(BF16)
    c_re_t = jnp.transpose(p["ssm_c_re"].astype(F32), (0, 2, 1))
    c_im_t = jnp.transpose(p["ssm_c_im"].astype(F32), (0, 2, 1))
    cmat = jnp.concatenate([_block_diag_tiles(c_re_t), _block_diag_tiles(-c_im_t)], axis=1).astype(BF16)
    steps = min(64, seq)
    ys_sb = _ssm(u_sb, bmat, cmat,
                 a_re.reshape(1, SSM_STATES), a_im.reshape(1, SSM_STATES), _row(p["ssm_d"]),
                 p["ssm_w_glu"].astype(BF16), _row(p["out_norm_ssm"]), batch, seq, steps)

    wr = jnp.concatenate([p["w_router_group"], p["w_router_expert"]], axis=1).astype(F32)
    wr = jnp.pad(wr, ((0, 0), (0, LANES - wr.shape[1]))).astype(BF16)
    tm_out = min(512, seq)
    x1, logits_t, h2p = _outproj(ya, ys_sb, yx, x2, p["w_o"].astype(BF16), _row(p["norm_ffn"]), wr,
                                 batch, seq, tm_out)

    dest, w_k, meta = _route(logits_t, min(2048, t))
    n_blocks = (2 * t) // MOE_BLOCK + N_EXPERTS
    n_chunks_max = (n_blocks + (CHUNK_BLOCKS - 1) * N_EXPERTS) // CHUNK_BLOCKS
    dest_flat = dest.reshape(2 * t)
    meta_flat = meta.reshape(META_ROWS * LANES)
    xs = _dispatch(dest_flat, meta_flat, h2p, n_blocks, min(2048, t))
    yb = _experts(meta_flat, xs, p["w_gate"], p["w_up"], p["w_down"], n_blocks, n_chunks_max)
    return _combine(dest_flat, yb, x1, w_k.T, min(512, t))


def kernel(x, mem, norm_mix, w_in, q_norm, k_norm, attn_sinks, rel_bias, ssm_lambda_re, ssm_lambda_im, ssm_log_dt, ssm_b_re, ssm_b_im, ssm_c_re, ssm_c_im, ssm_d, ssm_w_glu, mem_norm, w_mem_kv, xq_norm, xk_norm, out_norm_attn, out_norm_ssm, out_norm_xattn, w_o, norm_ffn, w_router_group, w_router_expert, w_gate, w_up, w_down):
    batch, seq, d = x.shape
    mem_len = mem.shape[1]
    per_layer = dict(norm_mix=norm_mix, w_in=w_in, q_norm=q_norm, k_norm=k_norm, attn_sinks=attn_sinks,
                     ssm_lambda_re=ssm_lambda_re, ssm_lambda_im=ssm_lambda_im, ssm_log_dt=ssm_log_dt,
                     ssm_b_re=ssm_b_re, ssm_b_im=ssm_b_im, ssm_c_re=ssm_c_re, ssm_c_im=ssm_c_im,
                     ssm_d=ssm_d, ssm_w_glu=ssm_w_glu, mem_norm=mem_norm, w_mem_kv=w_mem_kv,
                     xq_norm=xq_norm, xk_norm=xk_norm, out_norm_attn=out_norm_attn,
                     out_norm_ssm=out_norm_ssm, out_norm_xattn=out_norm_xattn, w_o=w_o, norm_ffn=norm_ffn,
                     w_router_group=w_router_group, w_router_expert=w_router_expert,
                     w_gate=w_gate, w_up=w_up, w_down=w_down)
    x2 = x.astype(F32).reshape(batch * seq, d)
    mem2 = mem.astype(F32).reshape(batch * mem_len, d)
    for l in range(norm_mix.shape[0]):
        p = {k: v[l] for k, v in per_layer.items()}
        p["rel_bias"] = rel_bias
        x2 = _layer(x2, mem2, batch, seq, mem_len, p)
    return x2.reshape(batch, seq, d).astype(x.dtype)
```

```python
import functools
import math

import numpy as np
import jax
import jax.numpy as jnp
from jax import lax
from jax.experimental import pallas as pl
from jax.experimental.pallas import tpu as pltpu

F32 = jnp.float32
BF16 = jnp.bfloat16
EPS = 1e-6

ATTN_HEADS = 16
ATTN_KV_HEADS = 2
ATTN_HEAD_DIM = 64
ATTN_WIDTH = ATTN_HEADS * ATTN_HEAD_DIM
WINDOW = 128
BLOCK = 128
REL_BUCKETS = 32
REL_MAX_DIST = 128
SSM_GROUP_CH = 16
SSM_GROUPS = 32
SSM_STATE = 64
SSM_WIDTH = SSM_GROUPS * SSM_GROUP_CH
XATTN_HEADS = 4
XATTN_HEAD_DIM = 128
XATTN_WIDTH = XATTN_HEADS * XATTN_HEAD_DIM
N_EXPERT_GROUPS = 8
EXPERTS_PER_GROUP = 8
N_EXPERTS = N_EXPERT_GROUPS * EXPERTS_PER_GROUP
D_EXPERT = 512
MOE_BLOCK = 128

LANES = 128
SUBLANES = 8
SSM_GROUPS_PER_TILE = LANES // SSM_GROUP_CH
SSM_TILES = SSM_WIDTH // LANES
SSM_TILE_STATE = SSM_GROUPS_PER_TILE * SSM_STATE
SSM_STATES = SSM_GROUPS * SSM_STATE
SSM_PIECES = 4
CHUNK_BLOCKS = 4
META_ROWS = 8
M_OWNER, M_FIRST, M_SIZE, M_NCHUNK, M_LAST, M_NBLK, M_NUSED, M_OWNER_CHUNKS = range(8)
VMEM_LIMIT = 56 * 1024 * 1024

_NT = (((1,), (1,)), ((), ()))


def _cparams(*sem):
    return pltpu.CompilerParams(dimension_semantics=sem, vmem_limit_bytes=VMEM_LIMIT)


def _rms(x, gain):
    ms = jnp.mean(x * x, axis=-1, keepdims=True)
    return x * lax.rsqrt(ms + EPS) * gain


def _t5_bucket_table():
    qi = np.arange(BLOCK, dtype=np.int32)[:, None]
    ki = np.arange(2 * BLOCK, dtype=np.int32)[None, :]
    delta = BLOCK + qi - ki
    n = np.maximum(delta, 0)
    max_exact = REL_BUCKETS // 2
    nf = np.maximum(n, 1).astype(np.float32)
    large = max_exact + (np.log(nf / np.float32(max_exact)) / np.float32(math.log(REL_MAX_DIST / max_exact))
                         * np.float32(REL_BUCKETS - max_exact)).astype(np.int32)
    large = np.minimum(large, REL_BUCKETS - 1)
    return np.where(n < max_exact, n, large).astype(np.int32)


def _upper(cols):
    k = lax.broadcasted_iota(jnp.int32, (BLOCK, cols), 0)
    q = lax.broadcasted_iota(jnp.int32, (BLOCK, cols), 1) % BLOCK
    return k > q


def _bias_kernel(rb_ref, bucket_ref, out_ref):
    pair = pl.program_id(0)
    bucket = bucket_ref[...]
    upper = _upper(BLOCK)
    for half in range(2):
        acc = jnp.zeros(bucket.shape, F32)
        for b in range(REL_BUCKETS):
            acc = jnp.where(bucket == b, rb_ref[b, 2 * pair + half], acc)
        cols = slice(half * BLOCK, (half + 1) * BLOCK)
        out_ref[0, 0, :, cols] = jnp.where(upper, acc[:BLOCK], acc[BLOCK:])
        out_ref[1, 0, :, cols] = jnp.where(upper, jnp.float32(-1e30), acc[BLOCK:])


def _bias_table(rel_bias):
    bucket = jnp.asarray(_t5_bucket_table().T)
    pairs = ATTN_HEADS // 2
    return pl.pallas_call(
        _bias_kernel,
        grid=(pairs,),
        in_specs=[pl.BlockSpec(memory_space=pltpu.SMEM),
                  pl.BlockSpec((2 * BLOCK, BLOCK), lambda h: (0, 0))],
        out_specs=pl.BlockSpec((2, 1, BLOCK, 2 * BLOCK), lambda h: (0, h, 0, 0)),
        out_shape=jax.ShapeDtypeStruct((2, pairs, BLOCK, 2 * BLOCK), F32),
        compiler_params=_cparams("arbitrary"),
        name="t5_bias_table",
    )(rel_bias.astype(F32), bucket)


def _ssm_prep_kernel(lr_ref, li_ref, ldt_ref, br_ref, bi_ref, are_ref, aim_ref, bbr_ref, bbi_ref):
    lr = lr_ref[...]
    li = li_ref[...]
    dt = jnp.exp(ldt_ref[...])
    mag = jnp.exp(lr * dt)
    a_re = mag * jnp.cos(li * dt)
    a_im = mag * jnp.sin(li * dt)
    den = lr * lr + li * li
    nr = a_re - 1.0
    ni = a_im
    coef_re = (nr * lr + ni * li) / den
    coef_im = (ni * lr - nr * li) / den
    are_ref[...] = a_re
    aim_ref[...] = a_im
    br = br_ref[...]
    bi = bi_ref[...]
    bbr_ref[...] = coef_re * br - coef_im * bi
    bbi_ref[...] = coef_re * bi + coef_im * br


def _ssm_prep(lam_re, lam_im, log_dt, b_re, b_im):
    g, n, c = b_re.shape
    vec = jax.ShapeDtypeStruct((g, 1, n), F32)
    mat = jax.ShapeDtypeStruct((g, c, n), F32)
    return pl.pallas_call(
        _ssm_prep_kernel,
        out_shape=(vec, vec, mat, mat),
        name="ssm_discretise",
    )(lam_re.astype(F32).reshape(g, 1, n), lam_im.astype(F32).reshape(g, 1, n),
      log_dt.astype(F32).reshape(g, 1, 1),
      jnp.transpose(b_re.astype(F32), (0, 2, 1)), jnp.transpose(b_im.astype(F32), (0, 2, 1)))


def _block_diag_tiles(m):
    g, r, c = m.shape
    t = g // SSM_GROUPS_PER_TILE
    eye = jnp.eye(SSM_GROUPS_PER_TILE, dtype=m.dtype)
    m4 = m.reshape(t, SSM_GROUPS_PER_TILE, r, c)
    out = m4[:, :, :, None, :] * eye[None, :, None, :, None]
    return out.reshape(t, SSM_GROUPS_PER_TILE * r, SSM_GROUPS_PER_TILE * c)


def _k_norm(k, kgain):
    dh = ATTN_HEAD_DIM
    lo = lax.broadcasted_iota(jnp.int32, (1, ATTN_KV_HEADS * dh), 1) < dh
    sq = k * k
    s_lo = jnp.sum(jnp.where(lo, sq, 0.0), axis=-1, keepdims=True)
    s_hi = jnp.sum(sq, axis=-1, keepdims=True) - s_lo
    return k * jnp.where(lo, lax.rsqrt(s_lo / dh + EPS), lax.rsqrt(s_hi / dh + EPS)) * kgain


def _swa_phases(q_of, kn, v_t, bias_of, sinks_ref, qgain, ogain, emit):
    dh = ATTN_HEAD_DIM
    upper = _upper(2 * BLOCK)
    first_head = lax.broadcasted_iota(jnp.int32, (1, 2 * BLOCK), 1) < BLOCK
    zeros = jnp.zeros((dh, BLOCK), F32)
    heads_per_kv = ATTN_HEADS // ATTN_KV_HEADS
    pairs = range(ATTN_HEADS // 2)
    kv_of = lambda pair: (2 * pair) // heads_per_kv
    state = {}

    def qk_all():
        logits = []
        for pair in pairs:
            cols = []
            for half in range(2):
                qh = q_of(2 * pair + half).astype(F32)
                ms = jnp.mean(qh * qh, axis=0, keepdims=True)
                qn = qh * lax.rsqrt(ms + EPS) * qgain
                cols.append(jnp.concatenate([qn, zeros] if kv_of(pair) == 0 else [zeros, qn], axis=0))
            rhs = jnp.concatenate(cols, axis=1).astype(BF16)
            logits.append(jnp.dot(kn, rhs, preferred_element_type=F32))
        state["logits"] = logits

    def softmax_all():
        probs = []
        for pair in pairs:
            both = state["logits"][pair]
            l = jnp.where(upper, both[:BLOCK], both[BLOCK:]) + bias_of(pair)
            sink = jnp.where(first_head, sinks_ref[2 * pair], sinks_ref[2 * pair + 1])
            m = jnp.maximum(jnp.max(l, axis=0, keepdims=True), sink)
            p = jnp.exp(l - m)
            den = jnp.sum(p, axis=0, keepdims=True) + jnp.exp(sink - m)
            pz = jnp.concatenate([jnp.where(upper, p, 0.0), jnp.where(upper, 0.0, p)], axis=0).astype(BF16)
            probs.append((pz, den))
        state["probs"] = probs

    def pv_all():
        outs = []
        for pair in pairs:
            g = kv_of(pair)
            pz, den = state["probs"][pair]
            o = jnp.dot(v_t[g * dh:(g + 1) * dh, :], pz, preferred_element_type=F32) / den
            outs += [o[:, :BLOCK], o[:, BLOCK:]]
        y_t = jnp.concatenate(outs, axis=0)
        ms = jnp.mean(y_t * y_t, axis=0, keepdims=True)
        emit((y_t * lax.rsqrt(ms + EPS) * ogain).T)

    return qk_all, softmax_all, pv_all


def _inproj_swa_kernel(sinks_ref, x_ref, g_ref, w_ref, bias_ref, qg_ref, kg_ref, og_ref,
                       u_ref, qx_ref, ya_ref, q_s, k_s, v_s, kprev_s, vprev_s, *, tiles_per_seq):
    i = pl.program_id(0)
    slot = i % 2
    old = 1 - slot
    tm = x_ref.shape[0]
    blocks = tm // BLOCK
    kvw = ATTN_KV_HEADS * ATTN_HEAD_DIM

    @pl.when(i == 0)
    def _():
        q_s[1] = jnp.zeros(q_s.shape[1:], q_s.dtype)
        k_s[1] = jnp.zeros(k_s.shape[1:], k_s.dtype)
        v_s[1] = jnp.zeros(v_s.shape[1:], v_s.dtype)
        kprev_s[...] = jnp.zeros_like(kprev_s)
        vprev_s[...] = jnp.zeros_like(vprev_s)

    h = _rms(x_ref[...], g_ref[...]).astype(BF16)
    proj = lambda c0, c1: jnp.dot(h, w_ref[:, c0:c1], preferred_element_type=F32)
    piece = 2 * LANES

    def q_piece(n):
        def run():
            q_s[slot, n * piece:(n + 1) * piece, :] = proj(n * piece, (n + 1) * piece).T.astype(q_s.dtype)
        return run

    def kv_piece():
        kv = proj(ATTN_WIDTH, ATTN_WIDTH + 2 * kvw)
        k_s[slot] = _k_norm(kv[:, :kvw], kg_ref[...]).astype(k_s.dtype)
        v_s[slot] = kv[:, kvw:].T.astype(v_s.dtype)

    def out_piece(ref, c0, n):
        def run():
            ref[:, n * piece:(n + 1) * piece] = proj(c0 + n * piece, c0 + (n + 1) * piece).astype(ref.dtype)
        return run

    c_u = ATTN_WIDTH + 2 * kvw
    c_qx = c_u + SSM_WIDTH
    projection = ([q_piece(n) for n in range(ATTN_WIDTH // piece)] + [kv_piece]
                  + [out_piece(u_ref, c_u, n) for n in range(SSM_WIDTH // piece)]
                  + [out_piece(qx_ref, c_qx, n) for n in range(XATTN_WIDTH // piece)])

    first_of_seq = (i + tiles_per_seq - 1) % tiles_per_seq == 0
    attention = []
    for blk in range(blocks):
        own = slice(blk * BLOCK, (blk + 1) * BLOCK)
        before = slice((blk - 1) * BLOCK, blk * BLOCK)
        k_before = kprev_s[...] if blk == 0 else k_s[old, before, :]
        v_before = vprev_s[...] if blk == 0 else v_s[old, :, before]
        table = jnp.where(first_of_seq, 1, 0) if blk == 0 else 0

        def emit(y, own=own):
            ya_ref[own, :] = y.astype(ya_ref.dtype)

        attention += _swa_phases(
            q_of=lambda hd, own=own: q_s[old, hd * ATTN_HEAD_DIM:(hd + 1) * ATTN_HEAD_DIM, own],
            kn=jnp.concatenate([k_before, k_s[old, own, :]], axis=0),
            v_t=jnp.concatenate([v_before, v_s[old, :, own]], axis=1),
            bias_of=lambda pair, table=table: bias_ref[table, pair],
            sinks_ref=sinks_ref, qgain=qg_ref[...], ogain=og_ref[...], emit=emit)

    longer, shorter = (attention, projection) if len(attention) >= len(projection) else (projection, attention)
    done = 0
    for n, fn in enumerate(longer):
        fn()
        due = (n + 1) * len(shorter) // len(longer)
        for extra in shorter[done:due]:
            extra()
        done = due

    last = slice((blocks - 1) * BLOCK, blocks * BLOCK)
    kprev_s[...] = k_s[old, last, :]
    vprev_s[...] = v_s[old, :, last]


def _inproj_swa(x2, gain, w_in, bias, sinks, qgain, kgain, ogain, batch, seq, tm):
    t, d = x2.shape
    n_tiles = t // tm
    tiles_per_seq = seq // tm
    kvw = ATTN_KV_HEADS * ATTN_HEAD_DIM
    proj_tile = lambda i: jnp.minimum(i, n_tiles - 1)
    attn_tile = lambda i: jnp.maximum(i - 1, 0)
    const = lambda shape: pl.BlockSpec(shape, lambda i: (0,) * len(shape), pipeline_mode=pl.Buffered(1))
    return pl.pallas_call(
        functools.partial(_inproj_swa_kernel, tiles_per_seq=tiles_per_seq),
        grid=(n_tiles + 1,),
        in_specs=[pl.BlockSpec(memory_space=pltpu.SMEM),
                  pl.BlockSpec((tm, d), lambda i: (proj_tile(i), 0)),
                  const((1, d)), const(w_in.shape), const(bias.shape),
                  const((ATTN_HEAD_DIM, BLOCK)), const((1, kvw)), const((ATTN_WIDTH, BLOCK))],
        out_specs=[
                   pl.BlockSpec((tm, SSM_WIDTH),
                                lambda i: (proj_tile(i) % tiles_per_seq, proj_tile(i) // tiles_per_seq)),
                   pl.BlockSpec((tm, XATTN_WIDTH), lambda i: (proj_tile(i), 0)),
                   pl.BlockSpec((tm, ATTN_WIDTH), lambda i: (attn_tile(i), 0))],
        out_shape=[jax.ShapeDtypeStruct((seq, batch * SSM_WIDTH), BF16),
                   jax.ShapeDtypeStruct((t, XATTN_WIDTH), BF16),
                   jax.ShapeDtypeStruct((t, ATTN_WIDTH), BF16)],
        scratch_shapes=[pltpu.VMEM((2, ATTN_WIDTH, tm), BF16),
                        pltpu.VMEM((2, tm, kvw), BF16),
                        pltpu.VMEM((2, kvw, tm), BF16),
                        pltpu.VMEM((BLOCK, kvw), BF16), pltpu.VMEM((kvw, BLOCK), BF16)],
        compiler_params=_cparams("arbitrary"),
        name="in_proj_swa",
    )(sinks, x2, gain, w_in, bias, qgain, kgain, ogain)


def _memkv_kernel(m_ref, g_ref, w_ref, kg_ref, k_ref, v_ref):
    h = _rms(m_ref[...], g_ref[...]).astype(BF16)
    km = jnp.dot(h, w_ref[:, :XATTN_WIDTH], preferred_element_type=F32)
    for hd in range(XATTN_HEADS):
        sl = slice(hd * XATTN_HEAD_DIM, (hd + 1) * XATTN_HEAD_DIM)
        k_ref[:, sl] = _rms(km[:, sl], kg_ref[...]).astype(k_ref.dtype)
    v_ref[...] = jnp.dot(h, w_ref[:, XATTN_WIDTH:], preferred_element_type=F32).astype(v_ref.dtype)


def _memkv(mem2, gain, w_kv, kgain, tm):
    r, d = mem2.shape
    row = lambda w: pl.BlockSpec((tm, w), lambda i: (i, 0))
    const = lambda shape: pl.BlockSpec(shape, lambda i: (0, 0))
    return pl.pallas_call(
        _memkv_kernel,
        grid=(r // tm,),
        in_specs=[row(d), const((1, d)), const((d, 2 * XATTN_WIDTH)), const((1, XATTN_HEAD_DIM))],
        out_specs=[row(XATTN_WIDTH), row(XATTN_WIDTH)],
        out_shape=[jax.ShapeDtypeStruct((r, XATTN_WIDTH), BF16)] * 2,
        compiler_params=_cparams("arbitrary"),
        name="mem_kv_proj",
    )(mem2, gain, w_kv, kgain)


def _xattn_kernel(q_ref, k_ref, v_ref, qg_ref, og_ref, out_ref):
    outs = []
    for hd in range(XATTN_HEADS):
        sl = slice(hd * XATTN_HEAD_DIM, (hd + 1) * XATTN_HEAD_DIM)
        qn = _rms(q_ref[:, sl].astype(F32), qg_ref[...]).astype(BF16)
        l = lax.dot_general(qn, k_ref[:, sl], _NT, preferred_element_type=F32)
        m = jnp.max(l, axis=-1, keepdims=True)
        p = jnp.exp(l - m)
        den = jnp.sum(p, axis=-1, keepdims=True)
        outs.append(jnp.dot(p.astype(BF16), v_ref[:, sl], preferred_element_type=F32) / den)
    y = jnp.concatenate(outs, axis=-1)
    out_ref[...] = _rms(y, og_ref[...]).astype(out_ref.dtype)


def _xattn(qx, km, vm, qgain, ogain, batch, seq, mem_len, tq):
    nq = seq // tq
    const2 = lambda b, i: (0, 0)
    return pl.pallas_call(
        _xattn_kernel,
        grid=(batch, nq),
        in_specs=[pl.BlockSpec((tq, XATTN_WIDTH), lambda b, i: (b * nq + i, 0)),
                  pl.BlockSpec((mem_len, XATTN_WIDTH), lambda b, i: (b, 0)),
                  pl.BlockSpec((mem_len, XATTN_WIDTH), lambda b, i: (b, 0)),
                  pl.BlockSpec((1, XATTN_HEAD_DIM), const2),
                  pl.BlockSpec((1, XATTN_WIDTH), const2)],
        out_specs=pl.BlockSpec((tq, XATTN_WIDTH), lambda b, i: (b * nq + i, 0)),
        out_shape=jax.ShapeDtypeStruct((batch * seq, XATTN_WIDTH), BF16),
        compiler_params=_cparams("arbitrary", "arbitrary"),
        name="mem_xattn",
    )(qx, km, vm, qgain, ogain)


def _ssm_kernel(u_ref, bmat_ref, cmat_ref, are_ref, aim_ref, d_ref, wglu_ref, og_ref, out_ref,
                tb_ref, xr_ref, xi_ref, sr_ref, si_ref, *, batch, steps):
    @pl.when(pl.program_id(0) == 0)
    def _():
        sr_ref[...] = jnp.zeros_like(sr_ref)
        si_ref[...] = jnp.zeros_like(si_ref)

    for b in range(batch):
        for j in range(SSM_TILES):
            c0 = b * SSM_WIDTH + j * LANES
            tb_ref[j, pl.ds(b, steps, stride=batch), :] = u_ref[:, c0:c0 + LANES].astype(F32)
    uf = jnp.concatenate([tb_ref[j] for j in range(SSM_TILES)], axis=-1)
    u = uf.astype(BF16)
    total = steps * batch
    piece = total // SSM_PIECES
    tiles_per_half = SSM_TILES // 2
    half_states = SSM_STATES // 2

    def bu_piece(j, k):
        rows = slice(k * piece, (k + 1) * piece)
        bu = jnp.dot(u[rows, j * LANES:(j + 1) * LANES], bmat_ref[j], preferred_element_type=F32)
        xr_ref[rows, j * SSM_TILE_STATE:(j + 1) * SSM_TILE_STATE] = bu[:, :SSM_TILE_STATE]
        xi_ref[rows, j * SSM_TILE_STATE:(j + 1) * SSM_TILE_STATE] = bu[:, SSM_TILE_STATE:]

    def c_piece(j, k):
        rows = slice(k * piece, (k + 1) * piece)
        sl = slice(j * SSM_TILE_STATE, (j + 1) * SSM_TILE_STATE)
        xcat = jnp.concatenate([xr_ref[rows, sl], xi_ref[rows, sl]], axis=-1).astype(BF16)
        return jnp.dot(xcat, cmat_ref[j], preferred_element_type=F32)

    def scan_half(hf, between):
        cs = slice(hf * half_states, (hf + 1) * half_states)
        ar = jnp.broadcast_to(are_ref[:, cs], (batch, half_states))
        ai = jnp.broadcast_to(aim_ref[:, cs], (batch, half_states))
        s_r, s_i = sr_ref[:, cs], si_ref[:, cs]
        every = steps // len(between)
        for t in range(steps):
            rows = slice(t * batch, (t + 1) * batch)
            s_r, s_i = (ar * s_r - ai * s_i + xr_ref[rows, cs], ar * s_i + ai * s_r + xi_ref[rows, cs])
            xr_ref[rows, cs] = s_r
            xi_ref[rows, cs] = s_i
            if t % every == every - 1:
                between[t // every]()
        sr_ref[:, cs] = s_r
        si_ref[:, cs] = s_i

    first = [(j, k) for j in range(tiles_per_half) for k in range(SSM_PIECES)]
    second = [(j, k) for j in range(tiles_per_half, SSM_TILES) for k in range(SSM_PIECES)]
    for j, k in first:
        bu_piece(j, k)
    scan_half(0, [functools.partial(bu_piece, j, k) for j, k in second])
    y_pieces = {}
    scan_half(1, [functools.partial(lambda j, k: y_pieces.__setitem__((j, k), c_piece(j, k)), j, k)
                  for j, k in first])
    for j, k in second:
        y_pieces[(j, k)] = c_piece(j, k)
    ys = [jnp.concatenate([y_pieces[(j, k)] for k in range(SSM_PIECES)], axis=0) for j in range(SSM_TILES)]
    y = jnp.concatenate(ys, axis=-1) + d_ref[...] * uf
    y = jax.nn.gelu(y)
    y = y * jax.nn.sigmoid(jnp.dot(y.astype(BF16), wglu_ref[...], preferred_element_type=F32))
    y = _rms(y, og_ref[...])
    for j in range(SSM_TILES):
        tb_ref[j] = y[:, j * LANES:(j + 1) * LANES]
    for b in range(batch):
        for j in range(SSM_TILES):
            c0 = b * SSM_WIDTH + j * LANES
            out_ref[:, c0:c0 + LANES] = tb_ref[j, pl.ds(b, steps, stride=batch), :].astype(out_ref.dtype)


def _ssm(u_sb, bmat, cmat, a_re, a_im, d_skip, w_glu, ogain, batch, seq, steps):
    rows = steps * batch
    const2 = lambda c: (0, 0)
    const3 = lambda c: (0, 0, 0)
    return pl.pallas_call(
        functools.partial(_ssm_kernel, batch=batch, steps=steps),
        grid=(seq // steps,),
        in_specs=[pl.BlockSpec((steps, batch * SSM_WIDTH), lambda c: (c, 0)),
                  pl.BlockSpec(bmat.shape, const3), pl.BlockSpec(cmat.shape, const3),
                  pl.BlockSpec((1, SSM_STATES), const2), pl.BlockSpec((1, SSM_STATES), const2),
                  pl.BlockSpec((1, SSM_WIDTH), const2),
                  pl.BlockSpec((SSM_WIDTH, SSM_WIDTH), const2),
                  pl.BlockSpec((1, SSM_WIDTH), const2)],
        out_specs=pl.BlockSpec((steps, batch * SSM_WIDTH), lambda c: (c, 0)),
        out_shape=jax.ShapeDtypeStruct((seq, batch * SSM_WIDTH), BF16),
        scratch_shapes=[pltpu.VMEM((SSM_TILES, rows, LANES), F32),
                        pltpu.VMEM((rows, SSM_STATES), F32), pltpu.VMEM((rows, SSM_STATES), F32),
                        pltpu.VMEM((batch, SSM_STATES), F32), pltpu.VMEM((batch, SSM_STATES), F32)],
        compiler_params=_cparams("arbitrary"),
        name="s5_layer",
    )(u_sb, bmat, cmat, a_re, a_im, d_skip, w_glu, ogain)


def _outproj_kernel(ya_ref, ys_ref, yx_ref, x_ref, wo_ref, g_ref, wr_ref, x1_ref, lt_ref, hp_ref, *, sub):
    for r0 in range(0, x_ref.shape[0], sub):
        rows = slice(r0, r0 + sub)
        mix = jnp.concatenate([ya_ref[rows, :], ys_ref[rows, :], yx_ref[rows, :]], axis=-1)
        x1 = x_ref[rows, :] + jnp.dot(mix, wo_ref[...], preferred_element_type=F32)
        x1_ref[rows, :] = x1
        h2 = _rms(x1, g_ref[...])
        hi = h2.astype(BF16)
        lt_ref[rows, :] = jnp.dot(hi, wr_ref[...], preferred_element_type=F32)
        hp_ref[rows, :] = _pack_rows(hi.astype(F32))


def _outproj(ya, ys_sb, yx, x2, w_o, gain, wr, batch, seq, tm):
    t, d = x2.shape
    nsb = seq // tm
    row = lambda w: pl.BlockSpec((tm, w), lambda i: (i, 0))
    const = lambda shape: pl.BlockSpec(shape, lambda i: (0, 0), pipeline_mode=pl.Buffered(1))
    return pl.pallas_call(
        functools.partial(_outproj_kernel, sub=min(256, tm)),
        grid=(t // tm,),
        in_specs=[row(ATTN_WIDTH),
                  pl.BlockSpec((tm, SSM_WIDTH), lambda i: (i % nsb, i // nsb)),
                  row(XATTN_WIDTH), row(d),
                  const(w_o.shape), const((1, d)), const(wr.shape)],
        out_specs=[row(d), row(LANES), row(d // 2)],
        out_shape=[jax.ShapeDtypeStruct((t, d), F32), jax.ShapeDtypeStruct((t, LANES), F32),
                   jax.ShapeDtypeStruct((t, d // 2), jnp.uint32)],
        compiler_params=_cparams("arbitrary"),
        name="out_proj_router",
    )(ya, ys_sb, yx, x2, w_o, gain, wr)


def _route_kernel(lt_ref, tri_ref, dest_ref, w_ref, meta_ref, cnt_ref, carry_ref, pstart_ref):
    phase = pl.program_id(0)
    c = pl.program_id(1)
    logits = lt_ref[...].T
    tc = logits.shape[1]
    ng, epg = N_EXPERT_GROUPS, EXPERTS_PER_GROUP
    row8 = lax.broadcasted_iota(jnp.int32, (ng, tc), 0)

    gl = logits[0:ng]
    gmax = jnp.max(gl, axis=0, keepdims=True)
    gidx = jnp.min(jnp.where(gl == gmax, row8, ng), axis=0, keepdims=True)
    gate = 1.0 / jnp.sum(jnp.exp(gl - gmax), axis=0, keepdims=True)
    sel = jnp.zeros((epg, tc), F32)
    for g in range(ng):
        sel = jnp.where(gidx == g, logits[ng + g * epg:ng + (g + 1) * epg], sel)
    v1 = jnp.max(sel, axis=0, keepdims=True)
    i1 = jnp.min(jnp.where(sel == v1, row8, epg), axis=0, keepdims=True)
    sel2 = jnp.where(row8 == i1, -jnp.inf, sel)
    v2 = jnp.max(sel2, axis=0, keepdims=True)
    i2 = jnp.min(jnp.where(sel2 == v2, row8, epg), axis=0, keepdims=True)
    e = jnp.exp(v2 - v1)
    w1 = gate * (1.0 / (1.0 + e))
    w2 = gate * (e / (1.0 + e))
    e1 = gidx * epg + i1
    e2 = gidx * epg + i2
    rowe = lax.broadcasted_iota(jnp.int32, (N_EXPERTS, tc), 0)
    oh1 = rowe == e1
    oh2 = rowe == e2
    member = jnp.where(jnp.logical_or(oh1, oh2), 1.0, 0.0)
    chunk_cnt = jnp.sum(member, axis=1, keepdims=True)

    @pl.when(phase == 0)
    def _():
        @pl.when(c == 0)
        def _():
            cnt_ref[...] = jnp.zeros_like(cnt_ref)
        cnt_ref[...] += chunk_cnt

    @pl.when(phase == 1)
    def _():
        @pl.when(c == 0)
        def _():
            cnt = cnt_ref[...]
            nblk = jnp.floor((cnt + (MOE_BLOCK - 1)) * (1.0 / MOE_BLOCK))
            nchunk = jnp.floor((nblk + (CHUNK_BLOCKS - 1)) * (1.0 / CHUNK_BLOCKS))
            r = lax.broadcasted_iota(jnp.int32, (N_EXPERTS, LANES), 0)
            cidx = lax.broadcasted_iota(jnp.int32, (N_EXPERTS, LANES), 1)
            to_row = lambda col: jnp.sum(jnp.where(r == cidx, col, 0.0), axis=0, keepdims=True)
            cumsum_col = lambda col: jnp.sum(jnp.where(cidx <= r, to_row(col), 0.0), axis=1, keepdims=True)
            cumsum_row = lambda col: jnp.sum(jnp.where(r <= cidx, col, 0.0), axis=0, keepdims=True)
            bend = cumsum_col(nblk)
            bstart = bend - nblk
            cend = cumsum_col(nchunk)
            cstart = cend - nchunk
            pstart_ref[...] = bstart * MOE_BLOCK
            carry_ref[...] = jnp.zeros_like(carry_ref)
            lanef = lax.broadcasted_iota(jnp.int32, (1, LANES), 1).astype(F32)
            owner = jnp.minimum(jnp.sum(jnp.where(cend <= lanef, 1.0, 0.0), axis=0, keepdims=True),
                                N_EXPERTS - 1.0)
            own = r.astype(F32) == owner
            pick = lambda col: jnp.sum(jnp.where(own, col, 0.0), axis=0, keepdims=True)
            idx = lanef - pick(cstart)
            first = pick(bstart) + CHUNK_BLOCKS * idx
            size = jnp.clip(pick(nblk) - CHUNK_BLOCKS * idx, 0.0, float(CHUNK_BLOCKS))
            zero = jnp.zeros((1, LANES), F32)
            rows = [owner, first, size,
                    zero + jnp.sum(nchunk, axis=0, keepdims=True),
                    cumsum_row(nblk) - 1.0,
                    to_row(nblk),
                    zero + jnp.sum(nblk, axis=0, keepdims=True),
                    pick(nchunk)]
            for k, v in enumerate(rows):
                meta_ref[k:k + 1, :] = v.astype(jnp.int32)

        before = carry_ref[...] + jnp.dot(member.astype(BF16), tri_ref[...], preferred_element_type=F32)
        pos = before + pstart_ref[...]
        dest_ref[0:1, :] = jnp.sum(jnp.where(oh1, pos, 0.0), axis=0, keepdims=True).astype(jnp.int32)
        dest_ref[1:2, :] = jnp.sum(jnp.where(oh2, pos, 0.0), axis=0, keepdims=True).astype(jnp.int32)
        w_ref[0:1, :] = w1
        w_ref[1:2, :] = w2
        carry_ref[...] += chunk_cnt


def _route(logits_t, tc):
    t = logits_t.shape[0]
    nc = t // tc
    tri = jnp.asarray(np.triu(np.ones((tc, tc), np.float32), k=1), dtype=BF16)
    return pl.pallas_call(
        _route_kernel,
        grid=(2, nc),
        in_specs=[pl.BlockSpec((tc, LANES), lambda p, c: (c, 0)),
                  pl.BlockSpec((tc, tc), lambda p, c: (0, 0))],
        out_specs=[pl.BlockSpec((2, tc), lambda p, c: (0, c * p)),
                   pl.BlockSpec((2, tc), lambda p, c: (0, c * p)),
                   pl.BlockSpec((META_ROWS, LANES), lambda p, c: (0, 0))],
        out_shape=[jax.ShapeDtypeStruct((2, t), jnp.int32), jax.ShapeDtypeStruct((2, t), F32),
                   jax.ShapeDtypeStruct((META_ROWS, LANES), jnp.int32)],
        scratch_shapes=[pltpu.VMEM((N_EXPERTS, 1), F32)] * 3,
        compiler_params=_cparams("arbitrary", "arbitrary"),
        name="moe_route",
    )(logits_t, tri)


def _meta(meta_ref, row, lane=0):
    return meta_ref[row * LANES + lane]


def _fill_blocks(meta_ref, zbuf, dst_ref, sem, n_blocks, *, expert_tails):
    zbuf[...] = jnp.zeros_like(zbuf)
    n_used = _meta(meta_ref, M_NUSED)
    block = lambda b: pltpu.make_async_copy(zbuf, dst_ref.at[pl.ds(b * MOE_BLOCK, MOE_BLOCK), :], sem)

    def tails(fn):
        def body(e, carry):
            @pl.when(_meta(meta_ref, M_NBLK, e) > 0)
            def _():
                fn(block(_meta(meta_ref, M_LAST, e)))
            return carry
        lax.fori_loop(0, N_EXPERTS, body, 0)

    def unused(fn):
        def body(b, carry):
            fn(block(b))
            return carry
        lax.fori_loop(n_used, n_blocks, body, 0)

    for phase in (lambda cp: cp.start(), lambda cp: cp.wait()):
        if expert_tails:
            tails(phase)
        unused(phase)


def _dispatch_kernel(dest_ref, meta_ref, h_ref, xs_ref, zbuf, sem_z, sem, *, tokens, n_blocks):
    tm = h_ref.shape[0]

    @pl.when(pl.program_id(0) == 0)
    def _():
        _fill_blocks(meta_ref, zbuf, xs_ref, sem_z, n_blocks, expert_tails=True)

    base = pl.program_id(0) * tm

    def issue(g, carry):
        r8 = pl.multiple_of(g * SUBLANES, SUBLANES)
        for s in range(SUBLANES):
            for k in range(2):
                d = dest_ref[k * tokens + base + r8 + s]
                pltpu.make_async_copy(h_ref.at[pl.ds(r8 + s, 1), :], xs_ref.at[pl.ds(d, 1), :], sem).start()
        return carry
    lax.fori_loop(0, tm // SUBLANES, issue, 0)
    for k in range(2):
        pltpu.make_async_copy(h_ref, xs_ref.at[pl.ds(0, tm), :], sem).wait()


def _dispatch(dest_flat, meta_flat, h2p, n_blocks, tm):
    t, w = h2p.shape
    grid_spec = pltpu.PrefetchScalarGridSpec(
        num_scalar_prefetch=2,
        grid=(t // tm,),
        in_specs=[pl.BlockSpec((tm, w), lambda i, d, m: (i, 0))],
        out_specs=pl.BlockSpec(memory_space=pl.ANY),
        scratch_shapes=[pltpu.VMEM((MOE_BLOCK, w), h2p.dtype),
                        pltpu.SemaphoreType.DMA(()), pltpu.SemaphoreType.DMA(())],
    )
    return pl.pallas_call(
        functools.partial(_dispatch_kernel, tokens=t, n_blocks=n_blocks),
        grid_spec=grid_spec,
        out_shape=jax.ShapeDtypeStruct((n_blocks * MOE_BLOCK, w), h2p.dtype),
        compiler_params=_cparams("arbitrary"),
        name="moe_dispatch",
    )(dest_flat, meta_flat, h2p)


def _pack_rows(x):
    bits = lax.bitcast_convert_type(x, jnp.uint32)
    half = x.shape[1] // 2
    return (bits[:, half:] & jnp.uint32(0xFFFF0000)) | (bits[:, :half] >> 16)


def _unpack_halves(words):
    lo = lax.bitcast_convert_type(words << 16, F32)
    hi = lax.bitcast_convert_type(words & jnp.uint32(0xFFFF0000), F32)
    return lo, hi


def _unpack_rows(words):
    return jnp.concatenate(_unpack_halves(words), axis=-1).astype(BF16)


def _expert_kernel(meta_ref, xs_ref, wg_ref, wu_ref, wd_ref, yb_ref,
                   xbuf, ybuf, zbuf, wg_f32, wu_f32, wd_f32, wslot_ref,
                   sem_in, sem_out, sem_z, sem_w, *, n_blocks):
    c = pl.program_id(0)
    n_chunks = _meta(meta_ref, M_NCHUNK)
    slot = c % 2

    def weight_copies(k, s):
        e = _meta(meta_ref, M_OWNER, k)
        return [pltpu.make_async_copy(src.at[e], dst.at[s], sem_w.at[s])
                for src, dst in ((wg_ref, wg_f32), (wu_ref, wu_f32), (wd_ref, wd_f32))]

    def in_copy(k, s, nb):
        rows = nb * MOE_BLOCK
        src = xs_ref.at[pl.ds(_meta(meta_ref, M_FIRST, k) * MOE_BLOCK, rows), :]
        return pltpu.make_async_copy(src, xbuf.at[s, pl.ds(0, rows), :], sem_in.at[s])

    def out_copy(k, s, nb):
        rows = nb * MOE_BLOCK
        dst = yb_ref.at[pl.ds(_meta(meta_ref, M_FIRST, k) * MOE_BLOCK, rows), :]
        return pltpu.make_async_copy(ybuf.at[s, pl.ds(0, rows), :], dst, sem_out.at[s])

    def by_size(k, fn):
        for nb in range(1, CHUNK_BLOCKS + 1):
            pl.when(_meta(meta_ref, M_SIZE, k) == nb)(functools.partial(fn, nb))

    @pl.when(c == 0)
    def _():
        by_size(0, lambda nb: in_copy(0, 0, nb).start())

    @pl.when(c + 1 < n_chunks)
    def _():
        by_size(c + 1, lambda nb: in_copy(c + 1, 1 - slot, nb).start())

    @pl.when(c < n_chunks)
    def _():
        prev = jnp.maximum(c - 1, 0)
        new_expert = jnp.logical_or(c == 0, _meta(meta_ref, M_OWNER, c) != _meta(meta_ref, M_OWNER, prev))

        @pl.when(c == 0)
        def _():
            wslot_ref[0] = 0
            for cp in weight_copies(0, 0):
                cp.start()

        @pl.when(new_expert)
        def _():
            ws = jnp.where(c == 0, 0, 1 - wslot_ref[0])
            wslot_ref[0] = ws
            nxt = c + _meta(meta_ref, M_OWNER_CHUNKS, c)

            @pl.when(nxt < n_chunks)
            def _():
                for cp in weight_copies(nxt, 1 - ws):
                    cp.start()
            for cp in weight_copies(c, ws):
                cp.wait()

        @pl.when(c >= 2)
        def _():
            by_size(c - 2, lambda nb: out_copy(c - 2, slot, nb).wait())

        def compute(nb):
            rows = nb * MOE_BLOCK
            ws = wslot_ref[0]
            in_copy(c, slot, nb).wait()
            h = _unpack_rows(xbuf[slot, 0:rows, :])
            gate = jnp.dot(h, wg_f32[ws].astype(BF16), preferred_element_type=F32)
            up = jnp.dot(h, wu_f32[ws].astype(BF16), preferred_element_type=F32)
            act = (jax.nn.silu(gate) * up).astype(BF16)
            y = jnp.dot(act, wd_f32[ws].astype(BF16), preferred_element_type=F32)
            ybuf[slot, 0:rows, :] = _pack_rows(y.astype(BF16).astype(F32))
            out_copy(c, slot, nb).start()
        by_size(c, compute)

    @pl.when(c == pl.num_programs(0) - 1)
    def _():
        _fill_blocks(meta_ref, zbuf, yb_ref, sem_z, n_blocks, expert_tails=False)
        for back in (2, 1):
            @pl.when(n_chunks >= back)
            def _(back=back):
                k = n_chunks - back
                by_size(k, lambda nb: out_copy(k, k % 2, nb).wait())


def _experts(meta_flat, xs, w_gate, w_up, w_down, n_blocks, n_chunks_max):
    d, de = w_gate.shape[1], w_gate.shape[2]
    rows = CHUNK_BLOCKS * MOE_BLOCK

    hbm = pl.BlockSpec(memory_space=pl.ANY)
    grid_spec = pltpu.PrefetchScalarGridSpec(
        num_scalar_prefetch=1,
        grid=(n_chunks_max,),
        in_specs=[hbm, hbm, hbm, hbm],
        out_specs=hbm,
        scratch_shapes=[pltpu.VMEM((2, rows, xs.shape[1]), xs.dtype),
                        pltpu.VMEM((2, rows, d // 2), jnp.uint32),
                        pltpu.VMEM((MOE_BLOCK, d // 2), jnp.uint32),
                        pltpu.VMEM((2, d, de), F32), pltpu.VMEM((2, d, de), F32), pltpu.VMEM((2, de, d), F32),
                        pltpu.SMEM((1,), jnp.int32),
                        pltpu.SemaphoreType.DMA((2,)), pltpu.SemaphoreType.DMA((2,)),
                        pltpu.SemaphoreType.DMA(()), pltpu.SemaphoreType.DMA((2,))],
    )
    return pl.pallas_call(
        functools.partial(_expert_kernel, n_blocks=n_blocks),
        grid_spec=grid_spec,
        out_shape=jax.ShapeDtypeStruct((n_blocks * MOE_BLOCK, d // 2), jnp.uint32),
        compiler_params=_cparams("arbitrary"),
        name="moe_experts",
    )(meta_flat, xs, w_gate, w_up, w_down)


def _combine_kernel(dest_ref, yb_ref, x1_ref, w_ref, out_ref, gbuf, sem, *, tokens):
    i = pl.program_id(0)
    tm = x1_ref.shape[0]
    slot = i % 2

    def gather(tile, s):
        base = tile * tm

        def issue(g, carry):
            r8 = pl.multiple_of(g * SUBLANES, SUBLANES)
            for sub in range(SUBLANES):
                for k in range(2):
                    d = dest_ref[k * tokens + base + r8 + sub]
                    pltpu.make_async_copy(yb_ref.at[pl.ds(d, 1), :], gbuf.at[s, k, pl.ds(r8 + sub, 1), :],
                                          sem.at[s]).start()
            return carry
        lax.fori_loop(0, tm // SUBLANES, issue, 0)

    @pl.when(i == 0)
    def _():
        gather(0, 0)

    @pl.when(i + 1 < pl.num_programs(0))
    def _():
        gather(i + 1, 1 - slot)

    for k in range(2):
        pltpu.make_async_copy(yb_ref.at[pl.ds(0, tm), :], gbuf.at[slot, k], sem.at[slot]).wait()
    w = w_ref[...]
    half = x1_ref.shape[1] // 2
    lo0, hi0 = _unpack_halves(gbuf[slot, 0])
    lo1, hi1 = _unpack_halves(gbuf[slot, 1])
    out_ref[:, :half] = x1_ref[:, :half] + (lo0 * w[:, 0:1] + lo1 * w[:, 1:2])
    out_ref[:, half:] = x1_ref[:, half:] + (hi0 * w[:, 0:1] + hi1 * w[:, 1:2])


def _combine(dest_flat, yb, x1, w_tok, tm):
    t, d = x1.shape
    grid_spec = pltpu.PrefetchScalarGridSpec(
        num_scalar_prefetch=1,
        grid=(t // tm,),
        in_specs=[pl.BlockSpec(memory_space=pl.ANY),
                  pl.BlockSpec((tm, d), lambda i, dr: (i, 0)),
                  pl.BlockSpec((tm, 2), lambda i, dr: (i, 0))],
        out_specs=pl.BlockSpec((tm, d), lambda i, dr: (i, 0)),
        scratch_shapes=[pltpu.VMEM((2, 2, tm, d // 2), jnp.uint32), pltpu.SemaphoreType.DMA((2,))],
    )
    return pl.pallas_call(
        functools.partial(_combine_kernel, tokens=t),
        grid_spec=grid_spec,
        out_shape=jax.ShapeDtypeStruct((t, d), F32),
        compiler_params=_cparams("arbitrary"),
        name="moe_combine",
    )(dest_flat, yb, x1, w_tok)


def _row(v):
    return v.astype(F32).reshape(1, -1)


def _layer(x2, mem2, batch, seq, mem_len, p):
    t, d = x2.shape

    bias = _bias_table(p["rel_bias"])
    col = lambda v: jnp.broadcast_to(v.astype(F32)[:, None], (v.shape[0], BLOCK))
    qgain = col(p["q_norm"]) * (1.0 / math.sqrt(ATTN_HEAD_DIM))
    kgain = jnp.tile(_row(p["k_norm"]), (1, ATTN_KV_HEADS))
    u_sb, qx, ya = _inproj_swa(x2, _row(p["norm_mix"]), p["w_in"].astype(BF16), bias, p["attn_sinks"].astype(F32),
                               qgain, kgain, col(p["out_norm_attn"]), batch, seq, min(512, seq))

    km, vm = _memkv(mem2, _row(p["mem_norm"]), p["w_mem_kv"].astype(BF16), _row(p["xk_norm"]),
                    min(256, mem2.shape[0]))
    xq_gain = _row(p["xq_norm"]) * (1.0 / math.sqrt(XATTN_HEAD_DIM))
    yx = _xattn(qx, km, vm, xq_gain, _row(p["out_norm_xattn"]), batch, seq, mem_len, min(512, seq))

    a_re, a_im, bbr, bbi = _ssm_prep(p["ssm_lambda_re"], p["ssm_lambda_im"], p["ssm_log_dt"],
                                     p["ssm_b_re"], p["ssm_b_im"])
    bmat = jnp.concatenate([_block_diag_tiles(bbr), _block_diag_tiles(bbi)], axis=-1).astype(BF16)
    c_re_t = jnp.transpose(p["ssm_c_re"].astype(F32), (0, 2, 1))
    c_im_t = jnp.transpose(p["ssm_c_im"].astype(F32), (0, 2, 1))
    cmat = jnp.concatenate([_block_diag_tiles(c_re_t), _block_diag_tiles(-c_im_t)], axis=1).astype(BF16)
    steps = min(64, seq)
    ys_sb = _ssm(u_sb, bmat, cmat,
                 a_re.reshape(1, SSM_STATES), a_im.reshape(1, SSM_STATES), _row(p["ssm_d"]),
                 p["ssm_w_glu"].astype(BF16), _row(p["out_norm_ssm"]), batch, seq, steps)

    wr = jnp.concatenate([p["w_router_group"], p["w_router_expert"]], axis=1).astype(F32)
    wr = jnp.pad(wr, ((0, 0), (0, LANES - wr.shape[1]))).astype(BF16)
    tm_out = min(512, seq)
    x1, logits_t, h2p = _outproj(ya, ys_sb, yx, x2, p["w_o"].astype(BF16), _row(p["norm_ffn"]), wr,
                                 batch, seq, tm_out)

    dest, w_k, meta = _route(logits_t, min(2048, t))
    n_blocks = (2 * t) // MOE_BLOCK + N_EXPERTS
    n_chunks_max = (n_blocks + (CHUNK_BLOCKS - 1) * N_EXPERTS) // CHUNK_BLOCKS
    dest_flat = dest.reshape(2 * t)
    meta_flat = meta.reshape(META_ROWS * LANES)
    xs = _dispatch(dest_flat, meta_flat, h2p, n_blocks, min(2048, t))
    yb = _experts(meta_flat, xs, p["w_gate"], p["w_up"], p["w_down"], n_blocks, n_chunks_max)
    return _combine(dest_flat, yb, x1, w_k.T, min(512, t))


def kernel(x, mem, norm_mix, w_in, q_norm, k_norm, attn_sinks, rel_bias, ssm_lambda_re, ssm_lambda_im, ssm_log_dt, ssm_b_re, ssm_b_im, ssm_c_re, ssm_c_im, ssm_d, ssm_w_glu, mem_norm, w_mem_kv, xq_norm, xk_norm, out_norm_attn, out_norm_ssm, out_norm_xattn, w_o, norm_ffn, w_router_group, w_router_expert, w_gate, w_up, w_down):
    batch, seq, d = x.shape
    mem_len = mem.shape[1]
    per_layer = dict(norm_mix=norm_mix, w_in=w_in, q_norm=q_norm, k_norm=k_norm, attn_sinks=attn_sinks,
                     ssm_lambda_re=ssm_lambda_re, ssm_lambda_im=ssm_lambda_im, ssm_log_dt=ssm_log_dt,
                     ssm_b_re=ssm_b_re, ssm_b_im=ssm_b_im, ssm_c_re=ssm_c_re, ssm_c_im=ssm_c_im,
                     ssm_d=ssm_d, ssm_w_glu=ssm_w_glu, mem_norm=mem_norm, w_mem_kv=w_mem_kv,
                     xq_norm=xq_norm, xk_norm=xk_norm, out_norm_attn=out_norm_attn,
                     out_norm_ssm=out_norm_ssm, out_norm_xattn=out_norm_xattn, w_o=w_o, norm_ffn=norm_ffn,
                     w_router_group=w_router_group, w_router_expert=w_router_expert,
                     w_gate=w_gate, w_up=w_up, w_down=w_down)
    x2 = x.astype(F32).reshape(batch * seq, d)
    mem2 = mem.astype(F32).reshape(batch * mem_len, d)
    for l in range(norm_mix.shape[0]):
        p = {k: v[l] for k, v in per_layer.items()}
        p["rel_bias"] = rel_bias
        x2 = _layer(x2, mem2, batch, seq, mem_len, p)
    return x2.reshape(batch, seq, d).astype(x.dtype)
```

```python
import functools
import math

import numpy as np
import jax
import jax.numpy as jnp
from jax import lax
from jax.experimental import pallas as pl
from jax.experimental.pallas import tpu as pltpu

F32 = jnp.float32
BF16 = jnp.bfloat16
EPS = 1e-6

ATTN_HEADS = 16
ATTN_KV_HEADS = 2
ATTN_HEAD_DIM = 64
ATTN_WIDTH = ATTN_HEADS * ATTN_HEAD_DIM
WINDOW = 128
BLOCK = 128
REL_BUCKETS = 32
REL_MAX_DIST = 128
SSM_GROUP_CH = 16
SSM_GROUPS = 32
SSM_STATE = 64
SSM_WIDTH = SSM_GROUPS * SSM_GROUP_CH
XATTN_HEADS = 4
XATTN_HEAD_DIM = 128
XATTN_WIDTH = XATTN_HEADS * XATTN_HEAD_DIM
N_EXPERT_GROUPS = 8
EXPERTS_PER_GROUP = 8
N_EXPERTS = N_EXPERT_GROUPS * EXPERTS_PER_GROUP
D_EXPERT = 512
MOE_BLOCK = 128

LANES = 128
SUBLANES = 8
SSM_GROUPS_PER_TILE = LANES // SSM_GROUP_CH
SSM_TILES = SSM_WIDTH // LANES
SSM_TILE_STATE = SSM_GROUPS_PER_TILE * SSM_STATE
SSM_STATES = SSM_GROUPS * SSM_STATE
SSM_PIECES = 4
CHUNK_BLOCKS = 4
COMBINE_SLOTS = 3
COMBINE_PIECE = 32
META_ROWS = 8
M_OWNER, M_FIRST, M_SIZE, M_NCHUNK, M_LAST, M_NBLK, M_NUSED, M_OWNER_CHUNKS = range(8)
VMEM_LIMIT = 56 * 1024 * 1024

_NT = (((1,), (1,)), ((), ()))


def _cparams(*sem):
    return pltpu.CompilerParams(dimension_semantics=sem, vmem_limit_bytes=VMEM_LIMIT)


def _rms(x, gain):
    ms = jnp.mean(x * x, axis=-1, keepdims=True)
    return x * lax.rsqrt(ms + EPS) * gain


def _t5_bucket_table():
    qi = np.arange(BLOCK, dtype=np.int32)[:, None]
    ki = np.arange(2 * BLOCK, dtype=np.int32)[None, :]
    delta = BLOCK + qi - ki
    n = np.maximum(delta, 0)
    max_exact = REL_BUCKETS // 2
    nf = np.maximum(n, 1).astype(np.float32)
    large = max_exact + (np.log(nf / np.float32(max_exact)) / np.float32(math.log(REL_MAX_DIST / max_exact))
                         * np.float32(REL_BUCKETS - max_exact)).astype(np.int32)
    large = np.minimum(large, REL_BUCKETS - 1)
    return np.where(n < max_exact, n, large).astype(np.int32)


def _upper(cols):
    k = lax.broadcasted_iota(jnp.int32, (BLOCK, cols), 0)
    q = lax.broadcasted_iota(jnp.int32, (BLOCK, cols), 1) % BLOCK
    return k > q


def _bias_kernel(rb_ref, bucket_ref, out_ref):
    pair = pl.program_id(0)
    bucket = bucket_ref[...]
    upper = _upper(BLOCK)
    for half in range(2):
        acc = jnp.zeros(bucket.shape, F32)
        for b in range(REL_BUCKETS):
            acc = jnp.where(bucket == b, rb_ref[b, 2 * pair + half], acc)
        cols = slice(half * BLOCK, (half + 1) * BLOCK)
        out_ref[0, 0, :, cols] = jnp.where(upper, acc[:BLOCK], acc[BLOCK:])
        out_ref[1, 0, :, cols] = jnp.where(upper, jnp.float32(-1e30), acc[BLOCK:])


def _bias_table(rel_bias):
    bucket = jnp.asarray(_t5_bucket_table().T)
    pairs = ATTN_HEADS // 2
    return pl.pallas_call(
        _bias_kernel,
        grid=(pairs,),
        in_specs=[pl.BlockSpec(memory_space=pltpu.SMEM),
                  pl.BlockSpec((2 * BLOCK, BLOCK), lambda h: (0, 0))],
        out_specs=pl.BlockSpec((2, 1, BLOCK, 2 * BLOCK), lambda h: (0, h, 0, 0)),
        out_shape=jax.ShapeDtypeStruct((2, pairs, BLOCK, 2 * BLOCK), F32),
        compiler_params=_cparams("arbitrary"),
        name="t5_bias_table",
    )(rel_bias.astype(F32), bucket)


def _ssm_prep_kernel(lr_ref, li_ref, ldt_ref, br_ref, bi_ref, are_ref, aim_ref, bbr_ref, bbi_ref):
    lr = lr_ref[...]
    li = li_ref[...]
    dt = jnp.exp(ldt_ref[...])
    mag = jnp.exp(lr * dt)
    a_re = mag * jnp.cos(li * dt)
    a_im = mag * jnp.sin(li * dt)
    den = lr * lr + li * li
    nr = a_re - 1.0
    ni = a_im
    coef_re = (nr * lr + ni * li) / den
    coef_im = (ni * lr - nr * li) / den
    are_ref[...] = a_re
    aim_ref[...] = a_im
    br = br_ref[...]
    bi = bi_ref[...]
    bbr_ref[...] = coef_re * br - coef_im * bi
    bbi_ref[...] = coef_re * bi + coef_im * br


def _ssm_prep(lam_re, lam_im, log_dt, b_re, b_im):
    g, n, c = b_re.shape
    vec = jax.ShapeDtypeStruct((g, 1, n), F32)
    mat = jax.ShapeDtypeStruct((g, c, n), F32)
    return pl.pallas_call(
        _ssm_prep_kernel,
        out_shape=(vec, vec, mat, mat),
        name="ssm_discretise",
    )(lam_re.astype(F32).reshape(g, 1, n), lam_im.astype(F32).reshape(g, 1, n),
      log_dt.astype(F32).reshape(g, 1, 1),
      jnp.transpose(b_re.astype(F32), (0, 2, 1)), jnp.transpose(b_im.astype(F32), (0, 2, 1)))


def _block_diag_tiles(m):
    g, r, c = m.shape
    t = g // SSM_GROUPS_PER_TILE
    eye = jnp.eye(SSM_GROUPS_PER_TILE, dtype=m.dtype)
    m4 = m.reshape(t, SSM_GROUPS_PER_TILE, r, c)
    out = m4[:, :, :, None, :] * eye[None, :, None, :, None]
    return out.reshape(t, SSM_GROUPS_PER_TILE * r, SSM_GROUPS_PER_TILE * c)


def _k_norm(k, kgain):
    dh = ATTN_HEAD_DIM
    lo = lax.broadcasted_iota(jnp.int32, (1, ATTN_KV_HEADS * dh), 1) < dh
    sq = k * k
    s_lo = jnp.sum(jnp.where(lo, sq, 0.0), axis=-1, keepdims=True)
    s_hi = jnp.sum(sq, axis=-1, keepdims=True) - s_lo
    return k * jnp.where(lo, lax.rsqrt(s_lo / dh + EPS), lax.rsqrt(s_hi / dh + EPS)) * kgain


def _swa_phases(q_of, kn, v_t, bias_of, sinks_ref, qgain, ogain, emit):
    dh = ATTN_HEAD_DIM
    upper = _upper(2 * BLOCK)
    first_head = lax.broadcasted_iota(jnp.int32, (1, 2 * BLOCK), 1) < BLOCK
    zeros = jnp.zeros((dh, BLOCK), F32)
    heads_per_kv = ATTN_HEADS // ATTN_KV_HEADS
    pairs = range(ATTN_HEADS // 2)
    kv_of = lambda pair: (2 * pair) // heads_per_kv
    state = {}

    def qk_all():
        logits = []
        for pair in pairs:
            cols = []
            for half in range(2):
                qh = q_of(2 * pair + half).astype(F32)
                ms = jnp.mean(qh * qh, axis=0, keepdims=True)
                qn = qh * lax.rsqrt(ms + EPS) * qgain
                cols.append(jnp.concatenate([qn, zeros] if kv_of(pair) == 0 else [zeros, qn], axis=0))
            rhs = jnp.concatenate(cols, axis=1).astype(BF16)
            logits.append(jnp.dot(kn, rhs, preferred_element_type=F32))
        state["logits"] = logits

    def softmax_all():
        probs = []
        for pair in pairs:
            both = state["logits"][pair]
            l = jnp.where(upper, both[:BLOCK], both[BLOCK:]) + bias_of(pair)
            sink = jnp.where(first_head, sinks_ref[2 * pair], sinks_ref[2 * pair + 1])
            m = jnp.maximum(jnp.max(l, axis=0, keepdims=True), sink)
            p = jnp.exp(l - m)
            den = jnp.sum(p, axis=0, keepdims=True) + jnp.exp(sink - m)
            pz = jnp.concatenate([jnp.where(upper, p, 0.0), jnp.where(upper, 0.0, p)], axis=0).astype(BF16)
            probs.append((pz, den))
        state["probs"] = probs

    def pv_all():
        outs = []
        for pair in pairs:
            g = kv_of(pair)
            pz, den = state["probs"][pair]
            o = jnp.dot(v_t[g * dh:(g + 1) * dh, :], pz, preferred_element_type=F32) / den
            outs += [o[:, :BLOCK], o[:, BLOCK:]]
        y_t = jnp.concatenate(outs, axis=0)
        ms = jnp.mean(y_t * y_t, axis=0, keepdims=True)
        emit((y_t * lax.rsqrt(ms + EPS) * ogain).T)

    return qk_all, softmax_all, pv_all


def _inproj_swa_kernel(sinks_ref, x_ref, g_ref, w_ref, bias_ref, qg_ref, kg_ref, og_ref,
                       u_ref, qx_ref, ya_ref, q_s, k_s, v_s, kprev_s, vprev_s, *, tiles_per_seq):
    i = pl.program_id(0)
    slot = i % 2
    old = 1 - slot
    tm = x_ref.shape[0]
    blocks = tm // BLOCK
    kvw = ATTN_KV_HEADS * ATTN_HEAD_DIM

    @pl.when(i == 0)
    def _():
        q_s[1] = jnp.zeros(q_s.shape[1:], q_s.dtype)
        k_s[1] = jnp.zeros(k_s.shape[1:], k_s.dtype)
        v_s[1] = jnp.zeros(v_s.shape[1:], v_s.dtype)
        kprev_s[...] = jnp.zeros_like(kprev_s)
        vprev_s[...] = jnp.zeros_like(vprev_s)

    h = _rms(x_ref[...], g_ref[...]).astype(BF16)
    proj = lambda c0, c1: jnp.dot(h, w_ref[:, c0:c1], preferred_element_type=F32)
    piece = 2 * LANES

    def q_piece(n):
        def run():
            q_s[slot, n * piece:(n + 1) * piece, :] = proj(n * piece, (n + 1) * piece).T.astype(q_s.dtype)
        return run

    def kv_piece():
        kv = proj(ATTN_WIDTH, ATTN_WIDTH + 2 * kvw)
        k_s[slot] = _k_norm(kv[:, :kvw], kg_ref[...]).astype(k_s.dtype)
        v_s[slot] = kv[:, kvw:].T.astype(v_s.dtype)

    def out_piece(ref, c0, n):
        def run():
            ref[:, n * piece:(n + 1) * piece] = proj(c0 + n * piece, c0 + (n + 1) * piece).astype(ref.dtype)
        return run

    c_u = ATTN_WIDTH + 2 * kvw
    c_qx = c_u + SSM_WIDTH
    projection = ([q_piece(n) for n in range(ATTN_WIDTH // piece)] + [kv_piece]
                  + [out_piece(u_ref, c_u, n) for n in range(SSM_WIDTH // piece)]
                  + [out_piece(qx_ref, c_qx, n) for n in range(XATTN_WIDTH // piece)])

    first_of_seq = (i + tiles_per_seq - 1) % tiles_per_seq == 0
    attention = []
    for blk in range(blocks):
        own = slice(blk * BLOCK, (blk + 1) * BLOCK)
        before = slice((blk - 1) * BLOCK, blk * BLOCK)
        k_before = kprev_s[...] if blk == 0 else k_s[old, before, :]
        v_before = vprev_s[...] if blk == 0 else v_s[old, :, before]
        table = jnp.where(first_of_seq, 1, 0) if blk == 0 else 0

        def emit(y, own=own):
            ya_ref[own, :] = y.astype(ya_ref.dtype)

        attention += _swa_phases(
            q_of=lambda hd, own=own: q_s[old, hd * ATTN_HEAD_DIM:(hd + 1) * ATTN_HEAD_DIM, own],
            kn=jnp.concatenate([k_before, k_s[old, own, :]], axis=0),
            v_t=jnp.concatenate([v_before, v_s[old, :, own]], axis=1),
            bias_of=lambda pair, table=table: bias_ref[table, pair],
            sinks_ref=sinks_ref, qgain=qg_ref[...], ogain=og_ref[...], emit=emit)

    longer, shorter = (attention, projection) if len(attention) >= len(projection) else (projection, attention)
    done = 0
    for n, fn in enumerate(longer):
        fn()
        due = (n + 1) * len(shorter) // len(longer)
        for extra in shorter[done:due]:
            extra()
        done = due

    last = slice((blocks - 1) * BLOCK, blocks * BLOCK)
    kprev_s[...] = k_s[old, last, :]
    vprev_s[...] = v_s[old, :, last]


def _inproj_swa(x2, gain, w_in, bias, sinks, qgain, kgain, ogain, batch, seq, tm):
    t, d = x2.shape
    n_tiles = t // tm
    tiles_per_seq = seq // tm
    kvw = ATTN_KV_HEADS * ATTN_HEAD_DIM
    proj_tile = lambda i: jnp.minimum(i, n_tiles - 1)
    attn_tile = lambda i: jnp.maximum(i - 1, 0)
    const = lambda shape: pl.BlockSpec(shape, lambda i: (0,) * len(shape), pipeline_mode=pl.Buffered(1))
    return pl.pallas_call(
        functools.partial(_inproj_swa_kernel, tiles_per_seq=tiles_per_seq),
        grid=(n_tiles + 1,),
        in_specs=[pl.BlockSpec(memory_space=pltpu.SMEM),
                  pl.BlockSpec((tm, d), lambda i: (proj_tile(i), 0)),
                  const((1, d)), const(w_in.shape), const(bias.shape),
                  const((ATTN_HEAD_DIM, BLOCK)), const((1, kvw)), const((ATTN_WIDTH, BLOCK))],
        out_specs=[
                   pl.BlockSpec((tm, SSM_WIDTH),
                                lambda i: (proj_tile(i) % tiles_per_seq, proj_tile(i) // tiles_per_seq)),
                   pl.BlockSpec((tm, XATTN_WIDTH), lambda i: (proj_tile(i), 0)),
                   pl.BlockSpec((tm, ATTN_WIDTH), lambda i: (attn_tile(i), 0))],
        out_shape=[jax.ShapeDtypeStruct((seq, batch * SSM_WIDTH), BF16),
                   jax.ShapeDtypeStruct((t, XATTN_WIDTH), BF16),
                   jax.ShapeDtypeStruct((t, ATTN_WIDTH), BF16)],
        scratch_shapes=[pltpu.VMEM((2, ATTN_WIDTH, tm), BF16),
                        pltpu.VMEM((2, tm, kvw), BF16),
                        pltpu.VMEM((2, kvw, tm), BF16),
                        pltpu.VMEM((BLOCK, kvw), BF16), pltpu.VMEM((kvw, BLOCK), BF16)],
        compiler_params=_cparams("arbitrary"),
        name="in_proj_swa",
    )(sinks, x2, gain, w_in, bias, qgain, kgain, ogain)


def _memkv_kernel(m_ref, g_ref, w_ref, kg_ref, k_ref, v_ref):
    h = _rms(m_ref[...], g_ref[...]).astype(BF16)
    km = jnp.dot(h, w_ref[:, :XATTN_WIDTH], preferred_element_type=F32)
    for hd in range(XATTN_HEADS):
        sl = slice(hd * XATTN_HEAD_DIM, (hd + 1) * XATTN_HEAD_DIM)
        k_ref[:, sl] = _rms(km[:, sl], kg_ref[...]).astype(k_ref.dtype)
    v_ref[...] = jnp.dot(h, w_ref[:, XATTN_WIDTH:], preferred_element_type=F32).astype(v_ref.dtype)


def _memkv(mem2, gain, w_kv, kgain, tm):
    r, d = mem2.shape
    row = lambda w: pl.BlockSpec((tm, w), lambda i: (i, 0))
    const = lambda shape: pl.BlockSpec(shape, lambda i: (0, 0))
    return pl.pallas_call(
        _memkv_kernel,
        grid=(r // tm,),
        in_specs=[row(d), const((1, d)), const((d, 2 * XATTN_WIDTH)), const((1, XATTN_HEAD_DIM))],
        out_specs=[row(XATTN_WIDTH), row(XATTN_WIDTH)],
        out_shape=[jax.ShapeDtypeStruct((r, XATTN_WIDTH), BF16)] * 2,
        compiler_params=_cparams("arbitrary"),
        name="mem_kv_proj",
    )(mem2, gain, w_kv, kgain)


def _xattn_kernel(q_ref, k_ref, v_ref, qg_ref, og_ref, out_ref):
    outs = []
    for hd in range(XATTN_HEADS):
        sl = slice(hd * XATTN_HEAD_DIM, (hd + 1) * XATTN_HEAD_DIM)
        qn = _rms(q_ref[:, sl].astype(F32), qg_ref[...]).astype(BF16)
        l = lax.dot_general(qn, k_ref[:, sl], _NT, preferred_element_type=F32)
        m = jnp.max(l, axis=-1, keepdims=True)
        p = jnp.exp(l - m)
        den = jnp.sum(p, axis=-1, keepdims=True)
        outs.append(jnp.dot(p.astype(BF16), v_ref[:, sl], preferred_element_type=F32) / den)
    y = jnp.concatenate(outs, axis=-1)
    out_ref[...] = _rms(y, og_ref[...]).astype(out_ref.dtype)


def _xattn(qx, km, vm, qgain, ogain, batch, seq, mem_len, tq):
    nq = seq // tq
    const2 = lambda b, i: (0, 0)
    return pl.pallas_call(
        _xattn_kernel,
        grid=(batch, nq),
        in_specs=[pl.BlockSpec((tq, XATTN_WIDTH), lambda b, i: (b * nq + i, 0)),
                  pl.BlockSpec((mem_len, XATTN_WIDTH), lambda b, i: (b, 0)),
                  pl.BlockSpec((mem_len, XATTN_WIDTH), lambda b, i: (b, 0)),
                  pl.BlockSpec((1, XATTN_HEAD_DIM), const2),
                  pl.BlockSpec((1, XATTN_WIDTH), const2)],
        out_specs=pl.BlockSpec((tq, XATTN_WIDTH), lambda b, i: (b * nq + i, 0)),
        out_shape=jax.ShapeDtypeStruct((batch * seq, XATTN_WIDTH), BF16),
        compiler_params=_cparams("arbitrary", "arbitrary"),
        name="mem_xattn",
    )(qx, km, vm, qgain, ogain)


def _ssm_kernel(u_ref, bmat_ref, cmat_ref, are_ref, aim_ref, d_ref, wglu_ref, og_ref, out_ref,
                tb_ref, xr_ref, xi_ref, sr_ref, si_ref, *, batch, steps):
    @pl.when(pl.program_id(0) == 0)
    def _():
        sr_ref[...] = jnp.zeros_like(sr_ref)
        si_ref[...] = jnp.zeros_like(si_ref)

    for b in range(batch):
        for j in range(SSM_TILES):
            c0 = b * SSM_WIDTH + j * LANES
            tb_ref[j, pl.ds(b, steps, stride=batch), :] = u_ref[:, c0:c0 + LANES].astype(F32)
    uf = jnp.concatenate([tb_ref[j] for j in range(SSM_TILES)], axis=-1)
    u = uf.astype(BF16)
    total = steps * batch
    piece = total // SSM_PIECES
    tiles_per_half = SSM_TILES // 2
    half_states = SSM_STATES // 2

    def bu_piece(j, k):
        rows = slice(k * piece, (k + 1) * piece)
        bu = jnp.dot(u[rows, j * LANES:(j + 1) * LANES], bmat_ref[j], preferred_element_type=F32)
        xr_ref[rows, j * SSM_TILE_STATE:(j + 1) * SSM_TILE_STATE] = bu[:, :SSM_TILE_STATE]
        xi_ref[rows, j * SSM_TILE_STATE:(j + 1) * SSM_TILE_STATE] = bu[:, SSM_TILE_STATE:]

    def c_piece(j, k):
        rows = slice(k * piece, (k + 1) * piece)
        sl = slice(j * SSM_TILE_STATE, (j + 1) * SSM_TILE_STATE)
        xcat = jnp.concatenate([xr_ref[rows, sl], xi_ref[rows, sl]], axis=-1).astype(BF16)
        return jnp.dot(xcat, cmat_ref[j], preferred_element_type=F32)

    def scan_half(hf, between):
        cs = slice(hf * half_states, (hf + 1) * half_states)
        ar = jnp.broadcast_to(are_ref[:, cs], (batch, half_states))
        ai = jnp.broadcast_to(aim_ref[:, cs], (batch, half_states))
        s_r, s_i = sr_ref[:, cs], si_ref[:, cs]
        every = steps // len(between)
        for t in range(steps):
            rows = slice(t * batch, (t + 1) * batch)
            s_r, s_i = (ar * s_r - ai * s_i + xr_ref[rows, cs], ar * s_i + ai * s_r + xi_ref[rows, cs])
            xr_ref[rows, cs] = s_r
            xi_ref[rows, cs] = s_i
            if t % every == every - 1:
                between[t // every]()
        sr_ref[:, cs] = s_r
        si_ref[:, cs] = s_i

    first = [(j, k) for j in range(tiles_per_half) for k in range(SSM_PIECES)]
    second = [(j, k) for j in range(tiles_per_half, SSM_TILES) for k in range(SSM_PIECES)]
    for j, k in first:
        bu_piece(j, k)
    scan_half(0, [functools.partial(bu_piece, j, k) for j, k in second])
    y_pieces = {}
    scan_half(1, [functools.partial(lambda j, k: y_pieces.__setitem__((j, k), c_piece(j, k)), j, k)
                  for j, k in first])
    for j, k in second:
        y_pieces[(j, k)] = c_piece(j, k)
    ys = [jnp.concatenate([y_pieces[(j, k)] for k in range(SSM_PIECES)], axis=0) for j in range(SSM_TILES)]
    y = jnp.concatenate(ys, axis=-1) + d_ref[...] * uf
    y = jax.nn.gelu(y)
    y = y * jax.nn.sigmoid(jnp.dot(y.astype(BF16), wglu_ref[...], preferred_element_type=F32))
    y = _rms(y, og_ref[...])
    for j in range(SSM_TILES):
        tb_ref[j] = y[:, j * LANES:(j + 1) * LANES]
    for b in range(batch):
        for j in range(SSM_TILES):
            c0 = b * SSM_WIDTH + j * LANES
            out_ref[:, c0:c0 + LANES] = tb_ref[j, pl.ds(b, steps, stride=batch), :].astype(out_ref.dtype)


def _ssm(u_sb, bmat, cmat, a_re, a_im, d_skip, w_glu, ogain, batch, seq, steps):
    rows = steps * batch
    const2 = lambda c: (0, 0)
    const3 = lambda c: (0, 0, 0)
    return pl.pallas_call(
        functools.partial(_ssm_kernel, batch=batch, steps=steps),
        grid=(seq // steps,),
        in_specs=[pl.BlockSpec((steps, batch * SSM_WIDTH), lambda c: (c, 0)),
                  pl.BlockSpec(bmat.shape, const3), pl.BlockSpec(cmat.shape, const3),
                  pl.BlockSpec((1, SSM_STATES), const2), pl.BlockSpec((1, SSM_STATES), const2),
                  pl.BlockSpec((1, SSM_WIDTH), const2),
                  pl.BlockSpec((SSM_WIDTH, SSM_WIDTH), const2),
                  pl.BlockSpec((1, SSM_WIDTH), const2)],
        out_specs=pl.BlockSpec((steps, batch * SSM_WIDTH), lambda c: (c, 0)),
        out_shape=jax.ShapeDtypeStruct((seq, batch * SSM_WIDTH), BF16),
        scratch_shapes=[pltpu.VMEM((SSM_TILES, rows, LANES), F32),
                        pltpu.VMEM((rows, SSM_STATES), F32), pltpu.VMEM((rows, SSM_STATES), F32),
                        pltpu.VMEM((batch, SSM_STATES), F32), pltpu.VMEM((batch, SSM_STATES), F32)],
        compiler_params=_cparams("arbitrary"),
        name="s5_layer",
    )(u_sb, bmat, cmat, a_re, a_im, d_skip, w_glu, ogain)


def _outproj_kernel(ya_ref, ys_ref, yx_ref, x_ref, wo_ref, g_ref, wr_ref, x1_ref, lt_ref, hp_ref, *, sub):
    for r0 in range(0, x_ref.shape[0], sub):
        rows = slice(r0, r0 + sub)
        mix = jnp.concatenate([ya_ref[rows, :], ys_ref[rows, :], yx_ref[rows, :]], axis=-1)
        x1 = x_ref[rows, :] + jnp.dot(mix, wo_ref[...], preferred_element_type=F32)
        x1_ref[rows, :] = x1
        h2 = _rms(x1, g_ref[...])
        hi = h2.astype(BF16)
        lt_ref[rows, :] = jnp.dot(hi, wr_ref[...], preferred_element_type=F32)
        hp_ref[rows, :] = _pack_rows(hi.astype(F32))


def _outproj(ya, ys_sb, yx, x2, w_o, gain, wr, batch, seq, tm):
    t, d = x2.shape
    nsb = seq // tm
    row = lambda w: pl.BlockSpec((tm, w), lambda i: (i, 0))
    const = lambda shape: pl.BlockSpec(shape, lambda i: (0, 0), pipeline_mode=pl.Buffered(1))
    return pl.pallas_call(
        functools.partial(_outproj_kernel, sub=min(256, tm)),
        grid=(t // tm,),
        in_specs=[row(ATTN_WIDTH),
                  pl.BlockSpec((tm, SSM_WIDTH), lambda i: (i % nsb, i // nsb)),
                  row(XATTN_WIDTH), row(d),
                  const(w_o.shape), const((1, d)), const(wr.shape)],
        out_specs=[row(d), row(LANES), row(d // 2)],
        out_shape=[jax.ShapeDtypeStruct((t, d), F32), jax.ShapeDtypeStruct((t, LANES), F32),
                   jax.ShapeDtypeStruct((t, d // 2), jnp.uint32)],
        compiler_params=_cparams("arbitrary"),
        name="out_proj_router",
    )(ya, ys_sb, yx, x2, w_o, gain, wr)


def _route_kernel(lt_ref, tri_ref, dest_ref, w_ref, meta_ref, cnt_ref, carry_ref, pstart_ref):
    phase = pl.program_id(0)
    c = pl.program_id(1)
    logits = lt_ref[...].T
    tc = logits.shape[1]
    ng, epg = N_EXPERT_GROUPS, EXPERTS_PER_GROUP
    row8 = lax.broadcasted_iota(jnp.int32, (ng, tc), 0)

    gl = logits[0:ng]
    gmax = jnp.max(gl, axis=0, keepdims=True)
    gidx = jnp.min(jnp.where(gl == gmax, row8, ng), axis=0, keepdims=True)
    gate = 1.0 / jnp.sum(jnp.exp(gl - gmax), axis=0, keepdims=True)
    sel = jnp.zeros((epg, tc), F32)
    for g in range(ng):
        sel = jnp.where(gidx == g, logits[ng + g * epg:ng + (g + 1) * epg], sel)
    v1 = jnp.max(sel, axis=0, keepdims=True)
    i1 = jnp.min(jnp.where(sel == v1, row8, epg), axis=0, keepdims=True)
    sel2 = jnp.where(row8 == i1, -jnp.inf, sel)
    v2 = jnp.max(sel2, axis=0, keepdims=True)
    i2 = jnp.min(jnp.where(sel2 == v2, row8, epg), axis=0, keepdims=True)
    e = jnp.exp(v2 - v1)
    w1 = gate * (1.0 / (1.0 + e))
    w2 = gate * (e / (1.0 + e))
    e1 = gidx * epg + i1
    e2 = gidx * epg + i2
    rowe = lax.broadcasted_iota(jnp.int32, (N_EXPERTS, tc), 0)
    oh1 = rowe == e1
    oh2 = rowe == e2
    member = jnp.where(jnp.logical_or(oh1, oh2), 1.0, 0.0)
    chunk_cnt = jnp.sum(member, axis=1, keepdims=True)

    @pl.when(phase == 0)
    def _():
        @pl.when(c == 0)
        def _():
            cnt_ref[...] = jnp.zeros_like(cnt_ref)
        cnt_ref[...] += chunk_cnt

    @pl.when(phase == 1)
    def _():
        @pl.when(c == 0)
        def _():
            cnt = cnt_ref[...]
            nblk = jnp.floor((cnt + (MOE_BLOCK - 1)) * (1.0 / MOE_BLOCK))
            nchunk = jnp.floor((nblk + (CHUNK_BLOCKS - 1)) * (1.0 / CHUNK_BLOCKS))
            r = lax.broadcasted_iota(jnp.int32, (N_EXPERTS, LANES), 0)
            cidx = lax.broadcasted_iota(jnp.int32, (N_EXPERTS, LANES), 1)
            to_row = lambda col: jnp.sum(jnp.where(r == cidx, col, 0.0), axis=0, keepdims=True)
            cumsum_col = lambda col: jnp.sum(jnp.where(cidx <= r, to_row(col), 0.0), axis=1, keepdims=True)
            cumsum_row = lambda col: jnp.sum(jnp.where(r <= cidx, col, 0.0), axis=0, keepdims=True)
            bend = cumsum_col(nblk)
            bstart = bend - nblk
            cend = cumsum_col(nchunk)
            cstart = cend - nchunk
            pstart_ref[...] = bstart * MOE_BLOCK
            carry_ref[...] = jnp.zeros_like(carry_ref)
            lanef = lax.broadcasted_iota(jnp.int32, (1, LANES), 1).astype(F32)
            owner = jnp.minimum(jnp.sum(jnp.where(cend <= lanef, 1.0, 0.0), axis=0, keepdims=True),
                                N_EXPERTS - 1.0)
            own = r.astype(F32) == owner
            pick = lambda col: jnp.sum(jnp.where(own, col, 0.0), axis=0, keepdims=True)
            idx = lanef - pick(cstart)
            first = pick(bstart) + CHUNK_BLOCKS * idx
            size = jnp.clip(pick(nblk) - CHUNK_BLOCKS * idx, 0.0, float(CHUNK_BLOCKS))
            zero = jnp.zeros((1, LANES), F32)
            rows = [owner, first, size,
                    zero + jnp.sum(nchunk, axis=0, keepdims=True),
                    cumsum_row(nblk) - 1.0,
                    to_row(nblk),
                    zero + jnp.sum(nblk, axis=0, keepdims=True),
                    pick(nchunk)]
            for k, v in enumerate(rows):
                meta_ref[k:k + 1, :] = v.astype(jnp.int32)

        before = carry_ref[...] + jnp.dot(member.astype(BF16), tri_ref[...], preferred_element_type=F32)
        pos = before + pstart_ref[...]
        dest_ref[0:1, :] = jnp.sum(jnp.where(oh1, pos, 0.0), axis=0, keepdims=True).astype(jnp.int32)
        dest_ref[1:2, :] = jnp.sum(jnp.where(oh2, pos, 0.0), axis=0, keepdims=True).astype(jnp.int32)
        w_ref[0:1, :] = w1
        w_ref[1:2, :] = w2
        carry_ref[...] += chunk_cnt


def _route(logits_t, tc):
    t = logits_t.shape[0]
    nc = t // tc
    tri = jnp.asarray(np.triu(np.ones((tc, tc), np.float32), k=1), dtype=BF16)
    return pl.pallas_call(
        _route_kernel,
        grid=(2, nc),
        in_specs=[pl.BlockSpec((tc, LANES), lambda p, c: (c, 0)),
                  pl.BlockSpec((tc, tc), lambda p, c: (0, 0))],
        out_specs=[pl.BlockSpec((2, tc), lambda p, c: (0, c * p)),
                   pl.BlockSpec((2, tc), lambda p, c: (0, c * p)),
                   pl.BlockSpec((META_ROWS, LANES), lambda p, c: (0, 0))],
        out_shape=[jax.ShapeDtypeStruct((2, t), jnp.int32), jax.ShapeDtypeStruct((2, t), F32),
                   jax.ShapeDtypeStruct((META_ROWS, LANES), jnp.int32)],
        scratch_shapes=[pltpu.VMEM((N_EXPERTS, 1), F32)] * 3,
        compiler_params=_cparams("arbitrary", "arbitrary"),
        name="moe_route",
    )(logits_t, tri)


def _meta(meta_ref, row, lane=0):
    return meta_ref[row * LANES + lane]


def _fill_blocks(meta_ref, zbuf, dst_ref, sem, n_blocks, *, expert_tails):
    zbuf[...] = jnp.zeros_like(zbuf)
    n_used = _meta(meta_ref, M_NUSED)
    block = lambda b: pltpu.make_async_copy(zbuf, dst_ref.at[pl.ds(b * MOE_BLOCK, MOE_BLOCK), :], sem)

    def tails(fn):
        def body(e, carry):
            @pl.when(_meta(meta_ref, M_NBLK, e) > 0)
            def _():
                fn(block(_meta(meta_ref, M_LAST, e)))
            return carry
        lax.fori_loop(0, N_EXPERTS, body, 0)

    def unused(fn):
        def body(b, carry):
            fn(block(b))
            return carry
        lax.fori_loop(n_used, n_blocks, body, 0)

    for phase in (lambda cp: cp.start(), lambda cp: cp.wait()):
        if expert_tails:
            tails(phase)
        unused(phase)


def _dispatch_kernel(dest_ref, meta_ref, h_ref, xs_ref, zbuf, sem_z, sem, *, tokens, n_blocks):
    tm = h_ref.shape[0]

    @pl.when(pl.program_id(0) == 0)
    def _():
        _fill_blocks(meta_ref, zbuf, xs_ref, sem_z, n_blocks, expert_tails=True)

    base = pl.program_id(0) * tm

    for r in range(tm):
        for k in range(2):
            d = dest_ref[k * tokens + base + r]
            pltpu.make_async_copy(h_ref.at[pl.ds(r, 1), :], xs_ref.at[pl.ds(d, 1), :], sem).start()
    for k in range(2):
        pltpu.make_async_copy(h_ref, xs_ref.at[pl.ds(0, tm), :], sem).wait()


def _dispatch(dest_flat, meta_flat, h2p, n_blocks, tm):
    t, w = h2p.shape
    grid_spec = pltpu.PrefetchScalarGridSpec(
        num_scalar_prefetch=2,
        grid=(t // tm,),
        in_specs=[pl.BlockSpec((tm, w), lambda i, d, m: (i, 0))],
        out_specs=pl.BlockSpec(memory_space=pl.ANY),
        scratch_shapes=[pltpu.VMEM((MOE_BLOCK, w), h2p.dtype),
                        pltpu.SemaphoreType.DMA(()), pltpu.SemaphoreType.DMA(())],
    )
    return pl.pallas_call(
        functools.partial(_dispatch_kernel, tokens=t, n_blocks=n_blocks),
        grid_spec=grid_spec,
        out_shape=jax.ShapeDtypeStruct((n_blocks * MOE_BLOCK, w), h2p.dtype),
        compiler_params=_cparams("arbitrary"),
        name="moe_dispatch",
    )(dest_flat, meta_flat, h2p)


def _pack_rows(x):
    bits = lax.bitcast_convert_type(x, jnp.uint32)
    half = x.shape[1] // 2
    return (bits[:, half:] & jnp.uint32(0xFFFF0000)) | (bits[:, :half] >> 16)


def _unpack_halves(words):
    lo = lax.bitcast_convert_type(words << 16, F32)
    hi = lax.bitcast_convert_type(words & jnp.uint32(0xFFFF0000), F32)
    return lo, hi


def _unpack_rows(words):
    return jnp.concatenate(_unpack_halves(words), axis=-1).astype(BF16)


def _expert_kernel(meta_ref, xs_ref, wg_ref, wu_ref, wd_ref, yb_ref,
                   xbuf, ybuf, zbuf, wg_f32, wu_f32, wd_f32, wslot_ref,
                   sem_in, sem_out, sem_z, sem_w, *, n_blocks):
    c = pl.program_id(0)
    n_chunks = _meta(meta_ref, M_NCHUNK)
    slot = c % 2

    def weight_copies(k, s):
        e = _meta(meta_ref, M_OWNER, k)
        return [pltpu.make_async_copy(src.at[e], dst.at[s], sem_w.at[s])
                for src, dst in ((wg_ref, wg_f32), (wu_ref, wu_f32), (wd_ref, wd_f32))]

    def in_copy(k, s, nb):
        rows = nb * MOE_BLOCK
        src = xs_ref.at[pl.ds(_meta(meta_ref, M_FIRST, k) * MOE_BLOCK, rows), :]
        return pltpu.make_async_copy(src, xbuf.at[s, pl.ds(0, rows), :], sem_in.at[s])

    def out_copy(k, s, nb):
        rows = nb * MOE_BLOCK
        dst = yb_ref.at[pl.ds(_meta(meta_ref, M_FIRST, k) * MOE_BLOCK, rows), :]
        return pltpu.make_async_copy(ybuf.at[s, pl.ds(0, rows), :], dst, sem_out.at[s])

    def by_size(k, fn):
        for nb in range(1, CHUNK_BLOCKS + 1):
            pl.when(_meta(meta_ref, M_SIZE, k) == nb)(functools.partial(fn, nb))

    @pl.when(c == 0)
    def _():
        by_size(0, lambda nb: in_copy(0, 0, nb).start())

    @pl.when(c + 1 < n_chunks)
    def _():
        by_size(c + 1, lambda nb: in_copy(c + 1, 1 - slot, nb).start())

    @pl.when(c < n_chunks)
    def _():
        prev = jnp.maximum(c - 1, 0)
        new_expert = jnp.logical_or(c == 0, _meta(meta_ref, M_OWNER, c) != _meta(meta_ref, M_OWNER, prev))

        @pl.when(c == 0)
        def _():
            wslot_ref[0] = 0
            for cp in weight_copies(0, 0):
                cp.start()

        @pl.when(new_expert)
        def _():
            ws = jnp.where(c == 0, 0, 1 - wslot_ref[0])
            wslot_ref[0] = ws
            nxt = c + _meta(meta_ref, M_OWNER_CHUNKS, c)

            @pl.when(nxt < n_chunks)
            def _():
                for cp in weight_copies(nxt, 1 - ws):
                    cp.start()
            for cp in weight_copies(c, ws):
                cp.wait()

        @pl.when(c >= 2)
        def _():
            by_size(c - 2, lambda nb: out_copy(c - 2, slot, nb).wait())

        def compute(nb):
            rows = nb * MOE_BLOCK
            ws = wslot_ref[0]
            in_copy(c, slot, nb).wait()
            h = _unpack_rows(xbuf[slot, 0:rows, :])
            gate = jnp.dot(h, wg_f32[ws].astype(BF16), preferred_element_type=F32)
            up = jnp.dot(h, wu_f32[ws].astype(BF16), preferred_element_type=F32)
            act = (jax.nn.silu(gate) * up).astype(BF16)
            y = jnp.dot(act, wd_f32[ws].astype(BF16), preferred_element_type=F32)
            ybuf[slot, 0:rows, :] = _pack_rows(y.astype(BF16).astype(F32))
            out_copy(c, slot, nb).start()
        by_size(c, compute)

    @pl.when(c == pl.num_programs(0) - 1)
    def _():
        _fill_blocks(meta_ref, zbuf, yb_ref, sem_z, n_blocks, expert_tails=False)
        for back in (2, 1):
            @pl.when(n_chunks >= back)
            def _(back=back):
                k = n_chunks - back
                by_size(k, lambda nb: out_copy(k, k % 2, nb).wait())


def _experts(meta_flat, xs, w_gate, w_up, w_down, n_blocks, n_chunks_max):
    d, de = w_gate.shape[1], w_gate.shape[2]
    rows = CHUNK_BLOCKS * MOE_BLOCK

    hbm = pl.BlockSpec(memory_space=pl.ANY)
    grid_spec = pltpu.PrefetchScalarGridSpec(
        num_scalar_prefetch=1,
        grid=(n_chunks_max,),
        in_specs=[hbm, hbm, hbm, hbm],
        out_specs=hbm,
        scratch_shapes=[pltpu.VMEM((2, rows, xs.shape[1]), xs.dtype),
                        pltpu.VMEM((2, rows, d // 2), jnp.uint32),
                        pltpu.VMEM((MOE_BLOCK, d // 2), jnp.uint32),
                        pltpu.VMEM((2, d, de), F32), pltpu.VMEM((2, d, de), F32), pltpu.VMEM((2, de, d), F32),
                        pltpu.SMEM((1,), jnp.int32),
                        pltpu.SemaphoreType.DMA((2,)), pltpu.SemaphoreType.DMA((2,)),
                        pltpu.SemaphoreType.DMA(()), pltpu.SemaphoreType.DMA((2,))],
    )
    return pl.pallas_call(
        functools.partial(_expert_kernel, n_blocks=n_blocks),
        grid_spec=grid_spec,
        out_shape=jax.ShapeDtypeStruct((n_blocks * MOE_BLOCK, d // 2), jnp.uint32),
        compiler_params=_cparams("arbitrary"),
        name="moe_experts",
    )(meta_flat, xs, w_gate, w_up, w_down)


def _combine_kernel(dest_ref, yb_ref, x1_ref, w_ref, out_ref, gbuf, sem, *, tokens):
    i = pl.program_id(0)
    n = pl.num_programs(0)
    tm = x1_ref.shape[0]
    half = x1_ref.shape[1] // 2
    slot = i % COMBINE_SLOTS
    ahead = COMBINE_SLOTS - 1

    def issue_row(tile, r):
        s = tile % COMBINE_SLOTS
        for k in range(2):
            d = dest_ref[k * tokens + tile * tm + r]
            pltpu.make_async_copy(yb_ref.at[pl.ds(d, 1), :], gbuf.at[s, k, pl.ds(r, 1), :], sem.at[s]).start()

    def combine_rows(r0):
        rows = slice(r0, r0 + COMBINE_PIECE)
        w = w_ref[rows, :]
        lo0, hi0 = _unpack_halves(gbuf[slot, 0, rows, :])
        lo1, hi1 = _unpack_halves(gbuf[slot, 1, rows, :])
        out_ref[rows, :half] = x1_ref[rows, :half] + (lo0 * w[:, 0:1] + lo1 * w[:, 1:2])
        out_ref[rows, half:] = x1_ref[rows, half:] + (hi0 * w[:, 0:1] + hi1 * w[:, 1:2])

    @pl.when(i == 0)
    def _():
        for first in range(ahead):
            @pl.when(first < n)
            def _(first=first):
                def body(g, carry):
                    r8 = pl.multiple_of(g * SUBLANES, SUBLANES)
                    for sub in range(SUBLANES):
                        issue_row(first, r8 + sub)
                    return carry
                lax.fori_loop(0, tm // SUBLANES, body, 0)

    for k in range(2):
        pltpu.make_async_copy(yb_ref.at[pl.ds(0, tm), :], gbuf.at[slot, k], sem.at[slot]).wait()

    @pl.when(i + ahead < n)
    def _():
        for r0 in range(0, tm, COMBINE_PIECE):
            for r in range(r0, r0 + COMBINE_PIECE):
                issue_row(i + ahead, r)
            combine_rows(r0)

    @pl.when(i + ahead >= n)
    def _():
        for r0 in range(0, tm, COMBINE_PIECE):
            combine_rows(r0)


def _combine(dest_flat, yb, x1, w_tok, tm):
    t, d = x1.shape
    grid_spec = pltpu.PrefetchScalarGridSpec(
        num_scalar_prefetch=1,
        grid=(t // tm,),
        in_specs=[pl.BlockSpec(memory_space=pl.ANY),
                  pl.BlockSpec((tm, d), lambda i, dr: (i, 0)),
                  pl.BlockSpec((tm, 2), lambda i, dr: (i, 0))],
        out_specs=pl.BlockSpec((tm, d), lambda i, dr: (i, 0)),
        scratch_shapes=[pltpu.VMEM((COMBINE_SLOTS, 2, tm, d // 2), jnp.uint32),
                        pltpu.SemaphoreType.DMA((COMBINE_SLOTS,))],
    )
    return pl.pallas_call(
        functools.partial(_combine_kernel, tokens=t),
        grid_spec=grid_spec,
        out_shape=jax.ShapeDtypeStruct((t, d), F32),
        compiler_params=_cparams("arbitrary"),
        name="moe_combine",
    )(dest_flat, yb, x1, w_tok)


def _row(v):
    return v.astype(F32).reshape(1, -1)


def _layer(x2, mem2, batch, seq, mem_len, p):
    t, d = x2.shape

    bias = _bias_table(p["rel_bias"])
    col = lambda v: jnp.broadcast_to(v.astype(F32)[:, None], (v.shape[0], BLOCK))
    qgain = col(p["q_norm"]) * (1.0 / math.sqrt(ATTN_HEAD_DIM))
    kgain = jnp.tile(_row(p["k_norm"]), (1, ATTN_KV_HEADS))
    u_sb, qx, ya = _inproj_swa(x2, _row(p["norm_mix"]), p["w_in"].astype(BF16), bias, p["attn_sinks"].astype(F32),
                               qgain, kgain, col(p["out_norm_attn"]), batch, seq, min(512, seq))

    km, vm = _memkv(mem2, _row(p["mem_norm"]), p["w_mem_kv"].astype(BF16), _row(p["xk_norm"]),
                    min(256, mem2.shape[0]))
    xq_gain = _row(p["xq_norm"]) * (1.0 / math.sqrt(XATTN_HEAD_DIM))
    yx = _xattn(qx, km, vm, xq_gain, _row(p["out_norm_xattn"]), batch, seq, mem_len, min(512, seq))

    a_re, a_im, bbr, bbi = _ssm_prep(p["ssm_lambda_re"], p["ssm_lambda_im"], p["ssm_log_dt"],
                                     p["ssm_b_re"], p["ssm_b_im"])
    bmat = jnp.concatenate([_block_diag_tiles(bbr), _block_diag_tiles(bbi)], axis=-1).astype(BF16)
    c_re_t = jnp.transpose(p["ssm_c_re"].astype(F32), (0, 2, 1))
    c_im_t = jnp.transpose(p["ssm_c_im"].astype(F32), (0, 2, 1))
    cmat = jnp.concatenate([_block_diag_tiles(c_re_t), _block_diag_tiles(-c_im_t)], axis=1).astype(BF16)
    steps = min(64, seq)
    ys_sb = _ssm(u_sb, bmat, cmat,
                 a_re.reshape(1, SSM_STATES), a_im.reshape(1, SSM_STATES), _row(p["ssm_d"]),
                 p["ssm_w_glu"].astype(BF16), _row(p["out_norm_ssm"]), batch, seq, steps)

    wr = jnp.concatenate([p["w_router_group"], p["w_router_expert"]], axis=1).astype(F32)
    wr = jnp.pad(wr, ((0, 0), (0, LANES - wr.shape[1]))).astype(BF16)
    tm_out = min(512, seq)
    x1, logits_t, h2p = _outproj(ya, ys_sb, yx, x2, p["w_o"].astype(BF16), _row(p["norm_ffn"]), wr,
                                 batch, seq, tm_out)

    dest, w_k, meta = _route(logits_t, min(2048, t))
    n_blocks = (2 * t) // MOE_BLOCK + N_EXPERTS
    n_chunks_max = (n_blocks + (CHUNK_BLOCKS - 1) * N_EXPERTS) // CHUNK_BLOCKS
    dest_flat = dest.reshape(2 * t)
    meta_flat = meta.reshape(META_ROWS * LANES)
    xs = _dispatch(dest_flat, meta_flat, h2p, n_blocks, min(1024, t))
    yb = _experts(meta_flat, xs, p["w_gate"], p["w_up"], p["w_down"], n_blocks, n_chunks_max)
    return _combine(dest_flat, yb, x1, w_k.T, min(512, t))


def kernel(x, mem, norm_mix, w_in, q_norm, k_norm, attn_sinks, rel_bias, ssm_lambda_re, ssm_lambda_im, ssm_log_dt, ssm_b_re, ssm_b_im, ssm_c_re, ssm_c_im, ssm_d, ssm_w_glu, mem_norm, w_mem_kv, xq_norm, xk_norm, out_norm_attn, out_norm_ssm, out_norm_xattn, w_o, norm_ffn, w_router_group, w_router_expert, w_gate, w_up, w_down):
    batch, seq, d = x.shape
    mem_len = mem.shape[1]
    per_layer = dict(norm_mix=norm_mix, w_in=w_in, q_norm=q_norm, k_norm=k_norm, attn_sinks=attn_sinks,
                     ssm_lambda_re=ssm_lambda_re, ssm_lambda_im=ssm_lambda_im, ssm_log_dt=ssm_log_dt,
                     ssm_b_re=ssm_b_re, ssm_b_im=ssm_b_im, ssm_c_re=ssm_c_re, ssm_c_im=ssm_c_im,
                     ssm_d=ssm_d, ssm_w_glu=ssm_w_glu, mem_norm=mem_norm, w_mem_kv=w_mem_kv,
                     xq_norm=xq_norm, xk_norm=xk_norm, out_norm_attn=out_norm_attn,
                     out_norm_ssm=out_norm_ssm, out_norm_xattn=out_norm_xattn, w_o=w_o, norm_ffn=norm_ffn,
                     w_router_group=w_router_group, w_router_expert=w_router_expert,
                     w_gate=w_gate, w_up=w_up, w_down=w_down)
    x2 = x.astype(F32).reshape(batch * seq, d)
    mem2 = mem.astype(F32).reshape(batch * mem_len, d)
    for l in range(norm_mix.shape[0]):
        p = {k: v[l] for k, v in per_layer.items()}
        p["rel_bias"] = rel_bias
        x2 = _layer(x2, mem2, batch, seq, mem_len, p)
    return x2.reshape(batch, seq, d).astype(x.dtype)
```

```python
import functools
import math

import numpy as np
import jax
import jax.numpy as jnp
from jax import lax
from jax.experimental import pallas as pl
from jax.experimental.pallas import tpu as pltpu

F32 = jnp.float32
BF16 = jnp.bfloat16
EPS = 1e-6

ATTN_HEADS = 16
ATTN_KV_HEADS = 2
ATTN_HEAD_DIM = 64
ATTN_WIDTH = ATTN_HEADS * ATTN_HEAD_DIM
WINDOW = 128
BLOCK = 128
REL_BUCKETS = 32
REL_MAX_DIST = 128
SSM_GROUP_CH = 16
SSM_GROUPS = 32
SSM_STATE = 64
SSM_WIDTH = SSM_GROUPS * SSM_GROUP_CH
XATTN_HEADS = 4
XATTN_HEAD_DIM = 128
XATTN_WIDTH = XATTN_HEADS * XATTN_HEAD_DIM
N_EXPERT_GROUPS = 8
EXPERTS_PER_GROUP = 8
N_EXPERTS = N_EXPERT_GROUPS * EXPERTS_PER_GROUP
D_EXPERT = 512
MOE_BLOCK = 128

LANES = 128
SUBLANES = 8
SSM_GROUPS_PER_TILE = LANES // SSM_GROUP_CH
SSM_TILES = SSM_WIDTH // LANES
SSM_TILE_STATE = SSM_GROUPS_PER_TILE * SSM_STATE
SSM_STATES = SSM_GROUPS * SSM_STATE
SSM_PIECES = 4
CHUNK_BLOCKS = 4
COMBINE_SLOTS = 3
COMBINE_PIECE = 32
META_ROWS = 8
M_OWNER, M_FIRST, M_SIZE, M_NCHUNK, M_LAST, M_NBLK, M_NUSED, M_OWNER_CHUNKS = range(8)
VMEM_LIMIT = 56 * 1024 * 1024

_NT = (((1,), (1,)), ((), ()))


def _cparams(*sem):
    return pltpu.CompilerParams(dimension_semantics=sem, vmem_limit_bytes=VMEM_LIMIT)


def _rms(x, gain):
    ms = jnp.mean(x * x, axis=-1, keepdims=True)
    return x * lax.rsqrt(ms + EPS) * gain


def _t5_bucket_table():
    qi = np.arange(BLOCK, dtype=np.int32)[:, None]
    ki = np.arange(2 * BLOCK, dtype=np.int32)[None, :]
    delta = BLOCK + qi - ki
    n = np.maximum(delta, 0)
    max_exact = REL_BUCKETS // 2
    nf = np.maximum(n, 1).astype(np.float32)
    large = max_exact + (np.log(nf / np.float32(max_exact)) / np.float32(math.log(REL_MAX_DIST / max_exact))
                         * np.float32(REL_BUCKETS - max_exact)).astype(np.int32)
    large = np.minimum(large, REL_BUCKETS - 1)
    return np.where(n < max_exact, n, large).astype(np.int32)


def _upper(cols):
    k = lax.broadcasted_iota(jnp.int32, (BLOCK, cols), 0)
    q = lax.broadcasted_iota(jnp.int32, (BLOCK, cols), 1) % BLOCK
    return k > q


def _bias_kernel(rb_ref, bucket_ref, out_ref):
    pair = pl.program_id(0)
    bucket = bucket_ref[...]
    upper = _upper(BLOCK)
    for half in range(2):
        acc = jnp.zeros(bucket.shape, F32)
        for b in range(REL_BUCKETS):
            acc = jnp.where(bucket == b, rb_ref[b, 2 * pair + half], acc)
        cols = slice(half * BLOCK, (half + 1) * BLOCK)
        out_ref[0, 0, :, cols] = jnp.where(upper, acc[:BLOCK], acc[BLOCK:])
        out_ref[1, 0, :, cols] = jnp.where(upper, jnp.float32(-1e30), acc[BLOCK:])


def _bias_table(rel_bias):
    bucket = jnp.asarray(_t5_bucket_table().T)
    pairs = ATTN_HEADS // 2
    return pl.pallas_call(
        _bias_kernel,
        grid=(pairs,),
        in_specs=[pl.BlockSpec(memory_space=pltpu.SMEM),
                  pl.BlockSpec((2 * BLOCK, BLOCK), lambda h: (0, 0))],
        out_specs=pl.BlockSpec((2, 1, BLOCK, 2 * BLOCK), lambda h: (0, h, 0, 0)),
        out_shape=jax.ShapeDtypeStruct((2, pairs, BLOCK, 2 * BLOCK), F32),
        compiler_params=_cparams("arbitrary"),
        name="t5_bias_table",
    )(rel_bias.astype(F32), bucket)


def _ssm_prep_kernel(lr_ref, li_ref, ldt_ref, br_ref, bi_ref, are_ref, aim_ref, bbr_ref, bbi_ref):
    lr = lr_ref[...]
    li = li_ref[...]
    dt = jnp.exp(ldt_ref[...])
    mag = jnp.exp(lr * dt)
    a_re = mag * jnp.cos(li * dt)
    a_im = mag * jnp.sin(li * dt)
    den = lr * lr + li * li
    nr = a_re - 1.0
    ni = a_im
    coef_re = (nr * lr + ni * li) / den
    coef_im = (ni * lr - nr * li) / den
    are_ref[...] = a_re
    aim_ref[...] = a_im
    br = br_ref[...]
    bi = bi_ref[...]
    bbr_ref[...] = coef_re * br - coef_im * bi
    bbi_ref[...] = coef_re * bi + coef_im * br


def _ssm_prep(lam_re, lam_im, log_dt, b_re, b_im):
    g, n, c = b_re.shape
    vec = jax.ShapeDtypeStruct((g, 1, n), F32)
    mat = jax.ShapeDtypeStruct((g, c, n), F32)
    return pl.pallas_call(
        _ssm_prep_kernel,
        out_shape=(vec, vec, mat, mat),
        name="ssm_discretise",
    )(lam_re.astype(F32).reshape(g, 1, n), lam_im.astype(F32).reshape(g, 1, n),
      log_dt.astype(F32).reshape(g, 1, 1),
      jnp.transpose(b_re.astype(F32), (0, 2, 1)), jnp.transpose(b_im.astype(F32), (0, 2, 1)))


def _block_diag_tiles(m):
    g, r, c = m.shape
    t = g // SSM_GROUPS_PER_TILE
    eye = jnp.eye(SSM_GROUPS_PER_TILE, dtype=m.dtype)
    m4 = m.reshape(t, SSM_GROUPS_PER_TILE, r, c)
    out = m4[:, :, :, None, :] * eye[None, :, None, :, None]
    return out.reshape(t, SSM_GROUPS_PER_TILE * r, SSM_GROUPS_PER_TILE * c)


def _k_norm(k, kgain):
    dh = ATTN_HEAD_DIM
    lo = lax.broadcasted_iota(jnp.int32, (1, ATTN_KV_HEADS * dh), 1) < dh
    sq = k * k
    s_lo = jnp.sum(jnp.where(lo, sq, 0.0), axis=-1, keepdims=True)
    s_hi = jnp.sum(sq, axis=-1, keepdims=True) - s_lo
    return k * jnp.where(lo, lax.rsqrt(s_lo / dh + EPS), lax.rsqrt(s_hi / dh + EPS)) * kgain


def _swa_phases(q_of, kn, v_t, bias_of, sinks_ref, qgain, ogain, emit):
    dh = ATTN_HEAD_DIM
    upper = _upper(2 * BLOCK)
    first_head = lax.broadcasted_iota(jnp.int32, (1, 2 * BLOCK), 1) < BLOCK
    zeros = jnp.zeros((dh, BLOCK), F32)
    heads_per_kv = ATTN_HEADS // ATTN_KV_HEADS
    pairs = range(ATTN_HEADS // 2)
    kv_of = lambda pair: (2 * pair) // heads_per_kv
    state = {}

    def qk_all():
        logits = []
        for pair in pairs:
            cols = []
            for half in range(2):
                qh = q_of(2 * pair + half).astype(F32)
                ms = jnp.mean(qh * qh, axis=0, keepdims=True)
                qn = qh * lax.rsqrt(ms + EPS) * qgain
                cols.append(jnp.concatenate([qn, zeros] if kv_of(pair) == 0 else [zeros, qn], axis=0))
            rhs = jnp.concatenate(cols, axis=1).astype(BF16)
            logits.append(jnp.dot(kn, rhs, preferred_element_type=F32))
        state["logits"] = logits

    def softmax_all():
        probs = []
        for pair in pairs:
            both = state["logits"][pair]
            l = jnp.where(upper, both[:BLOCK], both[BLOCK:]) + bias_of(pair)
            sink = jnp.where(first_head, sinks_ref[2 * pair], sinks_ref[2 * pair + 1])
            m = jnp.maximum(jnp.max(l, axis=0, keepdims=True), sink)
            p = jnp.exp(l - m)
            den = jnp.sum(p, axis=0, keepdims=True) + jnp.exp(sink - m)
            pz = jnp.concatenate([jnp.where(upper, p, 0.0), jnp.where(upper, 0.0, p)], axis=0).astype(BF16)
            probs.append((pz, den))
        state["probs"] = probs

    def pv_all():
        outs = []
        for pair in pairs:
            g = kv_of(pair)
            pz, den = state["probs"][pair]
            o = jnp.dot(v_t[g * dh:(g + 1) * dh, :], pz, preferred_element_type=F32) / den
            outs += [o[:, :BLOCK], o[:, BLOCK:]]
        y_t = jnp.concatenate(outs, axis=0)
        ms = jnp.mean(y_t * y_t, axis=0, keepdims=True)
        emit((y_t * lax.rsqrt(ms + EPS) * ogain).T)

    return qk_all, softmax_all, pv_all


def _inproj_swa_kernel(sinks_ref, x_ref, g_ref, w_ref, bias_ref, qg_ref, kg_ref, og_ref,
                       u_ref, qx_ref, ya_ref, q_s, k_s, v_s, kprev_s, vprev_s, *, tiles_per_seq):
    i = pl.program_id(0)
    slot = i % 2
    old = 1 - slot
    tm = x_ref.shape[0]
    blocks = tm // BLOCK
    kvw = ATTN_KV_HEADS * ATTN_HEAD_DIM

    @pl.when(i == 0)
    def _():
        q_s[1] = jnp.zeros(q_s.shape[1:], q_s.dtype)
        k_s[1] = jnp.zeros(k_s.shape[1:], k_s.dtype)
        v_s[1] = jnp.zeros(v_s.shape[1:], v_s.dtype)
        kprev_s[...] = jnp.zeros_like(kprev_s)
        vprev_s[...] = jnp.zeros_like(vprev_s)

    h = _rms(x_ref[...], g_ref[...]).astype(BF16)
    proj = lambda c0, c1: jnp.dot(h, w_ref[:, c0:c1].astype(BF16), preferred_element_type=F32)
    piece = 2 * LANES

    def q_piece(n):
        def run():
            q_s[slot, n * piece:(n + 1) * piece, :] = proj(n * piece, (n + 1) * piece).T.astype(q_s.dtype)
        return run

    def kv_piece():
        kv = proj(ATTN_WIDTH, ATTN_WIDTH + 2 * kvw)
        k_s[slot] = _k_norm(kv[:, :kvw], kg_ref[...]).astype(k_s.dtype)
        v_s[slot] = kv[:, kvw:].T.astype(v_s.dtype)

    def out_piece(ref, c0, n):
        def run():
            ref[:, n * piece:(n + 1) * piece] = proj(c0 + n * piece, c0 + (n + 1) * piece).astype(ref.dtype)
        return run

    c_u = ATTN_WIDTH + 2 * kvw
    c_qx = c_u + SSM_WIDTH
    projection = ([q_piece(n) for n in range(ATTN_WIDTH // piece)] + [kv_piece]
                  + [out_piece(u_ref, c_u, n) for n in range(SSM_WIDTH // piece)]
                  + [out_piece(qx_ref, c_qx, n) for n in range(XATTN_WIDTH // piece)])

    first_of_seq = (i + tiles_per_seq - 1) % tiles_per_seq == 0
    attention = []
    for blk in range(blocks):
        own = slice(blk * BLOCK, (blk + 1) * BLOCK)
        before = slice((blk - 1) * BLOCK, blk * BLOCK)
        k_before = kprev_s[...] if blk == 0 else k_s[old, before, :]
        v_before = vprev_s[...] if blk == 0 else v_s[old, :, before]
        table = jnp.where(first_of_seq, 1, 0) if blk == 0 else 0

        def emit(y, own=own):
            ya_ref[own, :] = y.astype(ya_ref.dtype)

        attention += _swa_phases(
            q_of=lambda hd, own=own: q_s[old, hd * ATTN_HEAD_DIM:(hd + 1) * ATTN_HEAD_DIM, own],
            kn=jnp.concatenate([k_before, k_s[old, own, :]], axis=0),
            v_t=jnp.concatenate([v_before, v_s[old, :, own]], axis=1),
            bias_of=lambda pair, table=table: bias_ref[table, pair],
            sinks_ref=sinks_ref, qgain=qg_ref[...], ogain=og_ref[...], emit=emit)

    longer, shorter = (attention, projection) if len(attention) >= len(projection) else (projection, attention)
    done = 0
    for n, fn in enumerate(longer):
        fn()
        due = (n + 1) * len(shorter) // len(longer)
        for extra in shorter[done:due]:
            extra()
        done = due

    last = slice((blocks - 1) * BLOCK, blocks * BLOCK)
    kprev_s[...] = k_s[old, last, :]
    vprev_s[...] = v_s[old, :, last]


def _inproj_swa(x2, gain, w_in, bias, sinks, qgain, kgain, ogain, batch, seq, tm):
    t, d = x2.shape
    n_tiles = t // tm
    tiles_per_seq = seq // tm
    kvw = ATTN_KV_HEADS * ATTN_HEAD_DIM
    proj_tile = lambda i: jnp.minimum(i, n_tiles - 1)
    attn_tile = lambda i: jnp.maximum(i - 1, 0)
    const = lambda shape: pl.BlockSpec(shape, lambda i: (0,) * len(shape), pipeline_mode=pl.Buffered(1))
    return pl.pallas_call(
        functools.partial(_inproj_swa_kernel, tiles_per_seq=tiles_per_seq),
        grid=(n_tiles + 1,),
        in_specs=[pl.BlockSpec(memory_space=pltpu.SMEM),
                  pl.BlockSpec((tm, d), lambda i: (proj_tile(i), 0)),
                  const((1, d)), const(w_in.shape), const(bias.shape),
                  const((ATTN_HEAD_DIM, BLOCK)), const((1, kvw)), const((ATTN_WIDTH, BLOCK))],
        out_specs=[
                   pl.BlockSpec((tm, SSM_WIDTH),
                                lambda i: (proj_tile(i) % tiles_per_seq, proj_tile(i) // tiles_per_seq)),
                   pl.BlockSpec((tm, XATTN_WIDTH), lambda i: (proj_tile(i), 0)),
                   pl.BlockSpec((tm, ATTN_WIDTH), lambda i: (attn_tile(i), 0))],
        out_shape=[jax.ShapeDtypeStruct((seq, batch * SSM_WIDTH), BF16),
                   jax.ShapeDtypeStruct((t, XATTN_WIDTH), BF16),
                   jax.ShapeDtypeStruct((t, ATTN_WIDTH), BF16)],
        scratch_shapes=[pltpu.VMEM((2, ATTN_WIDTH, tm), BF16),
                        pltpu.VMEM((2, tm, kvw), BF16),
                        pltpu.VMEM((2, kvw, tm), BF16),
                        pltpu.VMEM((BLOCK, kvw), BF16), pltpu.VMEM((kvw, BLOCK), BF16)],
        compiler_params=_cparams("arbitrary"),
        name="in_proj_swa",
    )(sinks, x2, gain, w_in, bias, qgain, kgain, ogain)


def _memkv_kernel(m_ref, g_ref, w_ref, kg_ref, k_ref, v_ref):
    h = _rms(m_ref[...], g_ref[...]).astype(BF16)
    km = jnp.dot(h, w_ref[:, :XATTN_WIDTH], preferred_element_type=F32)
    for hd in range(XATTN_HEADS):
        sl = slice(hd * XATTN_HEAD_DIM, (hd + 1) * XATTN_HEAD_DIM)
        k_ref[:, sl] = _rms(km[:, sl], kg_ref[...]).astype(k_ref.dtype)
    v_ref[...] = jnp.dot(h, w_ref[:, XATTN_WIDTH:], preferred_element_type=F32).astype(v_ref.dtype)


def _memkv(mem2, gain, w_kv, kgain, tm):
    r, d = mem2.shape
    row = lambda w: pl.BlockSpec((tm, w), lambda i: (i, 0))
    const = lambda shape: pl.BlockSpec(shape, lambda i: (0, 0))
    return pl.pallas_call(
        _memkv_kernel,
        grid=(r // tm,),
        in_specs=[row(d), const((1, d)), const((d, 2 * XATTN_WIDTH)), const((1, XATTN_HEAD_DIM))],
        out_specs=[row(XATTN_WIDTH), row(XATTN_WIDTH)],
        out_shape=[jax.ShapeDtypeStruct((r, XATTN_WIDTH), BF16)] * 2,
        compiler_params=_cparams("arbitrary"),
        name="mem_kv_proj",
    )(mem2, gain, w_kv, kgain)


def _xattn_kernel(q_ref, k_ref, v_ref, qg_ref, og_ref, out_ref):
    outs = []
    for hd in range(XATTN_HEADS):
        sl = slice(hd * XATTN_HEAD_DIM, (hd + 1) * XATTN_HEAD_DIM)
        qn = _rms(q_ref[:, sl].astype(F32), qg_ref[...]).astype(BF16)
        l = lax.dot_general(qn, k_ref[:, sl], _NT, preferred_element_type=F32)
        m = jnp.max(l, axis=-1, keepdims=True)
        p = jnp.exp(l - m)
        den = jnp.sum(p, axis=-1, keepdims=True)
        outs.append(jnp.dot(p.astype(BF16), v_ref[:, sl], preferred_element_type=F32) / den)
    y = jnp.concatenate(outs, axis=-1)
    out_ref[...] = _rms(y, og_ref[...]).astype(out_ref.dtype)


def _xattn(qx, km, vm, qgain, ogain, batch, seq, mem_len, tq):
    nq = seq // tq
    const2 = lambda b, i: (0, 0)
    return pl.pallas_call(
        _xattn_kernel,
        grid=(batch, nq),
        in_specs=[pl.BlockSpec((tq, XATTN_WIDTH), lambda b, i: (b * nq + i, 0)),
                  pl.BlockSpec((mem_len, XATTN_WIDTH), lambda b, i: (b, 0)),
                  pl.BlockSpec((mem_len, XATTN_WIDTH), lambda b, i: (b, 0)),
                  pl.BlockSpec((1, XATTN_HEAD_DIM), const2),
                  pl.BlockSpec((1, XATTN_WIDTH), const2)],
        out_specs=pl.BlockSpec((tq, XATTN_WIDTH), lambda b, i: (b * nq + i, 0)),
        out_shape=jax.ShapeDtypeStruct((batch * seq, XATTN_WIDTH), BF16),
        compiler_params=_cparams("arbitrary", "arbitrary"),
        name="mem_xattn",
    )(qx, km, vm, qgain, ogain)


def _ssm_kernel(u_ref, bmat_ref, cmat_ref, are_ref, aim_ref, d_ref, wglu_ref, og_ref, out_ref,
                tb_ref, xr_ref, xi_ref, sr_ref, si_ref, *, batch, steps):
    @pl.when(pl.program_id(0) == 0)
    def _():
        sr_ref[...] = jnp.zeros_like(sr_ref)
        si_ref[...] = jnp.zeros_like(si_ref)

    for b in range(batch):
        for j in range(SSM_TILES):
            c0 = b * SSM_WIDTH + j * LANES
            tb_ref[j, pl.ds(b, steps, stride=batch), :] = u_ref[:, c0:c0 + LANES].astype(F32)
    uf = jnp.concatenate([tb_ref[j] for j in range(SSM_TILES)], axis=-1)
    u = uf.astype(BF16)
    total = steps * batch
    piece = total // SSM_PIECES
    tiles_per_half = SSM_TILES // 2
    half_states = SSM_STATES // 2

    def bu_piece(j, k):
        rows = slice(k * piece, (k + 1) * piece)
        bu = jnp.dot(u[rows, j * LANES:(j + 1) * LANES], bmat_ref[j], preferred_element_type=F32)
        xr_ref[rows, j * SSM_TILE_STATE:(j + 1) * SSM_TILE_STATE] = bu[:, :SSM_TILE_STATE]
        xi_ref[rows, j * SSM_TILE_STATE:(j + 1) * SSM_TILE_STATE] = bu[:, SSM_TILE_STATE:]

    def c_piece(j, k):
        rows = slice(k * piece, (k + 1) * piece)
        sl = slice(j * SSM_TILE_STATE, (j + 1) * SSM_TILE_STATE)
        xcat = jnp.concatenate([xr_ref[rows, sl], xi_ref[rows, sl]], axis=-1).astype(BF16)
        return jnp.dot(xcat, cmat_ref[j], preferred_element_type=F32)

    def scan_half(hf, between):
        cs = slice(hf * half_states, (hf + 1) * half_states)
        ar = jnp.broadcast_to(are_ref[:, cs], (batch, half_states))
        ai = jnp.broadcast_to(aim_ref[:, cs], (batch, half_states))
        s_r, s_i = sr_ref[:, cs], si_ref[:, cs]
        every = steps // len(between)
        for t in range(steps):
            rows = slice(t * batch, (t + 1) * batch)
            s_r, s_i = (ar * s_r - ai * s_i + xr_ref[rows, cs], ar * s_i + ai * s_r + xi_ref[rows, cs])
            xr_ref[rows, cs] = s_r
            xi_ref[rows, cs] = s_i
            if t % every == every - 1:
                between[t // every]()
        sr_ref[:, cs] = s_r
        si_ref[:, cs] = s_i

    first = [(j, k) for j in range(tiles_per_half) for k in range(SSM_PIECES)]
    second = [(j, k) for j in range(tiles_per_half, SSM_TILES) for k in range(SSM_PIECES)]
    for j, k in first:
        bu_piece(j, k)
    scan_half(0, [functools.partial(bu_piece, j, k) for j, k in second])
    y_pieces = {}
    scan_half(1, [functools.partial(lambda j, k: y_pieces.__setitem__((j, k), c_piece(j, k)), j, k)
                  for j, k in first])
    for j, k in second:
        y_pieces[(j, k)] = c_piece(j, k)
    ys = [jnp.concatenate([y_pieces[(j, k)] for k in range(SSM_PIECES)], axis=0) for j in range(SSM_TILES)]
    y = jnp.concatenate(ys, axis=-1) + d_ref[...] * uf
    y = jax.nn.gelu(y)
    y = y * jax.nn.sigmoid(jnp.dot(y.astype(BF16), wglu_ref[...], preferred_element_type=F32))
    y = _rms(y, og_ref[...])
    for j in range(SSM_TILES):
        tb_ref[j] = y[:, j * LANES:(j + 1) * LANES]
    for b in range(batch):
        for j in range(SSM_TILES):
            c0 = b * SSM_WIDTH + j * LANES
            out_ref[:, c0:c0 + LANES] = tb_ref[j, pl.ds(b, steps, stride=batch), :].astype(out_ref.dtype)


def _ssm(u_sb, bmat, cmat, a_re, a_im, d_skip, w_glu, ogain, batch, seq, steps):
    rows = steps * batch
    const2 = lambda c: (0, 0)
    const3 = lambda c: (0, 0, 0)
    return pl.pallas_call(
        functools.partial(_ssm_kernel, batch=batch, steps=steps),
        grid=(seq // steps,),
        in_specs=[pl.BlockSpec((steps, batch * SSM_WIDTH), lambda c: (c, 0)),
                  pl.BlockSpec(bmat.shape, const3), pl.BlockSpec(cmat.shape, const3),
                  pl.BlockSpec((1, SSM_STATES), const2), pl.BlockSpec((1, SSM_STATES), const2),
                  pl.BlockSpec((1, SSM_WIDTH), const2),
                  pl.BlockSpec((SSM_WIDTH, SSM_WIDTH), const2),
                  pl.BlockSpec((1, SSM_WIDTH), const2)],
        out_specs=pl.BlockSpec((steps, batch * SSM_WIDTH), lambda c: (c, 0)),
        out_shape=jax.ShapeDtypeStruct((seq, batch * SSM_WIDTH), BF16),
        scratch_shapes=[pltpu.VMEM((SSM_TILES, rows, LANES), F32),
                        pltpu.VMEM((rows, SSM_STATES), F32), pltpu.VMEM((rows, SSM_STATES), F32),
                        pltpu.VMEM((batch, SSM_STATES), F32), pltpu.VMEM((batch, SSM_STATES), F32)],
        compiler_params=_cparams("arbitrary"),
        name="s5_layer",
    )(u_sb, bmat, cmat, a_re, a_im, d_skip, w_glu, ogain)


def _outproj_kernel(ya_ref, ys_ref, yx_ref, x_ref, wo_ref, g_ref, wr_ref, x1_ref, lt_ref, hp_ref, *, sub):
    for r0 in range(0, x_ref.shape[0], sub):
        rows = slice(r0, r0 + sub)
        mix = jnp.concatenate([ya_ref[rows, :], ys_ref[rows, :], yx_ref[rows, :]], axis=-1)
        x1 = x_ref[rows, :] + jnp.dot(mix, wo_ref[...].astype(BF16), preferred_element_type=F32)
        x1_ref[rows, :] = x1
        h2 = _rms(x1, g_ref[...])
        hi = h2.astype(BF16)
        lt_ref[rows, :] = jnp.dot(hi, wr_ref[...], preferred_element_type=F32)
        hp_ref[rows, :] = _pack_rows(hi.astype(F32))


def _outproj(ya, ys_sb, yx, x2, w_o, gain, wr, batch, seq, tm):
    t, d = x2.shape
    nsb = seq // tm
    row = lambda w: pl.BlockSpec((tm, w), lambda i: (i, 0))
    const = lambda shape: pl.BlockSpec(shape, lambda i: (0, 0), pipeline_mode=pl.Buffered(1))
    return pl.pallas_call(
        functools.partial(_outproj_kernel, sub=min(256, tm)),
        grid=(t // tm,),
        in_specs=[row(ATTN_WIDTH),
                  pl.BlockSpec((tm, SSM_WIDTH), lambda i: (i % nsb, i // nsb)),
                  row(XATTN_WIDTH), row(d),
                  const(w_o.shape), const((1, d)), const(wr.shape)],
        out_specs=[row(d), row(LANES), row(d // 2)],
        out_shape=[jax.ShapeDtypeStruct((t, d), F32), jax.ShapeDtypeStruct((t, LANES), F32),
                   jax.ShapeDtypeStruct((t, d // 2), jnp.uint32)],
        compiler_params=_cparams("arbitrary"),
        name="out_proj_router",
    )(ya, ys_sb, yx, x2, w_o, gain, wr)


def _route_kernel(lt_ref, tri_ref, dest_ref, w_ref, meta_ref, cnt_ref, carry_ref, pstart_ref):
    phase = pl.program_id(0)
    c = pl.program_id(1)
    logits = lt_ref[...].T
    tc = logits.shape[1]
    ng, epg = N_EXPERT_GROUPS, EXPERTS_PER_GROUP
    row8 = lax.broadcasted_iota(jnp.int32, (ng, tc), 0)

    gl = logits[0:ng]
    gmax = jnp.max(gl, axis=0, keepdims=True)
    gidx = jnp.min(jnp.where(gl == gmax, row8, ng), axis=0, keepdims=True)
    gate = 1.0 / jnp.sum(jnp.exp(gl - gmax), axis=0, keepdims=True)
    sel = jnp.zeros((epg, tc), F32)
    for g in range(ng):
        sel = jnp.where(gidx == g, logits[ng + g * epg:ng + (g + 1) * epg], sel)
    v1 = jnp.max(sel, axis=0, keepdims=True)
    i1 = jnp.min(jnp.where(sel == v1, row8, epg), axis=0, keepdims=True)
    sel2 = jnp.where(row8 == i1, -jnp.inf, sel)
    v2 = jnp.max(sel2, axis=0, keepdims=True)
    i2 = jnp.min(jnp.where(sel2 == v2, row8, epg), axis=0, keepdims=True)
    e = jnp.exp(v2 - v1)
    w1 = gate * (1.0 / (1.0 + e))
    w2 = gate * (e / (1.0 + e))
    e1 = gidx * epg + i1
    e2 = gidx * epg + i2
    rowe = lax.broadcasted_iota(jnp.int32, (N_EXPERTS, tc), 0)
    oh1 = rowe == e1
    oh2 = rowe == e2
    member = jnp.where(jnp.logical_or(oh1, oh2), 1.0, 0.0)
    chunk_cnt = jnp.sum(member, axis=1, keepdims=True)

    @pl.when(phase == 0)
    def _():
        @pl.when(c == 0)
        def _():
            cnt_ref[...] = jnp.zeros_like(cnt_ref)
        cnt_ref[...] += chunk_cnt

    @pl.when(phase == 1)
    def _():
        @pl.when(c == 0)
        def _():
            cnt = cnt_ref[...]
            nblk = jnp.floor((cnt + (MOE_BLOCK - 1)) * (1.0 / MOE_BLOCK))
            nchunk = jnp.floor((nblk + (CHUNK_BLOCKS - 1)) * (1.0 / CHUNK_BLOCKS))
            r = lax.broadcasted_iota(jnp.int32, (N_EXPERTS, LANES), 0)
            cidx = lax.broadcasted_iota(jnp.int32, (N_EXPERTS, LANES), 1)
            to_row = lambda col: jnp.sum(jnp.where(r == cidx, col, 0.0), axis=0, keepdims=True)
            cumsum_col = lambda col: jnp.sum(jnp.where(cidx <= r, to_row(col), 0.0), axis=1, keepdims=True)
            cumsum_row = lambda col: jnp.sum(jnp.where(r <= cidx, col, 0.0), axis=0, keepdims=True)
            bend = cumsum_col(nblk)
            bstart = bend - nblk
            cend = cumsum_col(nchunk)
            cstart = cend - nchunk
            pstart_ref[...] = bstart * MOE_BLOCK
            carry_ref[...] = jnp.zeros_like(carry_ref)
            lanef = lax.broadcasted_iota(jnp.int32, (1, LANES), 1).astype(F32)
            owner = jnp.minimum(jnp.sum(jnp.where(cend <= lanef, 1.0, 0.0), axis=0, keepdims=True),
                                N_EXPERTS - 1.0)
            own = r.astype(F32) == owner
            pick = lambda col: jnp.sum(jnp.where(own, col, 0.0), axis=0, keepdims=True)
            idx = lanef - pick(cstart)
            first = pick(bstart) + CHUNK_BLOCKS * idx
            size = jnp.clip(pick(nblk) - CHUNK_BLOCKS * idx, 0.0, float(CHUNK_BLOCKS))
            zero = jnp.zeros((1, LANES), F32)
            rows = [owner, first, size,
                    zero + jnp.sum(nchunk, axis=0, keepdims=True),
                    cumsum_row(nblk) - 1.0,
                    to_row(nblk),
                    zero + jnp.sum(nblk, axis=0, keepdims=True),
                    pick(nchunk)]
            for k, v in enumerate(rows):
                meta_ref[k:k + 1, :] = v.astype(jnp.int32)

        before = carry_ref[...] + jnp.dot(member.astype(BF16), tri_ref[...], preferred_element_type=F32)
        pos = before + pstart_ref[...]
        dest_ref[0:1, :] = jnp.sum(jnp.where(oh1, pos, 0.0), axis=0, keepdims=True).astype(jnp.int32)
        dest_ref[1:2, :] = jnp.sum(jnp.where(oh2, pos, 0.0), axis=0, keepdims=True).astype(jnp.int32)
        w_ref[0:1, :] = w1
        w_ref[1:2, :] = w2
        carry_ref[...] += chunk_cnt


def _route(logits_t, tc):
    t = logits_t.shape[0]
    nc = t // tc
    tri = jnp.asarray(np.triu(np.ones((tc, tc), np.float32), k=1), dtype=BF16)
    return pl.pallas_call(
        _route_kernel,
        grid=(2, nc),
        in_specs=[pl.BlockSpec((tc, LANES), lambda p, c: (c, 0)),
                  pl.BlockSpec((tc, tc), lambda p, c: (0, 0))],
        out_specs=[pl.BlockSpec((2, tc), lambda p, c: (0, c * p)),
                   pl.BlockSpec((2, tc), lambda p, c: (0, c * p)),
                   pl.BlockSpec((META_ROWS, LANES), lambda p, c: (0, 0))],
        out_shape=[jax.ShapeDtypeStruct((2, t), jnp.int32), jax.ShapeDtypeStruct((2, t), F32),
                   jax.ShapeDtypeStruct((META_ROWS, LANES), jnp.int32)],
        scratch_shapes=[pltpu.VMEM((N_EXPERTS, 1), F32)] * 3,
        compiler_params=_cparams("arbitrary", "arbitrary"),
        name="moe_route",
    )(logits_t, tri)


def _meta(meta_ref, row, lane=0):
    return meta_ref[row * LANES + lane]


def _fill_blocks(meta_ref, zbuf, dst_ref, sem, n_blocks, *, expert_tails):
    zbuf[...] = jnp.zeros_like(zbuf)
    n_used = _meta(meta_ref, M_NUSED)
    block = lambda b: pltpu.make_async_copy(zbuf, dst_ref.at[pl.ds(b * MOE_BLOCK, MOE_BLOCK), :], sem)

    def tails(fn):
        def body(e, carry):
            @pl.when(_meta(meta_ref, M_NBLK, e) > 0)
            def _():
                fn(block(_meta(meta_ref, M_LAST, e)))
            return carry
        lax.fori_loop(0, N_EXPERTS, body, 0)

    def unused(fn):
        def body(b, carry):
            fn(block(b))
            return carry
        lax.fori_loop(n_used, n_blocks, body, 0)

    for phase in (lambda cp: cp.start(), lambda cp: cp.wait()):
        if expert_tails:
            tails(phase)
        unused(phase)


def _dispatch_kernel(dest_ref, meta_ref, h_ref, xs_ref, zbuf, sem_z, sem, *, tokens, n_blocks):
    tm = h_ref.shape[0]

    @pl.when(pl.program_id(0) == 0)
    def _():
        _fill_blocks(meta_ref, zbuf, xs_ref, sem_z, n_blocks, expert_tails=True)

    base = pl.program_id(0) * tm

    for r in range(tm):
        for k in range(2):
            d = dest_ref[k * tokens + base + r]
            pltpu.make_async_copy(h_ref.at[pl.ds(r, 1), :], xs_ref.at[pl.ds(d, 1), :], sem).start()
    for k in range(2):
        pltpu.make_async_copy(h_ref, xs_ref.at[pl.ds(0, tm), :], sem).wait()


def _dispatch(dest_flat, meta_flat, h2p, n_blocks, tm):
    t, w = h2p.shape
    grid_spec = pltpu.PrefetchScalarGridSpec(
        num_scalar_prefetch=2,
        grid=(t // tm,),
        in_specs=[pl.BlockSpec((tm, w), lambda i, d, m: (i, 0))],
        out_specs=pl.BlockSpec(memory_space=pl.ANY),
        scratch_shapes=[pltpu.VMEM((MOE_BLOCK, w), h2p.dtype),
                        pltpu.SemaphoreType.DMA(()), pltpu.SemaphoreType.DMA(())],
    )
    return pl.pallas_call(
        functools.partial(_dispatch_kernel, tokens=t, n_blocks=n_blocks),
        grid_spec=grid_spec,
        out_shape=jax.ShapeDtypeStruct((n_blocks * MOE_BLOCK, w), h2p.dtype),
        compiler_params=_cparams("arbitrary"),
        name="moe_dispatch",
    )(dest_flat, meta_flat, h2p)


def _pack_rows(x):
    bits = lax.bitcast_convert_type(x, jnp.uint32)
    half = x.shape[1] // 2
    return (bits[:, half:] & jnp.uint32(0xFFFF0000)) | (bits[:, :half] >> 16)


def _unpack_halves(words):
    lo = lax.bitcast_convert_type(words << 16, F32)
    hi = lax.bitcast_convert_type(words & jnp.uint32(0xFFFF0000), F32)
    return lo, hi


def _unpack_rows(words):
    return jnp.concatenate(_unpack_halves(words), axis=-1).astype(BF16)


def _expert_kernel(meta_ref, xs_ref, wg_ref, wu_ref, wd_ref, yb_ref,
                   xbuf, ybuf, zbuf, wg_f32, wu_f32, wd_f32, wslot_ref,
                   sem_in, sem_out, sem_z, sem_w, *, n_blocks):
    n_chunks = _meta(meta_ref, M_NCHUNK)

    def weight_copies(k, s):
        e = _meta(meta_ref, M_OWNER, k)
        return [pltpu.make_async_copy(src.at[e], dst.at[s], sem_w.at[s])
                for src, dst in ((wg_ref, wg_f32), (wu_ref, wu_f32), (wd_ref, wd_f32))]

    def in_copy(k, s, nb):
        rows = nb * MOE_BLOCK
        src = xs_ref.at[pl.ds(_meta(meta_ref, M_FIRST, k) * MOE_BLOCK, rows), :]
        return pltpu.make_async_copy(src, xbuf.at[s, pl.ds(0, rows), :], sem_in.at[s])

    def out_copy(k, s, nb):
        rows = nb * MOE_BLOCK
        dst = yb_ref.at[pl.ds(_meta(meta_ref, M_FIRST, k) * MOE_BLOCK, rows), :]
        return pltpu.make_async_copy(ybuf.at[s, pl.ds(0, rows), :], dst, sem_out.at[s])

    def by_size(k, fn):
        for nb in range(1, CHUNK_BLOCKS + 1):
            pl.when(_meta(meta_ref, M_SIZE, k) == nb)(functools.partial(fn, nb))

    wslot_ref[0] = 0
    by_size(0, lambda nb: in_copy(0, 0, nb).start())
    for cp in weight_copies(0, 0):
        cp.start()

    def chunk_step(c, carry):
        slot = c % 2

        @pl.when(c + 1 < n_chunks)
        def _():
            by_size(c + 1, lambda nb: in_copy(c + 1, 1 - slot, nb).start())

        prev = jnp.maximum(c - 1, 0)
        new_expert = jnp.logical_or(c == 0, _meta(meta_ref, M_OWNER, c) != _meta(meta_ref, M_OWNER, prev))

        @pl.when(new_expert)
        def _():
            ws = jnp.where(c == 0, 0, 1 - wslot_ref[0])
            wslot_ref[0] = ws
            nxt = c + _meta(meta_ref, M_OWNER_CHUNKS, c)

            @pl.when(nxt < n_chunks)
            def _():
                for cp in weight_copies(nxt, 1 - ws):
                    cp.start()
            for cp in weight_copies(c, ws):
                cp.wait()

        @pl.when(c >= 2)
        def _():
            by_size(c - 2, lambda nb: out_copy(c - 2, slot, nb).wait())

        def compute(nb):
            rows = nb * MOE_BLOCK
            ws = wslot_ref[0]
            in_copy(c, slot, nb).wait()
            h = _unpack_rows(xbuf[slot, 0:rows, :])
            gate = jnp.dot(h, wg_f32[ws].astype(BF16), preferred_element_type=F32)
            up = jnp.dot(h, wu_f32[ws].astype(BF16), preferred_element_type=F32)
            act = (jax.nn.silu(gate) * up).astype(BF16)
            y = jnp.dot(act, wd_f32[ws].astype(BF16), preferred_element_type=F32)
            ybuf[slot, 0:rows, :] = _pack_rows(y.astype(BF16).astype(F32))
            out_copy(c, slot, nb).start()
        by_size(c, compute)
        return carry

    lax.fori_loop(0, n_chunks, chunk_step, 0)

    _fill_blocks(meta_ref, zbuf, yb_ref, sem_z, n_blocks, expert_tails=False)
    for back in (2, 1):
        @pl.when(n_chunks >= back)
        def _(back=back):
            k = n_chunks - back
            by_size(k, lambda nb: out_copy(k, k % 2, nb).wait())


def _experts(meta_flat, xs, w_gate, w_up, w_down, n_blocks):
    d, de = w_gate.shape[1], w_gate.shape[2]
    rows = CHUNK_BLOCKS * MOE_BLOCK
    assert (n_blocks + (CHUNK_BLOCKS - 1) * N_EXPERTS) // CHUNK_BLOCKS <= LANES

    hbm = pl.BlockSpec(memory_space=pl.ANY)
    grid_spec = pltpu.PrefetchScalarGridSpec(
        num_scalar_prefetch=1,
        grid=(1,),
        in_specs=[hbm, hbm, hbm, hbm],
        out_specs=hbm,
        scratch_shapes=[pltpu.VMEM((2, rows, xs.shape[1]), xs.dtype),
                        pltpu.VMEM((2, rows, d // 2), jnp.uint32),
                        pltpu.VMEM((MOE_BLOCK, d // 2), jnp.uint32),
                        pltpu.VMEM((2, d, de), F32), pltpu.VMEM((2, d, de), F32), pltpu.VMEM((2, de, d), F32),
                        pltpu.SMEM((1,), jnp.int32),
                        pltpu.SemaphoreType.DMA((2,)), pltpu.SemaphoreType.DMA((2,)),
                        pltpu.SemaphoreType.DMA(()), pltpu.SemaphoreType.DMA((2,))],
    )
    return pl.pallas_call(
        functools.partial(_expert_kernel, n_blocks=n_blocks),
        grid_spec=grid_spec,
        out_shape=jax.ShapeDtypeStruct((n_blocks * MOE_BLOCK, d // 2), jnp.uint32),
        compiler_params=_cparams("arbitrary"),
        name="moe_experts",
    )(meta_flat, xs, w_gate, w_up, w_down)


def _combine_kernel(dest_ref, yb_ref, x1_ref, w_ref, out_ref, gbuf, sem, *, tokens):
    i = pl.program_id(0)
    n = pl.num_programs(0)
    tm = x1_ref.shape[0]
    half = x1_ref.shape[1] // 2
    slot = i % COMBINE_SLOTS
    ahead = COMBINE_SLOTS - 1

    def issue_row(tile, r):
        s = tile % COMBINE_SLOTS
        for k in range(2):
            d = dest_ref[k * tokens + tile * tm + r]
            pltpu.make_async_copy(yb_ref.at[pl.ds(d, 1), :], gbuf.at[s, k, pl.ds(r, 1), :], sem.at[s]).start()

    def combine_rows(r0):
        rows = slice(r0, r0 + COMBINE_PIECE)
        w = w_ref[rows, :]
        lo0, hi0 = _unpack_halves(gbuf[slot, 0, rows, :])
        lo1, hi1 = _unpack_halves(gbuf[slot, 1, rows, :])
        out_ref[rows, :half] = x1_ref[rows, :half] + (lo0 * w[:, 0:1] + lo1 * w[:, 1:2])
        out_ref[rows, half:] = x1_ref[rows, half:] + (hi0 * w[:, 0:1] + hi1 * w[:, 1:2])

    @pl.when(i == 0)
    def _():
        for first in range(ahead):
            @pl.when(first < n)
            def _(first=first):
                def body(g, carry):
                    r8 = pl.multiple_of(g * SUBLANES, SUBLANES)
                    for sub in range(SUBLANES):
                        issue_row(first, r8 + sub)
                    return carry
                lax.fori_loop(0, tm // SUBLANES, body, 0)

    for k in range(2):
        pltpu.make_async_copy(yb_ref.at[pl.ds(0, tm), :], gbuf.at[slot, k], sem.at[slot]).wait()

    @pl.when(i + ahead < n)
    def _():
        for r0 in range(0, tm, COMBINE_PIECE):
            for r in range(r0, r0 + COMBINE_PIECE):
                issue_row(i + ahead, r)
            combine_rows(r0)

    @pl.when(i + ahead >= n)
    def _():
        for r0 in range(0, tm, COMBINE_PIECE):
            combine_rows(r0)


def _combine(dest_flat, yb, x1, w_tok, tm):
    t, d = x1.shape
    grid_spec = pltpu.PrefetchScalarGridSpec(
        num_scalar_prefetch=1,
        grid=(t // tm,),
        in_specs=[pl.BlockSpec(memory_space=pl.ANY),
                  pl.BlockSpec((tm, d), lambda i, dr: (i, 0)),
                  pl.BlockSpec((tm, 2), lambda i, dr: (i, 0))],
        out_specs=pl.BlockSpec((tm, d), lambda i, dr: (i, 0)),
        scratch_shapes=[pltpu.VMEM((COMBINE_SLOTS, 2, tm, d // 2), jnp.uint32),
                        pltpu.SemaphoreType.DMA((COMBINE_SLOTS,))],
    )
    return pl.pallas_call(
        functools.partial(_combine_kernel, tokens=t),
        grid_spec=grid_spec,
        out_shape=jax.ShapeDtypeStruct((t, d), F32),
        compiler_params=_cparams("arbitrary"),
        name="moe_combine",
    )(dest_flat, yb, x1, w_tok)


def _row(v):
    return v.astype(F32).reshape(1, -1)


def _layer(x2, mem2, batch, seq, mem_len, p):
    t, d = x2.shape

    bias = _bias_table(p["rel_bias"])
    col = lambda v: jnp.broadcast_to(v.astype(F32)[:, None], (v.shape[0], BLOCK))
    qgain = col(p["q_norm"]) * (1.0 / math.sqrt(ATTN_HEAD_DIM))
    kgain = jnp.tile(_row(p["k_norm"]), (1, ATTN_KV_HEADS))
    u_sb, qx, ya = _inproj_swa(x2, _row(p["norm_mix"]), p["w_in"].astype(F32), bias, p["attn_sinks"].astype(F32),
                               qgain, kgain, col(p["out_norm_attn"]), batch, seq, min(512, seq))

    km, vm = _memkv(mem2, _row(p["mem_norm"]), p["w_mem_kv"].astype(BF16), _row(p["xk_norm"]),
                    min(256, mem2.shape[0]))
    xq_gain = _row(p["xq_norm"]) * (1.0 / math.sqrt(XATTN_HEAD_DIM))
    yx = _xattn(qx, km, vm, xq_gain, _row(p["out_norm_xattn"]), batch, seq, mem_len, min(512, seq))

    a_re, a_im, bbr, bbi = _ssm_prep(p["ssm_lambda_re"], p["ssm_lambda_im"], p["ssm_log_dt"],
                                     p["ssm_b_re"], p["ssm_b_im"])
    bmat = jnp.concatenate([_block_diag_tiles(bbr), _block_diag_tiles(bbi)], axis=-1).astype(BF16)
    c_re_t = jnp.transpose(p["ssm_c_re"].astype(F32), (0, 2, 1))
    c_im_t = jnp.transpose(p["ssm_c_im"].astype(F32), (0, 2, 1))
    cmat = jnp.concatenate([_block_diag_tiles(c_re_t), _block_diag_tiles(-c_im_t)], axis=1).astype(BF16)
    steps = min(64, seq)
    ys_sb = _ssm(u_sb, bmat, cmat,
                 a_re.reshape(1, SSM_STATES), a_im.reshape(1, SSM_STATES), _row(p["ssm_d"]),
                 p["ssm_w_glu"].astype(BF16), _row(p["out_norm_ssm"]), batch, seq, steps)

    wr = jnp.concatenate([p["w_router_group"], p["w_router_expert"]], axis=1).astype(F32)
    wr = jnp.pad(wr, ((0, 0), (0, LANES - wr.shape[1]))).astype(BF16)
    tm_out = min(512, seq)
    x1, logits_t, h2p = _outproj(ya, ys_sb, yx, x2, p["w_o"].astype(F32), _row(p["norm_ffn"]), wr,
                                 batch, seq, tm_out)

    dest, w_k, meta = _route(logits_t, min(2048, t))
    n_blocks = (2 * t) // MOE_BLOCK + N_EXPERTS
    dest_flat = dest.reshape(2 * t)
    meta_flat = meta.reshape(META_ROWS * LANES)
    xs = _dispatch(dest_flat, meta_flat, h2p, n_blocks, min(1024, t))
    yb = _experts(meta_flat, xs, p["w_gate"], p["w_up"], p["w_down"], n_blocks)
    return _combine(dest_flat, yb, x1, w_k.T, min(512, t))


def kernel(x, mem, norm_mix, w_in, q_norm, k_norm, attn_sinks, rel_bias, ssm_lambda_re, ssm_lambda_im, ssm_log_dt, ssm_b_re, ssm_b_im, ssm_c_re, ssm_c_im, ssm_d, ssm_w_glu, mem_norm, w_mem_kv, xq_norm, xk_norm, out_norm_attn, out_norm_ssm, out_norm_xattn, w_o, norm_ffn, w_router_group, w_router_expert, w_gate, w_up, w_down):
    batch, seq, d = x.shape
    mem_len = mem.shape[1]
    per_layer = dict(norm_mix=norm_mix, w_in=w_in, q_norm=q_norm, k_norm=k_norm, attn_sinks=attn_sinks,
                     ssm_lambda_re=ssm_lambda_re, ssm_lambda_im=ssm_lambda_im, ssm_log_dt=ssm_log_dt,
                     ssm_b_re=ssm_b_re, ssm_b_im=ssm_b_im, ssm_c_re=ssm_c_re, ssm_c_im=ssm_c_im,
                     ssm_d=ssm_d, ssm_w_glu=ssm_w_glu, mem_norm=mem_norm, w_mem_kv=w_mem_kv,
                     xq_norm=xq_norm, xk_norm=xk_norm, out_norm_attn=out_norm_attn,
                     out_norm_ssm=out_norm_ssm, out_norm_xattn=out_norm_xattn, w_o=w_o, norm_ffn=norm_ffn,
                     w_router_group=w_router_group, w_router_expert=w_router_expert,
                     w_gate=w_gate, w_up=w_up, w_down=w_down)
    x2 = x.astype(F32).reshape(batch * seq, d)
    mem2 = mem.astype(F32).reshape(batch * mem_len, d)
    for l in range(norm_mix.shape[0]):
        p = {k: v[l] for k, v in per_layer.items()}
        p["rel_bias"] = rel_bias
        x2 = _layer(x2, mem2, batch, seq, mem_len, p)
    return x2.reshape(batch, seq, d).astype(x.dtype)
```

```python
import functools
import math

import numpy as np
import jax
import jax.numpy as jnp
from jax import lax
from jax.experimental import pallas as pl
from jax.experimental.pallas import tpu as pltpu

F32 = jnp.float32
BF16 = jnp.bfloat16
EPS = 1e-6

ATTN_HEADS = 16
ATTN_KV_HEADS = 2
ATTN_HEAD_DIM = 64
ATTN_WIDTH = ATTN_HEADS * ATTN_HEAD_DIM
WINDOW = 128
BLOCK = 128
REL_BUCKETS = 32
REL_MAX_DIST = 128
SSM_GROUP_CH = 16
SSM_GROUPS = 32
SSM_STATE = 64
SSM_WIDTH = SSM_GROUPS * SSM_GROUP_CH
XATTN_HEADS = 4
XATTN_HEAD_DIM = 128
XATTN_WIDTH = XATTN_HEADS * XATTN_HEAD_DIM
N_EXPERT_GROUPS = 8
EXPERTS_PER_GROUP = 8
N_EXPERTS = N_EXPERT_GROUPS * EXPERTS_PER_GROUP
MOE_BLOCK = 128

LANES = 128
SUBLANES = 8
MXU_WIDTH = 256
TOKEN_TILE = 512
MEM_TILE = 256
SSM_STEPS = 64
ROUTE_CHUNK = 2048
DISPATCH_TILE = 1024
SSM_GROUPS_PER_TILE = LANES // SSM_GROUP_CH
SSM_TILES = SSM_WIDTH // LANES
SSM_TILE_STATE = SSM_GROUPS_PER_TILE * SSM_STATE
SSM_STATES = SSM_GROUPS * SSM_STATE
SSM_PIECES = 4
CHUNK_BLOCKS = 4
COMBINE_SLOTS = 3
COMBINE_PIECE = 32
META_ROWS = 8
M_OWNER, M_FIRST, M_SIZE, M_NCHUNK, M_LAST, M_NBLK, M_NUSED, M_OWNER_CHUNKS = range(8)
VMEM_LIMIT = 56 * 1024 * 1024

_NT = (((1,), (1,)), ((), ()))


def _cparams(*sem):
    return pltpu.CompilerParams(dimension_semantics=sem, vmem_limit_bytes=VMEM_LIMIT)


def _rms(x, gain):
    ms = jnp.mean(x * x, axis=-1, keepdims=True)
    return x * lax.rsqrt(ms + EPS) * gain


def _t5_bucket_table():
    qi = np.arange(BLOCK, dtype=np.int32)[:, None]
    ki = np.arange(2 * BLOCK, dtype=np.int32)[None, :]
    delta = BLOCK + qi - ki
    n = np.maximum(delta, 0)
    max_exact = REL_BUCKETS // 2
    nf = np.maximum(n, 1).astype(np.float32)
    large = max_exact + (np.log(nf / np.float32(max_exact)) / np.float32(math.log(REL_MAX_DIST / max_exact))
                         * np.float32(REL_BUCKETS - max_exact)).astype(np.int32)
    large = np.minimum(large, REL_BUCKETS - 1)
    return np.where(n < max_exact, n, large).astype(np.int32)


def _upper(cols):
    k = lax.broadcasted_iota(jnp.int32, (BLOCK, cols), 0)
    q = lax.broadcasted_iota(jnp.int32, (BLOCK, cols), 1) % BLOCK
    return k > q


def _bias_kernel(rb_ref, bucket_ref, out_ref):
    pair = pl.program_id(0)
    bucket = bucket_ref[...]
    upper = _upper(BLOCK)
    for half in range(2):
        acc = jnp.zeros(bucket.shape, F32)
        for b in range(REL_BUCKETS):
            acc = jnp.where(bucket == b, rb_ref[b, 2 * pair + half], acc)
        cols = slice(half * BLOCK, (half + 1) * BLOCK)
        out_ref[0, 0, :, cols] = jnp.where(upper, acc[:BLOCK], acc[BLOCK:])
        out_ref[1, 0, :, cols] = jnp.where(upper, jnp.float32(-1e30), acc[BLOCK:])


def _bias_table(rel_bias):
    assert WINDOW == BLOCK
    bucket = jnp.asarray(_t5_bucket_table().T)
    pairs = ATTN_HEADS // 2
    return pl.pallas_call(
        _bias_kernel,
        grid=(pairs,),
        in_specs=[pl.BlockSpec(memory_space=pltpu.SMEM),
                  pl.BlockSpec((2 * BLOCK, BLOCK), lambda h: (0, 0))],
        out_specs=pl.BlockSpec((2, 1, BLOCK, 2 * BLOCK), lambda h: (0, h, 0, 0)),
        out_shape=jax.ShapeDtypeStruct((2, pairs, BLOCK, 2 * BLOCK), F32),
        compiler_params=_cparams("arbitrary"),
        name="t5_bias_table",
    )(rel_bias.astype(F32), bucket)


def _ssm_prep_kernel(lr_ref, li_ref, ldt_ref, br_ref, bi_ref, are_ref, aim_ref, bbr_ref, bbi_ref):
    lr = lr_ref[...]
    li = li_ref[...]
    dt = jnp.exp(ldt_ref[...])
    mag = jnp.exp(lr * dt)
    a_re = mag * jnp.cos(li * dt)
    a_im = mag * jnp.sin(li * dt)
    den = lr * lr + li * li
    nr = a_re - 1.0
    ni = a_im
    coef_re = (nr * lr + ni * li) / den
    coef_im = (ni * lr - nr * li) / den
    are_ref[...] = a_re
    aim_ref[...] = a_im
    br = br_ref[...]
    bi = bi_ref[...]
    bbr_ref[...] = coef_re * br - coef_im * bi
    bbi_ref[...] = coef_re * bi + coef_im * br


def _ssm_prep(lam_re, lam_im, log_dt, b_re, b_im):
    g, n, c = b_re.shape
    vec = jax.ShapeDtypeStruct((g, 1, n), F32)
    mat = jax.ShapeDtypeStruct((g, c, n), F32)
    return pl.pallas_call(
        _ssm_prep_kernel,
        out_shape=(vec, vec, mat, mat),
        name="ssm_discretise",
    )(lam_re.astype(F32).reshape(g, 1, n), lam_im.astype(F32).reshape(g, 1, n),
      log_dt.astype(F32).reshape(g, 1, 1),
      jnp.transpose(b_re.astype(F32), (0, 2, 1)), jnp.transpose(b_im.astype(F32), (0, 2, 1)))


def _block_diag_tiles(m):
    g, r, c = m.shape
    t = g // SSM_GROUPS_PER_TILE
    eye = jnp.eye(SSM_GROUPS_PER_TILE, dtype=m.dtype)
    m4 = m.reshape(t, SSM_GROUPS_PER_TILE, r, c)
    out = m4[:, :, :, None, :] * eye[None, :, None, :, None]
    return out.reshape(t, SSM_GROUPS_PER_TILE * r, SSM_GROUPS_PER_TILE * c)


def _k_norm(k, kgain):
    dh = ATTN_HEAD_DIM
    lo = lax.broadcasted_iota(jnp.int32, (1, ATTN_KV_HEADS * dh), 1) < dh
    sq = k * k
    s_lo = jnp.sum(jnp.where(lo, sq, 0.0), axis=-1, keepdims=True)
    s_hi = jnp.sum(sq, axis=-1, keepdims=True) - s_lo
    return k * jnp.where(lo, lax.rsqrt(s_lo / dh + EPS), lax.rsqrt(s_hi / dh + EPS)) * kgain


def _swa_phases(q_of, kn, v_t, bias_of, sinks_ref, qgain, ogain, emit):
    dh = ATTN_HEAD_DIM
    upper = _upper(2 * BLOCK)
    first_head = lax.broadcasted_iota(jnp.int32, (1, 2 * BLOCK), 1) < BLOCK
    zeros = jnp.zeros((dh, BLOCK), F32)
    heads_per_kv = ATTN_HEADS // ATTN_KV_HEADS
    pairs = range(ATTN_HEADS // 2)
    kv_of = lambda pair: (2 * pair) // heads_per_kv
    state = {}

    def qk_all():
        logits = []
        for pair in pairs:
            cols = []
            for half in range(2):
                qh = q_of(2 * pair + half).astype(F32)
                ms = jnp.mean(qh * qh, axis=0, keepdims=True)
                qn = qh * lax.rsqrt(ms + EPS) * qgain
                cols.append(jnp.concatenate([qn, zeros] if kv_of(pair) == 0 else [zeros, qn], axis=0))
            rhs = jnp.concatenate(cols, axis=1).astype(BF16)
            logits.append(jnp.dot(kn, rhs, preferred_element_type=F32))
        state["logits"] = logits

    def softmax_all():
        probs = []
        for pair in pairs:
            both = state["logits"][pair]
            l = jnp.where(upper, both[:BLOCK], both[BLOCK:]) + bias_of(pair)
            sink = jnp.where(first_head, sinks_ref[2 * pair], sinks_ref[2 * pair + 1])
            m = jnp.maximum(jnp.max(l, axis=0, keepdims=True), sink)
            p = jnp.exp(l - m)
            den = jnp.sum(p, axis=0, keepdims=True) + jnp.exp(sink - m)
            pz = jnp.concatenate([jnp.where(upper, p, 0.0), jnp.where(upper, 0.0, p)], axis=0).astype(BF16)
            probs.append((pz, den))
        state["probs"] = probs

    def pv_all():
        outs = []
        for pair in pairs:
            g = kv_of(pair)
            pz, den = state["probs"][pair]
            o = jnp.dot(v_t[g * dh:(g + 1) * dh, :], pz, preferred_element_type=F32) / den
            outs += [o[:, :BLOCK], o[:, BLOCK:]]
        y_t = jnp.concatenate(outs, axis=0)
        ms = jnp.mean(y_t * y_t, axis=0, keepdims=True)
        emit((y_t * lax.rsqrt(ms + EPS) * ogain).T)

    return qk_all, softmax_all, pv_all


def _inproj_swa_kernel(sinks_ref, x_ref, g_ref, w_ref, bias_ref, qg_ref, kg_ref, og_ref,
                       u_ref, qx_ref, ya_ref, q_s, k_s, v_s, kprev_s, vprev_s, *, tiles_per_seq):
    i = pl.program_id(0)
    slot = i % 2
    old = 1 - slot
    tm = x_ref.shape[0]
    blocks = tm // BLOCK
    kvw = ATTN_KV_HEADS * ATTN_HEAD_DIM

    @pl.when(i == 0)
    def _():
        q_s[1] = jnp.zeros(q_s.shape[1:], q_s.dtype)
        k_s[1] = jnp.zeros(k_s.shape[1:], k_s.dtype)
        v_s[1] = jnp.zeros(v_s.shape[1:], v_s.dtype)
        kprev_s[...] = jnp.zeros_like(kprev_s)
        vprev_s[...] = jnp.zeros_like(vprev_s)

    h = _rms(x_ref[...], g_ref[...]).astype(BF16)
    proj = lambda c0, c1: jnp.dot(h, w_ref[:, c0:c1].astype(BF16), preferred_element_type=F32)
    piece = MXU_WIDTH

    def q_piece(n):
        def run():
            q_s[slot, n * piece:(n + 1) * piece, :] = proj(n * piece, (n + 1) * piece).T.astype(q_s.dtype)
        return run

    def kv_piece():
        kv = proj(ATTN_WIDTH, ATTN_WIDTH + 2 * kvw)
        k_s[slot] = _k_norm(kv[:, :kvw], kg_ref[...]).astype(k_s.dtype)
        v_s[slot] = kv[:, kvw:].T.astype(v_s.dtype)

    def out_piece(ref, c0, n):
        def run():
            ref[:, n * piece:(n + 1) * piece] = proj(c0 + n * piece, c0 + (n + 1) * piece).astype(ref.dtype)
        return run

    c_u = ATTN_WIDTH + 2 * kvw
    c_qx = c_u + SSM_WIDTH
    projection = ([q_piece(n) for n in range(ATTN_WIDTH // piece)] + [kv_piece]
                  + [out_piece(u_ref, c_u, n) for n in range(SSM_WIDTH // piece)]
                  + [out_piece(qx_ref, c_qx, n) for n in range(XATTN_WIDTH // piece)])

    first_of_seq = (i + tiles_per_seq - 1) % tiles_per_seq == 0
    attention = []
    for blk in range(blocks):
        own = slice(blk * BLOCK, (blk + 1) * BLOCK)
        before = slice((blk - 1) * BLOCK, blk * BLOCK)
        k_before = kprev_s[...] if blk == 0 else k_s[old, before, :]
        v_before = vprev_s[...] if blk == 0 else v_s[old, :, before]
        table = jnp.where(first_of_seq, 1, 0) if blk == 0 else 0

        def emit(y, own=own):
            ya_ref[own, :] = y.astype(ya_ref.dtype)

        attention += _swa_phases(
            q_of=lambda hd, own=own: q_s[old, hd * ATTN_HEAD_DIM:(hd + 1) * ATTN_HEAD_DIM, own],
            kn=jnp.concatenate([k_before, k_s[old, own, :]], axis=0),
            v_t=jnp.concatenate([v_before, v_s[old, :, own]], axis=1),
            bias_of=lambda pair, table=table: bias_ref[table, pair],
            sinks_ref=sinks_ref, qgain=qg_ref[...], ogain=og_ref[...], emit=emit)

    longer, shorter = (attention, projection) if len(attention) >= len(projection) else (projection, attention)
    done = 0
    for n, fn in enumerate(longer):
        fn()
        due = (n + 1) * len(shorter) // len(longer)
        for extra in shorter[done:due]:
            extra()
        done = due

    last = slice((blocks - 1) * BLOCK, blocks * BLOCK)
    kprev_s[...] = k_s[old, last, :]
    vprev_s[...] = v_s[old, :, last]


def _inproj_swa(x2, gain, w_in, bias, sinks, qgain, kgain, ogain, batch, seq, tm):
    t, d = x2.shape
    n_tiles = t // tm
    tiles_per_seq = seq // tm
    kvw = ATTN_KV_HEADS * ATTN_HEAD_DIM
    proj_tile = lambda i: jnp.minimum(i, n_tiles - 1)
    attn_tile = lambda i: jnp.maximum(i - 1, 0)
    const = lambda shape: pl.BlockSpec(shape, lambda i: (0,) * len(shape), pipeline_mode=pl.Buffered(1))
    return pl.pallas_call(
        functools.partial(_inproj_swa_kernel, tiles_per_seq=tiles_per_seq),
        grid=(n_tiles + 1,),
        in_specs=[pl.BlockSpec(memory_space=pltpu.SMEM),
                  pl.BlockSpec((tm, d), lambda i: (proj_tile(i), 0)),
                  const((1, d)), const(w_in.shape), const(bias.shape),
                  const((ATTN_HEAD_DIM, BLOCK)), const((1, kvw)), const((ATTN_WIDTH, BLOCK))],
        out_specs=[
                   pl.BlockSpec((tm, SSM_WIDTH),
                                lambda i: (proj_tile(i) % tiles_per_seq, proj_tile(i) // tiles_per_seq)),
                   pl.BlockSpec((tm, XATTN_WIDTH), lambda i: (proj_tile(i), 0)),
                   pl.BlockSpec((tm, ATTN_WIDTH), lambda i: (attn_tile(i), 0))],
        out_shape=[jax.ShapeDtypeStruct((seq, batch * SSM_WIDTH), BF16),
                   jax.ShapeDtypeStruct((t, XATTN_WIDTH), BF16),
                   jax.ShapeDtypeStruct((t, ATTN_WIDTH), BF16)],
        scratch_shapes=[pltpu.VMEM((2, ATTN_WIDTH, tm), BF16),
                        pltpu.VMEM((2, tm, kvw), BF16),
                        pltpu.VMEM((2, kvw, tm), BF16),
                        pltpu.VMEM((BLOCK, kvw), BF16), pltpu.VMEM((kvw, BLOCK), BF16)],
        compiler_params=_cparams("arbitrary"),
        name="in_proj_swa",
    )(sinks, x2, gain, w_in, bias, qgain, kgain, ogain)


def _memkv_kernel(m_ref, g_ref, w_ref, kg_ref, k_ref, v_ref):
    h = _rms(m_ref[...], g_ref[...]).astype(BF16)
    km = jnp.dot(h, w_ref[:, :XATTN_WIDTH].astype(BF16), preferred_element_type=F32)
    for hd in range(XATTN_HEADS):
        sl = slice(hd * XATTN_HEAD_DIM, (hd + 1) * XATTN_HEAD_DIM)
        k_ref[:, sl] = _rms(km[:, sl], kg_ref[...]).astype(k_ref.dtype)
    v_ref[...] = jnp.dot(h, w_ref[:, XATTN_WIDTH:].astype(BF16), preferred_element_type=F32).astype(v_ref.dtype)


def _memkv(mem2, gain, w_kv, kgain, tm):
    r, d = mem2.shape
    row = lambda w: pl.BlockSpec((tm, w), lambda i: (i, 0))
    const = lambda shape: pl.BlockSpec(shape, lambda i: (0, 0), pipeline_mode=pl.Buffered(1))
    return pl.pallas_call(
        _memkv_kernel,
        grid=(r // tm,),
        in_specs=[row(d), const((1, d)), const((d, 2 * XATTN_WIDTH)), const((1, XATTN_HEAD_DIM))],
        out_specs=[row(XATTN_WIDTH), row(XATTN_WIDTH)],
        out_shape=[jax.ShapeDtypeStruct((r, XATTN_WIDTH), BF16)] * 2,
        compiler_params=_cparams("arbitrary"),
        name="mem_kv_proj",
    )(mem2, gain, w_kv, kgain)


def _xattn_kernel(q_ref, k_ref, v_ref, qg_ref, og_ref, out_ref):
    outs = []
    for hd in range(XATTN_HEADS):
        sl = slice(hd * XATTN_HEAD_DIM, (hd + 1) * XATTN_HEAD_DIM)
        qn = _rms(q_ref[:, sl].astype(F32), qg_ref[...]).astype(BF16)
        l = lax.dot_general(qn, k_ref[:, sl], _NT, preferred_element_type=F32)
        m = jnp.max(l, axis=-1, keepdims=True)
        p = jnp.exp(l - m)
        den = jnp.sum(p, axis=-1, keepdims=True)
        outs.append(jnp.dot(p.astype(BF16), v_ref[:, sl], preferred_element_type=F32) / den)
    y = jnp.concatenate(outs, axis=-1)
    out_ref[...] = _rms(y, og_ref[...]).astype(out_ref.dtype)


def _xattn(qx, km, vm, qgain, ogain, batch, seq, mem_len, tq):
    nq = seq // tq
    const2 = lambda b, i: (0, 0)
    return pl.pallas_call(
        _xattn_kernel,
        grid=(batch, nq),
        in_specs=[pl.BlockSpec((tq, XATTN_WIDTH), lambda b, i: (b * nq + i, 0)),
                  pl.BlockSpec((mem_len, XATTN_WIDTH), lambda b, i: (b, 0)),
                  pl.BlockSpec((mem_len, XATTN_WIDTH), lambda b, i: (b, 0)),
                  pl.BlockSpec((1, XATTN_HEAD_DIM), const2),
                  pl.BlockSpec((1, XATTN_WIDTH), const2)],
        out_specs=pl.BlockSpec((tq, XATTN_WIDTH), lambda b, i: (b * nq + i, 0)),
        out_shape=jax.ShapeDtypeStruct((batch * seq, XATTN_WIDTH), BF16),
        compiler_params=_cparams("arbitrary", "arbitrary"),
        name="mem_xattn",
    )(qx, km, vm, qgain, ogain)


def _ssm_kernel(u_ref, bmat_ref, cmat_ref, are_ref, aim_ref, d_ref, wglu_ref, og_ref, out_ref,
                tb_ref, xr_ref, xi_ref, sr_ref, si_ref, *, batch, steps):
    @pl.when(pl.program_id(0) == 0)
    def _():
        sr_ref[...] = jnp.zeros_like(sr_ref)
        si_ref[...] = jnp.zeros_like(si_ref)

    for b in range(batch):
        for j in range(SSM_TILES):
            c0 = b * SSM_WIDTH + j * LANES
            tb_ref[j, pl.ds(b, steps, stride=batch), :] = u_ref[:, c0:c0 + LANES].astype(F32)
    uf = jnp.concatenate([tb_ref[j] for j in range(SSM_TILES)], axis=-1)
    u = uf.astype(BF16)
    total = steps * batch
    piece = total // SSM_PIECES
    tiles_per_half = SSM_TILES // 2
    half_states = SSM_STATES // 2

    def bu_piece(j, k):
        rows = slice(k * piece, (k + 1) * piece)
        bu = jnp.dot(u[rows, j * LANES:(j + 1) * LANES], bmat_ref[j], preferred_element_type=F32)
        xr_ref[rows, j * SSM_TILE_STATE:(j + 1) * SSM_TILE_STATE] = bu[:, :SSM_TILE_STATE]
        xi_ref[rows, j * SSM_TILE_STATE:(j + 1) * SSM_TILE_STATE] = bu[:, SSM_TILE_STATE:]

    def c_piece(j, k):
        rows = slice(k * piece, (k + 1) * piece)
        sl = slice(j * SSM_TILE_STATE, (j + 1) * SSM_TILE_STATE)
        xcat = jnp.concatenate([xr_ref[rows, sl], xi_ref[rows, sl]], axis=-1).astype(BF16)
        return jnp.dot(xcat, cmat_ref[j], preferred_element_type=F32)

    def scan_half(hf, between):
        cs = slice(hf * half_states, (hf + 1) * half_states)
        ar = jnp.broadcast_to(are_ref[:, cs], (batch, half_states))
        ai = jnp.broadcast_to(aim_ref[:, cs], (batch, half_states))
        s_r, s_i = sr_ref[:, cs], si_ref[:, cs]
        every = steps // len(between)
        for t in range(steps):
            rows = slice(t * batch, (t + 1) * batch)
            s_r, s_i = (ar * s_r - ai * s_i + xr_ref[rows, cs], ar * s_i + ai * s_r + xi_ref[rows, cs])
            xr_ref[rows, cs] = s_r
            xi_ref[rows, cs] = s_i
            if t % every == every - 1:
                between[t // every]()
        sr_ref[:, cs] = s_r
        si_ref[:, cs] = s_i

    first = [(j, k) for j in range(tiles_per_half) for k in range(SSM_PIECES)]
    second = [(j, k) for j in range(tiles_per_half, SSM_TILES) for k in range(SSM_PIECES)]
    for j, k in first:
        bu_piece(j, k)
    scan_half(0, [functools.partial(bu_piece, j, k) for j, k in second])
    y_pieces = {}
    scan_half(1, [functools.partial(lambda j, k: y_pieces.__setitem__((j, k), c_piece(j, k)), j, k)
                  for j, k in first])
    for j, k in second:
        y_pieces[(j, k)] = c_piece(j, k)
    ys = [jnp.concatenate([y_pieces[(j, k)] for k in range(SSM_PIECES)], axis=0) for j in range(SSM_TILES)]
    y = jnp.concatenate(ys, axis=-1) + d_ref[...] * uf
    y = jax.nn.gelu(y)
    y = y * jax.nn.sigmoid(jnp.dot(y.astype(BF16), wglu_ref[...], preferred_element_type=F32))
    y = _rms(y, og_ref[...])
    for j in range(SSM_TILES):
        tb_ref[j] = y[:, j * LANES:(j + 1) * LANES]
    for b in range(batch):
        for j in range(SSM_TILES):
            c0 = b * SSM_WIDTH + j * LANES
            out_ref[:, c0:c0 + LANES] = tb_ref[j, pl.ds(b, steps, stride=batch), :].astype(out_ref.dtype)


def _ssm(u_sb, bmat, cmat, a_re, a_im, d_skip, w_glu, ogain, batch, seq, steps):
    rows = steps * batch
    const2 = lambda c: (0, 0)
    const3 = lambda c: (0, 0, 0)
    return pl.pallas_call(
        functools.partial(_ssm_kernel, batch=batch, steps=steps),
        grid=(seq // steps,),
        in_specs=[pl.BlockSpec((steps, batch * SSM_WIDTH), lambda c: (c, 0)),
                  pl.BlockSpec(bmat.shape, const3), pl.BlockSpec(cmat.shape, const3),
                  pl.BlockSpec((1, SSM_STATES), const2), pl.BlockSpec((1, SSM_STATES), const2),
                  pl.BlockSpec((1, SSM_WIDTH), const2),
                  pl.BlockSpec((SSM_WIDTH, SSM_WIDTH), const2),
                  pl.BlockSpec((1, SSM_WIDTH), const2)],
        out_specs=pl.BlockSpec((steps, batch * SSM_WIDTH), lambda c: (c, 0)),
        out_shape=jax.ShapeDtypeStruct((seq, batch * SSM_WIDTH), BF16),
        scratch_shapes=[pltpu.VMEM((SSM_TILES, rows, LANES), F32),
                        pltpu.VMEM((rows, SSM_STATES), F32), pltpu.VMEM((rows, SSM_STATES), F32),
                        pltpu.VMEM((batch, SSM_STATES), F32), pltpu.VMEM((batch, SSM_STATES), F32)],
        compiler_params=_cparams("arbitrary"),
        name="s5_layer",
    )(u_sb, bmat, cmat, a_re, a_im, d_skip, w_glu, ogain)


def _outproj_kernel(ya_ref, ys_ref, yx_ref, x_ref, wo_ref, g_ref, wr_ref, x1_ref, lt_ref, hp_ref, *, sub):
    for r0 in range(0, x_ref.shape[0], sub):
        rows = slice(r0, r0 + sub)
        mix = jnp.concatenate([ya_ref[rows, :], ys_ref[rows, :], yx_ref[rows, :]], axis=-1)
        x1 = x_ref[rows, :] + jnp.dot(mix, wo_ref[...].astype(BF16), preferred_element_type=F32)
        x1_ref[rows, :] = x1
        h2 = _rms(x1, g_ref[...])
        hi = h2.astype(BF16)
        lt_ref[rows, :] = jnp.dot(hi, wr_ref[...], preferred_element_type=F32)
        hp_ref[rows, :] = _pack_rows(hi.astype(F32))


def _outproj(ya, ys_sb, yx, x2, w_o, gain, wr, batch, seq, tm):
    t, d = x2.shape
    nsb = seq // tm
    row = lambda w: pl.BlockSpec((tm, w), lambda i: (i, 0))
    const = lambda shape: pl.BlockSpec(shape, lambda i: (0, 0), pipeline_mode=pl.Buffered(1))
    return pl.pallas_call(
        functools.partial(_outproj_kernel, sub=min(MXU_WIDTH, tm)),
        grid=(t // tm,),
        in_specs=[row(ATTN_WIDTH),
                  pl.BlockSpec((tm, SSM_WIDTH), lambda i: (i % nsb, i // nsb)),
                  row(XATTN_WIDTH), row(d),
                  const(w_o.shape), const((1, d)), const(wr.shape)],
        out_specs=[row(d), row(LANES), row(d // 2)],
        out_shape=[jax.ShapeDtypeStruct((t, d), F32), jax.ShapeDtypeStruct((t, LANES), F32),
                   jax.ShapeDtypeStruct((t, d // 2), jnp.uint32)],
        compiler_params=_cparams("arbitrary"),
        name="out_proj_router",
    )(ya, ys_sb, yx, x2, w_o, gain, wr)


def _route_kernel(lt_ref, tri_ref, dest_ref, w_ref, meta_ref, cnt_ref, carry_ref, pstart_ref):
    phase = pl.program_id(0)
    c = pl.program_id(1)
    logits = lt_ref[...].T
    tc = logits.shape[1]
    ng, epg = N_EXPERT_GROUPS, EXPERTS_PER_GROUP
    row8 = lax.broadcasted_iota(jnp.int32, (ng, tc), 0)

    gl = logits[0:ng]
    gmax = jnp.max(gl, axis=0, keepdims=True)
    gidx = jnp.min(jnp.where(gl == gmax, row8, ng), axis=0, keepdims=True)
    gate = 1.0 / jnp.sum(jnp.exp(gl - gmax), axis=0, keepdims=True)
    sel = jnp.zeros((epg, tc), F32)
    for g in range(ng):
        sel = jnp.where(gidx == g, logits[ng + g * epg:ng + (g + 1) * epg], sel)
    v1 = jnp.max(sel, axis=0, keepdims=True)
    i1 = jnp.min(jnp.where(sel == v1, row8, epg), axis=0, keepdims=True)
    sel2 = jnp.where(row8 == i1, -jnp.inf, sel)
    v2 = jnp.max(sel2, axis=0, keepdims=True)
    i2 = jnp.min(jnp.where(sel2 == v2, row8, epg), axis=0, keepdims=True)
    e = jnp.exp(v2 - v1)
    w1 = gate * (1.0 / (1.0 + e))
    w2 = gate * (e / (1.0 + e))
    e1 = gidx * epg + i1
    e2 = gidx * epg + i2
    rowe = lax.broadcasted_iota(jnp.int32, (N_EXPERTS, tc), 0)
    oh1 = rowe == e1
    oh2 = rowe == e2
    member = jnp.where(jnp.logical_or(oh1, oh2), 1.0, 0.0)
    chunk_cnt = jnp.sum(member, axis=1, keepdims=True)

    @pl.when(phase == 0)
    def _():
        @pl.when(c == 0)
        def _():
            cnt_ref[...] = jnp.zeros_like(cnt_ref)
        cnt_ref[...] += chunk_cnt

    @pl.when(phase == 1)
    def _():
        @pl.when(c == 0)
        def _():
            cnt = cnt_ref[...]
            nblk = jnp.floor((cnt + (MOE_BLOCK - 1)) * (1.0 / MOE_BLOCK))
            nchunk = jnp.floor((nblk + (CHUNK_BLOCKS - 1)) * (1.0 / CHUNK_BLOCKS))
            r = lax.broadcasted_iota(jnp.int32, (N_EXPERTS, LANES), 0)
            cidx = lax.broadcasted_iota(jnp.int32, (N_EXPERTS, LANES), 1)
            to_row = lambda col: jnp.sum(jnp.where(r == cidx, col, 0.0), axis=0, keepdims=True)
            cumsum_col = lambda col: jnp.sum(jnp.where(cidx <= r, to_row(col), 0.0), axis=1, keepdims=True)
            cumsum_row = lambda col: jnp.sum(jnp.where(r <= cidx, col, 0.0), axis=0, keepdims=True)
            bend = cumsum_col(nblk)
            bstart = bend - nblk
            cend = cumsum_col(nchunk)
            cstart = cend - nchunk
            pstart_ref[...] = bstart * MOE_BLOCK
            carry_ref[...] = jnp.zeros_like(carry_ref)
            lanef = lax.broadcasted_iota(jnp.int32, (1, LANES), 1).astype(F32)
            owner = jnp.minimum(jnp.sum(jnp.where(cend <= lanef, 1.0, 0.0), axis=0, keepdims=True),
                                N_EXPERTS - 1.0)
            own = r.astype(F32) == owner
            pick = lambda col: jnp.sum(jnp.where(own, col, 0.0), axis=0, keepdims=True)
            idx = lanef - pick(cstart)
            first = pick(bstart) + CHUNK_BLOCKS * idx
            size = jnp.clip(pick(nblk) - CHUNK_BLOCKS * idx, 0.0, float(CHUNK_BLOCKS))
            zero = jnp.zeros((1, LANES), F32)
            rows = [owner, first, size,
                    zero + jnp.sum(nchunk, axis=0, keepdims=True),
                    cumsum_row(nblk) - 1.0,
                    to_row(nblk),
                    zero + jnp.sum(nblk, axis=0, keepdims=True),
                    pick(nchunk)]
            for k, v in enumerate(rows):
                meta_ref[k:k + 1, :] = v.astype(jnp.int32)

        before = carry_ref[...] + jnp.dot(member.astype(BF16), tri_ref[...], preferred_element_type=F32)
        pos = before + pstart_ref[...]
        dest_ref[0:1, :] = jnp.sum(jnp.where(oh1, pos, 0.0), axis=0, keepdims=True).astype(jnp.int32)
        dest_ref[1:2, :] = jnp.sum(jnp.where(oh2, pos, 0.0), axis=0, keepdims=True).astype(jnp.int32)
        w_ref[0:1, :] = w1
        w_ref[1:2, :] = w2
        carry_ref[...] += chunk_cnt


def _route(logits_t, tc):
    t = logits_t.shape[0]
    nc = t // tc
    tri = jnp.asarray(np.triu(np.ones((tc, tc), np.float32), k=1), dtype=BF16)
    return pl.pallas_call(
        _route_kernel,
        grid=(2, nc),
        in_specs=[pl.BlockSpec((tc, LANES), lambda p, c: (c, 0)),
                  pl.BlockSpec((tc, tc), lambda p, c: (0, 0))],
        out_specs=[pl.BlockSpec((2, tc), lambda p, c: (0, c * p)),
                   pl.BlockSpec((2, tc), lambda p, c: (0, c * p)),
                   pl.BlockSpec((META_ROWS, LANES), lambda p, c: (0, 0))],
        out_shape=[jax.ShapeDtypeStruct((2, t), jnp.int32), jax.ShapeDtypeStruct((2, t), F32),
                   jax.ShapeDtypeStruct((META_ROWS, LANES), jnp.int32)],
        scratch_shapes=[pltpu.VMEM((N_EXPERTS, 1), F32)] * 3,
        compiler_params=_cparams("arbitrary", "arbitrary"),
        name="moe_route",
    )(logits_t, tri)


def _meta(meta_ref, row, lane=0):
    return meta_ref[row * LANES + lane]


def _fill_blocks(meta_ref, zbuf, dst_ref, sem, n_blocks, *, expert_tails):
    zbuf[...] = jnp.zeros_like(zbuf)
    n_used = _meta(meta_ref, M_NUSED)
    block = lambda b: pltpu.make_async_copy(zbuf, dst_ref.at[pl.ds(b * MOE_BLOCK, MOE_BLOCK), :], sem)

    def tails(fn):
        def body(e, carry):
            @pl.when(_meta(meta_ref, M_NBLK, e) > 0)
            def _():
                fn(block(_meta(meta_ref, M_LAST, e)))
            return carry
        lax.fori_loop(0, N_EXPERTS, body, 0)

    def unused(fn):
        def body(b, carry):
            fn(block(b))
            return carry
        lax.fori_loop(n_used, n_blocks, body, 0)

    for phase in (lambda cp: cp.start(), lambda cp: cp.wait()):
        if expert_tails:
            tails(phase)
        unused(phase)


def _dispatch_kernel(dest_ref, meta_ref, h_ref, xs_ref, zbuf, sem_z, sem, *, tokens, n_blocks):
    tm = h_ref.shape[0]

    @pl.when(pl.program_id(0) == 0)
    def _():
        _fill_blocks(meta_ref, zbuf, xs_ref, sem_z, n_blocks, expert_tails=True)

    base = pl.program_id(0) * tm

    for r in range(tm):
        for k in range(2):
            d = dest_ref[k * tokens + base + r]
            pltpu.make_async_copy(h_ref.at[pl.ds(r, 1), :], xs_ref.at[pl.ds(d, 1), :], sem).start()
    for k in range(2):
        pltpu.make_async_copy(h_ref, xs_ref.at[pl.ds(0, tm), :], sem).wait()


def _dispatch(dest_flat, meta_flat, h2p, n_blocks, tm):
    t, w = h2p.shape
    grid_spec = pltpu.PrefetchScalarGridSpec(
        num_scalar_prefetch=2,
        grid=(t // tm,),
        in_specs=[pl.BlockSpec((tm, w), lambda i, d, m: (i, 0))],
        out_specs=pl.BlockSpec(memory_space=pl.ANY),
        scratch_shapes=[pltpu.VMEM((MOE_BLOCK, w), h2p.dtype),
                        pltpu.SemaphoreType.DMA(()), pltpu.SemaphoreType.DMA(())],
    )
    return pl.pallas_call(
        functools.partial(_dispatch_kernel, tokens=t, n_blocks=n_blocks),
        grid_spec=grid_spec,
        out_shape=jax.ShapeDtypeStruct((n_blocks * MOE_BLOCK, w), h2p.dtype),
        compiler_params=_cparams("arbitrary"),
        name="moe_dispatch",
    )(dest_flat, meta_flat, h2p)


def _pack_rows(x):
    bits = lax.bitcast_convert_type(x, jnp.uint32)
    half = x.shape[1] // 2
    return (bits[:, half:] & jnp.uint32(0xFFFF0000)) | (bits[:, :half] >> 16)


def _unpack_halves(words):
    lo = lax.bitcast_convert_type(words << 16, F32)
    hi = lax.bitcast_convert_type(words & jnp.uint32(0xFFFF0000), F32)
    return lo, hi


def _unpack_rows(words):
    return jnp.concatenate(_unpack_halves(words), axis=-1).astype(BF16)


def _expert_kernel(meta_ref, xs_ref, wg_ref, wu_ref, wd_ref, yb_ref,
                   xbuf, ybuf, zbuf, wg_f32, wu_f32, wd_f32, wslot_ref,
                   sem_in, sem_out, sem_z, sem_w, *, n_blocks):
    n_chunks = _meta(meta_ref, M_NCHUNK)

    def weight_copies(k, s):
        e = _meta(meta_ref, M_OWNER, k)
        return [pltpu.make_async_copy(src.at[e], dst.at[s], sem_w.at[s])
                for src, dst in ((wg_ref, wg_f32), (wu_ref, wu_f32), (wd_ref, wd_f32))]

    def in_copy(k, s, nb):
        rows = nb * MOE_BLOCK
        src = xs_ref.at[pl.ds(_meta(meta_ref, M_FIRST, k) * MOE_BLOCK, rows), :]
        return pltpu.make_async_copy(src, xbuf.at[s, pl.ds(0, rows), :], sem_in.at[s])

    def out_copy(k, s, nb):
        rows = nb * MOE_BLOCK
        dst = yb_ref.at[pl.ds(_meta(meta_ref, M_FIRST, k) * MOE_BLOCK, rows), :]
        return pltpu.make_async_copy(ybuf.at[s, pl.ds(0, rows), :], dst, sem_out.at[s])

    def by_size(k, fn):
        for nb in range(1, CHUNK_BLOCKS + 1):
            pl.when(_meta(meta_ref, M_SIZE, k) == nb)(functools.partial(fn, nb))

    wslot_ref[0] = 0
    by_size(0, lambda nb: in_copy(0, 0, nb).start())
    for cp in weight_copies(0, 0):
        cp.start()

    def chunk_step(c, carry):
        slot = c % 2

        @pl.when(c + 1 < n_chunks)
        def _():
            by_size(c + 1, lambda nb: in_copy(c + 1, 1 - slot, nb).start())

        prev = jnp.maximum(c - 1, 0)
        new_expert = jnp.logical_or(c == 0, _meta(meta_ref, M_OWNER, c) != _meta(meta_ref, M_OWNER, prev))

        @pl.when(new_expert)
        def _():
            ws = jnp.where(c == 0, 0, 1 - wslot_ref[0])
            wslot_ref[0] = ws
            nxt = c + _meta(meta_ref, M_OWNER_CHUNKS, c)

            @pl.when(nxt < n_chunks)
            def _():
                for cp in weight_copies(nxt, 1 - ws):
                    cp.start()
            for cp in weight_copies(c, ws):
                cp.wait()

        @pl.when(c >= 2)
        def _():
            by_size(c - 2, lambda nb: out_copy(c - 2, slot, nb).wait())

        def compute(nb):
            rows = nb * MOE_BLOCK
            ws = wslot_ref[0]
            in_copy(c, slot, nb).wait()
            h = _unpack_rows(xbuf[slot, 0:rows, :])
            gate = jnp.dot(h, wg_f32[ws].astype(BF16), preferred_element_type=F32)
            up = jnp.dot(h, wu_f32[ws].astype(BF16), preferred_element_type=F32)
            act = (jax.nn.silu(gate) * up).astype(BF16)
            y = jnp.dot(act, wd_f32[ws].astype(BF16), preferred_element_type=F32)
            ybuf[slot, 0:rows, :] = _pack_rows(y.astype(BF16).astype(F32))
            out_copy(c, slot, nb).start()
        by_size(c, compute)
        return carry

    lax.fori_loop(0, n_chunks, chunk_step, 0)

    _fill_blocks(meta_ref, zbuf, yb_ref, sem_z, n_blocks, expert_tails=False)
    for back in (2, 1):
        @pl.when(n_chunks >= back)
        def _(back=back):
            k = n_chunks - back
            by_size(k, lambda nb: out_copy(k, k % 2, nb).wait())


def _experts(meta_flat, xs, w_gate, w_up, w_down, n_blocks):
    d, de = w_gate.shape[1], w_gate.shape[2]
    rows = CHUNK_BLOCKS * MOE_BLOCK
    assert (n_blocks + (CHUNK_BLOCKS - 1) * N_EXPERTS) // CHUNK_BLOCKS <= LANES

    hbm = pl.BlockSpec(memory_space=pl.ANY)
    grid_spec = pltpu.PrefetchScalarGridSpec(
        num_scalar_prefetch=1,
        grid=(1,),
        in_specs=[hbm, hbm, hbm, hbm],
        out_specs=hbm,
        scratch_shapes=[pltpu.VMEM((2, rows, xs.shape[1]), xs.dtype),
                        pltpu.VMEM((2, rows, d // 2), jnp.uint32),
                        pltpu.VMEM((MOE_BLOCK, d // 2), jnp.uint32),
                        pltpu.VMEM((2, d, de), F32), pltpu.VMEM((2, d, de), F32), pltpu.VMEM((2, de, d), F32),
                        pltpu.SMEM((1,), jnp.int32),
                        pltpu.SemaphoreType.DMA((2,)), pltpu.SemaphoreType.DMA((2,)),
                        pltpu.SemaphoreType.DMA(()), pltpu.SemaphoreType.DMA((2,))],
    )
    return pl.pallas_call(
        functools.partial(_expert_kernel, n_blocks=n_blocks),
        grid_spec=grid_spec,
        out_shape=jax.ShapeDtypeStruct((n_blocks * MOE_BLOCK, d // 2), jnp.uint32),
        compiler_params=_cparams("arbitrary"),
        name="moe_experts",
    )(meta_flat, xs, w_gate, w_up, w_down)


def _combine_kernel(dest_ref, yb_ref, x1_ref, w_ref, out_ref, gbuf, sem, *, tokens):
    i = pl.program_id(0)
    n = pl.num_programs(0)
    tm = x1_ref.shape[0]
    half = x1_ref.shape[1] // 2
    slot = i % COMBINE_SLOTS
    ahead = COMBINE_SLOTS - 1

    def issue_row(tile, r):
        s = tile % COMBINE_SLOTS
        for k in range(2):
            d = dest_ref[k * tokens + tile * tm + r]
            pltpu.make_async_copy(yb_ref.at[pl.ds(d, 1), :], gbuf.at[s, k, pl.ds(r, 1), :], sem.at[s]).start()

    def combine_rows(r0):
        rows = slice(r0, r0 + COMBINE_PIECE)
        w = w_ref[rows, :]
        lo0, hi0 = _unpack_halves(gbuf[slot, 0, rows, :])
        lo1, hi1 = _unpack_halves(gbuf[slot, 1, rows, :])
        out_ref[rows, :half] = x1_ref[rows, :half] + (lo0 * w[:, 0:1] + lo1 * w[:, 1:2])
        out_ref[rows, half:] = x1_ref[rows, half:] + (hi0 * w[:, 0:1] + hi1 * w[:, 1:2])

    @pl.when(i == 0)
    def _():
        for first in range(ahead):
            @pl.when(first < n)
            def _(first=first):
                def body(g, carry):
                    r8 = pl.multiple_of(g * SUBLANES, SUBLANES)
                    for sub in range(SUBLANES):
                        issue_row(first, r8 + sub)
                    return carry
                lax.fori_loop(0, tm // SUBLANES, body, 0)

    for k in range(2):
        pltpu.make_async_copy(yb_ref.at[pl.ds(0, tm), :], gbuf.at[slot, k], sem.at[slot]).wait()

    @pl.when(i + ahead < n)
    def _():
        for r0 in range(0, tm, COMBINE_PIECE):
            for r in range(r0, r0 + COMBINE_PIECE):
                issue_row(i + ahead, r)
            combine_rows(r0)

    @pl.when(i + ahead >= n)
    def _():
        for r0 in range(0, tm, COMBINE_PIECE):
            combine_rows(r0)


def _combine(dest_flat, yb, x1, w_tok, tm):
    t, d = x1.shape
    grid_spec = pltpu.PrefetchScalarGridSpec(
        num_scalar_prefetch=1,
        grid=(t // tm,),
        in_specs=[pl.BlockSpec(memory_space=pl.ANY),
                  pl.BlockSpec((tm, d), lambda i, dr: (i, 0)),
                  pl.BlockSpec((tm, 2), lambda i, dr: (i, 0))],
        out_specs=pl.BlockSpec((tm, d), lambda i, dr: (i, 0)),
        scratch_shapes=[pltpu.VMEM((COMBINE_SLOTS, 2, tm, d // 2), jnp.uint32),
                        pltpu.SemaphoreType.DMA((COMBINE_SLOTS,))],
    )
    return pl.pallas_call(
        functools.partial(_combine_kernel, tokens=t),
        grid_spec=grid_spec,
        out_shape=jax.ShapeDtypeStruct((t, d), F32),
        compiler_params=_cparams("arbitrary"),
        name="moe_combine",
    )(dest_flat, yb, x1, w_tok)


def _row(v):
    return v.astype(F32).reshape(1, -1)


def _layer(x2, mem2, batch, seq, mem_len, p):
    t, d = x2.shape

    bias = _bias_table(p["rel_bias"])
    col = lambda v: jnp.broadcast_to(v.astype(F32)[:, None], (v.shape[0], BLOCK))
    qgain = col(p["q_norm"]) * (1.0 / math.sqrt(ATTN_HEAD_DIM))
    kgain = jnp.tile(_row(p["k_norm"]), (1, ATTN_KV_HEADS))
    u_sb, qx, ya = _inproj_swa(x2, _row(p["norm_mix"]), p["w_in"].astype(F32), bias, p["attn_sinks"].astype(F32),
                               qgain, kgain, col(p["out_norm_attn"]), batch, seq, min(TOKEN_TILE, seq))

    km, vm = _memkv(mem2, _row(p["mem_norm"]), p["w_mem_kv"].astype(F32), _row(p["xk_norm"]),
                    min(MEM_TILE, mem2.shape[0]))
    xq_gain = _row(p["xq_norm"]) * (1.0 / math.sqrt(XATTN_HEAD_DIM))
    yx = _xattn(qx, km, vm, xq_gain, _row(p["out_norm_xattn"]), batch, seq, mem_len, min(TOKEN_TILE, seq))

    a_re, a_im, bbr, bbi = _ssm_prep(p["ssm_lambda_re"], p["ssm_lambda_im"], p["ssm_log_dt"],
                                     p["ssm_b_re"], p["ssm_b_im"])
    bmat = jnp.concatenate([_block_diag_tiles(bbr), _block_diag_tiles(bbi)], axis=-1).astype(BF16)
    c_re_t = jnp.transpose(p["ssm_c_re"].astype(F32), (0, 2, 1))
    c_im_t = jnp.transpose(p["ssm_c_im"].astype(F32), (0, 2, 1))
    cmat = jnp.concatenate([_block_diag_tiles(c_re_t), _block_diag_tiles(-c_im_t)], axis=1).astype(BF16)
    steps = min(SSM_STEPS, seq)
    ys_sb = _ssm(u_sb, bmat, cmat,
                 a_re.reshape(1, SSM_STATES), a_im.reshape(1, SSM_STATES), _row(p["ssm_d"]),
                 p["ssm_w_glu"].astype(BF16), _row(p["out_norm_ssm"]), batch, seq, steps)

    wr = jnp.concatenate([p["w_router_group"], p["w_router_expert"]], axis=1).astype(F32)
    wr = jnp.pad(wr, ((0, 0), (0, LANES - wr.shape[1]))).astype(BF16)
    tm_out = min(TOKEN_TILE, seq)
    x1, logits_t, h2p = _outproj(ya, ys_sb, yx, x2, p["w_o"].astype(F32), _row(p["norm_ffn"]), wr,
                                 batch, seq, tm_out)

    dest, w_k, meta = _route(logits_t, min(ROUTE_CHUNK, t))
    n_blocks = (2 * t) // MOE_BLOCK + N_EXPERTS
    dest_flat = dest.reshape(2 * t)
    meta_flat = meta.reshape(META_ROWS * LANES)
    xs = _dispatch(dest_flat, meta_flat, h2p, n_blocks, min(DISPATCH_TILE, t))
    yb = _experts(meta_flat, xs, p["w_gate"], p["w_up"], p["w_down"], n_blocks)
    return _combine(dest_flat, yb, x1, w_k.T, min(TOKEN_TILE, t))


def kernel(x, mem, norm_mix, w_in, q_norm, k_norm, attn_sinks, rel_bias, ssm_lambda_re, ssm_lambda_im, ssm_log_dt, ssm_b_re, ssm_b_im, ssm_c_re, ssm_c_im, ssm_d, ssm_w_glu, mem_norm, w_mem_kv, xq_norm, xk_norm, out_norm_attn, out_norm_ssm, out_norm_xattn, w_o, norm_ffn, w_router_group, w_router_expert, w_gate, w_up, w_down):
    batch, seq, d = x.shape
    mem_len = mem.shape[1]
    per_layer = dict(norm_mix=norm_mix, w_in=w_in, q_norm=q_norm, k_norm=k_norm, attn_sinks=attn_sinks,
                     ssm_lambda_re=ssm_lambda_re, ssm_lambda_im=ssm_lambda_im, ssm_log_dt=ssm_log_dt,
                     ssm_b_re=ssm_b_re, ssm_b_im=ssm_b_im, ssm_c_re=ssm_c_re, ssm_c_im=ssm_c_im,
                     ssm_d=ssm_d, ssm_w_glu=ssm_w_glu, mem_norm=mem_norm, w_mem_kv=w_mem_kv,
                     xq_norm=xq_norm, xk_norm=xk_norm, out_norm_attn=out_norm_attn,
                     out_norm_ssm=out_norm_ssm, out_norm_xattn=out_norm_xattn, w_o=w_o, norm_ffn=norm_ffn,
                     w_router_group=w_router_group, w_router_expert=w_router_expert,
                     w_gate=w_gate, w_up=w_up, w_down=w_down)
    x2 = x.astype(F32).reshape(batch * seq, d)
    mem2 = mem.astype(F32).reshape(batch * mem_len, d)
    for l in range(norm_mix.shape[0]):
        p = {k: v[l] for k, v in per_layer.items()}
        p["rel_bias"] = rel_bias
        x2 = _layer(x2, mem2, batch, seq, mem_len, p)
    return x2.reshape(batch, seq, d).astype(x.dtype)
```

```python
import functools
import math

import numpy as np
import jax
import jax.numpy as jnp
from jax import lax
from jax.experimental import pallas as pl
from jax.experimental.pallas import tpu as pltpu

F32 = jnp.float32
BF16 = jnp.bfloat16
EPS = 1e-6

ATTN_HEADS = 16
ATTN_KV_HEADS = 2
ATTN_HEAD_DIM = 64
ATTN_WIDTH = ATTN_HEADS * ATTN_HEAD_DIM
WINDOW = 128
BLOCK = 128
REL_BUCKETS = 32
REL_MAX_DIST = 128
SSM_GROUP_CH = 16
SSM_GROUPS = 32
SSM_STATE = 64
SSM_WIDTH = SSM_GROUPS * SSM_GROUP_CH
XATTN_HEADS = 4
XATTN_HEAD_DIM = 128
XATTN_WIDTH = XATTN_HEADS * XATTN_HEAD_DIM
N_EXPERT_GROUPS = 8
EXPERTS_PER_GROUP = 8
N_EXPERTS = N_EXPERT_GROUPS * EXPERTS_PER_GROUP
MOE_BLOCK = 128

LANES = 128
SUBLANES = 8
MXU_WIDTH = 256
TOKEN_TILE = 512
MEM_TILE = 256
SSM_STEPS = 128
ROUTE_CHUNK = 2048
DISPATCH_TILE = 1024
SSM_GROUPS_PER_TILE = LANES // SSM_GROUP_CH
SSM_TILES = SSM_WIDTH // LANES
SSM_TILE_STATE = SSM_GROUPS_PER_TILE * SSM_STATE
SSM_STATES = SSM_GROUPS * SSM_STATE
SSM_PIECES = 8
CHUNK_BLOCKS = 4
COMBINE_SLOTS = 3
COMBINE_PIECE = 32
META_ROWS = 8
M_OWNER, M_FIRST, M_SIZE, M_NCHUNK, M_LAST, M_NBLK, M_NUSED, M_OWNER_CHUNKS = range(8)
VMEM_LIMIT = 56 * 1024 * 1024

_NT = (((1,), (1,)), ((), ()))


def _cparams(*sem):
    return pltpu.CompilerParams(dimension_semantics=sem, vmem_limit_bytes=VMEM_LIMIT)


def _rms(x, gain):
    ms = jnp.mean(x * x, axis=-1, keepdims=True)
    return x * lax.rsqrt(ms + EPS) * gain


def _t5_bucket_table():
    qi = np.arange(BLOCK, dtype=np.int32)[:, None]
    ki = np.arange(2 * BLOCK, dtype=np.int32)[None, :]
    delta = BLOCK + qi - ki
    n = np.maximum(delta, 0)
    max_exact = REL_BUCKETS // 2
    nf = np.maximum(n, 1).astype(np.float32)
    large = max_exact + (np.log(nf / np.float32(max_exact)) / np.float32(math.log(REL_MAX_DIST / max_exact))
                         * np.float32(REL_BUCKETS - max_exact)).astype(np.int32)
    large = np.minimum(large, REL_BUCKETS - 1)
    return np.where(n < max_exact, n, large).astype(np.int32)


def _upper(cols):
    k = lax.broadcasted_iota(jnp.int32, (BLOCK, cols), 0)
    q = lax.broadcasted_iota(jnp.int32, (BLOCK, cols), 1) % BLOCK
    return k > q


def _bias_kernel(rb_ref, bucket_ref, out_ref):
    pair = pl.program_id(0)
    bucket = bucket_ref[...]
    upper = _upper(BLOCK)
    for half in range(2):
        acc = jnp.zeros(bucket.shape, F32)
        for b in range(REL_BUCKETS):
            acc = jnp.where(bucket == b, rb_ref[b, 2 * pair + half], acc)
        cols = slice(half * BLOCK, (half + 1) * BLOCK)
        out_ref[0, 0, :, cols] = jnp.where(upper, acc[:BLOCK], acc[BLOCK:])
        out_ref[1, 0, :, cols] = jnp.where(upper, jnp.float32(-1e30), acc[BLOCK:])


def _bias_table(rel_bias):
    assert WINDOW == BLOCK
    bucket = jnp.asarray(_t5_bucket_table().T)
    pairs = ATTN_HEADS // 2
    return pl.pallas_call(
        _bias_kernel,
        grid=(pairs,),
        in_specs=[pl.BlockSpec(memory_space=pltpu.SMEM),
                  pl.BlockSpec((2 * BLOCK, BLOCK), lambda h: (0, 0))],
        out_specs=pl.BlockSpec((2, 1, BLOCK, 2 * BLOCK), lambda h: (0, h, 0, 0)),
        out_shape=jax.ShapeDtypeStruct((2, pairs, BLOCK, 2 * BLOCK), F32),
        compiler_params=_cparams("arbitrary"),
        name="t5_bias_table",
    )(rel_bias.astype(F32), bucket)


def _ssm_prep_kernel(lr_ref, li_ref, ldt_ref, br_ref, bi_ref, are_ref, aim_ref, bbr_ref, bbi_ref):
    lr = lr_ref[...]
    li = li_ref[...]
    dt = jnp.exp(ldt_ref[...])
    mag = jnp.exp(lr * dt)
    a_re = mag * jnp.cos(li * dt)
    a_im = mag * jnp.sin(li * dt)
    den = lr * lr + li * li
    nr = a_re - 1.0
    ni = a_im
    coef_re = (nr * lr + ni * li) / den
    coef_im = (ni * lr - nr * li) / den
    are_ref[...] = a_re
    aim_ref[...] = a_im
    br = br_ref[...]
    bi = bi_ref[...]
    bbr_ref[...] = coef_re * br - coef_im * bi
    bbi_ref[...] = coef_re * bi + coef_im * br


def _ssm_prep(lam_re, lam_im, log_dt, b_re, b_im):
    g, n, c = b_re.shape
    vec = jax.ShapeDtypeStruct((g, 1, n), F32)
    mat = jax.ShapeDtypeStruct((g, c, n), F32)
    return pl.pallas_call(
        _ssm_prep_kernel,
        out_shape=(vec, vec, mat, mat),
        name="ssm_discretise",
    )(lam_re.astype(F32).reshape(g, 1, n), lam_im.astype(F32).reshape(g, 1, n),
      log_dt.astype(F32).reshape(g, 1, 1),
      jnp.transpose(b_re.astype(F32), (0, 2, 1)), jnp.transpose(b_im.astype(F32), (0, 2, 1)))


def _block_diag_tiles(m):
    g, r, c = m.shape
    t = g // SSM_GROUPS_PER_TILE
    eye = jnp.eye(SSM_GROUPS_PER_TILE, dtype=m.dtype)
    m4 = m.reshape(t, SSM_GROUPS_PER_TILE, r, c)
    out = m4[:, :, :, None, :] * eye[None, :, None, :, None]
    return out.reshape(t, SSM_GROUPS_PER_TILE * r, SSM_GROUPS_PER_TILE * c)


def _k_norm(k, kgain):
    dh = ATTN_HEAD_DIM
    lo = lax.broadcasted_iota(jnp.int32, (1, ATTN_KV_HEADS * dh), 1) < dh
    sq = k * k
    s_lo = jnp.sum(jnp.where(lo, sq, 0.0), axis=-1, keepdims=True)
    s_hi = jnp.sum(sq, axis=-1, keepdims=True) - s_lo
    return k * jnp.where(lo, lax.rsqrt(s_lo / dh + EPS), lax.rsqrt(s_hi / dh + EPS)) * kgain


def _swa_phases(q_of, kn, v_t, bias_of, sinks_ref, qgain, ogain, emit):
    dh = ATTN_HEAD_DIM
    upper = _upper(2 * BLOCK)
    first_head = lax.broadcasted_iota(jnp.int32, (1, 2 * BLOCK), 1) < BLOCK
    zeros = jnp.zeros((dh, BLOCK), F32)
    heads_per_kv = ATTN_HEADS // ATTN_KV_HEADS
    pairs = range(ATTN_HEADS // 2)
    kv_of = lambda pair: (2 * pair) // heads_per_kv
    state = {}

    def qk_all():
        logits = []
        for pair in pairs:
            cols = []
            for half in range(2):
                qh = q_of(2 * pair + half).astype(F32)
                ms = jnp.mean(qh * qh, axis=0, keepdims=True)
                qn = qh * lax.rsqrt(ms + EPS) * qgain
                cols.append(jnp.concatenate([qn, zeros] if kv_of(pair) == 0 else [zeros, qn], axis=0))
            rhs = jnp.concatenate(cols, axis=1).astype(BF16)
            logits.append(jnp.dot(kn, rhs, preferred_element_type=F32))
        state["logits"] = logits

    def softmax_all():
        probs = []
        for pair in pairs:
            both = state["logits"][pair]
            l = jnp.where(upper, both[:BLOCK], both[BLOCK:]) + bias_of(pair)
            sink = jnp.where(first_head, sinks_ref[2 * pair], sinks_ref[2 * pair + 1])
            m = jnp.maximum(jnp.max(l, axis=0, keepdims=True), sink)
            p = jnp.exp(l - m)
            den = jnp.sum(p, axis=0, keepdims=True) + jnp.exp(sink - m)
            pz = jnp.concatenate([jnp.where(upper, p, 0.0), jnp.where(upper, 0.0, p)], axis=0).astype(BF16)
            probs.append((pz, den))
        state["probs"] = probs

    def pv_all():
        outs = []
        for pair in pairs:
            g = kv_of(pair)
            pz, den = state["probs"][pair]
            o = jnp.dot(v_t[g * dh:(g + 1) * dh, :], pz, preferred_element_type=F32) / den
            outs += [o[:, :BLOCK], o[:, BLOCK:]]
        y_t = jnp.concatenate(outs, axis=0)
        ms = jnp.mean(y_t * y_t, axis=0, keepdims=True)
        emit((y_t * lax.rsqrt(ms + EPS) * ogain).T)

    return qk_all, softmax_all, pv_all


def _inproj_swa_kernel(sinks_ref, x_ref, g_ref, w_ref, bias_ref, qg_ref, kg_ref, og_ref,
                       u_ref, qx_ref, ya_ref, q_s, k_s, v_s, kprev_s, vprev_s, *, tiles_per_seq):
    i = pl.program_id(0)
    slot = i % 2
    old = 1 - slot
    tm = x_ref.shape[0]
    blocks = tm // BLOCK
    kvw = ATTN_KV_HEADS * ATTN_HEAD_DIM

    @pl.when(i == 0)
    def _():
        q_s[1] = jnp.zeros(q_s.shape[1:], q_s.dtype)
        k_s[1] = jnp.zeros(k_s.shape[1:], k_s.dtype)
        v_s[1] = jnp.zeros(v_s.shape[1:], v_s.dtype)
        kprev_s[...] = jnp.zeros_like(kprev_s)
        vprev_s[...] = jnp.zeros_like(vprev_s)

    h = _rms(x_ref[...], g_ref[...]).astype(BF16)
    proj = lambda c0, c1: jnp.dot(h, w_ref[:, c0:c1].astype(BF16), preferred_element_type=F32)
    piece = MXU_WIDTH

    def q_piece(n):
        def run():
            q_s[slot, n * piece:(n + 1) * piece, :] = proj(n * piece, (n + 1) * piece).T.astype(q_s.dtype)
        return run

    def kv_piece():
        kv = proj(ATTN_WIDTH, ATTN_WIDTH + 2 * kvw)
        k_s[slot] = _k_norm(kv[:, :kvw], kg_ref[...]).astype(k_s.dtype)
        v_s[slot] = kv[:, kvw:].T.astype(v_s.dtype)

    def out_piece(ref, c0, n):
        def run():
            ref[:, n * piece:(n + 1) * piece] = proj(c0 + n * piece, c0 + (n + 1) * piece).astype(ref.dtype)
        return run

    c_u = ATTN_WIDTH + 2 * kvw
    c_qx = c_u + SSM_WIDTH
    projection = ([q_piece(n) for n in range(ATTN_WIDTH // piece)] + [kv_piece]
                  + [out_piece(u_ref, c_u, n) for n in range(SSM_WIDTH // piece)]
                  + [out_piece(qx_ref, c_qx, n) for n in range(XATTN_WIDTH // piece)])

    first_of_seq = (i + tiles_per_seq - 1) % tiles_per_seq == 0
    attention = []
    for blk in range(blocks):
        own = slice(blk * BLOCK, (blk + 1) * BLOCK)
        before = slice((blk - 1) * BLOCK, blk * BLOCK)
        k_before = kprev_s[...] if blk == 0 else k_s[old, before, :]
        v_before = vprev_s[...] if blk == 0 else v_s[old, :, before]
        table = jnp.where(first_of_seq, 1, 0) if blk == 0 else 0

        def emit(y, own=own):
            ya_ref[own, :] = y.astype(ya_ref.dtype)

        attention += _swa_phases(
            q_of=lambda hd, own=own: q_s[old, hd * ATTN_HEAD_DIM:(hd + 1) * ATTN_HEAD_DIM, own],
            kn=jnp.concatenate([k_before, k_s[old, own, :]], axis=0),
            v_t=jnp.concatenate([v_before, v_s[old, :, own]], axis=1),
            bias_of=lambda pair, table=table: bias_ref[table, pair],
            sinks_ref=sinks_ref, qgain=qg_ref[...], ogain=og_ref[...], emit=emit)

    longer, shorter = (attention, projection) if len(attention) >= len(projection) else (projection, attention)
    done = 0
    for n, fn in enumerate(longer):
        fn()
        due = (n + 1) * len(shorter) // len(longer)
        for extra in shorter[done:due]:
            extra()
        done = due

    last = slice((blocks - 1) * BLOCK, blocks * BLOCK)
    kprev_s[...] = k_s[old, last, :]
    vprev_s[...] = v_s[old, :, last]


def _inproj_swa(x2, gain, w_in, bias, sinks, qgain, kgain, ogain, batch, seq, tm):
    t, d = x2.shape
    n_tiles = t // tm
    tiles_per_seq = seq // tm
    kvw = ATTN_KV_HEADS * ATTN_HEAD_DIM
    proj_tile = lambda i: jnp.minimum(i, n_tiles - 1)
    attn_tile = lambda i: jnp.maximum(i - 1, 0)
    const = lambda shape: pl.BlockSpec(shape, lambda i: (0,) * len(shape), pipeline_mode=pl.Buffered(1))
    return pl.pallas_call(
        functools.partial(_inproj_swa_kernel, tiles_per_seq=tiles_per_seq),
        grid=(n_tiles + 1,),
        in_specs=[pl.BlockSpec(memory_space=pltpu.SMEM),
                  pl.BlockSpec((tm, d), lambda i: (proj_tile(i), 0)),
                  const((1, d)), const(w_in.shape), const(bias.shape),
                  const((ATTN_HEAD_DIM, BLOCK)), const((1, kvw)), const((ATTN_WIDTH, BLOCK))],
        out_specs=[
                   pl.BlockSpec((tm, SSM_WIDTH),
                                lambda i: (proj_tile(i) % tiles_per_seq, proj_tile(i) // tiles_per_seq)),
                   pl.BlockSpec((tm, XATTN_WIDTH), lambda i: (proj_tile(i), 0)),
                   pl.BlockSpec((tm, ATTN_WIDTH), lambda i: (attn_tile(i), 0))],
        out_shape=[jax.ShapeDtypeStruct((seq, batch * SSM_WIDTH), BF16),
                   jax.ShapeDtypeStruct((t, XATTN_WIDTH), BF16),
                   jax.ShapeDtypeStruct((t, ATTN_WIDTH), BF16)],
        scratch_shapes=[pltpu.VMEM((2, ATTN_WIDTH, tm), BF16),
                        pltpu.VMEM((2, tm, kvw), BF16),
                        pltpu.VMEM((2, kvw, tm), BF16),
                        pltpu.VMEM((BLOCK, kvw), BF16), pltpu.VMEM((kvw, BLOCK), BF16)],
        compiler_params=_cparams("arbitrary"),
        name="in_proj_swa",
    )(sinks, x2, gain, w_in, bias, qgain, kgain, ogain)


def _memkv_kernel(m_ref, g_ref, w_ref, kg_ref, k_ref, v_ref):
    h = _rms(m_ref[...], g_ref[...]).astype(BF16)
    km = jnp.dot(h, w_ref[:, :XATTN_WIDTH].astype(BF16), preferred_element_type=F32)
    for hd in range(XATTN_HEADS):
        sl = slice(hd * XATTN_HEAD_DIM, (hd + 1) * XATTN_HEAD_DIM)
        k_ref[:, sl] = _rms(km[:, sl], kg_ref[...]).astype(k_ref.dtype)
    v_ref[...] = jnp.dot(h, w_ref[:, XATTN_WIDTH:].astype(BF16), preferred_element_type=F32).astype(v_ref.dtype)


def _memkv(mem2, gain, w_kv, kgain, tm):
    r, d = mem2.shape
    row = lambda w: pl.BlockSpec((tm, w), lambda i: (i, 0))
    const = lambda shape: pl.BlockSpec(shape, lambda i: (0, 0), pipeline_mode=pl.Buffered(1))
    return pl.pallas_call(
        _memkv_kernel,
        grid=(r // tm,),
        in_specs=[row(d), const((1, d)), const((d, 2 * XATTN_WIDTH)), const((1, XATTN_HEAD_DIM))],
        out_specs=[row(XATTN_WIDTH), row(XATTN_WIDTH)],
        out_shape=[jax.ShapeDtypeStruct((r, XATTN_WIDTH), BF16)] * 2,
        compiler_params=_cparams("arbitrary"),
        name="mem_kv_proj",
    )(mem2, gain, w_kv, kgain)


def _xattn_kernel(q_ref, k_ref, v_ref, qg_ref, og_ref, out_ref):
    outs = []
    for hd in range(XATTN_HEADS):
        sl = slice(hd * XATTN_HEAD_DIM, (hd + 1) * XATTN_HEAD_DIM)
        qn = _rms(q_ref[:, sl].astype(F32), qg_ref[...]).astype(BF16)
        l = lax.dot_general(qn, k_ref[:, sl], _NT, preferred_element_type=F32)
        m = jnp.max(l, axis=-1, keepdims=True)
        p = jnp.exp(l - m)
        den = jnp.sum(p, axis=-1, keepdims=True)
        outs.append(jnp.dot(p.astype(BF16), v_ref[:, sl], preferred_element_type=F32) / den)
    y = jnp.concatenate(outs, axis=-1)
    out_ref[...] = _rms(y, og_ref[...]).astype(out_ref.dtype)


def _xattn(qx, km, vm, qgain, ogain, batch, seq, mem_len, tq):
    nq = seq // tq
    const2 = lambda b, i: (0, 0)
    return pl.pallas_call(
        _xattn_kernel,
        grid=(batch, nq),
        in_specs=[pl.BlockSpec((tq, XATTN_WIDTH), lambda b, i: (b * nq + i, 0)),
                  pl.BlockSpec((mem_len, XATTN_WIDTH), lambda b, i: (b, 0)),
                  pl.BlockSpec((mem_len, XATTN_WIDTH), lambda b, i: (b, 0)),
                  pl.BlockSpec((1, XATTN_HEAD_DIM), const2),
                  pl.BlockSpec((1, XATTN_WIDTH), const2)],
        out_specs=pl.BlockSpec((tq, XATTN_WIDTH), lambda b, i: (b * nq + i, 0)),
        out_shape=jax.ShapeDtypeStruct((batch * seq, XATTN_WIDTH), BF16),
        compiler_params=_cparams("arbitrary", "arbitrary"),
        name="mem_xattn",
    )(qx, km, vm, qgain, ogain)


def _ssm_kernel(u_ref, bmat_ref, cmat_ref, are_ref, aim_ref, d_ref, wglu_ref, og_ref, out_ref,
                tb_ref, xr_ref, xi_ref, sr_ref, si_ref, *, batch, steps):
    @pl.when(pl.program_id(0) == 0)
    def _():
        sr_ref[...] = jnp.zeros_like(sr_ref)
        si_ref[...] = jnp.zeros_like(si_ref)

    for b in range(batch):
        for j in range(SSM_TILES):
            c0 = b * SSM_WIDTH + j * LANES
            tb_ref[j, pl.ds(b, steps, stride=batch), :] = u_ref[:, c0:c0 + LANES].astype(F32)
    uf = jnp.concatenate([tb_ref[j] for j in range(SSM_TILES)], axis=-1)
    u = uf.astype(BF16)
    total = steps * batch
    piece = total // SSM_PIECES
    tiles_per_half = SSM_TILES // 2
    half_states = SSM_STATES // 2

    def bu_piece(j, k):
        rows = slice(k * piece, (k + 1) * piece)
        bu = jnp.dot(u[rows, j * LANES:(j + 1) * LANES], bmat_ref[j], preferred_element_type=F32)
        xr_ref[rows, j * SSM_TILE_STATE:(j + 1) * SSM_TILE_STATE] = bu[:, :SSM_TILE_STATE]
        xi_ref[rows, j * SSM_TILE_STATE:(j + 1) * SSM_TILE_STATE] = bu[:, SSM_TILE_STATE:]

    def c_piece(j, k):
        rows = slice(k * piece, (k + 1) * piece)
        sl = slice(j * SSM_TILE_STATE, (j + 1) * SSM_TILE_STATE)
        xcat = jnp.concatenate([xr_ref[rows, sl], xi_ref[rows, sl]], axis=-1).astype(BF16)
        return jnp.dot(xcat, cmat_ref[j], preferred_element_type=F32)

    def scan_half(hf, between):
        cs = slice(hf * half_states, (hf + 1) * half_states)
        ar = jnp.broadcast_to(are_ref[:, cs], (batch, half_states))
        ai = jnp.broadcast_to(aim_ref[:, cs], (batch, half_states))
        s_r, s_i = sr_ref[:, cs], si_ref[:, cs]
        every = steps // len(between)
        for t in range(steps):
            rows = slice(t * batch, (t + 1) * batch)
            s_r, s_i = (ar * s_r - ai * s_i + xr_ref[rows, cs], ar * s_i + ai * s_r + xi_ref[rows, cs])
            xr_ref[rows, cs] = s_r
            xi_ref[rows, cs] = s_i
            if t % every == every - 1:
                between[t // every]()
        sr_ref[:, cs] = s_r
        si_ref[:, cs] = s_i

    first = [(j, k) for j in range(tiles_per_half) for k in range(SSM_PIECES)]
    second = [(j, k) for j in range(tiles_per_half, SSM_TILES) for k in range(SSM_PIECES)]
    for j, k in first:
        bu_piece(j, k)
    scan_half(0, [functools.partial(bu_piece, j, k) for j, k in second])
    y_pieces = {}
    scan_half(1, [functools.partial(lambda j, k: y_pieces.__setitem__((j, k), c_piece(j, k)), j, k)
                  for j, k in first])
    for j, k in second:
        y_pieces[(j, k)] = c_piece(j, k)
    ys = [jnp.concatenate([y_pieces[(j, k)] for k in range(SSM_PIECES)], axis=0) for j in range(SSM_TILES)]
    y = jnp.concatenate(ys, axis=-1) + d_ref[...] * uf
    y = jax.nn.gelu(y)
    y = y * jax.nn.sigmoid(jnp.dot(y.astype(BF16), wglu_ref[...], preferred_element_type=F32))
    y = _rms(y, og_ref[...])
    for j in range(SSM_TILES):
        tb_ref[j] = y[:, j * LANES:(j + 1) * LANES]
    for b in range(batch):
        for j in range(SSM_TILES):
            c0 = b * SSM_WIDTH + j * LANES
            out_ref[:, c0:c0 + LANES] = tb_ref[j, pl.ds(b, steps, stride=batch), :].astype(out_ref.dtype)


def _ssm(u_sb, bmat, cmat, a_re, a_im, d_skip, w_glu, ogain, batch, seq, steps):
    rows = steps * batch
    const2 = lambda c: (0, 0)
    const3 = lambda c: (0, 0, 0)
    return pl.pallas_call(
        functools.partial(_ssm_kernel, batch=batch, steps=steps),
        grid=(seq // steps,),
        in_specs=[pl.BlockSpec((steps, batch * SSM_WIDTH), lambda c: (c, 0)),
                  pl.BlockSpec(bmat.shape, const3), pl.BlockSpec(cmat.shape, const3),
                  pl.BlockSpec((1, SSM_STATES), const2), pl.BlockSpec((1, SSM_STATES), const2),
                  pl.BlockSpec((1, SSM_WIDTH), const2),
                  pl.BlockSpec((SSM_WIDTH, SSM_WIDTH), const2),
                  pl.BlockSpec((1, SSM_WIDTH), const2)],
        out_specs=pl.BlockSpec((steps, batch * SSM_WIDTH), lambda c: (c, 0)),
        out_shape=jax.ShapeDtypeStruct((seq, batch * SSM_WIDTH), BF16),
        scratch_shapes=[pltpu.VMEM((SSM_TILES, rows, LANES), F32),
                        pltpu.VMEM((rows, SSM_STATES), F32), pltpu.VMEM((rows, SSM_STATES), F32),
                        pltpu.VMEM((batch, SSM_STATES), F32), pltpu.VMEM((batch, SSM_STATES), F32)],
        compiler_params=_cparams("arbitrary"),
        name="s5_layer",
    )(u_sb, bmat, cmat, a_re, a_im, d_skip, w_glu, ogain)


def _outproj_kernel(ya_ref, ys_ref, yx_ref, x_ref, wo_ref, g_ref, wr_ref, x1_ref, lt_ref, hp_ref, *, sub):
    for r0 in range(0, x_ref.shape[0], sub):
        rows = slice(r0, r0 + sub)
        mix = jnp.concatenate([ya_ref[rows, :], ys_ref[rows, :], yx_ref[rows, :]], axis=-1)
        x1 = x_ref[rows, :] + jnp.dot(mix, wo_ref[...].astype(BF16), preferred_element_type=F32)
        x1_ref[rows, :] = x1
        h2 = _rms(x1, g_ref[...])
        hi = h2.astype(BF16)
        lt_ref[rows, :] = jnp.dot(hi, wr_ref[...], preferred_element_type=F32)
        hp_ref[rows, :] = _pack_rows(hi.astype(F32))


def _outproj(ya, ys_sb, yx, x2, w_o, gain, wr, batch, seq, tm):
    t, d = x2.shape
    nsb = seq // tm
    row = lambda w: pl.BlockSpec((tm, w), lambda i: (i, 0))
    const = lambda shape: pl.BlockSpec(shape, lambda i: (0, 0), pipeline_mode=pl.Buffered(1))
    return pl.pallas_call(
        functools.partial(_outproj_kernel, sub=min(MXU_WIDTH, tm)),
        grid=(t // tm,),
        in_specs=[row(ATTN_WIDTH),
                  pl.BlockSpec((tm, SSM_WIDTH), lambda i: (i % nsb, i // nsb)),
                  row(XATTN_WIDTH), row(d),
                  const(w_o.shape), const((1, d)), const(wr.shape)],
        out_specs=[row(d), row(LANES), row(d // 2)],
        out_shape=[jax.ShapeDtypeStruct((t, d), F32), jax.ShapeDtypeStruct((t, LANES), F32),
                   jax.ShapeDtypeStruct((t, d // 2), jnp.uint32)],
        compiler_params=_cparams("arbitrary"),
        name="out_proj_router",
    )(ya, ys_sb, yx, x2, w_o, gain, wr)


def _route_kernel(lt_ref, tri_ref, dest_ref, w_ref, meta_ref, cnt_ref, carry_ref, pstart_ref):
    phase = pl.program_id(0)
    c = pl.program_id(1)
    logits = lt_ref[...].T
    tc = logits.shape[1]
    ng, epg = N_EXPERT_GROUPS, EXPERTS_PER_GROUP
    row8 = lax.broadcasted_iota(jnp.int32, (ng, tc), 0)

    gl = logits[0:ng]
    gmax = jnp.max(gl, axis=0, keepdims=True)
    gidx = jnp.min(jnp.where(gl == gmax, row8, ng), axis=0, keepdims=True)
    gate = 1.0 / jnp.sum(jnp.exp(gl - gmax), axis=0, keepdims=True)
    sel = jnp.zeros((epg, tc), F32)
    for g in range(ng):
        sel = jnp.where(gidx == g, logits[ng + g * epg:ng + (g + 1) * epg], sel)
    v1 = jnp.max(sel, axis=0, keepdims=True)
    i1 = jnp.min(jnp.where(sel == v1, row8, epg), axis=0, keepdims=True)
    sel2 = jnp.where(row8 == i1, -jnp.inf, sel)
    v2 = jnp.max(sel2, axis=0, keepdims=True)
    i2 = jnp.min(jnp.where(sel2 == v2, row8, epg), axis=0, keepdims=True)
    e = jnp.exp(v2 - v1)
    w1 = gate * (1.0 / (1.0 + e))
    w2 = gate * (e / (1.0 + e))
    e1 = gidx * epg + i1
    e2 = gidx * epg + i2
    rowe = lax.broadcasted_iota(jnp.int32, (N_EXPERTS, tc), 0)
    oh1 = rowe == e1
    oh2 = rowe == e2
    member = jnp.where(jnp.logical_or(oh1, oh2), 1.0, 0.0)
    chunk_cnt = jnp.sum(member, axis=1, keepdims=True)

    @pl.when(phase == 0)
    def _():
        @pl.when(c == 0)
        def _():
            cnt_ref[...] = jnp.zeros_like(cnt_ref)
        cnt_ref[...] += chunk_cnt

    @pl.when(phase == 1)
    def _():
        @pl.when(c == 0)
        def _():
            cnt = cnt_ref[...]
            nblk = jnp.floor((cnt + (MOE_BLOCK - 1)) * (1.0 / MOE_BLOCK))
            nchunk = jnp.floor((nblk + (CHUNK_BLOCKS - 1)) * (1.0 / CHUNK_BLOCKS))
            r = lax.broadcasted_iota(jnp.int32, (N_EXPERTS, LANES), 0)
            cidx = lax.broadcasted_iota(jnp.int32, (N_EXPERTS, LANES), 1)
            to_row = lambda col: jnp.sum(jnp.where(r == cidx, col, 0.0), axis=0, keepdims=True)
            cumsum_col = lambda col: jnp.sum(jnp.where(cidx <= r, to_row(col), 0.0), axis=1, keepdims=True)
            cumsum_row = lambda col: jnp.sum(jnp.where(r <= cidx, col, 0.0), axis=0, keepdims=True)
            bend = cumsum_col(nblk)
            bstart = bend - nblk
            cend = cumsum_col(nchunk)
            cstart = cend - nchunk
            pstart_ref[...] = bstart * MOE_BLOCK
            carry_ref[...] = jnp.zeros_like(carry_ref)
            lanef = lax.broadcasted_iota(jnp.int32, (1, LANES), 1).astype(F32)
            owner = jnp.minimum(jnp.sum(jnp.where(cend <= lanef, 1.0, 0.0), axis=0, keepdims=True),
                                N_EXPERTS - 1.0)
            own = r.astype(F32) == owner
            pick = lambda col: jnp.sum(jnp.where(own, col, 0.0), axis=0, keepdims=True)
            idx = lanef - pick(cstart)
            first = pick(bstart) + CHUNK_BLOCKS * idx
            size = jnp.clip(pick(nblk) - CHUNK_BLOCKS * idx, 0.0, float(CHUNK_BLOCKS))
            zero = jnp.zeros((1, LANES), F32)
            rows = [owner, first, size,
                    zero + jnp.sum(nchunk, axis=0, keepdims=True),
                    cumsum_row(nblk) - 1.0,
                    to_row(nblk),
                    zero + jnp.sum(nblk, axis=0, keepdims=True),
                    pick(nchunk)]
            for k, v in enumerate(rows):
                meta_ref[k:k + 1, :] = v.astype(jnp.int32)

        before = carry_ref[...] + jnp.dot(member.astype(BF16), tri_ref[...], preferred_element_type=F32)
        pos = before + pstart_ref[...]
        dest_ref[0:1, :] = jnp.sum(jnp.where(oh1, pos, 0.0), axis=0, keepdims=True).astype(jnp.int32)
        dest_ref[1:2, :] = jnp.sum(jnp.where(oh2, pos, 0.0), axis=0, keepdims=True).astype(jnp.int32)
        w_ref[0:1, :] = w1
        w_ref[1:2, :] = w2
        carry_ref[...] += chunk_cnt


def _route(logits_t, tc):
    t = logits_t.shape[0]
    nc = t // tc
    tri = jnp.asarray(np.triu(np.ones((tc, tc), np.float32), k=1), dtype=BF16)
    return pl.pallas_call(
        _route_kernel,
        grid=(2, nc),
        in_specs=[pl.BlockSpec((tc, LANES), lambda p, c: (c, 0)),
                  pl.BlockSpec((tc, tc), lambda p, c: (0, 0))],
        out_specs=[pl.BlockSpec((2, tc), lambda p, c: (0, c * p)),
                   pl.BlockSpec((2, tc), lambda p, c: (0, c * p)),
                   pl.BlockSpec((META_ROWS, LANES), lambda p, c: (0, 0))],
        out_shape=[jax.ShapeDtypeStruct((2, t), jnp.int32), jax.ShapeDtypeStruct((2, t), F32),
                   jax.ShapeDtypeStruct((META_ROWS, LANES), jnp.int32)],
        scratch_shapes=[pltpu.VMEM((N_EXPERTS, 1), F32)] * 3,
        compiler_params=_cparams("arbitrary", "arbitrary"),
        name="moe_route",
    )(logits_t, tri)


def _meta(meta_ref, row, lane=0):
    return meta_ref[row * LANES + lane]


def _fill_blocks(meta_ref, zbuf, dst_ref, sem, n_blocks, *, expert_tails):
    zbuf[...] = jnp.zeros_like(zbuf)
    n_used = _meta(meta_ref, M_NUSED)
    block = lambda b: pltpu.make_async_copy(zbuf, dst_ref.at[pl.ds(b * MOE_BLOCK, MOE_BLOCK), :], sem)

    def tails(fn):
        def body(e, carry):
            @pl.when(_meta(meta_ref, M_NBLK, e) > 0)
            def _():
                fn(block(_meta(meta_ref, M_LAST, e)))
            return carry
        lax.fori_loop(0, N_EXPERTS, body, 0)

    def unused(fn):
        def body(b, carry):
            fn(block(b))
            return carry
        lax.fori_loop(n_used, n_blocks, body, 0)

    for phase in (lambda cp: cp.start(), lambda cp: cp.wait()):
        if expert_tails:
            tails(phase)
        unused(phase)


def _dispatch_kernel(dest_ref, meta_ref, h_ref, xs_ref, zbuf, sem_z, sem, *, tokens, n_blocks):
    tm = h_ref.shape[0]

    @pl.when(pl.program_id(0) == 0)
    def _():
        _fill_blocks(meta_ref, zbuf, xs_ref, sem_z, n_blocks, expert_tails=True)

    base = pl.program_id(0) * tm

    for r in range(tm):
        for k in range(2):
            d = dest_ref[k * tokens + base + r]
            pltpu.make_async_copy(h_ref.at[pl.ds(r, 1), :], xs_ref.at[pl.ds(d, 1), :], sem).start()
    for k in range(2):
        pltpu.make_async_copy(h_ref, xs_ref.at[pl.ds(0, tm), :], sem).wait()


def _dispatch(dest_flat, meta_flat, h2p, n_blocks, tm):
    t, w = h2p.shape
    grid_spec = pltpu.PrefetchScalarGridSpec(
        num_scalar_prefetch=2,
        grid=(t // tm,),
        in_specs=[pl.BlockSpec((tm, w), lambda i, d, m: (i, 0))],
        out_specs=pl.BlockSpec(memory_space=pl.ANY),
        scratch_shapes=[pltpu.VMEM((MOE_BLOCK, w), h2p.dtype),
                        pltpu.SemaphoreType.DMA(()), pltpu.SemaphoreType.DMA(())],
    )
    return pl.pallas_call(
        functools.partial(_dispatch_kernel, tokens=t, n_blocks=n_blocks),
        grid_spec=grid_spec,
        out_shape=jax.ShapeDtypeStruct((n_blocks * MOE_BLOCK, w), h2p.dtype),
        compiler_params=_cparams("arbitrary"),
        name="moe_dispatch",
    )(dest_flat, meta_flat, h2p)


def _pack_rows(x):
    bits = lax.bitcast_convert_type(x, jnp.uint32)
    half = x.shape[1] // 2
    return (bits[:, half:] & jnp.uint32(0xFFFF0000)) | (bits[:, :half] >> 16)


def _unpack_halves(words):
    lo = lax.bitcast_convert_type(words << 16, F32)
    hi = lax.bitcast_convert_type(words & jnp.uint32(0xFFFF0000), F32)
    return lo, hi


def _unpack_rows(words):
    return jnp.concatenate(_unpack_halves(words), axis=-1).astype(BF16)


def _expert_kernel(meta_ref, xs_ref, wg_ref, wu_ref, wd_ref, yb_ref,
                   xbuf, ybuf, zbuf, wg_f32, wu_f32, wd_f32, wslot_ref,
                   sem_in, sem_out, sem_z, sem_w, *, n_blocks):
    n_chunks = _meta(meta_ref, M_NCHUNK)

    def weight_copies(k, s):
        e = _meta(meta_ref, M_OWNER, k)
        return [pltpu.make_async_copy(src.at[e], dst.at[s], sem_w.at[s])
                for src, dst in ((wg_ref, wg_f32), (wu_ref, wu_f32), (wd_ref, wd_f32))]

    def in_copy(k, s, nb):
        rows = nb * MOE_BLOCK
        src = xs_ref.at[pl.ds(_meta(meta_ref, M_FIRST, k) * MOE_BLOCK, rows), :]
        return pltpu.make_async_copy(src, xbuf.at[s, pl.ds(0, rows), :], sem_in.at[s])

    def out_copy(k, s, nb):
        rows = nb * MOE_BLOCK
        dst = yb_ref.at[pl.ds(_meta(meta_ref, M_FIRST, k) * MOE_BLOCK, rows), :]
        return pltpu.make_async_copy(ybuf.at[s, pl.ds(0, rows), :], dst, sem_out.at[s])

    def by_size(k, fn):
        for nb in range(1, CHUNK_BLOCKS + 1):
            pl.when(_meta(meta_ref, M_SIZE, k) == nb)(functools.partial(fn, nb))

    wslot_ref[0] = 0
    by_size(0, lambda nb: in_copy(0, 0, nb).start())
    for cp in weight_copies(0, 0):
        cp.start()

    def chunk_step(c, carry):
        slot = c % 2

        @pl.when(c + 1 < n_chunks)
        def _():
            by_size(c + 1, lambda nb: in_copy(c + 1, 1 - slot, nb).start())

        prev = jnp.maximum(c - 1, 0)
        new_expert = jnp.logical_or(c == 0, _meta(meta_ref, M_OWNER, c) != _meta(meta_ref, M_OWNER, prev))

        @pl.when(new_expert)
        def _():
            ws = jnp.where(c == 0, 0, 1 - wslot_ref[0])
            wslot_ref[0] = ws
            nxt = c + _meta(meta_ref, M_OWNER_CHUNKS, c)

            @pl.when(nxt < n_chunks)
            def _():
                for cp in weight_copies(nxt, 1 - ws):
                    cp.start()
            for cp in weight_copies(c, ws):
                cp.wait()

        @pl.when(c >= 2)
        def _():
            by_size(c - 2, lambda nb: out_copy(c - 2, slot, nb).wait())

        def compute(nb):
            rows = nb * MOE_BLOCK
            ws = wslot_ref[0]
            in_copy(c, slot, nb).wait()
            h = _unpack_rows(xbuf[slot, 0:rows, :])
            gate = jnp.dot(h, wg_f32[ws].astype(BF16), preferred_element_type=F32)
            up = jnp.dot(h, wu_f32[ws].astype(BF16), preferred_element_type=F32)
            act = (jax.nn.silu(gate) * up).astype(BF16)
            y = jnp.dot(act, wd_f32[ws].astype(BF16), preferred_element_type=F32)
            ybuf[slot, 0:rows, :] = _pack_rows(y.astype(BF16).astype(F32))
            out_copy(c, slot, nb).start()
        by_size(c, compute)
        return carry

    lax.fori_loop(0, n_chunks, chunk_step, 0)

    _fill_blocks(meta_ref, zbuf, yb_ref, sem_z, n_blocks, expert_tails=False)
    for back in (2, 1):
        @pl.when(n_chunks >= back)
        def _(back=back):
            k = n_chunks - back
            by_size(k, lambda nb: out_copy(k, k % 2, nb).wait())


def _experts(meta_flat, xs, w_gate, w_up, w_down, n_blocks):
    d, de = w_gate.shape[1], w_gate.shape[2]
    rows = CHUNK_BLOCKS * MOE_BLOCK
    assert (n_blocks + (CHUNK_BLOCKS - 1) * N_EXPERTS) // CHUNK_BLOCKS <= LANES

    hbm = pl.BlockSpec(memory_space=pl.ANY)
    grid_spec = pltpu.PrefetchScalarGridSpec(
        num_scalar_prefetch=1,
        grid=(1,),
        in_specs=[hbm, hbm, hbm, hbm],
        out_specs=hbm,
        scratch_shapes=[pltpu.VMEM((2, rows, xs.shape[1]), xs.dtype),
                        pltpu.VMEM((2, rows, d // 2), jnp.uint32),
                        pltpu.VMEM((MOE_BLOCK, d // 2), jnp.uint32),
                        pltpu.VMEM((2, d, de), F32), pltpu.VMEM((2, d, de), F32), pltpu.VMEM((2, de, d), F32),
                        pltpu.SMEM((1,), jnp.int32),
                        pltpu.SemaphoreType.DMA((2,)), pltpu.SemaphoreType.DMA((2,)),
                        pltpu.SemaphoreType.DMA(()), pltpu.SemaphoreType.DMA((2,))],
    )
    return pl.pallas_call(
        functools.partial(_expert_kernel, n_blocks=n_blocks),
        grid_spec=grid_spec,
        out_shape=jax.ShapeDtypeStruct((n_blocks * MOE_BLOCK, d // 2), jnp.uint32),
        compiler_params=_cparams("arbitrary"),
        name="moe_experts",
    )(meta_flat, xs, w_gate, w_up, w_down)


def _combine_kernel(dest_ref, yb_ref, x1_ref, w_ref, out_ref, gbuf, sem, *, tokens):
    i = pl.program_id(0)
    n = pl.num_programs(0)
    tm = x1_ref.shape[0]
    half = x1_ref.shape[1] // 2
    slot = i % COMBINE_SLOTS
    ahead = COMBINE_SLOTS - 1

    def issue_row(tile, r):
        s = tile % COMBINE_SLOTS
        for k in range(2):
            d = dest_ref[k * tokens + tile * tm + r]
            pltpu.make_async_copy(yb_ref.at[pl.ds(d, 1), :], gbuf.at[s, k, pl.ds(r, 1), :], sem.at[s]).start()

    def combine_rows(r0):
        rows = slice(r0, r0 + COMBINE_PIECE)
        w = w_ref[rows, :]
        lo0, hi0 = _unpack_halves(gbuf[slot, 0, rows, :])
        lo1, hi1 = _unpack_halves(gbuf[slot, 1, rows, :])
        out_ref[rows, :half] = x1_ref[rows, :half] + (lo0 * w[:, 0:1] + lo1 * w[:, 1:2])
        out_ref[rows, half:] = x1_ref[rows, half:] + (hi0 * w[:, 0:1] + hi1 * w[:, 1:2])

    @pl.when(i == 0)
    def _():
        for first in range(ahead):
            @pl.when(first < n)
            def _(first=first):
                def body(g, carry):
                    r8 = pl.multiple_of(g * SUBLANES, SUBLANES)
                    for sub in range(SUBLANES):
                        issue_row(first, r8 + sub)
                    return carry
                lax.fori_loop(0, tm // SUBLANES, body, 0)

    for k in range(2):
        pltpu.make_async_copy(yb_ref.at[pl.ds(0, tm), :], gbuf.at[slot, k], sem.at[slot]).wait()

    @pl.when(i + ahead < n)
    def _():
        for r0 in range(0, tm, COMBINE_PIECE):
            for r in range(r0, r0 + COMBINE_PIECE):
                issue_row(i + ahead, r)
            combine_rows(r0)

    @pl.when(i + ahead >= n)
    def _():
        for r0 in range(0, tm, COMBINE_PIECE):
            combine_rows(r0)


def _combine(dest_flat, yb, x1, w_tok, tm):
    t, d = x1.shape
    grid_spec = pltpu.PrefetchScalarGridSpec(
        num_scalar_prefetch=1,
        grid=(t // tm,),
        in_specs=[pl.BlockSpec(memory_space=pl.ANY),
                  pl.BlockSpec((tm, d), lambda i, dr: (i, 0)),
                  pl.BlockSpec((tm, 2), lambda i, dr: (i, 0))],
        out_specs=pl.BlockSpec((tm, d), lambda i, dr: (i, 0)),
        scratch_shapes=[pltpu.VMEM((COMBINE_SLOTS, 2, tm, d // 2), jnp.uint32),
                        pltpu.SemaphoreType.DMA((COMBINE_SLOTS,))],
    )
    return pl.pallas_call(
        functools.partial(_combine_kernel, tokens=t),
        grid_spec=grid_spec,
        out_shape=jax.ShapeDtypeStruct((t, d), F32),
        compiler_params=_cparams("arbitrary"),
        name="moe_combine",
    )(dest_flat, yb, x1, w_tok)


def _row(v):
    return v.astype(F32).reshape(1, -1)


def _layer(x2, mem2, batch, seq, mem_len, p):
    t, d = x2.shape

    bias = _bias_table(p["rel_bias"])
    col = lambda v: jnp.broadcast_to(v.astype(F32)[:, None], (v.shape[0], BLOCK))
    qgain = col(p["q_norm"]) * (1.0 / math.sqrt(ATTN_HEAD_DIM))
    kgain = jnp.tile(_row(p["k_norm"]), (1, ATTN_KV_HEADS))
    u_sb, qx, ya = _inproj_swa(x2, _row(p["norm_mix"]), p["w_in"].astype(F32), bias, p["attn_sinks"].astype(F32),
                               qgain, kgain, col(p["out_norm_attn"]), batch, seq, min(TOKEN_TILE, seq))

    km, vm = _memkv(mem2, _row(p["mem_norm"]), p["w_mem_kv"].astype(F32), _row(p["xk_norm"]),
                    min(MEM_TILE, mem2.shape[0]))
    xq_gain = _row(p["xq_norm"]) * (1.0 / math.sqrt(XATTN_HEAD_DIM))
    yx = _xattn(qx, km, vm, xq_gain, _row(p["out_norm_xattn"]), batch, seq, mem_len, min(TOKEN_TILE, seq))

    a_re, a_im, bbr, bbi = _ssm_prep(p["ssm_lambda_re"], p["ssm_lambda_im"], p["ssm_log_dt"],
                                     p["ssm_b_re"], p["ssm_b_im"])
    bmat = jnp.concatenate([_block_diag_tiles(bbr), _block_diag_tiles(bbi)], axis=-1).astype(BF16)
    c_re_t = jnp.transpose(p["ssm_c_re"].astype(F32), (0, 2, 1))
    c_im_t = jnp.transpose(p["ssm_c_im"].astype(F32), (0, 2, 1))
    cmat = jnp.concatenate([_block_diag_tiles(c_re_t), _block_diag_tiles(-c_im_t)], axis=1).astype(BF16)
    steps = min(SSM_STEPS, seq)
    ys_sb = _ssm(u_sb, bmat, cmat,
                 a_re.reshape(1, SSM_STATES), a_im.reshape(1, SSM_STATES), _row(p["ssm_d"]),
                 p["ssm_w_glu"].astype(BF16), _row(p["out_norm_ssm"]), batch, seq, steps)

    wr = jnp.concatenate([p["w_router_group"], p["w_router_expert"]], axis=1).astype(F32)
    wr = jnp.pad(wr, ((0, 0), (0, LANES - wr.shape[1]))).astype(BF16)
    tm_out = min(TOKEN_TILE, seq)
    x1, logits_t, h2p = _outproj(ya, ys_sb, yx, x2, p["w_o"].astype(F32), _row(p["norm_ffn"]), wr,
                                 batch, seq, tm_out)

    dest, w_k, meta = _route(logits_t, min(ROUTE_CHUNK, t))
    n_blocks = (2 * t) // MOE_BLOCK + N_EXPERTS
    dest_flat = dest.reshape(2 * t)
    meta_flat = meta.reshape(META_ROWS * LANES)
    xs = _dispatch(dest_flat, meta_flat, h2p, n_blocks, min(DISPATCH_TILE, t))
    yb = _experts(meta_flat, xs, p["w_gate"], p["w_up"], p["w_down"], n_blocks)
    return _combine(dest_flat, yb, x1, w_k.T, min(TOKEN_TILE, t))


def kernel(x, mem, norm_mix, w_in, q_norm, k_norm, attn_sinks, rel_bias, ssm_lambda_re, ssm_lambda_im, ssm_log_dt, ssm_b_re, ssm_b_im, ssm_c_re, ssm_c_im, ssm_d, ssm_w_glu, mem_norm, w_mem_kv, xq_norm, xk_norm, out_norm_attn, out_norm_ssm, out_norm_xattn, w_o, norm_ffn, w_router_group, w_router_expert, w_gate, w_up, w_down):
    batch, seq, d = x.shape
    mem_len = mem.shape[1]
    per_layer = dict(norm_mix=norm_mix, w_in=w_in, q_norm=q_norm, k_norm=k_norm, attn_sinks=attn_sinks,
                     ssm_lambda_re=ssm_lambda_re, ssm_lambda_im=ssm_lambda_im, ssm_log_dt=ssm_log_dt,
                     ssm_b_re=ssm_b_re, ssm_b_im=ssm_b_im, ssm_c_re=ssm_c_re, ssm_c_im=ssm_c_im,
                     ssm_d=ssm_d, ssm_w_glu=ssm_w_glu, mem_norm=mem_norm, w_mem_kv=w_mem_kv,
                     xq_norm=xq_norm, xk_norm=xk_norm, out_norm_attn=out_norm_attn,
                     out_norm_ssm=out_norm_ssm, out_norm_xattn=out_norm_xattn, w_o=w_o, norm_ffn=norm_ffn,
                     w_router_group=w_router_group, w_router_expert=w_router_expert,
                     w_gate=w_gate, w_up=w_up, w_down=w_down)
    x2 = x.astype(F32).reshape(batch * seq, d)
    mem2 = mem.astype(F32).reshape(batch * mem_len, d)
    for l in range(norm_mix.shape[0]):
        p = {k: v[l] for k, v in per_layer.items()}
        p["rel_bias"] = rel_bias
        x2 = _layer(x2, mem2, batch, seq, mem_len, p)
    return x2.reshape(batch, seq, d).astype(x.dtype)
```

```python
import functools
import math

import numpy as np
import jax
import jax.numpy as jnp
from jax import lax
from jax.experimental import pallas as pl
from jax.experimental.pallas import tpu as pltpu

F32 = jnp.float32
BF16 = jnp.bfloat16
EPS = 1e-6

ATTN_HEADS = 16
ATTN_KV_HEADS = 2
ATTN_HEAD_DIM = 64
ATTN_WIDTH = ATTN_HEADS * ATTN_HEAD_DIM
WINDOW = 128
BLOCK = 128
REL_BUCKETS = 32
REL_MAX_DIST = 128
SSM_GROUP_CH = 16
SSM_GROUPS = 32
SSM_STATE = 64
SSM_WIDTH = SSM_GROUPS * SSM_GROUP_CH
XATTN_HEADS = 4
XATTN_HEAD_DIM = 128
XATTN_WIDTH = XATTN_HEADS * XATTN_HEAD_DIM
N_EXPERT_GROUPS = 8
EXPERTS_PER_GROUP = 8
N_EXPERTS = N_EXPERT_GROUPS * EXPERTS_PER_GROUP
MOE_BLOCK = 128

LANES = 128
SUBLANES = 8
MXU_WIDTH = 256
TOKEN_TILE = 512
XATTN_TILE = 1024
MEM_TILE = 512
SSM_STEPS = 128
ROUTE_CHUNK = 2048
DISPATCH_TILE = 1024
SSM_GROUPS_PER_TILE = LANES // SSM_GROUP_CH
SSM_TILES = SSM_WIDTH // LANES
SSM_TILE_STATE = SSM_GROUPS_PER_TILE * SSM_STATE
SSM_STATES = SSM_GROUPS * SSM_STATE
SSM_PIECES = 8
CHUNK_BLOCKS = 4
COMBINE_SLOTS = 3
COMBINE_PIECE = 32
META_ROWS = 8
M_OWNER, M_FIRST, M_SIZE, M_NCHUNK, M_LAST, M_NBLK, M_NUSED, M_OWNER_CHUNKS = range(8)
VMEM_LIMIT = 56 * 1024 * 1024

_NT = (((1,), (1,)), ((), ()))


def _cparams(*sem):
    return pltpu.CompilerParams(dimension_semantics=sem, vmem_limit_bytes=VMEM_LIMIT)


def _rms(x, gain):
    ms = jnp.mean(x * x, axis=-1, keepdims=True)
    return x * lax.rsqrt(ms + EPS) * gain


def _t5_bucket_table():
    qi = np.arange(BLOCK, dtype=np.int32)[:, None]
    ki = np.arange(2 * BLOCK, dtype=np.int32)[None, :]
    delta = BLOCK + qi - ki
    n = np.maximum(delta, 0)
    max_exact = REL_BUCKETS // 2
    nf = np.maximum(n, 1).astype(np.float32)
    large = max_exact + (np.log(nf / np.float32(max_exact)) / np.float32(math.log(REL_MAX_DIST / max_exact))
                         * np.float32(REL_BUCKETS - max_exact)).astype(np.int32)
    large = np.minimum(large, REL_BUCKETS - 1)
    return np.where(n < max_exact, n, large).astype(np.int32)


def _upper(cols):
    k = lax.broadcasted_iota(jnp.int32, (BLOCK, cols), 0)
    q = lax.broadcasted_iota(jnp.int32, (BLOCK, cols), 1) % BLOCK
    return k > q


def _bias_kernel(rb_ref, bucket_ref, out_ref):
    pair = pl.program_id(0)
    bucket = bucket_ref[...]
    upper = _upper(BLOCK)
    for half in range(2):
        acc = jnp.zeros(bucket.shape, F32)
        for b in range(REL_BUCKETS):
            acc = jnp.where(bucket == b, rb_ref[b, 2 * pair + half], acc)
        cols = slice(half * BLOCK, (half + 1) * BLOCK)
        out_ref[0, 0, :, cols] = jnp.where(upper, acc[:BLOCK], acc[BLOCK:])
        out_ref[1, 0, :, cols] = jnp.where(upper, jnp.float32(-1e30), acc[BLOCK:])


def _bias_table(rel_bias):
    assert WINDOW == BLOCK
    bucket = jnp.asarray(_t5_bucket_table().T)
    pairs = ATTN_HEADS // 2
    return pl.pallas_call(
        _bias_kernel,
        grid=(pairs,),
        in_specs=[pl.BlockSpec(memory_space=pltpu.SMEM),
                  pl.BlockSpec((2 * BLOCK, BLOCK), lambda h: (0, 0))],
        out_specs=pl.BlockSpec((2, 1, BLOCK, 2 * BLOCK), lambda h: (0, h, 0, 0)),
        out_shape=jax.ShapeDtypeStruct((2, pairs, BLOCK, 2 * BLOCK), F32),
        compiler_params=_cparams("arbitrary"),
        name="t5_bias_table",
    )(rel_bias.astype(F32), bucket)


def _ssm_prep_kernel(lr_ref, li_ref, ldt_ref, br_ref, bi_ref, are_ref, aim_ref, bbr_ref, bbi_ref):
    lr = lr_ref[...]
    li = li_ref[...]
    dt = jnp.exp(ldt_ref[...])
    mag = jnp.exp(lr * dt)
    a_re = mag * jnp.cos(li * dt)
    a_im = mag * jnp.sin(li * dt)
    den = lr * lr + li * li
    nr = a_re - 1.0
    ni = a_im
    coef_re = (nr * lr + ni * li) / den
    coef_im = (ni * lr - nr * li) / den
    are_ref[...] = a_re
    aim_ref[...] = a_im
    br = br_ref[...]
    bi = bi_ref[...]
    bbr_ref[...] = coef_re * br - coef_im * bi
    bbi_ref[...] = coef_re * bi + coef_im * br


def _ssm_prep(lam_re, lam_im, log_dt, b_re, b_im):
    g, n, c = b_re.shape
    vec = jax.ShapeDtypeStruct((g, 1, n), F32)
    mat = jax.ShapeDtypeStruct((g, c, n), F32)
    return pl.pallas_call(
        _ssm_prep_kernel,
        out_shape=(vec, vec, mat, mat),
        name="ssm_discretise",
    )(lam_re.astype(F32).reshape(g, 1, n), lam_im.astype(F32).reshape(g, 1, n),
      log_dt.astype(F32).reshape(g, 1, 1),
      jnp.transpose(b_re.astype(F32), (0, 2, 1)), jnp.transpose(b_im.astype(F32), (0, 2, 1)))


def _block_diag_tiles(m):
    g, r, c = m.shape
    t = g // SSM_GROUPS_PER_TILE
    eye = jnp.eye(SSM_GROUPS_PER_TILE, dtype=m.dtype)
    m4 = m.reshape(t, SSM_GROUPS_PER_TILE, r, c)
    out = m4[:, :, :, None, :] * eye[None, :, None, :, None]
    return out.reshape(t, SSM_GROUPS_PER_TILE * r, SSM_GROUPS_PER_TILE * c)


def _k_norm(k, kgain):
    dh = ATTN_HEAD_DIM
    lo = lax.broadcasted_iota(jnp.int32, (1, ATTN_KV_HEADS * dh), 1) < dh
    sq = k * k
    s_lo = jnp.sum(jnp.where(lo, sq, 0.0), axis=-1, keepdims=True)
    s_hi = jnp.sum(sq, axis=-1, keepdims=True) - s_lo
    return k * jnp.where(lo, lax.rsqrt(s_lo / dh + EPS), lax.rsqrt(s_hi / dh + EPS)) * kgain


def _swa_phases(q_of, kn, v_t, bias_of, sinks_ref, qgain, ogain, emit):
    dh = ATTN_HEAD_DIM
    upper = _upper(2 * BLOCK)
    first_head = lax.broadcasted_iota(jnp.int32, (1, 2 * BLOCK), 1) < BLOCK
    zeros = jnp.zeros((dh, BLOCK), F32)
    heads_per_kv = ATTN_HEADS // ATTN_KV_HEADS
    pairs = range(ATTN_HEADS // 2)
    kv_of = lambda pair: (2 * pair) // heads_per_kv
    state = {}

    def qk_all():
        logits = []
        for pair in pairs:
            cols = []
            for half in range(2):
                qh = q_of(2 * pair + half).astype(F32)
                ms = jnp.mean(qh * qh, axis=0, keepdims=True)
                qn = qh * lax.rsqrt(ms + EPS) * qgain
                cols.append(jnp.concatenate([qn, zeros] if kv_of(pair) == 0 else [zeros, qn], axis=0))
            rhs = jnp.concatenate(cols, axis=1).astype(BF16)
            logits.append(jnp.dot(kn, rhs, preferred_element_type=F32))
        state["logits"] = logits

    def softmax_all():
        probs = []
        for pair in pairs:
            both = state["logits"][pair]
            l = jnp.where(upper, both[:BLOCK], both[BLOCK:]) + bias_of(pair)
            sink = jnp.where(first_head, sinks_ref[2 * pair], sinks_ref[2 * pair + 1])
            m = jnp.maximum(jnp.max(l, axis=0, keepdims=True), sink)
            p = jnp.exp(l - m)
            den = jnp.sum(p, axis=0, keepdims=True) + jnp.exp(sink - m)
            pz = jnp.concatenate([jnp.where(upper, p, 0.0), jnp.where(upper, 0.0, p)], axis=0).astype(BF16)
            probs.append((pz, den))
        state["probs"] = probs

    def pv_all():
        outs = []
        for pair in pairs:
            g = kv_of(pair)
            pz, den = state["probs"][pair]
            o = jnp.dot(v_t[g * dh:(g + 1) * dh, :], pz, preferred_element_type=F32) / den
            outs += [o[:, :BLOCK], o[:, BLOCK:]]
        y_t = jnp.concatenate(outs, axis=0)
        ms = jnp.mean(y_t * y_t, axis=0, keepdims=True)
        emit((y_t * lax.rsqrt(ms + EPS) * ogain).T)

    return qk_all, softmax_all, pv_all


def _inproj_swa_kernel(sinks_ref, x_ref, g_ref, w_ref, bias_ref, qg_ref, kg_ref, og_ref,
                       u_ref, qx_ref, ya_ref, q_s, k_s, v_s, kprev_s, vprev_s, *, tiles_per_seq):
    i = pl.program_id(0)
    slot = i % 2
    old = 1 - slot
    tm = x_ref.shape[0]
    blocks = tm // BLOCK
    kvw = ATTN_KV_HEADS * ATTN_HEAD_DIM

    @pl.when(i == 0)
    def _():
        q_s[1] = jnp.zeros(q_s.shape[1:], q_s.dtype)
        k_s[1] = jnp.zeros(k_s.shape[1:], k_s.dtype)
        v_s[1] = jnp.zeros(v_s.shape[1:], v_s.dtype)
        kprev_s[...] = jnp.zeros_like(kprev_s)
        vprev_s[...] = jnp.zeros_like(vprev_s)

    h = _rms(x_ref[...], g_ref[...]).astype(BF16)
    proj = lambda c0, c1: jnp.dot(h, w_ref[:, c0:c1].astype(BF16), preferred_element_type=F32)
    piece = MXU_WIDTH

    def q_piece(n):
        def run():
            q_s[slot, n * piece:(n + 1) * piece, :] = proj(n * piece, (n + 1) * piece).T.astype(q_s.dtype)
        return run

    def kv_piece():
        kv = proj(ATTN_WIDTH, ATTN_WIDTH + 2 * kvw)
        k_s[slot] = _k_norm(kv[:, :kvw], kg_ref[...]).astype(k_s.dtype)
        v_s[slot] = kv[:, kvw:].T.astype(v_s.dtype)

    def out_piece(ref, c0, n):
        def run():
            ref[:, n * piece:(n + 1) * piece] = proj(c0 + n * piece, c0 + (n + 1) * piece).astype(ref.dtype)
        return run

    c_u = ATTN_WIDTH + 2 * kvw
    c_qx = c_u + SSM_WIDTH
    projection = ([q_piece(n) for n in range(ATTN_WIDTH // piece)] + [kv_piece]
                  + [out_piece(u_ref, c_u, n) for n in range(SSM_WIDTH // piece)]
                  + [out_piece(qx_ref, c_qx, n) for n in range(XATTN_WIDTH // piece)])

    first_of_seq = (i + tiles_per_seq - 1) % tiles_per_seq == 0
    attention = []
    for blk in range(blocks):
        own = slice(blk * BLOCK, (blk + 1) * BLOCK)
        before = slice((blk - 1) * BLOCK, blk * BLOCK)
        k_before = kprev_s[...] if blk == 0 else k_s[old, before, :]
        v_before = vprev_s[...] if blk == 0 else v_s[old, :, before]
        table = jnp.where(first_of_seq, 1, 0) if blk == 0 else 0

        def emit(y, own=own):
            ya_ref[own, :] = y.astype(ya_ref.dtype)

        attention += _swa_phases(
            q_of=lambda hd, own=own: q_s[old, hd * ATTN_HEAD_DIM:(hd + 1) * ATTN_HEAD_DIM, own],
            kn=jnp.concatenate([k_before, k_s[old, own, :]], axis=0),
            v_t=jnp.concatenate([v_before, v_s[old, :, own]], axis=1),
            bias_of=lambda pair, table=table: bias_ref[table, pair],
            sinks_ref=sinks_ref, qgain=qg_ref[...], ogain=og_ref[...], emit=emit)

    longer, shorter = (attention, projection) if len(attention) >= len(projection) else (projection, attention)
    done = 0
    for n, fn in enumerate(longer):
        fn()
        due = (n + 1) * len(shorter) // len(longer)
        for extra in shorter[done:due]:
            extra()
        done = due

    last = slice((blocks - 1) * BLOCK, blocks * BLOCK)
    kprev_s[...] = k_s[old, last, :]
    vprev_s[...] = v_s[old, :, last]


def _inproj_swa(x2, gain, w_in, bias, sinks, qgain, kgain, ogain, batch, seq, tm):
    t, d = x2.shape
    n_tiles = t // tm
    tiles_per_seq = seq // tm
    kvw = ATTN_KV_HEADS * ATTN_HEAD_DIM
    proj_tile = lambda i: jnp.minimum(i, n_tiles - 1)
    attn_tile = lambda i: jnp.maximum(i - 1, 0)
    const = lambda shape: pl.BlockSpec(shape, lambda i: (0,) * len(shape), pipeline_mode=pl.Buffered(1))
    return pl.pallas_call(
        functools.partial(_inproj_swa_kernel, tiles_per_seq=tiles_per_seq),
        grid=(n_tiles + 1,),
        in_specs=[pl.BlockSpec(memory_space=pltpu.SMEM),
                  pl.BlockSpec((tm, d), lambda i: (proj_tile(i), 0)),
                  const((1, d)), const(w_in.shape), const(bias.shape),
                  const((ATTN_HEAD_DIM, BLOCK)), const((1, kvw)), const((ATTN_WIDTH, BLOCK))],
        out_specs=[
                   pl.BlockSpec((tm, SSM_WIDTH),
                                lambda i: (proj_tile(i) % tiles_per_seq, proj_tile(i) // tiles_per_seq)),
                   pl.BlockSpec((tm, XATTN_WIDTH), lambda i: (proj_tile(i), 0)),
                   pl.BlockSpec((tm, ATTN_WIDTH), lambda i: (attn_tile(i), 0))],
        out_shape=[jax.ShapeDtypeStruct((seq, batch * SSM_WIDTH), BF16),
                   jax.ShapeDtypeStruct((t, XATTN_WIDTH), BF16),
                   jax.ShapeDtypeStruct((t, ATTN_WIDTH), BF16)],
        scratch_shapes=[pltpu.VMEM((2, ATTN_WIDTH, tm), BF16),
                        pltpu.VMEM((2, tm, kvw), BF16),
                        pltpu.VMEM((2, kvw, tm), BF16),
                        pltpu.VMEM((BLOCK, kvw), BF16), pltpu.VMEM((kvw, BLOCK), BF16)],
        compiler_params=_cparams("arbitrary"),
        name="in_proj_swa",
    )(sinks, x2, gain, w_in, bias, qgain, kgain, ogain)


def _memkv_kernel(m_ref, g_ref, w_ref, kg_ref, k_ref, v_ref):
    h = _rms(m_ref[...], g_ref[...]).astype(BF16)
    km = jnp.dot(h, w_ref[:, :XATTN_WIDTH].astype(BF16), preferred_element_type=F32)
    for hd in range(XATTN_HEADS):
        sl = slice(hd * XATTN_HEAD_DIM, (hd + 1) * XATTN_HEAD_DIM)
        k_ref[:, sl] = _rms(km[:, sl], kg_ref[...]).astype(k_ref.dtype)
    v_ref[...] = jnp.dot(h, w_ref[:, XATTN_WIDTH:].astype(BF16), preferred_element_type=F32).astype(v_ref.dtype)


def _memkv(mem2, gain, w_kv, kgain, tm):
    r, d = mem2.shape
    row = lambda w: pl.BlockSpec((tm, w), lambda i: (i, 0))
    const = lambda shape: pl.BlockSpec(shape, lambda i: (0, 0), pipeline_mode=pl.Buffered(1))
    return pl.pallas_call(
        _memkv_kernel,
        grid=(r // tm,),
        in_specs=[row(d), const((1, d)), const((d, 2 * XATTN_WIDTH)), const((1, XATTN_HEAD_DIM))],
        out_specs=[row(XATTN_WIDTH), row(XATTN_WIDTH)],
        out_shape=[jax.ShapeDtypeStruct((r, XATTN_WIDTH), BF16)] * 2,
        compiler_params=_cparams("arbitrary"),
        name="mem_kv_proj",
    )(mem2, gain, w_kv, kgain)


def _xattn_kernel(q_ref, k_ref, v_ref, qg_ref, og_ref, out_ref):
    outs = []
    for hd in range(XATTN_HEADS):
        sl = slice(hd * XATTN_HEAD_DIM, (hd + 1) * XATTN_HEAD_DIM)
        qn = _rms(q_ref[:, sl].astype(F32), qg_ref[...]).astype(BF16)
        l = lax.dot_general(qn, k_ref[:, sl], _NT, preferred_element_type=F32)
        m = jnp.max(l, axis=-1, keepdims=True)
        p = jnp.exp(l - m)
        den = jnp.sum(p, axis=-1, keepdims=True)
        outs.append(jnp.dot(p.astype(BF16), v_ref[:, sl], preferred_element_type=F32) / den)
    y = jnp.concatenate(outs, axis=-1)
    out_ref[...] = _rms(y, og_ref[...]).astype(out_ref.dtype)


def _xattn(qx, km, vm, qgain, ogain, batch, seq, mem_len, tq):
    nq = seq // tq
    const2 = lambda b, i: (0, 0)
    return pl.pallas_call(
        _xattn_kernel,
        grid=(batch, nq),
        in_specs=[pl.BlockSpec((tq, XATTN_WIDTH), lambda b, i: (b * nq + i, 0)),
                  pl.BlockSpec((mem_len, XATTN_WIDTH), lambda b, i: (b, 0)),
                  pl.BlockSpec((mem_len, XATTN_WIDTH), lambda b, i: (b, 0)),
                  pl.BlockSpec((1, XATTN_HEAD_DIM), const2),
                  pl.BlockSpec((1, XATTN_WIDTH), const2)],
        out_specs=pl.BlockSpec((tq, XATTN_WIDTH), lambda b, i: (b * nq + i, 0)),
        out_shape=jax.ShapeDtypeStruct((batch * seq, XATTN_WIDTH), BF16),
        compiler_params=_cparams("arbitrary", "arbitrary"),
        name="mem_xattn",
    )(qx, km, vm, qgain, ogain)


def _ssm_kernel(u_ref, bmat_ref, cmat_ref, are_ref, aim_ref, d_ref, wglu_ref, og_ref, out_ref,
                tb_ref, xr_ref, xi_ref, sr_ref, si_ref, *, batch, steps):
    @pl.when(pl.program_id(0) == 0)
    def _():
        sr_ref[...] = jnp.zeros_like(sr_ref)
        si_ref[...] = jnp.zeros_like(si_ref)

    for b in range(batch):
        for j in range(SSM_TILES):
            c0 = b * SSM_WIDTH + j * LANES
            tb_ref[j, pl.ds(b, steps, stride=batch), :] = u_ref[:, c0:c0 + LANES].astype(F32)
    uf = jnp.concatenate([tb_ref[j] for j in range(SSM_TILES)], axis=-1)
    u = uf.astype(BF16)
    total = steps * batch
    piece = total // SSM_PIECES
    tiles_per_half = SSM_TILES // 2
    half_states = SSM_STATES // 2

    def bu_piece(j, k):
        rows = slice(k * piece, (k + 1) * piece)
        bu = jnp.dot(u[rows, j * LANES:(j + 1) * LANES], bmat_ref[j], preferred_element_type=F32)
        xr_ref[rows, j * SSM_TILE_STATE:(j + 1) * SSM_TILE_STATE] = bu[:, :SSM_TILE_STATE]
        xi_ref[rows, j * SSM_TILE_STATE:(j + 1) * SSM_TILE_STATE] = bu[:, SSM_TILE_STATE:]

    def c_piece(j, k):
        rows = slice(k * piece, (k + 1) * piece)
        sl = slice(j * SSM_TILE_STATE, (j + 1) * SSM_TILE_STATE)
        xcat = jnp.concatenate([xr_ref[rows, sl], xi_ref[rows, sl]], axis=-1).astype(BF16)
        return jnp.dot(xcat, cmat_ref[j], preferred_element_type=F32)

    def scan_half(hf, between):
        cs = slice(hf * half_states, (hf + 1) * half_states)
        ar = jnp.broadcast_to(are_ref[:, cs], (batch, half_states))
        ai = jnp.broadcast_to(aim_ref[:, cs], (batch, half_states))
        s_r, s_i = sr_ref[:, cs], si_ref[:, cs]
        every = steps // len(between)
        for t in range(steps):
            rows = slice(t * batch, (t + 1) * batch)
            s_r, s_i = (ar * s_r - ai * s_i + xr_ref[rows, cs], ar * s_i + ai * s_r + xi_ref[rows, cs])
            xr_ref[rows, cs] = s_r
            xi_ref[rows, cs] = s_i
            if t % every == every - 1:
                between[t // every]()
        sr_ref[:, cs] = s_r
        si_ref[:, cs] = s_i

    first = [(j, k) for j in range(tiles_per_half) for k in range(SSM_PIECES)]
    second = [(j, k) for j in range(tiles_per_half, SSM_TILES) for k in range(SSM_PIECES)]
    for j, k in first:
        bu_piece(j, k)
    scan_half(0, [functools.partial(bu_piece, j, k) for j, k in second])
    y_pieces = {}
    scan_half(1, [functools.partial(lambda j, k: y_pieces.__setitem__((j, k), c_piece(j, k)), j, k)
                  for j, k in first])
    for j, k in second:
        y_pieces[(j, k)] = c_piece(j, k)
    ys = [jnp.concatenate([y_pieces[(j, k)] for k in range(SSM_PIECES)], axis=0) for j in range(SSM_TILES)]
    y = jnp.concatenate(ys, axis=-1) + d_ref[...] * uf
    y = jax.nn.gelu(y)
    y = y * jax.nn.sigmoid(jnp.dot(y.astype(BF16), wglu_ref[...], preferred_element_type=F32))
    y = _rms(y, og_ref[...])
    for j in range(SSM_TILES):
        tb_ref[j] = y[:, j * LANES:(j + 1) * LANES]
    for b in range(batch):
        for j in range(SSM_TILES):
            c0 = b * SSM_WIDTH + j * LANES
            out_ref[:, c0:c0 + LANES] = tb_ref[j, pl.ds(b, steps, stride=batch), :].astype(out_ref.dtype)


def _ssm(u_sb, bmat, cmat, a_re, a_im, d_skip, w_glu, ogain, batch, seq, steps):
    rows = steps * batch
    const2 = lambda c: (0, 0)
    const3 = lambda c: (0, 0, 0)
    return pl.pallas_call(
        functools.partial(_ssm_kernel, batch=batch, steps=steps),
        grid=(seq // steps,),
        in_specs=[pl.BlockSpec((steps, batch * SSM_WIDTH), lambda c: (c, 0)),
                  pl.BlockSpec(bmat.shape, const3), pl.BlockSpec(cmat.shape, const3),
                  pl.BlockSpec((1, SSM_STATES), const2), pl.BlockSpec((1, SSM_STATES), const2),
                  pl.BlockSpec((1, SSM_WIDTH), const2),
                  pl.BlockSpec((SSM_WIDTH, SSM_WIDTH), const2),
                  pl.BlockSpec((1, SSM_WIDTH), const2)],
        out_specs=pl.BlockSpec((steps, batch * SSM_WIDTH), lambda c: (c, 0)),
        out_shape=jax.ShapeDtypeStruct((seq, batch * SSM_WIDTH), BF16),
        scratch_shapes=[pltpu.VMEM((SSM_TILES, rows, LANES), F32),
                        pltpu.VMEM((rows, SSM_STATES), F32), pltpu.VMEM((rows, SSM_STATES), F32),
                        pltpu.VMEM((batch, SSM_STATES), F32), pltpu.VMEM((batch, SSM_STATES), F32)],
        compiler_params=_cparams("arbitrary"),
        name="s5_layer",
    )(u_sb, bmat, cmat, a_re, a_im, d_skip, w_glu, ogain)


def _outproj_kernel(ya_ref, ys_ref, yx_ref, x_ref, wo_ref, g_ref, wr_ref, x1_ref, lt_ref, hp_ref, *, sub):
    for r0 in range(0, x_ref.shape[0], sub):
        rows = slice(r0, r0 + sub)
        mix = jnp.concatenate([ya_ref[rows, :], ys_ref[rows, :], yx_ref[rows, :]], axis=-1)
        x1 = x_ref[rows, :] + jnp.dot(mix, wo_ref[...].astype(BF16), preferred_element_type=F32)
        x1_ref[rows, :] = x1
        h2 = _rms(x1, g_ref[...])
        hi = h2.astype(BF16)
        lt_ref[rows, :] = jnp.dot(hi, wr_ref[...], preferred_element_type=F32)
        hp_ref[rows, :] = _pack_rows(hi.astype(F32))


def _outproj(ya, ys_sb, yx, x2, w_o, gain, wr, batch, seq, tm):
    t, d = x2.shape
    nsb = seq // tm
    row = lambda w: pl.BlockSpec((tm, w), lambda i: (i, 0))
    const = lambda shape: pl.BlockSpec(shape, lambda i: (0, 0), pipeline_mode=pl.Buffered(1))
    return pl.pallas_call(
        functools.partial(_outproj_kernel, sub=min(MXU_WIDTH, tm)),
        grid=(t // tm,),
        in_specs=[row(ATTN_WIDTH),
                  pl.BlockSpec((tm, SSM_WIDTH), lambda i: (i % nsb, i // nsb)),
                  row(XATTN_WIDTH), row(d),
                  const(w_o.shape), const((1, d)), const(wr.shape)],
        out_specs=[row(d), row(LANES), row(d // 2)],
        out_shape=[jax.ShapeDtypeStruct((t, d), F32), jax.ShapeDtypeStruct((t, LANES), F32),
                   jax.ShapeDtypeStruct((t, d // 2), jnp.uint32)],
        compiler_params=_cparams("arbitrary"),
        name="out_proj_router",
    )(ya, ys_sb, yx, x2, w_o, gain, wr)


def _route_kernel(lt_ref, tri_ref, dest_ref, w_ref, meta_ref, cnt_ref, carry_ref, pstart_ref):
    phase = pl.program_id(0)
    c = pl.program_id(1)
    logits = lt_ref[...].T
    tc = logits.shape[1]
    ng, epg = N_EXPERT_GROUPS, EXPERTS_PER_GROUP
    row8 = lax.broadcasted_iota(jnp.int32, (ng, tc), 0)

    gl = logits[0:ng]
    gmax = jnp.max(gl, axis=0, keepdims=True)
    gidx = jnp.min(jnp.where(gl == gmax, row8, ng), axis=0, keepdims=True)
    gate = 1.0 / jnp.sum(jnp.exp(gl - gmax), axis=0, keepdims=True)
    sel = jnp.zeros((epg, tc), F32)
    for g in range(ng):
        sel = jnp.where(gidx == g, logits[ng + g * epg:ng + (g + 1) * epg], sel)
    v1 = jnp.max(sel, axis=0, keepdims=True)
    i1 = jnp.min(jnp.where(sel == v1, row8, epg), axis=0, keepdims=True)
    sel2 = jnp.where(row8 == i1, -jnp.inf, sel)
    v2 = jnp.max(sel2, axis=0, keepdims=True)
    i2 = jnp.min(jnp.where(sel2 == v2, row8, epg), axis=0, keepdims=True)
    e = jnp.exp(v2 - v1)
    w1 = gate * (1.0 / (1.0 + e))
    w2 = gate * (e / (1.0 + e))
    e1 = gidx * epg + i1
    e2 = gidx * epg + i2
    rowe = lax.broadcasted_iota(jnp.int32, (N_EXPERTS, tc), 0)
    oh1 = rowe == e1
    oh2 = rowe == e2
    member = jnp.where(jnp.logical_or(oh1, oh2), 1.0, 0.0)
    chunk_cnt = jnp.sum(member, axis=1, keepdims=True)

    @pl.when(phase == 0)
    def _():
        @pl.when(c == 0)
        def _():
            cnt_ref[...] = jnp.zeros_like(cnt_ref)
        cnt_ref[...] += chunk_cnt

    @pl.when(phase == 1)
    def _():
        @pl.when(c == 0)
        def _():
            cnt = cnt_ref[...]
            nblk = jnp.floor((cnt + (MOE_BLOCK - 1)) * (1.0 / MOE_BLOCK))
            nchunk = jnp.floor((nblk + (CHUNK_BLOCKS - 1)) * (1.0 / CHUNK_BLOCKS))
            r = lax.broadcasted_iota(jnp.int32, (N_EXPERTS, LANES), 0)
            cidx = lax.broadcasted_iota(jnp.int32, (N_EXPERTS, LANES), 1)
            to_row = lambda col: jnp.sum(jnp.where(r == cidx, col, 0.0), axis=0, keepdims=True)
            cumsum_col = lambda col: jnp.sum(jnp.where(cidx <= r, to_row(col), 0.0), axis=1, keepdims=True)
            cumsum_row = lambda col: jnp.sum(jnp.where(r <= cidx, col, 0.0), axis=0, keepdims=True)
            bend = cumsum_col(nblk)
            bstart = bend - nblk
            cend = cumsum_col(nchunk)
            cstart = cend - nchunk
            pstart_ref[...] = bstart * MOE_BLOCK
            carry_ref[...] = jnp.zeros_like(carry_ref)
            lanef = lax.broadcasted_iota(jnp.int32, (1, LANES), 1).astype(F32)
            owner = jnp.minimum(jnp.sum(jnp.where(cend <= lanef, 1.0, 0.0), axis=0, keepdims=True),
                                N_EXPERTS - 1.0)
            own = r.astype(F32) == owner
            pick = lambda col: jnp.sum(jnp.where(own, col, 0.0), axis=0, keepdims=True)
            idx = lanef - pick(cstart)
            first = pick(bstart) + CHUNK_BLOCKS * idx
            size = jnp.clip(pick(nblk) - CHUNK_BLOCKS * idx, 0.0, float(CHUNK_BLOCKS))
            zero = jnp.zeros((1, LANES), F32)
            rows = [owner, first, size,
                    zero + jnp.sum(nchunk, axis=0, keepdims=True),
                    cumsum_row(nblk) - 1.0,
                    to_row(nblk),
                    zero + jnp.sum(nblk, axis=0, keepdims=True),
                    pick(nchunk)]
            for k, v in enumerate(rows):
                meta_ref[k:k + 1, :] = v.astype(jnp.int32)

        before = carry_ref[...] + jnp.dot(member.astype(BF16), tri_ref[...], preferred_element_type=F32)
        pos = before + pstart_ref[...]
        dest_ref[0:1, :] = jnp.sum(jnp.where(oh1, pos, 0.0), axis=0, keepdims=True).astype(jnp.int32)
        dest_ref[1:2, :] = jnp.sum(jnp.where(oh2, pos, 0.0), axis=0, keepdims=True).astype(jnp.int32)
        w_ref[0:1, :] = w1
        w_ref[1:2, :] = w2
        carry_ref[...] += chunk_cnt


def _route(logits_t, tc):
    t = logits_t.shape[0]
    nc = t // tc
    tri = jnp.asarray(np.triu(np.ones((tc, tc), np.float32), k=1), dtype=BF16)
    return pl.pallas_call(
        _route_kernel,
        grid=(2, nc),
        in_specs=[pl.BlockSpec((tc, LANES), lambda p, c: (c, 0)),
                  pl.BlockSpec((tc, tc), lambda p, c: (0, 0))],
        out_specs=[pl.BlockSpec((2, tc), lambda p, c: (0, c * p)),
                   pl.BlockSpec((2, tc), lambda p, c: (0, c * p)),
                   pl.BlockSpec((META_ROWS, LANES), lambda p, c: (0, 0))],
        out_shape=[jax.ShapeDtypeStruct((2, t), jnp.int32), jax.ShapeDtypeStruct((2, t), F32),
                   jax.ShapeDtypeStruct((META_ROWS, LANES), jnp.int32)],
        scratch_shapes=[pltpu.VMEM((N_EXPERTS, 1), F32)] * 3,
        compiler_params=_cparams("arbitrary", "arbitrary"),
        name="moe_route",
    )(logits_t, tri)


def _meta(meta_ref, row, lane=0):
    return meta_ref[row * LANES + lane]


def _fill_blocks(meta_ref, zbuf, dst_ref, sem, n_blocks, *, expert_tails):
    zbuf[...] = jnp.zeros_like(zbuf)
    n_used = _meta(meta_ref, M_NUSED)
    block = lambda b: pltpu.make_async_copy(zbuf, dst_ref.at[pl.ds(b * MOE_BLOCK, MOE_BLOCK), :], sem)

    def tails(fn):
        def body(e, carry):
            @pl.when(_meta(meta_ref, M_NBLK, e) > 0)
            def _():
                fn(block(_meta(meta_ref, M_LAST, e)))
            return carry
        lax.fori_loop(0, N_EXPERTS, body, 0)

    def unused(fn):
        def body(b, carry):
            fn(block(b))
            return carry
        lax.fori_loop(n_used, n_blocks, body, 0)

    for phase in (lambda cp: cp.start(), lambda cp: cp.wait()):
        if expert_tails:
            tails(phase)
        unused(phase)


def _dispatch_kernel(dest_ref, meta_ref, h_ref, xs_ref, zbuf, sem_z, sem, *, tokens, n_blocks):
    tm = h_ref.shape[0]

    @pl.when(pl.program_id(0) == 0)
    def _():
        _fill_blocks(meta_ref, zbuf, xs_ref, sem_z, n_blocks, expert_tails=True)

    base = pl.program_id(0) * tm

    for r in range(tm):
        for k in range(2):
            d = dest_ref[k * tokens + base + r]
            pltpu.make_async_copy(h_ref.at[pl.ds(r, 1), :], xs_ref.at[pl.ds(d, 1), :], sem).start()
    for k in range(2):
        pltpu.make_async_copy(h_ref, xs_ref.at[pl.ds(0, tm), :], sem).wait()


def _dispatch(dest_flat, meta_flat, h2p, n_blocks, tm):
    t, w = h2p.shape
    grid_spec = pltpu.PrefetchScalarGridSpec(
        num_scalar_prefetch=2,
        grid=(t // tm,),
        in_specs=[pl.BlockSpec((tm, w), lambda i, d, m: (i, 0))],
        out_specs=pl.BlockSpec(memory_space=pl.ANY),
        scratch_shapes=[pltpu.VMEM((MOE_BLOCK, w), h2p.dtype),
                        pltpu.SemaphoreType.DMA(()), pltpu.SemaphoreType.DMA(())],
    )
    return pl.pallas_call(
        functools.partial(_dispatch_kernel, tokens=t, n_blocks=n_blocks),
        grid_spec=grid_spec,
        out_shape=jax.ShapeDtypeStruct((n_blocks * MOE_BLOCK, w), h2p.dtype),
        compiler_params=_cparams("arbitrary"),
        name="moe_dispatch",
    )(dest_flat, meta_flat, h2p)


def _pack_rows(x):
    bits = lax.bitcast_convert_type(x, jnp.uint32)
    half = x.shape[1] // 2
    return (bits[:, half:] & jnp.uint32(0xFFFF0000)) | (bits[:, :half] >> 16)


def _unpack_halves(words):
    lo = lax.bitcast_convert_type(words << 16, F32)
    hi = lax.bitcast_convert_type(words & jnp.uint32(0xFFFF0000), F32)
    return lo, hi


def _unpack_rows(words):
    return jnp.concatenate(_unpack_halves(words), axis=-1).astype(BF16)


def _expert_kernel(meta_ref, xs_ref, wg_ref, wu_ref, wd_ref, yb_ref,
                   xbuf, ybuf, zbuf, wg_f32, wu_f32, wd_f32, wslot_ref,
                   sem_in, sem_out, sem_z, sem_w, *, n_blocks):
    n_chunks = _meta(meta_ref, M_NCHUNK)

    def weight_copies(k, s):
        e = _meta(meta_ref, M_OWNER, k)
        return [pltpu.make_async_copy(src.at[e], dst.at[s], sem_w.at[s])
                for src, dst in ((wg_ref, wg_f32), (wu_ref, wu_f32), (wd_ref, wd_f32))]

    def in_copy(k, s, nb):
        rows = nb * MOE_BLOCK
        src = xs_ref.at[pl.ds(_meta(meta_ref, M_FIRST, k) * MOE_BLOCK, rows), :]
        return pltpu.make_async_copy(src, xbuf.at[s, pl.ds(0, rows), :], sem_in.at[s])

    def out_copy(k, s, nb):
        rows = nb * MOE_BLOCK
        dst = yb_ref.at[pl.ds(_meta(meta_ref, M_FIRST, k) * MOE_BLOCK, rows), :]
        return pltpu.make_async_copy(ybuf.at[s, pl.ds(0, rows), :], dst, sem_out.at[s])

    def by_size(k, fn):
        for nb in range(1, CHUNK_BLOCKS + 1):
            pl.when(_meta(meta_ref, M_SIZE, k) == nb)(functools.partial(fn, nb))

    wslot_ref[0] = 0
    by_size(0, lambda nb: in_copy(0, 0, nb).start())
    for cp in weight_copies(0, 0):
        cp.start()

    def chunk_step(c, carry):
        slot = c % 2

        @pl.when(c + 1 < n_chunks)
        def _():
            by_size(c + 1, lambda nb: in_copy(c + 1, 1 - slot, nb).start())

        prev = jnp.maximum(c - 1, 0)
        new_expert = jnp.logical_or(c == 0, _meta(meta_ref, M_OWNER, c) != _meta(meta_ref, M_OWNER, prev))

        @pl.when(new_expert)
        def _():
            ws = jnp.where(c == 0, 0, 1 - wslot_ref[0])
            wslot_ref[0] = ws
            nxt = c + _meta(meta_ref, M_OWNER_CHUNKS, c)

            @pl.when(nxt < n_chunks)
            def _():
                for cp in weight_copies(nxt, 1 - ws):
                    cp.start()
            for cp in weight_copies(c, ws):
                cp.wait()

        @pl.when(c >= 2)
        def _():
            by_size(c - 2, lambda nb: out_copy(c - 2, slot, nb).wait())

        def compute(nb):
            rows = nb * MOE_BLOCK
            ws = wslot_ref[0]
            in_copy(c, slot, nb).wait()
            h = _unpack_rows(xbuf[slot, 0:rows, :])
            gate = jnp.dot(h, wg_f32[ws].astype(BF16), preferred_element_type=F32)
            up = jnp.dot(h, wu_f32[ws].astype(BF16), preferred_element_type=F32)
            act = (jax.nn.silu(gate) * up).astype(BF16)
            y = jnp.dot(act, wd_f32[ws].astype(BF16), preferred_element_type=F32)
            ybuf[slot, 0:rows, :] = _pack_rows(y.astype(BF16).astype(F32))
            out_copy(c, slot, nb).start()
        by_size(c, compute)
        return carry

    lax.fori_loop(0, n_chunks, chunk_step, 0)

    _fill_blocks(meta_ref, zbuf, yb_ref, sem_z, n_blocks, expert_tails=False)
    for back in (2, 1):
        @pl.when(n_chunks >= back)
        def _(back=back):
            k = n_chunks - back
            by_size(k, lambda nb: out_copy(k, k % 2, nb).wait())


def _experts(meta_flat, xs, w_gate, w_up, w_down, n_blocks):
    d, de = w_gate.shape[1], w_gate.shape[2]
    rows = CHUNK_BLOCKS * MOE_BLOCK
    assert (n_blocks + (CHUNK_BLOCKS - 1) * N_EXPERTS) // CHUNK_BLOCKS <= LANES

    hbm = pl.BlockSpec(memory_space=pl.ANY)
    grid_spec = pltpu.PrefetchScalarGridSpec(
        num_scalar_prefetch=1,
        grid=(1,),
        in_specs=[hbm, hbm, hbm, hbm],
        out_specs=hbm,
        scratch_shapes=[pltpu.VMEM((2, rows, xs.shape[1]), xs.dtype),
                        pltpu.VMEM((2, rows, d // 2), jnp.uint32),
                        pltpu.VMEM((MOE_BLOCK, d // 2), jnp.uint32),
                        pltpu.VMEM((2, d, de), F32), pltpu.VMEM((2, d, de), F32), pltpu.VMEM((2, de, d), F32),
                        pltpu.SMEM((1,), jnp.int32),
                        pltpu.SemaphoreType.DMA((2,)), pltpu.SemaphoreType.DMA((2,)),
                        pltpu.SemaphoreType.DMA(()), pltpu.SemaphoreType.DMA((2,))],
    )
    return pl.pallas_call(
        functools.partial(_expert_kernel, n_blocks=n_blocks),
        grid_spec=grid_spec,
        out_shape=jax.ShapeDtypeStruct((n_blocks * MOE_BLOCK, d // 2), jnp.uint32),
        compiler_params=_cparams("arbitrary"),
        name="moe_experts",
    )(meta_flat, xs, w_gate, w_up, w_down)


def _combine_kernel(dest_ref, yb_ref, x1_ref, w_ref, out_ref, gbuf, sem, *, tokens):
    i = pl.program_id(0)
    n = pl.num_programs(0)
    tm = x1_ref.shape[0]
    half = x1_ref.shape[1] // 2
    slot = i % COMBINE_SLOTS
    ahead = COMBINE_SLOTS - 1

    def issue_row(tile, r):
        s = tile % COMBINE_SLOTS
        for k in range(2):
            d = dest_ref[k * tokens + tile * tm + r]
            pltpu.make_async_copy(yb_ref.at[pl.ds(d, 1), :], gbuf.at[s, k, pl.ds(r, 1), :], sem.at[s]).start()

    def combine_rows(r0):
        rows = slice(r0, r0 + COMBINE_PIECE)
        w = w_ref[rows, :]
        lo0, hi0 = _unpack_halves(gbuf[slot, 0, rows, :])
        lo1, hi1 = _unpack_halves(gbuf[slot, 1, rows, :])
        out_ref[rows, :half] = x1_ref[rows, :half] + (lo0 * w[:, 0:1] + lo1 * w[:, 1:2])
        out_ref[rows, half:] = x1_ref[rows, half:] + (hi0 * w[:, 0:1] + hi1 * w[:, 1:2])

    @pl.when(i == 0)
    def _():
        for first in range(ahead):
            @pl.when(first < n)
            def _(first=first):
                def body(g, carry):
                    r8 = pl.multiple_of(g * SUBLANES, SUBLANES)
                    for sub in range(SUBLANES):
                        issue_row(first, r8 + sub)
                    return carry
                lax.fori_loop(0, tm // SUBLANES, body, 0)

    for k in range(2):
        pltpu.make_async_copy(yb_ref.at[pl.ds(0, tm), :], gbuf.at[slot, k], sem.at[slot]).wait()

    @pl.when(i + ahead < n)
    def _():
        for r0 in range(0, tm, COMBINE_PIECE):
            for r in range(r0, r0 + COMBINE_PIECE):
                issue_row(i + ahead, r)
            combine_rows(r0)

    @pl.when(i + ahead >= n)
    def _():
        for r0 in range(0, tm, COMBINE_PIECE):
            combine_rows(r0)


def _combine(dest_flat, yb, x1, w_tok, tm):
    t, d = x1.shape
    grid_spec = pltpu.PrefetchScalarGridSpec(
        num_scalar_prefetch=1,
        grid=(t // tm,),
        in_specs=[pl.BlockSpec(memory_space=pl.ANY),
                  pl.BlockSpec((tm, d), lambda i, dr: (i, 0)),
                  pl.BlockSpec((tm, 2), lambda i, dr: (i, 0))],
        out_specs=pl.BlockSpec((tm, d), lambda i, dr: (i, 0)),
        scratch_shapes=[pltpu.VMEM((COMBINE_SLOTS, 2, tm, d // 2), jnp.uint32),
                        pltpu.SemaphoreType.DMA((COMBINE_SLOTS,))],
    )
    return pl.pallas_call(
        functools.partial(_combine_kernel, tokens=t),
        grid_spec=grid_spec,
        out_shape=jax.ShapeDtypeStruct((t, d), F32),
        compiler_params=_cparams("arbitrary"),
        name="moe_combine",
    )(dest_flat, yb, x1, w_tok)


def _row(v):
    return v.astype(F32).reshape(1, -1)


def _layer(x2, mem2, batch, seq, mem_len, p):
    t, d = x2.shape

    bias = _bias_table(p["rel_bias"])
    col = lambda v: jnp.broadcast_to(v.astype(F32)[:, None], (v.shape[0], BLOCK))
    qgain = col(p["q_norm"]) * (1.0 / math.sqrt(ATTN_HEAD_DIM))
    kgain = jnp.tile(_row(p["k_norm"]), (1, ATTN_KV_HEADS))
    u_sb, qx, ya = _inproj_swa(x2, _row(p["norm_mix"]), p["w_in"].astype(F32), bias, p["attn_sinks"].astype(F32),
                               qgain, kgain, col(p["out_norm_attn"]), batch, seq, min(TOKEN_TILE, seq))

    km, vm = _memkv(mem2, _row(p["mem_norm"]), p["w_mem_kv"].astype(F32), _row(p["xk_norm"]),
                    min(MEM_TILE, mem2.shape[0]))
    xq_gain = _row(p["xq_norm"]) * (1.0 / math.sqrt(XATTN_HEAD_DIM))
    yx = _xattn(qx, km, vm, xq_gain, _row(p["out_norm_xattn"]), batch, seq, mem_len, min(XATTN_TILE, seq))

    a_re, a_im, bbr, bbi = _ssm_prep(p["ssm_lambda_re"], p["ssm_lambda_im"], p["ssm_log_dt"],
                                     p["ssm_b_re"], p["ssm_b_im"])
    bmat = jnp.concatenate([_block_diag_tiles(bbr), _block_diag_tiles(bbi)], axis=-1).astype(BF16)
    c_re_t = jnp.transpose(p["ssm_c_re"].astype(F32), (0, 2, 1))
    c_im_t = jnp.transpose(p["ssm_c_im"].astype(F32), (0, 2, 1))
    cmat = jnp.concatenate([_block_diag_tiles(c_re_t), _block_diag_tiles(-c_im_t)], axis=1).astype(BF16)
    steps = min(SSM_STEPS, seq)
    ys_sb = _ssm(u_sb, bmat, cmat,
                 a_re.reshape(1, SSM_STATES), a_im.reshape(1, SSM_STATES), _row(p["ssm_d"]),
                 p["ssm_w_glu"].astype(BF16), _row(p["out_norm_ssm"]), batch, seq, steps)

    wr = jnp.concatenate([p["w_router_group"], p["w_router_expert"]], axis=1).astype(F32)
    wr = jnp.pad(wr, ((0, 0), (0, LANES - wr.shape[1]))).astype(BF16)
    tm_out = min(TOKEN_TILE, seq)
    x1, logits_t, h2p = _outproj(ya, ys_sb, yx, x2, p["w_o"].astype(F32), _row(p["norm_ffn"]), wr,
                                 batch, seq, tm_out)

    dest, w_k, meta = _route(logits_t, min(ROUTE_CHUNK, t))
    n_blocks = (2 * t) // MOE_BLOCK + N_EXPERTS
    dest_flat = dest.reshape(2 * t)
    meta_flat = meta.reshape(META_ROWS * LANES)
    xs = _dispatch(dest_flat, meta_flat, h2p, n_blocks, min(DISPATCH_TILE, t))
    yb = _experts(meta_flat, xs, p["w_gate"], p["w_up"], p["w_down"], n_blocks)
    return _combine(dest_flat, yb, x1, w_k.T, min(TOKEN_TILE, t))


def kernel(x, mem, norm_mix, w_in, q_norm, k_norm, attn_sinks, rel_bias, ssm_lambda_re, ssm_lambda_im, ssm_log_dt, ssm_b_re, ssm_b_im, ssm_c_re, ssm_c_im, ssm_d, ssm_w_glu, mem_norm, w_mem_kv, xq_norm, xk_norm, out_norm_attn, out_norm_ssm, out_norm_xattn, w_o, norm_ffn, w_router_group, w_router_expert, w_gate, w_up, w_down):
    batch, seq, d = x.shape
    mem_len = mem.shape[1]
    per_layer = dict(norm_mix=norm_mix, w_in=w_in, q_norm=q_norm, k_norm=k_norm, attn_sinks=attn_sinks,
                     ssm_lambda_re=ssm_lambda_re, ssm_lambda_im=ssm_lambda_im, ssm_log_dt=ssm_log_dt,
                     ssm_b_re=ssm_b_re, ssm_b_im=ssm_b_im, ssm_c_re=ssm_c_re, ssm_c_im=ssm_c_im,
                     ssm_d=ssm_d, ssm_w_glu=ssm_w_glu, mem_norm=mem_norm, w_mem_kv=w_mem_kv,
                     xq_norm=xq_norm, xk_norm=xk_norm, out_norm_attn=out_norm_attn,
                     out_norm_ssm=out_norm_ssm, out_norm_xattn=out_norm_xattn, w_o=w_o, norm_ffn=norm_ffn,
                     w_router_group=w_router_group, w_router_expert=w_router_expert,
                     w_gate=w_gate, w_up=w_up, w_down=w_down)
    x2 = x.astype(F32).reshape(batch * seq, d)
    mem2 = mem.astype(F32).reshape(batch * mem_len, d)
    for l in range(norm_mix.shape[0]):
        p = {k: v[l] for k, v in per_layer.items()}
        p["rel_bias"] = rel_bias
        x2 = _layer(x2, mem2, batch, seq, mem_len, p)
    return x2.reshape(batch, seq, d).astype(x.dtype)
```

```python
import functools
import math

import numpy as np
import jax
import jax.numpy as jnp
from jax import lax
from jax.experimental import pallas as pl
from jax.experimental.pallas import tpu as pltpu

F32 = jnp.float32
BF16 = jnp.bfloat16
EPS = 1e-6

ATTN_HEADS = 16
ATTN_KV_HEADS = 2
ATTN_HEAD_DIM = 64
ATTN_WIDTH = ATTN_HEADS * ATTN_HEAD_DIM
WINDOW = 128
BLOCK = 128
REL_BUCKETS = 32
REL_MAX_DIST = 128
SSM_GROUP_CH = 16
SSM_GROUPS = 32
SSM_STATE = 64
SSM_WIDTH = SSM_GROUPS * SSM_GROUP_CH
XATTN_HEADS = 4
XATTN_HEAD_DIM = 128
XATTN_WIDTH = XATTN_HEADS * XATTN_HEAD_DIM
N_EXPERT_GROUPS = 8
EXPERTS_PER_GROUP = 8
N_EXPERTS = N_EXPERT_GROUPS * EXPERTS_PER_GROUP
MOE_BLOCK = 128

LANES = 128
SUBLANES = 8
MXU_WIDTH = 256
TOKEN_TILE = 512
XATTN_TILE = 1024
MEM_TILE = 512
SSM_STEPS = 128
ROUTE_CHUNK = 2048
DISPATCH_TILE = 2048
SSM_GROUPS_PER_TILE = LANES // SSM_GROUP_CH
SSM_TILES = SSM_WIDTH // LANES
SSM_TILE_STATE = SSM_GROUPS_PER_TILE * SSM_STATE
SSM_STATES = SSM_GROUPS * SSM_STATE
SSM_PIECES = 8
CHUNK_BLOCKS = 4
COMBINE_SLOTS = 3
COMBINE_PIECE = 32
META_ROWS = 8
M_OWNER, M_FIRST, M_SIZE, M_NCHUNK, M_LAST, M_NBLK, M_NUSED, M_OWNER_CHUNKS = range(8)
VMEM_LIMIT = 56 * 1024 * 1024

_NT = (((1,), (1,)), ((), ()))


def _cparams(*sem):
    return pltpu.CompilerParams(dimension_semantics=sem, vmem_limit_bytes=VMEM_LIMIT)


def _rms(x, gain):
    ms = jnp.mean(x * x, axis=-1, keepdims=True)
    return x * lax.rsqrt(ms + EPS) * gain


def _t5_bucket_table():
    qi = np.arange(BLOCK, dtype=np.int32)[:, None]
    ki = np.arange(2 * BLOCK, dtype=np.int32)[None, :]
    delta = BLOCK + qi - ki
    n = np.maximum(delta, 0)
    max_exact = REL_BUCKETS // 2
    nf = np.maximum(n, 1).astype(np.float32)
    large = max_exact + (np.log(nf / np.float32(max_exact)) / np.float32(math.log(REL_MAX_DIST / max_exact))
                         * np.float32(REL_BUCKETS - max_exact)).astype(np.int32)
    large = np.minimum(large, REL_BUCKETS - 1)
    return np.where(n < max_exact, n, large).astype(np.int32)


def _upper(cols):
    k = lax.broadcasted_iota(jnp.int32, (BLOCK, cols), 0)
    q = lax.broadcasted_iota(jnp.int32, (BLOCK, cols), 1) % BLOCK
    return k > q


def _bias_kernel(rb_ref, bucket_ref, out_ref):
    pair = pl.program_id(0)
    bucket = bucket_ref[...]
    upper = _upper(BLOCK)
    for half in range(2):
        acc = jnp.zeros(bucket.shape, F32)
        for b in range(REL_BUCKETS):
            acc = jnp.where(bucket == b, rb_ref[b, 2 * pair + half], acc)
        cols = slice(half * BLOCK, (half + 1) * BLOCK)
        out_ref[0, 0, :, cols] = jnp.where(upper, acc[:BLOCK], acc[BLOCK:])
        out_ref[1, 0, :, cols] = jnp.where(upper, jnp.float32(-1e30), acc[BLOCK:])


def _bias_table(rel_bias):
    assert WINDOW == BLOCK
    bucket = jnp.asarray(_t5_bucket_table().T)
    pairs = ATTN_HEADS // 2
    return pl.pallas_call(
        _bias_kernel,
        grid=(pairs,),
        in_specs=[pl.BlockSpec(memory_space=pltpu.SMEM),
                  pl.BlockSpec((2 * BLOCK, BLOCK), lambda h: (0, 0))],
        out_specs=pl.BlockSpec((2, 1, BLOCK, 2 * BLOCK), lambda h: (0, h, 0, 0)),
        out_shape=jax.ShapeDtypeStruct((2, pairs, BLOCK, 2 * BLOCK), F32),
        compiler_params=_cparams("arbitrary"),
        name="t5_bias_table",
    )(rel_bias.astype(F32), bucket)


def _ssm_prep_kernel(lr_ref, li_ref, ldt_ref, br_ref, bi_ref, are_ref, aim_ref, bbr_ref, bbi_ref):
    lr = lr_ref[...]
    li = li_ref[...]
    dt = jnp.exp(ldt_ref[...])
    mag = jnp.exp(lr * dt)
    a_re = mag * jnp.cos(li * dt)
    a_im = mag * jnp.sin(li * dt)
    den = lr * lr + li * li
    nr = a_re - 1.0
    ni = a_im
    coef_re = (nr * lr + ni * li) / den
    coef_im = (ni * lr - nr * li) / den
    are_ref[...] = a_re
    aim_ref[...] = a_im
    br = br_ref[...]
    bi = bi_ref[...]
    bbr_ref[...] = coef_re * br - coef_im * bi
    bbi_ref[...] = coef_re * bi + coef_im * br


def _ssm_prep(lam_re, lam_im, log_dt, b_re, b_im):
    g, n, c = b_re.shape
    vec = jax.ShapeDtypeStruct((g, 1, n), F32)
    mat = jax.ShapeDtypeStruct((g, c, n), F32)
    return pl.pallas_call(
        _ssm_prep_kernel,
        out_shape=(vec, vec, mat, mat),
        name="ssm_discretise",
    )(lam_re.astype(F32).reshape(g, 1, n), lam_im.astype(F32).reshape(g, 1, n),
      log_dt.astype(F32).reshape(g, 1, 1),
      jnp.transpose(b_re.astype(F32), (0, 2, 1)), jnp.transpose(b_im.astype(F32), (0, 2, 1)))


def _block_diag_tiles(m):
    g, r, c = m.shape
    t = g // SSM_GROUPS_PER_TILE
    eye = jnp.eye(SSM_GROUPS_PER_TILE, dtype=m.dtype)
    m4 = m.reshape(t, SSM_GROUPS_PER_TILE, r, c)
    out = m4[:, :, :, None, :] * eye[None, :, None, :, None]
    return out.reshape(t, SSM_GROUPS_PER_TILE * r, SSM_GROUPS_PER_TILE * c)


def _k_norm(k, kgain):
    dh = ATTN_HEAD_DIM
    lo = lax.broadcasted_iota(jnp.int32, (1, ATTN_KV_HEADS * dh), 1) < dh
    sq = k * k
    s_lo = jnp.sum(jnp.where(lo, sq, 0.0), axis=-1, keepdims=True)
    s_hi = jnp.sum(sq, axis=-1, keepdims=True) - s_lo
    return k * jnp.where(lo, lax.rsqrt(s_lo / dh + EPS), lax.rsqrt(s_hi / dh + EPS)) * kgain


def _swa_phases(q_of, kn, v_t, bias_of, sinks_ref, qgain, ogain, emit):
    dh = ATTN_HEAD_DIM
    upper = _upper(2 * BLOCK)
    first_head = lax.broadcasted_iota(jnp.int32, (1, 2 * BLOCK), 1) < BLOCK
    zeros = jnp.zeros((dh, BLOCK), F32)
    heads_per_kv = ATTN_HEADS // ATTN_KV_HEADS
    pairs = range(ATTN_HEADS // 2)
    kv_of = lambda pair: (2 * pair) // heads_per_kv
    state = {}

    def qk_all():
        logits = []
        for pair in pairs:
            cols = []
            for half in range(2):
                qh = q_of(2 * pair + half).astype(F32)
                ms = jnp.mean(qh * qh, axis=0, keepdims=True)
                qn = qh * lax.rsqrt(ms + EPS) * qgain
                cols.append(jnp.concatenate([qn, zeros] if kv_of(pair) == 0 else [zeros, qn], axis=0))
            rhs = jnp.concatenate(cols, axis=1).astype(BF16)
            logits.append(jnp.dot(kn, rhs, preferred_element_type=F32))
        state["logits"] = logits

    def softmax_all():
        probs = []
        for pair in pairs:
            both = state["logits"][pair]
            l = jnp.where(upper, both[:BLOCK], both[BLOCK:]) + bias_of(pair)
            sink = jnp.where(first_head, sinks_ref[2 * pair], sinks_ref[2 * pair + 1])
            m = jnp.maximum(jnp.max(l, axis=0, keepdims=True), sink)
            p = jnp.exp(l - m)
            den = jnp.sum(p, axis=0, keepdims=True) + jnp.exp(sink - m)
            pz = jnp.concatenate([jnp.where(upper, p, 0.0), jnp.where(upper, 0.0, p)], axis=0).astype(BF16)
            probs.append((pz, den))
        state["probs"] = probs

    def pv_all():
        outs = []
        for pair in pairs:
            g = kv_of(pair)
            pz, den = state["probs"][pair]
            o = jnp.dot(v_t[g * dh:(g + 1) * dh, :], pz, preferred_element_type=F32) / den
            outs += [o[:, :BLOCK], o[:, BLOCK:]]
        y_t = jnp.concatenate(outs, axis=0)
        ms = jnp.mean(y_t * y_t, axis=0, keepdims=True)
        emit((y_t * lax.rsqrt(ms + EPS) * ogain).T)

    return qk_all, softmax_all, pv_all


def _inproj_swa_kernel(sinks_ref, x_ref, g_ref, w_ref, bias_ref, qg_ref, kg_ref, og_ref,
                       u_ref, qx_ref, ya_ref, q_s, k_s, v_s, kprev_s, vprev_s, *, tiles_per_seq):
    i = pl.program_id(0)
    slot = i % 2
    old = 1 - slot
    tm = x_ref.shape[0]
    blocks = tm // BLOCK
    kvw = ATTN_KV_HEADS * ATTN_HEAD_DIM

    @pl.when(i == 0)
    def _():
        q_s[1] = jnp.zeros(q_s.shape[1:], q_s.dtype)
        k_s[1] = jnp.zeros(k_s.shape[1:], k_s.dtype)
        v_s[1] = jnp.zeros(v_s.shape[1:], v_s.dtype)
        kprev_s[...] = jnp.zeros_like(kprev_s)
        vprev_s[...] = jnp.zeros_like(vprev_s)

    h = _rms(x_ref[...], g_ref[...]).astype(BF16)
    proj = lambda c0, c1: jnp.dot(h, w_ref[:, c0:c1].astype(BF16), preferred_element_type=F32)
    piece = MXU_WIDTH

    def q_piece(n):
        def run():
            q_s[slot, n * piece:(n + 1) * piece, :] = proj(n * piece, (n + 1) * piece).T.astype(q_s.dtype)
        return run

    def kv_piece():
        kv = proj(ATTN_WIDTH, ATTN_WIDTH + 2 * kvw)
        k_s[slot] = _k_norm(kv[:, :kvw], kg_ref[...]).astype(k_s.dtype)
        v_s[slot] = kv[:, kvw:].T.astype(v_s.dtype)

    def out_piece(ref, c0, n):
        def run():
            ref[:, n * piece:(n + 1) * piece] = proj(c0 + n * piece, c0 + (n + 1) * piece).astype(ref.dtype)
        return run

    c_u = ATTN_WIDTH + 2 * kvw
    c_qx = c_u + SSM_WIDTH
    projection = ([q_piece(n) for n in range(ATTN_WIDTH // piece)] + [kv_piece]
                  + [out_piece(u_ref, c_u, n) for n in range(SSM_WIDTH // piece)]
                  + [out_piece(qx_ref, c_qx, n) for n in range(XATTN_WIDTH // piece)])

    first_of_seq = (i + tiles_per_seq - 1) % tiles_per_seq == 0
    attention = []
    for blk in range(blocks):
        own = slice(blk * BLOCK, (blk + 1) * BLOCK)
        before = slice((blk - 1) * BLOCK, blk * BLOCK)
        k_before = kprev_s[...] if blk == 0 else k_s[old, before, :]
        v_before = vprev_s[...] if blk == 0 else v_s[old, :, before]
        table = jnp.where(first_of_seq, 1, 0) if blk == 0 else 0

        def emit(y, own=own):
            ya_ref[own, :] = y.astype(ya_ref.dtype)

        attention += _swa_phases(
            q_of=lambda hd, own=own: q_s[old, hd * ATTN_HEAD_DIM:(hd + 1) * ATTN_HEAD_DIM, own],
            kn=jnp.concatenate([k_before, k_s[old, own, :]], axis=0),
            v_t=jnp.concatenate([v_before, v_s[old, :, own]], axis=1),
            bias_of=lambda pair, table=table: bias_ref[table, pair],
            sinks_ref=sinks_ref, qgain=qg_ref[...], ogain=og_ref[...], emit=emit)

    longer, shorter = (attention, projection) if len(attention) >= len(projection) else (projection, attention)
    done = 0
    for n, fn in enumerate(longer):
        fn()
        due = (n + 1) * len(shorter) // len(longer)
        for extra in shorter[done:due]:
            extra()
        done = due

    last = slice((blocks - 1) * BLOCK, blocks * BLOCK)
    kprev_s[...] = k_s[old, last, :]
    vprev_s[...] = v_s[old, :, last]


def _inproj_swa(x2, gain, w_in, bias, sinks, qgain, kgain, ogain, batch, seq, tm):
    t, d = x2.shape
    n_tiles = t // tm
    tiles_per_seq = seq // tm
    kvw = ATTN_KV_HEADS * ATTN_HEAD_DIM
    proj_tile = lambda i: jnp.minimum(i, n_tiles - 1)
    attn_tile = lambda i: jnp.maximum(i - 1, 0)
    const = lambda shape: pl.BlockSpec(shape, lambda i: (0,) * len(shape), pipeline_mode=pl.Buffered(1))
    return pl.pallas_call(
        functools.partial(_inproj_swa_kernel, tiles_per_seq=tiles_per_seq),
        grid=(n_tiles + 1,),
        in_specs=[pl.BlockSpec(memory_space=pltpu.SMEM),
                  pl.BlockSpec((tm, d), lambda i: (proj_tile(i), 0)),
                  const((1, d)), const(w_in.shape), const(bias.shape),
                  const((ATTN_HEAD_DIM, BLOCK)), const((1, kvw)), const((ATTN_WIDTH, BLOCK))],
        out_specs=[
                   pl.BlockSpec((tm, SSM_WIDTH),
                                lambda i: (proj_tile(i) % tiles_per_seq, proj_tile(i) // tiles_per_seq)),
                   pl.BlockSpec((tm, XATTN_WIDTH), lambda i: (proj_tile(i), 0)),
                   pl.BlockSpec((tm, ATTN_WIDTH), lambda i: (attn_tile(i), 0))],
        out_shape=[jax.ShapeDtypeStruct((seq, batch * SSM_WIDTH), BF16),
                   jax.ShapeDtypeStruct((t, XATTN_WIDTH), BF16),
                   jax.ShapeDtypeStruct((t, ATTN_WIDTH), BF16)],
        scratch_shapes=[pltpu.VMEM((2, ATTN_WIDTH, tm), BF16),
                        pltpu.VMEM((2, tm, kvw), BF16),
                        pltpu.VMEM((2, kvw, tm), BF16),
                        pltpu.VMEM((BLOCK, kvw), BF16), pltpu.VMEM((kvw, BLOCK), BF16)],
        compiler_params=_cparams("arbitrary"),
        name="in_proj_swa",
    )(sinks, x2, gain, w_in, bias, qgain, kgain, ogain)


def _memkv_kernel(m_ref, g_ref, w_ref, kg_ref, k_ref, v_ref):
    h = _rms(m_ref[...], g_ref[...]).astype(BF16)
    km = jnp.dot(h, w_ref[:, :XATTN_WIDTH].astype(BF16), preferred_element_type=F32)
    for hd in range(XATTN_HEADS):
        sl = slice(hd * XATTN_HEAD_DIM, (hd + 1) * XATTN_HEAD_DIM)
        k_ref[:, sl] = _rms(km[:, sl], kg_ref[...]).astype(k_ref.dtype)
    v_ref[...] = jnp.dot(h, w_ref[:, XATTN_WIDTH:].astype(BF16), preferred_element_type=F32).astype(v_ref.dtype)


def _memkv(mem2, gain, w_kv, kgain, tm):
    r, d = mem2.shape
    row = lambda w: pl.BlockSpec((tm, w), lambda i: (i, 0))
    const = lambda shape: pl.BlockSpec(shape, lambda i: (0, 0), pipeline_mode=pl.Buffered(1))
    return pl.pallas_call(
        _memkv_kernel,
        grid=(r // tm,),
        in_specs=[row(d), const((1, d)), const((d, 2 * XATTN_WIDTH)), const((1, XATTN_HEAD_DIM))],
        out_specs=[row(XATTN_WIDTH), row(XATTN_WIDTH)],
        out_shape=[jax.ShapeDtypeStruct((r, XATTN_WIDTH), BF16)] * 2,
        compiler_params=_cparams("arbitrary"),
        name="mem_kv_proj",
    )(mem2, gain, w_kv, kgain)


def _xattn_kernel(q_ref, k_ref, v_ref, qg_ref, og_ref, out_ref):
    outs = []
    for hd in range(XATTN_HEADS):
        sl = slice(hd * XATTN_HEAD_DIM, (hd + 1) * XATTN_HEAD_DIM)
        qn = _rms(q_ref[:, sl].astype(F32), qg_ref[...]).astype(BF16)
        l = lax.dot_general(qn, k_ref[:, sl], _NT, preferred_element_type=F32)
        m = jnp.max(l, axis=-1, keepdims=True)
        p = jnp.exp(l - m)
        den = jnp.sum(p, axis=-1, keepdims=True)
        outs.append(jnp.dot(p.astype(BF16), v_ref[:, sl], preferred_element_type=F32) / den)
    y = jnp.concatenate(outs, axis=-1)
    out_ref[...] = _rms(y, og_ref[...]).astype(out_ref.dtype)


def _xattn(qx, km, vm, qgain, ogain, batch, seq, mem_len, tq):
    nq = seq // tq
    const2 = lambda b, i: (0, 0)
    return pl.pallas_call(
        _xattn_kernel,
        grid=(batch, nq),
        in_specs=[pl.BlockSpec((tq, XATTN_WIDTH), lambda b, i: (b * nq + i, 0)),
                  pl.BlockSpec((mem_len, XATTN_WIDTH), lambda b, i: (b, 0)),
                  pl.BlockSpec((mem_len, XATTN_WIDTH), lambda b, i: (b, 0)),
                  pl.BlockSpec((1, XATTN_HEAD_DIM), const2),
                  pl.BlockSpec((1, XATTN_WIDTH), const2)],
        out_specs=pl.BlockSpec((tq, XATTN_WIDTH), lambda b, i: (b * nq + i, 0)),
        out_shape=jax.ShapeDtypeStruct((batch * seq, XATTN_WIDTH), BF16),
        compiler_params=_cparams("arbitrary", "arbitrary"),
        name="mem_xattn",
    )(qx, km, vm, qgain, ogain)


def _ssm_kernel(u_ref, bmat_ref, cmat_ref, are_ref, aim_ref, d_ref, wglu_ref, og_ref, out_ref,
                tb_ref, xr_ref, xi_ref, sr_ref, si_ref, *, batch, steps):
    @pl.when(pl.program_id(0) == 0)
    def _():
        sr_ref[...] = jnp.zeros_like(sr_ref)
        si_ref[...] = jnp.zeros_like(si_ref)

    for b in range(batch):
        for j in range(SSM_TILES):
            c0 = b * SSM_WIDTH + j * LANES
            tb_ref[j, pl.ds(b, steps, stride=batch), :] = u_ref[:, c0:c0 + LANES].astype(F32)
    uf = jnp.concatenate([tb_ref[j] for j in range(SSM_TILES)], axis=-1)
    u = uf.astype(BF16)
    total = steps * batch
    piece = total // SSM_PIECES
    tiles_per_half = SSM_TILES // 2
    half_states = SSM_STATES // 2

    def bu_piece(j, k):
        rows = slice(k * piece, (k + 1) * piece)
        bu = jnp.dot(u[rows, j * LANES:(j + 1) * LANES], bmat_ref[j], preferred_element_type=F32)
        xr_ref[rows, j * SSM_TILE_STATE:(j + 1) * SSM_TILE_STATE] = bu[:, :SSM_TILE_STATE]
        xi_ref[rows, j * SSM_TILE_STATE:(j + 1) * SSM_TILE_STATE] = bu[:, SSM_TILE_STATE:]

    def c_piece(j, k):
        rows = slice(k * piece, (k + 1) * piece)
        sl = slice(j * SSM_TILE_STATE, (j + 1) * SSM_TILE_STATE)
        xcat = jnp.concatenate([xr_ref[rows, sl], xi_ref[rows, sl]], axis=-1).astype(BF16)
        return jnp.dot(xcat, cmat_ref[j], preferred_element_type=F32)

    def scan_half(hf, between):
        cs = slice(hf * half_states, (hf + 1) * half_states)
        ar = jnp.broadcast_to(are_ref[:, cs], (batch, half_states))
        ai = jnp.broadcast_to(aim_ref[:, cs], (batch, half_states))
        s_r, s_i = sr_ref[:, cs], si_ref[:, cs]
        every = steps // len(between)
        for t in range(steps):
            rows = slice(t * batch, (t + 1) * batch)
            s_r, s_i = (ar * s_r - ai * s_i + xr_ref[rows, cs], ar * s_i + ai * s_r + xi_ref[rows, cs])
            xr_ref[rows, cs] = s_r
            xi_ref[rows, cs] = s_i
            if t % every == every - 1:
                between[t // every]()
        sr_ref[:, cs] = s_r
        si_ref[:, cs] = s_i

    first = [(j, k) for j in range(tiles_per_half) for k in range(SSM_PIECES)]
    second = [(j, k) for j in range(tiles_per_half, SSM_TILES) for k in range(SSM_PIECES)]
    for j, k in first:
        bu_piece(j, k)
    scan_half(0, [functools.partial(bu_piece, j, k) for j, k in second])
    y_pieces = {}
    scan_half(1, [functools.partial(lambda j, k: y_pieces.__setitem__((j, k), c_piece(j, k)), j, k)
                  for j, k in first])
    for j, k in second:
        y_pieces[(j, k)] = c_piece(j, k)
    ys = [jnp.concatenate([y_pieces[(j, k)] for k in range(SSM_PIECES)], axis=0) for j in range(SSM_TILES)]
    y = jnp.concatenate(ys, axis=-1) + d_ref[...] * uf
    y = jax.nn.gelu(y)
    y = y * jax.nn.sigmoid(jnp.dot(y.astype(BF16), wglu_ref[...], preferred_element_type=F32))
    y = _rms(y, og_ref[...])
    for j in range(SSM_TILES):
        tb_ref[j] = y[:, j * LANES:(j + 1) * LANES]
    for b in range(batch):
        for j in range(SSM_TILES):
            c0 = b * SSM_WIDTH + j * LANES
            out_ref[:, c0:c0 + LANES] = tb_ref[j, pl.ds(b, steps, stride=batch), :].astype(out_ref.dtype)


def _ssm(u_sb, bmat, cmat, a_re, a_im, d_skip, w_glu, ogain, batch, seq, steps):
    rows = steps * batch
    const2 = lambda c: (0, 0)
    const3 = lambda c: (0, 0, 0)
    return pl.pallas_call(
        functools.partial(_ssm_kernel, batch=batch, steps=steps),
        grid=(seq // steps,),
        in_specs=[pl.BlockSpec((steps, batch * SSM_WIDTH), lambda c: (c, 0)),
                  pl.BlockSpec(bmat.shape, const3), pl.BlockSpec(cmat.shape, const3),
                  pl.BlockSpec((1, SSM_STATES), const2), pl.BlockSpec((1, SSM_STATES), const2),
                  pl.BlockSpec((1, SSM_WIDTH), const2),
                  pl.BlockSpec((SSM_WIDTH, SSM_WIDTH), const2),
                  pl.BlockSpec((1, SSM_WIDTH), const2)],
        out_specs=pl.BlockSpec((steps, batch * SSM_WIDTH), lambda c: (c, 0)),
        out_shape=jax.ShapeDtypeStruct((seq, batch * SSM_WIDTH), BF16),
        scratch_shapes=[pltpu.VMEM((SSM_TILES, rows, LANES), F32),
                        pltpu.VMEM((rows, SSM_STATES), F32), pltpu.VMEM((rows, SSM_STATES), F32),
                        pltpu.VMEM((batch, SSM_STATES), F32), pltpu.VMEM((batch, SSM_STATES), F32)],
        compiler_params=_cparams("arbitrary"),
        name="s5_layer",
    )(u_sb, bmat, cmat, a_re, a_im, d_skip, w_glu, ogain)


def _outproj_kernel(ya_ref, ys_ref, yx_ref, x_ref, wo_ref, g_ref, wr_ref, x1_ref, lt_ref, hp_ref, *, sub):
    for r0 in range(0, x_ref.shape[0], sub):
        rows = slice(r0, r0 + sub)
        mix = jnp.concatenate([ya_ref[rows, :], ys_ref[rows, :], yx_ref[rows, :]], axis=-1)
        x1 = x_ref[rows, :] + jnp.dot(mix, wo_ref[...].astype(BF16), preferred_element_type=F32)
        x1_ref[rows, :] = x1
        h2 = _rms(x1, g_ref[...])
        hi = h2.astype(BF16)
        lt_ref[rows, :] = jnp.dot(hi, wr_ref[...], preferred_element_type=F32)
        hp_ref[rows, :] = _pack_rows(hi.astype(F32))


def _outproj(ya, ys_sb, yx, x2, w_o, gain, wr, batch, seq, tm):
    t, d = x2.shape
    nsb = seq // tm
    row = lambda w: pl.BlockSpec((tm, w), lambda i: (i, 0))
    const = lambda shape: pl.BlockSpec(shape, lambda i: (0, 0), pipeline_mode=pl.Buffered(1))
    return pl.pallas_call(
        functools.partial(_outproj_kernel, sub=min(MXU_WIDTH, tm)),
        grid=(t // tm,),
        in_specs=[row(ATTN_WIDTH),
                  pl.BlockSpec((tm, SSM_WIDTH), lambda i: (i % nsb, i // nsb)),
                  row(XATTN_WIDTH), row(d),
                  const(w_o.shape), const((1, d)), const(wr.shape)],
        out_specs=[row(d), row(LANES), row(d // 2)],
        out_shape=[jax.ShapeDtypeStruct((t, d), F32), jax.ShapeDtypeStruct((t, LANES), F32),
                   jax.ShapeDtypeStruct((t, d // 2), jnp.uint32)],
        compiler_params=_cparams("arbitrary"),
        name="out_proj_router",
    )(ya, ys_sb, yx, x2, w_o, gain, wr)


def _route_kernel(lt_ref, tri_ref, dest_ref, w_ref, meta_ref, cnt_ref, carry_ref, pstart_ref):
    phase = pl.program_id(0)
    c = pl.program_id(1)
    logits = lt_ref[...].T
    tc = logits.shape[1]
    ng, epg = N_EXPERT_GROUPS, EXPERTS_PER_GROUP
    row8 = lax.broadcasted_iota(jnp.int32, (ng, tc), 0)

    gl = logits[0:ng]
    gmax = jnp.max(gl, axis=0, keepdims=True)
    gidx = jnp.min(jnp.where(gl == gmax, row8, ng), axis=0, keepdims=True)
    gate = 1.0 / jnp.sum(jnp.exp(gl - gmax), axis=0, keepdims=True)
    sel = jnp.zeros((epg, tc), F32)
    for g in range(ng):
        sel = jnp.where(gidx == g, logits[ng + g * epg:ng + (g + 1) * epg], sel)
    v1 = jnp.max(sel, axis=0, keepdims=True)
    i1 = jnp.min(jnp.where(sel == v1, row8, epg), axis=0, keepdims=True)
    sel2 = jnp.where(row8 == i1, -jnp.inf, sel)
    v2 = jnp.max(sel2, axis=0, keepdims=True)
    i2 = jnp.min(jnp.where(sel2 == v2, row8, epg), axis=0, keepdims=True)
    e = jnp.exp(v2 - v1)
    w1 = gate * (1.0 / (1.0 + e))
    w2 = gate * (e / (1.0 + e))
    e1 = gidx * epg + i1
    e2 = gidx * epg + i2
    rowe = lax.broadcasted_iota(jnp.int32, (N_EXPERTS, tc), 0)
    oh1 = rowe == e1
    oh2 = rowe == e2
    member = jnp.where(jnp.logical_or(oh1, oh2), 1.0, 0.0)
    chunk_cnt = jnp.sum(member, axis=1, keepdims=True)

    @pl.when(phase == 0)
    def _():
        @pl.when(c == 0)
        def _():
            cnt_ref[...] = jnp.zeros_like(cnt_ref)
        cnt_ref[...] += chunk_cnt

    @pl.when(phase == 1)
    def _():
        @pl.when(c == 0)
        def _():
            cnt = cnt_ref[...]
            nblk = jnp.floor((cnt + (MOE_BLOCK - 1)) * (1.0 / MOE_BLOCK))
            nchunk = jnp.floor((nblk + (CHUNK_BLOCKS - 1)) * (1.0 / CHUNK_BLOCKS))
            r = lax.broadcasted_iota(jnp.int32, (N_EXPERTS, LANES), 0)
            cidx = lax.broadcasted_iota(jnp.int32, (N_EXPERTS, LANES), 1)
            to_row = lambda col: jnp.sum(jnp.where(r == cidx, col, 0.0), axis=0, keepdims=True)
            cumsum_col = lambda col: jnp.sum(jnp.where(cidx <= r, to_row(col), 0.0), axis=1, keepdims=True)
            cumsum_row = lambda col: jnp.sum(jnp.where(r <= cidx, col, 0.0), axis=0, keepdims=True)
            bend = cumsum_col(nblk)
            bstart = bend - nblk
            cend = cumsum_col(nchunk)
            cstart = cend - nchunk
            pstart_ref[...] = bstart * MOE_BLOCK
            carry_ref[...] = jnp.zeros_like(carry_ref)
            lanef = lax.broadcasted_iota(jnp.int32, (1, LANES), 1).astype(F32)
            owner = jnp.minimum(jnp.sum(jnp.where(cend <= lanef, 1.0, 0.0), axis=0, keepdims=True),
                                N_EXPERTS - 1.0)
            own = r.astype(F32) == owner
            pick = lambda col: jnp.sum(jnp.where(own, col, 0.0), axis=0, keepdims=True)
            idx = lanef - pick(cstart)
            first = pick(bstart) + CHUNK_BLOCKS * idx
            size = jnp.clip(pick(nblk) - CHUNK_BLOCKS * idx, 0.0, float(CHUNK_BLOCKS))
            zero = jnp.zeros((1, LANES), F32)
            rows = [owner, first, size,
                    zero + jnp.sum(nchunk, axis=0, keepdims=True),
                    cumsum_row(nblk) - 1.0,
                    to_row(nblk),
                    zero + jnp.sum(nblk, axis=0, keepdims=True),
                    pick(nchunk)]
            for k, v in enumerate(rows):
                meta_ref[k:k + 1, :] = v.astype(jnp.int32)

        before = carry_ref[...] + jnp.dot(member.astype(BF16), tri_ref[...], preferred_element_type=F32)
        pos = before + pstart_ref[...]
        dest_ref[0:1, :] = jnp.sum(jnp.where(oh1, pos, 0.0), axis=0, keepdims=True).astype(jnp.int32)
        dest_ref[1:2, :] = jnp.sum(jnp.where(oh2, pos, 0.0), axis=0, keepdims=True).astype(jnp.int32)
        w_ref[0:1, :] = w1
        w_ref[1:2, :] = w2
        carry_ref[...] += chunk_cnt


def _route(logits_t, tc):
    t = logits_t.shape[0]
    nc = t // tc
    tri = jnp.asarray(np.triu(np.ones((tc, tc), np.float32), k=1), dtype=BF16)
    return pl.pallas_call(
        _route_kernel,
        grid=(2, nc),
        in_specs=[pl.BlockSpec((tc, LANES), lambda p, c: (c, 0)),
                  pl.BlockSpec((tc, tc), lambda p, c: (0, 0))],
        out_specs=[pl.BlockSpec((2, tc), lambda p, c: (0, c * p)),
                   pl.BlockSpec((2, tc), lambda p, c: (0, c * p)),
                   pl.BlockSpec((META_ROWS, LANES), lambda p, c: (0, 0))],
        out_shape=[jax.ShapeDtypeStruct((2, t), jnp.int32), jax.ShapeDtypeStruct((2, t), F32),
                   jax.ShapeDtypeStruct((META_ROWS, LANES), jnp.int32)],
        scratch_shapes=[pltpu.VMEM((N_EXPERTS, 1), F32)] * 3,
        compiler_params=_cparams("arbitrary", "arbitrary"),
        name="moe_route",
    )(logits_t, tri)


def _meta(meta_ref, row, lane=0):
    return meta_ref[row * LANES + lane]


def _fill_blocks(meta_ref, zbuf, dst_ref, sem, n_blocks, *, expert_tails):
    zbuf[...] = jnp.zeros_like(zbuf)
    n_used = _meta(meta_ref, M_NUSED)
    block = lambda b: pltpu.make_async_copy(zbuf, dst_ref.at[pl.ds(b * MOE_BLOCK, MOE_BLOCK), :], sem)

    def tails(fn):
        def body(e, carry):
            @pl.when(_meta(meta_ref, M_NBLK, e) > 0)
            def _():
                fn(block(_meta(meta_ref, M_LAST, e)))
            return carry
        lax.fori_loop(0, N_EXPERTS, body, 0)

    def unused(fn):
        def body(b, carry):
            fn(block(b))
            return carry
        lax.fori_loop(n_used, n_blocks, body, 0)

    for phase in (lambda cp: cp.start(), lambda cp: cp.wait()):
        if expert_tails:
            tails(phase)
        unused(phase)


def _dispatch_kernel(dest_ref, meta_ref, h_ref, xs_ref, zbuf, sem_z, sem, *, tokens, n_blocks):
    tm = h_ref.shape[0]

    @pl.when(pl.program_id(0) == 0)
    def _():
        _fill_blocks(meta_ref, zbuf, xs_ref, sem_z, n_blocks, expert_tails=True)

    base = pl.program_id(0) * tm

    for r in range(tm):
        for k in range(2):
            d = dest_ref[k * tokens + base + r]
            pltpu.make_async_copy(h_ref.at[pl.ds(r, 1), :], xs_ref.at[pl.ds(d, 1), :], sem).start()
    for k in range(2):
        pltpu.make_async_copy(h_ref, xs_ref.at[pl.ds(0, tm), :], sem).wait()


def _dispatch(dest_flat, meta_flat, h2p, n_blocks, tm):
    t, w = h2p.shape
    grid_spec = pltpu.PrefetchScalarGridSpec(
        num_scalar_prefetch=2,
        grid=(t // tm,),
        in_specs=[pl.BlockSpec((tm, w), lambda i, d, m: (i, 0))],
        out_specs=pl.BlockSpec(memory_space=pl.ANY),
        scratch_shapes=[pltpu.VMEM((MOE_BLOCK, w), h2p.dtype),
                        pltpu.SemaphoreType.DMA(()), pltpu.SemaphoreType.DMA(())],
    )
    return pl.pallas_call(
        functools.partial(_dispatch_kernel, tokens=t, n_blocks=n_blocks),
        grid_spec=grid_spec,
        out_shape=jax.ShapeDtypeStruct((n_blocks * MOE_BLOCK, w), h2p.dtype),
        compiler_params=_cparams("arbitrary"),
        name="moe_dispatch",
    )(dest_flat, meta_flat, h2p)


def _pack_rows(x):
    bits = lax.bitcast_convert_type(x, jnp.uint32)
    half = x.shape[1] // 2
    return (bits[:, half:] & jnp.uint32(0xFFFF0000)) | (bits[:, :half] >> 16)


def _unpack_halves(words):
    lo = lax.bitcast_convert_type(words << 16, F32)
    hi = lax.bitcast_convert_type(words & jnp.uint32(0xFFFF0000), F32)
    return lo, hi


def _unpack_rows(words):
    return jnp.concatenate(_unpack_halves(words), axis=-1).astype(BF16)


def _expert_kernel(meta_ref, xs_ref, wg_ref, wu_ref, wd_ref, yb_ref,
                   xbuf, ybuf, zbuf, wg_f32, wu_f32, wd_f32, wslot_ref,
                   sem_in, sem_out, sem_z, sem_w, *, n_blocks):
    n_chunks = _meta(meta_ref, M_NCHUNK)

    def weight_copies(k, s):
        e = _meta(meta_ref, M_OWNER, k)
        return [pltpu.make_async_copy(src.at[e], dst.at[s], sem_w.at[s])
                for src, dst in ((wg_ref, wg_f32), (wu_ref, wu_f32), (wd_ref, wd_f32))]

    def in_copy(k, s, nb):
        rows = nb * MOE_BLOCK
        src = xs_ref.at[pl.ds(_meta(meta_ref, M_FIRST, k) * MOE_BLOCK, rows), :]
        return pltpu.make_async_copy(src, xbuf.at[s, pl.ds(0, rows), :], sem_in.at[s])

    def out_copy(k, s, nb):
        rows = nb * MOE_BLOCK
        dst = yb_ref.at[pl.ds(_meta(meta_ref, M_FIRST, k) * MOE_BLOCK, rows), :]
        return pltpu.make_async_copy(ybuf.at[s, pl.ds(0, rows), :], dst, sem_out.at[s])

    def by_size(k, fn):
        for nb in range(1, CHUNK_BLOCKS + 1):
            pl.when(_meta(meta_ref, M_SIZE, k) == nb)(functools.partial(fn, nb))

    wslot_ref[0] = 0
    by_size(0, lambda nb: in_copy(0, 0, nb).start())
    for cp in weight_copies(0, 0):
        cp.start()

    def chunk_step(c, carry):
        slot = c % 2

        @pl.when(c + 1 < n_chunks)
        def _():
            by_size(c + 1, lambda nb: in_copy(c + 1, 1 - slot, nb).start())

        prev = jnp.maximum(c - 1, 0)
        new_expert = jnp.logical_or(c == 0, _meta(meta_ref, M_OWNER, c) != _meta(meta_ref, M_OWNER, prev))

        @pl.when(new_expert)
        def _():
            ws = jnp.where(c == 0, 0, 1 - wslot_ref[0])
            wslot_ref[0] = ws
            nxt = c + _meta(meta_ref, M_OWNER_CHUNKS, c)

            @pl.when(nxt < n_chunks)
            def _():
                for cp in weight_copies(nxt, 1 - ws):
                    cp.start()
            for cp in weight_copies(c, ws):
                cp.wait()

        @pl.when(c >= 2)
        def _():
            by_size(c - 2, lambda nb: out_copy(c - 2, slot, nb).wait())

        def compute(nb):
            rows = nb * MOE_BLOCK
            ws = wslot_ref[0]
            in_copy(c, slot, nb).wait()
            h = _unpack_rows(xbuf[slot, 0:rows, :])
            gate = jnp.dot(h, wg_f32[ws].astype(BF16), preferred_element_type=F32)
            up = jnp.dot(h, wu_f32[ws].astype(BF16), preferred_element_type=F32)
            act = (jax.nn.silu(gate) * up).astype(BF16)
            y = jnp.dot(act, wd_f32[ws].astype(BF16), preferred_element_type=F32)
            ybuf[slot, 0:rows, :] = _pack_rows(y.astype(BF16).astype(F32))
            out_copy(c, slot, nb).start()
        by_size(c, compute)
        return carry

    lax.fori_loop(0, n_chunks, chunk_step, 0)

    _fill_blocks(meta_ref, zbuf, yb_ref, sem_z, n_blocks, expert_tails=False)
    for back in (2, 1):
        @pl.when(n_chunks >= back)
        def _(back=back):
            k = n_chunks - back
            by_size(k, lambda nb: out_copy(k, k % 2, nb).wait())


def _experts(meta_flat, xs, w_gate, w_up, w_down, n_blocks):
    d, de = w_gate.shape[1], w_gate.shape[2]
    rows = CHUNK_BLOCKS * MOE_BLOCK
    assert (n_blocks + (CHUNK_BLOCKS - 1) * N_EXPERTS) // CHUNK_BLOCKS <= LANES

    hbm = pl.BlockSpec(memory_space=pl.ANY)
    grid_spec = pltpu.PrefetchScalarGridSpec(
        num_scalar_prefetch=1,
        grid=(1,),
        in_specs=[hbm, hbm, hbm, hbm],
        out_specs=hbm,
        scratch_shapes=[pltpu.VMEM((2, rows, xs.shape[1]), xs.dtype),
                        pltpu.VMEM((2, rows, d // 2), jnp.uint32),
                        pltpu.VMEM((MOE_BLOCK, d // 2), jnp.uint32),
                        pltpu.VMEM((2, d, de), F32), pltpu.VMEM((2, d, de), F32), pltpu.VMEM((2, de, d), F32),
                        pltpu.SMEM((1,), jnp.int32),
                        pltpu.SemaphoreType.DMA((2,)), pltpu.SemaphoreType.DMA((2,)),
                        pltpu.SemaphoreType.DMA(()), pltpu.SemaphoreType.DMA((2,))],
    )
    return pl.pallas_call(
        functools.partial(_expert_kernel, n_blocks=n_blocks),
        grid_spec=grid_spec,
        out_shape=jax.ShapeDtypeStruct((n_blocks * MOE_BLOCK, d // 2), jnp.uint32),
        compiler_params=_cparams("arbitrary"),
        name="moe_experts",
    )(meta_flat, xs, w_gate, w_up, w_down)


def _combine_kernel(dest_ref, yb_ref, x1_ref, w_ref, out_ref, gbuf, sem, *, tokens):
    i = pl.program_id(0)
    n = pl.num_programs(0)
    tm = x1_ref.shape[0]
    half = x1_ref.shape[1] // 2
    slot = i % COMBINE_SLOTS
    ahead = COMBINE_SLOTS - 1

    def issue_row(tile, r):
        s = tile % COMBINE_SLOTS
        for k in range(2):
            d = dest_ref[k * tokens + tile * tm + r]
            pltpu.make_async_copy(yb_ref.at[pl.ds(d, 1), :], gbuf.at[s, k, pl.ds(r, 1), :], sem.at[s]).start()

    def combine_rows(r0):
        rows = slice(r0, r0 + COMBINE_PIECE)
        w = w_ref[rows, :]
        lo0, hi0 = _unpack_halves(gbuf[slot, 0, rows, :])
        lo1, hi1 = _unpack_halves(gbuf[slot, 1, rows, :])
        out_ref[rows, :half] = x1_ref[rows, :half] + (lo0 * w[:, 0:1] + lo1 * w[:, 1:2])
        out_ref[rows, half:] = x1_ref[rows, half:] + (hi0 * w[:, 0:1] + hi1 * w[:, 1:2])

    @pl.when(i == 0)
    def _():
        for first in range(ahead):
            @pl.when(first < n)
            def _(first=first):
                def body(g, carry):
                    r8 = pl.multiple_of(g * SUBLANES, SUBLANES)
                    for sub in range(SUBLANES):
                        issue_row(first, r8 + sub)
                    return carry
                lax.fori_loop(0, tm // SUBLANES, body, 0)

    for k in range(2):
        pltpu.make_async_copy(yb_ref.at[pl.ds(0, tm), :], gbuf.at[slot, k], sem.at[slot]).wait()

    @pl.when(i + ahead < n)
    def _():
        for r0 in range(0, tm, COMBINE_PIECE):
            for r in range(r0, r0 + COMBINE_PIECE):
                issue_row(i + ahead, r)
            combine_rows(r0)

    @pl.when(i + ahead >= n)
    def _():
        for r0 in range(0, tm, COMBINE_PIECE):
            combine_rows(r0)


def _combine(dest_flat, yb, x1, w_tok, tm):
    t, d = x1.shape
    grid_spec = pltpu.PrefetchScalarGridSpec(
        num_scalar_prefetch=1,
        grid=(t // tm,),
        in_specs=[pl.BlockSpec(memory_space=pl.ANY),
                  pl.BlockSpec((tm, d), lambda i, dr: (i, 0)),
                  pl.BlockSpec((tm, 2), lambda i, dr: (i, 0))],
        out_specs=pl.BlockSpec((tm, d), lambda i, dr: (i, 0)),
        scratch_shapes=[pltpu.VMEM((COMBINE_SLOTS, 2, tm, d // 2), jnp.uint32),
                        pltpu.SemaphoreType.DMA((COMBINE_SLOTS,))],
    )
    return pl.pallas_call(
        functools.partial(_combine_kernel, tokens=t),
        grid_spec=grid_spec,
        out_shape=jax.ShapeDtypeStruct((t, d), F32),
        compiler_params=_cparams("arbitrary"),
        name="moe_combine",
    )(dest_flat, yb, x1, w_tok)


def _row(v):
    return v.astype(F32).reshape(1, -1)


def _layer(x2, mem2, batch, seq, mem_len, p):
    t, d = x2.shape

    bias = _bias_table(p["rel_bias"])
    col = lambda v: jnp.broadcast_to(v.astype(F32)[:, None], (v.shape[0], BLOCK))
    qgain = col(p["q_norm"]) * (1.0 / math.sqrt(ATTN_HEAD_DIM))
    kgain = jnp.tile(_row(p["k_norm"]), (1, ATTN_KV_HEADS))
    u_sb, qx, ya = _inproj_swa(x2, _row(p["norm_mix"]), p["w_in"].astype(F32), bias, p["attn_sinks"].astype(F32),
                               qgain, kgain, col(p["out_norm_attn"]), batch, seq, min(TOKEN_TILE, seq))

    km, vm = _memkv(mem2, _row(p["mem_norm"]), p["w_mem_kv"].astype(F32), _row(p["xk_norm"]),
                    min(MEM_TILE, mem2.shape[0]))
    xq_gain = _row(p["xq_norm"]) * (1.0 / math.sqrt(XATTN_HEAD_DIM))
    yx = _xattn(qx, km, vm, xq_gain, _row(p["out_norm_xattn"]), batch, seq, mem_len, min(XATTN_TILE, seq))

    a_re, a_im, bbr, bbi = _ssm_prep(p["ssm_lambda_re"], p["ssm_lambda_im"], p["ssm_log_dt"],
                                     p["ssm_b_re"], p["ssm_b_im"])
    bmat = jnp.concatenate([_block_diag_tiles(bbr), _block_diag_tiles(bbi)], axis=-1).astype(BF16)
    c_re_t = jnp.transpose(p["ssm_c_re"].astype(F32), (0, 2, 1))
    c_im_t = jnp.transpose(p["ssm_c_im"].astype(F32), (0, 2, 1))
    cmat = jnp.concatenate([_block_diag_tiles(c_re_t), _block_diag_tiles(-c_im_t)], axis=1).astype(BF16)
    steps = min(SSM_STEPS, seq)
    ys_sb = _ssm(u_sb, bmat, cmat,
                 a_re.reshape(1, SSM_STATES), a_im.reshape(1, SSM_STATES), _row(p["ssm_d"]),
                 p["ssm_w_glu"].astype(BF16), _row(p["out_norm_ssm"]), batch, seq, steps)

    wr = jnp.concatenate([p["w_router_group"], p["w_router_expert"]], axis=1).astype(F32)
    wr = jnp.pad(wr, ((0, 0), (0, LANES - wr.shape[1]))).astype(BF16)
    tm_out = min(TOKEN_TILE, seq)
    x1, logits_t, h2p = _outproj(ya, ys_sb, yx, x2, p["w_o"].astype(F32), _row(p["norm_ffn"]), wr,
                                 batch, seq, tm_out)

    dest, w_k, meta = _route(logits_t, min(ROUTE_CHUNK, t))
    n_blocks = (2 * t) // MOE_BLOCK + N_EXPERTS
    dest_flat = dest.reshape(2 * t)
    meta_flat = meta.reshape(META_ROWS * LANES)
    xs = _dispatch(dest_flat, meta_flat, h2p, n_blocks, min(DISPATCH_TILE, t))
    yb = _experts(meta_flat, xs, p["w_gate"], p["w_up"], p["w_down"], n_blocks)
    return _combine(dest_flat, yb, x1, w_k.T, min(TOKEN_TILE, t))


def kernel(x, mem, norm_mix, w_in, q_norm, k_norm, attn_sinks, rel_bias, ssm_lambda_re, ssm_lambda_im, ssm_log_dt, ssm_b_re, ssm_b_im, ssm_c_re, ssm_c_im, ssm_d, ssm_w_glu, mem_norm, w_mem_kv, xq_norm, xk_norm, out_norm_attn, out_norm_ssm, out_norm_xattn, w_o, norm_ffn, w_router_group, w_router_expert, w_gate, w_up, w_down):
    batch, seq, d = x.shape
    mem_len = mem.shape[1]
    per_layer = dict(norm_mix=norm_mix, w_in=w_in, q_norm=q_norm, k_norm=k_norm, attn_sinks=attn_sinks,
                     ssm_lambda_re=ssm_lambda_re, ssm_lambda_im=ssm_lambda_im, ssm_log_dt=ssm_log_dt,
                     ssm_b_re=ssm_b_re, ssm_b_im=ssm_b_im, ssm_c_re=ssm_c_re, ssm_c_im=ssm_c_im,
                     ssm_d=ssm_d, ssm_w_glu=ssm_w_glu, mem_norm=mem_norm, w_mem_kv=w_mem_kv,
                     xq_norm=xq_norm, xk_norm=xk_norm, out_norm_attn=out_norm_attn,
                     out_norm_ssm=out_norm_ssm, out_norm_xattn=out_norm_xattn, w_o=w_o, norm_ffn=norm_ffn,
                     w_router_group=w_router_group, w_router_expert=w_router_expert,
                     w_gate=w_gate, w_up=w_up, w_down=w_down)
    x2 = x.astype(F32).reshape(batch * seq, d)
    mem2 = mem.astype(F32).reshape(batch * mem_len, d)
    for l in range(norm_mix.shape[0]):
        p = {k: v[l] for k, v in per_layer.items()}
        p["rel_bias"] = rel_bias
        x2 = _layer(x2, mem2, batch, seq, mem_len, p)
    return x2.reshape(batch, seq, d).astype(x.dtype)
```

```python
import functools
import math

import numpy as np
import jax
import jax.numpy as jnp
from jax import lax
from jax.experimental import pallas as pl
from jax.experimental.pallas import tpu as pltpu

F32 = jnp.float32
BF16 = jnp.bfloat16
EPS = 1e-6

ATTN_HEADS = 16
ATTN_KV_HEADS = 2
ATTN_HEAD_DIM = 64
ATTN_WIDTH = ATTN_HEADS * ATTN_HEAD_DIM
WINDOW = 128
BLOCK = 128
REL_BUCKETS = 32
REL_MAX_DIST = 128
SSM_GROUP_CH = 16
SSM_GROUPS = 32
SSM_STATE = 64
SSM_WIDTH = SSM_GROUPS * SSM_GROUP_CH
XATTN_HEADS = 4
XATTN_HEAD_DIM = 128
XATTN_WIDTH = XATTN_HEADS * XATTN_HEAD_DIM
N_EXPERT_GROUPS = 8
EXPERTS_PER_GROUP = 8
N_EXPERTS = N_EXPERT_GROUPS * EXPERTS_PER_GROUP
MOE_BLOCK = 128

LANES = 128
SUBLANES = 8
MXU_WIDTH = 256
TOKEN_TILE = 512
XATTN_TILE = 1024
MEM_TILE = 512
SSM_STEPS = 128
ROUTE_CHUNK = 2048
DISPATCH_TILE = 1024
SSM_GROUPS_PER_TILE = LANES // SSM_GROUP_CH
SSM_TILES = SSM_WIDTH // LANES
SSM_TILE_STATE = SSM_GROUPS_PER_TILE * SSM_STATE
SSM_STATES = SSM_GROUPS * SSM_STATE
SSM_PIECES = 8
CHUNK_BLOCKS = 4
COMBINE_SLOTS = 3
COMBINE_PIECE = 32
META_ROWS = 8
M_OWNER, M_FIRST, M_SIZE, M_NCHUNK, M_LAST, M_NBLK, M_NUSED, M_OWNER_CHUNKS = range(8)
VMEM_LIMIT = 56 * 1024 * 1024

_NT = (((1,), (1,)), ((), ()))


def _cparams(*sem):
    return pltpu.CompilerParams(dimension_semantics=sem, vmem_limit_bytes=VMEM_LIMIT)


def _rms(x, gain):
    ms = jnp.mean(x * x, axis=-1, keepdims=True)
    return x * lax.rsqrt(ms + EPS) * gain


def _t5_bucket_table():
    qi = np.arange(BLOCK, dtype=np.int32)[:, None]
    ki = np.arange(2 * BLOCK, dtype=np.int32)[None, :]
    delta = BLOCK + qi - ki
    n = np.maximum(delta, 0)
    max_exact = REL_BUCKETS // 2
    nf = np.maximum(n, 1).astype(np.float32)
    large = max_exact + (np.log(nf / np.float32(max_exact)) / np.float32(math.log(REL_MAX_DIST / max_exact))
                         * np.float32(REL_BUCKETS - max_exact)).astype(np.int32)
    large = np.minimum(large, REL_BUCKETS - 1)
    return np.where(n < max_exact, n, large).astype(np.int32)


def _upper(cols):
    k = lax.broadcasted_iota(jnp.int32, (BLOCK, cols), 0)
    q = lax.broadcasted_iota(jnp.int32, (BLOCK, cols), 1) % BLOCK
    return k > q


def _bias_kernel(rb_ref, bucket_ref, out_ref):
    pair = pl.program_id(0)
    bucket = bucket_ref[...]
    upper = _upper(BLOCK)
    for half in range(2):
        acc = jnp.zeros(bucket.shape, F32)
        for b in range(REL_BUCKETS):
            acc = jnp.where(bucket == b, rb_ref[b, 2 * pair + half], acc)
        cols = slice(half * BLOCK, (half + 1) * BLOCK)
        out_ref[0, 0, :, cols] = jnp.where(upper, acc[:BLOCK], acc[BLOCK:])
        out_ref[1, 0, :, cols] = jnp.where(upper, jnp.float32(-1e30), acc[BLOCK:])


def _bias_table(rel_bias):
    assert WINDOW == BLOCK
    bucket = jnp.asarray(_t5_bucket_table().T)
    pairs = ATTN_HEADS // 2
    return pl.pallas_call(
        _bias_kernel,
        grid=(pairs,),
        in_specs=[pl.BlockSpec(memory_space=pltpu.SMEM),
                  pl.BlockSpec((2 * BLOCK, BLOCK), lambda h: (0, 0))],
        out_specs=pl.BlockSpec((2, 1, BLOCK, 2 * BLOCK), lambda h: (0, h, 0, 0)),
        out_shape=jax.ShapeDtypeStruct((2, pairs, BLOCK, 2 * BLOCK), F32),
        compiler_params=_cparams("arbitrary"),
        name="t5_bias_table",
    )(rel_bias.astype(F32), bucket)


def _ssm_prep_kernel(lr_ref, li_ref, ldt_ref, br_ref, bi_ref, are_ref, aim_ref, bbr_ref, bbi_ref):
    lr = lr_ref[...]
    li = li_ref[...]
    dt = jnp.exp(ldt_ref[...])
    mag = jnp.exp(lr * dt)
    a_re = mag * jnp.cos(li * dt)
    a_im = mag * jnp.sin(li * dt)
    den = lr * lr + li * li
    nr = a_re - 1.0
    ni = a_im
    coef_re = (nr * lr + ni * li) / den
    coef_im = (ni * lr - nr * li) / den
    are_ref[...] = a_re
    aim_ref[...] = a_im
    br = br_ref[...]
    bi = bi_ref[...]
    bbr_ref[...] = coef_re * br - coef_im * bi
    bbi_ref[...] = coef_re * bi + coef_im * br


def _ssm_prep(lam_re, lam_im, log_dt, b_re, b_im):
    g, n, c = b_re.shape
    vec = jax.ShapeDtypeStruct((g, 1, n), F32)
    mat = jax.ShapeDtypeStruct((g, c, n), F32)
    return pl.pallas_call(
        _ssm_prep_kernel,
        out_shape=(vec, vec, mat, mat),
        name="ssm_discretise",
    )(lam_re.astype(F32).reshape(g, 1, n), lam_im.astype(F32).reshape(g, 1, n),
      log_dt.astype(F32).reshape(g, 1, 1),
      jnp.transpose(b_re.astype(F32), (0, 2, 1)), jnp.transpose(b_im.astype(F32), (0, 2, 1)))


def _block_diag_tiles(m):
    g, r, c = m.shape
    t = g // SSM_GROUPS_PER_TILE
    eye = jnp.eye(SSM_GROUPS_PER_TILE, dtype=m.dtype)
    m4 = m.reshape(t, SSM_GROUPS_PER_TILE, r, c)
    out = m4[:, :, :, None, :] * eye[None, :, None, :, None]
    return out.reshape(t, SSM_GROUPS_PER_TILE * r, SSM_GROUPS_PER_TILE * c)


def _k_norm(k, kgain):
    dh = ATTN_HEAD_DIM
    lo = lax.broadcasted_iota(jnp.int32, (1, ATTN_KV_HEADS * dh), 1) < dh
    sq = k * k
    s_lo = jnp.sum(jnp.where(lo, sq, 0.0), axis=-1, keepdims=True)
    s_hi = jnp.sum(sq, axis=-1, keepdims=True) - s_lo
    return k * jnp.where(lo, lax.rsqrt(s_lo / dh + EPS), lax.rsqrt(s_hi / dh + EPS)) * kgain


def _swa_phases(q_of, kn, v_t, bias_of, sinks_ref, qgain, ogain, emit):
    dh = ATTN_HEAD_DIM
    upper = _upper(2 * BLOCK)
    first_head = lax.broadcasted_iota(jnp.int32, (1, 2 * BLOCK), 1) < BLOCK
    zeros = jnp.zeros((dh, BLOCK), F32)
    heads_per_kv = ATTN_HEADS // ATTN_KV_HEADS
    pairs = range(ATTN_HEADS // 2)
    kv_of = lambda pair: (2 * pair) // heads_per_kv
    state = {}

    def qk_all():
        logits = []
        for pair in pairs:
            cols = []
            for half in range(2):
                qh = q_of(2 * pair + half).astype(F32)
                ms = jnp.mean(qh * qh, axis=0, keepdims=True)
                qn = qh * lax.rsqrt(ms + EPS) * qgain
                cols.append(jnp.concatenate([qn, zeros] if kv_of(pair) == 0 else [zeros, qn], axis=0))
            rhs = jnp.concatenate(cols, axis=1).astype(BF16)
            logits.append(jnp.dot(kn, rhs, preferred_element_type=F32))
        state["logits"] = logits

    def softmax_all():
        probs = []
        for pair in pairs:
            both = state["logits"][pair]
            l = jnp.where(upper, both[:BLOCK], both[BLOCK:]) + bias_of(pair)
            sink = jnp.where(first_head, sinks_ref[2 * pair], sinks_ref[2 * pair + 1])
            m = jnp.maximum(jnp.max(l, axis=0, keepdims=True), sink)
            p = jnp.exp(l - m)
            den = jnp.sum(p, axis=0, keepdims=True) + jnp.exp(sink - m)
            pz = jnp.concatenate([jnp.where(upper, p, 0.0), jnp.where(upper, 0.0, p)], axis=0).astype(BF16)
            probs.append((pz, den))
        state["probs"] = probs

    def pv_all():
        outs = []
        for pair in pairs:
            g = kv_of(pair)
            pz, den = state["probs"][pair]
            o = jnp.dot(v_t[g * dh:(g + 1) * dh, :], pz, preferred_element_type=F32) / den
            outs += [o[:, :BLOCK], o[:, BLOCK:]]
        y_t = jnp.concatenate(outs, axis=0)
        ms = jnp.mean(y_t * y_t, axis=0, keepdims=True)
        emit((y_t * lax.rsqrt(ms + EPS) * ogain).T)

    return qk_all, softmax_all, pv_all


def _inproj_swa_kernel(sinks_ref, x_ref, g_ref, w_ref, bias_ref, qg_ref, kg_ref, og_ref,
                       u_ref, qx_ref, ya_ref, q_s, k_s, v_s, kprev_s, vprev_s, *, tiles_per_seq):
    i = pl.program_id(0)
    slot = i % 2
    old = 1 - slot
    tm = x_ref.shape[0]
    blocks = tm // BLOCK
    kvw = ATTN_KV_HEADS * ATTN_HEAD_DIM

    @pl.when(i == 0)
    def _():
        q_s[1] = jnp.zeros(q_s.shape[1:], q_s.dtype)
        k_s[1] = jnp.zeros(k_s.shape[1:], k_s.dtype)
        v_s[1] = jnp.zeros(v_s.shape[1:], v_s.dtype)
        kprev_s[...] = jnp.zeros_like(kprev_s)
        vprev_s[...] = jnp.zeros_like(vprev_s)

    h = _rms(x_ref[...], g_ref[...]).astype(BF16)
    proj = lambda c0, c1: jnp.dot(h, w_ref[:, c0:c1].astype(BF16), preferred_element_type=F32)
    piece = MXU_WIDTH

    def q_piece(n):
        def run():
            q_s[slot, n * piece:(n + 1) * piece, :] = proj(n * piece, (n + 1) * piece).T.astype(q_s.dtype)
        return run

    def kv_piece():
        kv = proj(ATTN_WIDTH, ATTN_WIDTH + 2 * kvw)
        k_s[slot] = _k_norm(kv[:, :kvw], kg_ref[...]).astype(k_s.dtype)
        v_s[slot] = kv[:, kvw:].T.astype(v_s.dtype)

    def out_piece(ref, c0, n):
        def run():
            ref[:, n * piece:(n + 1) * piece] = proj(c0 + n * piece, c0 + (n + 1) * piece).astype(ref.dtype)
        return run

    c_u = ATTN_WIDTH + 2 * kvw
    c_qx = c_u + SSM_WIDTH
    projection = ([q_piece(n) for n in range(ATTN_WIDTH // piece)] + [kv_piece]
                  + [out_piece(u_ref, c_u, n) for n in range(SSM_WIDTH // piece)]
                  + [out_piece(qx_ref, c_qx, n) for n in range(XATTN_WIDTH // piece)])

    first_of_seq = (i + tiles_per_seq - 1) % tiles_per_seq == 0
    attention = []
    for blk in range(blocks):
        own = slice(blk * BLOCK, (blk + 1) * BLOCK)
        before = slice((blk - 1) * BLOCK, blk * BLOCK)
        k_before = kprev_s[...] if blk == 0 else k_s[old, before, :]
        v_before = vprev_s[...] if blk == 0 else v_s[old, :, before]
        table = jnp.where(first_of_seq, 1, 0) if blk == 0 else 0

        def emit(y, own=own):
            ya_ref[own, :] = y.astype(ya_ref.dtype)

        attention += _swa_phases(
            q_of=lambda hd, own=own: q_s[old, hd * ATTN_HEAD_DIM:(hd + 1) * ATTN_HEAD_DIM, own],
            kn=jnp.concatenate([k_before, k_s[old, own, :]], axis=0),
            v_t=jnp.concatenate([v_before, v_s[old, :, own]], axis=1),
            bias_of=lambda pair, table=table: bias_ref[table, pair],
            sinks_ref=sinks_ref, qgain=qg_ref[...], ogain=og_ref[...], emit=emit)

    longer, shorter = (attention, projection) if len(attention) >= len(projection) else (projection, attention)
    done = 0
    for n, fn in enumerate(longer):
        fn()
        due = (n + 1) * len(shorter) // len(longer)
        for extra in shorter[done:due]:
            extra()
        done = due

    last = slice((blocks - 1) * BLOCK, blocks * BLOCK)
    kprev_s[...] = k_s[old, last, :]
    vprev_s[...] = v_s[old, :, last]


def _inproj_swa(x2, gain, w_in, bias, sinks, qgain, kgain, ogain, batch, seq, tm):
    t, d = x2.shape
    n_tiles = t // tm
    tiles_per_seq = seq // tm
    kvw = ATTN_KV_HEADS * ATTN_HEAD_DIM
    proj_tile = lambda i: jnp.minimum(i, n_tiles - 1)
    attn_tile = lambda i: jnp.maximum(i - 1, 0)
    const = lambda shape: pl.BlockSpec(shape, lambda i: (0,) * len(shape), pipeline_mode=pl.Buffered(1))
    return pl.pallas_call(
        functools.partial(_inproj_swa_kernel, tiles_per_seq=tiles_per_seq),
        grid=(n_tiles + 1,),
        in_specs=[pl.BlockSpec(memory_space=pltpu.SMEM),
                  pl.BlockSpec((tm, d), lambda i: (proj_tile(i), 0)),
                  const((1, d)), const(w_in.shape), const(bias.shape),
                  const((ATTN_HEAD_DIM, BLOCK)), const((1, kvw)), const((ATTN_WIDTH, BLOCK))],
        out_specs=[
                   pl.BlockSpec((tm, SSM_WIDTH),
                                lambda i: (proj_tile(i) % tiles_per_seq, proj_tile(i) // tiles_per_seq)),
                   pl.BlockSpec((tm, XATTN_WIDTH), lambda i: (proj_tile(i), 0)),
                   pl.BlockSpec((tm, ATTN_WIDTH), lambda i: (attn_tile(i), 0))],
        out_shape=[jax.ShapeDtypeStruct((seq, batch * SSM_WIDTH), BF16),
                   jax.ShapeDtypeStruct((t, XATTN_WIDTH), BF16),
                   jax.ShapeDtypeStruct((t, ATTN_WIDTH), BF16)],
        scratch_shapes=[pltpu.VMEM((2, ATTN_WIDTH, tm), BF16),
                        pltpu.VMEM((2, tm, kvw), BF16),
                        pltpu.VMEM((2, kvw, tm), BF16),
                        pltpu.VMEM((BLOCK, kvw), BF16), pltpu.VMEM((kvw, BLOCK), BF16)],
        compiler_params=_cparams("arbitrary"),
        name="in_proj_swa",
    )(sinks, x2, gain, w_in, bias, qgain, kgain, ogain)


def _memkv_kernel(m_ref, g_ref, w_ref, kg_ref, k_ref, v_ref):
    h = _rms(m_ref[...], g_ref[...]).astype(BF16)
    km = jnp.dot(h, w_ref[:, :XATTN_WIDTH].astype(BF16), preferred_element_type=F32)
    for hd in range(XATTN_HEADS):
        sl = slice(hd * XATTN_HEAD_DIM, (hd + 1) * XATTN_HEAD_DIM)
        k_ref[:, sl] = _rms(km[:, sl], kg_ref[...]).astype(k_ref.dtype)
    v_ref[...] = jnp.dot(h, w_ref[:, XATTN_WIDTH:].astype(BF16), preferred_element_type=F32).astype(v_ref.dtype)


def _memkv(mem2, gain, w_kv, kgain, tm):
    r, d = mem2.shape
    row = lambda w: pl.BlockSpec((tm, w), lambda i: (i, 0))
    const = lambda shape: pl.BlockSpec(shape, lambda i: (0, 0), pipeline_mode=pl.Buffered(1))
    return pl.pallas_call(
        _memkv_kernel,
        grid=(r // tm,),
        in_specs=[row(d), const((1, d)), const((d, 2 * XATTN_WIDTH)), const((1, XATTN_HEAD_DIM))],
        out_specs=[row(XATTN_WIDTH), row(XATTN_WIDTH)],
        out_shape=[jax.ShapeDtypeStruct((r, XATTN_WIDTH), BF16)] * 2,
        compiler_params=_cparams("arbitrary"),
        name="mem_kv_proj",
    )(mem2, gain, w_kv, kgain)


def _xattn_kernel(q_ref, k_ref, v_ref, qg_ref, og_ref, out_ref):
    outs = []
    for hd in range(XATTN_HEADS):
        sl = slice(hd * XATTN_HEAD_DIM, (hd + 1) * XATTN_HEAD_DIM)
        qn = _rms(q_ref[:, sl].astype(F32), qg_ref[...]).astype(BF16)
        l = lax.dot_general(qn, k_ref[:, sl], _NT, preferred_element_type=F32)
        m = jnp.max(l, axis=-1, keepdims=True)
        p = jnp.exp(l - m)
        den = jnp.sum(p, axis=-1, keepdims=True)
        outs.append(jnp.dot(p.astype(BF16), v_ref[:, sl], preferred_element_type=F32) / den)
    y = jnp.concatenate(outs, axis=-1)
    out_ref[...] = _rms(y, og_ref[...]).astype(out_ref.dtype)


def _xattn(qx, km, vm, qgain, ogain, batch, seq, mem_len, tq):
    nq = seq // tq
    const2 = lambda b, i: (0, 0)
    return pl.pallas_call(
        _xattn_kernel,
        grid=(batch, nq),
        in_specs=[pl.BlockSpec((tq, XATTN_WIDTH), lambda b, i: (b * nq + i, 0)),
                  pl.BlockSpec((mem_len, XATTN_WIDTH), lambda b, i: (b, 0)),
                  pl.BlockSpec((mem_len, XATTN_WIDTH), lambda b, i: (b, 0)),
                  pl.BlockSpec((1, XATTN_HEAD_DIM), const2),
                  pl.BlockSpec((1, XATTN_WIDTH), const2)],
        out_specs=pl.BlockSpec((tq, XATTN_WIDTH), lambda b, i: (b * nq + i, 0)),
        out_shape=jax.ShapeDtypeStruct((batch * seq, XATTN_WIDTH), BF16),
        compiler_params=_cparams("arbitrary", "arbitrary"),
        name="mem_xattn",
    )(qx, km, vm, qgain, ogain)


def _ssm_kernel(u_ref, bmat_ref, cmat_ref, are_ref, aim_ref, d_ref, wglu_ref, og_ref, out_ref,
                tb_ref, xr_ref, xi_ref, sr_ref, si_ref, *, batch, steps):
    @pl.when(pl.program_id(0) == 0)
    def _():
        sr_ref[...] = jnp.zeros_like(sr_ref)
        si_ref[...] = jnp.zeros_like(si_ref)

    for b in range(batch):
        for j in range(SSM_TILES):
            c0 = b * SSM_WIDTH + j * LANES
            tb_ref[j, pl.ds(b, steps, stride=batch), :] = u_ref[:, c0:c0 + LANES].astype(F32)
    uf = jnp.concatenate([tb_ref[j] for j in range(SSM_TILES)], axis=-1)
    u = uf.astype(BF16)
    total = steps * batch
    piece = total // SSM_PIECES
    tiles_per_half = SSM_TILES // 2
    half_states = SSM_STATES // 2

    def bu_piece(j, k):
        rows = slice(k * piece, (k + 1) * piece)
        bu = jnp.dot(u[rows, j * LANES:(j + 1) * LANES], bmat_ref[j], preferred_element_type=F32)
        xr_ref[rows, j * SSM_TILE_STATE:(j + 1) * SSM_TILE_STATE] = bu[:, :SSM_TILE_STATE]
        xi_ref[rows, j * SSM_TILE_STATE:(j + 1) * SSM_TILE_STATE] = bu[:, SSM_TILE_STATE:]

    def c_piece(j, k):
        rows = slice(k * piece, (k + 1) * piece)
        sl = slice(j * SSM_TILE_STATE, (j + 1) * SSM_TILE_STATE)
        xcat = jnp.concatenate([xr_ref[rows, sl], xi_ref[rows, sl]], axis=-1).astype(BF16)
        return jnp.dot(xcat, cmat_ref[j], preferred_element_type=F32)

    def scan_half(hf, between):
        cs = slice(hf * half_states, (hf + 1) * half_states)
        ar = jnp.broadcast_to(are_ref[:, cs], (batch, half_states))
        ai = jnp.broadcast_to(aim_ref[:, cs], (batch, half_states))
        s_r, s_i = sr_ref[:, cs], si_ref[:, cs]
        every = steps // len(between)
        for t in range(steps):
            rows = slice(t * batch, (t + 1) * batch)
            s_r, s_i = (ar * s_r - ai * s_i + xr_ref[rows, cs], ar * s_i + ai * s_r + xi_ref[rows, cs])
            xr_ref[rows, cs] = s_r
            xi_ref[rows, cs] = s_i
            if t % every == every - 1:
                between[t // every]()
        sr_ref[:, cs] = s_r
        si_ref[:, cs] = s_i

    first = [(j, k) for j in range(tiles_per_half) for k in range(SSM_PIECES)]
    second = [(j, k) for j in range(tiles_per_half, SSM_TILES) for k in range(SSM_PIECES)]
    for j, k in first:
        bu_piece(j, k)
    scan_half(0, [functools.partial(bu_piece, j, k) for j, k in second])
    y_pieces = {}
    scan_half(1, [functools.partial(lambda j, k: y_pieces.__setitem__((j, k), c_piece(j, k)), j, k)
                  for j, k in first])
    for j, k in second:
        y_pieces[(j, k)] = c_piece(j, k)
    ys = [jnp.concatenate([y_pieces[(j, k)] for k in range(SSM_PIECES)], axis=0) for j in range(SSM_TILES)]
    y = jnp.concatenate(ys, axis=-1) + d_ref[...] * uf
    y = jax.nn.gelu(y)
    y = y * jax.nn.sigmoid(jnp.dot(y.astype(BF16), wglu_ref[...], preferred_element_type=F32))
    y = _rms(y, og_ref[...])
    for j in range(SSM_TILES):
        tb_ref[j] = y[:, j * LANES:(j + 1) * LANES]
    for b in range(batch):
        for j in range(SSM_TILES):
            c0 = b * SSM_WIDTH + j * LANES
            out_ref[:, c0:c0 + LANES] = tb_ref[j, pl.ds(b, steps, stride=batch), :].astype(out_ref.dtype)


def _ssm(u_sb, bmat, cmat, a_re, a_im, d_skip, w_glu, ogain, batch, seq, steps):
    rows = steps * batch
    const2 = lambda c: (0, 0)
    const3 = lambda c: (0, 0, 0)
    return pl.pallas_call(
        functools.partial(_ssm_kernel, batch=batch, steps=steps),
        grid=(seq // steps,),
        in_specs=[pl.BlockSpec((steps, batch * SSM_WIDTH), lambda c: (c, 0)),
                  pl.BlockSpec(bmat.shape, const3), pl.BlockSpec(cmat.shape, const3),
                  pl.BlockSpec((1, SSM_STATES), const2), pl.BlockSpec((1, SSM_STATES), const2),
                  pl.BlockSpec((1, SSM_WIDTH), const2),
                  pl.BlockSpec((SSM_WIDTH, SSM_WIDTH), const2),
                  pl.BlockSpec((1, SSM_WIDTH), const2)],
        out_specs=pl.BlockSpec((steps, batch * SSM_WIDTH), lambda c: (c, 0)),
        out_shape=jax.ShapeDtypeStruct((seq, batch * SSM_WIDTH), BF16),
        scratch_shapes=[pltpu.VMEM((SSM_TILES, rows, LANES), F32),
                        pltpu.VMEM((rows, SSM_STATES), F32), pltpu.VMEM((rows, SSM_STATES), F32),
                        pltpu.VMEM((batch, SSM_STATES), F32), pltpu.VMEM((batch, SSM_STATES), F32)],
        compiler_params=_cparams("arbitrary"),
        name="s5_layer",
    )(u_sb, bmat, cmat, a_re, a_im, d_skip, w_glu, ogain)


def _outproj_kernel(ya_ref, ys_ref, yx_ref, x_ref, wo_ref, g_ref, wr_ref, x1_ref, lt_ref, hp_ref, *, sub):
    for r0 in range(0, x_ref.shape[0], sub):
        rows = slice(r0, r0 + sub)
        mix = jnp.concatenate([ya_ref[rows, :], ys_ref[rows, :], yx_ref[rows, :]], axis=-1)
        x1 = x_ref[rows, :] + jnp.dot(mix, wo_ref[...].astype(BF16), preferred_element_type=F32)
        x1_ref[rows, :] = x1
        h2 = _rms(x1, g_ref[...])
        hi = h2.astype(BF16)
        lt_ref[rows, :] = jnp.dot(hi, wr_ref[...], preferred_element_type=F32)
        hp_ref[rows, :] = _pack_rows(hi.astype(F32))


def _outproj(ya, ys_sb, yx, x2, w_o, gain, wr, batch, seq, tm):
    t, d = x2.shape
    nsb = seq // tm
    row = lambda w: pl.BlockSpec((tm, w), lambda i: (i, 0))
    const = lambda shape: pl.BlockSpec(shape, lambda i: (0, 0), pipeline_mode=pl.Buffered(1))
    return pl.pallas_call(
        functools.partial(_outproj_kernel, sub=min(MXU_WIDTH, tm)),
        grid=(t // tm,),
        in_specs=[row(ATTN_WIDTH),
                  pl.BlockSpec((tm, SSM_WIDTH), lambda i: (i % nsb, i // nsb)),
                  row(XATTN_WIDTH), row(d),
                  const(w_o.shape), const((1, d)), const(wr.shape)],
        out_specs=[row(d), row(LANES), row(d // 2)],
        out_shape=[jax.ShapeDtypeStruct((t, d), F32), jax.ShapeDtypeStruct((t, LANES), F32),
                   jax.ShapeDtypeStruct((t, d // 2), jnp.uint32)],
        compiler_params=_cparams("arbitrary"),
        name="out_proj_router",
    )(ya, ys_sb, yx, x2, w_o, gain, wr)


def _route_kernel(lt_ref, tri_ref, dest_ref, w_ref, meta_ref, cnt_ref, carry_ref, pstart_ref):
    phase = pl.program_id(0)
    c = pl.program_id(1)
    logits = lt_ref[...].T
    tc = logits.shape[1]
    ng, epg = N_EXPERT_GROUPS, EXPERTS_PER_GROUP
    row8 = lax.broadcasted_iota(jnp.int32, (ng, tc), 0)

    gl = logits[0:ng]
    gmax = jnp.max(gl, axis=0, keepdims=True)
    gidx = jnp.min(jnp.where(gl == gmax, row8, ng), axis=0, keepdims=True)
    gate = 1.0 / jnp.sum(jnp.exp(gl - gmax), axis=0, keepdims=True)
    sel = jnp.zeros((epg, tc), F32)
    for g in range(ng):
        sel = jnp.where(gidx == g, logits[ng + g * epg:ng + (g + 1) * epg], sel)
    v1 = jnp.max(sel, axis=0, keepdims=True)
    i1 = jnp.min(jnp.where(sel == v1, row8, epg), axis=0, keepdims=True)
    sel2 = jnp.where(row8 == i1, -jnp.inf, sel)
    v2 = jnp.max(sel2, axis=0, keepdims=True)
    i2 = jnp.min(jnp.where(sel2 == v2, row8, epg), axis=0, keepdims=True)
    e = jnp.exp(v2 - v1)
    w1 = gate * (1.0 / (1.0 + e))
    w2 = gate * (e / (1.0 + e))
    e1 = gidx * epg + i1
    e2 = gidx * epg + i2
    rowe = lax.broadcasted_iota(jnp.int32, (N_EXPERTS, tc), 0)
    oh1 = rowe == e1
    oh2 = rowe == e2
    member = jnp.where(jnp.logical_or(oh1, oh2), 1.0, 0.0)
    chunk_cnt = jnp.sum(member, axis=1, keepdims=True)

    @pl.when(phase == 0)
    def _():
        @pl.when(c == 0)
        def _():
            cnt_ref[...] = jnp.zeros_like(cnt_ref)
        cnt_ref[...] += chunk_cnt

    @pl.when(phase == 1)
    def _():
        @pl.when(c == 0)
        def _():
            cnt = cnt_ref[...]
            nblk = jnp.floor((cnt + (MOE_BLOCK - 1)) * (1.0 / MOE_BLOCK))
            nchunk = jnp.floor((nblk + (CHUNK_BLOCKS - 1)) * (1.0 / CHUNK_BLOCKS))
            r = lax.broadcasted_iota(jnp.int32, (N_EXPERTS, LANES), 0)
            cidx = lax.broadcasted_iota(jnp.int32, (N_EXPERTS, LANES), 1)
            to_row = lambda col: jnp.sum(jnp.where(r == cidx, col, 0.0), axis=0, keepdims=True)
            cumsum_col = lambda col: jnp.sum(jnp.where(cidx <= r, to_row(col), 0.0), axis=1, keepdims=True)
            cumsum_row = lambda col: jnp.sum(jnp.where(r <= cidx, col, 0.0), axis=0, keepdims=True)
            bend = cumsum_col(nblk)
            bstart = bend - nblk
            cend = cumsum_col(nchunk)
            cstart = cend - nchunk
            pstart_ref[...] = bstart * MOE_BLOCK
            carry_ref[...] = jnp.zeros_like(carry_ref)
            lanef = lax.broadcasted_iota(jnp.int32, (1, LANES), 1).astype(F32)
            owner = jnp.minimum(jnp.sum(jnp.where(cend <= lanef, 1.0, 0.0), axis=0, keepdims=True),
                                N_EXPERTS - 1.0)
            own = r.astype(F32) == owner
            pick = lambda col: jnp.sum(jnp.where(own, col, 0.0), axis=0, keepdims=True)
            idx = lanef - pick(cstart)
            first = pick(bstart) + CHUNK_BLOCKS * idx
            size = jnp.clip(pick(nblk) - CHUNK_BLOCKS * idx, 0.0, float(CHUNK_BLOCKS))
            zero = jnp.zeros((1, LANES), F32)
            rows = [owner, first, size,
                    zero + jnp.sum(nchunk, axis=0, keepdims=True),
                    cumsum_row(nblk) - 1.0,
                    to_row(nblk),
                    zero + jnp.sum(nblk, axis=0, keepdims=True),
                    pick(nchunk)]
            for k, v in enumerate(rows):
                meta_ref[k:k + 1, :] = v.astype(jnp.int32)

        before = carry_ref[...] + jnp.dot(member.astype(BF16), tri_ref[...], preferred_element_type=F32)
        pos = before + pstart_ref[...]
        dest_ref[0:1, :] = jnp.sum(jnp.where(oh1, pos, 0.0), axis=0, keepdims=True).astype(jnp.int32)
        dest_ref[1:2, :] = jnp.sum(jnp.where(oh2, pos, 0.0), axis=0, keepdims=True).astype(jnp.int32)
        w_ref[0:1, :] = w1
        w_ref[1:2, :] = w2
        carry_ref[...] += chunk_cnt


def _route(logits_t, tc):
    t = logits_t.shape[0]
    nc = t // tc
    tri = jnp.asarray(np.triu(np.ones((tc, tc), np.float32), k=1), dtype=BF16)
    return pl.pallas_call(
        _route_kernel,
        grid=(2, nc),
        in_specs=[pl.BlockSpec((tc, LANES), lambda p, c: (c, 0)),
                  pl.BlockSpec((tc, tc), lambda p, c: (0, 0))],
        out_specs=[pl.BlockSpec((2, tc), lambda p, c: (0, c * p)),
                   pl.BlockSpec((2, tc), lambda p, c: (0, c * p)),
                   pl.BlockSpec((META_ROWS, LANES), lambda p, c: (0, 0))],
        out_shape=[jax.ShapeDtypeStruct((2, t), jnp.int32), jax.ShapeDtypeStruct((2, t), F32),
                   jax.ShapeDtypeStruct((META_ROWS, LANES), jnp.int32)],
        scratch_shapes=[pltpu.VMEM((N_EXPERTS, 1), F32)] * 3,
        compiler_params=_cparams("arbitrary", "arbitrary"),
        name="moe_route",
    )(logits_t, tri)


def _meta(meta_ref, row, lane=0):
    return meta_ref[row * LANES + lane]


def _fill_blocks(meta_ref, zbuf, dst_ref, sem, n_blocks, *, expert_tails):
    zbuf[...] = jnp.zeros_like(zbuf)
    n_used = _meta(meta_ref, M_NUSED)
    block = lambda b: pltpu.make_async_copy(zbuf, dst_ref.at[pl.ds(b * MOE_BLOCK, MOE_BLOCK), :], sem)

    def tails(fn):
        def body(e, carry):
            @pl.when(_meta(meta_ref, M_NBLK, e) > 0)
            def _():
                fn(block(_meta(meta_ref, M_LAST, e)))
            return carry
        lax.fori_loop(0, N_EXPERTS, body, 0)

    def unused(fn):
        def body(b, carry):
            fn(block(b))
            return carry
        lax.fori_loop(n_used, n_blocks, body, 0)

    for phase in (lambda cp: cp.start(), lambda cp: cp.wait()):
        if expert_tails:
            tails(phase)
        unused(phase)


def _dispatch_kernel(dest_ref, meta_ref, h_ref, xs_ref, zbuf, sem_z, sem, *, tokens, n_blocks):
    tm = h_ref.shape[0]

    @pl.when(pl.program_id(0) == 0)
    def _():
        _fill_blocks(meta_ref, zbuf, xs_ref, sem_z, n_blocks, expert_tails=True)

    base = pl.program_id(0) * tm

    for r in range(tm):
        for k in range(2):
            d = dest_ref[k * tokens + base + r]
            pltpu.make_async_copy(h_ref.at[pl.ds(r, 1), :], xs_ref.at[pl.ds(d, 1), :], sem).start(priority=k)
    for k in range(2):
        pltpu.make_async_copy(h_ref, xs_ref.at[pl.ds(0, tm), :], sem).wait()


def _dispatch(dest_flat, meta_flat, h2p, n_blocks, tm):
    t, w = h2p.shape
    grid_spec = pltpu.PrefetchScalarGridSpec(
        num_scalar_prefetch=2,
        grid=(t // tm,),
        in_specs=[pl.BlockSpec((tm, w), lambda i, d, m: (i, 0))],
        out_specs=pl.BlockSpec(memory_space=pl.ANY),
        scratch_shapes=[pltpu.VMEM((MOE_BLOCK, w), h2p.dtype),
                        pltpu.SemaphoreType.DMA(()), pltpu.SemaphoreType.DMA(())],
    )
    return pl.pallas_call(
        functools.partial(_dispatch_kernel, tokens=t, n_blocks=n_blocks),
        grid_spec=grid_spec,
        out_shape=jax.ShapeDtypeStruct((n_blocks * MOE_BLOCK, w), h2p.dtype),
        compiler_params=_cparams("arbitrary"),
        name="moe_dispatch",
    )(dest_flat, meta_flat, h2p)


def _pack_rows(x):
    bits = lax.bitcast_convert_type(x, jnp.uint32)
    half = x.shape[1] // 2
    return (bits[:, half:] & jnp.uint32(0xFFFF0000)) | (bits[:, :half] >> 16)


def _unpack_halves(words):
    lo = lax.bitcast_convert_type(words << 16, F32)
    hi = lax.bitcast_convert_type(words & jnp.uint32(0xFFFF0000), F32)
    return lo, hi


def _unpack_rows(words):
    return jnp.concatenate(_unpack_halves(words), axis=-1).astype(BF16)


def _expert_kernel(meta_ref, xs_ref, wg_ref, wu_ref, wd_ref, yb_ref,
                   xbuf, ybuf, zbuf, wg_f32, wu_f32, wd_f32, wslot_ref,
                   sem_in, sem_out, sem_z, sem_w, *, n_blocks):
    n_chunks = _meta(meta_ref, M_NCHUNK)

    def weight_copies(k, s):
        e = _meta(meta_ref, M_OWNER, k)
        return [pltpu.make_async_copy(src.at[e], dst.at[s], sem_w.at[s])
                for src, dst in ((wg_ref, wg_f32), (wu_ref, wu_f32), (wd_ref, wd_f32))]

    def in_copy(k, s, nb):
        rows = nb * MOE_BLOCK
        src = xs_ref.at[pl.ds(_meta(meta_ref, M_FIRST, k) * MOE_BLOCK, rows), :]
        return pltpu.make_async_copy(src, xbuf.at[s, pl.ds(0, rows), :], sem_in.at[s])

    def out_copy(k, s, nb):
        rows = nb * MOE_BLOCK
        dst = yb_ref.at[pl.ds(_meta(meta_ref, M_FIRST, k) * MOE_BLOCK, rows), :]
        return pltpu.make_async_copy(ybuf.at[s, pl.ds(0, rows), :], dst, sem_out.at[s])

    def by_size(k, fn):
        for nb in range(1, CHUNK_BLOCKS + 1):
            pl.when(_meta(meta_ref, M_SIZE, k) == nb)(functools.partial(fn, nb))

    wslot_ref[0] = 0
    by_size(0, lambda nb: in_copy(0, 0, nb).start())
    for cp in weight_copies(0, 0):
        cp.start()

    def chunk_step(c, carry):
        slot = c % 2

        @pl.when(c + 1 < n_chunks)
        def _():
            by_size(c + 1, lambda nb: in_copy(c + 1, 1 - slot, nb).start())

        prev = jnp.maximum(c - 1, 0)
        new_expert = jnp.logical_or(c == 0, _meta(meta_ref, M_OWNER, c) != _meta(meta_ref, M_OWNER, prev))

        @pl.when(new_expert)
        def _():
            ws = jnp.where(c == 0, 0, 1 - wslot_ref[0])
            wslot_ref[0] = ws
            nxt = c + _meta(meta_ref, M_OWNER_CHUNKS, c)

            @pl.when(nxt < n_chunks)
            def _():
                for cp in weight_copies(nxt, 1 - ws):
                    cp.start()
            for cp in weight_copies(c, ws):
                cp.wait()

        @pl.when(c >= 2)
        def _():
            by_size(c - 2, lambda nb: out_copy(c - 2, slot, nb).wait())

        def compute(nb):
            rows = nb * MOE_BLOCK
            ws = wslot_ref[0]
            in_copy(c, slot, nb).wait()
            h = _unpack_rows(xbuf[slot, 0:rows, :])
            gate = jnp.dot(h, wg_f32[ws].astype(BF16), preferred_element_type=F32)
            up = jnp.dot(h, wu_f32[ws].astype(BF16), preferred_element_type=F32)
            act = (jax.nn.silu(gate) * up).astype(BF16)
            y = jnp.dot(act, wd_f32[ws].astype(BF16), preferred_element_type=F32)
            ybuf[slot, 0:rows, :] = _pack_rows(y.astype(BF16).astype(F32))
            out_copy(c, slot, nb).start()
        by_size(c, compute)
        return carry

    lax.fori_loop(0, n_chunks, chunk_step, 0)

    _fill_blocks(meta_ref, zbuf, yb_ref, sem_z, n_blocks, expert_tails=False)
    for back in (2, 1):
        @pl.when(n_chunks >= back)
        def _(back=back):
            k = n_chunks - back
            by_size(k, lambda nb: out_copy(k, k % 2, nb).wait())


def _experts(meta_flat, xs, w_gate, w_up, w_down, n_blocks):
    d, de = w_gate.shape[1], w_gate.shape[2]
    rows = CHUNK_BLOCKS * MOE_BLOCK
    assert (n_blocks + (CHUNK_BLOCKS - 1) * N_EXPERTS) // CHUNK_BLOCKS <= LANES

    hbm = pl.BlockSpec(memory_space=pl.ANY)
    grid_spec = pltpu.PrefetchScalarGridSpec(
        num_scalar_prefetch=1,
        grid=(1,),
        in_specs=[hbm, hbm, hbm, hbm],
        out_specs=hbm,
        scratch_shapes=[pltpu.VMEM((2, rows, xs.shape[1]), xs.dtype),
                        pltpu.VMEM((2, rows, d // 2), jnp.uint32),
                        pltpu.VMEM((MOE_BLOCK, d // 2), jnp.uint32),
                        pltpu.VMEM((2, d, de), F32), pltpu.VMEM((2, d, de), F32), pltpu.VMEM((2, de, d), F32),
                        pltpu.SMEM((1,), jnp.int32),
                        pltpu.SemaphoreType.DMA((2,)), pltpu.SemaphoreType.DMA((2,)),
                        pltpu.SemaphoreType.DMA(()), pltpu.SemaphoreType.DMA((2,))],
    )
    return pl.pallas_call(
        functools.partial(_expert_kernel, n_blocks=n_blocks),
        grid_spec=grid_spec,
        out_shape=jax.ShapeDtypeStruct((n_blocks * MOE_BLOCK, d // 2), jnp.uint32),
        compiler_params=_cparams("arbitrary"),
        name="moe_experts",
    )(meta_flat, xs, w_gate, w_up, w_down)


def _combine_kernel(dest_ref, yb_ref, x1_ref, w_ref, out_ref, gbuf, sem, *, tokens):
    i = pl.program_id(0)
    n = pl.num_programs(0)
    tm = x1_ref.shape[0]
    half = x1_ref.shape[1] // 2
    slot = i % COMBINE_SLOTS
    ahead = COMBINE_SLOTS - 1

    def issue_row(tile, r):
        s = tile % COMBINE_SLOTS
        for k in range(2):
            d = dest_ref[k * tokens + tile * tm + r]
            pltpu.make_async_copy(yb_ref.at[pl.ds(d, 1), :], gbuf.at[s, k, pl.ds(r, 1), :],
                                  sem.at[s]).start(priority=k)

    def combine_rows(r0):
        rows = slice(r0, r0 + COMBINE_PIECE)
        w = w_ref[rows, :]
        lo0, hi0 = _unpack_halves(gbuf[slot, 0, rows, :])
        lo1, hi1 = _unpack_halves(gbuf[slot, 1, rows, :])
        out_ref[rows, :half] = x1_ref[rows, :half] + (lo0 * w[:, 0:1] + lo1 * w[:, 1:2])
        out_ref[rows, half:] = x1_ref[rows, half:] + (hi0 * w[:, 0:1] + hi1 * w[:, 1:2])

    @pl.when(i == 0)
    def _():
        for first in range(ahead):
            @pl.when(first < n)
            def _(first=first):
                def body(g, carry):
                    r8 = pl.multiple_of(g * SUBLANES, SUBLANES)
                    for sub in range(SUBLANES):
                        issue_row(first, r8 + sub)
                    return carry
                lax.fori_loop(0, tm // SUBLANES, body, 0)

    for k in range(2):
        pltpu.make_async_copy(yb_ref.at[pl.ds(0, tm), :], gbuf.at[slot, k], sem.at[slot]).wait()

    @pl.when(i + ahead < n)
    def _():
        for r0 in range(0, tm, COMBINE_PIECE):
            for r in range(r0, r0 + COMBINE_PIECE):
                issue_row(i + ahead, r)
            combine_rows(r0)

    @pl.when(i + ahead >= n)
    def _():
        for r0 in range(0, tm, COMBINE_PIECE):
            combine_rows(r0)


def _combine(dest_flat, yb, x1, w_tok, tm):
    t, d = x1.shape
    grid_spec = pltpu.PrefetchScalarGridSpec(
        num_scalar_prefetch=1,
        grid=(t // tm,),
        in_specs=[pl.BlockSpec(memory_space=pl.ANY),
                  pl.BlockSpec((tm, d), lambda i, dr: (i, 0)),
                  pl.BlockSpec((tm, 2), lambda i, dr: (i, 0))],
        out_specs=pl.BlockSpec((tm, d), lambda i, dr: (i, 0)),
        scratch_shapes=[pltpu.VMEM((COMBINE_SLOTS, 2, tm, d // 2), jnp.uint32),
                        pltpu.SemaphoreType.DMA((COMBINE_SLOTS,))],
    )
    return pl.pallas_call(
        functools.partial(_combine_kernel, tokens=t),
        grid_spec=grid_spec,
        out_shape=jax.ShapeDtypeStruct((t, d), F32),
        compiler_params=_cparams("arbitrary"),
        name="moe_combine",
    )(dest_flat, yb, x1, w_tok)


def _row(v):
    return v.astype(F32).reshape(1, -1)


def _layer(x2, mem2, batch, seq, mem_len, p):
    t, d = x2.shape

    bias = _bias_table(p["rel_bias"])
    col = lambda v: jnp.broadcast_to(v.astype(F32)[:, None], (v.shape[0], BLOCK))
    qgain = col(p["q_norm"]) * (1.0 / math.sqrt(ATTN_HEAD_DIM))
    kgain = jnp.tile(_row(p["k_norm"]), (1, ATTN_KV_HEADS))
    u_sb, qx, ya = _inproj_swa(x2, _row(p["norm_mix"]), p["w_in"].astype(F32), bias, p["attn_sinks"].astype(F32),
                               qgain, kgain, col(p["out_norm_attn"]), batch, seq, min(TOKEN_TILE, seq))

    km, vm = _memkv(mem2, _row(p["mem_norm"]), p["w_mem_kv"].astype(F32), _row(p["xk_norm"]),
                    min(MEM_TILE, mem2.shape[0]))
    xq_gain = _row(p["xq_norm"]) * (1.0 / math.sqrt(XATTN_HEAD_DIM))
    yx = _xattn(qx, km, vm, xq_gain, _row(p["out_norm_xattn"]), batch, seq, mem_len, min(XATTN_TILE, seq))

    a_re, a_im, bbr, bbi = _ssm_prep(p["ssm_lambda_re"], p["ssm_lambda_im"], p["ssm_log_dt"],
                                     p["ssm_b_re"], p["ssm_b_im"])
    bmat = jnp.concatenate([_block_diag_tiles(bbr), _block_diag_tiles(bbi)], axis=-1).astype(BF16)
    c_re_t = jnp.transpose(p["ssm_c_re"].astype(F32), (0, 2, 1))
    c_im_t = jnp.transpose(p["ssm_c_im"].astype(F32), (0, 2, 1))
    cmat = jnp.concatenate([_block_diag_tiles(c_re_t), _block_diag_tiles(-c_im_t)], axis=1).astype(BF16)
    steps = min(SSM_STEPS, seq)
    ys_sb = _ssm(u_sb, bmat, cmat,
                 a_re.reshape(1, SSM_STATES), a_im.reshape(1, SSM_STATES), _row(p["ssm_d"]),
                 p["ssm_w_glu"].astype(BF16), _row(p["out_norm_ssm"]), batch, seq, steps)

    wr = jnp.concatenate([p["w_router_group"], p["w_router_expert"]], axis=1).astype(F32)
    wr = jnp.pad(wr, ((0, 0), (0, LANES - wr.shape[1]))).astype(BF16)
    tm_out = min(TOKEN_TILE, seq)
    x1, logits_t, h2p = _outproj(ya, ys_sb, yx, x2, p["w_o"].astype(F32), _row(p["norm_ffn"]), wr,
                                 batch, seq, tm_out)

    dest, w_k, meta = _route(logits_t, min(ROUTE_CHUNK, t))
    n_blocks = (2 * t) // MOE_BLOCK + N_EXPERTS
    dest_flat = dest.reshape(2 * t)
    meta_flat = meta.reshape(META_ROWS * LANES)
    xs = _dispatch(dest_flat, meta_flat, h2p, n_blocks, min(DISPATCH_TILE, t))
    yb = _experts(meta_flat, xs, p["w_gate"], p["w_up"], p["w_down"], n_blocks)
    return _combine(dest_flat, yb, x1, w_k.T, min(TOKEN_TILE, t))


def kernel(x, mem, norm_mix, w_in, q_norm, k_norm, attn_sinks, rel_bias, ssm_lambda_re, ssm_lambda_im, ssm_log_dt, ssm_b_re, ssm_b_im, ssm_c_re, ssm_c_im, ssm_d, ssm_w_glu, mem_norm, w_mem_kv, xq_norm, xk_norm, out_norm_attn, out_norm_ssm, out_norm_xattn, w_o, norm_ffn, w_router_group, w_router_expert, w_gate, w_up, w_down):
    batch, seq, d = x.shape
    mem_len = mem.shape[1]
    per_layer = dict(norm_mix=norm_mix, w_in=w_in, q_norm=q_norm, k_norm=k_norm, attn_sinks=attn_sinks,
                     ssm_lambda_re=ssm_lambda_re, ssm_lambda_im=ssm_lambda_im, ssm_log_dt=ssm_log_dt,
                     ssm_b_re=ssm_b_re, ssm_b_im=ssm_b_im, ssm_c_re=ssm_c_re, ssm_c_im=ssm_c_im,
                     ssm_d=ssm_d, ssm_w_glu=ssm_w_glu, mem_norm=mem_norm, w_mem_kv=w_mem_kv,
                     xq_norm=xq_norm, xk_norm=xk_norm, out_norm_attn=out_norm_attn,
                     out_norm_ssm=out_norm_ssm, out_norm_xattn=out_norm_xattn, w_o=w_o, norm_ffn=norm_ffn,
                     w_router_group=w_router_group, w_router_expert=w_router_expert,
                     w_gate=w_gate, w_up=w_up, w_down=w_down)
    x2 = x.astype(F32).reshape(batch * seq, d)
    mem2 = mem.astype(F32).reshape(batch * mem_len, d)
    for l in range(norm_mix.shape[0]):
        p = {k: v[l] for k, v in per_layer.items()}
        p["rel_bias"] = rel_bias
        x2 = _layer(x2, mem2, batch, seq, mem_len, p)
    return x2.reshape(batch, seq, d).astype(x.dtype)
```

```python
import functools
import math

import numpy as np
import jax
import jax.numpy as jnp
from jax import lax
from jax.experimental import pallas as pl
from jax.experimental.pallas import tpu as pltpu

F32 = jnp.float32
BF16 = jnp.bfloat16
EPS = 1e-6

ATTN_HEADS = 16
ATTN_KV_HEADS = 2
ATTN_HEAD_DIM = 64
ATTN_WIDTH = ATTN_HEADS * ATTN_HEAD_DIM
WINDOW = 128
BLOCK = 128
REL_BUCKETS = 32
REL_MAX_DIST = 128
SSM_GROUP_CH = 16
SSM_GROUPS = 32
SSM_STATE = 64
SSM_WIDTH = SSM_GROUPS * SSM_GROUP_CH
XATTN_HEADS = 4
XATTN_HEAD_DIM = 128
XATTN_WIDTH = XATTN_HEADS * XATTN_HEAD_DIM
N_EXPERT_GROUPS = 8
EXPERTS_PER_GROUP = 8
N_EXPERTS = N_EXPERT_GROUPS * EXPERTS_PER_GROUP
MOE_BLOCK = 128

LANES = 128
SUBLANES = 8
MXU_WIDTH = 256
TOKEN_TILE = 512
XATTN_TILE = 1024
MEM_TILE = 512
SSM_STEPS = 128
ROUTE_CHUNK = 2048
DISPATCH_TILE = 2048
SSM_GROUPS_PER_TILE = LANES // SSM_GROUP_CH
SSM_TILES = SSM_WIDTH // LANES
SSM_TILE_STATE = SSM_GROUPS_PER_TILE * SSM_STATE
SSM_STATES = SSM_GROUPS * SSM_STATE
SSM_PIECES = 8
CHUNK_BLOCKS = 4
COMBINE_SLOTS = 3
COMBINE_PIECE = 32
META_ROWS = 8
M_OWNER, M_FIRST, M_SIZE, M_NCHUNK, M_LAST, M_NBLK, M_NUSED, M_OWNER_CHUNKS = range(8)
VMEM_LIMIT = 56 * 1024 * 1024

_NT = (((1,), (1,)), ((), ()))


def _cparams(*sem):
    return pltpu.CompilerParams(dimension_semantics=sem, vmem_limit_bytes=VMEM_LIMIT)


def _rms(x, gain):
    ms = jnp.mean(x * x, axis=-1, keepdims=True)
    return x * lax.rsqrt(ms + EPS) * gain


def _t5_bucket_table():
    qi = np.arange(BLOCK, dtype=np.int32)[:, None]
    ki = np.arange(2 * BLOCK, dtype=np.int32)[None, :]
    delta = BLOCK + qi - ki
    n = np.maximum(delta, 0)
    max_exact = REL_BUCKETS // 2
    nf = np.maximum(n, 1).astype(np.float32)
    large = max_exact + (np.log(nf / np.float32(max_exact)) / np.float32(math.log(REL_MAX_DIST / max_exact))
                         * np.float32(REL_BUCKETS - max_exact)).astype(np.int32)
    large = np.minimum(large, REL_BUCKETS - 1)
    return np.where(n < max_exact, n, large).astype(np.int32)


def _upper(cols):
    k = lax.broadcasted_iota(jnp.int32, (BLOCK, cols), 0)
    q = lax.broadcasted_iota(jnp.int32, (BLOCK, cols), 1) % BLOCK
    return k > q


def _bias_kernel(rb_ref, bucket_ref, out_ref):
    pair = pl.program_id(0)
    bucket = bucket_ref[...]
    upper = _upper(BLOCK)
    for half in range(2):
        acc = jnp.zeros(bucket.shape, F32)
        for b in range(REL_BUCKETS):
            acc = jnp.where(bucket == b, rb_ref[b, 2 * pair + half], acc)
        cols = slice(half * BLOCK, (half + 1) * BLOCK)
        out_ref[0, 0, :, cols] = jnp.where(upper, acc[:BLOCK], acc[BLOCK:])
        out_ref[1, 0, :, cols] = jnp.where(upper, jnp.float32(-1e30), acc[BLOCK:])


def _bias_table(rel_bias):
    assert WINDOW == BLOCK
    bucket = jnp.asarray(_t5_bucket_table().T)
    pairs = ATTN_HEADS // 2
    return pl.pallas_call(
        _bias_kernel,
        grid=(pairs,),
        in_specs=[pl.BlockSpec(memory_space=pltpu.SMEM),
                  pl.BlockSpec((2 * BLOCK, BLOCK), lambda h: (0, 0))],
        out_specs=pl.BlockSpec((2, 1, BLOCK, 2 * BLOCK), lambda h: (0, h, 0, 0)),
        out_shape=jax.ShapeDtypeStruct((2, pairs, BLOCK, 2 * BLOCK), F32),
        compiler_params=_cparams("arbitrary"),
        name="t5_bias_table",
    )(rel_bias.astype(F32), bucket)


def _ssm_prep_kernel(lr_ref, li_ref, ldt_ref, br_ref, bi_ref, are_ref, aim_ref, bbr_ref, bbi_ref):
    lr = lr_ref[...]
    li = li_ref[...]
    dt = jnp.exp(ldt_ref[...])
    mag = jnp.exp(lr * dt)
    a_re = mag * jnp.cos(li * dt)
    a_im = mag * jnp.sin(li * dt)
    den = lr * lr + li * li
    nr = a_re - 1.0
    ni = a_im
    coef_re = (nr * lr + ni * li) / den
    coef_im = (ni * lr - nr * li) / den
    are_ref[...] = a_re
    aim_ref[...] = a_im
    br = br_ref[...]
    bi = bi_ref[...]
    bbr_ref[...] = coef_re * br - coef_im * bi
    bbi_ref[...] = coef_re * bi + coef_im * br


def _ssm_prep(lam_re, lam_im, log_dt, b_re, b_im):
    g, n, c = b_re.shape
    vec = jax.ShapeDtypeStruct((g, 1, n), F32)
    mat = jax.ShapeDtypeStruct((g, c, n), F32)
    return pl.pallas_call(
        _ssm_prep_kernel,
        out_shape=(vec, vec, mat, mat),
        name="ssm_discretise",
    )(lam_re.astype(F32).reshape(g, 1, n), lam_im.astype(F32).reshape(g, 1, n),
      log_dt.astype(F32).reshape(g, 1, 1),
      jnp.transpose(b_re.astype(F32), (0, 2, 1)), jnp.transpose(b_im.astype(F32), (0, 2, 1)))


def _block_diag_tiles(m):
    g, r, c = m.shape
    t = g // SSM_GROUPS_PER_TILE
    eye = jnp.eye(SSM_GROUPS_PER_TILE, dtype=m.dtype)
    m4 = m.reshape(t, SSM_GROUPS_PER_TILE, r, c)
    out = m4[:, :, :, None, :] * eye[None, :, None, :, None]
    return out.reshape(t, SSM_GROUPS_PER_TILE * r, SSM_GROUPS_PER_TILE * c)


def _k_norm(k, kgain):
    dh = ATTN_HEAD_DIM
    lo = lax.broadcasted_iota(jnp.int32, (1, ATTN_KV_HEADS * dh), 1) < dh
    sq = k * k
    s_lo = jnp.sum(jnp.where(lo, sq, 0.0), axis=-1, keepdims=True)
    s_hi = jnp.sum(sq, axis=-1, keepdims=True) - s_lo
    return k * jnp.where(lo, lax.rsqrt(s_lo / dh + EPS), lax.rsqrt(s_hi / dh + EPS)) * kgain


def _swa_phases(q_of, kn, v_t, bias_of, sinks_ref, qgain, ogain, emit):
    dh = ATTN_HEAD_DIM
    upper = _upper(2 * BLOCK)
    first_head = lax.broadcasted_iota(jnp.int32, (1, 2 * BLOCK), 1) < BLOCK
    zeros = jnp.zeros((dh, BLOCK), F32)
    heads_per_kv = ATTN_HEADS // ATTN_KV_HEADS
    pairs = range(ATTN_HEADS // 2)
    kv_of = lambda pair: (2 * pair) // heads_per_kv
    state = {}

    def qk_all():
        logits = []
        for pair in pairs:
            cols = []
            for half in range(2):
                qh = q_of(2 * pair + half).astype(F32)
                ms = jnp.mean(qh * qh, axis=0, keepdims=True)
                qn = qh * lax.rsqrt(ms + EPS) * qgain
                cols.append(jnp.concatenate([qn, zeros] if kv_of(pair) == 0 else [zeros, qn], axis=0))
            rhs = jnp.concatenate(cols, axis=1).astype(BF16)
            logits.append(jnp.dot(kn, rhs, preferred_element_type=F32))
        state["logits"] = logits

    def softmax_all():
        probs = []
        for pair in pairs:
            both = state["logits"][pair]
            l = jnp.where(upper, both[:BLOCK], both[BLOCK:]) + bias_of(pair)
            sink = jnp.where(first_head, sinks_ref[2 * pair], sinks_ref[2 * pair + 1])
            m = jnp.maximum(jnp.max(l, axis=0, keepdims=True), sink)
            p = jnp.exp(l - m)
            den = jnp.sum(p, axis=0, keepdims=True) + jnp.exp(sink - m)
            pz = jnp.concatenate([jnp.where(upper, p, 0.0), jnp.where(upper, 0.0, p)], axis=0).astype(BF16)
            probs.append((pz, den))
        state["probs"] = probs

    def pv_all():
        outs = []
        for pair in pairs:
            g = kv_of(pair)
            pz, den = state["probs"][pair]
            o = jnp.dot(v_t[g * dh:(g + 1) * dh, :], pz, preferred_element_type=F32) / den
            outs += [o[:, :BLOCK], o[:, BLOCK:]]
        y_t = jnp.concatenate(outs, axis=0)
        ms = jnp.mean(y_t * y_t, axis=0, keepdims=True)
        emit((y_t * lax.rsqrt(ms + EPS) * ogain).T)

    return qk_all, softmax_all, pv_all


def _inproj_swa_kernel(sinks_ref, x_ref, g_ref, w_ref, bias_ref, qg_ref, kg_ref, og_ref,
                       u_ref, qx_ref, ya_ref, q_s, k_s, v_s, kprev_s, vprev_s, *, tiles_per_seq):
    i = pl.program_id(0)
    slot = i % 2
    old = 1 - slot
    tm = x_ref.shape[0]
    blocks = tm // BLOCK
    kvw = ATTN_KV_HEADS * ATTN_HEAD_DIM

    @pl.when(i == 0)
    def _():
        q_s[1] = jnp.zeros(q_s.shape[1:], q_s.dtype)
        k_s[1] = jnp.zeros(k_s.shape[1:], k_s.dtype)
        v_s[1] = jnp.zeros(v_s.shape[1:], v_s.dtype)
        kprev_s[...] = jnp.zeros_like(kprev_s)
        vprev_s[...] = jnp.zeros_like(vprev_s)

    h = _rms(x_ref[...], g_ref[...]).astype(BF16)
    proj = lambda c0, c1: jnp.dot(h, w_ref[:, c0:c1].astype(BF16), preferred_element_type=F32)
    piece = MXU_WIDTH

    def q_piece(n):
        def run():
            q_s[slot, n * piece:(n + 1) * piece, :] = proj(n * piece, (n + 1) * piece).T.astype(q_s.dtype)
        return run

    def kv_piece():
        kv = proj(ATTN_WIDTH, ATTN_WIDTH + 2 * kvw)
        k_s[slot] = _k_norm(kv[:, :kvw], kg_ref[...]).astype(k_s.dtype)
        v_s[slot] = kv[:, kvw:].T.astype(v_s.dtype)

    def out_piece(ref, c0, n):
        def run():
            ref[:, n * piece:(n + 1) * piece] = proj(c0 + n * piece, c0 + (n + 1) * piece).astype(ref.dtype)
        return run

    c_u = ATTN_WIDTH + 2 * kvw
    c_qx = c_u + SSM_WIDTH
    projection = ([q_piece(n) for n in range(ATTN_WIDTH // piece)] + [kv_piece]
                  + [out_piece(u_ref, c_u, n) for n in range(SSM_WIDTH // piece)]
                  + [out_piece(qx_ref, c_qx, n) for n in range(XATTN_WIDTH // piece)])

    first_of_seq = (i + tiles_per_seq - 1) % tiles_per_seq == 0
    attention = []
    for blk in range(blocks):
        own = slice(blk * BLOCK, (blk + 1) * BLOCK)
        before = slice((blk - 1) * BLOCK, blk * BLOCK)
        k_before = kprev_s[...] if blk == 0 else k_s[old, before, :]
        v_before = vprev_s[...] if blk == 0 else v_s[old, :, before]
        table = jnp.where(first_of_seq, 1, 0) if blk == 0 else 0

        def emit(y, own=own):
            ya_ref[own, :] = y.astype(ya_ref.dtype)

        attention += _swa_phases(
            q_of=lambda hd, own=own: q_s[old, hd * ATTN_HEAD_DIM:(hd + 1) * ATTN_HEAD_DIM, own],
            kn=jnp.concatenate([k_before, k_s[old, own, :]], axis=0),
            v_t=jnp.concatenate([v_before, v_s[old, :, own]], axis=1),
            bias_of=lambda pair, table=table: bias_ref[table, pair],
            sinks_ref=sinks_ref, qgain=qg_ref[...], ogain=og_ref[...], emit=emit)

    longer, shorter = (attention, projection) if len(attention) >= len(projection) else (projection, attention)
    done = 0
    for n, fn in enumerate(longer):
        fn()
        due = (n + 1) * len(shorter) // len(longer)
        for extra in shorter[done:due]:
            extra()
        done = due

    last = slice((blocks - 1) * BLOCK, blocks * BLOCK)
    kprev_s[...] = k_s[old, last, :]
    vprev_s[...] = v_s[old, :, last]


def _inproj_swa(x2, gain, w_in, bias, sinks, qgain, kgain, ogain, batch, seq, tm):
    t, d = x2.shape
    n_tiles = t // tm
    tiles_per_seq = seq // tm
    kvw = ATTN_KV_HEADS * ATTN_HEAD_DIM
    proj_tile = lambda i: jnp.minimum(i, n_tiles - 1)
    attn_tile = lambda i: jnp.maximum(i - 1, 0)
    const = lambda shape: pl.BlockSpec(shape, lambda i: (0,) * len(shape), pipeline_mode=pl.Buffered(1))
    return pl.pallas_call(
        functools.partial(_inproj_swa_kernel, tiles_per_seq=tiles_per_seq),
        grid=(n_tiles + 1,),
        in_specs=[pl.BlockSpec(memory_space=pltpu.SMEM),
                  pl.BlockSpec((tm, d), lambda i: (proj_tile(i), 0)),
                  const((1, d)), const(w_in.shape), const(bias.shape),
                  const((ATTN_HEAD_DIM, BLOCK)), const((1, kvw)), const((ATTN_WIDTH, BLOCK))],
        out_specs=[
                   pl.BlockSpec((tm, SSM_WIDTH),
                                lambda i: (proj_tile(i) % tiles_per_seq, proj_tile(i) // tiles_per_seq)),
                   pl.BlockSpec((tm, XATTN_WIDTH), lambda i: (proj_tile(i), 0)),
                   pl.BlockSpec((tm, ATTN_WIDTH), lambda i: (attn_tile(i), 0))],
        out_shape=[jax.ShapeDtypeStruct((seq, batch * SSM_WIDTH), BF16),
                   jax.ShapeDtypeStruct((t, XATTN_WIDTH), BF16),
                   jax.ShapeDtypeStruct((t, ATTN_WIDTH), BF16)],
        scratch_shapes=[pltpu.VMEM((2, ATTN_WIDTH, tm), BF16),
                        pltpu.VMEM((2, tm, kvw), BF16),
                        pltpu.VMEM((2, kvw, tm), BF16),
                        pltpu.VMEM((BLOCK, kvw), BF16), pltpu.VMEM((kvw, BLOCK), BF16)],
        compiler_params=_cparams("arbitrary"),
        name="in_proj_swa",
    )(sinks, x2, gain, w_in, bias, qgain, kgain, ogain)


def _memkv_kernel(m_ref, g_ref, w_ref, kg_ref, k_ref, v_ref):
    h = _rms(m_ref[...], g_ref[...]).astype(BF16)
    km = jnp.dot(h, w_ref[:, :XATTN_WIDTH].astype(BF16), preferred_element_type=F32)
    for hd in range(XATTN_HEADS):
        sl = slice(hd * XATTN_HEAD_DIM, (hd + 1) * XATTN_HEAD_DIM)
        k_ref[:, sl] = _rms(km[:, sl], kg_ref[...]).astype(k_ref.dtype)
    v_ref[...] = jnp.dot(h, w_ref[:, XATTN_WIDTH:].astype(BF16), preferred_element_type=F32).astype(v_ref.dtype)


def _memkv(mem2, gain, w_kv, kgain, tm):
    r, d = mem2.shape
    row = lambda w: pl.BlockSpec((tm, w), lambda i: (i, 0))
    const = lambda shape: pl.BlockSpec(shape, lambda i: (0, 0), pipeline_mode=pl.Buffered(1))
    return pl.pallas_call(
        _memkv_kernel,
        grid=(r // tm,),
        in_specs=[row(d), const((1, d)), const((d, 2 * XATTN_WIDTH)), const((1, XATTN_HEAD_DIM))],
        out_specs=[row(XATTN_WIDTH), row(XATTN_WIDTH)],
        out_shape=[jax.ShapeDtypeStruct((r, XATTN_WIDTH), BF16)] * 2,
        compiler_params=_cparams("arbitrary"),
        name="mem_kv_proj",
    )(mem2, gain, w_kv, kgain)


def _xattn_kernel(q_ref, k_ref, v_ref, qg_ref, og_ref, out_ref):
    outs = []
    for hd in range(XATTN_HEADS):
        sl = slice(hd * XATTN_HEAD_DIM, (hd + 1) * XATTN_HEAD_DIM)
        qn = _rms(q_ref[:, sl].astype(F32), qg_ref[...]).astype(BF16)
        l = lax.dot_general(qn, k_ref[:, sl], _NT, preferred_element_type=F32)
        m = jnp.max(l, axis=-1, keepdims=True)
        p = jnp.exp(l - m)
        den = jnp.sum(p, axis=-1, keepdims=True)
        outs.append(jnp.dot(p.astype(BF16), v_ref[:, sl], preferred_element_type=F32) / den)
    y = jnp.concatenate(outs, axis=-1)
    out_ref[...] = _rms(y, og_ref[...]).astype(out_ref.dtype)


def _xattn(qx, km, vm, qgain, ogain, batch, seq, mem_len, tq):
    nq = seq // tq
    const2 = lambda b, i: (0, 0)
    return pl.pallas_call(
        _xattn_kernel,
        grid=(batch, nq),
        in_specs=[pl.BlockSpec((tq, XATTN_WIDTH), lambda b, i: (b * nq + i, 0)),
                  pl.BlockSpec((mem_len, XATTN_WIDTH), lambda b, i: (b, 0)),
                  pl.BlockSpec((mem_len, XATTN_WIDTH), lambda b, i: (b, 0)),
                  pl.BlockSpec((1, XATTN_HEAD_DIM), const2),
                  pl.BlockSpec((1, XATTN_WIDTH), const2)],
        out_specs=pl.BlockSpec((tq, XATTN_WIDTH), lambda b, i: (b * nq + i, 0)),
        out_shape=jax.ShapeDtypeStruct((batch * seq, XATTN_WIDTH), BF16),
        compiler_params=_cparams("arbitrary", "arbitrary"),
        name="mem_xattn",
    )(qx, km, vm, qgain, ogain)


def _ssm_kernel(u_ref, bmat_ref, cmat_ref, are_ref, aim_ref, d_ref, wglu_ref, og_ref, out_ref,
                tb_ref, xr_ref, xi_ref, sr_ref, si_ref, *, batch, steps):
    @pl.when(pl.program_id(0) == 0)
    def _():
        sr_ref[...] = jnp.zeros_like(sr_ref)
        si_ref[...] = jnp.zeros_like(si_ref)

    for b in range(batch):
        for j in range(SSM_TILES):
            c0 = b * SSM_WIDTH + j * LANES
            tb_ref[j, pl.ds(b, steps, stride=batch), :] = u_ref[:, c0:c0 + LANES].astype(F32)
    uf = jnp.concatenate([tb_ref[j] for j in range(SSM_TILES)], axis=-1)
    u = uf.astype(BF16)
    total = steps * batch
    piece = total // SSM_PIECES
    tiles_per_half = SSM_TILES // 2
    half_states = SSM_STATES // 2

    def bu_piece(j, k):
        rows = slice(k * piece, (k + 1) * piece)
        bu = jnp.dot(u[rows, j * LANES:(j + 1) * LANES], bmat_ref[j], preferred_element_type=F32)
        xr_ref[rows, j * SSM_TILE_STATE:(j + 1) * SSM_TILE_STATE] = bu[:, :SSM_TILE_STATE]
        xi_ref[rows, j * SSM_TILE_STATE:(j + 1) * SSM_TILE_STATE] = bu[:, SSM_TILE_STATE:]

    def c_piece(j, k):
        rows = slice(k * piece, (k + 1) * piece)
        sl = slice(j * SSM_TILE_STATE, (j + 1) * SSM_TILE_STATE)
        xcat = jnp.concatenate([xr_ref[rows, sl], xi_ref[rows, sl]], axis=-1).astype(BF16)
        return jnp.dot(xcat, cmat_ref[j], preferred_element_type=F32)

    def scan_half(hf, between):
        cs = slice(hf * half_states, (hf + 1) * half_states)
        ar = jnp.broadcast_to(are_ref[:, cs], (batch, half_states))
        ai = jnp.broadcast_to(aim_ref[:, cs], (batch, half_states))
        s_r, s_i = sr_ref[:, cs], si_ref[:, cs]
        every = steps // len(between)
        for t in range(steps):
            rows = slice(t * batch, (t + 1) * batch)
            s_r, s_i = (ar * s_r - ai * s_i + xr_ref[rows, cs], ar * s_i + ai * s_r + xi_ref[rows, cs])
            xr_ref[rows, cs] = s_r
            xi_ref[rows, cs] = s_i
            if t % every == every - 1:
                between[t // every]()
        sr_ref[:, cs] = s_r
        si_ref[:, cs] = s_i

    first = [(j, k) for j in range(tiles_per_half) for k in range(SSM_PIECES)]
    second = [(j, k) for j in range(tiles_per_half, SSM_TILES) for k in range(SSM_PIECES)]
    for j, k in first:
        bu_piece(j, k)
    scan_half(0, [functools.partial(bu_piece, j, k) for j, k in second])
    y_pieces = {}
    scan_half(1, [functools.partial(lambda j, k: y_pieces.__setitem__((j, k), c_piece(j, k)), j, k)
                  for j, k in first])
    for j, k in second:
        y_pieces[(j, k)] = c_piece(j, k)
    ys = [jnp.concatenate([y_pieces[(j, k)] for k in range(SSM_PIECES)], axis=0) for j in range(SSM_TILES)]
    y = jnp.concatenate(ys, axis=-1) + d_ref[...] * uf
    y = jax.nn.gelu(y)
    y = y * jax.nn.sigmoid(jnp.dot(y.astype(BF16), wglu_ref[...], preferred_element_type=F32))
    y = _rms(y, og_ref[...])
    for j in range(SSM_TILES):
        tb_ref[j] = y[:, j * LANES:(j + 1) * LANES]
    for b in range(batch):
        for j in range(SSM_TILES):
            c0 = b * SSM_WIDTH + j * LANES
            out_ref[:, c0:c0 + LANES] = tb_ref[j, pl.ds(b, steps, stride=batch), :].astype(out_ref.dtype)


def _ssm(u_sb, bmat, cmat, a_re, a_im, d_skip, w_glu, ogain, batch, seq, steps):
    rows = steps * batch
    const2 = lambda c: (0, 0)
    const3 = lambda c: (0, 0, 0)
    return pl.pallas_call(
        functools.partial(_ssm_kernel, batch=batch, steps=steps),
        grid=(seq // steps,),
        in_specs=[pl.BlockSpec((steps, batch * SSM_WIDTH), lambda c: (c, 0)),
                  pl.BlockSpec(bmat.shape, const3), pl.BlockSpec(cmat.shape, const3),
                  pl.BlockSpec((1, SSM_STATES), const2), pl.BlockSpec((1, SSM_STATES), const2),
                  pl.BlockSpec((1, SSM_WIDTH), const2),
                  pl.BlockSpec((SSM_WIDTH, SSM_WIDTH), const2),
                  pl.BlockSpec((1, SSM_WIDTH), const2)],
        out_specs=pl.BlockSpec((steps, batch * SSM_WIDTH), lambda c: (c, 0)),
        out_shape=jax.ShapeDtypeStruct((seq, batch * SSM_WIDTH), BF16),
        scratch_shapes=[pltpu.VMEM((SSM_TILES, rows, LANES), F32),
                        pltpu.VMEM((rows, SSM_STATES), F32), pltpu.VMEM((rows, SSM_STATES), F32),
                        pltpu.VMEM((batch, SSM_STATES), F32), pltpu.VMEM((batch, SSM_STATES), F32)],
        compiler_params=_cparams("arbitrary"),
        name="s5_layer",
    )(u_sb, bmat, cmat, a_re, a_im, d_skip, w_glu, ogain)


def _outproj_kernel(ya_ref, ys_ref, yx_ref, x_ref, wo_ref, g_ref, wr_ref, x1_ref, lt_ref, hp_ref, *, sub):
    for r0 in range(0, x_ref.shape[0], sub):
        rows = slice(r0, r0 + sub)
        mix = jnp.concatenate([ya_ref[rows, :], ys_ref[rows, :], yx_ref[rows, :]], axis=-1)
        x1 = x_ref[rows, :] + jnp.dot(mix, wo_ref[...].astype(BF16), preferred_element_type=F32)
        x1_ref[rows, :] = x1
        h2 = _rms(x1, g_ref[...])
        hi = h2.astype(BF16)
        lt_ref[rows, :] = jnp.dot(hi, wr_ref[...], preferred_element_type=F32)
        hp_ref[rows, :] = _pack_rows(hi.astype(F32))


def _outproj(ya, ys_sb, yx, x2, w_o, gain, wr, batch, seq, tm):
    t, d = x2.shape
    nsb = seq // tm
    row = lambda w: pl.BlockSpec((tm, w), lambda i: (i, 0))
    const = lambda shape: pl.BlockSpec(shape, lambda i: (0, 0), pipeline_mode=pl.Buffered(1))
    return pl.pallas_call(
        functools.partial(_outproj_kernel, sub=min(MXU_WIDTH, tm)),
        grid=(t // tm,),
        in_specs=[row(ATTN_WIDTH),
                  pl.BlockSpec((tm, SSM_WIDTH), lambda i: (i % nsb, i // nsb)),
                  row(XATTN_WIDTH), row(d),
                  const(w_o.shape), const((1, d)), const(wr.shape)],
        out_specs=[row(d), row(LANES), row(d // 2)],
        out_shape=[jax.ShapeDtypeStruct((t, d), F32), jax.ShapeDtypeStruct((t, LANES), F32),
                   jax.ShapeDtypeStruct((t, d // 2), jnp.uint32)],
        compiler_params=_cparams("arbitrary"),
        name="out_proj_router",
    )(ya, ys_sb, yx, x2, w_o, gain, wr)


def _route_kernel(lt_ref, tri_ref, dest_ref, w_ref, meta_ref, cnt_ref, carry_ref, pstart_ref):
    phase = pl.program_id(0)
    c = pl.program_id(1)
    logits = lt_ref[...].T
    tc = logits.shape[1]
    ng, epg = N_EXPERT_GROUPS, EXPERTS_PER_GROUP
    row8 = lax.broadcasted_iota(jnp.int32, (ng, tc), 0)

    gl = logits[0:ng]
    gmax = jnp.max(gl, axis=0, keepdims=True)
    gidx = jnp.min(jnp.where(gl == gmax, row8, ng), axis=0, keepdims=True)
    gate = 1.0 / jnp.sum(jnp.exp(gl - gmax), axis=0, keepdims=True)
    sel = jnp.zeros((epg, tc), F32)
    for g in range(ng):
        sel = jnp.where(gidx == g, logits[ng + g * epg:ng + (g + 1) * epg], sel)
    v1 = jnp.max(sel, axis=0, keepdims=True)
    i1 = jnp.min(jnp.where(sel == v1, row8, epg), axis=0, keepdims=True)
    sel2 = jnp.where(row8 == i1, -jnp.inf, sel)
    v2 = jnp.max(sel2, axis=0, keepdims=True)
    i2 = jnp.min(jnp.where(sel2 == v2, row8, epg), axis=0, keepdims=True)
    e = jnp.exp(v2 - v1)
    w1 = gate * (1.0 / (1.0 + e))
    w2 = gate * (e / (1.0 + e))
    e1 = gidx * epg + i1
    e2 = gidx * epg + i2
    rowe = lax.broadcasted_iota(jnp.int32, (N_EXPERTS, tc), 0)
    oh1 = rowe == e1
    oh2 = rowe == e2
    member = jnp.where(jnp.logical_or(oh1, oh2), 1.0, 0.0)
    chunk_cnt = jnp.sum(member, axis=1, keepdims=True)

    @pl.when(phase == 0)
    def _():
        @pl.when(c == 0)
        def _():
            cnt_ref[...] = jnp.zeros_like(cnt_ref)
        cnt_ref[...] += chunk_cnt

    @pl.when(phase == 1)
    def _():
        @pl.when(c == 0)
        def _():
            cnt = cnt_ref[...]
            nblk = jnp.floor((cnt + (MOE_BLOCK - 1)) * (1.0 / MOE_BLOCK))
            nchunk = jnp.floor((nblk + (CHUNK_BLOCKS - 1)) * (1.0 / CHUNK_BLOCKS))
            r = lax.broadcasted_iota(jnp.int32, (N_EXPERTS, LANES), 0)
            cidx = lax.broadcasted_iota(jnp.int32, (N_EXPERTS, LANES), 1)
            to_row = lambda col: jnp.sum(jnp.where(r == cidx, col, 0.0), axis=0, keepdims=True)
            cumsum_col = lambda col: jnp.sum(jnp.where(cidx <= r, to_row(col), 0.0), axis=1, keepdims=True)
            cumsum_row = lambda col: jnp.sum(jnp.where(r <= cidx, col, 0.0), axis=0, keepdims=True)
            bend = cumsum_col(nblk)
            bstart = bend - nblk
            cend = cumsum_col(nchunk)
            cstart = cend - nchunk
            pstart_ref[...] = bstart * MOE_BLOCK
            carry_ref[...] = jnp.zeros_like(carry_ref)
            lanef = lax.broadcasted_iota(jnp.int32, (1, LANES), 1).astype(F32)
            owner = jnp.minimum(jnp.sum(jnp.where(cend <= lanef, 1.0, 0.0), axis=0, keepdims=True),
                                N_EXPERTS - 1.0)
            own = r.astype(F32) == owner
            pick = lambda col: jnp.sum(jnp.where(own, col, 0.0), axis=0, keepdims=True)
            idx = lanef - pick(cstart)
            first = pick(bstart) + CHUNK_BLOCKS * idx
            size = jnp.clip(pick(nblk) - CHUNK_BLOCKS * idx, 0.0, float(CHUNK_BLOCKS))
            zero = jnp.zeros((1, LANES), F32)
            rows = [owner, first, size,
                    zero + jnp.sum(nchunk, axis=0, keepdims=True),
                    cumsum_row(nblk) - 1.0,
                    to_row(nblk),
                    zero + jnp.sum(nblk, axis=0, keepdims=True),
                    pick(nchunk)]
            for k, v in enumerate(rows):
                meta_ref[k:k + 1, :] = v.astype(jnp.int32)

        before = carry_ref[...] + jnp.dot(member.astype(BF16), tri_ref[...], preferred_element_type=F32)
        pos = before + pstart_ref[...]
        dest_ref[0:1, :] = jnp.sum(jnp.where(oh1, pos, 0.0), axis=0, keepdims=True).astype(jnp.int32)
        dest_ref[1:2, :] = jnp.sum(jnp.where(oh2, pos, 0.0), axis=0, keepdims=True).astype(jnp.int32)
        w_ref[0:1, :] = w1
        w_ref[1:2, :] = w2
        carry_ref[...] += chunk_cnt


def _route(logits_t, tc):
    t = logits_t.shape[0]
    nc = t // tc
    tri = jnp.asarray(np.triu(np.ones((tc, tc), np.float32), k=1), dtype=BF16)
    return pl.pallas_call(
        _route_kernel,
        grid=(2, nc),
        in_specs=[pl.BlockSpec((tc, LANES), lambda p, c: (c, 0)),
                  pl.BlockSpec((tc, tc), lambda p, c: (0, 0))],
        out_specs=[pl.BlockSpec((2, tc), lambda p, c: (0, c * p)),
                   pl.BlockSpec((2, tc), lambda p, c: (0, c * p)),
                   pl.BlockSpec((META_ROWS, LANES), lambda p, c: (0, 0))],
        out_shape=[jax.ShapeDtypeStruct((2, t), jnp.int32), jax.ShapeDtypeStruct((2, t), F32),
                   jax.ShapeDtypeStruct((META_ROWS, LANES), jnp.int32)],
        scratch_shapes=[pltpu.VMEM((N_EXPERTS, 1), F32)] * 3,
        compiler_params=_cparams("arbitrary", "arbitrary"),
        name="moe_route",
    )(logits_t, tri)


def _meta(meta_ref, row, lane=0):
    return meta_ref[row * LANES + lane]


def _fill_blocks(meta_ref, zbuf, dst_ref, sem, n_blocks, *, expert_tails):
    zbuf[...] = jnp.zeros_like(zbuf)
    n_used = _meta(meta_ref, M_NUSED)
    block = lambda b: pltpu.make_async_copy(zbuf, dst_ref.at[pl.ds(b * MOE_BLOCK, MOE_BLOCK), :], sem)

    def tails(fn):
        def body(e, carry):
            @pl.when(_meta(meta_ref, M_NBLK, e) > 0)
            def _():
                fn(block(_meta(meta_ref, M_LAST, e)))
            return carry
        lax.fori_loop(0, N_EXPERTS, body, 0)

    def unused(fn):
        def body(b, carry):
            fn(block(b))
            return carry
        lax.fori_loop(n_used, n_blocks, body, 0)

    for phase in (lambda cp: cp.start(), lambda cp: cp.wait()):
        if expert_tails:
            tails(phase)
        unused(phase)


def _dispatch_kernel(dest_ref, meta_ref, h_ref, xs_ref, zbuf, sem_z, sem, *, tokens, n_blocks):
    tm = h_ref.shape[0]

    @pl.when(pl.program_id(0) == 0)
    def _():
        _fill_blocks(meta_ref, zbuf, xs_ref, sem_z, n_blocks, expert_tails=True)

    base = pl.program_id(0) * tm

    for r in range(tm):
        for k in range(2):
            d = dest_ref[k * tokens + base + r]
            pltpu.make_async_copy(h_ref.at[pl.ds(r, 1), :], xs_ref.at[pl.ds(d, 1), :], sem).start(priority=k)
    for k in range(2):
        pltpu.make_async_copy(h_ref, xs_ref.at[pl.ds(0, tm), :], sem).wait()


def _dispatch(dest_flat, meta_flat, h2p, n_blocks, tm):
    t, w = h2p.shape
    grid_spec = pltpu.PrefetchScalarGridSpec(
        num_scalar_prefetch=2,
        grid=(t // tm,),
        in_specs=[pl.BlockSpec((tm, w), lambda i, d, m: (i, 0))],
        out_specs=pl.BlockSpec(memory_space=pl.ANY),
        scratch_shapes=[pltpu.VMEM((MOE_BLOCK, w), h2p.dtype),
                        pltpu.SemaphoreType.DMA(()), pltpu.SemaphoreType.DMA(())],
    )
    return pl.pallas_call(
        functools.partial(_dispatch_kernel, tokens=t, n_blocks=n_blocks),
        grid_spec=grid_spec,
        out_shape=jax.ShapeDtypeStruct((n_blocks * MOE_BLOCK, w), h2p.dtype),
        compiler_params=_cparams("arbitrary"),
        name="moe_dispatch",
    )(dest_flat, meta_flat, h2p)


def _pack_rows(x):
    bits = lax.bitcast_convert_type(x, jnp.uint32)
    half = x.shape[1] // 2
    return (bits[:, half:] & jnp.uint32(0xFFFF0000)) | (bits[:, :half] >> 16)


def _unpack_halves(words):
    lo = lax.bitcast_convert_type(words << 16, F32)
    hi = lax.bitcast_convert_type(words & jnp.uint32(0xFFFF0000), F32)
    return lo, hi


def _unpack_rows(words):
    return jnp.concatenate(_unpack_halves(words), axis=-1).astype(BF16)


def _expert_kernel(meta_ref, xs_ref, wg_ref, wu_ref, wd_ref, yb_ref,
                   xbuf, ybuf, zbuf, wg_f32, wu_f32, wd_f32, wslot_ref,
                   sem_in, sem_out, sem_z, sem_w, *, n_blocks):
    n_chunks = _meta(meta_ref, M_NCHUNK)

    def weight_copies(k, s):
        e = _meta(meta_ref, M_OWNER, k)
        return [pltpu.make_async_copy(src.at[e], dst.at[s], sem_w.at[s])
                for src, dst in ((wg_ref, wg_f32), (wu_ref, wu_f32), (wd_ref, wd_f32))]

    def in_copy(k, s, nb):
        rows = nb * MOE_BLOCK
        src = xs_ref.at[pl.ds(_meta(meta_ref, M_FIRST, k) * MOE_BLOCK, rows), :]
        return pltpu.make_async_copy(src, xbuf.at[s, pl.ds(0, rows), :], sem_in.at[s])

    def out_copy(k, s, nb):
        rows = nb * MOE_BLOCK
        dst = yb_ref.at[pl.ds(_meta(meta_ref, M_FIRST, k) * MOE_BLOCK, rows), :]
        return pltpu.make_async_copy(ybuf.at[s, pl.ds(0, rows), :], dst, sem_out.at[s])

    def by_size(k, fn):
        for nb in range(1, CHUNK_BLOCKS + 1):
            pl.when(_meta(meta_ref, M_SIZE, k) == nb)(functools.partial(fn, nb))

    wslot_ref[0] = 0
    by_size(0, lambda nb: in_copy(0, 0, nb).start())
    for cp in weight_copies(0, 0):
        cp.start()

    def chunk_step(c, carry):
        slot = c % 2

        @pl.when(c + 1 < n_chunks)
        def _():
            by_size(c + 1, lambda nb: in_copy(c + 1, 1 - slot, nb).start())

        prev = jnp.maximum(c - 1, 0)
        new_expert = jnp.logical_or(c == 0, _meta(meta_ref, M_OWNER, c) != _meta(meta_ref, M_OWNER, prev))

        @pl.when(new_expert)
        def _():
            ws = jnp.where(c == 0, 0, 1 - wslot_ref[0])
            wslot_ref[0] = ws
            nxt = c + _meta(meta_ref, M_OWNER_CHUNKS, c)

            @pl.when(nxt < n_chunks)
            def _():
                for cp in weight_copies(nxt, 1 - ws):
                    cp.start()
            for cp in weight_copies(c, ws):
                cp.wait()

        @pl.when(c >= 2)
        def _():
            by_size(c - 2, lambda nb: out_copy(c - 2, slot, nb).wait())

        def compute(nb):
            rows = nb * MOE_BLOCK
            ws = wslot_ref[0]
            in_copy(c, slot, nb).wait()
            h = _unpack_rows(xbuf[slot, 0:rows, :])
            gate = jnp.dot(h, wg_f32[ws].astype(BF16), preferred_element_type=F32)
            up = jnp.dot(h, wu_f32[ws].astype(BF16), preferred_element_type=F32)
            act = (jax.nn.silu(gate) * up).astype(BF16)
            y = jnp.dot(act, wd_f32[ws].astype(BF16), preferred_element_type=F32)
            ybuf[slot, 0:rows, :] = _pack_rows(y.astype(BF16).astype(F32))
            out_copy(c, slot, nb).start()
        by_size(c, compute)
        return carry

    lax.fori_loop(0, n_chunks, chunk_step, 0)

    _fill_blocks(meta_ref, zbuf, yb_ref, sem_z, n_blocks, expert_tails=False)
    for back in (2, 1):
        @pl.when(n_chunks >= back)
        def _(back=back):
            k = n_chunks - back
            by_size(k, lambda nb: out_copy(k, k % 2, nb).wait())


def _experts(meta_flat, xs, w_gate, w_up, w_down, n_blocks):
    d, de = w_gate.shape[1], w_gate.shape[2]
    rows = CHUNK_BLOCKS * MOE_BLOCK
    assert (n_blocks + (CHUNK_BLOCKS - 1) * N_EXPERTS) // CHUNK_BLOCKS <= LANES

    hbm = pl.BlockSpec(memory_space=pl.ANY)
    grid_spec = pltpu.PrefetchScalarGridSpec(
        num_scalar_prefetch=1,
        grid=(1,),
        in_specs=[hbm, hbm, hbm, hbm],
        out_specs=hbm,
        scratch_shapes=[pltpu.VMEM((2, rows, xs.shape[1]), xs.dtype),
                        pltpu.VMEM((2, rows, d // 2), jnp.uint32),
                        pltpu.VMEM((MOE_BLOCK, d // 2), jnp.uint32),
                        pltpu.VMEM((2, d, de), F32), pltpu.VMEM((2, d, de), F32), pltpu.VMEM((2, de, d), F32),
                        pltpu.SMEM((1,), jnp.int32),
                        pltpu.SemaphoreType.DMA((2,)), pltpu.SemaphoreType.DMA((2,)),
                        pltpu.SemaphoreType.DMA(()), pltpu.SemaphoreType.DMA((2,))],
    )
    return pl.pallas_call(
        functools.partial(_expert_kernel, n_blocks=n_blocks),
        grid_spec=grid_spec,
        out_shape=jax.ShapeDtypeStruct((n_blocks * MOE_BLOCK, d // 2), jnp.uint32),
        compiler_params=_cparams("arbitrary"),
        name="moe_experts",
    )(meta_flat, xs, w_gate, w_up, w_down)


def _combine_kernel(dest_ref, yb_ref, x1_ref, w_ref, out_ref, gbuf, sem, *, tokens):
    i = pl.program_id(0)
    n = pl.num_programs(0)
    tm = x1_ref.shape[0]
    half = x1_ref.shape[1] // 2
    slot = i % COMBINE_SLOTS
    ahead = COMBINE_SLOTS - 1

    def issue_row(tile, r):
        s = tile % COMBINE_SLOTS
        for k in range(2):
            d = dest_ref[k * tokens + tile * tm + r]
            pltpu.make_async_copy(yb_ref.at[pl.ds(d, 1), :], gbuf.at[s, k, pl.ds(r, 1), :],
                                  sem.at[s]).start(priority=k)

    def combine_rows(r0):
        rows = slice(r0, r0 + COMBINE_PIECE)
        w = w_ref[rows, :]
        lo0, hi0 = _unpack_halves(gbuf[slot, 0, rows, :])
        lo1, hi1 = _unpack_halves(gbuf[slot, 1, rows, :])
        out_ref[rows, :half] = x1_ref[rows, :half] + (lo0 * w[:, 0:1] + lo1 * w[:, 1:2])
        out_ref[rows, half:] = x1_ref[rows, half:] + (hi0 * w[:, 0:1] + hi1 * w[:, 1:2])

    @pl.when(i == 0)
    def _():
        for first in range(ahead):
            @pl.when(first < n)
            def _(first=first):
                def body(g, carry):
                    r8 = pl.multiple_of(g * SUBLANES, SUBLANES)
                    for sub in range(SUBLANES):
                        issue_row(first, r8 + sub)
                    return carry
                lax.fori_loop(0, tm // SUBLANES, body, 0)

    for k in range(2):
        pltpu.make_async_copy(yb_ref.at[pl.ds(0, tm), :], gbuf.at[slot, k], sem.at[slot]).wait()

    @pl.when(i + ahead < n)
    def _():
        for r0 in range(0, tm, COMBINE_PIECE):
            for r in range(r0, r0 + COMBINE_PIECE):
                issue_row(i + ahead, r)
            combine_rows(r0)

    @pl.when(i + ahead >= n)
    def _():
        for r0 in range(0, tm, COMBINE_PIECE):
            combine_rows(r0)


def _combine(dest_flat, yb, x1, w_tok, tm):
    t, d = x1.shape
    grid_spec = pltpu.PrefetchScalarGridSpec(
        num_scalar_prefetch=1,
        grid=(t // tm,),
        in_specs=[pl.BlockSpec(memory_space=pl.ANY),
                  pl.BlockSpec((tm, d), lambda i, dr: (i, 0)),
                  pl.BlockSpec((tm, 2), lambda i, dr: (i, 0))],
        out_specs=pl.BlockSpec((tm, d), lambda i, dr: (i, 0)),
        scratch_shapes=[pltpu.VMEM((COMBINE_SLOTS, 2, tm, d // 2), jnp.uint32),
                        pltpu.SemaphoreType.DMA((COMBINE_SLOTS,))],
    )
    return pl.pallas_call(
        functools.partial(_combine_kernel, tokens=t),
        grid_spec=grid_spec,
        out_shape=jax.ShapeDtypeStruct((t, d), F32),
        compiler_params=_cparams("arbitrary"),
        name="moe_combine",
    )(dest_flat, yb, x1, w_tok)


def _row(v):
    return v.astype(F32).reshape(1, -1)


def _layer(x2, mem2, batch, seq, mem_len, p):
    t, d = x2.shape

    bias = _bias_table(p["rel_bias"])
    col = lambda v: jnp.broadcast_to(v.astype(F32)[:, None], (v.shape[0], BLOCK))
    qgain = col(p["q_norm"]) * (1.0 / math.sqrt(ATTN_HEAD_DIM))
    kgain = jnp.tile(_row(p["k_norm"]), (1, ATTN_KV_HEADS))
    u_sb, qx, ya = _inproj_swa(x2, _row(p["norm_mix"]), p["w_in"].astype(F32), bias, p["attn_sinks"].astype(F32),
                               qgain, kgain, col(p["out_norm_attn"]), batch, seq, min(TOKEN_TILE, seq))

    km, vm = _memkv(mem2, _row(p["mem_norm"]), p["w_mem_kv"].astype(F32), _row(p["xk_norm"]),
                    min(MEM_TILE, mem2.shape[0]))
    xq_gain = _row(p["xq_norm"]) * (1.0 / math.sqrt(XATTN_HEAD_DIM))
    yx = _xattn(qx, km, vm, xq_gain, _row(p["out_norm_xattn"]), batch, seq, mem_len, min(XATTN_TILE, seq))

    a_re, a_im, bbr, bbi = _ssm_prep(p["ssm_lambda_re"], p["ssm_lambda_im"], p["ssm_log_dt"],
                                     p["ssm_b_re"], p["ssm_b_im"])
    bmat = jnp.concatenate([_block_diag_tiles(bbr), _block_diag_tiles(bbi)], axis=-1).astype(BF16)
    c_re_t = jnp.transpose(p["ssm_c_re"].astype(F32), (0, 2, 1))
    c_im_t = jnp.transpose(p["ssm_c_im"].astype(F32), (0, 2, 1))
    cmat = jnp.concatenate([_block_diag_tiles(c_re_t), _block_diag_tiles(-c_im_t)], axis=1).astype(BF16)
    steps = min(SSM_STEPS, seq)
    ys_sb = _ssm(u_sb, bmat, cmat,
                 a_re.reshape(1, SSM_STATES), a_im.reshape(1, SSM_STATES), _row(p["ssm_d"]),
                 p["ssm_w_glu"].astype(BF16), _row(p["out_norm_ssm"]), batch, seq, steps)

    wr = jnp.concatenate([p["w_router_group"], p["w_router_expert"]], axis=1).astype(F32)
    wr = jnp.pad(wr, ((0, 0), (0, LANES - wr.shape[1]))).astype(BF16)
    tm_out = min(TOKEN_TILE, seq)
    x1, logits_t, h2p = _outproj(ya, ys_sb, yx, x2, p["w_o"].astype(F32), _row(p["norm_ffn"]), wr,
                                 batch, seq, tm_out)

    dest, w_k, meta = _route(logits_t, min(ROUTE_CHUNK, t))
    n_blocks = (2 * t) // MOE_BLOCK + N_EXPERTS
    dest_flat = dest.reshape(2 * t)
    meta_flat = meta.reshape(META_ROWS * LANES)
    xs = _dispatch(dest_flat, meta_flat, h2p, n_blocks, min(DISPATCH_TILE, t))
    yb = _experts(meta_flat, xs, p["w_gate"], p["w_up"], p["w_down"], n_blocks)
    return _combine(dest_flat, yb, x1, w_k.T, min(TOKEN_TILE, t))


def kernel(x, mem, norm_mix, w_in, q_norm, k_norm, attn_sinks, rel_bias, ssm_lambda_re, ssm_lambda_im, ssm_log_dt, ssm_b_re, ssm_b_im, ssm_c_re, ssm_c_im, ssm_d, ssm_w_glu, mem_norm, w_mem_kv, xq_norm, xk_norm, out_norm_attn, out_norm_ssm, out_norm_xattn, w_o, norm_ffn, w_router_group, w_router_expert, w_gate, w_up, w_down):
    batch, seq, d = x.shape
    mem_len = mem.shape[1]
    per_layer = dict(norm_mix=norm_mix, w_in=w_in, q_norm=q_norm, k_norm=k_norm, attn_sinks=attn_sinks,
                     ssm_lambda_re=ssm_lambda_re, ssm_lambda_im=ssm_lambda_im, ssm_log_dt=ssm_log_dt,
                     ssm_b_re=ssm_b_re, ssm_b_im=ssm_b_im, ssm_c_re=ssm_c_re, ssm_c_im=ssm_c_im,
                     ssm_d=ssm_d, ssm_w_glu=ssm_w_glu, mem_norm=mem_norm, w_mem_kv=w_mem_kv,
                     xq_norm=xq_norm, xk_norm=xk_norm, out_norm_attn=out_norm_attn,
                     out_norm_ssm=out_norm_ssm, out_norm_xattn=out_norm_xattn, w_o=w_o, norm_ffn=norm_ffn,
                     w_router_group=w_router_group, w_router_expert=w_router_expert,
                     w_gate=w_gate, w_up=w_up, w_down=w_down)
    x2 = x.astype(F32).reshape(batch * seq, d)
    mem2 = mem.astype(F32).reshape(batch * mem_len, d)
    for l in range(norm_mix.shape[0]):
        p = {k: v[l] for k, v in per_layer.items()}
        p["rel_bias"] = rel_bias
        x2 = _layer(x2, mem2, batch, seq, mem_len, p)
    return x2.reshape(batch, seq, d).astype(x.dtype)
```
